```python
import math
import jax, jax.numpy as jnp
from jax import lax
import numpy as np

D_MODEL = 2048
BATCH = 4
SEQ = 4096
DEPTH = 2

GRID_W = 64
CTX_LEN = 256
N_MIXERS = 2
N_HYENA_LAYERS = (DEPTH + 1) // 2
N_S5_LAYERS = DEPTH // 2
HY_EMB = 33
HY_BANDS = (HY_EMB - 1) // 2
HY_ORDER = 64
HY_DECAY_TARGET = 1e-2
HY_FAST_PCT = 0.3
HY_SLOW_PCT = 1.5
S5_H = 16
S5_P = 64
S5_G = D_MODEL // S5_H
S5_DT_MIN = 1e-3
S5_DT_MAX = 1e-1
N_GROUPS = 4
EXPERTS_PER_GROUP = 8
N_EXPERTS = N_GROUPS * EXPERTS_PER_GROUP
TOP_K = 2
D_EXPERT = D_MODEL // 2
MOE_BLOCK = 128
NORM_EPS = 1e-6

kernel_name = "hybrid_hyena_s5_hmoe_dit"


def _rmsnorm(x, g):
    x32 = x.astype(jnp.float32)
    y = x32 * lax.rsqrt(jnp.mean(x32 * x32, axis=-1, keepdims=True) + NORM_EPS)
    return y.astype(x.dtype) * g


def _modulate(h, shift, scale):
    return h * (1 + scale) + shift


def _hyena_filter(L, fw1, fb1, fw2, fb2, fw3, freq):
    f32 = jnp.float32
    t = jnp.linspace(0.0, 1.0, L, dtype=f32)[:, None]
    w = (2.0 * math.pi / L) * jnp.arange(L, dtype=f32)[:, None]
    bands = jnp.linspace(1e-4, HY_BANDS - 1, HY_BANDS, dtype=f32)[None, :]
    z = jnp.concatenate([t, jnp.cos(bands * w), -jnp.sin(bands * w)], axis=-1)
    fr = freq.astype(f32)
    h = jnp.sin(fr * (z @ fw1.astype(f32) + fb1.astype(f32)))
    h = jnp.sin(fr * (h @ fw2.astype(f32) + fb2.astype(f32)))
    h = (h @ fw3.astype(f32)).reshape(L, 2, D_MODEL)
    max_decay = math.log(HY_DECAY_TARGET) / HY_FAST_PCT
    min_decay = math.log(HY_DECAY_TARGET) / HY_SLOW_PCT
    deltas = jnp.abs(jnp.linspace(min_decay, max_decay, D_MODEL, dtype=f32))
    h = h * jnp.exp(-t * deltas)[:, None, :]
    k = jnp.concatenate([h[:, 0], jnp.zeros((1, D_MODEL), f32), h[:0:-1, 1]], axis=0)
    return k / jnp.sum(jnp.abs(k), axis=0, keepdims=True)


def _hyena_mix(u, w_in, b_in, conv_w, conv_b, fw1, fb1, fw2, fb2, fw3, freq, skip, w_out, b_out):
    L = u.shape[1]
    z = u @ w_in + b_in
    zp = jnp.pad(z, ((0, 0), (1, 1), (0, 0)))
    z = zp[:, :-2] * conv_w[0] + zp[:, 1:-1] * conv_w[1] + zp[:, 2:] * conv_w[2] + conv_b
    x0, x1, v = jnp.split(z, 3, axis=-1)
    v = (v * x1).astype(jnp.float32)
    k = _hyena_filter(L, fw1, fb1, fw2, fb2, fw3, freq)
    n = 2 * L
    y = jnp.fft.irfft(jnp.fft.rfft(v, n=n, axis=1) * jnp.fft.rfft(k, n=n, axis=0)[None], n=n, axis=1)[:, :L]
    y = y + skip.astype(jnp.float32) * v
    y = y.astype(u.dtype) * x0
    return y @ w_out + b_out


def _ssm_op(e1, e2):
    a1, b1 = e1
    a2, b2 = e2
    return a1 * a2, a2 * b1 + b2


def _diag_scan(bu, lam_bar, h0):
    if h0 is not None:
        bu = bu.at[:, 0].add(lam_bar * h0)
    a = jnp.broadcast_to(lam_bar, (1, bu.shape[1]) + lam_bar.shape)
    _, h = lax.associative_scan(_ssm_op, (a, bu), axis=1)
    return h


def _s5_mix(u_lat, u_ctx, a_re, a_im, log_step, b_re, b_im, c_re, c_im, d_skip, w1, b1, w2, b2, ctx_out):
    f32 = jnp.float32
    Bn, L, _ = u_lat.shape
    Lc = u_ctx.shape[1]
    ul = u_lat.astype(f32)
    uc = u_ctx.astype(f32)
    ulg = ul.reshape(Bn, L, S5_G, S5_H)
    ucg = uc.reshape(Bn, Lc, S5_G, S5_H)
    y_lat = d_skip.astype(f32) * ul
    y_ctx = d_skip.astype(f32) * uc if ctx_out else None
    for d in range(2):
        lam = lax.complex(a_re[d].astype(f32), a_im[d].astype(f32))
        step = jnp.exp(log_step[d].astype(f32))[:, None]
        lam_bar = jnp.exp(lam * step)
        b_bar = ((lam_bar - 1.0) / lam)[..., None] * lax.complex(b_re[d].astype(f32), b_im[d].astype(f32))
        c_mat = lax.complex(c_re[d].astype(f32), c_im[d].astype(f32))
        orient = (lambda t: jnp.flip(t, axis=1)) if d == 1 else (lambda t: t)
        h_ctx = _diag_scan(orient(jnp.einsum('blgh,gph->blgp', ucg, b_bar)), lam_bar, None)
        h_lat = _diag_scan(orient(jnp.einsum('blgh,gph->blgp', ulg, b_bar)), lam_bar, h_ctx[:, -1])
        y_lat = y_lat + orient(jnp.real(jnp.einsum('blgp,ghp->blgh', h_lat, c_mat))).reshape(Bn, L, D_MODEL)
        if ctx_out:
            y_ctx = y_ctx + orient(jnp.real(jnp.einsum('blgp,ghp->blgh', h_ctx, c_mat))).reshape(Bn, Lc, D_MODEL)

    def glu(y, dtype):
        y = jax.nn.gelu(y).astype(dtype)
        return (y @ w1 + b1) * jax.nn.sigmoid(y @ w2 + b2)

    o_lat = glu(y_lat, u_lat.dtype)
    o_ctx = glu(y_ctx, u_ctx.dtype) if ctx_out else None
    return o_lat, o_ctx


def _hier_moe(t, wg, bg, we, be, w_gate, w_up, w_down):
    T = t.shape[0]
    pg = jax.nn.softmax((t @ wg + bg).astype(jnp.float32), axis=-1)
    p_top, g_idx = lax.top_k(pg, 1)
    le = (t @ we + be).astype(jnp.float32).reshape(T, N_GROUPS, EXPERTS_PER_GROUP)
    le_sel = jnp.take_along_axis(le, g_idx[:, :, None], axis=1)[:, 0]
    vals, e_idx = lax.top_k(le_sel, TOP_K)
    gate = p_top * jax.nn.softmax(vals, axis=-1)
    expert = g_idx * EXPERTS_PER_GROUP + e_idx

    A = T * TOP_K
    e_flat = expert.reshape(-1)
    order = jnp.argsort(e_flat)
    e_sorted = e_flat[order]
    tok_sorted = order // TOP_K
    gate_sorted = gate.reshape(-1)[order].astype(t.dtype)
    counts = jax.ops.segment_sum(jnp.ones((A,), jnp.int32), e_flat, num_segments=N_EXPERTS)
    start = jnp.cumsum(counts) - counts
    padded = (counts + MOE_BLOCK - 1) // MOE_BLOCK * MOE_BLOCK
    pad_end = jnp.cumsum(padded)
    pad_start = pad_end - padded
    dest = pad_start[e_sorted] + jnp.arange(A, dtype=jnp.int32) - start[e_sorted]
    n_blocks = -(-A // MOE_BLOCK) + N_EXPERTS
    buf = jnp.zeros((n_blocks * MOE_BLOCK, t.shape[1]), t.dtype).at[dest].set(t[tok_sorted])
    block_expert = jnp.minimum(
        jnp.searchsorted(pad_end, jnp.arange(n_blocks, dtype=jnp.int32) * MOE_BLOCK, side='right'),
        N_EXPERTS - 1)

    def expert_block(args):
        xb, e = args
        h = jax.nn.silu(xb @ w_gate[e]) * (xb @ w_up[e])
        return h @ w_down[e]

    out = lax.map(expert_block, (buf.reshape(n_blocks, MOE_BLOCK, -1), block_expert))
    y = out.reshape(n_blocks * MOE_BLOCK, -1)[dest] * gate_sorted[:, None]
    return jax.ops.segment_sum(y, tok_sorted, num_segments=T)


def setup_inputs(seed: int = 0) -> dict:
    key = jax.random.key(seed)
    ks = iter(jax.random.split(key, 64))
    f32 = jnp.float32
    D = D_MODEL
    NH, NS = N_HYENA_LAYERS, N_S5_LAYERS

    def nrm(shape, scale):
        return scale * jax.random.normal(next(ks), shape, f32)

    return {
        "x": nrm((BATCH, SEQ, D), 1.0),
        "c": nrm((BATCH, D), 1.0),
        "ctx": nrm((BATCH, CTX_LEN, D), 1.0),
        "c_ctx": nrm((D,), 1.0),
        "ada_w": nrm((DEPTH, D, 6 * D), 0.5 * D ** -0.5),
        "ada_b": nrm((DEPTH, 6 * D), 0.02),
        "norm_g": 1.0 + nrm((DEPTH, 2, D), 0.05),
        "final_g": 1.0 + nrm((D,), 0.05),
        "hy_w_in": nrm((NH, D, 3 * D), D ** -0.5),
        "hy_b_in": nrm((NH, 3 * D), 0.02),
        "hy_conv_w": nrm((NH, 3, 3 * D), 3 ** -0.5),
        "hy_conv_b": nrm((NH, 3 * D), 0.02),
        "hy_fw1": nrm((NH, HY_EMB, HY_ORDER), HY_EMB ** -0.5),
        "hy_fb1": nrm((NH, HY_ORDER), 0.5),
        "hy_fw2": nrm((NH, HY_ORDER, HY_ORDER), HY_ORDER ** -0.5),
        "hy_fb2": nrm((NH, HY_ORDER), 0.5),
        "hy_fw3": nrm((NH, HY_ORDER, 2 * D), HY_ORDER ** -0.5),
        "hy_freq": 1.0 + nrm((NH, HY_ORDER), 0.1),
        "hy_skip": nrm((NH, D), 0.5),
        "hy_w_out": nrm((NH, D, D), D ** -0.5),
        "hy_b_out": nrm((NH, D), 0.02),
        "s5_a_re": -0.5 + nrm((NS, 2, S5_G, S5_P), 0.01),
        "s5_a_im": math.pi * jnp.arange(S5_P, dtype=f32) + nrm((NS, 2, S5_G, S5_P), 0.01),
        "s5_log_step": jax.random.uniform(next(ks), (NS, 2, S5_G), f32,
                                          math.log(S5_DT_MIN), math.log(S5_DT_MAX)),
        "s5_b_re": nrm((NS, 2, S5_G, S5_P, S5_H), (2 * S5_H) ** -0.5),
        "s5_b_im": nrm((NS, 2, S5_G, S5_P, S5_H), (2 * S5_H) ** -0.5),
        "s5_c_re": nrm((NS, 2, S5_G, S5_H, S5_P), S5_P ** -0.5),
        "s5_c_im": nrm((NS, 2, S5_G, S5_H, S5_P), S5_P ** -0.5),
        "s5_d": nrm((NS, D), 1.0),
        "s5_w1": nrm((NS, D, D), D ** -0.5),
        "s5_b1": nrm((NS, D), 0.02),
        "s5_w2": nrm((NS, D, D), D ** -0.5),
        "s5_b2": nrm((NS, D), 0.02),
        "moe_wg": nrm((DEPTH, D, N_GROUPS), D ** -0.5),
        "moe_bg": nrm((DEPTH, N_GROUPS), 0.01),
        "moe_we": nrm((DEPTH, D, N_EXPERTS), D ** -0.5),
        "moe_be": nrm((DEPTH, N_EXPERTS), 0.01),
        "moe_w_gate": nrm((DEPTH, N_EXPERTS, D, D_EXPERT), D ** -0.5),
        "moe_w_up": nrm((DEPTH, N_EXPERTS, D, D_EXPERT), D ** -0.5),
        "moe_w_down": nrm((DEPTH, N_EXPERTS, D_EXPERT, D), D_EXPERT ** -0.5),
    }


def reference(x, c, ctx, c_ctx, ada_w, ada_b, norm_g, final_g,
              hy_w_in, hy_b_in, hy_conv_w, hy_conv_b, hy_fw1, hy_fb1, hy_fw2, hy_fb2, hy_fw3,
              hy_freq, hy_skip, hy_w_out, hy_b_out,
              s5_a_re, s5_a_im, s5_log_step, s5_b_re, s5_b_im, s5_c_re, s5_c_im, s5_d,
              s5_w1, s5_b1, s5_w2, s5_b2,
              moe_wg, moe_bg, moe_we, moe_be, moe_w_gate, moe_w_up, moe_w_down):
    xl, xc = x, ctx
    for i in range(DEPTH):
        last = i == DEPTH - 1
        j = i // N_MIXERS
        sh_a, sc_a, gt_a, sh_f, sc_f, gt_f = [m[:, None, :] for m in
                                             jnp.split(jax.nn.silu(c) @ ada_w[i] + ada_b[i], 6, axis=-1)]
        csh_a, csc_a, cgt_a, csh_f, csc_f, cgt_f = jnp.split(jax.nn.silu(c_ctx) @ ada_w[i] + ada_b[i], 6, axis=-1)

        hl = _modulate(_rmsnorm(xl, norm_g[i, 0]), sh_a, sc_a)
        hc = _modulate(_rmsnorm(xc, norm_g[i, 0]), csh_a, csc_a)
        if i % N_MIXERS == 0:
            ol = _hyena_mix(hl, hy_w_in[j], hy_b_in[j], hy_conv_w[j], hy_conv_b[j], hy_fw1[j], hy_fb1[j],
                            hy_fw2[j], hy_fb2[j], hy_fw3[j], hy_freq[j], hy_skip[j], hy_w_out[j], hy_b_out[j])
            oc = None if last else _hyena_mix(hc, hy_w_in[j], hy_b_in[j], hy_conv_w[j], hy_conv_b[j], hy_fw1[j],
                                              hy_fb1[j], hy_fw2[j], hy_fb2[j], hy_fw3[j], hy_freq[j],
                                              hy_skip[j], hy_w_out[j], hy_b_out[j])
        else:
            ol, oc = _s5_mix(hl, hc, s5_a_re[j], s5_a_im[j], s5_log_step[j], s5_b_re[j], s5_b_im[j],
                             s5_c_re[j], s5_c_im[j], s5_d[j], s5_w1[j], s5_b1[j], s5_w2[j], s5_b2[j],
                             not last)
        xl = xl + gt_a * ol
        if not last:
            xc = xc + cgt_a * oc

        n_lat = xl.shape[0] * xl.shape[1]
        tok = _modulate(_rmsnorm(xl, norm_g[i, 1]), sh_f, sc_f).reshape(n_lat, D_MODEL)
        if not last:
            tok_c = _modulate(_rmsnorm(xc, norm_g[i, 1]), csh_f, csc_f).reshape(-1, D_MODEL)
            tok = jnp.concatenate([tok, tok_c], axis=0)
        mo = _hier_moe(tok, moe_wg[i], moe_bg[i], moe_we[i], moe_be[i],
                       moe_w_gate[i], moe_w_up[i], moe_w_down[i])
        xl = xl + gt_f * mo[:n_lat].reshape(xl.shape)
        if not last:
            xc = xc + cgt_f * mo[n_lat:].reshape(xc.shape)
    return _rmsnorm(xl, final_g)
```

```python
import functools
import math

import jax
import jax.numpy as jnp
from jax import lax
from jax.experimental import pallas as pl
from jax.experimental.pallas import tpu as pltpu

F32 = jnp.float32
BF16 = jnp.bfloat16
HIGHEST = lax.Precision.HIGHEST

NORM_EPS = 1e-6
HY_DECAY_TARGET = 1e-2
HY_FAST_PCT = 0.3
HY_SLOW_PCT = 1.5
TOP_K = 2

V7X_VMEM_LIMIT_BYTES = 56 * 1024 * 1024
LANES = 128
SUBLANES = 8
S5_TAU = 8
S5_SEGS = 2
MOE_BM = 256
NEG_BIG = -1e30


def _cparams(*sem):
    return pltpu.CompilerParams(dimension_semantics=sem, vmem_limit_bytes=V7X_VMEM_LIMIT_BYTES)


def _norm_mod(x, g, shift, scale):
    ms = jnp.mean(x * x, axis=-1, keepdims=True)
    return (x * lax.rsqrt(ms + NORM_EPS) * g) * (1.0 + scale) + shift


def _dot(a, b):
    return jnp.dot(a, b, preferred_element_type=F32)


def _ada_body(c_ref, w_ref, b_ref, o_ref):
    x = c_ref[...]
    s = (x * jax.nn.sigmoid(x)).astype(BF16)
    o_ref[...] = _dot(s, w_ref[...].astype(BF16)) + b_ref[...]


def ada_mod(c_all, ada_w, ada_b):
    depth, d, n = ada_w.shape
    tn = min(n, 1024)
    return pl.pallas_call(
        _ada_body,
        grid=(depth, n // tn),
        in_specs=[pl.BlockSpec((SUBLANES, d), lambda l, j: (0, 0)),
                  pl.BlockSpec((None, d, tn), lambda l, j: (l, 0, j)),
                  pl.BlockSpec((None, 1, tn), lambda l, j: (l, 0, j))],
        out_specs=pl.BlockSpec((None, SUBLANES, tn), lambda l, j: (l, 0, j)),
        out_shape=jax.ShapeDtypeStruct((depth, SUBLANES, n), F32),
        compiler_params=_cparams("parallel", "parallel"),
        name="ada_mod",
    )(c_all, ada_w, ada_b.reshape(depth, 1, n))


def _hy_in_body(xp_ref, xm_ref, xn_ref, g_ref, sh_ref, sc_ref,
                w0_ref, w1_ref, w2_ref, b0_ref, b1_ref, b2_ref,
                cw0_ref, cw1_ref, cw2_ref, cb0_ref, cb1_ref, cb2_ref,
                v_ref, x0_ref):
    i = pl.program_id(2)
    ni = pl.num_programs(2)
    tm = xm_ref.shape[0]
    x = jnp.concatenate([xp_ref[...], xm_ref[...], xn_ref[...]], axis=0)
    h = _norm_mod(x, g_ref[...], sh_ref[...], sc_ref[...]).astype(BF16)
    rows = lax.broadcasted_iota(jnp.int32, (tm + 2 * SUBLANES, 1), 0)
    valid = jnp.logical_and(jnp.logical_or(rows >= SUBLANES, i > 0),
                            jnp.logical_or(rows < tm + SUBLANES, i < ni - 1))

    def part(w_ref, b_ref, cw_ref, cb_ref):
        z = jnp.where(valid, _dot(h, w_ref[...]) + b_ref[...], 0.0)
        cw = cw_ref[...]
        zp = pltpu.roll(z, 1, axis=0)[SUBLANES:tm + SUBLANES]
        zn = pltpu.roll(z, tm + 2 * SUBLANES - 1, axis=0)[SUBLANES:tm + SUBLANES]
        return zp * cw[0:1] + z[SUBLANES:tm + SUBLANES] * cw[1:2] + zn * cw[2:3] + cb_ref[...]

    x0 = part(w0_ref, b0_ref, cw0_ref, cb0_ref)
    x1 = part(w1_ref, b1_ref, cw1_ref, cb1_ref)
    v = part(w2_ref, b2_ref, cw2_ref, cb2_ref) * x1
    v_ref[...] = v.astype(BF16)
    x0_ref[...] = x0.astype(BF16)


def hyena_in(x, g, shift, scale, w_in, b_in, conv_w, conv_b):
    bsz, seq, d = x.shape
    tm = min(seq, 512)
    tn = min(d, 1024)
    nj = d // tn
    r8 = tm // SUBLANES
    last8 = seq // SUBLANES - 1
    row = lambda a: a.reshape(1, -1)
    wspec = lambda k: pl.BlockSpec((d, tn), lambda j, b, i: (0, k * nj + j))
    rspec = lambda k: pl.BlockSpec((1, tn), lambda j, b, i: (0, k * nj + j))
    cspec = lambda k: pl.BlockSpec((3, tn), lambda j, b, i: (0, k * nj + j))
    mspec = pl.BlockSpec((None, 1, d), lambda j, b, i: (b, 0, 0))
    out_spec = pl.BlockSpec((None, tm, tn), lambda j, b, i: (b, i, j))
    return pl.pallas_call(
        _hy_in_body,
        grid=(nj, bsz, seq // tm),
        in_specs=[pl.BlockSpec((None, SUBLANES, d), lambda j, b, i: (b, jnp.maximum(i * r8 - 1, 0), 0)),
                  pl.BlockSpec((None, tm, d), lambda j, b, i: (b, i, 0)),
                  pl.BlockSpec((None, SUBLANES, d), lambda j, b, i: (b, jnp.minimum((i + 1) * r8, last8), 0)),
                  pl.BlockSpec((1, d), lambda j, b, i: (0, 0)), mspec, mspec,
                  wspec(0), wspec(1), wspec(2), rspec(0), rspec(1), rspec(2),
                  cspec(0), cspec(1), cspec(2), rspec(0), rspec(1), rspec(2)],
        out_specs=[out_spec, out_spec],
        out_shape=[jax.ShapeDtypeStruct((bsz, seq, d), BF16)] * 2,
        compiler_params=_cparams("parallel", "parallel", "parallel"),
        name="hyena_in",
    )(x, x, x, row(g), shift, scale, w_in, w_in, w_in, row(b_in), row(b_in), row(b_in),
      conv_w, conv_w, conv_w, row(conv_b), row(conv_b), row(conv_b))


def _dft_tables(seq, blk):
    n = 2 * seq
    s = jnp.arange(seq, dtype=jnp.int32)[None, :]
    fl = jnp.arange(blk, dtype=jnp.int32)[:, None]
    fh = (jnp.arange(seq // blk, dtype=jnp.int32) * blk)[:, None]
    w = 2.0 * math.pi / n
    ang_b = ((fl * s) % n).astype(F32) * w
    ang_a = ((fh * s) % n).astype(F32) * w
    return (jnp.cos(ang_a)[:, None, :], jnp.sin(ang_a)[:, None, :], jnp.cos(ang_b), jnp.sin(ang_b))


def _dft_gen_body(ca_ref, sa_ref, cb_ref, sb_ref, c_ref, s_ref):
    ca, sa, cb, sb = ca_ref[...], sa_ref[...], cb_ref[...], sb_ref[...]
    c_ref[...] = (ca * cb - sa * sb).astype(BF16)
    s_ref[...] = (sa * cb + ca * sb).astype(BF16)


def dft_matrices(seq):
    blk = min(seq, 256)
    ca, sa, cb, sb = _dft_tables(seq, blk)
    aspec = pl.BlockSpec((None, 1, seq), lambda i: (i, 0, 0))
    bspec = pl.BlockSpec((blk, seq), lambda i: (0, 0))
    ospec = pl.BlockSpec((blk, seq), lambda i: (i, 0))
    return pl.pallas_call(
        _dft_gen_body,
        grid=(seq // blk,),
        in_specs=[aspec, aspec, bspec, bspec],
        out_specs=[ospec, ospec],
        out_shape=[jax.ShapeDtypeStruct((seq, seq), BF16)] * 2,
        compiler_params=_cparams("parallel"),
        name="dft_matrices",
    )(ca, sa, cb, sb)


def _alt_sign(rows):
    return jnp.where((rows & 1) == 0, 1.0, -1.0).astype(F32)


def _filt_body(h2_ref, wf_ref, wb_ref, dl_ref, a_ref, d_ref, ny_ref):
    seq = h2_ref.shape[0]
    h2 = h2_ref[...]
    row = lax.broadcasted_iota(jnp.int32, (seq, 1), 0)
    t = row.astype(F32) * (1.0 / (seq - 1))
    win = jnp.exp(-t * dl_ref[...])
    hf = jnp.dot(h2, wf_ref[...], precision=HIGHEST, preferred_element_type=F32) * win
    hb = jnp.dot(h2, wb_ref[...], precision=HIGHEST, preferred_element_type=F32) * win
    hb = jnp.where(row == 0, 0.0, hb)
    nrm = (jnp.sum(jnp.abs(hf), axis=0, keepdims=True) + jnp.sum(jnp.abs(hb), axis=0, keepdims=True))
    inv = 1.0 / nrm
    a = (hf + hb) * inv
    a_ref[...] = a.astype(BF16)
    d_ref[...] = ((hb - hf) * inv).astype(BF16)
    ny = jnp.sum(a * _alt_sign(row), axis=0, keepdims=True) * (1.0 / (2 * seq))
    ny_ref[...] = jnp.broadcast_to(ny, ny_ref.shape)


def _khat_body(a_ref, d_ref, c_ref, s_ref, kr_ref, ki_ref):
    i = pl.program_id(1)
    tm = c_ref.shape[0]
    seq = c_ref.shape[1]
    f = i * tm + lax.broadcasted_iota(jnp.int32, (tm, 1), 0)
    w = jnp.where(f == 0, 1.0, 2.0).astype(F32) * (1.0 / (2 * seq))
    kr_ref[...] = _dot(c_ref[...], a_ref[...]) * w
    ki_ref[...] = _dot(s_ref[...], d_ref[...]) * w


def hyena_filter_hat(seq, fw1, fb1, fw2, fb2, fw3, freq, cmat, smat):
    d = fw3.shape[1] // 2
    bands_n = (fw1.shape[0] - 1) // 2
    t = jnp.linspace(0.0, 1.0, seq, dtype=F32)[:, None]
    w = (2.0 * math.pi / seq) * jnp.arange(seq, dtype=F32)[:, None]
    bands = jnp.linspace(1e-4, bands_n - 1, bands_n, dtype=F32)[None, :]
    z = jnp.concatenate([t, jnp.cos(bands * w), -jnp.sin(bands * w)], axis=-1)
    h = jnp.sin(freq * (jnp.dot(z, fw1, precision=HIGHEST) + fb1))
    h2 = jnp.sin(freq * (jnp.dot(h, fw2, precision=HIGHEST) + fb2))
    max_decay = math.log(HY_DECAY_TARGET) / HY_FAST_PCT
    min_decay = math.log(HY_DECAY_TARGET) / HY_SLOW_PCT
    deltas = jnp.abs(jnp.linspace(min_decay, max_decay, d, dtype=F32))[None, :]

    order = h2.shape[1]
    tn = min(d, 256)
    nj = d // tn
    a, dd, ny = pl.pallas_call(
        _filt_body,
        grid=(nj,),
        in_specs=[pl.BlockSpec((seq, order), lambda j: (0, 0)),
                  pl.BlockSpec((order, tn), lambda j: (0, j)),
                  pl.BlockSpec((order, tn), lambda j: (0, nj + j)),
                  pl.BlockSpec((1, tn), lambda j: (0, j))],
        out_specs=[pl.BlockSpec((seq, tn), lambda j: (0, j)),
                   pl.BlockSpec((seq, tn), lambda j: (0, j)),
                   pl.BlockSpec((SUBLANES, tn), lambda j: (0, j))],
        out_shape=[jax.ShapeDtypeStruct((seq, d), BF16), jax.ShapeDtypeStruct((seq, d), BF16),
                   jax.ShapeDtypeStruct((SUBLANES, d), F32)],
        compiler_params=_cparams("parallel"),
        name="hyena_filter_taps",
    )(h2, fw3, fw3, deltas)

    tm = min(seq, 512)
    tn2 = min(d, 512)
    kr, ki = pl.pallas_call(
        _khat_body,
        grid=(d // tn2, seq // tm),
        in_specs=[pl.BlockSpec((seq, tn2), lambda j, i: (0, j)),
                  pl.BlockSpec((seq, tn2), lambda j, i: (0, j)),
                  pl.BlockSpec((tm, seq), lambda j, i: (i, 0)),
                  pl.BlockSpec((tm, seq), lambda j, i: (i, 0))],
        out_specs=[pl.BlockSpec((tm, tn2), lambda j, i: (i, j))] * 2,
        out_shape=[jax.ShapeDtypeStruct((seq, d), F32)] * 2,
        compiler_params=_cparams("parallel", "parallel"),
        name="hyena_filter_dft",
    )(a, dd, cmat, smat)
    return kr, ki, ny


def _dft_fwd_body(v_ref, c_ref, s_ref, kr_ref, ki_ref, kn_ref, ya_ref, yb_ref, yn_ref):
    i = pl.program_id(2)
    v = v_ref[...]
    vr = _dot(c_ref[...], v)
    p = _dot(s_ref[...], v)
    kr = kr_ref[...]
    ki = ki_ref[...]
    ya_ref[...] = (vr * kr + p * ki).astype(BF16)
    yb_ref[...] = (p * kr - vr * ki).astype(BF16)

    @pl.when(i == 0)
    def _():
        seq = v.shape[0]
        row = lax.broadcasted_iota(jnp.int32, (seq, 1), 0)
        vl = jnp.sum(v.astype(F32) * _alt_sign(row), axis=0, keepdims=True)
        yn_ref[...] = jnp.broadcast_to(vl * kn_ref[0:1, :], yn_ref.shape)


def _dft_inv_body(ya_ref, yb_ref, c_ref, s_ref, v_ref, x0_ref, skip_ref, yn_ref, o_ref):
    i = pl.program_id(2)
    tm = c_ref.shape[0]
    acc = _dot(c_ref[...], ya_ref[...]) + _dot(s_ref[...], yb_ref[...])
    t = i * tm + lax.broadcasted_iota(jnp.int32, (tm, 1), 0)
    y = acc + _alt_sign(t) * yn_ref[0:1, :] + skip_ref[...] * v_ref[...].astype(F32)
    o_ref[...] = (y * x0_ref[...].astype(F32)).astype(BF16)


def hyena_conv(v, x0, skip, kr, ki, kn, cmat, smat):
    bsz, seq, d = v.shape
    tm = min(seq, 512)
    tn = min(d, 512)
    grid = (bsz, d // tn, seq // tm)
    full = pl.BlockSpec((None, seq, tn), lambda b, j, i: (b, 0, j))
    mat = pl.BlockSpec((tm, seq), lambda b, j, i: (i, 0))
    tile = pl.BlockSpec((None, tm, tn), lambda b, j, i: (b, i, j))
    ktile = pl.BlockSpec((tm, tn), lambda b, j, i: (i, j))
    nyq = pl.BlockSpec((None, SUBLANES, tn), lambda b, j, i: (b, 0, j))
    ya, yb, yn = pl.pallas_call(
        _dft_fwd_body,
        grid=grid,
        in_specs=[full, mat, mat, ktile, ktile, pl.BlockSpec((SUBLANES, tn), lambda b, j, i: (0, j))],
        out_specs=[tile, tile, nyq],
        out_shape=[jax.ShapeDtypeStruct((bsz, seq, d), BF16)] * 2
        + [jax.ShapeDtypeStruct((bsz, SUBLANES, d), F32)],
        compiler_params=_cparams("parallel", "parallel", "arbitrary"),
        name="hyena_dft_fwd",
    )(v, cmat, smat, kr, ki, kn)
    return pl.pallas_call(
        _dft_inv_body,
        grid=grid,
        in_specs=[full, full, mat, mat, tile, tile, pl.BlockSpec((1, tn), lambda b, j, i: (0, j)), nyq],
        out_specs=tile,
        out_shape=jax.ShapeDtypeStruct((bsz, seq, d), BF16),
        compiler_params=_cparams("parallel", "parallel", "parallel"),
        name="hyena_dft_inv",
    )(ya, yb, cmat, smat, v, x0, skip.reshape(1, d), yn)


def _mm_res_body(x_ref, w_ref, b_ref, res_ref, gate_ref, o_ref):
    o_ref[...] = res_ref[...] + gate_ref[...] * (_dot(x_ref[...], w_ref[...]) + b_ref[...])


def mm_residual(x, w, b, res, gate):
    bsz, seq, k = x.shape
    n = w.shape[1]
    tm = min(seq, 512)
    return pl.pallas_call(
        _mm_res_body,
        grid=(bsz, seq // tm),
        in_specs=[pl.BlockSpec((None, tm, k), lambda b, i: (b, i, 0)),
                  pl.BlockSpec((k, n), lambda b, i: (0, 0)),
                  pl.BlockSpec((1, n), lambda b, i: (0, 0)),
                  pl.BlockSpec((None, tm, n), lambda b, i: (b, i, 0)),
                  pl.BlockSpec((None, 1, n), lambda b, i: (b, 0, 0))],
        out_specs=pl.BlockSpec((None, tm, n), lambda b, i: (b, i, 0)),
        out_shape=jax.ShapeDtypeStruct((bsz, seq, n), F32),
        compiler_params=_cparams("parallel", "parallel"),
        name="mm_residual",
    )(x, w, b.reshape(1, n), res, gate)


def _moe_pre_body(x_ref, g_ref, sh_ref, sc_ref, wr_ref, br_ref, tok_ref, eid_ref, gate_ref, *, n_groups, n_experts):
    tok = _norm_mod(x_ref[...], g_ref[...], sh_ref[...], sc_ref[...])
    tok_ref[...] = tok
    logits = jnp.dot(tok, wr_ref[...], precision=HIGHEST, preferred_element_type=F32) + br_ref[...]
    lane = lax.broadcasted_iota(jnp.int32, logits.shape, 1)
    per = n_experts // n_groups
    big = jnp.int32(1 << 20)
    gmask = jnp.logical_and(lane >= n_experts, lane < n_experts + n_groups)
    gl = jnp.where(gmask, logits, NEG_BIG)
    gmax = jnp.max(gl, axis=-1, keepdims=True)
    gidx = jnp.min(jnp.where(gl == gmax, lane - n_experts, big), axis=-1, keepdims=True)
    p_top = 1.0 / jnp.sum(jnp.where(gmask, jnp.exp(gl - gmax), 0.0), axis=-1, keepdims=True)
    lo = gidx * per
    emask = jnp.logical_and(lane >= lo, lane < lo + per)
    el = jnp.where(emask, logits, NEG_BIG)
    m1 = jnp.max(el, axis=-1, keepdims=True)
    i1 = jnp.min(jnp.where(el == m1, lane, big), axis=-1, keepdims=True)
    el2 = jnp.where(lane == i1, NEG_BIG, el)
    m2 = jnp.max(el2, axis=-1, keepdims=True)
    i2 = jnp.min(jnp.where(el2 == m2, lane, big), axis=-1, keepdims=True)
    e21 = jnp.exp(m2 - m1)
    g1 = p_top / (1.0 + e21)
    g2 = g1 * e21
    eid_ref[...] = jnp.where(lane == 0, i1, jnp.where(lane == 1, i2, 0))
    gate_ref[...] = jnp.where(lane == 0, g1, jnp.where(lane == 1, g2, 0.0))


def moe_pre(x, g, shift, scale, wr, br, n_groups, n_experts):
    bsz, seq, d = x.shape
    tm = min(seq, 256)
    mspec = pl.BlockSpec((None, 1, d), lambda b, i: (b, 0, 0))
    rout = pl.BlockSpec((None, tm, LANES), lambda b, i: (b, i, 0))
    return pl.pallas_call(
        functools.partial(_moe_pre_body, n_groups=n_groups, n_experts=n_experts),
        grid=(bsz, seq // tm),
        in_specs=[pl.BlockSpec((None, tm, d), lambda b, i: (b, i, 0)),
                  pl.BlockSpec((1, d), lambda b, i: (0, 0)), mspec, mspec,
                  pl.BlockSpec((d, LANES), lambda b, i: (0, 0)),
                  pl.BlockSpec((1, LANES), lambda b, i: (0, 0))],
        out_specs=[pl.BlockSpec((None, tm, d), lambda b, i: (b, i, 0)), rout, rout],
        out_shape=[jax.ShapeDtypeStruct((bsz, seq, d), F32),
                   jax.ShapeDtypeStruct((bsz, seq, LANES), jnp.int32),
                   jax.ShapeDtypeStruct((bsz, seq, LANES), F32)],
        compiler_params=_cparams("parallel", "parallel"),
        name="moe_pre",
    )(x, g.reshape(1, d), shift, scale, wr, br)


def _dispatch_body(dest_ref, tok_ref, xs_in_ref, xs_ref, sem):
    del xs_in_ref
    g = pl.program_id(0)
    rows = dest_ref.shape[0] // TOP_K

    def copy(r, k):
        return pltpu.make_async_copy(tok_ref.at[pl.ds(g * rows + r, 1)],
                                     xs_ref.at[pl.ds(dest_ref[TOP_K * r + k], 1)], sem)

    def start(r, c):
        for k in range(TOP_K):
            copy(r, k).start()
        return c

    def wait(r, c):
        for k in range(TOP_K):
            copy(r, k).wait()
        return c

    lax.fori_loop(0, rows, start, 0)
    lax.fori_loop(0, rows, wait, 0)


def moe_dispatch(tok, dest, n_rows):
    t, d = tok.shape
    rows = min(t, 256)
    xs0 = jnp.zeros((n_rows, d), tok.dtype)
    return pl.pallas_call(
        _dispatch_body,
        grid=(t // rows,),
        in_specs=[pl.BlockSpec((rows * TOP_K,), lambda g: (g,), memory_space=pltpu.SMEM),
                  pl.BlockSpec(memory_space=pl.ANY),
                  pl.BlockSpec(memory_space=pl.ANY)],
        out_specs=pl.BlockSpec(memory_space=pl.ANY),
        out_shape=jax.ShapeDtypeStruct((n_rows, d), tok.dtype),
        scratch_shapes=[pltpu.SemaphoreType.DMA(())],
        input_output_aliases={2: 0},
        compiler_params=_cparams("arbitrary"),
        name="moe_dispatch",
    )(dest, tok, xs0)


def _expert_body(be_ref, bv_ref, x_ref, wg_ref, wu_ref, wd_ref, o_ref):
    del be_ref
    i = pl.program_id(0)

    @pl.when(bv_ref[i] > 0)
    def _():
        x = x_ref[...].astype(BF16)
        gate = _dot(x, wg_ref[...])
        up = _dot(x, wu_ref[...])
        h = (gate * jax.nn.sigmoid(gate) * up).astype(BF16)
        o_ref[...] = _dot(h, wd_ref[...])

    @pl.when(bv_ref[i] == 0)
    def _():
        o_ref[...] = jnp.zeros_like(o_ref)


def moe_experts(xs, block_expert, block_valid, w_gate, w_up, w_down):
    n_rows, d = xs.shape
    n_blocks = n_rows // MOE_BM
    dh = w_gate.shape[2]
    grid_spec = pltpu.PrefetchScalarGridSpec(
        num_scalar_prefetch=2,
        grid=(n_blocks,),
        in_specs=[pl.BlockSpec((MOE_BM, d), lambda i, be, bv: (i, 0)),
                  pl.BlockSpec((None, d, dh), lambda i, be, bv: (be[i], 0, 0)),
                  pl.BlockSpec((None, d, dh), lambda i, be, bv: (be[i], 0, 0)),
                  pl.BlockSpec((None, dh, d), lambda i, be, bv: (be[i], 0, 0))],
        out_specs=pl.BlockSpec((MOE_BM, d), lambda i, be, bv: (i, 0)),
    )
    return pl.pallas_call(
        _expert_body,
        grid_spec=grid_spec,
        out_shape=jax.ShapeDtypeStruct((n_rows, d), F32),
        compiler_params=_cparams("arbitrary"),
        name="moe_experts",
    )(block_expert, block_valid, xs, w_gate, w_up, w_down)


def _combine_body(dest_ref, os_ref, gate_ref, res_ref, gt_ref, fg_ref, o_ref, buf, sem, *, final_norm):
    rows = res_ref.shape[0]

    def copy(r, k):
        return pltpu.make_async_copy(os_ref.at[pl.ds(dest_ref[TOP_K * r + k], 1)], buf.at[k, pl.ds(r, 1)], sem)

    def start(r, c):
        for k in range(TOP_K):
            copy(r, k).start()
        return c

    def wait(r, c):
        for k in range(TOP_K):
            copy(r, k).wait()
        return c

    lax.fori_loop(0, rows, start, 0)
    lax.fori_loop(0, rows, wait, 0)
    gates = gate_ref[...]
    mo = gates[:, 0:1] * buf[0] + gates[:, 1:2] * buf[1]
    y = res_ref[...] + gt_ref[...] * mo
    if final_norm:
        ms = jnp.mean(y * y, axis=-1, keepdims=True)
        y = y * lax.rsqrt(ms + NORM_EPS) * fg_ref[...]
    o_ref[...] = y


def moe_combine(os, dest, gates, res, gt, final_g, final_norm):
    bsz, seq, d = res.shape
    rows = min(seq, 256)
    nt = seq // rows
    return pl.pallas_call(
        functools.partial(_combine_body, final_norm=final_norm),
        grid=(bsz, nt),
        in_specs=[pl.BlockSpec((rows * TOP_K,), lambda b, i: (b * nt + i,), memory_space=pltpu.SMEM),
                  pl.BlockSpec(memory_space=pl.ANY),
                  pl.BlockSpec((None, rows, LANES), lambda b, i: (b, i, 0)),
                  pl.BlockSpec((None, rows, d), lambda b, i: (b, i, 0)),
                  pl.BlockSpec((None, 1, d), lambda b, i: (b, 0, 0)),
                  pl.BlockSpec((1, d), lambda b, i: (0, 0))],
        out_specs=pl.BlockSpec((None, rows, d), lambda b, i: (b, i, 0)),
        out_shape=jax.ShapeDtypeStruct((bsz, seq, d), F32),
        scratch_shapes=[pltpu.VMEM((TOP_K, rows, d), F32), pltpu.SemaphoreType.DMA(())],
        compiler_params=_cparams("arbitrary", "arbitrary"),
        name="moe_combine",
    )(dest, os, gates, res, gt, final_g.reshape(1, d))


def _route_plan(eid, n_experts):
    e_flat = eid.reshape(-1)
    a = e_flat.shape[0]
    onehot = (e_flat[:, None] == jnp.arange(n_experts, dtype=jnp.int32)[None, :]).astype(jnp.int32)
    csum = jnp.cumsum(onehot, axis=0)
    counts = csum[-1]
    rank = jnp.sum(onehot * csum, axis=1) - 1
    padded = (counts + MOE_BM - 1) // MOE_BM * MOE_BM
    pad_end = jnp.cumsum(padded)
    pad_start = pad_end - padded
    dest = jnp.sum(onehot * pad_start[None, :], axis=1) + rank
    n_blocks = -(-a // MOE_BM) + n_experts
    starts = jnp.arange(n_blocks, dtype=jnp.int32) * MOE_BM
    block_expert = jnp.minimum(jnp.sum((pad_end[None, :] <= starts[:, None]).astype(jnp.int32), axis=1),
                               n_experts - 1)
    block_valid = (starts < pad_end[-1]).astype(jnp.int32)
    return dest.astype(jnp.int32), block_expert.astype(jnp.int32), block_valid, n_blocks * MOE_BM


def hier_moe(streams, norm_g, wg, bg, we, be, w_gate, w_up, w_down, final_g, final_norm):
    n_groups = wg.shape[1]
    n_experts = we.shape[1]
    d = wg.shape[0]
    wr = jnp.zeros((d, LANES), F32).at[:, :n_experts].set(we).at[:, n_experts:n_experts + n_groups].set(wg)
    br = jnp.zeros((1, LANES), F32).at[0, :n_experts].set(be).at[0, n_experts:n_experts + n_groups].set(bg)
    toks, eids, gates = [], [], []
    for x, sh, sc, _ in streams:
        tok, eid, gate = moe_pre(x, norm_g, sh, sc, wr, br, n_groups, n_experts)
        toks.append(tok.reshape(-1, d))
        eids.append(eid[..., :TOP_K].reshape(-1, TOP_K))
        gates.append(gate)
    tok_all = jnp.concatenate(toks, axis=0) if len(toks) > 1 else toks[0]
    eid_all = jnp.concatenate(eids, axis=0) if len(eids) > 1 else eids[0]
    dest, block_expert, block_valid, n_rows = _route_plan(eid_all, n_experts)
    xs = moe_dispatch(tok_all, dest, n_rows)
    os = moe_experts(xs, block_expert, block_valid,
                     w_gate.astype(BF16), w_up.astype(BF16), w_down.astype(BF16))
    outs = []
    off = 0
    for (x, _, _, gt), gate in zip(streams, gates):
        n_assign = x.shape[0] * x.shape[1] * TOP_K
        outs.append(moe_combine(os, dest[off:off + n_assign], gate, x, gt, final_g, final_norm))
        off += n_assign
    return outs


def _norm_mod_body(x_ref, g_ref, sh_ref, sc_ref, o_ref):
    o_ref[...] = _norm_mod(x_ref[...], g_ref[...], sh_ref[...], sc_ref[...]).astype(o_ref.dtype)


def norm_mod(x, g, shift, scale):
    bsz, seq, d = x.shape
    tm = min(seq, 512)
    mspec = pl.BlockSpec((None, 1, d), lambda b, i: (b, 0, 0))
    return pl.pallas_call(
        _norm_mod_body,
        grid=(bsz, seq // tm),
        in_specs=[pl.BlockSpec((None, tm, d), lambda b, i: (b, i, 0)),
                  pl.BlockSpec((1, d), lambda b, i: (0, 0)), mspec, mspec],
        out_specs=pl.BlockSpec((None, tm, d), lambda b, i: (b, i, 0)),
        out_shape=jax.ShapeDtypeStruct((bsz, seq, d), BF16),
        compiler_params=_cparams("parallel", "parallel"),
        name="norm_mod",
    )(x, g.reshape(1, d), shift, scale)


def _s5_arrange(h):
    bsz, t, d = h.shape
    c = t // (S5_SEGS * S5_TAU)
    h = h.reshape(bsz, S5_SEGS, c, S5_TAU, d // LANES, LANES)
    return h.transpose(2, 0, 1, 4, 3, 5).reshape(c * bsz * S5_SEGS, d * S5_TAU)


def _s5_unarrange(y, bsz):
    r, w = y.shape
    d = w // S5_TAU
    c = r // (bsz * S5_SEGS)
    y = y.reshape(c, bsz, S5_SEGS, d // LANES, S5_TAU, LANES)
    return y.transpose(1, 2, 0, 4, 3, 5).reshape(bsz, S5_SEGS * c * S5_TAU, d)


def _s5_operators(a_re, a_im, log_step, b_re, b_im, c_re, c_im):
    n_g, n_p = a_re.shape[1:]
    n_h = b_re.shape[-1]
    gpt = LANES // n_h
    n_j = n_g // gpt
    tau = S5_TAU
    lam_step = lax.complex(a_re, a_im) * jnp.exp(log_step)[..., None]
    lam_bar = jnp.exp(lam_step)
    b_bar = ((lam_bar - 1.0) / lax.complex(a_re, a_im))[..., None] * lax.complex(b_re, b_im)
    c_mat = lax.complex(c_re, c_im)
    ks = jnp.arange(tau + 1, dtype=F32)[None, :, None, None]
    pw = jnp.exp(lam_step[:, None] * ks)
    eye = jnp.eye(gpt, dtype=F32)
    ein = functools.partial(jnp.einsum, precision=HIGHEST)
    ws, wh, lt = [], [], []
    wu = 0.0
    for d in range(2):
        pos = jnp.arange(tau) if d == 0 else jnp.arange(tau)[::-1]
        inj = pw[d][tau - 1 - pos][..., None] * b_bar[d][None]
        inj = inj.reshape(tau, n_j, gpt, n_p, n_h)
        w_re = ein('tjgph,gk->jtghkp', inj.real, eye).reshape(n_j, tau * LANES, gpt * n_p)
        w_im = ein('tjgph,gk->jtghkp', inj.imag, eye).reshape(n_j, tau * LANES, gpt * n_p)
        ws += [w_re, w_im]
        cl = c_mat[d][None] * pw[d][pos + 1][:, :, None, :]
        cl = cl.reshape(tau, n_j, gpt, n_h, n_p)
        h_re = ein('tjgop,gk->jgptko', cl.real, eye).reshape(n_j, gpt * n_p, tau * LANES)
        h_im = ein('tjgop,gk->jgptko', -cl.imag, eye).reshape(n_j, gpt * n_p, tau * LANES)
        wh.append(jnp.concatenate([h_re, h_im], axis=1))
        mk = ein('gop,kgp,gph->kgoh', c_mat[d], pw[d][:tau], b_bar[d]).real
        diff = pos[:, None] - pos[None, :]
        tz = jnp.where((diff >= 0)[:, :, None, None, None], mk[jnp.clip(diff, 0, tau - 1)], 0.0)
        tz = tz.reshape(tau, tau, n_j, gpt, n_h, n_h)
        wu = wu + ein('stjgoh,gk->jtghsko', tz, eye).reshape(n_j, tau * LANES, tau * LANES)
        lt_d = pw[d][tau].reshape(n_j, 1, gpt * n_p)
        lt.append(jnp.concatenate([lt_d.real, lt_d.imag], axis=-1))
    return (jnp.concatenate(ws, axis=-1).astype(BF16), wu.astype(BF16), jnp.stack(wh).astype(BF16),
            jnp.stack(lt).astype(F32))


def _s5_inj_body(x_ref, w_ref, o_ref):
    o_ref[...] = _dot(x_ref[...], w_ref[...])


def s5_inject(xr, ws):
    r = xr.shape[0]
    n_j, k, n = ws.shape
    tm = r // 2 if r % 32 == 0 else r
    return pl.pallas_call(
        _s5_inj_body,
        grid=(n_j, r // tm),
        in_specs=[pl.BlockSpec((tm, k), lambda j, i: (i, j)),
                  pl.BlockSpec((None, k, n), lambda j, i: (j, 0, 0))],
        out_specs=pl.BlockSpec((tm, n), lambda j, i: (i, j)),
        out_shape=jax.ShapeDtypeStruct((r, n_j * n), F32),
        compiler_params=_cparams("parallel", "parallel"),
        name="s5_inject",
    )(xr, ws)


def _cmul(ar, ai, br, bi):
    return ar * br - ai * bi, ar * bi + ai * br


def _s5_scan_body(s_ref, lt_ref, h_ref, raw_ref, *, n_ctx, n_lat, bsz):
    d = pl.program_id(1)
    w2 = lt_ref.shape[-1]
    w = w2 // 2
    rows = bsz * S5_SEGS
    seg = lax.broadcasted_iota(jnp.int32, (rows, 1), 0) % S5_SEGS
    is_late = seg != d
    lam_r = lt_ref[:, 0:w]
    lam_i = lt_ref[:, w:w2]
    zero = jnp.zeros((rows, w), F32)
    one = (jnp.ones((1, w), F32), jnp.zeros((1, w), F32))

    def swap_segments(x):
        return jnp.where(seg == 0, pltpu.roll(x, rows - 1, axis=0), pltpu.roll(x, 1, axis=0))

    def phase(c0, n_steps, hin_r, hin_i, write):
        def chunk(k):
            return c0 + jnp.where(d == 0, k, n_steps - 1 - k)

        def step_raw(k, carry):
            hr, hi = carry
            c = chunk(k)
            raw_ref[c - c0, :, 0:w] = hr
            raw_ref[c - c0, :, w:w2] = hi
            nr, ni = _cmul(lam_r, lam_i, hr, hi)
            return nr + s_ref[c, :, 0:w], ni + s_ref[c, :, w:w2]

        er, ei = lax.fori_loop(0, n_steps, step_raw, (zero, zero))
        pr, pi = lax.fori_loop(0, n_steps, lambda k, q: _cmul(lam_r, lam_i, *q), one)
        dr, di = _cmul(pr, pi, hin_r, hin_i)
        first_r = jnp.where(is_late, 0.0, er + dr)
        first_i = jnp.where(is_late, 0.0, ei + di)
        carry_r = jnp.where(is_late, swap_segments(first_r), hin_r)
        carry_i = jnp.where(is_late, swap_segments(first_i), hin_i)
        if write:
            def step_fix(k, q):
                c = chunk(k)
                fr, fi = _cmul(q[0], q[1], carry_r, carry_i)
                h_ref[c - c0, :, 0:w] = (raw_ref[c - c0, :, 0:w] + fr).astype(h_ref.dtype)
                h_ref[c - c0, :, w:w2] = (raw_ref[c - c0, :, w:w2] + fi).astype(h_ref.dtype)
                return _cmul(lam_r, lam_i, q[0], q[1])

            lax.fori_loop(0, n_steps, step_fix, one)
        lr, li = _cmul(pr, pi, carry_r, carry_i)
        last_r = jnp.where(is_late, er + lr, 0.0)
        last_i = jnp.where(is_late, ei + li, 0.0)
        return (jnp.where(is_late, 0.0, swap_segments(last_r)), jnp.where(is_late, 0.0, swap_segments(last_i)))

    hr, hi = zero, zero
    if n_ctx:
        hr, hi = phase(0, n_ctx, hr, hi, False)
    phase(n_ctx, n_lat, hr, hi, True)


def s5_scan(s, lt, n_ctx, n_lat, bsz):
    assert S5_SEGS == 2
    n_c, rows, _ = s.shape
    n_j = lt.shape[1]
    w2 = lt.shape[-1]
    return pl.pallas_call(
        functools.partial(_s5_scan_body, n_ctx=n_ctx, n_lat=n_lat, bsz=bsz),
        grid=(n_j, 2),
        in_specs=[pl.BlockSpec((n_c, rows, w2), lambda j, d: (0, 0, 2 * j + d)),
                  pl.BlockSpec((None, None, 1, w2), lambda j, d: (d, j, 0, 0))],
        out_specs=pl.BlockSpec((None, n_lat, rows, w2), lambda j, d: (d, 0, 0, j)),
        out_shape=jax.ShapeDtypeStruct((2, n_lat, rows, n_j * w2), BF16),
        scratch_shapes=[pltpu.VMEM((max(n_ctx, n_lat), rows, w2), F32)],
        compiler_params=_cparams("parallel", "parallel"),
        name="s5_scan",
    )(s, lt)


def _s5_out_body(x_ref, hf_ref, hb_ref, wu_ref, whf_ref, whb_ref, o_ref):
    o_ref[...] = (_dot(x_ref[...], wu_ref[...]) + _dot(hf_ref[...], whf_ref[...])
                  + _dot(hb_ref[...], whb_ref[...]))


def s5_readout(xr, h, wu, wh):
    r = xr.shape[0]
    n_j, k, n = wu.shape
    w2 = wh.shape[2]
    tm = min(r, 1024)
    return pl.pallas_call(
        _s5_out_body,
        grid=(n_j, r // tm),
        in_specs=[pl.BlockSpec((tm, k), lambda j, i: (i, j)),
                  pl.BlockSpec((None, tm, w2), lambda j, i: (0, i, j)),
                  pl.BlockSpec((None, tm, w2), lambda j, i: (1, i, j)),
                  pl.BlockSpec((None, k, n), lambda j, i: (j, 0, 0)),
                  pl.BlockSpec((None, None, w2, n), lambda j, i: (0, j, 0, 0)),
                  pl.BlockSpec((None, None, w2, n), lambda j, i: (1, j, 0, 0))],
        out_specs=pl.BlockSpec((tm, n), lambda j, i: (i, j)),
        out_shape=jax.ShapeDtypeStruct((r, n_j * n), F32),
        compiler_params=_cparams("parallel", "parallel"),
        name="s5_readout",
    )(xr, h, h, wu, wh, wh)


def _gelu_tanh(x):
    return 0.5 * x * (1.0 + jnp.tanh(math.sqrt(2.0 / math.pi) * (x + 0.044715 * (x * x * x))))


def _s5_glu_body(x_ref, y_ref, g_ref, sh_ref, sc_ref, dk_ref, w1_ref, w2_ref, b1_ref, b2_ref, gt_ref, o_ref):
    x = x_ref[...]
    u = _norm_mod(x, g_ref[...], sh_ref[...], sc_ref[...])
    y = _gelu_tanh(y_ref[...] + dk_ref[...] * u).astype(BF16)
    o = (_dot(y, w1_ref[...]) + b1_ref[...]) * jax.nn.sigmoid(_dot(y, w2_ref[...]) + b2_ref[...])
    o_ref[...] = x + gt_ref[...] * o


def s5_glu(x, y, g, shift, scale, d_skip, w1, b1, w2, b2, gate):
    bsz, seq, d = x.shape
    tm = min(seq, 256)
    row = lambda a: a.reshape(1, d)
    rspec = pl.BlockSpec((1, d), lambda b, i: (0, 0))
    mspec = pl.BlockSpec((None, 1, d), lambda b, i: (b, 0, 0))
    tile = pl.BlockSpec((None, tm, d), lambda b, i: (b, i, 0))
    wspec = pl.BlockSpec((d, d), lambda b, i: (0, 0))
    return pl.pallas_call(
        _s5_glu_body,
        grid=(bsz, seq // tm),
        in_specs=[tile, tile, rspec, mspec, mspec, rspec, wspec, wspec, rspec, rspec, mspec],
        out_specs=tile,
        out_shape=jax.ShapeDtypeStruct((bsz, seq, d), F32),
        compiler_params=_cparams("parallel", "parallel"),
        name="s5_glu",
    )(x, y, row(g), shift, scale, row(d_skip), w1, w2, row(b1), row(b2), gate)


def s5_mix(xl, xc, g, sh_l, sc_l, sh_c, sc_c, gate_l, a_re, a_im, log_step, b_re, b_im, c_re, c_im, d_skip,
           w1, b1, w2, b2):
    bsz, seq, d = xl.shape
    hl = norm_mod(xl, g, sh_l, sc_l)
    hc = norm_mod(xc, g, sh_c, sc_c)
    xr_c = _s5_arrange(hc)
    xr_l = _s5_arrange(hl)
    rows = bsz * S5_SEGS
    n_ctx = xr_c.shape[0] // rows
    n_lat = xr_l.shape[0] // rows
    ws, wu, wh, lt = _s5_operators(a_re, a_im, log_step, b_re, b_im, c_re, c_im)
    s = s5_inject(jnp.concatenate([xr_c, xr_l], axis=0), ws)
    h = s5_scan(s.reshape(n_ctx + n_lat, rows, -1), lt, n_ctx, n_lat, bsz)
    y = s5_readout(xr_l, h.reshape(2, n_lat * rows, -1), wu, wh)
    y = _s5_unarrange(y, bsz)
    return s5_glu(xl, y, g, sh_l, sc_l, d_skip, w1.astype(BF16), b1, w2.astype(BF16), b2, gate_l)


def hyena_mix(x, g, shift, scale, gate, w_in, b_in, conv_w, conv_b, fw1, fb1, fw2, fb2, fw3, freq, skip,
              w_out, b_out):
    seq = x.shape[1]
    cmat, smat = dft_matrices(seq)
    kr, ki, kn = hyena_filter_hat(seq, fw1, fb1, fw2, fb2, fw3, freq, cmat, smat)
    v, x0 = hyena_in(x, g, shift, scale, w_in, b_in, conv_w, conv_b)
    yg = hyena_conv(v, x0, skip, kr, ki, kn, cmat, smat)
    return mm_residual(yg, w_out, b_out, x, gate)


def kernel(x, c, ctx, c_ctx, ada_w, ada_b, norm_g, final_g, hy_w_in, hy_b_in, hy_conv_w, hy_conv_b, hy_fw1,
           hy_fb1, hy_fw2, hy_fb2, hy_fw3, hy_freq, hy_skip, hy_w_out, hy_b_out, s5_a_re, s5_a_im,
           s5_log_step, s5_b_re, s5_b_im, s5_c_re, s5_c_im, s5_d, s5_w1, s5_b1, s5_w2, s5_b2, moe_wg, moe_bg,
           moe_we, moe_be, moe_w_gate, moe_w_up, moe_w_down):
    bsz, _, d = x.shape
    depth = ada_w.shape[0]
    assert depth == 2 and bsz < SUBLANES
    c_all = jnp.zeros((SUBLANES, d), F32).at[:bsz].set(c).at[bsz].set(c_ctx)
    mods = ada_mod(c_all, ada_w, ada_b)

    def mod_rows(layer, k):
        lat = mods[layer, :bsz, k * d:(k + 1) * d][:, None, :]
        cx = jnp.broadcast_to(mods[layer, bsz, k * d:(k + 1) * d][None, None, :], (bsz, 1, d))
        return lat, cx

    (sh_a, csh_a), (sc_a, csc_a), (gt_a, cgt_a) = mod_rows(0, 0), mod_rows(0, 1), mod_rows(0, 2)
    (sh_f, csh_f), (sc_f, csc_f), (gt_f, cgt_f) = mod_rows(0, 3), mod_rows(0, 4), mod_rows(0, 5)
    hy = (hy_w_in[0].astype(BF16), hy_b_in[0], hy_conv_w[0], hy_conv_b[0], hy_fw1[0], hy_fb1[0], hy_fw2[0],
          hy_fb2[0], hy_fw3[0], hy_freq[0], hy_skip[0], hy_w_out[0].astype(BF16), hy_b_out[0])
    xl = hyena_mix(x, norm_g[0, 0], sh_a, sc_a, gt_a, *hy)
    xc = hyena_mix(ctx, norm_g[0, 0], csh_a, csc_a, cgt_a, *hy)
    xl, xc = hier_moe([(xl, sh_f, sc_f, gt_f), (xc, csh_f, csc_f, cgt_f)], norm_g[0, 1],
                      moe_wg[0], moe_bg[0], moe_we[0], moe_be[0], moe_w_gate[0], moe_w_up[0], moe_w_down[0],
                      final_g, False)

    (sh_a, csh_a), (sc_a, csc_a), (gt_a, _) = mod_rows(1, 0), mod_rows(1, 1), mod_rows(1, 2)
    (sh_f, _), (sc_f, _), (gt_f, _) = mod_rows(1, 3), mod_rows(1, 4), mod_rows(1, 5)
    xl = s5_mix(xl, xc, norm_g[1, 0], sh_a, sc_a, csh_a, csc_a, gt_a, s5_a_re[0], s5_a_im[0], s5_log_step[0],
                s5_b_re[0], s5_b_im[0], s5_c_re[0], s5_c_im[0], s5_d[0], s5_w1[0], s5_b1[0], s5_w2[0], s5_b2[0])
    (out,) = hier_moe([(xl, sh_f, sc_f, gt_f)], norm_g[1, 1], moe_wg[1], moe_bg[1], moe_we[1], moe_be[1],
                      moe_w_gate[1], moe_w_up[1], moe_w_down[1], final_g, True)
    return out
```

```python
import functools
import math

import jax
import jax.numpy as jnp
from jax import lax
from jax.experimental import pallas as pl
from jax.experimental.pallas import tpu as pltpu

F32 = jnp.float32
BF16 = jnp.bfloat16
HIGHEST = lax.Precision.HIGHEST

NORM_EPS = 1e-6
HY_DECAY_TARGET = 1e-2
HY_FAST_PCT = 0.3
HY_SLOW_PCT = 1.5
TOP_K = 2

V7X_VMEM_LIMIT_BYTES = 56 * 1024 * 1024
LANES = 128
SUBLANES = 8
S5_TAU = 8
S5_SEGS = 2
MOE_BM = 256
NEG_BIG = -1e30


def _cparams(*sem):
    return pltpu.CompilerParams(dimension_semantics=sem, vmem_limit_bytes=V7X_VMEM_LIMIT_BYTES)


def _norm_mod(x, g, shift, scale):
    ms = jnp.mean(x * x, axis=-1, keepdims=True)
    return (x * lax.rsqrt(ms + NORM_EPS) * g) * (1.0 + scale) + shift


def _dot(a, b):
    return jnp.dot(a, b, preferred_element_type=F32)


def _ada_body(c_ref, w_ref, b_ref, o_ref):
    x = c_ref[...]
    s = (x * jax.nn.sigmoid(x)).astype(BF16)
    o_ref[...] = _dot(s, w_ref[...].astype(BF16)) + b_ref[...]


def ada_mod(c_all, ada_w, ada_b):
    depth, d, n = ada_w.shape
    tn = min(n, 1024)
    return pl.pallas_call(
        _ada_body,
        grid=(depth, n // tn),
        in_specs=[pl.BlockSpec((SUBLANES, d), lambda l, j: (0, 0)),
                  pl.BlockSpec((None, d, tn), lambda l, j: (l, 0, j)),
                  pl.BlockSpec((None, 1, tn), lambda l, j: (l, 0, j))],
        out_specs=pl.BlockSpec((None, SUBLANES, tn), lambda l, j: (l, 0, j)),
        out_shape=jax.ShapeDtypeStruct((depth, SUBLANES, n), F32),
        compiler_params=_cparams("parallel", "parallel"),
        name="ada_mod",
    )(c_all, ada_w, ada_b.reshape(depth, 1, n))


def _hy_in_body(xp_ref, xm_ref, xn_ref, g_ref, sh_ref, sc_ref,
                w0_ref, w1_ref, w2_ref, b0_ref, b1_ref, b2_ref,
                cw0_ref, cw1_ref, cw2_ref, cb0_ref, cb1_ref, cb2_ref,
                v_ref, x0_ref):
    i = pl.program_id(2)
    ni = pl.num_programs(2)
    tm = xm_ref.shape[0]
    x = jnp.concatenate([xp_ref[...], xm_ref[...], xn_ref[...]], axis=0)
    h = _norm_mod(x, g_ref[...], sh_ref[...], sc_ref[...]).astype(BF16)
    rows = lax.broadcasted_iota(jnp.int32, (tm + 2 * SUBLANES, 1), 0)
    valid = jnp.logical_and(jnp.logical_or(rows >= SUBLANES, i > 0),
                            jnp.logical_or(rows < tm + SUBLANES, i < ni - 1))

    def part(w_ref, b_ref, cw_ref, cb_ref):
        z = jnp.where(valid, _dot(h, w_ref[...]) + b_ref[...], 0.0)
        cw = cw_ref[...]
        zp = pltpu.roll(z, 1, axis=0)[SUBLANES:tm + SUBLANES]
        zn = pltpu.roll(z, tm + 2 * SUBLANES - 1, axis=0)[SUBLANES:tm + SUBLANES]
        return zp * cw[0:1] + z[SUBLANES:tm + SUBLANES] * cw[1:2] + zn * cw[2:3] + cb_ref[...]

    x0 = part(w0_ref, b0_ref, cw0_ref, cb0_ref)
    x1 = part(w1_ref, b1_ref, cw1_ref, cb1_ref)
    v = part(w2_ref, b2_ref, cw2_ref, cb2_ref) * x1
    v_ref[...] = v.astype(BF16)
    x0_ref[...] = x0.astype(BF16)


def hyena_in(x, g, shift, scale, w_in, b_in, conv_w, conv_b):
    bsz, seq, d = x.shape
    tm = min(seq, 512)
    tn = min(d, 1024)
    nj = d // tn
    r8 = tm // SUBLANES
    last8 = seq // SUBLANES - 1
    row = lambda a: a.reshape(1, -1)
    wspec = lambda k: pl.BlockSpec((d, tn), lambda j, b, i: (0, k * nj + j))
    rspec = lambda k: pl.BlockSpec((1, tn), lambda j, b, i: (0, k * nj + j))
    cspec = lambda k: pl.BlockSpec((3, tn), lambda j, b, i: (0, k * nj + j))
    mspec = pl.BlockSpec((None, 1, d), lambda j, b, i: (b, 0, 0))
    out_spec = pl.BlockSpec((None, tm, tn), lambda j, b, i: (b, i, j))
    return pl.pallas_call(
        _hy_in_body,
        grid=(nj, bsz, seq // tm),
        in_specs=[pl.BlockSpec((None, SUBLANES, d), lambda j, b, i: (b, jnp.maximum(i * r8 - 1, 0), 0)),
                  pl.BlockSpec((None, tm, d), lambda j, b, i: (b, i, 0)),
                  pl.BlockSpec((None, SUBLANES, d), lambda j, b, i: (b, jnp.minimum((i + 1) * r8, last8), 0)),
                  pl.BlockSpec((1, d), lambda j, b, i: (0, 0)), mspec, mspec,
                  wspec(0), wspec(1), wspec(2), rspec(0), rspec(1), rspec(2),
                  cspec(0), cspec(1), cspec(2), rspec(0), rspec(1), rspec(2)],
        out_specs=[out_spec, out_spec],
        out_shape=[jax.ShapeDtypeStruct((bsz, seq, d), BF16)] * 2,
        compiler_params=_cparams("parallel", "parallel", "parallel"),
        name="hyena_in",
    )(x, x, x, row(g), shift, scale, w_in, w_in, w_in, row(b_in), row(b_in), row(b_in),
      conv_w, conv_w, conv_w, row(conv_b), row(conv_b), row(conv_b))


def _dft_tables(seq, blk):
    n = 2 * seq
    s = jnp.arange(seq, dtype=jnp.int32)[None, :]
    fl = jnp.arange(blk, dtype=jnp.int32)[:, None]
    fh = (jnp.arange(seq // blk, dtype=jnp.int32) * blk)[:, None]
    w = 2.0 * math.pi / n
    ang_b = ((fl * s) % n).astype(F32) * w
    ang_a = ((fh * s) % n).astype(F32) * w
    return (jnp.cos(ang_a)[:, None, :], jnp.sin(ang_a)[:, None, :], jnp.cos(ang_b), jnp.sin(ang_b))


def _dft_gen_body(ca_ref, sa_ref, cb_ref, sb_ref, c_ref, s_ref):
    ca, sa, cb, sb = ca_ref[...], sa_ref[...], cb_ref[...], sb_ref[...]
    c_ref[...] = (ca * cb - sa * sb).astype(BF16)
    s_ref[...] = (sa * cb + ca * sb).astype(BF16)


def dft_matrices(seq):
    blk = min(seq, 256)
    ca, sa, cb, sb = _dft_tables(seq, blk)
    aspec = pl.BlockSpec((None, 1, seq), lambda i: (i, 0, 0))
    bspec = pl.BlockSpec((blk, seq), lambda i: (0, 0))
    ospec = pl.BlockSpec((blk, seq), lambda i: (i, 0))
    return pl.pallas_call(
        _dft_gen_body,
        grid=(seq // blk,),
        in_specs=[aspec, aspec, bspec, bspec],
        out_specs=[ospec, ospec],
        out_shape=[jax.ShapeDtypeStruct((seq, seq), BF16)] * 2,
        compiler_params=_cparams("parallel"),
        name="dft_matrices",
    )(ca, sa, cb, sb)


def _alt_sign(rows):
    return jnp.where((rows & 1) == 0, 1.0, -1.0).astype(F32)


def _filt_body(h2_ref, wf_ref, wb_ref, dl_ref, a_ref, d_ref, ny_ref):
    seq = h2_ref.shape[0]
    h2 = h2_ref[...]
    row = lax.broadcasted_iota(jnp.int32, (seq, 1), 0)
    t = row.astype(F32) * (1.0 / (seq - 1))
    win = jnp.exp(-t * dl_ref[...])
    hf = jnp.dot(h2, wf_ref[...], precision=HIGHEST, preferred_element_type=F32) * win
    hb = jnp.dot(h2, wb_ref[...], precision=HIGHEST, preferred_element_type=F32) * win
    hb = jnp.where(row == 0, 0.0, hb)
    nrm = (jnp.sum(jnp.abs(hf), axis=0, keepdims=True) + jnp.sum(jnp.abs(hb), axis=0, keepdims=True))
    inv = 1.0 / nrm
    a = (hf + hb) * inv
    a_ref[...] = a.astype(BF16)
    d_ref[...] = ((hb - hf) * inv).astype(BF16)
    ny = jnp.sum(a * _alt_sign(row), axis=0, keepdims=True) * (1.0 / (2 * seq))
    ny_ref[...] = jnp.broadcast_to(ny, ny_ref.shape)


def _khat_body(a_ref, d_ref, c_ref, s_ref, kr_ref, ki_ref):
    i = pl.program_id(1)
    tm = c_ref.shape[0]
    seq = c_ref.shape[1]
    f = i * tm + lax.broadcasted_iota(jnp.int32, (tm, 1), 0)
    w = jnp.where(f == 0, 1.0, 2.0).astype(F32) * (1.0 / (2 * seq))
    kr_ref[...] = _dot(c_ref[...], a_ref[...]) * w
    ki_ref[...] = _dot(s_ref[...], d_ref[...]) * w


def hyena_filter_hat(seq, fw1, fb1, fw2, fb2, fw3, freq, cmat, smat):
    d = fw3.shape[1] // 2
    bands_n = (fw1.shape[0] - 1) // 2
    t = jnp.linspace(0.0, 1.0, seq, dtype=F32)[:, None]
    w = (2.0 * math.pi / seq) * jnp.arange(seq, dtype=F32)[:, None]
    bands = jnp.linspace(1e-4, bands_n - 1, bands_n, dtype=F32)[None, :]
    z = jnp.concatenate([t, jnp.cos(bands * w), -jnp.sin(bands * w)], axis=-1)
    h = jnp.sin(freq * (jnp.dot(z, fw1, precision=HIGHEST) + fb1))
    h2 = jnp.sin(freq * (jnp.dot(h, fw2, precision=HIGHEST) + fb2))
    max_decay = math.log(HY_DECAY_TARGET) / HY_FAST_PCT
    min_decay = math.log(HY_DECAY_TARGET) / HY_SLOW_PCT
    deltas = jnp.abs(jnp.linspace(min_decay, max_decay, d, dtype=F32))[None, :]

    order = h2.shape[1]
    tn = min(d, 256)
    nj = d // tn
    a, dd, ny = pl.pallas_call(
        _filt_body,
        grid=(nj,),
        in_specs=[pl.BlockSpec((seq, order), lambda j: (0, 0)),
                  pl.BlockSpec((order, tn), lambda j: (0, j)),
                  pl.BlockSpec((order, tn), lambda j: (0, nj + j)),
                  pl.BlockSpec((1, tn), lambda j: (0, j))],
        out_specs=[pl.BlockSpec((seq, tn), lambda j: (0, j)),
                   pl.BlockSpec((seq, tn), lambda j: (0, j)),
                   pl.BlockSpec((SUBLANES, tn), lambda j: (0, j))],
        out_shape=[jax.ShapeDtypeStruct((seq, d), BF16), jax.ShapeDtypeStruct((seq, d), BF16),
                   jax.ShapeDtypeStruct((SUBLANES, d), F32)],
        compiler_params=_cparams("parallel"),
        name="hyena_filter_taps",
    )(h2, fw3, fw3, deltas)

    tm = min(seq, 512)
    tn2 = min(d, 512)
    kr, ki = pl.pallas_call(
        _khat_body,
        grid=(d // tn2, seq // tm),
        in_specs=[pl.BlockSpec((seq, tn2), lambda j, i: (0, j)),
                  pl.BlockSpec((seq, tn2), lambda j, i: (0, j)),
                  pl.BlockSpec((tm, seq), lambda j, i: (i, 0)),
                  pl.BlockSpec((tm, seq), lambda j, i: (i, 0))],
        out_specs=[pl.BlockSpec((tm, tn2), lambda j, i: (i, j))] * 2,
        out_shape=[jax.ShapeDtypeStruct((seq, d), F32)] * 2,
        compiler_params=_cparams("parallel", "parallel"),
        name="hyena_filter_dft",
    )(a, dd, cmat, smat)
    return kr, ki, ny


def _dft_fwd_body(v_ref, c_ref, s_ref, kr_ref, ki_ref, kn_ref, ya_ref, yb_ref, yn_ref):
    i = pl.program_id(2)
    v = v_ref[...]
    vr = _dot(c_ref[...], v)
    p = _dot(s_ref[...], v)
    kr = kr_ref[...]
    ki = ki_ref[...]
    ya_ref[...] = (vr * kr + p * ki).astype(BF16)
    yb_ref[...] = (p * kr - vr * ki).astype(BF16)

    @pl.when(i == 0)
    def _():
        seq = v.shape[0]
        row = lax.broadcasted_iota(jnp.int32, (seq, 1), 0)
        vl = jnp.sum(v.astype(F32) * _alt_sign(row), axis=0, keepdims=True)
        yn_ref[...] = jnp.broadcast_to(vl * kn_ref[0:1, :], yn_ref.shape)


def _dft_inv_body(ya_ref, yb_ref, c_ref, s_ref, v_ref, x0_ref, skip_ref, yn_ref, o_ref):
    i = pl.program_id(2)
    tm = c_ref.shape[0]
    acc = _dot(c_ref[...], ya_ref[...]) + _dot(s_ref[...], yb_ref[...])
    t = i * tm + lax.broadcasted_iota(jnp.int32, (tm, 1), 0)
    y = acc + _alt_sign(t) * yn_ref[0:1, :] + skip_ref[...] * v_ref[...].astype(F32)
    o_ref[...] = (y * x0_ref[...].astype(F32)).astype(BF16)


def hyena_conv(v, x0, skip, kr, ki, kn, cmat, smat):
    bsz, seq, d = v.shape
    tm = min(seq, 512)
    tn = min(d, 512)
    grid = (bsz, d // tn, seq // tm)
    full = pl.BlockSpec((None, seq, tn), lambda b, j, i: (b, 0, j))
    mat = pl.BlockSpec((tm, seq), lambda b, j, i: (i, 0))
    tile = pl.BlockSpec((None, tm, tn), lambda b, j, i: (b, i, j))
    ktile = pl.BlockSpec((tm, tn), lambda b, j, i: (i, j))
    nyq = pl.BlockSpec((None, SUBLANES, tn), lambda b, j, i: (b, 0, j))
    ya, yb, yn = pl.pallas_call(
        _dft_fwd_body,
        grid=grid,
        in_specs=[full, mat, mat, ktile, ktile, pl.BlockSpec((SUBLANES, tn), lambda b, j, i: (0, j))],
        out_specs=[tile, tile, nyq],
        out_shape=[jax.ShapeDtypeStruct((bsz, seq, d), BF16)] * 2
        + [jax.ShapeDtypeStruct((bsz, SUBLANES, d), F32)],
        compiler_params=_cparams("parallel", "parallel", "arbitrary"),
        name="hyena_dft_fwd",
    )(v, cmat, smat, kr, ki, kn)
    return pl.pallas_call(
        _dft_inv_body,
        grid=grid,
        in_specs=[full, full, mat, mat, tile, tile, pl.BlockSpec((1, tn), lambda b, j, i: (0, j)), nyq],
        out_specs=tile,
        out_shape=jax.ShapeDtypeStruct((bsz, seq, d), BF16),
        compiler_params=_cparams("parallel", "parallel", "parallel"),
        name="hyena_dft_inv",
    )(ya, yb, cmat, smat, v, x0, skip.reshape(1, d), yn)


def _mm_res_body(x_ref, w_ref, b_ref, res_ref, gate_ref, o_ref):
    o_ref[...] = res_ref[...] + gate_ref[...] * (_dot(x_ref[...], w_ref[...]) + b_ref[...])


def mm_residual(x, w, b, res, gate):
    bsz, seq, k = x.shape
    n = w.shape[1]
    tm = min(seq, 512)
    return pl.pallas_call(
        _mm_res_body,
        grid=(bsz, seq // tm),
        in_specs=[pl.BlockSpec((None, tm, k), lambda b, i: (b, i, 0)),
                  pl.BlockSpec((k, n), lambda b, i: (0, 0)),
                  pl.BlockSpec((1, n), lambda b, i: (0, 0)),
                  pl.BlockSpec((None, tm, n), lambda b, i: (b, i, 0)),
                  pl.BlockSpec((None, 1, n), lambda b, i: (b, 0, 0))],
        out_specs=pl.BlockSpec((None, tm, n), lambda b, i: (b, i, 0)),
        out_shape=jax.ShapeDtypeStruct((bsz, seq, n), F32),
        compiler_params=_cparams("parallel", "parallel"),
        name="mm_residual",
    )(x, w, b.reshape(1, n), res, gate)


def _moe_pre_body(x_ref, g_ref, sh_ref, sc_ref, wr_ref, br_ref, tok_ref, eid_ref, gate_ref, *, n_groups, n_experts):
    tok = _norm_mod(x_ref[...], g_ref[...], sh_ref[...], sc_ref[...])
    tok_ref[...] = tok
    logits = jnp.dot(tok, wr_ref[...], precision=HIGHEST, preferred_element_type=F32) + br_ref[...]
    lane = lax.broadcasted_iota(jnp.int32, logits.shape, 1)
    per = n_experts // n_groups
    big = jnp.int32(1 << 20)
    gmask = jnp.logical_and(lane >= n_experts, lane < n_experts + n_groups)
    gl = jnp.where(gmask, logits, NEG_BIG)
    gmax = jnp.max(gl, axis=-1, keepdims=True)
    gidx = jnp.min(jnp.where(gl == gmax, lane - n_experts, big), axis=-1, keepdims=True)
    p_top = 1.0 / jnp.sum(jnp.where(gmask, jnp.exp(gl - gmax), 0.0), axis=-1, keepdims=True)
    lo = gidx * per
    emask = jnp.logical_and(lane >= lo, lane < lo + per)
    el = jnp.where(emask, logits, NEG_BIG)
    m1 = jnp.max(el, axis=-1, keepdims=True)
    i1 = jnp.min(jnp.where(el == m1, lane, big), axis=-1, keepdims=True)
    el2 = jnp.where(lane == i1, NEG_BIG, el)
    m2 = jnp.max(el2, axis=-1, keepdims=True)
    i2 = jnp.min(jnp.where(el2 == m2, lane, big), axis=-1, keepdims=True)
    e21 = jnp.exp(m2 - m1)
    g1 = p_top / (1.0 + e21)
    g2 = g1 * e21
    eid_ref[...] = jnp.where(lane == 0, i1, jnp.where(lane == 1, i2, 0))
    gate_ref[...] = jnp.where(lane == 0, g1, jnp.where(lane == 1, g2, 0.0))


def moe_pre(x, g, shift, scale, wr, br, n_groups, n_experts):
    bsz, seq, d = x.shape
    tm = min(seq, 256)
    mspec = pl.BlockSpec((None, 1, d), lambda b, i: (b, 0, 0))
    rout = pl.BlockSpec((None, tm, LANES), lambda b, i: (b, i, 0))
    return pl.pallas_call(
        functools.partial(_moe_pre_body, n_groups=n_groups, n_experts=n_experts),
        grid=(bsz, seq // tm),
        in_specs=[pl.BlockSpec((None, tm, d), lambda b, i: (b, i, 0)),
                  pl.BlockSpec((1, d), lambda b, i: (0, 0)), mspec, mspec,
                  pl.BlockSpec((d, LANES), lambda b, i: (0, 0)),
                  pl.BlockSpec((1, LANES), lambda b, i: (0, 0))],
        out_specs=[pl.BlockSpec((None, tm, d), lambda b, i: (b, i, 0)), rout, rout],
        out_shape=[jax.ShapeDtypeStruct((bsz, seq, d), F32),
                   jax.ShapeDtypeStruct((bsz, seq, LANES), jnp.int32),
                   jax.ShapeDtypeStruct((bsz, seq, LANES), F32)],
        compiler_params=_cparams("parallel", "parallel"),
        name="moe_pre",
    )(x, g.reshape(1, d), shift, scale, wr, br)


def _start_row_gather(idx_ref, n_rows, stride, offset, src_hbm, dst_vmem, sem):
    def body(r, c):
        pltpu.make_async_copy(src_hbm.at[pl.ds(idx_ref[stride * r + offset], 1)],
                              dst_vmem.at[pl.ds(r, 1)], sem).start()
        return c

    lax.fori_loop(0, n_rows, body, 0, unroll=8)


def _wait_row_gather(n_rows, src_hbm, dst_vmem, sem):
    pltpu.make_async_copy(src_hbm.at[pl.ds(0, n_rows)], dst_vmem, sem).wait()


def _expert_body(be_ref, bv_ref, src_ref, nxt_ref, tok_ref, wg_ref, wu_ref, wd_ref, o_ref, xbuf, sem):
    del be_ref
    i = pl.program_id(0)
    n = pl.num_programs(0)
    slot = i % 2

    @pl.when(jnp.logical_and(i == 0, bv_ref[0] > 0))
    def _():
        _start_row_gather(src_ref, MOE_BM, 1, 0, tok_ref, xbuf.at[0], sem.at[0])

    @pl.when(jnp.logical_and(i + 1 < n, bv_ref[jnp.minimum(i + 1, n - 1)] > 0))
    def _():
        _start_row_gather(nxt_ref, MOE_BM, 1, 0, tok_ref, xbuf.at[1 - slot], sem.at[1 - slot])

    @pl.when(bv_ref[i] > 0)
    def _():
        _wait_row_gather(MOE_BM, tok_ref, xbuf.at[slot], sem.at[slot])
        x = xbuf[slot].astype(BF16)
        gate = _dot(x, wg_ref[...])
        up = _dot(x, wu_ref[...])
        h = (gate * jax.nn.sigmoid(gate) * up).astype(BF16)
        o_ref[...] = _dot(h, wd_ref[...])

    @pl.when(bv_ref[i] == 0)
    def _():
        o_ref[...] = jnp.zeros_like(o_ref)


def moe_experts(tok, src_tok, block_expert, block_valid, w_gate, w_up, w_down):
    d = tok.shape[1]
    n_rows = src_tok.shape[0]
    n_blocks = n_rows // MOE_BM
    dh = w_gate.shape[2]
    grid_spec = pltpu.PrefetchScalarGridSpec(
        num_scalar_prefetch=2,
        grid=(n_blocks,),
        in_specs=[pl.BlockSpec((MOE_BM,), lambda i, be, bv: (i,), memory_space=pltpu.SMEM),
                  pl.BlockSpec((MOE_BM,), lambda i, be, bv: (jnp.minimum(i + 1, n_blocks - 1),),
                               memory_space=pltpu.SMEM),
                  pl.BlockSpec(memory_space=pl.ANY),
                  pl.BlockSpec((None, d, dh), lambda i, be, bv: (be[i], 0, 0)),
                  pl.BlockSpec((None, d, dh), lambda i, be, bv: (be[i], 0, 0)),
                  pl.BlockSpec((None, dh, d), lambda i, be, bv: (be[i], 0, 0))],
        out_specs=pl.BlockSpec((MOE_BM, d), lambda i, be, bv: (i, 0)),
        scratch_shapes=[pltpu.VMEM((2, MOE_BM, d), F32), pltpu.SemaphoreType.DMA((2,))],
    )
    return pl.pallas_call(
        _expert_body,
        grid_spec=grid_spec,
        out_shape=jax.ShapeDtypeStruct((n_rows, d), F32),
        compiler_params=_cparams("arbitrary"),
        name="moe_experts",
    )(block_expert, block_valid, src_tok, src_tok, tok, w_gate, w_up, w_down)


def _combine_body(dest_ref, nxt_ref, os_ref, gate_ref, res_ref, gt_ref, fg_ref, o_ref, buf, sem, *, final_norm):
    rows = res_ref.shape[0]
    i = pl.program_id(0)
    n = pl.num_programs(0)
    slot = i % 2

    def start(idx_ref, s):
        for k in range(TOP_K):
            _start_row_gather(idx_ref, rows, TOP_K, k, os_ref, buf.at[s, k], sem.at[s])

    @pl.when(i == 0)
    def _():
        start(dest_ref, 0)

    @pl.when(i + 1 < n)
    def _():
        start(nxt_ref, 1 - slot)

    for k in range(TOP_K):
        _wait_row_gather(rows, os_ref, buf.at[slot, k], sem.at[slot])
    gates = gate_ref[...]
    mo = gates[:, 0:1] * buf[slot, 0] + gates[:, 1:2] * buf[slot, 1]
    y = res_ref[...] + gt_ref[...] * mo
    if final_norm:
        ms = jnp.mean(y * y, axis=-1, keepdims=True)
        y = y * lax.rsqrt(ms + NORM_EPS) * fg_ref[...]
    o_ref[...] = y


def moe_combine(os, dest, gates, res, gt, final_g, final_norm):
    bsz, seq, d = res.shape
    rows = min(seq, 256)
    nt = seq // rows
    n = bsz * nt
    tile = pl.BlockSpec((None, rows, d), lambda i: (i // nt, i % nt, 0))
    return pl.pallas_call(
        functools.partial(_combine_body, final_norm=final_norm),
        grid=(n,),
        in_specs=[pl.BlockSpec((rows * TOP_K,), lambda i: (i,), memory_space=pltpu.SMEM),
                  pl.BlockSpec((rows * TOP_K,), lambda i: (jnp.minimum(i + 1, n - 1),), memory_space=pltpu.SMEM),
                  pl.BlockSpec(memory_space=pl.ANY),
                  pl.BlockSpec((None, rows, LANES), lambda i: (i // nt, i % nt, 0)),
                  tile,
                  pl.BlockSpec((None, 1, d), lambda i: (i // nt, 0, 0)),
                  pl.BlockSpec((1, d), lambda i: (0, 0))],
        out_specs=tile,
        out_shape=jax.ShapeDtypeStruct((bsz, seq, d), F32),
        scratch_shapes=[pltpu.VMEM((2, TOP_K, rows, d), F32), pltpu.SemaphoreType.DMA((2,))],
        compiler_params=_cparams("arbitrary"),
        name="moe_combine",
    )(dest, dest, os, gates, res, gt, final_g.reshape(1, d))


def _route_plan(eid, n_experts):
    e_flat = eid.reshape(-1)
    a = e_flat.shape[0]
    order = jnp.argsort(e_flat)
    pos = jnp.argsort(order)
    experts = jnp.arange(n_experts, dtype=jnp.int32)
    counts = jnp.sum((e_flat[:, None] == experts[None, :]).astype(jnp.int32), axis=0)
    start = jnp.cumsum(counts) - counts
    padded = (counts + MOE_BM - 1) // MOE_BM * MOE_BM
    pad_end = jnp.cumsum(padded)
    pad_start = pad_end - padded
    dest = pad_start[e_flat] + pos - start[e_flat]
    n_blocks = -(-a // MOE_BM) + n_experts
    starts = jnp.arange(n_blocks, dtype=jnp.int32) * MOE_BM
    block_expert = jnp.minimum(jnp.sum((pad_end[None, :] <= starts[:, None]).astype(jnp.int32), axis=1),
                               n_experts - 1)
    block_valid = (starts < pad_end[-1]).astype(jnp.int32)
    rows = jnp.arange(n_blocks * MOE_BM, dtype=jnp.int32)
    e_row = jnp.repeat(block_expert, MOE_BM)
    off = rows - pad_start[e_row]
    live = jnp.logical_and(off < counts[e_row], jnp.repeat(block_valid, MOE_BM) > 0)
    src_tok = jnp.where(live, order[jnp.clip(start[e_row] + off, 0, a - 1)] // TOP_K, 0)
    return dest.astype(jnp.int32), src_tok.astype(jnp.int32), block_expert.astype(jnp.int32), block_valid


def hier_moe(streams, norm_g, wg, bg, we, be, w_gate, w_up, w_down, final_g, final_norm):
    n_groups = wg.shape[1]
    n_experts = we.shape[1]
    d = wg.shape[0]
    wr = jnp.zeros((d, LANES), F32).at[:, :n_experts].set(we).at[:, n_experts:n_experts + n_groups].set(wg)
    br = jnp.zeros((1, LANES), F32).at[0, :n_experts].set(be).at[0, n_experts:n_experts + n_groups].set(bg)
    toks, eids, gates = [], [], []
    for x, sh, sc, _ in streams:
        tok, eid, gate = moe_pre(x, norm_g, sh, sc, wr, br, n_groups, n_experts)
        toks.append(tok.reshape(-1, d))
        eids.append(eid[..., :TOP_K].reshape(-1, TOP_K))
        gates.append(gate)
    tok_all = jnp.concatenate(toks, axis=0) if len(toks) > 1 else toks[0]
    eid_all = jnp.concatenate(eids, axis=0) if len(eids) > 1 else eids[0]
    dest, src_tok, block_expert, block_valid = _route_plan(eid_all, n_experts)
    os = moe_experts(tok_all, src_tok, block_expert, block_valid,
                     w_gate.astype(BF16), w_up.astype(BF16), w_down.astype(BF16))
    outs = []
    off = 0
    for (x, _, _, gt), gate in zip(streams, gates):
        n_assign = x.shape[0] * x.shape[1] * TOP_K
        outs.append(moe_combine(os, dest[off:off + n_assign], gate, x, gt, final_g, final_norm))
        off += n_assign
    return outs


def _norm_mod_body(x_ref, g_ref, sh_ref, sc_ref, o_ref):
    o_ref[...] = _norm_mod(x_ref[...], g_ref[...], sh_ref[...], sc_ref[...]).astype(o_ref.dtype)


def norm_mod(x, g, shift, scale):
    bsz, seq, d = x.shape
    tm = min(seq, 512)
    mspec = pl.BlockSpec((None, 1, d), lambda b, i: (b, 0, 0))
    return pl.pallas_call(
        _norm_mod_body,
        grid=(bsz, seq // tm),
        in_specs=[pl.BlockSpec((None, tm, d), lambda b, i: (b, i, 0)),
                  pl.BlockSpec((1, d), lambda b, i: (0, 0)), mspec, mspec],
        out_specs=pl.BlockSpec((None, tm, d), lambda b, i: (b, i, 0)),
        out_shape=jax.ShapeDtypeStruct((bsz, seq, d), BF16),
        compiler_params=_cparams("parallel", "parallel"),
        name="norm_mod",
    )(x, g.reshape(1, d), shift, scale)


def _s5_arrange(h):
    bsz, t, d = h.shape
    c = t // (S5_SEGS * S5_TAU)
    h = h.reshape(bsz, S5_SEGS, c, S5_TAU, d // LANES, LANES)
    return h.transpose(2, 0, 1, 4, 3, 5).reshape(c * bsz * S5_SEGS, d * S5_TAU)


def _s5_unarrange(y, bsz):
    r, w = y.shape
    d = w // S5_TAU
    c = r // (bsz * S5_SEGS)
    y = y.reshape(c, bsz, S5_SEGS, d // LANES, S5_TAU, LANES)
    return y.transpose(1, 2, 0, 4, 3, 5).reshape(bsz, S5_SEGS * c * S5_TAU, d)


def _s5_operators(a_re, a_im, log_step, b_re, b_im, c_re, c_im):
    n_g, n_p = a_re.shape[1:]
    n_h = b_re.shape[-1]
    gpt = LANES // n_h
    n_j = n_g // gpt
    tau = S5_TAU
    lam_step = lax.complex(a_re, a_im) * jnp.exp(log_step)[..., None]
    lam_bar = jnp.exp(lam_step)
    b_bar = ((lam_bar - 1.0) / lax.complex(a_re, a_im))[..., None] * lax.complex(b_re, b_im)
    c_mat = lax.complex(c_re, c_im)
    ks = jnp.arange(tau + 1, dtype=F32)[None, :, None, None]
    pw = jnp.exp(lam_step[:, None] * ks)
    eye = jnp.eye(gpt, dtype=F32)
    ein = functools.partial(jnp.einsum, precision=HIGHEST)
    ws, wh, lt = [], [], []
    wu = 0.0
    for d in range(2):
        pos = jnp.arange(tau) if d == 0 else jnp.arange(tau)[::-1]
        inj = pw[d][tau - 1 - pos][..., None] * b_bar[d][None]
        inj = inj.reshape(tau, n_j, gpt, n_p, n_h)
        w_re = ein('tjgph,gk->jtghkp', inj.real, eye).reshape(n_j, tau * LANES, gpt * n_p)
        w_im = ein('tjgph,gk->jtghkp', inj.imag, eye).reshape(n_j, tau * LANES, gpt * n_p)
        ws += [w_re, w_im]
        cl = c_mat[d][None] * pw[d][pos + 1][:, :, None, :]
        cl = cl.reshape(tau, n_j, gpt, n_h, n_p)
        h_re = ein('tjgop,gk->jgptko', cl.real, eye).reshape(n_j, gpt * n_p, tau * LANES)
        h_im = ein('tjgop,gk->jgptko', -cl.imag, eye).reshape(n_j, gpt * n_p, tau * LANES)
        wh.append(jnp.concatenate([h_re, h_im], axis=1))
        mk = ein('gop,kgp,gph->kgoh', c_mat[d], pw[d][:tau], b_bar[d]).real
        diff = pos[:, None] - pos[None, :]
        tz = jnp.where((diff >= 0)[:, :, None, None, None], mk[jnp.clip(diff, 0, tau - 1)], 0.0)
        tz = tz.reshape(tau, tau, n_j, gpt, n_h, n_h)
        wu = wu + ein('stjgoh,gk->jtghsko', tz, eye).reshape(n_j, tau * LANES, tau * LANES)
        lt_d = pw[d][tau].reshape(n_j, 1, gpt * n_p)
        lt.append(jnp.concatenate([lt_d.real, lt_d.imag], axis=-1))
    return (jnp.concatenate(ws, axis=-1).astype(BF16), wu.astype(BF16), jnp.stack(wh).astype(BF16),
            jnp.stack(lt).astype(F32))


def _s5_inj_body(x_ref, w_ref, o_ref):
    o_ref[...] = _dot(x_ref[...], w_ref[...])


def s5_inject(xr, ws):
    r = xr.shape[0]
    n_j, k, n = ws.shape
    tm = r // 2 if r % 32 == 0 else r
    return pl.pallas_call(
        _s5_inj_body,
        grid=(n_j, r // tm),
        in_specs=[pl.BlockSpec((tm, k), lambda j, i: (i, j)),
                  pl.BlockSpec((None, k, n), lambda j, i: (j, 0, 0))],
        out_specs=pl.BlockSpec((tm, n), lambda j, i: (i, j)),
        out_shape=jax.ShapeDtypeStruct((r, n_j * n), F32),
        compiler_params=_cparams("parallel", "parallel"),
        name="s5_inject",
    )(xr, ws)


def _cmul(ar, ai, br, bi):
    return ar * br - ai * bi, ar * bi + ai * br


def _s5_scan_body(s_ref, lt_ref, h_ref, raw_ref, *, n_ctx, n_lat, bsz):
    d = pl.program_id(1)
    w2 = lt_ref.shape[-1]
    w = w2 // 2
    rows = bsz * S5_SEGS
    seg = lax.broadcasted_iota(jnp.int32, (rows, 1), 0) % S5_SEGS
    is_late = seg != d
    lam_r = lt_ref[:, 0:w]
    lam_i = lt_ref[:, w:w2]
    zero = jnp.zeros((rows, w), F32)
    one = (jnp.ones((1, w), F32), jnp.zeros((1, w), F32))

    def swap_segments(x):
        return jnp.where(seg == 0, pltpu.roll(x, rows - 1, axis=0), pltpu.roll(x, 1, axis=0))

    def phase(c0, n_steps, hin_r, hin_i, write):
        def chunk(k):
            return c0 + jnp.where(d == 0, k, n_steps - 1 - k)

        def step_raw(k, carry):
            hr, hi = carry
            c = chunk(k)
            raw_ref[c - c0, :, 0:w] = hr
            raw_ref[c - c0, :, w:w2] = hi
            nr, ni = _cmul(lam_r, lam_i, hr, hi)
            return nr + s_ref[c, :, 0:w], ni + s_ref[c, :, w:w2]

        er, ei = lax.fori_loop(0, n_steps, step_raw, (zero, zero))
        pr, pi = lax.fori_loop(0, n_steps, lambda k, q: _cmul(lam_r, lam_i, *q), one)
        dr, di = _cmul(pr, pi, hin_r, hin_i)
        first_r = jnp.where(is_late, 0.0, er + dr)
        first_i = jnp.where(is_late, 0.0, ei + di)
        carry_r = jnp.where(is_late, swap_segments(first_r), hin_r)
        carry_i = jnp.where(is_late, swap_segments(first_i), hin_i)
        if write:
            def step_fix(k, q):
                c = chunk(k)
                fr, fi = _cmul(q[0], q[1], carry_r, carry_i)
                h_ref[c - c0, :, 0:w] = (raw_ref[c - c0, :, 0:w] + fr).astype(h_ref.dtype)
                h_ref[c - c0, :, w:w2] = (raw_ref[c - c0, :, w:w2] + fi).astype(h_ref.dtype)
                return _cmul(lam_r, lam_i, q[0], q[1])

            lax.fori_loop(0, n_steps, step_fix, one)
        lr, li = _cmul(pr, pi, carry_r, carry_i)
        last_r = jnp.where(is_late, er + lr, 0.0)
        last_i = jnp.where(is_late, ei + li, 0.0)
        return (jnp.where(is_late, 0.0, swap_segments(last_r)), jnp.where(is_late, 0.0, swap_segments(last_i)))

    hr, hi = zero, zero
    if n_ctx:
        hr, hi = phase(0, n_ctx, hr, hi, False)
    phase(n_ctx, n_lat, hr, hi, True)


def s5_scan(s, lt, n_ctx, n_lat, bsz):
    assert S5_SEGS == 2
    n_c, rows, _ = s.shape
    n_j = lt.shape[1]
    w2 = lt.shape[-1]
    return pl.pallas_call(
        functools.partial(_s5_scan_body, n_ctx=n_ctx, n_lat=n_lat, bsz=bsz),
        grid=(n_j, 2),
        in_specs=[pl.BlockSpec((n_c, rows, w2), lambda j, d: (0, 0, 2 * j + d)),
                  pl.BlockSpec((None, None, 1, w2), lambda j, d: (d, j, 0, 0))],
        out_specs=pl.BlockSpec((None, n_lat, rows, w2), lambda j, d: (d, 0, 0, j)),
        out_shape=jax.ShapeDtypeStruct((2, n_lat, rows, n_j * w2), BF16),
        scratch_shapes=[pltpu.VMEM((max(n_ctx, n_lat), rows, w2), F32)],
        compiler_params=_cparams("parallel", "parallel"),
        name="s5_scan",
    )(s, lt)


def _s5_out_body(x_ref, hf_ref, hb_ref, wu_ref, whf_ref, whb_ref, o_ref):
    o_ref[...] = (_dot(x_ref[...], wu_ref[...]) + _dot(hf_ref[...], whf_ref[...])
                  + _dot(hb_ref[...], whb_ref[...]))


def s5_readout(xr, h, wu, wh):
    r = xr.shape[0]
    n_j, k, n = wu.shape
    w2 = wh.shape[2]
    tm = min(r, 1024)
    return pl.pallas_call(
        _s5_out_body,
        grid=(n_j, r // tm),
        in_specs=[pl.BlockSpec((tm, k), lambda j, i: (i, j)),
                  pl.BlockSpec((None, tm, w2), lambda j, i: (0, i, j)),
                  pl.BlockSpec((None, tm, w2), lambda j, i: (1, i, j)),
                  pl.BlockSpec((None, k, n), lambda j, i: (j, 0, 0)),
                  pl.BlockSpec((None, None, w2, n), lambda j, i: (0, j, 0, 0)),
                  pl.BlockSpec((None, None, w2, n), lambda j, i: (1, j, 0, 0))],
        out_specs=pl.BlockSpec((tm, n), lambda j, i: (i, j)),
        out_shape=jax.ShapeDtypeStruct((r, n_j * n), F32),
        compiler_params=_cparams("parallel", "parallel"),
        name="s5_readout",
    )(xr, h, h, wu, wh, wh)


def _gelu_tanh(x):
    return 0.5 * x * (1.0 + jnp.tanh(math.sqrt(2.0 / math.pi) * (x + 0.044715 * (x * x * x))))


def _s5_glu_body(x_ref, y_ref, g_ref, sh_ref, sc_ref, dk_ref, w1_ref, w2_ref, b1_ref, b2_ref, gt_ref, o_ref):
    x = x_ref[...]
    u = _norm_mod(x, g_ref[...], sh_ref[...], sc_ref[...])
    y = _gelu_tanh(y_ref[...] + dk_ref[...] * u).astype(BF16)
    o = (_dot(y, w1_ref[...]) + b1_ref[...]) * jax.nn.sigmoid(_dot(y, w2_ref[...]) + b2_ref[...])
    o_ref[...] = x + gt_ref[...] * o


def s5_glu(x, y, g, shift, scale, d_skip, w1, b1, w2, b2, gate):
    bsz, seq, d = x.shape
    tm = min(seq, 256)
    row = lambda a: a.reshape(1, d)
    rspec = pl.BlockSpec((1, d), lambda b, i: (0, 0))
    mspec = pl.BlockSpec((None, 1, d), lambda b, i: (b, 0, 0))
    tile = pl.BlockSpec((None, tm, d), lambda b, i: (b, i, 0))
    wspec = pl.BlockSpec((d, d), lambda b, i: (0, 0))
    return pl.pallas_call(
        _s5_glu_body,
        grid=(bsz, seq // tm),
        in_specs=[tile, tile, rspec, mspec, mspec, rspec, wspec, wspec, rspec, rspec, mspec],
        out_specs=tile,
        out_shape=jax.ShapeDtypeStruct((bsz, seq, d), F32),
        compiler_params=_cparams("parallel", "parallel"),
        name="s5_glu",
    )(x, y, row(g), shift, scale, row(d_skip), w1, w2, row(b1), row(b2), gate)


def s5_mix(xl, xc, g, sh_l, sc_l, sh_c, sc_c, gate_l, a_re, a_im, log_step, b_re, b_im, c_re, c_im, d_skip,
           w1, b1, w2, b2):
    bsz, seq, d = xl.shape
    hl = norm_mod(xl, g, sh_l, sc_l)
    hc = norm_mod(xc, g, sh_c, sc_c)
    xr_c = _s5_arrange(hc)
    xr_l = _s5_arrange(hl)
    rows = bsz * S5_SEGS
    n_ctx = xr_c.shape[0] // rows
    n_lat = xr_l.shape[0] // rows
    ws, wu, wh, lt = _s5_operators(a_re, a_im, log_step, b_re, b_im, c_re, c_im)
    s = s5_inject(jnp.concatenate([xr_c, xr_l], axis=0), ws)
    h = s5_scan(s.reshape(n_ctx + n_lat, rows, -1), lt, n_ctx, n_lat, bsz)
    y = s5_readout(xr_l, h.reshape(2, n_lat * rows, -1), wu, wh)
    y = _s5_unarrange(y, bsz)
    return s5_glu(xl, y, g, sh_l, sc_l, d_skip, w1.astype(BF16), b1, w2.astype(BF16), b2, gate_l)


def hyena_mix(x, g, shift, scale, gate, w_in, b_in, conv_w, conv_b, fw1, fb1, fw2, fb2, fw3, freq, skip,
              w_out, b_out):
    seq = x.shape[1]
    cmat, smat = dft_matrices(seq)
    kr, ki, kn = hyena_filter_hat(seq, fw1, fb1, fw2, fb2, fw3, freq, cmat, smat)
    v, x0 = hyena_in(x, g, shift, scale, w_in, b_in, conv_w, conv_b)
    yg = hyena_conv(v, x0, skip, kr, ki, kn, cmat, smat)
    return mm_residual(yg, w_out, b_out, x, gate)


def kernel(x, c, ctx, c_ctx, ada_w, ada_b, norm_g, final_g, hy_w_in, hy_b_in, hy_conv_w, hy_conv_b, hy_fw1,
           hy_fb1, hy_fw2, hy_fb2, hy_fw3, hy_freq, hy_skip, hy_w_out, hy_b_out, s5_a_re, s5_a_im,
           s5_log_step, s5_b_re, s5_b_im, s5_c_re, s5_c_im, s5_d, s5_w1, s5_b1, s5_w2, s5_b2, moe_wg, moe_bg,
           moe_we, moe_be, moe_w_gate, moe_w_up, moe_w_down):
    bsz, _, d = x.shape
    depth = ada_w.shape[0]
    assert depth == 2 and bsz < SUBLANES
    c_all = jnp.zeros((SUBLANES, d), F32).at[:bsz].set(c).at[bsz].set(c_ctx)
    mods = ada_mod(c_all, ada_w, ada_b)

    def mod_rows(layer, k):
        lat = mods[layer, :bsz, k * d:(k + 1) * d][:, None, :]
        cx = jnp.broadcast_to(mods[layer, bsz, k * d:(k + 1) * d][None, None, :], (bsz, 1, d))
        return lat, cx

    (sh_a, csh_a), (sc_a, csc_a), (gt_a, cgt_a) = mod_rows(0, 0), mod_rows(0, 1), mod_rows(0, 2)
    (sh_f, csh_f), (sc_f, csc_f), (gt_f, cgt_f) = mod_rows(0, 3), mod_rows(0, 4), mod_rows(0, 5)
    hy = (hy_w_in[0].astype(BF16), hy_b_in[0], hy_conv_w[0], hy_conv_b[0], hy_fw1[0], hy_fb1[0], hy_fw2[0],
          hy_fb2[0], hy_fw3[0], hy_freq[0], hy_skip[0], hy_w_out[0].astype(BF16), hy_b_out[0])
    xl = hyena_mix(x, norm_g[0, 0], sh_a, sc_a, gt_a, *hy)
    xc = hyena_mix(ctx, norm_g[0, 0], csh_a, csc_a, cgt_a, *hy)
    xl, xc = hier_moe([(xl, sh_f, sc_f, gt_f), (xc, csh_f, csc_f, cgt_f)], norm_g[0, 1],
                      moe_wg[0], moe_bg[0], moe_we[0], moe_be[0], moe_w_gate[0], moe_w_up[0], moe_w_down[0],
                      final_g, False)

    (sh_a, csh_a), (sc_a, csc_a), (gt_a, _) = mod_rows(1, 0), mod_rows(1, 1), mod_rows(1, 2)
    (sh_f, _), (sc_f, _), (gt_f, _) = mod_rows(1, 3), mod_rows(1, 4), mod_rows(1, 5)
    xl = s5_mix(xl, xc, norm_g[1, 0], sh_a, sc_a, csh_a, csc_a, gt_a, s5_a_re[0], s5_a_im[0], s5_log_step[0],
                s5_b_re[0], s5_b_im[0], s5_c_re[0], s5_c_im[0], s5_d[0], s5_w1[0], s5_b1[0], s5_w2[0], s5_b2[0])
    (out,) = hier_moe([(xl, sh_f, sc_f, gt_f)], norm_g[1, 1], moe_wg[1], moe_bg[1], moe_we[1], moe_be[1],
                      moe_w_gate[1], moe_w_up[1], moe_w_down[1], final_g, True)
    return out
```

```python
import functools
import math

import jax
import jax.numpy as jnp
from jax import lax
from jax.experimental import pallas as pl
from jax.experimental.pallas import tpu as pltpu

F32 = jnp.float32
BF16 = jnp.bfloat16
HIGHEST = lax.Precision.HIGHEST

NORM_EPS = 1e-6
HY_DECAY_TARGET = 1e-2
HY_FAST_PCT = 0.3
HY_SLOW_PCT = 1.5
TOP_K = 2

V7X_VMEM_LIMIT_BYTES = 56 * 1024 * 1024
LANES = 128
SUBLANES = 8
S5_TAU = 8
S5_SEGS = 2
MOE_TM = 256
MOE_BM = 256
MOE_NCH = 12
MOE_STAGE = 4
NEG_BIG = -1e30


def _cparams(*sem):
    return pltpu.CompilerParams(dimension_semantics=sem, vmem_limit_bytes=V7X_VMEM_LIMIT_BYTES)


def _norm_mod(x, g, shift, scale):
    ms = jnp.mean(x * x, axis=-1, keepdims=True)
    return (x * lax.rsqrt(ms + NORM_EPS) * g) * (1.0 + scale) + shift


def _dot(a, b):
    return jnp.dot(a, b, preferred_element_type=F32)


def _ada_body(c_ref, w_ref, b_ref, o_ref):
    x = c_ref[...]
    s = (x * jax.nn.sigmoid(x)).astype(BF16)
    o_ref[...] = _dot(s, w_ref[...].astype(BF16)) + b_ref[...]


def ada_mod(c_all, ada_w, ada_b):
    depth, d, n = ada_w.shape
    tn = min(n, 1024)
    return pl.pallas_call(
        _ada_body,
        grid=(depth, n // tn),
        in_specs=[pl.BlockSpec((SUBLANES, d), lambda l, j: (0, 0)),
                  pl.BlockSpec((None, d, tn), lambda l, j: (l, 0, j)),
                  pl.BlockSpec((None, 1, tn), lambda l, j: (l, 0, j))],
        out_specs=pl.BlockSpec((None, SUBLANES, tn), lambda l, j: (l, 0, j)),
        out_shape=jax.ShapeDtypeStruct((depth, SUBLANES, n), F32),
        compiler_params=_cparams("parallel", "parallel"),
        name="ada_mod",
    )(c_all, ada_w, ada_b.reshape(depth, 1, n))


def _hy_in_body(xp_ref, xm_ref, xn_ref, g_ref, sh_ref, sc_ref,
                w0_ref, w1_ref, w2_ref, b0_ref, b1_ref, b2_ref,
                cw0_ref, cw1_ref, cw2_ref, cb0_ref, cb1_ref, cb2_ref,
                v_ref, x0_ref):
    i = pl.program_id(2)
    ni = pl.num_programs(2)
    tm = xm_ref.shape[0]
    x = jnp.concatenate([xp_ref[...], xm_ref[...], xn_ref[...]], axis=0)
    h = _norm_mod(x, g_ref[...], sh_ref[...], sc_ref[...]).astype(BF16)
    rows = lax.broadcasted_iota(jnp.int32, (tm + 2 * SUBLANES, 1), 0)
    valid = jnp.logical_and(jnp.logical_or(rows >= SUBLANES, i > 0),
                            jnp.logical_or(rows < tm + SUBLANES, i < ni - 1))

    def part(w_ref, b_ref, cw_ref, cb_ref):
        z = jnp.where(valid, _dot(h, w_ref[...]) + b_ref[...], 0.0)
        cw = cw_ref[...]
        zp = pltpu.roll(z, 1, axis=0)[SUBLANES:tm + SUBLANES]
        zn = pltpu.roll(z, tm + 2 * SUBLANES - 1, axis=0)[SUBLANES:tm + SUBLANES]
        return zp * cw[0:1] + z[SUBLANES:tm + SUBLANES] * cw[1:2] + zn * cw[2:3] + cb_ref[...]

    x0 = part(w0_ref, b0_ref, cw0_ref, cb0_ref)
    x1 = part(w1_ref, b1_ref, cw1_ref, cb1_ref)
    v = part(w2_ref, b2_ref, cw2_ref, cb2_ref) * x1
    v_ref[...] = v.astype(BF16)
    x0_ref[...] = x0.astype(BF16)


def hyena_in(x, g, shift, scale, w_in, b_in, conv_w, conv_b):
    bsz, seq, d = x.shape
    tm = min(seq, 512)
    tn = min(d, 1024)
    nj = d // tn
    r8 = tm // SUBLANES
    last8 = seq // SUBLANES - 1
    row = lambda a: a.reshape(1, -1)
    wspec = lambda k: pl.BlockSpec((d, tn), lambda j, b, i: (0, k * nj + j))
    rspec = lambda k: pl.BlockSpec((1, tn), lambda j, b, i: (0, k * nj + j))
    cspec = lambda k: pl.BlockSpec((3, tn), lambda j, b, i: (0, k * nj + j))
    mspec = pl.BlockSpec((None, 1, d), lambda j, b, i: (b, 0, 0))
    out_spec = pl.BlockSpec((None, tm, tn), lambda j, b, i: (b, i, j))
    return pl.pallas_call(
        _hy_in_body,
        grid=(nj, bsz, seq // tm),
        in_specs=[pl.BlockSpec((None, SUBLANES, d), lambda j, b, i: (b, jnp.maximum(i * r8 - 1, 0), 0)),
                  pl.BlockSpec((None, tm, d), lambda j, b, i: (b, i, 0)),
                  pl.BlockSpec((None, SUBLANES, d), lambda j, b, i: (b, jnp.minimum((i + 1) * r8, last8), 0)),
                  pl.BlockSpec((1, d), lambda j, b, i: (0, 0)), mspec, mspec,
                  wspec(0), wspec(1), wspec(2), rspec(0), rspec(1), rspec(2),
                  cspec(0), cspec(1), cspec(2), rspec(0), rspec(1), rspec(2)],
        out_specs=[out_spec, out_spec],
        out_shape=[jax.ShapeDtypeStruct((bsz, seq, d), BF16)] * 2,
        compiler_params=_cparams("parallel", "parallel", "parallel"),
        name="hyena_in",
    )(x, x, x, row(g), shift, scale, w_in, w_in, w_in, row(b_in), row(b_in), row(b_in),
      conv_w, conv_w, conv_w, row(conv_b), row(conv_b), row(conv_b))


def _dft_tables(seq, blk):
    n = 2 * seq
    s = jnp.arange(seq, dtype=jnp.int32)[None, :]
    fl = jnp.arange(blk, dtype=jnp.int32)[:, None]
    fh = (jnp.arange(seq // blk, dtype=jnp.int32) * blk)[:, None]
    w = 2.0 * math.pi / n
    ang_b = ((fl * s) % n).astype(F32) * w
    ang_a = ((fh * s) % n).astype(F32) * w
    return (jnp.cos(ang_a)[:, None, :], jnp.sin(ang_a)[:, None, :], jnp.cos(ang_b), jnp.sin(ang_b))


def _dft_gen_body(ca_ref, sa_ref, cb_ref, sb_ref, c_ref, s_ref):
    ca, sa, cb, sb = ca_ref[...], sa_ref[...], cb_ref[...], sb_ref[...]
    c_ref[...] = (ca * cb - sa * sb).astype(BF16)
    s_ref[...] = (sa * cb + ca * sb).astype(BF16)


def dft_matrices(seq):
    blk = min(seq, 256)
    ca, sa, cb, sb = _dft_tables(seq, blk)
    aspec = pl.BlockSpec((None, 1, seq), lambda i: (i, 0, 0))
    bspec = pl.BlockSpec((blk, seq), lambda i: (0, 0))
    ospec = pl.BlockSpec((blk, seq), lambda i: (i, 0))
    return pl.pallas_call(
        _dft_gen_body,
        grid=(seq // blk,),
        in_specs=[aspec, aspec, bspec, bspec],
        out_specs=[ospec, ospec],
        out_shape=[jax.ShapeDtypeStruct((seq, seq), BF16)] * 2,
        compiler_params=_cparams("parallel"),
        name="dft_matrices",
    )(ca, sa, cb, sb)


def _alt_sign(rows):
    return jnp.where((rows & 1) == 0, 1.0, -1.0).astype(F32)


def _filt_body(h2_ref, wf_ref, wb_ref, dl_ref, a_ref, d_ref, ny_ref):
    seq = h2_ref.shape[0]
    h2 = h2_ref[...]
    row = lax.broadcasted_iota(jnp.int32, (seq, 1), 0)
    t = row.astype(F32) * (1.0 / (seq - 1))
    win = jnp.exp(-t * dl_ref[...])
    hf = jnp.dot(h2, wf_ref[...], precision=HIGHEST, preferred_element_type=F32) * win
    hb = jnp.dot(h2, wb_ref[...], precision=HIGHEST, preferred_element_type=F32) * win
    hb = jnp.where(row == 0, 0.0, hb)
    nrm = (jnp.sum(jnp.abs(hf), axis=0, keepdims=True) + jnp.sum(jnp.abs(hb), axis=0, keepdims=True))
    inv = 1.0 / nrm
    a = (hf + hb) * inv
    a_ref[...] = a.astype(BF16)
    d_ref[...] = ((hb - hf) * inv).astype(BF16)
    ny = jnp.sum(a * _alt_sign(row), axis=0, keepdims=True) * (1.0 / (2 * seq))
    ny_ref[...] = jnp.broadcast_to(ny, ny_ref.shape)


def _khat_body(a_ref, d_ref, c_ref, s_ref, kr_ref, ki_ref):
    i = pl.program_id(1)
    tm = c_ref.shape[0]
    seq = c_ref.shape[1]
    f = i * tm + lax.broadcasted_iota(jnp.int32, (tm, 1), 0)
    w = jnp.where(f == 0, 1.0, 2.0).astype(F32) * (1.0 / (2 * seq))
    kr_ref[...] = _dot(c_ref[...], a_ref[...]) * w
    ki_ref[...] = _dot(s_ref[...], d_ref[...]) * w


def hyena_filter_hat(seq, fw1, fb1, fw2, fb2, fw3, freq, cmat, smat):
    d = fw3.shape[1] // 2
    bands_n = (fw1.shape[0] - 1) // 2
    t = jnp.linspace(0.0, 1.0, seq, dtype=F32)[:, None]
    w = (2.0 * math.pi / seq) * jnp.arange(seq, dtype=F32)[:, None]
    bands = jnp.linspace(1e-4, bands_n - 1, bands_n, dtype=F32)[None, :]
    z = jnp.concatenate([t, jnp.cos(bands * w), -jnp.sin(bands * w)], axis=-1)
    h = jnp.sin(freq * (jnp.dot(z, fw1, precision=HIGHEST) + fb1))
    h2 = jnp.sin(freq * (jnp.dot(h, fw2, precision=HIGHEST) + fb2))
    max_decay = math.log(HY_DECAY_TARGET) / HY_FAST_PCT
    min_decay = math.log(HY_DECAY_TARGET) / HY_SLOW_PCT
    deltas = jnp.abs(jnp.linspace(min_decay, max_decay, d, dtype=F32))[None, :]

    order = h2.shape[1]
    tn = min(d, 256)
    nj = d // tn
    a, dd, ny = pl.pallas_call(
        _filt_body,
        grid=(nj,),
        in_specs=[pl.BlockSpec((seq, order), lambda j: (0, 0)),
                  pl.BlockSpec((order, tn), lambda j: (0, j)),
                  pl.BlockSpec((order, tn), lambda j: (0, nj + j)),
                  pl.BlockSpec((1, tn), lambda j: (0, j))],
        out_specs=[pl.BlockSpec((seq, tn), lambda j: (0, j)),
                   pl.BlockSpec((seq, tn), lambda j: (0, j)),
                   pl.BlockSpec((SUBLANES, tn), lambda j: (0, j))],
        out_shape=[jax.ShapeDtypeStruct((seq, d), BF16), jax.ShapeDtypeStruct((seq, d), BF16),
                   jax.ShapeDtypeStruct((SUBLANES, d), F32)],
        compiler_params=_cparams("parallel"),
        name="hyena_filter_taps",
    )(h2, fw3, fw3, deltas)

    tm = min(seq, 512)
    tn2 = min(d, 512)
    kr, ki = pl.pallas_call(
        _khat_body,
        grid=(d // tn2, seq // tm),
        in_specs=[pl.BlockSpec((seq, tn2), lambda j, i: (0, j)),
                  pl.BlockSpec((seq, tn2), lambda j, i: (0, j)),
                  pl.BlockSpec((tm, seq), lambda j, i: (i, 0)),
                  pl.BlockSpec((tm, seq), lambda j, i: (i, 0))],
        out_specs=[pl.BlockSpec((tm, tn2), lambda j, i: (i, j))] * 2,
        out_shape=[jax.ShapeDtypeStruct((seq, d), F32)] * 2,
        compiler_params=_cparams("parallel", "parallel"),
        name="hyena_filter_dft",
    )(a, dd, cmat, smat)
    return kr, ki, ny


def _dft_fwd_body(v_ref, c_ref, s_ref, kr_ref, ki_ref, kn_ref, ya_ref, yb_ref, yn_ref):
    i = pl.program_id(2)
    v = v_ref[...]
    vr = _dot(c_ref[...], v)
    p = _dot(s_ref[...], v)
    kr = kr_ref[...]
    ki = ki_ref[...]
    ya_ref[...] = (vr * kr + p * ki).astype(BF16)
    yb_ref[...] = (p * kr - vr * ki).astype(BF16)

    @pl.when(i == 0)
    def _():
        seq = v.shape[0]
        row = lax.broadcasted_iota(jnp.int32, (seq, 1), 0)
        vl = jnp.sum(v.astype(F32) * _alt_sign(row), axis=0, keepdims=True)
        yn_ref[...] = jnp.broadcast_to(vl * kn_ref[0:1, :], yn_ref.shape)


def _dft_inv_body(ya_ref, yb_ref, c_ref, s_ref, v_ref, x0_ref, skip_ref, yn_ref, o_ref):
    i = pl.program_id(2)
    tm = c_ref.shape[0]
    acc = _dot(c_ref[...], ya_ref[...]) + _dot(s_ref[...], yb_ref[...])
    t = i * tm + lax.broadcasted_iota(jnp.int32, (tm, 1), 0)
    y = acc + _alt_sign(t) * yn_ref[0:1, :] + skip_ref[...] * v_ref[...].astype(F32)
    o_ref[...] = (y * x0_ref[...].astype(F32)).astype(BF16)


def hyena_conv(v, x0, skip, kr, ki, kn, cmat, smat):
    bsz, seq, d = v.shape
    tm = min(seq, 512)
    tn = min(d, 512)
    grid = (bsz, d // tn, seq // tm)
    full = pl.BlockSpec((None, seq, tn), lambda b, j, i: (b, 0, j))
    mat = pl.BlockSpec((tm, seq), lambda b, j, i: (i, 0))
    tile = pl.BlockSpec((None, tm, tn), lambda b, j, i: (b, i, j))
    ktile = pl.BlockSpec((tm, tn), lambda b, j, i: (i, j))
    nyq = pl.BlockSpec((None, SUBLANES, tn), lambda b, j, i: (b, 0, j))
    ya, yb, yn = pl.pallas_call(
        _dft_fwd_body,
        grid=grid,
        in_specs=[full, mat, mat, ktile, ktile, pl.BlockSpec((SUBLANES, tn), lambda b, j, i: (0, j))],
        out_specs=[tile, tile, nyq],
        out_shape=[jax.ShapeDtypeStruct((bsz, seq, d), BF16)] * 2
        + [jax.ShapeDtypeStruct((bsz, SUBLANES, d), F32)],
        compiler_params=_cparams("parallel", "parallel", "arbitrary"),
        name="hyena_dft_fwd",
    )(v, cmat, smat, kr, ki, kn)
    return pl.pallas_call(
        _dft_inv_body,
        grid=grid,
        in_specs=[full, full, mat, mat, tile, tile, pl.BlockSpec((1, tn), lambda b, j, i: (0, j)), nyq],
        out_specs=tile,
        out_shape=jax.ShapeDtypeStruct((bsz, seq, d), BF16),
        compiler_params=_cparams("parallel", "parallel", "parallel"),
        name="hyena_dft_inv",
    )(ya, yb, cmat, smat, v, x0, skip.reshape(1, d), yn)


def _mm_res_body(x_ref, w_ref, b_ref, res_ref, gate_ref, o_ref):
    o_ref[...] = res_ref[...] + gate_ref[...] * (_dot(x_ref[...], w_ref[...]) + b_ref[...])


def mm_residual(x, w, b, res, gate):
    bsz, seq, k = x.shape
    n = w.shape[1]
    tm = min(seq, 512)
    return pl.pallas_call(
        _mm_res_body,
        grid=(bsz, seq // tm),
        in_specs=[pl.BlockSpec((None, tm, k), lambda b, i: (b, i, 0)),
                  pl.BlockSpec((k, n), lambda b, i: (0, 0)),
                  pl.BlockSpec((1, n), lambda b, i: (0, 0)),
                  pl.BlockSpec((None, tm, n), lambda b, i: (b, i, 0)),
                  pl.BlockSpec((None, 1, n), lambda b, i: (b, 0, 0))],
        out_specs=pl.BlockSpec((None, tm, n), lambda b, i: (b, i, 0)),
        out_shape=jax.ShapeDtypeStruct((bsz, seq, n), F32),
        compiler_params=_cparams("parallel", "parallel"),
        name="mm_residual",
    )(x, w, b.reshape(1, n), res, gate)


def _moe_pre_body(*refs, n_groups, n_experts, tile_offs):
    n_streams = len(tile_offs) - 1
    g_ref, wr_ref, br_ref, tok_ref, eid_ref, gate_ref = refs[3 * n_streams:]
    i = pl.program_id(0)
    for k in range(n_streams):
        x_ref, sh_ref, sc_ref = refs[3 * k:3 * k + 3]

        @pl.when(jnp.logical_and(i >= tile_offs[k], i < tile_offs[k + 1]))
        def _():
            tok = _norm_mod(x_ref[...], g_ref[...], sh_ref[...], sc_ref[...])
            _route_tokens(tok, wr_ref, br_ref, tok_ref, eid_ref, gate_ref, n_groups, n_experts)


def _route_tokens(tok, wr_ref, br_ref, tok_ref, eid_ref, gate_ref, n_groups, n_experts):
    tok_ref[...] = tok
    logits = jnp.dot(tok, wr_ref[...], precision=HIGHEST, preferred_element_type=F32) + br_ref[...]
    lane = lax.broadcasted_iota(jnp.int32, logits.shape, 1)
    per = n_experts // n_groups
    big = jnp.int32(1 << 20)
    gmask = jnp.logical_and(lane >= n_experts, lane < n_experts + n_groups)
    gl = jnp.where(gmask, logits, NEG_BIG)
    gmax = jnp.max(gl, axis=-1, keepdims=True)
    gidx = jnp.min(jnp.where(gl == gmax, lane - n_experts, big), axis=-1, keepdims=True)
    p_top = 1.0 / jnp.sum(jnp.where(gmask, jnp.exp(gl - gmax), 0.0), axis=-1, keepdims=True)
    lo = gidx * per
    emask = jnp.logical_and(lane >= lo, lane < lo + per)
    el = jnp.where(emask, logits, NEG_BIG)
    m1 = jnp.max(el, axis=-1, keepdims=True)
    i1 = jnp.min(jnp.where(el == m1, lane, big), axis=-1, keepdims=True)
    el2 = jnp.where(lane == i1, NEG_BIG, el)
    m2 = jnp.max(el2, axis=-1, keepdims=True)
    i2 = jnp.min(jnp.where(el2 == m2, lane, big), axis=-1, keepdims=True)
    e21 = jnp.exp(m2 - m1)
    g1 = p_top / (1.0 + e21)
    g2 = g1 * e21
    eid_ref[...] = jnp.where(lane == 0, i1, jnp.where(lane == 1, i2, 0))
    gate_ref[...] = jnp.where(lane == 0, g1, jnp.where(lane == 1, g2, 0.0))


def moe_pre(streams, g, wr, br, n_groups, n_experts):
    d = streams[0][0].shape[2]
    tm = MOE_TM
    tile_offs = [0]
    in_specs, args = [], []
    for x, shift, scale in streams:
        bsz, seq, _ = x.shape
        nt = seq // tm
        n_tiles = bsz * nt
        off = tile_offs[-1]
        tile_offs.append(off + n_tiles)

        def tile(i, off=off, n_tiles=n_tiles):
            return jnp.clip(i - off, 0, n_tiles - 1)

        in_specs += [pl.BlockSpec((None, tm, d), lambda i, tile=tile, nt=nt: (tile(i) // nt, tile(i) % nt, 0)),
                     pl.BlockSpec((None, 1, d), lambda i, tile=tile, nt=nt: (tile(i) // nt, 0, 0)),
                     pl.BlockSpec((None, 1, d), lambda i, tile=tile, nt=nt: (tile(i) // nt, 0, 0))]
        args += [x, shift, scale]
    in_specs += [pl.BlockSpec((1, d), lambda i: (0, 0)),
                 pl.BlockSpec((d, LANES), lambda i: (0, 0)),
                 pl.BlockSpec((1, LANES), lambda i: (0, 0))]
    args += [g.reshape(1, d), wr, br]
    total = tile_offs[-1] * tm
    rout = pl.BlockSpec((tm, LANES), lambda i: (i, 0))
    tok, eid, gate = pl.pallas_call(
        functools.partial(_moe_pre_body, n_groups=n_groups, n_experts=n_experts, tile_offs=tuple(tile_offs)),
        grid=(tile_offs[-1],),
        in_specs=in_specs,
        out_specs=[pl.BlockSpec((tm, d), lambda i: (i, 0)), rout, rout],
        out_shape=[jax.ShapeDtypeStruct((total, d), F32),
                   jax.ShapeDtypeStruct((total, LANES), jnp.int32),
                   jax.ShapeDtypeStruct((total, LANES), F32)],
        compiler_params=_cparams("parallel"),
        name="moe_pre",
    )(*args)
    return tok, eid, gate, tile_offs[:-1]


def _start_row_gather(idx_ref, n_rows, stride, offset, src_hbm, dst_vmem, sem):
    def body(r, c):
        pltpu.make_async_copy(src_hbm.at[pl.ds(idx_ref[stride * r + offset], 1)],
                              dst_vmem.at[pl.ds(r, 1)], sem).start()
        return c

    lax.fori_loop(0, n_rows, body, 0, unroll=8)


def _wait_row_gather(n_rows, src_hbm, dst_vmem, sem):
    pltpu.make_async_copy(src_hbm.at[pl.ds(0, n_rows)], dst_vmem, sem).wait()


def _expert_body(bv_ref, rk_ref, pe_ref, tot_ref, src_ref, nxt_ref, tok_ref, wg_hbm, wu_hbm, wd_hbm, o_ref,
                 xbuf, xsem, wg_v, wu_v, wd_v, stg, wsem, cnt, *, layer):
    i = pl.program_id(0)
    n = pl.num_programs(0)
    slot = i % 2
    cr, cc = stg.shape[1:]
    total = tot_ref[0]
    mats_hbm = (wg_hbm, wu_hbm, wd_hbm)
    mats_vmem = (wg_v, wu_v, wd_v)

    @pl.when(i == 0)
    def _():
        cnt[0] = 0
        cnt[1] = 0

    @pl.when(jnp.logical_and(i == 0, bv_ref[0] > 0))
    def _():
        _start_row_gather(src_ref, MOE_BM, 1, 0, tok_ref, xbuf.at[0], xsem.at[0])

    @pl.when(jnp.logical_and(i + 1 < n, bv_ref[jnp.minimum(i + 1, n - 1)] > 0))
    def _():
        _start_row_gather(nxt_ref, MOE_BM, 1, 0, tok_ref, xbuf.at[1 - slot], xsem.at[1 - slot])

    def chunk_geom(c):
        q = c % MOE_NCH
        m = q // 4
        sub = q % 4
        r0 = jnp.where(m < 2, sub, sub // 2) * cr
        c0 = jnp.where(m < 2, 0, sub % 2) * cc
        return m, pl.multiple_of(r0, cr), pl.multiple_of(c0, cc)

    def issue(c):
        e = pe_ref[c // MOE_NCH]
        m, r0, c0 = chunk_geom(c)
        s = c % MOE_STAGE
        for k, w_hbm in enumerate(mats_hbm):
            @pl.when(m == k)
            def _():
                pltpu.make_async_copy(w_hbm.at[layer, e, pl.ds(r0, cr), pl.ds(c0, cc)], stg.at[s],
                                      wsem.at[s]).start()

    def cast(c):
        m, r0, c0 = chunk_geom(c)
        s = c % MOE_STAGE
        ws = (c // MOE_NCH) % 2
        pltpu.make_async_copy(wg_hbm.at[layer, 0, pl.ds(0, cr), pl.ds(0, cc)], stg.at[s], wsem.at[s]).wait()
        val = stg[s].astype(BF16)
        for k, w_v in enumerate(mats_vmem):
            @pl.when(m == k)
            def _():
                w_v[ws, pl.ds(r0, cr), pl.ds(c0, cc)] = val

    valid = bv_ref[i] > 0
    rank = rk_ref[i]
    issued = cnt[0]
    done = cnt[1]
    limit = jnp.minimum(total, MOE_NCH * (rank + 2))
    need = jnp.where(valid, MOE_NCH * (rank + 1), done)

    def fill(issued, done):
        hi = jnp.minimum(limit, done + MOE_STAGE)

        def body(c, carry):
            issue(c)
            return carry

        lax.fori_loop(issued, hi, body, 0)
        return jnp.maximum(issued, hi)

    def cast_and_refill(c, issued):
        cast(c)
        more = issued < jnp.minimum(limit, c + 1 + MOE_STAGE)

        @pl.when(more)
        def _():
            issue(issued)

        return issued + more.astype(jnp.int32)

    issued = fill(issued, done)
    issued = lax.fori_loop(done, need, cast_and_refill, issued)
    done = jnp.maximum(done, need)

    @pl.when(valid)
    def _():
        ws = rank % 2
        _wait_row_gather(MOE_BM, tok_ref, xbuf.at[slot], xsem.at[slot])
        x = xbuf[slot].astype(BF16)
        gate = _dot(x, wg_v[ws])
        up = _dot(x, wu_v[ws])
        h = (gate * jax.nn.sigmoid(gate) * up).astype(BF16)
        o_ref[...] = _dot(h, wd_v[ws])

    @pl.when(jnp.logical_not(valid))
    def _():
        o_ref[...] = jnp.zeros_like(o_ref)

    fetched = issued
    issued = lax.fori_loop(done, fetched, cast_and_refill, issued)
    done = jnp.maximum(done, fetched)
    last = i == n - 1
    tail = jnp.where(last, issued, done)

    def drain(c, carry):
        cast(c)
        return carry

    lax.fori_loop(done, tail, drain, 0)
    cnt[0] = issued
    cnt[1] = jnp.maximum(done, tail)


def moe_experts(tok, src_tok, block_valid, block_rank, present, n_chunks, w_gate, w_up, w_down, layer):
    d = tok.shape[1]
    n_rows = src_tok.shape[0]
    n_blocks = n_rows // MOE_BM
    dh = w_gate.shape[3]
    assert 2 * dh == d and MOE_NCH == 12
    cr, cc = d // 4, dh
    any_spec = pl.BlockSpec(memory_space=pl.ANY)
    grid_spec = pltpu.PrefetchScalarGridSpec(
        num_scalar_prefetch=4,
        grid=(n_blocks,),
        in_specs=[pl.BlockSpec((MOE_BM,), lambda i, *_: (i,), memory_space=pltpu.SMEM),
                  pl.BlockSpec((MOE_BM,), lambda i, *_: (jnp.minimum(i + 1, n_blocks - 1),),
                               memory_space=pltpu.SMEM),
                  any_spec, any_spec, any_spec, any_spec],
        out_specs=pl.BlockSpec((MOE_BM, d), lambda i, *_: (i, 0)),
        scratch_shapes=[pltpu.VMEM((2, MOE_BM, d), F32), pltpu.SemaphoreType.DMA((2,)),
                        pltpu.VMEM((2, d, dh), BF16), pltpu.VMEM((2, d, dh), BF16), pltpu.VMEM((2, dh, d), BF16),
                        pltpu.VMEM((MOE_STAGE, cr, cc), F32), pltpu.SemaphoreType.DMA((MOE_STAGE,)),
                        pltpu.SMEM((2,), jnp.int32)],
    )
    return pl.pallas_call(
        functools.partial(_expert_body, layer=layer),
        grid_spec=grid_spec,
        out_shape=jax.ShapeDtypeStruct((n_rows, d), F32),
        compiler_params=_cparams("arbitrary"),
        name="moe_experts",
    )(block_valid, block_rank, present, n_chunks, src_tok, src_tok, tok, w_gate, w_up, w_down)


def _combine_body(dest_ref, nxt_ref, os_ref, gate_ref, res_ref, gt_ref, fg_ref, o_ref, buf, sem, *, final_norm):
    rows = res_ref.shape[0]
    i = pl.program_id(0)
    n = pl.num_programs(0)
    slot = i % 2

    def start(idx_ref, s):
        for k in range(TOP_K):
            _start_row_gather(idx_ref, rows, TOP_K, k, os_ref, buf.at[s, k], sem.at[s])

    @pl.when(i == 0)
    def _():
        start(dest_ref, 0)

    @pl.when(i + 1 < n)
    def _():
        start(nxt_ref, 1 - slot)

    for k in range(TOP_K):
        _wait_row_gather(rows, os_ref, buf.at[slot, k], sem.at[slot])
    gates = gate_ref[...]
    mo = gates[:, 0:1] * buf[slot, 0] + gates[:, 1:2] * buf[slot, 1]
    y = res_ref[...] + gt_ref[...] * mo
    if final_norm:
        ms = jnp.mean(y * y, axis=-1, keepdims=True)
        y = y * lax.rsqrt(ms + NORM_EPS) * fg_ref[...]
    o_ref[...] = y


def moe_combine(os, dest, gates, tile0, res, gt, final_g, final_norm):
    bsz, seq, d = res.shape
    rows = MOE_TM
    nt = seq // rows
    n = bsz * nt
    tile = pl.BlockSpec((None, rows, d), lambda i: (i // nt, i % nt, 0))
    return pl.pallas_call(
        functools.partial(_combine_body, final_norm=final_norm),
        grid=(n,),
        in_specs=[pl.BlockSpec((rows * TOP_K,), lambda i: (tile0 + i,), memory_space=pltpu.SMEM),
                  pl.BlockSpec((rows * TOP_K,), lambda i: (tile0 + jnp.minimum(i + 1, n - 1),),
                               memory_space=pltpu.SMEM),
                  pl.BlockSpec(memory_space=pl.ANY),
                  pl.BlockSpec((rows, LANES), lambda i: (tile0 + i, 0)),
                  tile,
                  pl.BlockSpec((None, 1, d), lambda i: (i // nt, 0, 0)),
                  pl.BlockSpec((1, d), lambda i: (0, 0))],
        out_specs=tile,
        out_shape=jax.ShapeDtypeStruct((bsz, seq, d), F32),
        scratch_shapes=[pltpu.VMEM((2, TOP_K, rows, d), F32), pltpu.SemaphoreType.DMA((2,))],
        compiler_params=_cparams("arbitrary"),
        name="moe_combine",
    )(dest, dest, os, gates, res, gt, final_g.reshape(1, d))


def _route_plan(eid, n_experts):
    e_flat = eid.reshape(-1)
    a = e_flat.shape[0]
    order = jnp.argsort(e_flat)
    pos = jnp.argsort(order)
    experts = jnp.arange(n_experts, dtype=jnp.int32)
    counts = jnp.sum((e_flat[:, None] == experts[None, :]).astype(jnp.int32), axis=0)
    start = jnp.cumsum(counts) - counts
    padded = (counts + MOE_BM - 1) // MOE_BM * MOE_BM
    pad_end = jnp.cumsum(padded)
    pad_start = pad_end - padded
    dest = pad_start[e_flat] + pos - start[e_flat]
    n_blocks = -(-a // MOE_BM) + n_experts
    starts = jnp.arange(n_blocks, dtype=jnp.int32) * MOE_BM
    block_expert = jnp.minimum(jnp.sum((pad_end[None, :] <= starts[:, None]).astype(jnp.int32), axis=1),
                               n_experts - 1)
    block_valid = (starts < pad_end[-1]).astype(jnp.int32)
    rows = jnp.arange(n_blocks * MOE_BM, dtype=jnp.int32)
    e_row = jnp.repeat(block_expert, MOE_BM)
    off = rows - pad_start[e_row]
    live = jnp.logical_and(off < counts[e_row], jnp.repeat(block_valid, MOE_BM) > 0)
    src_tok = jnp.where(live, order[jnp.clip(start[e_row] + off, 0, a - 1)] // TOP_K, 0)
    has = (counts > 0).astype(jnp.int32)
    block_rank = (jnp.cumsum(has) - 1)[block_expert]
    present = jnp.argsort(1 - has)
    n_chunks = (MOE_NCH * jnp.sum(has)).reshape(1)
    i32 = lambda v: v.astype(jnp.int32)
    return i32(dest), i32(src_tok), block_valid, i32(block_rank), i32(present), i32(n_chunks)


def hier_moe(streams, norm_g, wg, bg, we, be, w_gate, w_up, w_down, layer, final_g, final_norm):
    n_groups = wg.shape[1]
    n_experts = we.shape[1]
    d = wg.shape[0]
    wr = jnp.zeros((d, LANES), F32).at[:, :n_experts].set(we).at[:, n_experts:n_experts + n_groups].set(wg)
    br = jnp.zeros((1, LANES), F32).at[0, :n_experts].set(be).at[0, n_experts:n_experts + n_groups].set(bg)
    tok, eid, gates, tile0s = moe_pre([s[:3] for s in streams], norm_g, wr, br, n_groups, n_experts)
    dest, src_tok, block_valid, block_rank, present, n_chunks = _route_plan(eid[:, :TOP_K], n_experts)
    os = moe_experts(tok, src_tok, block_valid, block_rank, present, n_chunks, w_gate, w_up, w_down, layer)
    return [moe_combine(os, dest, gates, tile0, x, gt, final_g, final_norm)
            for (x, _, _, gt), tile0 in zip(streams, tile0s)]


def _norm_mod_body(x_ref, g_ref, sh_ref, sc_ref, o_ref):
    o_ref[...] = _norm_mod(x_ref[...], g_ref[...], sh_ref[...], sc_ref[...]).astype(o_ref.dtype)


def norm_mod(x, g, shift, scale):
    bsz, seq, d = x.shape
    tm = min(seq, 512)
    mspec = pl.BlockSpec((None, 1, d), lambda b, i: (b, 0, 0))
    return pl.pallas_call(
        _norm_mod_body,
        grid=(bsz, seq // tm),
        in_specs=[pl.BlockSpec((None, tm, d), lambda b, i: (b, i, 0)),
                  pl.BlockSpec((1, d), lambda b, i: (0, 0)), mspec, mspec],
        out_specs=pl.BlockSpec((None, tm, d), lambda b, i: (b, i, 0)),
        out_shape=jax.ShapeDtypeStruct((bsz, seq, d), BF16),
        compiler_params=_cparams("parallel", "parallel"),
        name="norm_mod",
    )(x, g.reshape(1, d), shift, scale)


def _s5_arrange(h):
    bsz, t, d = h.shape
    c = t // (S5_SEGS * S5_TAU)
    h = h.reshape(bsz, S5_SEGS, c, S5_TAU, d // LANES, LANES)
    return h.transpose(2, 0, 1, 4, 3, 5).reshape(c * bsz * S5_SEGS, d * S5_TAU)


def _s5_unarrange(y, bsz):
    r, w = y.shape
    d = w // S5_TAU
    c = r // (bsz * S5_SEGS)
    y = y.reshape(c, bsz, S5_SEGS, d // LANES, S5_TAU, LANES)
    return y.transpose(1, 2, 0, 4, 3, 5).reshape(bsz, S5_SEGS * c * S5_TAU, d)


def _s5_operators(a_re, a_im, log_step, b_re, b_im, c_re, c_im):
    n_g, n_p = a_re.shape[1:]
    n_h = b_re.shape[-1]
    gpt = LANES // n_h
    n_j = n_g // gpt
    tau = S5_TAU
    assert tau * n_h == LANES and 2 * n_p == LANES
    lam_step = lax.complex(a_re, a_im) * jnp.exp(log_step)[..., None]
    lam_bar = jnp.exp(lam_step)
    b_bar = ((lam_bar - 1.0) / lax.complex(a_re, a_im))[..., None] * lax.complex(b_re, b_im)
    c_mat = lax.complex(c_re, c_im)
    ks = jnp.arange(tau + 1, dtype=F32)[None, :, None, None]
    pw = jnp.exp(lam_step[:, None] * ks)
    ein = functools.partial(jnp.einsum, precision=HIGHEST)
    inj_c, cl_c, lt = [], [], []
    tz_c = 0.0
    for d in range(2):
        pos = jnp.arange(tau) if d == 0 else jnp.arange(tau)[::-1]
        inj = (pw[d][tau - 1 - pos][..., None] * b_bar[d][None]).reshape(tau, n_j, gpt, n_p, n_h)
        inj = inj.transpose(1, 0, 2, 4, 3).reshape(n_j, tau * LANES, n_p)
        inj_c.append(jnp.concatenate([inj.real, inj.imag], axis=-1))
        cl = (c_mat[d][None] * pw[d][pos + 1][:, :, None, :]).reshape(tau, n_j, gpt, n_h, n_p)
        cl = cl.transpose(1, 2, 4, 0, 3).reshape(n_j, gpt * n_p, tau * n_h)
        cl_c.append(jnp.concatenate([cl.real, -cl.imag], axis=1))
        mk = ein('gop,kgp,gph->kgoh', c_mat[d], pw[d][:tau], b_bar[d]).real
        diff = pos[:, None] - pos[None, :]
        tz = jnp.where((diff >= 0)[:, :, None, None, None], mk[jnp.clip(diff, 0, tau - 1)], 0.0)
        tz = tz.reshape(tau, tau, n_j, gpt, n_h, n_h)
        tz_c = tz_c + tz.transpose(2, 1, 3, 5, 0, 4).reshape(n_j, tau * LANES, tau * n_h)
        lt_d = pw[d][tau].reshape(n_j, 1, gpt * n_p)
        lt.append(jnp.concatenate([lt_d.real, lt_d.imag], axis=-1))
    ws, wu, wh = s5_expand(jnp.stack(inj_c).astype(BF16), jnp.stack(cl_c).astype(BF16), tz_c.astype(BF16),
                           n_h, n_p)
    return ws, wu, wh, jnp.stack(lt).astype(F32)


def _s5_expand_body(inj_ref, cl_ref, tz_ref, ws_ref, wu_ref, wh_ref, *, n_h, n_p):
    rows = tz_ref.shape[0]
    gpt = LANES // n_h
    row = lax.broadcasted_iota(jnp.int32, (rows, LANES), 0)
    lane = lax.broadcasted_iota(jnp.int32, (rows, LANES), 1)
    sel_r = lax.broadcasted_iota(jnp.int32, (LANES, LANES), 0)
    sel_l = lax.broadcasted_iota(jnp.int32, (LANES, LANES), 1)
    grp_in = (row // n_h) % gpt
    grp_st = (row // n_p) % gpt

    def spread_out(m, t, grp_row):
        sel = jnp.logical_and(sel_r // n_h == t, sel_r % n_h == sel_l % n_h).astype(BF16)
        return jnp.where(grp_row == lane // n_h, _dot(m, sel), 0.0).astype(BF16)

    def spread_state(m, c, q, grp_row):
        sel = jnp.logical_and(sel_r // n_p == c, sel_r % n_p == sel_l % n_p).astype(BF16)
        return jnp.where(grp_row == (LANES // n_p) * q + lane // n_p, _dot(m, sel), 0.0).astype(BF16)

    w2 = 2 * gpt * n_p
    tz = tz_ref[...]
    for t in range(S5_TAU):
        wu_ref[:, t * LANES:(t + 1) * LANES] = spread_out(tz, t, grp_in)
    for d in range(2):
        cl = cl_ref[d]
        inj = inj_ref[d]
        for t in range(S5_TAU):
            wh_ref[d, :, t * LANES:(t + 1) * LANES] = spread_out(cl, t, grp_st)
        for c in range(2):
            for q in range(gpt * n_p // LANES):
                lo = d * w2 + c * gpt * n_p + q * LANES
                ws_ref[:, lo:lo + LANES] = spread_state(inj, c, q, grp_in)


def s5_expand(inj_c, cl_c, tz_c, n_h, n_p):
    n_j, rows, _ = tz_c.shape
    gpt = LANES // n_h
    w2 = 2 * gpt * n_p
    assert rows == S5_TAU * LANES == w2
    cspec = pl.BlockSpec((2, None, rows, LANES), lambda j: (0, j, 0, 0))
    return pl.pallas_call(
        functools.partial(_s5_expand_body, n_h=n_h, n_p=n_p),
        grid=(n_j,),
        in_specs=[cspec, cspec, pl.BlockSpec((None, rows, LANES), lambda j: (j, 0, 0))],
        out_specs=[pl.BlockSpec((None, rows, 2 * w2), lambda j: (j, 0, 0)),
                   pl.BlockSpec((None, rows, rows), lambda j: (j, 0, 0)),
                   pl.BlockSpec((2, None, w2, rows), lambda j: (0, j, 0, 0))],
        out_shape=[jax.ShapeDtypeStruct((n_j, rows, 2 * w2), BF16),
                   jax.ShapeDtypeStruct((n_j, rows, rows), BF16),
                   jax.ShapeDtypeStruct((2, n_j, w2, rows), BF16)],
        compiler_params=_cparams("parallel"),
        name="s5_expand",
    )(inj_c, cl_c, tz_c)


def _s5_inj_body(x_ref, w_ref, o_ref):
    o_ref[...] = _dot(x_ref[...], w_ref[...])


def s5_inject(xr, ws):
    r = xr.shape[0]
    n_j, k, n = ws.shape
    tm = r // 2 if r % 32 == 0 else r
    return pl.pallas_call(
        _s5_inj_body,
        grid=(n_j, r // tm),
        in_specs=[pl.BlockSpec((tm, k), lambda j, i: (i, j)),
                  pl.BlockSpec((None, k, n), lambda j, i: (j, 0, 0))],
        out_specs=pl.BlockSpec((tm, n), lambda j, i: (i, j)),
        out_shape=jax.ShapeDtypeStruct((r, n_j * n), F32),
        compiler_params=_cparams("parallel", "parallel"),
        name="s5_inject",
    )(xr, ws)


def _cmul(ar, ai, br, bi):
    return ar * br - ai * bi, ar * bi + ai * br


def _s5_scan_body(s_ref, lt_ref, h_ref, raw_ref, *, n_ctx, n_lat, bsz):
    d = pl.program_id(1)
    w2 = lt_ref.shape[-1]
    w = w2 // 2
    rows = bsz * S5_SEGS
    seg = lax.broadcasted_iota(jnp.int32, (rows, 1), 0) % S5_SEGS
    is_late = seg != d
    lam_r = lt_ref[:, 0:w]
    lam_i = lt_ref[:, w:w2]
    zero = jnp.zeros((rows, w), F32)
    one = (jnp.ones((1, w), F32), jnp.zeros((1, w), F32))

    def swap_segments(x):
        return jnp.where(seg == 0, pltpu.roll(x, rows - 1, axis=0), pltpu.roll(x, 1, axis=0))

    def phase(c0, n_steps, hin_r, hin_i, write):
        def chunk(k):
            return c0 + jnp.where(d == 0, k, n_steps - 1 - k)

        def step_raw(k, carry):
            hr, hi = carry
            c = chunk(k)
            raw_ref[c - c0, :, 0:w] = hr
            raw_ref[c - c0, :, w:w2] = hi
            nr, ni = _cmul(lam_r, lam_i, hr, hi)
            return nr + s_ref[c, :, 0:w], ni + s_ref[c, :, w:w2]

        er, ei = lax.fori_loop(0, n_steps, step_raw, (zero, zero))
        pr, pi = lax.fori_loop(0, n_steps, lambda k, q: _cmul(lam_r, lam_i, *q), one)
        dr, di = _cmul(pr, pi, hin_r, hin_i)
        first_r = jnp.where(is_late, 0.0, er + dr)
        first_i = jnp.where(is_late, 0.0, ei + di)
        carry_r = jnp.where(is_late, swap_segments(first_r), hin_r)
        carry_i = jnp.where(is_late, swap_segments(first_i), hin_i)
        if write:
            def step_fix(k, q):
                c = chunk(k)
                fr, fi = _cmul(q[0], q[1], carry_r, carry_i)
                h_ref[c - c0, :, 0:w] = (raw_ref[c - c0, :, 0:w] + fr).astype(h_ref.dtype)
                h_ref[c - c0, :, w:w2] = (raw_ref[c - c0, :, w:w2] + fi).astype(h_ref.dtype)
                return _cmul(lam_r, lam_i, q[0], q[1])

            lax.fori_loop(0, n_steps, step_fix, one)
        lr, li = _cmul(pr, pi, carry_r, carry_i)
        last_r = jnp.where(is_late, er + lr, 0.0)
        last_i = jnp.where(is_late, ei + li, 0.0)
        return (jnp.where(is_late, 0.0, swap_segments(last_r)), jnp.where(is_late, 0.0, swap_segments(last_i)))

    hr, hi = zero, zero
    if n_ctx:
        hr, hi = phase(0, n_ctx, hr, hi, False)
    phase(n_ctx, n_lat, hr, hi, True)


def s5_scan(s, lt, n_ctx, n_lat, bsz):
    assert S5_SEGS == 2
    n_c, rows, _ = s.shape
    n_j = lt.shape[1]
    w2 = lt.shape[-1]
    return pl.pallas_call(
        functools.partial(_s5_scan_body, n_ctx=n_ctx, n_lat=n_lat, bsz=bsz),
        grid=(n_j, 2),
        in_specs=[pl.BlockSpec((n_c, rows, w2), lambda j, d: (0, 0, 2 * j + d)),
                  pl.BlockSpec((None, None, 1, w2), lambda j, d: (d, j, 0, 0))],
        out_specs=pl.BlockSpec((None, n_lat, rows, w2), lambda j, d: (d, 0, 0, j)),
        out_shape=jax.ShapeDtypeStruct((2, n_lat, rows, n_j * w2), BF16),
        scratch_shapes=[pltpu.VMEM((max(n_ctx, n_lat), rows, w2), F32)],
        compiler_params=_cparams("parallel", "parallel"),
        name="s5_scan",
    )(s, lt)


def _s5_out_body(x_ref, hf_ref, hb_ref, wu_ref, whf_ref, whb_ref, o_ref):
    o_ref[...] = (_dot(x_ref[...], wu_ref[...]) + _dot(hf_ref[...], whf_ref[...])
                  + _dot(hb_ref[...], whb_ref[...]))


def s5_readout(xr, h, wu, wh):
    r = xr.shape[0]
    n_j, k, n = wu.shape
    w2 = wh.shape[2]
    tm = min(r, 1024)
    return pl.pallas_call(
        _s5_out_body,
        grid=(n_j, r // tm),
        in_specs=[pl.BlockSpec((tm, k), lambda j, i: (i, j)),
                  pl.BlockSpec((None, tm, w2), lambda j, i: (0, i, j)),
                  pl.BlockSpec((None, tm, w2), lambda j, i: (1, i, j)),
                  pl.BlockSpec((None, k, n), lambda j, i: (j, 0, 0)),
                  pl.BlockSpec((None, None, w2, n), lambda j, i: (0, j, 0, 0)),
                  pl.BlockSpec((None, None, w2, n), lambda j, i: (1, j, 0, 0))],
        out_specs=pl.BlockSpec((tm, n), lambda j, i: (i, j)),
        out_shape=jax.ShapeDtypeStruct((r, n_j * n), F32),
        compiler_params=_cparams("parallel", "parallel"),
        name="s5_readout",
    )(xr, h, h, wu, wh, wh)


def _gelu_tanh(x):
    return 0.5 * x * (1.0 + jnp.tanh(math.sqrt(2.0 / math.pi) * (x + 0.044715 * (x * x * x))))


def _s5_glu_body(x_ref, y_ref, g_ref, sh_ref, sc_ref, dk_ref, w1_ref, w2_ref, b1_ref, b2_ref, gt_ref, o_ref):
    x = x_ref[...]
    u = _norm_mod(x, g_ref[...], sh_ref[...], sc_ref[...])
    y = _gelu_tanh(y_ref[...] + dk_ref[...] * u).astype(BF16)
    o = (_dot(y, w1_ref[...]) + b1_ref[...]) * jax.nn.sigmoid(_dot(y, w2_ref[...]) + b2_ref[...])
    o_ref[...] = x + gt_ref[...] * o


def s5_glu(x, y, g, shift, scale, d_skip, w1, b1, w2, b2, gate):
    bsz, seq, d = x.shape
    tm = min(seq, 256)
    row = lambda a: a.reshape(1, d)
    rspec = pl.BlockSpec((1, d), lambda b, i: (0, 0))
    mspec = pl.BlockSpec((None, 1, d), lambda b, i: (b, 0, 0))
    tile = pl.BlockSpec((None, tm, d), lambda b, i: (b, i, 0))
    wspec = pl.BlockSpec((d, d), lambda b, i: (0, 0))
    return pl.pallas_call(
        _s5_glu_body,
        grid=(bsz, seq // tm),
        in_specs=[tile, tile, rspec, mspec, mspec, rspec, wspec, wspec, rspec, rspec, mspec],
        out_specs=tile,
        out_shape=jax.ShapeDtypeStruct((bsz, seq, d), F32),
        compiler_params=_cparams("parallel", "parallel"),
        name="s5_glu",
    )(x, y, row(g), shift, scale, row(d_skip), w1, w2, row(b1), row(b2), gate)


def s5_mix(xl, xc, g, sh_l, sc_l, sh_c, sc_c, gate_l, a_re, a_im, log_step, b_re, b_im, c_re, c_im, d_skip,
           w1, b1, w2, b2):
    bsz, seq, d = xl.shape
    hl = norm_mod(xl, g, sh_l, sc_l)
    hc = norm_mod(xc, g, sh_c, sc_c)
    xr_c = _s5_arrange(hc)
    xr_l = _s5_arrange(hl)
    rows = bsz * S5_SEGS
    n_ctx = xr_c.shape[0] // rows
    n_lat = xr_l.shape[0] // rows
    ws, wu, wh, lt = _s5_operators(a_re, a_im, log_step, b_re, b_im, c_re, c_im)
    s = s5_inject(jnp.concatenate([xr_c, xr_l], axis=0), ws)
    h = s5_scan(s.reshape(n_ctx + n_lat, rows, -1), lt, n_ctx, n_lat, bsz)
    y = s5_readout(xr_l, h.reshape(2, n_lat * rows, -1), wu, wh)
    y = _s5_unarrange(y, bsz)
    return s5_glu(xl, y, g, sh_l, sc_l, d_skip, w1.astype(BF16), b1, w2.astype(BF16), b2, gate_l)


def hyena_mix(x, g, shift, scale, gate, w_in, b_in, conv_w, conv_b, fw1, fb1, fw2, fb2, fw3, freq, skip,
              w_out, b_out):
    seq = x.shape[1]
    cmat, smat = dft_matrices(seq)
    kr, ki, kn = hyena_filter_hat(seq, fw1, fb1, fw2, fb2, fw3, freq, cmat, smat)
    v, x0 = hyena_in(x, g, shift, scale, w_in, b_in, conv_w, conv_b)
    yg = hyena_conv(v, x0, skip, kr, ki, kn, cmat, smat)
    return mm_residual(yg, w_out, b_out, x, gate)


def kernel(x, c, ctx, c_ctx, ada_w, ada_b, norm_g, final_g, hy_w_in, hy_b_in, hy_conv_w, hy_conv_b, hy_fw1,
           hy_fb1, hy_fw2, hy_fb2, hy_fw3, hy_freq, hy_skip, hy_w_out, hy_b_out, s5_a_re, s5_a_im,
           s5_log_step, s5_b_re, s5_b_im, s5_c_re, s5_c_im, s5_d, s5_w1, s5_b1, s5_w2, s5_b2, moe_wg, moe_bg,
           moe_we, moe_be, moe_w_gate, moe_w_up, moe_w_down):
    bsz, _, d = x.shape
    depth = ada_w.shape[0]
    assert depth == 2 and bsz < SUBLANES
    c_all = jnp.zeros((SUBLANES, d), F32).at[:bsz].set(c).at[bsz].set(c_ctx)
    mods = ada_mod(c_all, ada_w, ada_b)

    def mod_rows(layer, k):
        lat = mods[layer, :bsz, k * d:(k + 1) * d][:, None, :]
        cx = jnp.broadcast_to(mods[layer, bsz, k * d:(k + 1) * d][None, None, :], (bsz, 1, d))
        return lat, cx

    (sh_a, csh_a), (sc_a, csc_a), (gt_a, cgt_a) = mod_rows(0, 0), mod_rows(0, 1), mod_rows(0, 2)
    (sh_f, csh_f), (sc_f, csc_f), (gt_f, cgt_f) = mod_rows(0, 3), mod_rows(0, 4), mod_rows(0, 5)
    hy = (hy_w_in[0].astype(BF16), hy_b_in[0], hy_conv_w[0], hy_conv_b[0], hy_fw1[0], hy_fb1[0], hy_fw2[0],
          hy_fb2[0], hy_fw3[0], hy_freq[0], hy_skip[0], hy_w_out[0].astype(BF16), hy_b_out[0])
    xl = hyena_mix(x, norm_g[0, 0], sh_a, sc_a, gt_a, *hy)
    xc = hyena_mix(ctx, norm_g[0, 0], csh_a, csc_a, cgt_a, *hy)
    xl, xc = hier_moe([(xl, sh_f, sc_f, gt_f), (xc, csh_f, csc_f, cgt_f)], norm_g[0, 1],
                      moe_wg[0], moe_bg[0], moe_we[0], moe_be[0], moe_w_gate, moe_w_up, moe_w_down, 0,
                      final_g, False)

    (sh_a, csh_a), (sc_a, csc_a), (gt_a, _) = mod_rows(1, 0), mod_rows(1, 1), mod_rows(1, 2)
    (sh_f, _), (sc_f, _), (gt_f, _) = mod_rows(1, 3), mod_rows(1, 4), mod_rows(1, 5)
    xl = s5_mix(xl, xc, norm_g[1, 0], sh_a, sc_a, csh_a, csc_a, gt_a, s5_a_re[0], s5_a_im[0], s5_log_step[0],
                s5_b_re[0], s5_b_im[0], s5_c_re[0], s5_c_im[0], s5_d[0], s5_w1[0], s5_b1[0], s5_w2[0], s5_b2[0])
    (out,) = hier_moe([(xl, sh_f, sc_f, gt_f)], norm_g[1, 1], moe_wg[1], moe_bg[1], moe_we[1], moe_be[1],
                      moe_w_gate, moe_w_up, moe_w_down, 1, final_g, True)
    return out
```

```python
import functools
import math

import jax
import jax.numpy as jnp
from jax import lax
from jax.experimental import pallas as pl
from jax.experimental.pallas import tpu as pltpu

F32 = jnp.float32
BF16 = jnp.bfloat16
HIGHEST = lax.Precision.HIGHEST

NORM_EPS = 1e-6
HY_DECAY_TARGET = 1e-2
HY_FAST_PCT = 0.3
HY_SLOW_PCT = 1.5
TOP_K = 2

V7X_VMEM_LIMIT_BYTES = 56 * 1024 * 1024
LANES = 128
SUBLANES = 8
S5_TAU = 8
S5_SEGS = 2
MOE_TM = 256
MOE_BM = 256
MOE_NCH = 12
MOE_STAGE = 4
NEG_BIG = -1e30


def _cparams(*sem):
    return pltpu.CompilerParams(dimension_semantics=sem, vmem_limit_bytes=V7X_VMEM_LIMIT_BYTES)


def _norm_mod(x, g, shift, scale):
    ms = jnp.mean(x * x, axis=-1, keepdims=True)
    return (x * lax.rsqrt(ms + NORM_EPS) * g) * (1.0 + scale) + shift


def _dot(a, b):
    return jnp.dot(a, b, preferred_element_type=F32)


def _ada_body(c_ref, w_ref, b_ref, o_ref):
    x = c_ref[...]
    s = (x * jax.nn.sigmoid(x)).astype(BF16)
    o_ref[...] = _dot(s, w_ref[...].astype(BF16)) + b_ref[...]


def ada_mod(c_all, ada_w, ada_b):
    depth, d, n = ada_w.shape
    tn = min(n, 1024)
    return pl.pallas_call(
        _ada_body,
        grid=(depth, n // tn),
        in_specs=[pl.BlockSpec((SUBLANES, d), lambda l, j: (0, 0)),
                  pl.BlockSpec((None, d, tn), lambda l, j: (l, 0, j)),
                  pl.BlockSpec((None, 1, tn), lambda l, j: (l, 0, j))],
        out_specs=pl.BlockSpec((None, SUBLANES, tn), lambda l, j: (l, 0, j)),
        out_shape=jax.ShapeDtypeStruct((depth, SUBLANES, n), F32),
        compiler_params=_cparams("parallel", "parallel"),
        name="ada_mod",
    )(c_all, ada_w, ada_b.reshape(depth, 1, n))


def _hy_in_body(xp_ref, xm_ref, xn_ref, g_ref, sh_ref, sc_ref,
                w0_ref, w1_ref, w2_ref, b0_ref, b1_ref, b2_ref,
                cw0_ref, cw1_ref, cw2_ref, cb0_ref, cb1_ref, cb2_ref,
                v_ref, x0_ref):
    i = pl.program_id(2)
    ni = pl.num_programs(2)
    tm = xm_ref.shape[0]
    x = jnp.concatenate([xp_ref[...], xm_ref[...], xn_ref[...]], axis=0)
    h = _norm_mod(x, g_ref[...], sh_ref[...], sc_ref[...]).astype(BF16)
    rows = lax.broadcasted_iota(jnp.int32, (tm + 2 * SUBLANES, 1), 0)
    valid = jnp.logical_and(jnp.logical_or(rows >= SUBLANES, i > 0),
                            jnp.logical_or(rows < tm + SUBLANES, i < ni - 1))

    def part(w_ref, b_ref, cw_ref, cb_ref):
        z = jnp.where(valid, _dot(h, w_ref[...]) + b_ref[...], 0.0)
        cw = cw_ref[...]
        zp = pltpu.roll(z, 1, axis=0)[SUBLANES:tm + SUBLANES]
        zn = pltpu.roll(z, tm + 2 * SUBLANES - 1, axis=0)[SUBLANES:tm + SUBLANES]
        return zp * cw[0:1] + z[SUBLANES:tm + SUBLANES] * cw[1:2] + zn * cw[2:3] + cb_ref[...]

    x0 = part(w0_ref, b0_ref, cw0_ref, cb0_ref)
    x1 = part(w1_ref, b1_ref, cw1_ref, cb1_ref)
    v = part(w2_ref, b2_ref, cw2_ref, cb2_ref) * x1
    v_ref[...] = v.astype(v_ref.dtype)
    x0_ref[...] = x0.astype(BF16)


def hyena_in(x, g, shift, scale, w_in, b_in, conv_w, conv_b, v_dtype):
    bsz, seq, d = x.shape
    tm = min(seq, 512)
    tn = min(d, 1024)
    nj = d // tn
    r8 = tm // SUBLANES
    last8 = seq // SUBLANES - 1
    row = lambda a: a.reshape(1, -1)
    wspec = lambda k: pl.BlockSpec((d, tn), lambda j, b, i: (0, k * nj + j))
    rspec = lambda k: pl.BlockSpec((1, tn), lambda j, b, i: (0, k * nj + j))
    cspec = lambda k: pl.BlockSpec((3, tn), lambda j, b, i: (0, k * nj + j))
    mspec = pl.BlockSpec((None, 1, d), lambda j, b, i: (b, 0, 0))
    out_spec = pl.BlockSpec((None, tm, tn), lambda j, b, i: (b, i, j))
    return pl.pallas_call(
        _hy_in_body,
        grid=(nj, bsz, seq // tm),
        in_specs=[pl.BlockSpec((None, SUBLANES, d), lambda j, b, i: (b, jnp.maximum(i * r8 - 1, 0), 0)),
                  pl.BlockSpec((None, tm, d), lambda j, b, i: (b, i, 0)),
                  pl.BlockSpec((None, SUBLANES, d), lambda j, b, i: (b, jnp.minimum((i + 1) * r8, last8), 0)),
                  pl.BlockSpec((1, d), lambda j, b, i: (0, 0)), mspec, mspec,
                  wspec(0), wspec(1), wspec(2), rspec(0), rspec(1), rspec(2),
                  cspec(0), cspec(1), cspec(2), rspec(0), rspec(1), rspec(2)],
        out_specs=[out_spec, out_spec],
        out_shape=[jax.ShapeDtypeStruct((bsz, seq, d), v_dtype), jax.ShapeDtypeStruct((bsz, seq, d), BF16)],
        compiler_params=_cparams("parallel", "parallel", "parallel"),
        name="hyena_in",
    )(x, x, x, row(g), shift, scale, w_in, w_in, w_in, row(b_in), row(b_in), row(b_in),
      conv_w, conv_w, conv_w, row(conv_b), row(conv_b), row(conv_b))


def _dft_tables(seq, blk):
    n = 2 * seq
    s = jnp.arange(seq, dtype=jnp.int32)[None, :]
    fl = jnp.arange(blk, dtype=jnp.int32)[:, None]
    fh = (jnp.arange(seq // blk, dtype=jnp.int32) * blk)[:, None]
    w = 2.0 * math.pi / n
    ang_b = ((fl * s) % n).astype(F32) * w
    ang_a = ((fh * s) % n).astype(F32) * w
    return (jnp.cos(ang_a)[:, None, :], jnp.sin(ang_a)[:, None, :], jnp.cos(ang_b), jnp.sin(ang_b))


def _dft_gen_body(ca_ref, sa_ref, cb_ref, sb_ref, c_ref, s_ref):
    ca, sa, cb, sb = ca_ref[...], sa_ref[...], cb_ref[...], sb_ref[...]
    c_ref[...] = (ca * cb - sa * sb).astype(BF16)
    s_ref[...] = (sa * cb + ca * sb).astype(BF16)


def dft_matrices(seq):
    blk = min(seq, 256)
    ca, sa, cb, sb = _dft_tables(seq, blk)
    aspec = pl.BlockSpec((None, 1, seq), lambda i: (i, 0, 0))
    bspec = pl.BlockSpec((blk, seq), lambda i: (0, 0))
    ospec = pl.BlockSpec((blk, seq), lambda i: (i, 0))
    return pl.pallas_call(
        _dft_gen_body,
        grid=(seq // blk,),
        in_specs=[aspec, aspec, bspec, bspec],
        out_specs=[ospec, ospec],
        out_shape=[jax.ShapeDtypeStruct((seq, seq), BF16)] * 2,
        compiler_params=_cparams("parallel"),
        name="dft_matrices",
    )(ca, sa, cb, sb)


def _alt_sign(rows):
    return jnp.where((rows & 1) == 0, 1.0, -1.0).astype(F32)


def _filt_body(h2_ref, wf_ref, wb_ref, dl_ref, a_ref, d_ref, ny_ref):
    seq = h2_ref.shape[0]
    h2 = h2_ref[...]
    row = lax.broadcasted_iota(jnp.int32, (seq, 1), 0)
    t = row.astype(F32) * (1.0 / (seq - 1))
    win = jnp.exp(-t * dl_ref[...])
    hf = jnp.dot(h2, wf_ref[...], precision=HIGHEST, preferred_element_type=F32) * win
    hb = jnp.dot(h2, wb_ref[...], precision=HIGHEST, preferred_element_type=F32) * win
    hb = jnp.where(row == 0, 0.0, hb)
    nrm = (jnp.sum(jnp.abs(hf), axis=0, keepdims=True) + jnp.sum(jnp.abs(hb), axis=0, keepdims=True))
    inv = 1.0 / nrm
    a = (hf + hb) * inv
    a_ref[...] = a.astype(a_ref.dtype)
    d_ref[...] = ((hb - hf) * inv).astype(d_ref.dtype)
    ny = jnp.sum(a * _alt_sign(row), axis=0, keepdims=True) * (1.0 / (2 * seq))
    ny_ref[...] = jnp.broadcast_to(ny, ny_ref.shape)


def _khat_body(a_ref, d_ref, c_ref, s_ref, kr_ref, ki_ref):
    i = pl.program_id(1)
    tm = c_ref.shape[0]
    seq = c_ref.shape[1]
    f = i * tm + lax.broadcasted_iota(jnp.int32, (tm, 1), 0)
    w = jnp.where(f == 0, 1.0, 2.0).astype(F32) * (1.0 / (2 * seq))
    kr_ref[...] = _dot(c_ref[...], a_ref[...]) * w
    ki_ref[...] = _dot(s_ref[...], d_ref[...]) * w


def hyena_filter_taps(seq, fw1, fb1, fw2, fb2, fw3, freq, taps_dtype):
    d = fw3.shape[1] // 2
    bands_n = (fw1.shape[0] - 1) // 2
    t = jnp.linspace(0.0, 1.0, seq, dtype=F32)[:, None]
    w = (2.0 * math.pi / seq) * jnp.arange(seq, dtype=F32)[:, None]
    bands = jnp.linspace(1e-4, bands_n - 1, bands_n, dtype=F32)[None, :]
    z = jnp.concatenate([t, jnp.cos(bands * w), -jnp.sin(bands * w)], axis=-1)
    h = jnp.sin(freq * (jnp.dot(z, fw1, precision=HIGHEST) + fb1))
    h2 = jnp.sin(freq * (jnp.dot(h, fw2, precision=HIGHEST) + fb2))
    max_decay = math.log(HY_DECAY_TARGET) / HY_FAST_PCT
    min_decay = math.log(HY_DECAY_TARGET) / HY_SLOW_PCT
    deltas = jnp.abs(jnp.linspace(min_decay, max_decay, d, dtype=F32))[None, :]

    order = h2.shape[1]
    tn = min(d, 256)
    nj = d // tn
    return pl.pallas_call(
        _filt_body,
        grid=(nj,),
        in_specs=[pl.BlockSpec((seq, order), lambda j: (0, 0)),
                  pl.BlockSpec((order, tn), lambda j: (0, j)),
                  pl.BlockSpec((order, tn), lambda j: (0, nj + j)),
                  pl.BlockSpec((1, tn), lambda j: (0, j))],
        out_specs=[pl.BlockSpec((seq, tn), lambda j: (0, j)),
                   pl.BlockSpec((seq, tn), lambda j: (0, j)),
                   pl.BlockSpec((SUBLANES, tn), lambda j: (0, j))],
        out_shape=[jax.ShapeDtypeStruct((seq, d), taps_dtype), jax.ShapeDtypeStruct((seq, d), taps_dtype),
                   jax.ShapeDtypeStruct((SUBLANES, d), F32)],
        compiler_params=_cparams("parallel"),
        name="hyena_filter_taps",
    )(h2, fw3, fw3, deltas)


def hyena_filter_dft(a, dd, cmat, smat):
    seq, d = a.shape
    tm = min(seq, 512)
    tn2 = min(d, 512)
    return pl.pallas_call(
        _khat_body,
        grid=(d // tn2, seq // tm),
        in_specs=[pl.BlockSpec((seq, tn2), lambda j, i: (0, j)),
                  pl.BlockSpec((seq, tn2), lambda j, i: (0, j)),
                  pl.BlockSpec((tm, seq), lambda j, i: (i, 0)),
                  pl.BlockSpec((tm, seq), lambda j, i: (i, 0))],
        out_specs=[pl.BlockSpec((tm, tn2), lambda j, i: (i, j))] * 2,
        out_shape=[jax.ShapeDtypeStruct((seq, d), F32)] * 2,
        compiler_params=_cparams("parallel", "parallel"),
        name="hyena_filter_dft",
    )(a, dd, cmat, smat)


def _dft_fwd_body(v_ref, c_ref, s_ref, kr_ref, ki_ref, kn_ref, ya_ref, yb_ref, yn_ref):
    i = pl.program_id(2)
    v = v_ref[...]
    vr = _dot(c_ref[...], v)
    p = _dot(s_ref[...], v)
    kr = kr_ref[...]
    ki = ki_ref[...]
    ya_ref[...] = (vr * kr + p * ki).astype(BF16)
    yb_ref[...] = (p * kr - vr * ki).astype(BF16)

    @pl.when(i == 0)
    def _():
        seq = v.shape[0]
        row = lax.broadcasted_iota(jnp.int32, (seq, 1), 0)
        vl = jnp.sum(v.astype(F32) * _alt_sign(row), axis=0, keepdims=True)
        yn_ref[...] = jnp.broadcast_to(vl * kn_ref[0:1, :], yn_ref.shape)


def _dft_inv_body(ya_ref, yb_ref, c_ref, s_ref, v_ref, x0_ref, skip_ref, yn_ref, o_ref):
    i = pl.program_id(2)
    tm = c_ref.shape[0]
    acc = _dot(c_ref[...], ya_ref[...]) + _dot(s_ref[...], yb_ref[...])
    t = i * tm + lax.broadcasted_iota(jnp.int32, (tm, 1), 0)
    y = acc + _alt_sign(t) * yn_ref[0:1, :] + skip_ref[...] * v_ref[...].astype(F32)
    o_ref[...] = (y * x0_ref[...].astype(F32)).astype(BF16)


def hyena_conv(v, x0, skip, kr, ki, kn, cmat, smat):
    bsz, seq, d = v.shape
    tm = min(seq, 512)
    tn = min(d, 512)
    grid = (bsz, d // tn, seq // tm)
    full = pl.BlockSpec((None, seq, tn), lambda b, j, i: (b, 0, j))
    mat = pl.BlockSpec((tm, seq), lambda b, j, i: (i, 0))
    tile = pl.BlockSpec((None, tm, tn), lambda b, j, i: (b, i, j))
    ktile = pl.BlockSpec((tm, tn), lambda b, j, i: (i, j))
    nyq = pl.BlockSpec((None, SUBLANES, tn), lambda b, j, i: (b, 0, j))
    ya, yb, yn = pl.pallas_call(
        _dft_fwd_body,
        grid=grid,
        in_specs=[full, mat, mat, ktile, ktile, pl.BlockSpec((SUBLANES, tn), lambda b, j, i: (0, j))],
        out_specs=[tile, tile, nyq],
        out_shape=[jax.ShapeDtypeStruct((bsz, seq, d), BF16)] * 2
        + [jax.ShapeDtypeStruct((bsz, SUBLANES, d), F32)],
        compiler_params=_cparams("parallel", "parallel", "arbitrary"),
        name="hyena_dft_fwd",
    )(v, cmat, smat, kr, ki, kn)
    return pl.pallas_call(
        _dft_inv_body,
        grid=grid,
        in_specs=[full, full, mat, mat, tile, tile, pl.BlockSpec((1, tn), lambda b, j, i: (0, j)), nyq],
        out_specs=tile,
        out_shape=jax.ShapeDtypeStruct((bsz, seq, d), BF16),
        compiler_params=_cparams("parallel", "parallel", "parallel"),
        name="hyena_dft_inv",
    )(ya, yb, cmat, smat, v, x0, skip.reshape(1, d), yn)


FFT_N2 = 256
FFT_MIN_SEQ = 1024


def _fft_matrices(seq):
    n = 2 * seq
    n2 = FFT_N2
    n1 = n // n2
    r8 = SUBLANES
    q = jnp.arange(n2 // r8, dtype=jnp.int32)[:, None, None, None]
    f1 = jnp.arange(n1, dtype=jnp.int32)[None, :, None, None]
    r = jnp.arange(r8, dtype=jnp.int32)[None, None, :, None]
    t1 = jnp.arange(n1 // 2, dtype=jnp.int32)[None, None, None, :]
    ang = ((f1 * (t1 * n2 + q * r8 + r)) % n).astype(F32) * (2.0 * math.pi / n)
    g = jnp.stack([jnp.cos(ang), -jnp.sin(ang)], axis=3)
    eye = jnp.eye(r8, dtype=F32)[None, None, :, None, None, :]
    ma = (g[..., None] * eye).reshape(n2 // r8, n1 * r8 * 2, (n1 // 2) * r8).astype(BF16)
    f2 = jnp.arange(n2, dtype=jnp.int32)[:, None]
    t2 = jnp.arange(n2, dtype=jnp.int32)[None, :]
    th = ((f2 * t2) % n2).astype(F32) * (2.0 * math.pi / n2)
    co, si = jnp.cos(th), jnp.sin(th)
    wc = jnp.stack([jnp.stack([co, si], axis=-1), jnp.stack([-si, co], axis=-1)], axis=0)
    wc = wc.reshape(2 * n2, 2 * n2).astype(BF16)
    return ma, jnp.swapaxes(ma, 1, 2), wc, wc.T


def _fft_stage_a(x_ref, ma_ref, s1):
    n1h, n_q, r8, tn = x_ref.shape
    n1 = s1.shape[0]

    def body(q, carry):
        x = x_ref[:, pl.ds(q, 1), :, :].reshape(n1h * r8, tn).astype(BF16)
        a = _dot(ma_ref[q], x).astype(BF16)
        s1[:, pl.ds(pl.multiple_of(q * 2 * r8, 2 * r8), 2 * r8), :] = a.reshape(n1, 2 * r8, tn)
        return carry

    lax.fori_loop(0, n_q, body, 0)


def _fft_conv_body(v_ref, x0_ref, k_ref, skip_ref, ma_ref, mat_ref, wc_ref, wci_ref, o_ref, s1, ysc):
    n1h, n_q, r8, tn = v_ref.shape
    n1 = s1.shape[0]
    n2 = s1.shape[1] // 2
    seq = n1h * n_q * r8
    _fft_stage_a(v_ref, ma_ref, s1)

    def slab(f, carry):
        y = _dot(wc_ref[...], s1[f])
        yr, yi = y[:n2], y[n2:]
        kr = k_ref[f, 0].astype(F32)
        ki = k_ref[f, 1].astype(F32)
        p = jnp.concatenate([yr * kr - yi * ki, yr * ki + yi * kr], axis=0).astype(BF16)
        s1[f] = _dot(wci_ref[...], p).astype(BF16)
        return carry

    lax.fori_loop(0, n1, slab, 0)

    def inv_a(q, carry):
        z = s1[:, pl.ds(pl.multiple_of(q * 2 * r8, 2 * r8), 2 * r8), :].reshape(n1 * 2 * r8, tn)
        ysc[:, pl.ds(q, 1), :, :] = _dot(mat_ref[q], z).reshape(n1h, 1, r8, tn)
        return carry

    lax.fori_loop(0, n_q, inv_a, 0)
    y = ysc[...].reshape(seq, tn) + skip_ref[...] * v_ref[...].reshape(seq, tn)
    o_ref[...] = (y * x0_ref[...].astype(F32)).astype(BF16)


def _fft_filter_body(a_ref, d_ref, ma_ref, wc_ref, k_ref, s1):
    n1 = s1.shape[0]
    n2 = s1.shape[1] // 2
    scale = 1.0 / (n1 * n2)
    for src_ref, part, sign in ((a_ref, 0, scale), (d_ref, 1, -scale)):
        _fft_stage_a(src_ref, ma_ref, s1)

        def slab(f, carry):
            y = _dot(wc_ref[part * n2:(part + 1) * n2, :], s1[f])
            k_ref[f, part] = (y * sign).astype(BF16)
            return carry

        lax.fori_loop(0, n1, slab, 0)


def hyena_conv_fft(v, x0, skip, a, dd):
    bsz, seq, d = v.shape
    n2 = FFT_N2
    n1 = 2 * seq // n2
    n_q = n2 // SUBLANES
    tn = min(d, 256)
    ma, mat, wc, wci = _fft_matrices(seq)
    const = lambda shape: pl.BlockSpec(shape, lambda *_: (0,) * len(shape), pipeline_mode=pl.Buffered(1))
    view = lambda t: t.reshape(t.shape[:-2] + (n1 // 2, n_q, SUBLANES, d))
    tap = pl.BlockSpec((n1 // 2, n_q, SUBLANES, tn), lambda j: (0, 0, 0, j))
    khat = pl.pallas_call(
        _fft_filter_body,
        grid=(d // tn,),
        in_specs=[tap, tap, const(ma.shape), const(wc.shape)],
        out_specs=pl.BlockSpec((n1, 2, n2, tn), lambda j: (0, 0, 0, j)),
        out_shape=jax.ShapeDtypeStruct((n1, 2, n2, d), BF16),
        scratch_shapes=[pltpu.VMEM((n1, 2 * n2, tn), BF16)],
        compiler_params=_cparams("parallel"),
        name="hyena_filter_fft",
    )(view(a), view(dd), ma, wc)
    return pl.pallas_call(
        _fft_conv_body,
        grid=(d // tn, bsz),
        in_specs=[pl.BlockSpec((None, n1 // 2, n_q, SUBLANES, tn), lambda j, b: (b, 0, 0, 0, j)),
                  pl.BlockSpec((None, seq, tn), lambda j, b: (b, 0, j)),
                  pl.BlockSpec((n1, 2, n2, tn), lambda j, b: (0, 0, 0, j), pipeline_mode=pl.Buffered(1)),
                  pl.BlockSpec((1, tn), lambda j, b: (0, j)),
                  const(ma.shape), const(mat.shape), const(wc.shape), const(wci.shape)],
        out_specs=pl.BlockSpec((None, seq, tn), lambda j, b: (b, 0, j)),
        out_shape=jax.ShapeDtypeStruct((bsz, seq, d), BF16),
        scratch_shapes=[pltpu.VMEM((n1, 2 * n2, tn), BF16), pltpu.VMEM((n1 // 2, n_q, SUBLANES, tn), F32)],
        compiler_params=_cparams("parallel", "arbitrary"),
        name="hyena_conv_fft",
    )(view(v), x0, khat, skip.reshape(1, d), ma, mat, wc, wci)


def _mm_res_body(x_ref, w_ref, b_ref, res_ref, gate_ref, o_ref):
    o_ref[...] = res_ref[...] + gate_ref[...] * (_dot(x_ref[...], w_ref[...]) + b_ref[...])


def mm_residual(x, w, b, res, gate):
    bsz, seq, k = x.shape
    n = w.shape[1]
    tm = min(seq, 512)
    return pl.pallas_call(
        _mm_res_body,
        grid=(bsz, seq // tm),
        in_specs=[pl.BlockSpec((None, tm, k), lambda b, i: (b, i, 0)),
                  pl.BlockSpec((k, n), lambda b, i: (0, 0)),
                  pl.BlockSpec((1, n), lambda b, i: (0, 0)),
                  pl.BlockSpec((None, tm, n), lambda b, i: (b, i, 0)),
                  pl.BlockSpec((None, 1, n), lambda b, i: (b, 0, 0))],
        out_specs=pl.BlockSpec((None, tm, n), lambda b, i: (b, i, 0)),
        out_shape=jax.ShapeDtypeStruct((bsz, seq, n), F32),
        compiler_params=_cparams("parallel", "parallel"),
        name="mm_residual",
    )(x, w, b.reshape(1, n), res, gate)


def _moe_pre_body(*refs, n_groups, n_experts, tile_offs):
    n_streams = len(tile_offs) - 1
    g_ref, wr_ref, br_ref, tok_ref, eid_ref, gate_ref = refs[3 * n_streams:]
    i = pl.program_id(0)
    for k in range(n_streams):
        x_ref, sh_ref, sc_ref = refs[3 * k:3 * k + 3]

        @pl.when(jnp.logical_and(i >= tile_offs[k], i < tile_offs[k + 1]))
        def _():
            tok = _norm_mod(x_ref[...], g_ref[...], sh_ref[...], sc_ref[...])
            _route_tokens(tok, wr_ref, br_ref, tok_ref, eid_ref, gate_ref, n_groups, n_experts)


def _route_tokens(tok, wr_ref, br_ref, tok_ref, eid_ref, gate_ref, n_groups, n_experts):
    tok_ref[...] = tok
    logits = jnp.dot(tok, wr_ref[...], precision=HIGHEST, preferred_element_type=F32) + br_ref[...]
    lane = lax.broadcasted_iota(jnp.int32, logits.shape, 1)
    per = n_experts // n_groups
    big = jnp.int32(1 << 20)
    gmask = jnp.logical_and(lane >= n_experts, lane < n_experts + n_groups)
    gl = jnp.where(gmask, logits, NEG_BIG)
    gmax = jnp.max(gl, axis=-1, keepdims=True)
    gidx = jnp.min(jnp.where(gl == gmax, lane - n_experts, big), axis=-1, keepdims=True)
    p_top = 1.0 / jnp.sum(jnp.where(gmask, jnp.exp(gl - gmax), 0.0), axis=-1, keepdims=True)
    lo = gidx * per
    emask = jnp.logical_and(lane >= lo, lane < lo + per)
    el = jnp.where(emask, logits, NEG_BIG)
    m1 = jnp.max(el, axis=-1, keepdims=True)
    i1 = jnp.min(jnp.where(el == m1, lane, big), axis=-1, keepdims=True)
    el2 = jnp.where(lane == i1, NEG_BIG, el)
    m2 = jnp.max(el2, axis=-1, keepdims=True)
    i2 = jnp.min(jnp.where(el2 == m2, lane, big), axis=-1, keepdims=True)
    e21 = jnp.exp(m2 - m1)
    g1 = p_top / (1.0 + e21)
    g2 = g1 * e21
    eid_ref[...] = jnp.where(lane == 0, i1, jnp.where(lane == 1, i2, 0))
    gate_ref[...] = jnp.where(lane == 0, g1, jnp.where(lane == 1, g2, 0.0))


def moe_pre(streams, g, wr, br, n_groups, n_experts):
    d = streams[0][0].shape[2]
    tm = MOE_TM
    tile_offs = [0]
    in_specs, args = [], []
    for x, shift, scale in streams:
        bsz, seq, _ = x.shape
        nt = seq // tm
        n_tiles = bsz * nt
        off = tile_offs[-1]
        tile_offs.append(off + n_tiles)

        def tile(i, off=off, n_tiles=n_tiles):
            return jnp.clip(i - off, 0, n_tiles - 1)

        in_specs += [pl.BlockSpec((None, tm, d), lambda i, tile=tile, nt=nt: (tile(i) // nt, tile(i) % nt, 0)),
                     pl.BlockSpec((None, 1, d), lambda i, tile=tile, nt=nt: (tile(i) // nt, 0, 0)),
                     pl.BlockSpec((None, 1, d), lambda i, tile=tile, nt=nt: (tile(i) // nt, 0, 0))]
        args += [x, shift, scale]
    in_specs += [pl.BlockSpec((1, d), lambda i: (0, 0)),
                 pl.BlockSpec((d, LANES), lambda i: (0, 0)),
                 pl.BlockSpec((1, LANES), lambda i: (0, 0))]
    args += [g.reshape(1, d), wr, br]
    total = tile_offs[-1] * tm
    rout = pl.BlockSpec((tm, LANES), lambda i: (i, 0))
    tok, eid, gate = pl.pallas_call(
        functools.partial(_moe_pre_body, n_groups=n_groups, n_experts=n_experts, tile_offs=tuple(tile_offs)),
        grid=(tile_offs[-1],),
        in_specs=in_specs,
        out_specs=[pl.BlockSpec((tm, d), lambda i: (i, 0)), rout, rout],
        out_shape=[jax.ShapeDtypeStruct((total, d), F32),
                   jax.ShapeDtypeStruct((total, LANES), jnp.int32),
                   jax.ShapeDtypeStruct((total, LANES), F32)],
        compiler_params=_cparams("parallel"),
        name="moe_pre",
    )(*args)
    return tok, eid, gate, tile_offs[:-1]


def _start_row_gather(idx_ref, n_rows, stride, offset, src_hbm, dst_vmem, sem):
    def body(r, c):
        pltpu.make_async_copy(src_hbm.at[pl.ds(idx_ref[stride * r + offset], 1)],
                              dst_vmem.at[pl.ds(r, 1)], sem).start()
        return c

    lax.fori_loop(0, n_rows, body, 0, unroll=8)


def _wait_row_gather(n_rows, src_hbm, dst_vmem, sem):
    pltpu.make_async_copy(src_hbm.at[pl.ds(0, n_rows)], dst_vmem, sem).wait()


def _expert_body(bv_ref, rk_ref, pe_ref, tot_ref, src_ref, nxt_ref, tok_ref, wg_hbm, wu_hbm, wd_hbm, o_ref,
                 xbuf, xsem, wg_v, wu_v, wd_v, stg, wsem, cnt, *, layer):
    i = pl.program_id(0)
    n = pl.num_programs(0)
    slot = i % 2
    cr, cc = stg.shape[1:]
    total = tot_ref[0]
    mats_hbm = (wg_hbm, wu_hbm, wd_hbm)
    mats_vmem = (wg_v, wu_v, wd_v)

    @pl.when(i == 0)
    def _():
        cnt[0] = 0
        cnt[1] = 0

    @pl.when(jnp.logical_and(i == 0, bv_ref[0] > 0))
    def _():
        _start_row_gather(src_ref, MOE_BM, 1, 0, tok_ref, xbuf.at[0], xsem.at[0])

    @pl.when(jnp.logical_and(i + 1 < n, bv_ref[jnp.minimum(i + 1, n - 1)] > 0))
    def _():
        _start_row_gather(nxt_ref, MOE_BM, 1, 0, tok_ref, xbuf.at[1 - slot], xsem.at[1 - slot])

    def chunk_geom(c):
        q = c % MOE_NCH
        m = q // 4
        sub = q % 4
        r0 = jnp.where(m < 2, sub, sub // 2) * cr
        c0 = jnp.where(m < 2, 0, sub % 2) * cc
        return m, pl.multiple_of(r0, cr), pl.multiple_of(c0, cc)

    def issue(c):
        e = pe_ref[c // MOE_NCH]
        m, r0, c0 = chunk_geom(c)
        s = c % MOE_STAGE
        for k, w_hbm in enumerate(mats_hbm):
            @pl.when(m == k)
            def _():
                pltpu.make_async_copy(w_hbm.at[layer, e, pl.ds(r0, cr), pl.ds(c0, cc)], stg.at[s],
                                      wsem.at[s]).start()

    def cast(c):
        m, r0, c0 = chunk_geom(c)
        s = c % MOE_STAGE
        ws = (c // MOE_NCH) % 2
        pltpu.make_async_copy(wg_hbm.at[layer, 0, pl.ds(0, cr), pl.ds(0, cc)], stg.at[s], wsem.at[s]).wait()
        val = stg[s].astype(BF16)
        for k, w_v in enumerate(mats_vmem):
            @pl.when(m == k)
            def _():
                w_v[ws, pl.ds(r0, cr), pl.ds(c0, cc)] = val

    valid = bv_ref[i] > 0
    rank = rk_ref[i]
    issued = cnt[0]
    done = cnt[1]
    limit = jnp.minimum(total, MOE_NCH * (rank + 2))
    need = jnp.where(valid, MOE_NCH * (rank + 1), done)

    def fill(issued, done):
        hi = jnp.minimum(limit, done + MOE_STAGE)

        def body(c, carry):
            issue(c)
            return carry

        lax.fori_loop(issued, hi, body, 0)
        return jnp.maximum(issued, hi)

    def cast_and_refill(c, issued):
        cast(c)
        more = issued < jnp.minimum(limit, c + 1 + MOE_STAGE)

        @pl.when(more)
        def _():
            issue(issued)

        return issued + more.astype(jnp.int32)

    issued = fill(issued, done)
    issued = lax.fori_loop(done, need, cast_and_refill, issued)
    done = jnp.maximum(done, need)

    @pl.when(valid)
    def _():
        ws = rank % 2
        _wait_row_gather(MOE_BM, tok_ref, xbuf.at[slot], xsem.at[slot])
        x = xbuf[slot].astype(BF16)
        gate = _dot(x, wg_v[ws])
        up = _dot(x, wu_v[ws])
        h = (gate * jax.nn.sigmoid(gate) * up).astype(BF16)
        o_ref[...] = _dot(h, wd_v[ws])

    @pl.when(jnp.logical_not(valid))
    def _():
        o_ref[...] = jnp.zeros_like(o_ref)

    fetched = issued
    issued = lax.fori_loop(done, fetched, cast_and_refill, issued)
    done = jnp.maximum(done, fetched)
    last = i == n - 1
    tail = jnp.where(last, issued, done)

    def drain(c, carry):
        cast(c)
        return carry

    lax.fori_loop(done, tail, drain, 0)
    cnt[0] = issued
    cnt[1] = jnp.maximum(done, tail)


def moe_experts(tok, src_tok, block_valid, block_rank, present, n_chunks, w_gate, w_up, w_down, layer):
    d = tok.shape[1]
    n_rows = src_tok.shape[0]
    n_blocks = n_rows // MOE_BM
    dh = w_gate.shape[3]
    assert 2 * dh == d and MOE_NCH == 12
    cr, cc = d // 4, dh
    any_spec = pl.BlockSpec(memory_space=pl.ANY)
    grid_spec = pltpu.PrefetchScalarGridSpec(
        num_scalar_prefetch=4,
        grid=(n_blocks,),
        in_specs=[pl.BlockSpec((MOE_BM,), lambda i, *_: (i,), memory_space=pltpu.SMEM),
                  pl.BlockSpec((MOE_BM,), lambda i, *_: (jnp.minimum(i + 1, n_blocks - 1),),
                               memory_space=pltpu.SMEM),
                  any_spec, any_spec, any_spec, any_spec],
        out_specs=pl.BlockSpec((MOE_BM, d), lambda i, *_: (i, 0)),
        scratch_shapes=[pltpu.VMEM((2, MOE_BM, d), F32), pltpu.SemaphoreType.DMA((2,)),
                        pltpu.VMEM((2, d, dh), BF16), pltpu.VMEM((2, d, dh), BF16), pltpu.VMEM((2, dh, d), BF16),
                        pltpu.VMEM((MOE_STAGE, cr, cc), F32), pltpu.SemaphoreType.DMA((MOE_STAGE,)),
                        pltpu.SMEM((2,), jnp.int32)],
    )
    return pl.pallas_call(
        functools.partial(_expert_body, layer=layer),
        grid_spec=grid_spec,
        out_shape=jax.ShapeDtypeStruct((n_rows, d), F32),
        compiler_params=_cparams("arbitrary"),
        name="moe_experts",
    )(block_valid, block_rank, present, n_chunks, src_tok, src_tok, tok, w_gate, w_up, w_down)


def _combine_body(dest_ref, nxt_ref, os_ref, gate_ref, res_ref, gt_ref, fg_ref, o_ref, buf, sem, *, final_norm):
    rows = res_ref.shape[0]
    i = pl.program_id(0)
    n = pl.num_programs(0)
    slot = i % 2

    def start(idx_ref, s):
        for k in range(TOP_K):
            _start_row_gather(idx_ref, rows, TOP_K, k, os_ref, buf.at[s, k], sem.at[s])

    @pl.when(i == 0)
    def _():
        start(dest_ref, 0)

    @pl.when(i + 1 < n)
    def _():
        start(nxt_ref, 1 - slot)

    for k in range(TOP_K):
        _wait_row_gather(rows, os_ref, buf.at[slot, k], sem.at[slot])
    gates = gate_ref[...]
    mo = gates[:, 0:1] * buf[slot, 0] + gates[:, 1:2] * buf[slot, 1]
    y = res_ref[...] + gt_ref[...] * mo
    if final_norm:
        ms = jnp.mean(y * y, axis=-1, keepdims=True)
        y = y * lax.rsqrt(ms + NORM_EPS) * fg_ref[...]
    o_ref[...] = y


def moe_combine(os, dest, gates, tile0, res, gt, final_g, final_norm):
    bsz, seq, d = res.shape
    rows = MOE_TM
    nt = seq // rows
    n = bsz * nt
    tile = pl.BlockSpec((None, rows, d), lambda i: (i // nt, i % nt, 0))
    return pl.pallas_call(
        functools.partial(_combine_body, final_norm=final_norm),
        grid=(n,),
        in_specs=[pl.BlockSpec((rows * TOP_K,), lambda i: (tile0 + i,), memory_space=pltpu.SMEM),
                  pl.BlockSpec((rows * TOP_K,), lambda i: (tile0 + jnp.minimum(i + 1, n - 1),),
                               memory_space=pltpu.SMEM),
                  pl.BlockSpec(memory_space=pl.ANY),
                  pl.BlockSpec((rows, LANES), lambda i: (tile0 + i, 0)),
                  tile,
                  pl.BlockSpec((None, 1, d), lambda i: (i // nt, 0, 0)),
                  pl.BlockSpec((1, d), lambda i: (0, 0))],
        out_specs=tile,
        out_shape=jax.ShapeDtypeStruct((bsz, seq, d), F32),
        scratch_shapes=[pltpu.VMEM((2, TOP_K, rows, d), F32), pltpu.SemaphoreType.DMA((2,))],
        compiler_params=_cparams("arbitrary"),
        name="moe_combine",
    )(dest, dest, os, gates, res, gt, final_g.reshape(1, d))


def _route_plan(eid, n_experts):
    e_flat = eid.reshape(-1)
    a = e_flat.shape[0]
    order = jnp.argsort(e_flat)
    pos = jnp.argsort(order)
    experts = jnp.arange(n_experts, dtype=jnp.int32)
    counts = jnp.sum((e_flat[:, None] == experts[None, :]).astype(jnp.int32), axis=0)
    start = jnp.cumsum(counts) - counts
    padded = (counts + MOE_BM - 1) // MOE_BM * MOE_BM
    pad_end = jnp.cumsum(padded)
    pad_start = pad_end - padded
    dest = pad_start[e_flat] + pos - start[e_flat]
    n_blocks = -(-a // MOE_BM) + n_experts
    starts = jnp.arange(n_blocks, dtype=jnp.int32) * MOE_BM
    block_expert = jnp.minimum(jnp.sum((pad_end[None, :] <= starts[:, None]).astype(jnp.int32), axis=1),
                               n_experts - 1)
    block_valid = (starts < pad_end[-1]).astype(jnp.int32)
    rows = jnp.arange(n_blocks * MOE_BM, dtype=jnp.int32)
    e_row = jnp.repeat(block_expert, MOE_BM)
    off = rows - pad_start[e_row]
    live = jnp.logical_and(off < counts[e_row], jnp.repeat(block_valid, MOE_BM) > 0)
    src_tok = jnp.where(live, order[jnp.clip(start[e_row] + off, 0, a - 1)] // TOP_K, 0)
    has = (counts > 0).astype(jnp.int32)
    block_rank = (jnp.cumsum(has) - 1)[block_expert]
    present = jnp.argsort(1 - has)
    n_chunks = (MOE_NCH * jnp.sum(has)).reshape(1)
    i32 = lambda v: v.astype(jnp.int32)
    return i32(dest), i32(src_tok), block_valid, i32(block_rank), i32(present), i32(n_chunks)


def hier_moe(streams, norm_g, wg, bg, we, be, w_gate, w_up, w_down, layer, final_g, final_norm):
    n_groups = wg.shape[1]
    n_experts = we.shape[1]
    d = wg.shape[0]
    wr = jnp.zeros((d, LANES), F32).at[:, :n_experts].set(we).at[:, n_experts:n_experts + n_groups].set(wg)
    br = jnp.zeros((1, LANES), F32).at[0, :n_experts].set(be).at[0, n_experts:n_experts + n_groups].set(bg)
    tok, eid, gates, tile0s = moe_pre([s[:3] for s in streams], norm_g, wr, br, n_groups, n_experts)
    dest, src_tok, block_valid, block_rank, present, n_chunks = _route_plan(eid[:, :TOP_K], n_experts)
    os = moe_experts(tok, src_tok, block_valid, block_rank, present, n_chunks, w_gate, w_up, w_down, layer)
    return [moe_combine(os, dest, gates, tile0, x, gt, final_g, final_norm)
            for (x, _, _, gt), tile0 in zip(streams, tile0s)]


def _norm_mod_body(x_ref, g_ref, sh_ref, sc_ref, o_ref):
    o_ref[...] = _norm_mod(x_ref[...], g_ref[...], sh_ref[...], sc_ref[...]).astype(o_ref.dtype)


def norm_mod(x, g, shift, scale):
    bsz, seq, d = x.shape
    tm = min(seq, 512)
    mspec = pl.BlockSpec((None, 1, d), lambda b, i: (b, 0, 0))
    return pl.pallas_call(
        _norm_mod_body,
        grid=(bsz, seq // tm),
        in_specs=[pl.BlockSpec((None, tm, d), lambda b, i: (b, i, 0)),
                  pl.BlockSpec((1, d), lambda b, i: (0, 0)), mspec, mspec],
        out_specs=pl.BlockSpec((None, tm, d), lambda b, i: (b, i, 0)),
        out_shape=jax.ShapeDtypeStruct((bsz, seq, d), BF16),
        compiler_params=_cparams("parallel", "parallel"),
        name="norm_mod",
    )(x, g.reshape(1, d), shift, scale)


def _s5_arrange(h):
    bsz, t, d = h.shape
    c = t // (S5_SEGS * S5_TAU)
    h = h.reshape(bsz, S5_SEGS, c, S5_TAU, d // LANES, LANES)
    return h.transpose(2, 0, 1, 4, 3, 5).reshape(c * bsz * S5_SEGS, d * S5_TAU)


def _s5_unarrange(y, bsz):
    r, w = y.shape
    d = w // S5_TAU
    c = r // (bsz * S5_SEGS)
    y = y.reshape(c, bsz, S5_SEGS, d // LANES, S5_TAU, LANES)
    return y.transpose(1, 2, 0, 4, 3, 5).reshape(bsz, S5_SEGS * c * S5_TAU, d)


def _s5_operators(a_re, a_im, log_step, b_re, b_im, c_re, c_im):
    n_g, n_p = a_re.shape[1:]
    n_h = b_re.shape[-1]
    gpt = LANES // n_h
    n_j = n_g // gpt
    tau = S5_TAU
    assert tau * n_h == LANES and 2 * n_p == LANES
    lam_step = lax.complex(a_re, a_im) * jnp.exp(log_step)[..., None]
    lam_bar = jnp.exp(lam_step)
    b_bar = ((lam_bar - 1.0) / lax.complex(a_re, a_im))[..., None] * lax.complex(b_re, b_im)
    c_mat = lax.complex(c_re, c_im)
    ks = jnp.arange(tau + 1, dtype=F32)[None, :, None, None]
    pw = jnp.exp(lam_step[:, None] * ks)
    ein = functools.partial(jnp.einsum, precision=HIGHEST)
    inj_c, cl_c, lt = [], [], []
    tz_c = 0.0
    for d in range(2):
        pos = jnp.arange(tau) if d == 0 else jnp.arange(tau)[::-1]
        inj = (pw[d][tau - 1 - pos][..., None] * b_bar[d][None]).reshape(tau, n_j, gpt, n_p, n_h)
        inj = inj.transpose(1, 0, 2, 4, 3).reshape(n_j, tau * LANES, n_p)
        inj_c.append(jnp.concatenate([inj.real, inj.imag], axis=-1))
        cl = (c_mat[d][None] * pw[d][pos + 1][:, :, None, :]).reshape(tau, n_j, gpt, n_h, n_p)
        cl = cl.transpose(1, 2, 4, 0, 3).reshape(n_j, gpt * n_p, tau * n_h)
        cl_c.append(jnp.concatenate([cl.real, -cl.imag], axis=1))
        mk = ein('gop,kgp,gph->kgoh', c_mat[d], pw[d][:tau], b_bar[d]).real
        diff = pos[:, None] - pos[None, :]
        tz = jnp.where((diff >= 0)[:, :, None, None, None], mk[jnp.clip(diff, 0, tau - 1)], 0.0)
        tz = tz.reshape(tau, tau, n_j, gpt, n_h, n_h)
        tz_c = tz_c + tz.transpose(2, 1, 3, 5, 0, 4).reshape(n_j, tau * LANES, tau * n_h)
        lt_d = pw[d][tau].reshape(n_j, 1, gpt * n_p)
        lt.append(jnp.concatenate([lt_d.real, lt_d.imag], axis=-1))
    ws, wu, wh = s5_expand(jnp.stack(inj_c).astype(BF16), jnp.stack(cl_c).astype(BF16), tz_c.astype(BF16),
                           n_h, n_p)
    return ws, wu, wh, jnp.stack(lt).astype(F32)


def _s5_expand_body(inj_ref, cl_ref, tz_ref, ws_ref, wu_ref, wh_ref, *, n_h, n_p):
    rows = tz_ref.shape[0]
    gpt = LANES // n_h
    row = lax.broadcasted_iota(jnp.int32, (rows, LANES), 0)
    lane = lax.broadcasted_iota(jnp.int32, (rows, LANES), 1)
    sel_r = lax.broadcasted_iota(jnp.int32, (LANES, LANES), 0)
    sel_l = lax.broadcasted_iota(jnp.int32, (LANES, LANES), 1)
    grp_in = (row // n_h) % gpt
    grp_st = (row // n_p) % gpt

    def spread_out(m, t, grp_row):
        sel = jnp.logical_and(sel_r // n_h == t, sel_r % n_h == sel_l % n_h).astype(BF16)
        return jnp.where(grp_row == lane // n_h, _dot(m, sel), 0.0).astype(BF16)

    def spread_state(m, c, q, grp_row):
        sel = jnp.logical_and(sel_r // n_p == c, sel_r % n_p == sel_l % n_p).astype(BF16)
        return jnp.where(grp_row == (LANES // n_p) * q + lane // n_p, _dot(m, sel), 0.0).astype(BF16)

    w2 = 2 * gpt * n_p
    tz = tz_ref[...]
    for t in range(S5_TAU):
        wu_ref[:, t * LANES:(t + 1) * LANES] = spread_out(tz, t, grp_in)
    for d in range(2):
        cl = cl_ref[d]
        inj = inj_ref[d]
        for t in range(S5_TAU):
            wh_ref[d, :, t * LANES:(t + 1) * LANES] = spread_out(cl, t, grp_st)
        for c in range(2):
            for q in range(gpt * n_p // LANES):
                lo = d * w2 + c * gpt * n_p + q * LANES
                ws_ref[:, lo:lo + LANES] = spread_state(inj, c, q, grp_in)


def s5_expand(inj_c, cl_c, tz_c, n_h, n_p):
    n_j, rows, _ = tz_c.shape
    gpt = LANES // n_h
    w2 = 2 * gpt * n_p
    assert rows == S5_TAU * LANES == w2
    cspec = pl.BlockSpec((2, None, rows, LANES), lambda j: (0, j, 0, 0))
    return pl.pallas_call(
        functools.partial(_s5_expand_body, n_h=n_h, n_p=n_p),
        grid=(n_j,),
        in_specs=[cspec, cspec, pl.BlockSpec((None, rows, LANES), lambda j: (j, 0, 0))],
        out_specs=[pl.BlockSpec((None, rows, 2 * w2), lambda j: (j, 0, 0)),
                   pl.BlockSpec((None, rows, rows), lambda j: (j, 0, 0)),
                   pl.BlockSpec((2, None, w2, rows), lambda j: (0, j, 0, 0))],
        out_shape=[jax.ShapeDtypeStruct((n_j, rows, 2 * w2), BF16),
                   jax.ShapeDtypeStruct((n_j, rows, rows), BF16),
                   jax.ShapeDtypeStruct((2, n_j, w2, rows), BF16)],
        compiler_params=_cparams("parallel"),
        name="s5_expand",
    )(inj_c, cl_c, tz_c)


def _s5_inj_body(x_ref, w_ref, o_ref):
    o_ref[...] = _dot(x_ref[...], w_ref[...])


def s5_inject(xr, ws):
    r = xr.shape[0]
    n_j, k, n = ws.shape
    tm = r // 2 if r % 32 == 0 else r
    return pl.pallas_call(
        _s5_inj_body,
        grid=(n_j, r // tm),
        in_specs=[pl.BlockSpec((tm, k), lambda j, i: (i, j)),
                  pl.BlockSpec((None, k, n), lambda j, i: (j, 0, 0))],
        out_specs=pl.BlockSpec((tm, n), lambda j, i: (i, j)),
        out_shape=jax.ShapeDtypeStruct((r, n_j * n), F32),
        compiler_params=_cparams("parallel", "parallel"),
        name="s5_inject",
    )(xr, ws)


def _cmul(ar, ai, br, bi):
    return ar * br - ai * bi, ar * bi + ai * br


def _s5_scan_body(s_ref, lt_ref, h_ref, raw_ref, *, n_ctx, n_lat, bsz):
    d = pl.program_id(1)
    w2 = lt_ref.shape[-1]
    w = w2 // 2
    rows = bsz * S5_SEGS
    seg = lax.broadcasted_iota(jnp.int32, (rows, 1), 0) % S5_SEGS
    is_late = seg != d
    lam_r = lt_ref[:, 0:w]
    lam_i = lt_ref[:, w:w2]
    zero = jnp.zeros((rows, w), F32)
    one = (jnp.ones((1, w), F32), jnp.zeros((1, w), F32))

    def swap_segments(x):
        return jnp.where(seg == 0, pltpu.roll(x, rows - 1, axis=0), pltpu.roll(x, 1, axis=0))

    def phase(c0, n_steps, hin_r, hin_i, write):
        def chunk(k):
            return c0 + jnp.where(d == 0, k, n_steps - 1 - k)

        def step_raw(k, carry):
            hr, hi = carry
            c = chunk(k)
            raw_ref[c - c0, :, 0:w] = hr
            raw_ref[c - c0, :, w:w2] = hi
            nr, ni = _cmul(lam_r, lam_i, hr, hi)
            return nr + s_ref[c, :, 0:w], ni + s_ref[c, :, w:w2]

        er, ei = lax.fori_loop(0, n_steps, step_raw, (zero, zero))
        pr, pi = lax.fori_loop(0, n_steps, lambda k, q: _cmul(lam_r, lam_i, *q), one)
        dr, di = _cmul(pr, pi, hin_r, hin_i)
        first_r = jnp.where(is_late, 0.0, er + dr)
        first_i = jnp.where(is_late, 0.0, ei + di)
        carry_r = jnp.where(is_late, swap_segments(first_r), hin_r)
        carry_i = jnp.where(is_late, swap_segments(first_i), hin_i)
        if write:
            def step_fix(k, q):
                c = chunk(k)
                fr, fi = _cmul(q[0], q[1], carry_r, carry_i)
                h_ref[c - c0, :, 0:w] = (raw_ref[c - c0, :, 0:w] + fr).astype(h_ref.dtype)
                h_ref[c - c0, :, w:w2] = (raw_ref[c - c0, :, w:w2] + fi).astype(h_ref.dtype)
                return _cmul(lam_r, lam_i, q[0], q[1])

            lax.fori_loop(0, n_steps, step_fix, one)
        lr, li = _cmul(pr, pi, carry_r, carry_i)
        last_r = jnp.where(is_late, er + lr, 0.0)
        last_i = jnp.where(is_late, ei + li, 0.0)
        return (jnp.where(is_late, 0.0, swap_segments(last_r)), jnp.where(is_late, 0.0, swap_segments(last_i)))

    hr, hi = zero, zero
    if n_ctx:
        hr, hi = phase(0, n_ctx, hr, hi, False)
    phase(n_ctx, n_lat, hr, hi, True)


def s5_scan(s, lt, n_ctx, n_lat, bsz):
    assert S5_SEGS == 2
    n_c, rows, _ = s.shape
    n_j = lt.shape[1]
    w2 = lt.shape[-1]
    return pl.pallas_call(
        functools.partial(_s5_scan_body, n_ctx=n_ctx, n_lat=n_lat, bsz=bsz),
        grid=(n_j, 2),
        in_specs=[pl.BlockSpec((n_c, rows, w2), lambda j, d: (0, 0, 2 * j + d)),
                  pl.BlockSpec((None, None, 1, w2), lambda j, d: (d, j, 0, 0))],
        out_specs=pl.BlockSpec((None, n_lat, rows, w2), lambda j, d: (d, 0, 0, j)),
        out_shape=jax.ShapeDtypeStruct((2, n_lat, rows, n_j * w2), BF16),
        scratch_shapes=[pltpu.VMEM((max(n_ctx, n_lat), rows, w2), F32)],
        compiler_params=_cparams("parallel", "parallel"),
        name="s5_scan",
    )(s, lt)


def _s5_out_body(x_ref, hf_ref, hb_ref, wu_ref, whf_ref, whb_ref, o_ref):
    o_ref[...] = (_dot(x_ref[...], wu_ref[...]) + _dot(hf_ref[...], whf_ref[...])
                  + _dot(hb_ref[...], whb_ref[...]))


def s5_readout(xr, h, wu, wh):
    r = xr.shape[0]
    n_j, k, n = wu.shape
    w2 = wh.shape[2]
    tm = min(r, 1024)
    return pl.pallas_call(
        _s5_out_body,
        grid=(n_j, r // tm),
        in_specs=[pl.BlockSpec((tm, k), lambda j, i: (i, j)),
                  pl.BlockSpec((None, tm, w2), lambda j, i: (0, i, j)),
                  pl.BlockSpec((None, tm, w2), lambda j, i: (1, i, j)),
                  pl.BlockSpec((None, k, n), lambda j, i: (j, 0, 0)),
                  pl.BlockSpec((None, None, w2, n), lambda j, i: (0, j, 0, 0)),
                  pl.BlockSpec((None, None, w2, n), lambda j, i: (1, j, 0, 0))],
        out_specs=pl.BlockSpec((tm, n), lambda j, i: (i, j)),
        out_shape=jax.ShapeDtypeStruct((r, n_j * n), F32),
        compiler_params=_cparams("parallel", "parallel"),
        name="s5_readout",
    )(xr, h, h, wu, wh, wh)


def _gelu_tanh(x):
    return 0.5 * x * (1.0 + jnp.tanh(math.sqrt(2.0 / math.pi) * (x + 0.044715 * (x * x * x))))


def _s5_glu_body(x_ref, y_ref, g_ref, sh_ref, sc_ref, dk_ref, w1_ref, w2_ref, b1_ref, b2_ref, gt_ref, o_ref):
    x = x_ref[...]
    u = _norm_mod(x, g_ref[...], sh_ref[...], sc_ref[...])
    y = _gelu_tanh(y_ref[...] + dk_ref[...] * u).astype(BF16)
    o = (_dot(y, w1_ref[...]) + b1_ref[...]) * jax.nn.sigmoid(_dot(y, w2_ref[...]) + b2_ref[...])
    o_ref[...] = x + gt_ref[...] * o


def s5_glu(x, y, g, shift, scale, d_skip, w1, b1, w2, b2, gate):
    bsz, seq, d = x.shape
    tm = min(seq, 256)
    row = lambda a: a.reshape(1, d)
    rspec = pl.BlockSpec((1, d), lambda b, i: (0, 0))
    mspec = pl.BlockSpec((None, 1, d), lambda b, i: (b, 0, 0))
    tile = pl.BlockSpec((None, tm, d), lambda b, i: (b, i, 0))
    wspec = pl.BlockSpec((d, d), lambda b, i: (0, 0))
    return pl.pallas_call(
        _s5_glu_body,
        grid=(bsz, seq // tm),
        in_specs=[tile, tile, rspec, mspec, mspec, rspec, wspec, wspec, rspec, rspec, mspec],
        out_specs=tile,
        out_shape=jax.ShapeDtypeStruct((bsz, seq, d), F32),
        compiler_params=_cparams("parallel", "parallel"),
        name="s5_glu",
    )(x, y, row(g), shift, scale, row(d_skip), w1, w2, row(b1), row(b2), gate)


def s5_mix(xl, xc, g, sh_l, sc_l, sh_c, sc_c, gate_l, a_re, a_im, log_step, b_re, b_im, c_re, c_im, d_skip,
           w1, b1, w2, b2):
    bsz, seq, d = xl.shape
    hl = norm_mod(xl, g, sh_l, sc_l)
    hc = norm_mod(xc, g, sh_c, sc_c)
    xr_c = _s5_arrange(hc)
    xr_l = _s5_arrange(hl)
    rows = bsz * S5_SEGS
    n_ctx = xr_c.shape[0] // rows
    n_lat = xr_l.shape[0] // rows
    ws, wu, wh, lt = _s5_operators(a_re, a_im, log_step, b_re, b_im, c_re, c_im)
    s = s5_inject(jnp.concatenate([xr_c, xr_l], axis=0), ws)
    h = s5_scan(s.reshape(n_ctx + n_lat, rows, -1), lt, n_ctx, n_lat, bsz)
    y = s5_readout(xr_l, h.reshape(2, n_lat * rows, -1), wu, wh)
    y = _s5_unarrange(y, bsz)
    return s5_glu(xl, y, g, sh_l, sc_l, d_skip, w1.astype(BF16), b1, w2.astype(BF16), b2, gate_l)


def hyena_mix(x, g, shift, scale, gate, w_in, b_in, conv_w, conv_b, fw1, fb1, fw2, fb2, fw3, freq, skip,
              w_out, b_out):
    seq = x.shape[1]
    if seq >= FFT_MIN_SEQ and (2 * seq) % (2 * FFT_N2) == 0:
        a, dd, _ = hyena_filter_taps(seq, fw1, fb1, fw2, fb2, fw3, freq, F32)
        v, x0 = hyena_in(x, g, shift, scale, w_in, b_in, conv_w, conv_b, F32)
        yg = hyena_conv_fft(v, x0, skip, a, dd)
    else:
        cmat, smat = dft_matrices(seq)
        a, dd, kn = hyena_filter_taps(seq, fw1, fb1, fw2, fb2, fw3, freq, BF16)
        kr, ki = hyena_filter_dft(a, dd, cmat, smat)
        v, x0 = hyena_in(x, g, shift, scale, w_in, b_in, conv_w, conv_b, BF16)
        yg = hyena_conv(v, x0, skip, kr, ki, kn, cmat, smat)
    return mm_residual(yg, w_out, b_out, x, gate)


def kernel(x, c, ctx, c_ctx, ada_w, ada_b, norm_g, final_g, hy_w_in, hy_b_in, hy_conv_w, hy_conv_b, hy_fw1,
           hy_fb1, hy_fw2, hy_fb2, hy_fw3, hy_freq, hy_skip, hy_w_out, hy_b_out, s5_a_re, s5_a_im,
           s5_log_step, s5_b_re, s5_b_im, s5_c_re, s5_c_im, s5_d, s5_w1, s5_b1, s5_w2, s5_b2, moe_wg, moe_bg,
           moe_we, moe_be, moe_w_gate, moe_w_up, moe_w_down):
    bsz, _, d = x.shape
    depth = ada_w.shape[0]
    assert depth == 2 and bsz < SUBLANES
    c_all = jnp.zeros((SUBLANES, d), F32).at[:bsz].set(c).at[bsz].set(c_ctx)
    mods = ada_mod(c_all, ada_w, ada_b)

    def mod_rows(layer, k):
        lat = mods[layer, :bsz, k * d:(k + 1) * d][:, None, :]
        cx = jnp.broadcast_to(mods[layer, bsz, k * d:(k + 1) * d][None, None, :], (bsz, 1, d))
        return lat, cx

    (sh_a, csh_a), (sc_a, csc_a), (gt_a, cgt_a) = mod_rows(0, 0), mod_rows(0, 1), mod_rows(0, 2)
    (sh_f, csh_f), (sc_f, csc_f), (gt_f, cgt_f) = mod_rows(0, 3), mod_rows(0, 4), mod_rows(0, 5)
    hy = (hy_w_in[0].astype(BF16), hy_b_in[0], hy_conv_w[0], hy_conv_b[0], hy_fw1[0], hy_fb1[0], hy_fw2[0],
          hy_fb2[0], hy_fw3[0], hy_freq[0], hy_skip[0], hy_w_out[0].astype(BF16), hy_b_out[0])
    xl = hyena_mix(x, norm_g[0, 0], sh_a, sc_a, gt_a, *hy)
    xc = hyena_mix(ctx, norm_g[0, 0], csh_a, csc_a, cgt_a, *hy)
    xl, xc = hier_moe([(xl, sh_f, sc_f, gt_f), (xc, csh_f, csc_f, cgt_f)], norm_g[0, 1],
                      moe_wg[0], moe_bg[0], moe_we[0], moe_be[0], moe_w_gate, moe_w_up, moe_w_down, 0,
                      final_g, False)

    (sh_a, csh_a), (sc_a, csc_a), (gt_a, _) = mod_rows(1, 0), mod_rows(1, 1), mod_rows(1, 2)
    (sh_f, _), (sc_f, _), (gt_f, _) = mod_rows(1, 3), mod_rows(1, 4), mod_rows(1, 5)
    xl = s5_mix(xl, xc, norm_g[1, 0], sh_a, sc_a, csh_a, csc_a, gt_a, s5_a_re[0], s5_a_im[0], s5_log_step[0],
                s5_b_re[0], s5_b_im[0], s5_c_re[0], s5_c_im[0], s5_d[0], s5_w1[0], s5_b1[0], s5_w2[0], s5_b2[0])
    (out,) = hier_moe([(xl, sh_f, sc_f, gt_f)], norm_g[1, 1], moe_wg[1], moe_bg[1], moe_we[1], moe_be[1],
                      moe_w_gate, moe_w_up, moe_w_down, 1, final_g, True)
    return out
```

```python
import functools
import math

import jax
import jax.numpy as jnp
from jax import lax
from jax.experimental import pallas as pl
from jax.experimental.pallas import tpu as pltpu

F32 = jnp.float32
BF16 = jnp.bfloat16
HIGHEST = lax.Precision.HIGHEST

NORM_EPS = 1e-6
HY_DECAY_TARGET = 1e-2
HY_FAST_PCT = 0.3
HY_SLOW_PCT = 1.5
TOP_K = 2

V7X_VMEM_LIMIT_BYTES = 56 * 1024 * 1024
LANES = 128
SUBLANES = 8
S5_TAU = 8
S5_SEGS = 2
MOE_TM = 256
MOE_BM = 256
MOE_NCH = 12
MOE_STAGE = 4
NEG_BIG = -1e30


def _cparams(*sem):
    return pltpu.CompilerParams(dimension_semantics=sem, vmem_limit_bytes=V7X_VMEM_LIMIT_BYTES)


def _norm_mod(x, g, shift, scale):
    ms = jnp.mean(x * x, axis=-1, keepdims=True)
    return (x * lax.rsqrt(ms + NORM_EPS) * g) * (1.0 + scale) + shift


def _dot(a, b):
    return jnp.dot(a, b, preferred_element_type=F32)


def _ada_body(c_ref, w_ref, b_ref, o_ref):
    x = c_ref[...]
    s = (x * jax.nn.sigmoid(x)).astype(BF16)
    o_ref[...] = _dot(s, w_ref[...].astype(BF16)) + b_ref[...]


def ada_mod(c_all, ada_w, ada_b):
    depth, d, n = ada_w.shape
    tn = min(n, 1024)
    return pl.pallas_call(
        _ada_body,
        grid=(depth, n // tn),
        in_specs=[pl.BlockSpec((SUBLANES, d), lambda l, j: (0, 0)),
                  pl.BlockSpec((None, d, tn), lambda l, j: (l, 0, j)),
                  pl.BlockSpec((None, 1, tn), lambda l, j: (l, 0, j))],
        out_specs=pl.BlockSpec((None, SUBLANES, tn), lambda l, j: (l, 0, j)),
        out_shape=jax.ShapeDtypeStruct((depth, SUBLANES, n), F32),
        compiler_params=_cparams("parallel", "parallel"),
        name="ada_mod",
    )(c_all, ada_w, ada_b.reshape(depth, 1, n))


def _hy_in_body(xp_ref, xm_ref, xn_ref, g_ref, sh_ref, sc_ref,
                w0_ref, w1_ref, w2_ref, b0_ref, b1_ref, b2_ref,
                cw0_ref, cw1_ref, cw2_ref, cb0_ref, cb1_ref, cb2_ref,
                v_ref, x0_ref):
    i = pl.program_id(2)
    ni = pl.num_programs(2)
    tm = xm_ref.shape[0]
    x = jnp.concatenate([xp_ref[...], xm_ref[...], xn_ref[...]], axis=0)
    h = _norm_mod(x, g_ref[...], sh_ref[...], sc_ref[...]).astype(BF16)
    rows = lax.broadcasted_iota(jnp.int32, (tm + 2 * SUBLANES, 1), 0)
    valid = jnp.logical_and(jnp.logical_or(rows >= SUBLANES, i > 0),
                            jnp.logical_or(rows < tm + SUBLANES, i < ni - 1))

    def part(w_ref, b_ref, cw_ref, cb_ref):
        z = jnp.where(valid, _dot(h, w_ref[...]) + b_ref[...], 0.0)
        cw = cw_ref[...]
        zp = pltpu.roll(z, 1, axis=0)[SUBLANES:tm + SUBLANES]
        zn = pltpu.roll(z, tm + 2 * SUBLANES - 1, axis=0)[SUBLANES:tm + SUBLANES]
        return zp * cw[0:1] + z[SUBLANES:tm + SUBLANES] * cw[1:2] + zn * cw[2:3] + cb_ref[...]

    x0 = part(w0_ref, b0_ref, cw0_ref, cb0_ref)
    x1 = part(w1_ref, b1_ref, cw1_ref, cb1_ref)
    v = part(w2_ref, b2_ref, cw2_ref, cb2_ref) * x1
    v_ref[...] = v.astype(v_ref.dtype)
    x0_ref[...] = x0.astype(BF16)


def hyena_in(x, g, shift, scale, w_in, b_in, conv_w, conv_b, v_dtype):
    bsz, seq, d = x.shape
    tm = min(seq, 512)
    tn = min(d, 1024)
    nj = d // tn
    r8 = tm // SUBLANES
    last8 = seq // SUBLANES - 1
    row = lambda a: a.reshape(1, -1)
    wspec = lambda k: pl.BlockSpec((d, tn), lambda j, b, i: (0, k * nj + j))
    rspec = lambda k: pl.BlockSpec((1, tn), lambda j, b, i: (0, k * nj + j))
    cspec = lambda k: pl.BlockSpec((3, tn), lambda j, b, i: (0, k * nj + j))
    mspec = pl.BlockSpec((None, 1, d), lambda j, b, i: (b, 0, 0))
    out_spec = pl.BlockSpec((None, tm, tn), lambda j, b, i: (b, i, j))
    return pl.pallas_call(
        _hy_in_body,
        grid=(nj, bsz, seq // tm),
        in_specs=[pl.BlockSpec((None, SUBLANES, d), lambda j, b, i: (b, jnp.maximum(i * r8 - 1, 0), 0)),
                  pl.BlockSpec((None, tm, d), lambda j, b, i: (b, i, 0)),
                  pl.BlockSpec((None, SUBLANES, d), lambda j, b, i: (b, jnp.minimum((i + 1) * r8, last8), 0)),
                  pl.BlockSpec((1, d), lambda j, b, i: (0, 0)), mspec, mspec,
                  wspec(0), wspec(1), wspec(2), rspec(0), rspec(1), rspec(2),
                  cspec(0), cspec(1), cspec(2), rspec(0), rspec(1), rspec(2)],
        out_specs=[out_spec, out_spec],
        out_shape=[jax.ShapeDtypeStruct((bsz, seq, d), v_dtype), jax.ShapeDtypeStruct((bsz, seq, d), BF16)],
        compiler_params=_cparams("parallel", "parallel", "parallel"),
        name="hyena_in",
    )(x, x, x, row(g), shift, scale, w_in, w_in, w_in, row(b_in), row(b_in), row(b_in),
      conv_w, conv_w, conv_w, row(conv_b), row(conv_b), row(conv_b))


def _dft_tables(seq, blk):
    n = 2 * seq
    s = jnp.arange(seq, dtype=jnp.int32)[None, :]
    fl = jnp.arange(blk, dtype=jnp.int32)[:, None]
    fh = (jnp.arange(seq // blk, dtype=jnp.int32) * blk)[:, None]
    w = 2.0 * math.pi / n
    ang_b = ((fl * s) % n).astype(F32) * w
    ang_a = ((fh * s) % n).astype(F32) * w
    return (jnp.cos(ang_a)[:, None, :], jnp.sin(ang_a)[:, None, :], jnp.cos(ang_b), jnp.sin(ang_b))


def _dft_gen_body(ca_ref, sa_ref, cb_ref, sb_ref, c_ref, s_ref):
    ca, sa, cb, sb = ca_ref[...], sa_ref[...], cb_ref[...], sb_ref[...]
    c_ref[...] = (ca * cb - sa * sb).astype(BF16)
    s_ref[...] = (sa * cb + ca * sb).astype(BF16)


def dft_matrices(seq):
    blk = min(seq, 256)
    ca, sa, cb, sb = _dft_tables(seq, blk)
    aspec = pl.BlockSpec((None, 1, seq), lambda i: (i, 0, 0))
    bspec = pl.BlockSpec((blk, seq), lambda i: (0, 0))
    ospec = pl.BlockSpec((blk, seq), lambda i: (i, 0))
    return pl.pallas_call(
        _dft_gen_body,
        grid=(seq // blk,),
        in_specs=[aspec, aspec, bspec, bspec],
        out_specs=[ospec, ospec],
        out_shape=[jax.ShapeDtypeStruct((seq, seq), BF16)] * 2,
        compiler_params=_cparams("parallel"),
        name="dft_matrices",
    )(ca, sa, cb, sb)


def _alt_sign(rows):
    return jnp.where((rows & 1) == 0, 1.0, -1.0).astype(F32)


def _filt_body(h2_ref, wf_ref, wb_ref, dl_ref, a_ref, d_ref, ny_ref):
    seq = h2_ref.shape[0]
    h2 = h2_ref[...]
    row = lax.broadcasted_iota(jnp.int32, (seq, 1), 0)
    t = row.astype(F32) * (1.0 / (seq - 1))
    win = jnp.exp(-t * dl_ref[...])
    hf = jnp.dot(h2, wf_ref[...], precision=HIGHEST, preferred_element_type=F32) * win
    hb = jnp.dot(h2, wb_ref[...], precision=HIGHEST, preferred_element_type=F32) * win
    hb = jnp.where(row == 0, 0.0, hb)
    nrm = (jnp.sum(jnp.abs(hf), axis=0, keepdims=True) + jnp.sum(jnp.abs(hb), axis=0, keepdims=True))
    inv = 1.0 / nrm
    a = (hf + hb) * inv
    a_ref[...] = a.astype(a_ref.dtype)
    d_ref[...] = ((hb - hf) * inv).astype(d_ref.dtype)
    ny = jnp.sum(a * _alt_sign(row), axis=0, keepdims=True) * (1.0 / (2 * seq))
    ny_ref[...] = jnp.broadcast_to(ny, ny_ref.shape)


def _khat_body(a_ref, d_ref, c_ref, s_ref, kr_ref, ki_ref):
    i = pl.program_id(1)
    tm = c_ref.shape[0]
    seq = c_ref.shape[1]
    f = i * tm + lax.broadcasted_iota(jnp.int32, (tm, 1), 0)
    w = jnp.where(f == 0, 1.0, 2.0).astype(F32) * (1.0 / (2 * seq))
    kr_ref[...] = _dot(c_ref[...], a_ref[...]) * w
    ki_ref[...] = _dot(s_ref[...], d_ref[...]) * w


def hyena_filter_taps(seq, fw1, fb1, fw2, fb2, fw3, freq, taps_dtype):
    d = fw3.shape[1] // 2
    bands_n = (fw1.shape[0] - 1) // 2
    t = jnp.linspace(0.0, 1.0, seq, dtype=F32)[:, None]
    w = (2.0 * math.pi / seq) * jnp.arange(seq, dtype=F32)[:, None]
    bands = jnp.linspace(1e-4, bands_n - 1, bands_n, dtype=F32)[None, :]
    z = jnp.concatenate([t, jnp.cos(bands * w), -jnp.sin(bands * w)], axis=-1)
    h = jnp.sin(freq * (jnp.dot(z, fw1, precision=HIGHEST) + fb1))
    h2 = jnp.sin(freq * (jnp.dot(h, fw2, precision=HIGHEST) + fb2))
    max_decay = math.log(HY_DECAY_TARGET) / HY_FAST_PCT
    min_decay = math.log(HY_DECAY_TARGET) / HY_SLOW_PCT
    deltas = jnp.abs(jnp.linspace(min_decay, max_decay, d, dtype=F32))[None, :]

    order = h2.shape[1]
    tn = min(d, 256)
    nj = d // tn
    return pl.pallas_call(
        _filt_body,
        grid=(nj,),
        in_specs=[pl.BlockSpec((seq, order), lambda j: (0, 0)),
                  pl.BlockSpec((order, tn), lambda j: (0, j)),
                  pl.BlockSpec((order, tn), lambda j: (0, nj + j)),
                  pl.BlockSpec((1, tn), lambda j: (0, j))],
        out_specs=[pl.BlockSpec((seq, tn), lambda j: (0, j)),
                   pl.BlockSpec((seq, tn), lambda j: (0, j)),
                   pl.BlockSpec((SUBLANES, tn), lambda j: (0, j))],
        out_shape=[jax.ShapeDtypeStruct((seq, d), taps_dtype), jax.ShapeDtypeStruct((seq, d), taps_dtype),
                   jax.ShapeDtypeStruct((SUBLANES, d), F32)],
        compiler_params=_cparams("parallel"),
        name="hyena_filter_taps",
    )(h2, fw3, fw3, deltas)


def hyena_filter_dft(a, dd, cmat, smat):
    seq, d = a.shape
    tm = min(seq, 512)
    tn2 = min(d, 512)
    return pl.pallas_call(
        _khat_body,
        grid=(d // tn2, seq // tm),
        in_specs=[pl.BlockSpec((seq, tn2), lambda j, i: (0, j)),
                  pl.BlockSpec((seq, tn2), lambda j, i: (0, j)),
                  pl.BlockSpec((tm, seq), lambda j, i: (i, 0)),
                  pl.BlockSpec((tm, seq), lambda j, i: (i, 0))],
        out_specs=[pl.BlockSpec((tm, tn2), lambda j, i: (i, j))] * 2,
        out_shape=[jax.ShapeDtypeStruct((seq, d), F32)] * 2,
        compiler_params=_cparams("parallel", "parallel"),
        name="hyena_filter_dft",
    )(a, dd, cmat, smat)


def _dft_fwd_body(v_ref, c_ref, s_ref, kr_ref, ki_ref, kn_ref, ya_ref, yb_ref, yn_ref):
    i = pl.program_id(2)
    v = v_ref[...]
    vr = _dot(c_ref[...], v)
    p = _dot(s_ref[...], v)
    kr = kr_ref[...]
    ki = ki_ref[...]
    ya_ref[...] = (vr * kr + p * ki).astype(BF16)
    yb_ref[...] = (p * kr - vr * ki).astype(BF16)

    @pl.when(i == 0)
    def _():
        seq = v.shape[0]
        row = lax.broadcasted_iota(jnp.int32, (seq, 1), 0)
        vl = jnp.sum(v.astype(F32) * _alt_sign(row), axis=0, keepdims=True)
        yn_ref[...] = jnp.broadcast_to(vl * kn_ref[0:1, :], yn_ref.shape)


def _dft_inv_body(ya_ref, yb_ref, c_ref, s_ref, v_ref, x0_ref, skip_ref, yn_ref, o_ref):
    i = pl.program_id(2)
    tm = c_ref.shape[0]
    acc = _dot(c_ref[...], ya_ref[...]) + _dot(s_ref[...], yb_ref[...])
    t = i * tm + lax.broadcasted_iota(jnp.int32, (tm, 1), 0)
    y = acc + _alt_sign(t) * yn_ref[0:1, :] + skip_ref[...] * v_ref[...].astype(F32)
    o_ref[...] = (y * x0_ref[...].astype(F32)).astype(BF16)


def hyena_conv(v, x0, skip, kr, ki, kn, cmat, smat):
    bsz, seq, d = v.shape
    tm = min(seq, 512)
    tn = min(d, 512)
    grid = (bsz, d // tn, seq // tm)
    full = pl.BlockSpec((None, seq, tn), lambda b, j, i: (b, 0, j))
    mat = pl.BlockSpec((tm, seq), lambda b, j, i: (i, 0))
    tile = pl.BlockSpec((None, tm, tn), lambda b, j, i: (b, i, j))
    ktile = pl.BlockSpec((tm, tn), lambda b, j, i: (i, j))
    nyq = pl.BlockSpec((None, SUBLANES, tn), lambda b, j, i: (b, 0, j))
    ya, yb, yn = pl.pallas_call(
        _dft_fwd_body,
        grid=grid,
        in_specs=[full, mat, mat, ktile, ktile, pl.BlockSpec((SUBLANES, tn), lambda b, j, i: (0, j))],
        out_specs=[tile, tile, nyq],
        out_shape=[jax.ShapeDtypeStruct((bsz, seq, d), BF16)] * 2
        + [jax.ShapeDtypeStruct((bsz, SUBLANES, d), F32)],
        compiler_params=_cparams("parallel", "parallel", "arbitrary"),
        name="hyena_dft_fwd",
    )(v, cmat, smat, kr, ki, kn)
    return pl.pallas_call(
        _dft_inv_body,
        grid=grid,
        in_specs=[full, full, mat, mat, tile, tile, pl.BlockSpec((1, tn), lambda b, j, i: (0, j)), nyq],
        out_specs=tile,
        out_shape=jax.ShapeDtypeStruct((bsz, seq, d), BF16),
        compiler_params=_cparams("parallel", "parallel", "parallel"),
        name="hyena_dft_inv",
    )(ya, yb, cmat, smat, v, x0, skip.reshape(1, d), yn)


FFT_N2 = 256
FFT_MIN_SEQ = 1024
FFT_UNROLL = 8


def _fft_matrices(seq):
    n = 2 * seq
    n2 = FFT_N2
    n1 = n // n2
    r8 = SUBLANES
    q = jnp.arange(n2 // r8, dtype=jnp.int32)[:, None, None, None]
    f1 = jnp.arange(n1, dtype=jnp.int32)[None, :, None, None]
    r = jnp.arange(r8, dtype=jnp.int32)[None, None, :, None]
    t1 = jnp.arange(n1 // 2, dtype=jnp.int32)[None, None, None, :]
    ang = ((f1 * (t1 * n2 + q * r8 + r)) % n).astype(F32) * (2.0 * math.pi / n)
    g = jnp.stack([jnp.cos(ang), -jnp.sin(ang)], axis=3)
    eye = jnp.eye(r8, dtype=F32)[None, None, :, None, None, :]
    ma = (g[..., None] * eye).reshape(n2 // r8, n1 * r8 * 2, (n1 // 2) * r8).astype(BF16)
    f2 = jnp.arange(n2, dtype=jnp.int32)[:, None]
    t2 = jnp.arange(n2, dtype=jnp.int32)[None, :]
    th = ((f2 * t2) % n2).astype(F32) * (2.0 * math.pi / n2)
    co, si = jnp.cos(th), jnp.sin(th)
    wc = jnp.stack([jnp.stack([co, si], axis=-1), jnp.stack([-si, co], axis=-1)], axis=0)
    wc = wc.reshape(2 * n2, 2 * n2).astype(BF16)
    return ma, jnp.swapaxes(ma, 1, 2), wc, wc.T


def _fft_stage_a(x_ref, ma_ref, s1):
    n1h, n_q, r8, tn = x_ref.shape
    n1 = s1.shape[0]

    def body(q, carry):
        x = x_ref[:, pl.ds(q, 1), :, :].reshape(n1h * r8, tn).astype(BF16)
        a = _dot(ma_ref[q], x).astype(BF16)
        s1[:, pl.ds(pl.multiple_of(q * 2 * r8, 2 * r8), 2 * r8), :] = a.reshape(n1, 2 * r8, tn)
        return carry

    lax.fori_loop(0, n_q, body, 0, unroll=FFT_UNROLL)


def _fft_conv_body(v_ref, x0_ref, k_ref, skip_ref, ma_ref, mat_ref, wc_ref, wci_ref, o_ref, s1, ysc):
    n1h, n_q, r8, tn = v_ref.shape
    n1 = s1.shape[0]
    n2 = s1.shape[1] // 2
    seq = n1h * n_q * r8
    _fft_stage_a(v_ref, ma_ref, s1)

    def slab(f, carry):
        y = _dot(wc_ref[...], s1[f])
        yr, yi = y[:n2], y[n2:]
        kr = k_ref[f, 0].astype(F32)
        ki = k_ref[f, 1].astype(F32)
        p = jnp.concatenate([yr * kr - yi * ki, yr * ki + yi * kr], axis=0).astype(BF16)
        s1[f] = _dot(wci_ref[...], p).astype(BF16)
        return carry

    lax.fori_loop(0, n1, slab, 0, unroll=FFT_UNROLL)

    def inv_a(q, carry):
        z = s1[:, pl.ds(pl.multiple_of(q * 2 * r8, 2 * r8), 2 * r8), :].reshape(n1 * 2 * r8, tn)
        ysc[:, pl.ds(q, 1), :, :] = _dot(mat_ref[q], z).reshape(n1h, 1, r8, tn)
        return carry

    lax.fori_loop(0, n_q, inv_a, 0, unroll=FFT_UNROLL)
    y = ysc[...].reshape(seq, tn) + skip_ref[...] * v_ref[...].reshape(seq, tn)
    o_ref[...] = (y * x0_ref[...].astype(F32)).astype(BF16)


def _fft_filter_body(a_ref, d_ref, ma_ref, wc_ref, k_ref, s1):
    n1 = s1.shape[0]
    n2 = s1.shape[1] // 2
    scale = 1.0 / (n1 * n2)
    for src_ref, part, sign in ((a_ref, 0, scale), (d_ref, 1, -scale)):
        _fft_stage_a(src_ref, ma_ref, s1)

        def slab(f, carry):
            y = _dot(wc_ref[part * n2:(part + 1) * n2, :], s1[f])
            k_ref[f, part] = (y * sign).astype(BF16)
            return carry

        lax.fori_loop(0, n1, slab, 0, unroll=FFT_UNROLL)


def hyena_conv_fft(v, x0, skip, a, dd):
    bsz, seq, d = v.shape
    n2 = FFT_N2
    n1 = 2 * seq // n2
    n_q = n2 // SUBLANES
    tn = min(d, 256)
    ma, mat, wc, wci = _fft_matrices(seq)
    const = lambda shape: pl.BlockSpec(shape, lambda *_: (0,) * len(shape), pipeline_mode=pl.Buffered(1))
    view = lambda t: t.reshape(t.shape[:-2] + (n1 // 2, n_q, SUBLANES, d))
    tap = pl.BlockSpec((n1 // 2, n_q, SUBLANES, tn), lambda j: (0, 0, 0, j))
    khat = pl.pallas_call(
        _fft_filter_body,
        grid=(d // tn,),
        in_specs=[tap, tap, const(ma.shape), const(wc.shape)],
        out_specs=pl.BlockSpec((n1, 2, n2, tn), lambda j: (0, 0, 0, j)),
        out_shape=jax.ShapeDtypeStruct((n1, 2, n2, d), BF16),
        scratch_shapes=[pltpu.VMEM((n1, 2 * n2, tn), BF16)],
        compiler_params=_cparams("parallel"),
        name="hyena_filter_fft",
    )(view(a), view(dd), ma, wc)
    return pl.pallas_call(
        _fft_conv_body,
        grid=(d // tn, bsz),
        in_specs=[pl.BlockSpec((None, n1 // 2, n_q, SUBLANES, tn), lambda j, b: (b, 0, 0, 0, j)),
                  pl.BlockSpec((None, seq, tn), lambda j, b: (b, 0, j)),
                  pl.BlockSpec((n1, 2, n2, tn), lambda j, b: (0, 0, 0, j), pipeline_mode=pl.Buffered(1)),
                  pl.BlockSpec((1, tn), lambda j, b: (0, j)),
                  const(ma.shape), const(mat.shape), const(wc.shape), const(wci.shape)],
        out_specs=pl.BlockSpec((None, seq, tn), lambda j, b: (b, 0, j)),
        out_shape=jax.ShapeDtypeStruct((bsz, seq, d), BF16),
        scratch_shapes=[pltpu.VMEM((n1, 2 * n2, tn), BF16), pltpu.VMEM((n1 // 2, n_q, SUBLANES, tn), F32)],
        compiler_params=_cparams("parallel", "arbitrary"),
        name="hyena_conv_fft",
    )(view(v), x0, khat, skip.reshape(1, d), ma, mat, wc, wci)


def _mm_res_body(x_ref, w_ref, b_ref, res_ref, gate_ref, o_ref):
    o_ref[...] = res_ref[...] + gate_ref[...] * (_dot(x_ref[...], w_ref[...]) + b_ref[...])


def mm_residual(x, w, b, res, gate):
    bsz, seq, k = x.shape
    n = w.shape[1]
    tm = min(seq, 512)
    return pl.pallas_call(
        _mm_res_body,
        grid=(bsz, seq // tm),
        in_specs=[pl.BlockSpec((None, tm, k), lambda b, i: (b, i, 0)),
                  pl.BlockSpec((k, n), lambda b, i: (0, 0)),
                  pl.BlockSpec((1, n), lambda b, i: (0, 0)),
                  pl.BlockSpec((None, tm, n), lambda b, i: (b, i, 0)),
                  pl.BlockSpec((None, 1, n), lambda b, i: (b, 0, 0))],
        out_specs=pl.BlockSpec((None, tm, n), lambda b, i: (b, i, 0)),
        out_shape=jax.ShapeDtypeStruct((bsz, seq, n), F32),
        compiler_params=_cparams("parallel", "parallel"),
        name="mm_residual",
    )(x, w, b.reshape(1, n), res, gate)


def _moe_pre_body(*refs, n_groups, n_experts, tile_offs):
    n_streams = len(tile_offs) - 1
    g_ref, wr_ref, br_ref, tok_ref, eid_ref, gate_ref = refs[3 * n_streams:]
    i = pl.program_id(0)
    for k in range(n_streams):
        x_ref, sh_ref, sc_ref = refs[3 * k:3 * k + 3]

        @pl.when(jnp.logical_and(i >= tile_offs[k], i < tile_offs[k + 1]))
        def _():
            tok = _norm_mod(x_ref[...], g_ref[...], sh_ref[...], sc_ref[...])
            _route_tokens(tok, wr_ref, br_ref, tok_ref, eid_ref, gate_ref, n_groups, n_experts)


def _route_tokens(tok, wr_ref, br_ref, tok_ref, eid_ref, gate_ref, n_groups, n_experts):
    tok_ref[...] = tok
    t_hi = tok.astype(BF16)
    t_lo = (tok - t_hi.astype(F32)).astype(BF16)
    logits = (_dot(t_hi, wr_ref[0]) + _dot(t_hi, wr_ref[1]) + _dot(t_lo, wr_ref[0])) + br_ref[...]
    lane = lax.broadcasted_iota(jnp.int32, logits.shape, 1)
    per = n_experts // n_groups
    big = jnp.int32(1 << 20)
    gmask = jnp.logical_and(lane >= n_experts, lane < n_experts + n_groups)
    gl = jnp.where(gmask, logits, NEG_BIG)
    gmax = jnp.max(gl, axis=-1, keepdims=True)
    gidx = jnp.min(jnp.where(gl == gmax, lane - n_experts, big), axis=-1, keepdims=True)
    p_top = 1.0 / jnp.sum(jnp.where(gmask, jnp.exp(gl - gmax), 0.0), axis=-1, keepdims=True)
    lo = gidx * per
    emask = jnp.logical_and(lane >= lo, lane < lo + per)
    el = jnp.where(emask, logits, NEG_BIG)
    m1 = jnp.max(el, axis=-1, keepdims=True)
    i1 = jnp.min(jnp.where(el == m1, lane, big), axis=-1, keepdims=True)
    el2 = jnp.where(lane == i1, NEG_BIG, el)
    m2 = jnp.max(el2, axis=-1, keepdims=True)
    i2 = jnp.min(jnp.where(el2 == m2, lane, big), axis=-1, keepdims=True)
    e21 = jnp.exp(m2 - m1)
    g1 = p_top / (1.0 + e21)
    g2 = g1 * e21
    eid_ref[...] = jnp.where(lane == 0, i1, jnp.where(lane == 1, i2, 0))
    gate_ref[...] = jnp.where(lane == 0, g1, jnp.where(lane == 1, g2, 0.0))


def moe_pre(streams, g, wr, br, n_groups, n_experts):
    d = streams[0][0].shape[2]
    tm = MOE_TM
    tile_offs = [0]
    in_specs, args = [], []
    for x, shift, scale in streams:
        bsz, seq, _ = x.shape
        nt = seq // tm
        n_tiles = bsz * nt
        off = tile_offs[-1]
        tile_offs.append(off + n_tiles)

        def tile(i, off=off, n_tiles=n_tiles):
            return jnp.clip(i - off, 0, n_tiles - 1)

        in_specs += [pl.BlockSpec((None, tm, d), lambda i, tile=tile, nt=nt: (tile(i) // nt, tile(i) % nt, 0)),
                     pl.BlockSpec((None, 1, d), lambda i, tile=tile, nt=nt: (tile(i) // nt, 0, 0)),
                     pl.BlockSpec((None, 1, d), lambda i, tile=tile, nt=nt: (tile(i) // nt, 0, 0))]
        args += [x, shift, scale]
    in_specs += [pl.BlockSpec((1, d), lambda i: (0, 0)),
                 pl.BlockSpec((2, d, LANES), lambda i: (0, 0, 0)),
                 pl.BlockSpec((1, LANES), lambda i: (0, 0))]
    wr_hi = wr.astype(BF16)
    wr_split = jnp.stack([wr_hi, (wr - wr_hi.astype(F32)).astype(BF16)])
    args += [g.reshape(1, d), wr_split, br]
    total = tile_offs[-1] * tm
    rout = pl.BlockSpec((tm, LANES), lambda i: (i, 0))
    tok, eid, gate = pl.pallas_call(
        functools.partial(_moe_pre_body, n_groups=n_groups, n_experts=n_experts, tile_offs=tuple(tile_offs)),
        grid=(tile_offs[-1],),
        in_specs=in_specs,
        out_specs=[pl.BlockSpec((tm, d), lambda i: (i, 0)), rout, rout],
        out_shape=[jax.ShapeDtypeStruct((total, d), F32),
                   jax.ShapeDtypeStruct((total, LANES), jnp.int32),
                   jax.ShapeDtypeStruct((total, LANES), F32)],
        compiler_params=_cparams("parallel"),
        name="moe_pre",
    )(*args)
    return tok, eid, gate, tile_offs[:-1]


def _start_row_gather(idx_ref, n_rows, stride, offset, src_hbm, dst_vmem, sem):
    def body(r, c):
        pltpu.make_async_copy(src_hbm.at[pl.ds(idx_ref[stride * r + offset], 1)],
                              dst_vmem.at[pl.ds(r, 1)], sem).start()
        return c

    lax.fori_loop(0, n_rows, body, 0, unroll=8)


def _wait_row_gather(n_rows, src_hbm, dst_vmem, sem):
    pltpu.make_async_copy(src_hbm.at[pl.ds(0, n_rows)], dst_vmem, sem).wait()


def _expert_body(bv_ref, rk_ref, pe_ref, tot_ref, src_ref, nxt_ref, tok_ref, wg_hbm, wu_hbm, wd_hbm, o_ref,
                 xbuf, xsem, wcache, stg, wsem, cnt, *, layer):
    i = pl.program_id(0)
    n = pl.num_programs(0)
    slot = i % 2
    cr, cc = stg.shape[1:]
    total = tot_ref[0]
    mats_hbm = (wg_hbm, wu_hbm, wd_hbm)

    @pl.when(i == 0)
    def _():
        cnt[0] = 0
        cnt[1] = 0

    @pl.when(jnp.logical_and(i == 0, bv_ref[0] > 0))
    def _():
        _start_row_gather(src_ref, MOE_BM, 1, 0, tok_ref, xbuf.at[0], xsem.at[0])

    @pl.when(jnp.logical_and(i + 1 < n, bv_ref[jnp.minimum(i + 1, n - 1)] > 0))
    def _():
        _start_row_gather(nxt_ref, MOE_BM, 1, 0, tok_ref, xbuf.at[1 - slot], xsem.at[1 - slot])

    def chunk_geom(c):
        q = c % MOE_NCH
        m = q // 4
        sub = q % 4
        r0 = jnp.where(m < 2, sub, sub // 2) * cr
        c0 = jnp.where(m < 2, 0, sub % 2) * cc
        return m, pl.multiple_of(r0, cr), pl.multiple_of(c0, cc)

    def issue(c):
        e = pe_ref[c // MOE_NCH]
        m, r0, c0 = chunk_geom(c)
        s = c % MOE_STAGE
        for k, w_hbm in enumerate(mats_hbm):
            @pl.when(m == k)
            def _():
                pltpu.make_async_copy(w_hbm.at[layer, e, pl.ds(r0, cr), pl.ds(c0, cc)], stg.at[s],
                                      wsem.at[s]).start()

    def cast(c):
        s = c % MOE_STAGE
        pltpu.make_async_copy(wg_hbm.at[layer, 0, pl.ds(0, cr), pl.ds(0, cc)], stg.at[s], wsem.at[s]).wait()
        wcache[(c // MOE_NCH) % 2, c % MOE_NCH] = stg[s].astype(BF16)

    valid = bv_ref[i] > 0
    rank = rk_ref[i]
    issued = cnt[0]
    done = cnt[1]
    limit = jnp.minimum(total, MOE_NCH * (rank + 2))
    need = jnp.where(valid, MOE_NCH * (rank + 1), done)

    def fill(issued, done):
        hi = jnp.minimum(limit, done + MOE_STAGE)

        def body(c, carry):
            issue(c)
            return carry

        lax.fori_loop(issued, hi, body, 0)
        return jnp.maximum(issued, hi)

    def cast_and_refill(c, issued):
        cast(c)
        more = issued < jnp.minimum(limit, c + 1 + MOE_STAGE)

        @pl.when(more)
        def _():
            issue(issued)

        return issued + more.astype(jnp.int32)

    issued = fill(issued, done)
    issued = lax.fori_loop(done, need, cast_and_refill, issued)
    done = jnp.maximum(done, need)

    @pl.when(valid)
    def _():
        ws = rank % 2
        _wait_row_gather(MOE_BM, tok_ref, xbuf.at[slot], xsem.at[slot])
        x = xbuf[slot].astype(BF16)
        gate = sum(_dot(x[:, k * cr:(k + 1) * cr], wcache[ws, k]) for k in range(4))
        up = sum(_dot(x[:, k * cr:(k + 1) * cr], wcache[ws, 4 + k]) for k in range(4))
        h = (gate * jax.nn.sigmoid(gate) * up).astype(BF16)
        for half in range(2):
            o_ref[:, half * cc:(half + 1) * cc] = sum(
                _dot(h[:, k * cr:(k + 1) * cr], wcache[ws, 8 + 2 * k + half]) for k in range(2))

    @pl.when(jnp.logical_not(valid))
    def _():
        o_ref[...] = jnp.zeros_like(o_ref)

    fetched = issued
    issued = lax.fori_loop(done, fetched, cast_and_refill, issued)
    done = jnp.maximum(done, fetched)
    last = i == n - 1
    tail = jnp.where(last, issued, done)

    def drain(c, carry):
        cast(c)
        return carry

    lax.fori_loop(done, tail, drain, 0)
    cnt[0] = issued
    cnt[1] = jnp.maximum(done, tail)


def moe_experts(tok, src_tok, block_valid, block_rank, present, n_chunks, w_gate, w_up, w_down, layer):
    d = tok.shape[1]
    n_rows = src_tok.shape[0]
    n_blocks = n_rows // MOE_BM
    dh = w_gate.shape[3]
    assert 2 * dh == d and MOE_NCH == 12
    cr, cc = d // 4, dh
    any_spec = pl.BlockSpec(memory_space=pl.ANY)
    grid_spec = pltpu.PrefetchScalarGridSpec(
        num_scalar_prefetch=4,
        grid=(n_blocks,),
        in_specs=[pl.BlockSpec((MOE_BM,), lambda i, *_: (i,), memory_space=pltpu.SMEM),
                  pl.BlockSpec((MOE_BM,), lambda i, *_: (jnp.minimum(i + 1, n_blocks - 1),),
                               memory_space=pltpu.SMEM),
                  any_spec, any_spec, any_spec, any_spec],
        out_specs=pl.BlockSpec((MOE_BM, d), lambda i, *_: (i, 0)),
        scratch_shapes=[pltpu.VMEM((2, MOE_BM, d), F32), pltpu.SemaphoreType.DMA((2,)),
                        pltpu.VMEM((2, MOE_NCH, cr, cc), BF16),
                        pltpu.VMEM((MOE_STAGE, cr, cc), F32), pltpu.SemaphoreType.DMA((MOE_STAGE,)),
                        pltpu.SMEM((2,), jnp.int32)],
    )
    return pl.pallas_call(
        functools.partial(_expert_body, layer=layer),
        grid_spec=grid_spec,
        out_shape=jax.ShapeDtypeStruct((n_rows, d), F32),
        compiler_params=_cparams("arbitrary"),
        name="moe_experts",
    )(block_valid, block_rank, present, n_chunks, src_tok, src_tok, tok, w_gate, w_up, w_down)


def _combine_body(dest_ref, nxt_ref, os_ref, gate_ref, res_ref, gt_ref, fg_ref, o_ref, buf, sem, *, final_norm):
    rows = res_ref.shape[0]
    i = pl.program_id(0)
    n = pl.num_programs(0)
    slot = i % 2

    def start(idx_ref, s):
        for k in range(TOP_K):
            _start_row_gather(idx_ref, rows, TOP_K, k, os_ref, buf.at[s, k], sem.at[s])

    @pl.when(i == 0)
    def _():
        start(dest_ref, 0)

    @pl.when(i + 1 < n)
    def _():
        start(nxt_ref, 1 - slot)

    for k in range(TOP_K):
        _wait_row_gather(rows, os_ref, buf.at[slot, k], sem.at[slot])
    gates = gate_ref[...]
    mo = gates[:, 0:1] * buf[slot, 0] + gates[:, 1:2] * buf[slot, 1]
    y = res_ref[...] + gt_ref[...] * mo
    if final_norm:
        ms = jnp.mean(y * y, axis=-1, keepdims=True)
        y = y * lax.rsqrt(ms + NORM_EPS) * fg_ref[...]
    o_ref[...] = y


def moe_combine(os, dest, gates, tile0, res, gt, final_g, final_norm):
    bsz, seq, d = res.shape
    rows = MOE_TM
    nt = seq // rows
    n = bsz * nt
    tile = pl.BlockSpec((None, rows, d), lambda i: (i // nt, i % nt, 0))
    return pl.pallas_call(
        functools.partial(_combine_body, final_norm=final_norm),
        grid=(n,),
        in_specs=[pl.BlockSpec((rows * TOP_K,), lambda i: (tile0 + i,), memory_space=pltpu.SMEM),
                  pl.BlockSpec((rows * TOP_K,), lambda i: (tile0 + jnp.minimum(i + 1, n - 1),),
                               memory_space=pltpu.SMEM),
                  pl.BlockSpec(memory_space=pl.ANY),
                  pl.BlockSpec((rows, LANES), lambda i: (tile0 + i, 0)),
                  tile,
                  pl.BlockSpec((None, 1, d), lambda i: (i // nt, 0, 0)),
                  pl.BlockSpec((1, d), lambda i: (0, 0))],
        out_specs=tile,
        out_shape=jax.ShapeDtypeStruct((bsz, seq, d), F32),
        scratch_shapes=[pltpu.VMEM((2, TOP_K, rows, d), F32), pltpu.SemaphoreType.DMA((2,))],
        compiler_params=_cparams("arbitrary"),
        name="moe_combine",
    )(dest, dest, os, gates, res, gt, final_g.reshape(1, d))


def _route_plan(eid, n_experts):
    e_flat = eid.reshape(-1)
    a = e_flat.shape[0]
    order = jnp.argsort(e_flat)
    pos = jnp.argsort(order)
    experts = jnp.arange(n_experts, dtype=jnp.int32)
    counts = jnp.sum((e_flat[:, None] == experts[None, :]).astype(jnp.int32), axis=0)
    start = jnp.cumsum(counts) - counts
    padded = (counts + MOE_BM - 1) // MOE_BM * MOE_BM
    pad_end = jnp.cumsum(padded)
    pad_start = pad_end - padded
    dest = pad_start[e_flat] + pos - start[e_flat]
    n_blocks = -(-a // MOE_BM) + n_experts
    starts = jnp.arange(n_blocks, dtype=jnp.int32) * MOE_BM
    block_expert = jnp.minimum(jnp.sum((pad_end[None, :] <= starts[:, None]).astype(jnp.int32), axis=1),
                               n_experts - 1)
    block_valid = (starts < pad_end[-1]).astype(jnp.int32)
    rows = jnp.arange(n_blocks * MOE_BM, dtype=jnp.int32)
    e_row = jnp.repeat(block_expert, MOE_BM)
    off = rows - pad_start[e_row]
    live = jnp.logical_and(off < counts[e_row], jnp.repeat(block_valid, MOE_BM) > 0)
    src_tok = jnp.where(live, order[jnp.clip(start[e_row] + off, 0, a - 1)] // TOP_K, 0)
    has = (counts > 0).astype(jnp.int32)
    block_rank = (jnp.cumsum(has) - 1)[block_expert]
    present = jnp.argsort(1 - has)
    n_chunks = (MOE_NCH * jnp.sum(has)).reshape(1)
    i32 = lambda v: v.astype(jnp.int32)
    return i32(dest), i32(src_tok), block_valid, i32(block_rank), i32(present), i32(n_chunks)


def hier_moe(streams, norm_g, wg, bg, we, be, w_gate, w_up, w_down, layer, final_g, final_norm):
    n_groups = wg.shape[1]
    n_experts = we.shape[1]
    d = wg.shape[0]
    wr = jnp.zeros((d, LANES), F32).at[:, :n_experts].set(we).at[:, n_experts:n_experts + n_groups].set(wg)
    br = jnp.zeros((1, LANES), F32).at[0, :n_experts].set(be).at[0, n_experts:n_experts + n_groups].set(bg)
    tok, eid, gates, tile0s = moe_pre([s[:3] for s in streams], norm_g, wr, br, n_groups, n_experts)
    dest, src_tok, block_valid, block_rank, present, n_chunks = _route_plan(eid[:, :TOP_K], n_experts)
    os = moe_experts(tok, src_tok, block_valid, block_rank, present, n_chunks, w_gate, w_up, w_down, layer)
    return [moe_combine(os, dest, gates, tile0, x, gt, final_g, final_norm)
            for (x, _, _, gt), tile0 in zip(streams, tile0s)]


def _norm_mod_body(x_ref, g_ref, sh_ref, sc_ref, o_ref):
    o_ref[...] = _norm_mod(x_ref[...], g_ref[...], sh_ref[...], sc_ref[...]).astype(o_ref.dtype)


def norm_mod(x, g, shift, scale):
    bsz, seq, d = x.shape
    tm = min(seq, 512)
    mspec = pl.BlockSpec((None, 1, d), lambda b, i: (b, 0, 0))
    return pl.pallas_call(
        _norm_mod_body,
        grid=(bsz, seq // tm),
        in_specs=[pl.BlockSpec((None, tm, d), lambda b, i: (b, i, 0)),
                  pl.BlockSpec((1, d), lambda b, i: (0, 0)), mspec, mspec],
        out_specs=pl.BlockSpec((None, tm, d), lambda b, i: (b, i, 0)),
        out_shape=jax.ShapeDtypeStruct((bsz, seq, d), BF16),
        compiler_params=_cparams("parallel", "parallel"),
        name="norm_mod",
    )(x, g.reshape(1, d), shift, scale)


def _s5_arrange(h):
    bsz, t, d = h.shape
    c = t // (S5_SEGS * S5_TAU)
    h = h.reshape(bsz, S5_SEGS, c, S5_TAU, d // LANES, LANES)
    return h.transpose(2, 0, 1, 4, 3, 5).reshape(c * bsz * S5_SEGS, d * S5_TAU)


def _s5_unarrange(y, bsz):
    r, w = y.shape
    d = w // S5_TAU
    c = r // (bsz * S5_SEGS)
    y = y.reshape(c, bsz, S5_SEGS, d // LANES, S5_TAU, LANES)
    return y.transpose(1, 2, 0, 4, 3, 5).reshape(bsz, S5_SEGS * c * S5_TAU, d)


def _s5_operators(a_re, a_im, log_step, b_re, b_im, c_re, c_im):
    n_g, n_p = a_re.shape[1:]
    n_h = b_re.shape[-1]
    gpt = LANES // n_h
    n_j = n_g // gpt
    tau = S5_TAU
    assert tau * n_h == LANES and 2 * n_p == LANES
    lam_step = lax.complex(a_re, a_im) * jnp.exp(log_step)[..., None]
    lam_bar = jnp.exp(lam_step)
    b_bar = ((lam_bar - 1.0) / lax.complex(a_re, a_im))[..., None] * lax.complex(b_re, b_im)
    c_mat = lax.complex(c_re, c_im)
    ks = jnp.arange(tau + 1, dtype=F32)[None, :, None, None]
    pw = jnp.exp(lam_step[:, None] * ks)
    ein = functools.partial(jnp.einsum, precision=HIGHEST)
    inj_c, cl_c, lt = [], [], []
    tz_c = 0.0
    for d in range(2):
        pos = jnp.arange(tau) if d == 0 else jnp.arange(tau)[::-1]
        inj = (pw[d][tau - 1 - pos][..., None] * b_bar[d][None]).reshape(tau, n_j, gpt, n_p, n_h)
        inj = inj.transpose(1, 0, 2, 4, 3).reshape(n_j, tau * LANES, n_p)
        inj_c.append(jnp.concatenate([inj.real, inj.imag], axis=-1))
        cl = (c_mat[d][None] * pw[d][pos + 1][:, :, None, :]).reshape(tau, n_j, gpt, n_h, n_p)
        cl = cl.transpose(1, 2, 4, 0, 3).reshape(n_j, gpt * n_p, tau * n_h)
        cl_c.append(jnp.concatenate([cl.real, -cl.imag], axis=1))
        mk = ein('gop,kgp,gph->kgoh', c_mat[d], pw[d][:tau], b_bar[d]).real
        diff = pos[:, None] - pos[None, :]
        tz = jnp.where((diff >= 0)[:, :, None, None, None], mk[jnp.clip(diff, 0, tau - 1)], 0.0)
        tz = tz.reshape(tau, tau, n_j, gpt, n_h, n_h)
        tz_c = tz_c + tz.transpose(2, 1, 3, 5, 0, 4).reshape(n_j, tau * LANES, tau * n_h)
        lt_d = pw[d][tau].reshape(n_j, 1, gpt * n_p)
        lt.append(jnp.concatenate([lt_d.real, lt_d.imag], axis=-1))
    ws, wu, wh = s5_expand(jnp.stack(inj_c).astype(BF16), jnp.stack(cl_c).astype(BF16), tz_c.astype(BF16),
                           n_h, n_p)
    return ws, wu, wh, jnp.stack(lt).astype(F32)


def _s5_expand_body(inj_ref, cl_ref, tz_ref, ws_ref, wu_ref, wh_ref, *, n_h, n_p):
    rows = tz_ref.shape[0]
    gpt = LANES // n_h
    row = lax.broadcasted_iota(jnp.int32, (rows, LANES), 0)
    lane = lax.broadcasted_iota(jnp.int32, (rows, LANES), 1)
    sel_r = lax.broadcasted_iota(jnp.int32, (LANES, LANES), 0)
    sel_l = lax.broadcasted_iota(jnp.int32, (LANES, LANES), 1)
    grp_in = (row // n_h) % gpt
    grp_st = (row // n_p) % gpt

    def spread_out(m, t, grp_row):
        sel = jnp.logical_and(sel_r // n_h == t, sel_r % n_h == sel_l % n_h).astype(BF16)
        return jnp.where(grp_row == lane // n_h, _dot(m, sel), 0.0).astype(BF16)

    def spread_state(m, c, q, grp_row):
        sel = jnp.logical_and(sel_r // n_p == c, sel_r % n_p == sel_l % n_p).astype(BF16)
        return jnp.where(grp_row == (LANES // n_p) * q + lane // n_p, _dot(m, sel), 0.0).astype(BF16)

    w2 = 2 * gpt * n_p
    tz = tz_ref[...]
    for t in range(S5_TAU):
        wu_ref[:, t * LANES:(t + 1) * LANES] = spread_out(tz, t, grp_in)
    for d in range(2):
        cl = cl_ref[d]
        inj = inj_ref[d]
        for t in range(S5_TAU):
            wh_ref[d, :, t * LANES:(t + 1) * LANES] = spread_out(cl, t, grp_st)
        for c in range(2):
            for q in range(gpt * n_p // LANES):
                lo = d * w2 + c * gpt * n_p + q * LANES
                ws_ref[:, lo:lo + LANES] = spread_state(inj, c, q, grp_in)


def s5_expand(inj_c, cl_c, tz_c, n_h, n_p):
    n_j, rows, _ = tz_c.shape
    gpt = LANES // n_h
    w2 = 2 * gpt * n_p
    assert rows == S5_TAU * LANES == w2
    cspec = pl.BlockSpec((2, None, rows, LANES), lambda j: (0, j, 0, 0))
    return pl.pallas_call(
        functools.partial(_s5_expand_body, n_h=n_h, n_p=n_p),
        grid=(n_j,),
        in_specs=[cspec, cspec, pl.BlockSpec((None, rows, LANES), lambda j: (j, 0, 0))],
        out_specs=[pl.BlockSpec((None, rows, 2 * w2), lambda j: (j, 0, 0)),
                   pl.BlockSpec((None, rows, rows), lambda j: (j, 0, 0)),
                   pl.BlockSpec((2, None, w2, rows), lambda j: (0, j, 0, 0))],
        out_shape=[jax.ShapeDtypeStruct((n_j, rows, 2 * w2), BF16),
                   jax.ShapeDtypeStruct((n_j, rows, rows), BF16),
                   jax.ShapeDtypeStruct((2, n_j, w2, rows), BF16)],
        compiler_params=_cparams("parallel"),
        name="s5_expand",
    )(inj_c, cl_c, tz_c)


def _s5_inj_body(x_ref, w_ref, o_ref):
    o_ref[...] = _dot(x_ref[...], w_ref[...])


def s5_inject(xr, ws):
    r = xr.shape[0]
    n_j, k, n = ws.shape
    tm = r // 2 if r % 32 == 0 else r
    return pl.pallas_call(
        _s5_inj_body,
        grid=(n_j, r // tm),
        in_specs=[pl.BlockSpec((tm, k), lambda j, i: (i, j)),
                  pl.BlockSpec((None, k, n), lambda j, i: (j, 0, 0))],
        out_specs=pl.BlockSpec((tm, n), lambda j, i: (i, j)),
        out_shape=jax.ShapeDtypeStruct((r, n_j * n), F32),
        compiler_params=_cparams("parallel", "parallel"),
        name="s5_inject",
    )(xr, ws)


def _cmul(ar, ai, br, bi):
    return ar * br - ai * bi, ar * bi + ai * br


def _s5_scan_body(s_ref, lt_ref, h_ref, raw_ref, *, n_ctx, n_lat, bsz):
    d = pl.program_id(1)
    w2 = lt_ref.shape[-1]
    w = w2 // 2
    rows = bsz * S5_SEGS
    seg = lax.broadcasted_iota(jnp.int32, (rows, 1), 0) % S5_SEGS
    is_late = seg != d
    lam_r = lt_ref[:, 0:w]
    lam_i = lt_ref[:, w:w2]
    zero = jnp.zeros((rows, w), F32)
    one = (jnp.ones((1, w), F32), jnp.zeros((1, w), F32))

    def swap_segments(x):
        return jnp.where(seg == 0, pltpu.roll(x, rows - 1, axis=0), pltpu.roll(x, 1, axis=0))

    def phase(c0, n_steps, hin_r, hin_i, write):
        def chunk(k):
            return c0 + jnp.where(d == 0, k, n_steps - 1 - k)

        def step_raw(k, carry):
            hr, hi = carry
            c = chunk(k)
            raw_ref[c - c0, :, 0:w] = hr
            raw_ref[c - c0, :, w:w2] = hi
            nr, ni = _cmul(lam_r, lam_i, hr, hi)
            return nr + s_ref[c, :, 0:w], ni + s_ref[c, :, w:w2]

        er, ei = lax.fori_loop(0, n_steps, step_raw, (zero, zero))
        pr, pi = lax.fori_loop(0, n_steps, lambda k, q: _cmul(lam_r, lam_i, *q), one)
        dr, di = _cmul(pr, pi, hin_r, hin_i)
        first_r = jnp.where(is_late, 0.0, er + dr)
        first_i = jnp.where(is_late, 0.0, ei + di)
        carry_r = jnp.where(is_late, swap_segments(first_r), hin_r)
        carry_i = jnp.where(is_late, swap_segments(first_i), hin_i)
        if write:
            def step_fix(k, q):
                c = chunk(k)
                fr, fi = _cmul(q[0], q[1], carry_r, carry_i)
                h_ref[c - c0, :, 0:w] = (raw_ref[c - c0, :, 0:w] + fr).astype(h_ref.dtype)
                h_ref[c - c0, :, w:w2] = (raw_ref[c - c0, :, w:w2] + fi).astype(h_ref.dtype)
                return _cmul(lam_r, lam_i, q[0], q[1])

            lax.fori_loop(0, n_steps, step_fix, one)
        lr, li = _cmul(pr, pi, carry_r, carry_i)
        last_r = jnp.where(is_late, er + lr, 0.0)
        last_i = jnp.where(is_late, ei + li, 0.0)
        return (jnp.where(is_late, 0.0, swap_segments(last_r)), jnp.where(is_late, 0.0, swap_segments(last_i)))

    hr, hi = zero, zero
    if n_ctx:
        hr, hi = phase(0, n_ctx, hr, hi, False)
    phase(n_ctx, n_lat, hr, hi, True)


def s5_scan(s, lt, n_ctx, n_lat, bsz):
    assert S5_SEGS == 2
    n_c, rows, _ = s.shape
    n_j = lt.shape[1]
    w2 = lt.shape[-1]
    return pl.pallas_call(
        functools.partial(_s5_scan_body, n_ctx=n_ctx, n_lat=n_lat, bsz=bsz),
        grid=(n_j, 2),
        in_specs=[pl.BlockSpec((n_c, rows, w2), lambda j, d: (0, 0, 2 * j + d)),
                  pl.BlockSpec((None, None, 1, w2), lambda j, d: (d, j, 0, 0))],
        out_specs=pl.BlockSpec((None, n_lat, rows, w2), lambda j, d: (d, 0, 0, j)),
        out_shape=jax.ShapeDtypeStruct((2, n_lat, rows, n_j * w2), BF16),
        scratch_shapes=[pltpu.VMEM((max(n_ctx, n_lat), rows, w2), F32)],
        compiler_params=_cparams("parallel", "parallel"),
        name="s5_scan",
    )(s, lt)


def _s5_out_body(x_ref, hf_ref, hb_ref, wu_ref, whf_ref, whb_ref, o_ref):
    o_ref[...] = (_dot(x_ref[...], wu_ref[...]) + _dot(hf_ref[...], whf_ref[...])
                  + _dot(hb_ref[...], whb_ref[...]))


def s5_readout(xr, h, wu, wh):
    r = xr.shape[0]
    n_j, k, n = wu.shape
    w2 = wh.shape[2]
    tm = min(r, 1024)
    return pl.pallas_call(
        _s5_out_body,
        grid=(n_j, r // tm),
        in_specs=[pl.BlockSpec((tm, k), lambda j, i: (i, j)),
                  pl.BlockSpec((None, tm, w2), lambda j, i: (0, i, j)),
                  pl.BlockSpec((None, tm, w2), lambda j, i: (1, i, j)),
                  pl.BlockSpec((None, k, n), lambda j, i: (j, 0, 0)),
                  pl.BlockSpec((None, None, w2, n), lambda j, i: (0, j, 0, 0)),
                  pl.BlockSpec((None, None, w2, n), lambda j, i: (1, j, 0, 0))],
        out_specs=pl.BlockSpec((tm, n), lambda j, i: (i, j)),
        out_shape=jax.ShapeDtypeStruct((r, n_j * n), F32),
        compiler_params=_cparams("parallel", "parallel"),
        name="s5_readout",
    )(xr, h, h, wu, wh, wh)


def _gelu_tanh(x):
    return 0.5 * x * (1.0 + jnp.tanh(math.sqrt(2.0 / math.pi) * (x + 0.044715 * (x * x * x))))


def _s5_glu_body(x_ref, y_ref, g_ref, sh_ref, sc_ref, dk_ref, w1_ref, w2_ref, b1_ref, b2_ref, gt_ref, o_ref):
    x = x_ref[...]
    u = _norm_mod(x, g_ref[...], sh_ref[...], sc_ref[...])
    y = _gelu_tanh(y_ref[...] + dk_ref[...] * u).astype(BF16)
    o = (_dot(y, w1_ref[...]) + b1_ref[...]) * jax.nn.sigmoid(_dot(y, w2_ref[...]) + b2_ref[...])
    o_ref[...] = x + gt_ref[...] * o


def s5_glu(x, y, g, shift, scale, d_skip, w1, b1, w2, b2, gate):
    bsz, seq, d = x.shape
    tm = min(seq, 256)
    row = lambda a: a.reshape(1, d)
    rspec = pl.BlockSpec((1, d), lambda b, i: (0, 0))
    mspec = pl.BlockSpec((None, 1, d), lambda b, i: (b, 0, 0))
    tile = pl.BlockSpec((None, tm, d), lambda b, i: (b, i, 0))
    wspec = pl.BlockSpec((d, d), lambda b, i: (0, 0))
    return pl.pallas_call(
        _s5_glu_body,
        grid=(bsz, seq // tm),
        in_specs=[tile, tile, rspec, mspec, mspec, rspec, wspec, wspec, rspec, rspec, mspec],
        out_specs=tile,
        out_shape=jax.ShapeDtypeStruct((bsz, seq, d), F32),
        compiler_params=_cparams("parallel", "parallel"),
        name="s5_glu",
    )(x, y, row(g), shift, scale, row(d_skip), w1, w2, row(b1), row(b2), gate)


def s5_mix(xl, xc, g, sh_l, sc_l, sh_c, sc_c, gate_l, a_re, a_im, log_step, b_re, b_im, c_re, c_im, d_skip,
           w1, b1, w2, b2):
    bsz, seq, d = xl.shape
    hl = norm_mod(xl, g, sh_l, sc_l)
    hc = norm_mod(xc, g, sh_c, sc_c)
    xr_c = _s5_arrange(hc)
    xr_l = _s5_arrange(hl)
    rows = bsz * S5_SEGS
    n_ctx = xr_c.shape[0] // rows
    n_lat = xr_l.shape[0] // rows
    ws, wu, wh, lt = _s5_operators(a_re, a_im, log_step, b_re, b_im, c_re, c_im)
    s = s5_inject(jnp.concatenate([xr_c, xr_l], axis=0), ws)
    h = s5_scan(s.reshape(n_ctx + n_lat, rows, -1), lt, n_ctx, n_lat, bsz)
    y = s5_readout(xr_l, h.reshape(2, n_lat * rows, -1), wu, wh)
    y = _s5_unarrange(y, bsz)
    return s5_glu(xl, y, g, sh_l, sc_l, d_skip, w1.astype(BF16), b1, w2.astype(BF16), b2, gate_l)


def hyena_mix(x, g, shift, scale, gate, w_in, b_in, conv_w, conv_b, fw1, fb1, fw2, fb2, fw3, freq, skip,
              w_out, b_out):
    seq = x.shape[1]
    if seq >= FFT_MIN_SEQ and (2 * seq) % (2 * FFT_N2) == 0:
        a, dd, _ = hyena_filter_taps(seq, fw1, fb1, fw2, fb2, fw3, freq, F32)
        v, x0 = hyena_in(x, g, shift, scale, w_in, b_in, conv_w, conv_b, F32)
        yg = hyena_conv_fft(v, x0, skip, a, dd)
    else:
        cmat, smat = dft_matrices(seq)
        a, dd, kn = hyena_filter_taps(seq, fw1, fb1, fw2, fb2, fw3, freq, BF16)
        kr, ki = hyena_filter_dft(a, dd, cmat, smat)
        v, x0 = hyena_in(x, g, shift, scale, w_in, b_in, conv_w, conv_b, BF16)
        yg = hyena_conv(v, x0, skip, kr, ki, kn, cmat, smat)
    return mm_residual(yg, w_out, b_out, x, gate)


def kernel(x, c, ctx, c_ctx, ada_w, ada_b, norm_g, final_g, hy_w_in, hy_b_in, hy_conv_w, hy_conv_b, hy_fw1,
           hy_fb1, hy_fw2, hy_fb2, hy_fw3, hy_freq, hy_skip, hy_w_out, hy_b_out, s5_a_re, s5_a_im,
           s5_log_step, s5_b_re, s5_b_im, s5_c_re, s5_c_im, s5_d, s5_w1, s5_b1, s5_w2, s5_b2, moe_wg, moe_bg,
           moe_we, moe_be, moe_w_gate, moe_w_up, moe_w_down):
    bsz, _, d = x.shape
    depth = ada_w.shape[0]
    assert depth == 2 and bsz < SUBLANES
    c_all = jnp.zeros((SUBLANES, d), F32).at[:bsz].set(c).at[bsz].set(c_ctx)
    mods = ada_mod(c_all, ada_w, ada_b)

    def mod_rows(layer, k):
        lat = mods[layer, :bsz, k * d:(k + 1) * d][:, None, :]
        cx = jnp.broadcast_to(mods[layer, bsz, k * d:(k + 1) * d][None, None, :], (bsz, 1, d))
        return lat, cx

    (sh_a, csh_a), (sc_a, csc_a), (gt_a, cgt_a) = mod_rows(0, 0), mod_rows(0, 1), mod_rows(0, 2)
    (sh_f, csh_f), (sc_f, csc_f), (gt_f, cgt_f) = mod_rows(0, 3), mod_rows(0, 4), mod_rows(0, 5)
    hy = (hy_w_in[0].astype(BF16), hy_b_in[0], hy_conv_w[0], hy_conv_b[0], hy_fw1[0], hy_fb1[0], hy_fw2[0],
          hy_fb2[0], hy_fw3[0], hy_freq[0], hy_skip[0], hy_w_out[0].astype(BF16), hy_b_out[0])
    xl = hyena_mix(x, norm_g[0, 0], sh_a, sc_a, gt_a, *hy)
    xc = hyena_mix(ctx, norm_g[0, 0], csh_a, csc_a, cgt_a, *hy)
    xl, xc = hier_moe([(xl, sh_f, sc_f, gt_f), (xc, csh_f, csc_f, cgt_f)], norm_g[0, 1],
                      moe_wg[0], moe_bg[0], moe_we[0], moe_be[0], moe_w_gate, moe_w_up, moe_w_down, 0,
                      final_g, False)

    (sh_a, csh_a), (sc_a, csc_a), (gt_a, _) = mod_rows(1, 0), mod_rows(1, 1), mod_rows(1, 2)
    (sh_f, _), (sc_f, _), (gt_f, _) = mod_rows(1, 3), mod_rows(1, 4), mod_rows(1, 5)
    xl = s5_mix(xl, xc, norm_g[1, 0], sh_a, sc_a, csh_a, csc_a, gt_a, s5_a_re[0], s5_a_im[0], s5_log_step[0],
                s5_b_re[0], s5_b_im[0], s5_c_re[0], s5_c_im[0], s5_d[0], s5_w1[0], s5_b1[0], s5_w2[0], s5_b2[0])
    (out,) = hier_moe([(xl, sh_f, sc_f, gt_f)], norm_g[1, 1], moe_wg[1], moe_bg[1], moe_we[1], moe_be[1],
                      moe_w_gate, moe_w_up, moe_w_down, 1, final_g, True)
    return out
```

```python
import functools
import math

import jax
import jax.numpy as jnp
from jax import lax
from jax.experimental import pallas as pl
from jax.experimental.pallas import tpu as pltpu

F32 = jnp.float32
BF16 = jnp.bfloat16
HIGHEST = lax.Precision.HIGHEST

NORM_EPS = 1e-6
HY_DECAY_TARGET = 1e-2
HY_FAST_PCT = 0.3
HY_SLOW_PCT = 1.5
TOP_K = 2

V7X_VMEM_LIMIT_BYTES = 56 * 1024 * 1024
LANES = 128
SUBLANES = 8
S5_TAU = 8
S5_SEGS = 2
MOE_TM = 256
MOE_BM = 256
MOE_NCH = 12
MOE_STAGE = 4
NEG_BIG = -1e30


def _cparams(*sem):
    return pltpu.CompilerParams(dimension_semantics=sem, vmem_limit_bytes=V7X_VMEM_LIMIT_BYTES)


def _norm_mod(x, g, shift, scale):
    ms = jnp.mean(x * x, axis=-1, keepdims=True)
    return (x * lax.rsqrt(ms + NORM_EPS) * g) * (1.0 + scale) + shift


def _dot(a, b):
    return jnp.dot(a, b, preferred_element_type=F32)


def _ada_body(c_ref, w_ref, b_ref, o_ref):
    x = c_ref[...]
    s = (x * jax.nn.sigmoid(x)).astype(BF16)
    o_ref[...] = _dot(s, w_ref[...].astype(BF16)) + b_ref[...]


def ada_mod(c_all, ada_w, ada_b):
    depth, d, n = ada_w.shape
    tn = min(n, 1024)
    return pl.pallas_call(
        _ada_body,
        grid=(depth, n // tn),
        in_specs=[pl.BlockSpec((SUBLANES, d), lambda l, j: (0, 0)),
                  pl.BlockSpec((None, d, tn), lambda l, j: (l, 0, j)),
                  pl.BlockSpec((None, 1, tn), lambda l, j: (l, 0, j))],
        out_specs=pl.BlockSpec((None, SUBLANES, tn), lambda l, j: (l, 0, j)),
        out_shape=jax.ShapeDtypeStruct((depth, SUBLANES, n), F32),
        compiler_params=_cparams("parallel", "parallel"),
        name="ada_mod",
    )(c_all, ada_w, ada_b.reshape(depth, 1, n))


def _hy_in_body(xp_ref, xm_ref, xn_ref, g_ref, sh_ref, sc_ref,
                w0_ref, w1_ref, w2_ref, b0_ref, b1_ref, b2_ref,
                cw0_ref, cw1_ref, cw2_ref, cb0_ref, cb1_ref, cb2_ref,
                v_ref, x0_ref):
    i = pl.program_id(2)
    ni = pl.num_programs(2)
    tm = xm_ref.shape[0]
    x = jnp.concatenate([xp_ref[...], xm_ref[...], xn_ref[...]], axis=0)
    h = _norm_mod(x, g_ref[...], sh_ref[...], sc_ref[...]).astype(BF16)
    rows = lax.broadcasted_iota(jnp.int32, (tm + 2 * SUBLANES, 1), 0)
    valid = jnp.logical_and(jnp.logical_or(rows >= SUBLANES, i > 0),
                            jnp.logical_or(rows < tm + SUBLANES, i < ni - 1))

    def part(w_ref, b_ref, cw_ref, cb_ref):
        z = jnp.where(valid, _dot(h, w_ref[...]) + b_ref[...], 0.0)
        cw = cw_ref[...]
        zp = pltpu.roll(z, 1, axis=0)[SUBLANES:tm + SUBLANES]
        zn = pltpu.roll(z, tm + 2 * SUBLANES - 1, axis=0)[SUBLANES:tm + SUBLANES]
        return zp * cw[0:1] + z[SUBLANES:tm + SUBLANES] * cw[1:2] + zn * cw[2:3] + cb_ref[...]

    x0 = part(w0_ref, b0_ref, cw0_ref, cb0_ref)
    x1 = part(w1_ref, b1_ref, cw1_ref, cb1_ref)
    v = part(w2_ref, b2_ref, cw2_ref, cb2_ref) * x1
    v_ref[...] = v.astype(v_ref.dtype)
    x0_ref[...] = x0.astype(BF16)


def hyena_in(x, g, shift, scale, w_in, b_in, conv_w, conv_b, v_dtype):
    bsz, seq, d = x.shape
    tm = min(seq, 512)
    tn = min(d, 1024)
    nj = d // tn
    r8 = tm // SUBLANES
    last8 = seq // SUBLANES - 1
    row = lambda a: a.reshape(1, -1)
    wspec = lambda k: pl.BlockSpec((d, tn), lambda j, b, i: (0, k * nj + j))
    rspec = lambda k: pl.BlockSpec((1, tn), lambda j, b, i: (0, k * nj + j))
    cspec = lambda k: pl.BlockSpec((3, tn), lambda j, b, i: (0, k * nj + j))
    mspec = pl.BlockSpec((None, 1, d), lambda j, b, i: (b, 0, 0))
    out_spec = pl.BlockSpec((None, tm, tn), lambda j, b, i: (b, i, j))
    return pl.pallas_call(
        _hy_in_body,
        grid=(nj, bsz, seq // tm),
        in_specs=[pl.BlockSpec((None, SUBLANES, d), lambda j, b, i: (b, jnp.maximum(i * r8 - 1, 0), 0)),
                  pl.BlockSpec((None, tm, d), lambda j, b, i: (b, i, 0)),
                  pl.BlockSpec((None, SUBLANES, d), lambda j, b, i: (b, jnp.minimum((i + 1) * r8, last8), 0)),
                  pl.BlockSpec((1, d), lambda j, b, i: (0, 0)), mspec, mspec,
                  wspec(0), wspec(1), wspec(2), rspec(0), rspec(1), rspec(2),
                  cspec(0), cspec(1), cspec(2), rspec(0), rspec(1), rspec(2)],
        out_specs=[out_spec, out_spec],
        out_shape=[jax.ShapeDtypeStruct((bsz, seq, d), v_dtype), jax.ShapeDtypeStruct((bsz, seq, d), BF16)],
        compiler_params=_cparams("parallel", "parallel", "parallel"),
        name="hyena_in",
    )(x, x, x, row(g), shift, scale, w_in, w_in, w_in, row(b_in), row(b_in), row(b_in),
      conv_w, conv_w, conv_w, row(conv_b), row(conv_b), row(conv_b))


def _dft_tables(seq, blk):
    n = 2 * seq
    s = jnp.arange(seq, dtype=jnp.int32)[None, :]
    fl = jnp.arange(blk, dtype=jnp.int32)[:, None]
    fh = (jnp.arange(seq // blk, dtype=jnp.int32) * blk)[:, None]
    w = 2.0 * math.pi / n
    ang_b = ((fl * s) % n).astype(F32) * w
    ang_a = ((fh * s) % n).astype(F32) * w
    return (jnp.cos(ang_a)[:, None, :], jnp.sin(ang_a)[:, None, :], jnp.cos(ang_b), jnp.sin(ang_b))


def _dft_gen_body(ca_ref, sa_ref, cb_ref, sb_ref, c_ref, s_ref):
    ca, sa, cb, sb = ca_ref[...], sa_ref[...], cb_ref[...], sb_ref[...]
    c_ref[...] = (ca * cb - sa * sb).astype(BF16)
    s_ref[...] = (sa * cb + ca * sb).astype(BF16)


def dft_matrices(seq):
    blk = min(seq, 256)
    ca, sa, cb, sb = _dft_tables(seq, blk)
    aspec = pl.BlockSpec((None, 1, seq), lambda i: (i, 0, 0))
    bspec = pl.BlockSpec((blk, seq), lambda i: (0, 0))
    ospec = pl.BlockSpec((blk, seq), lambda i: (i, 0))
    return pl.pallas_call(
        _dft_gen_body,
        grid=(seq // blk,),
        in_specs=[aspec, aspec, bspec, bspec],
        out_specs=[ospec, ospec],
        out_shape=[jax.ShapeDtypeStruct((seq, seq), BF16)] * 2,
        compiler_params=_cparams("parallel"),
        name="dft_matrices",
    )(ca, sa, cb, sb)


def _alt_sign(rows):
    return jnp.where((rows & 1) == 0, 1.0, -1.0).astype(F32)


def _filt_body(h2_ref, wf_ref, wb_ref, dl_ref, a_ref, d_ref, ny_ref):
    seq = h2_ref.shape[0]
    h2 = h2_ref[...]
    row = lax.broadcasted_iota(jnp.int32, (seq, 1), 0)
    t = row.astype(F32) * (1.0 / (seq - 1))
    win = jnp.exp(-t * dl_ref[...])
    hf = jnp.dot(h2, wf_ref[...], precision=HIGHEST, preferred_element_type=F32) * win
    hb = jnp.dot(h2, wb_ref[...], precision=HIGHEST, preferred_element_type=F32) * win
    hb = jnp.where(row == 0, 0.0, hb)
    nrm = (jnp.sum(jnp.abs(hf), axis=0, keepdims=True) + jnp.sum(jnp.abs(hb), axis=0, keepdims=True))
    inv = 1.0 / nrm
    a = (hf + hb) * inv
    a_ref[...] = a.astype(a_ref.dtype)
    d_ref[...] = ((hb - hf) * inv).astype(d_ref.dtype)
    ny = jnp.sum(a * _alt_sign(row), axis=0, keepdims=True) * (1.0 / (2 * seq))
    ny_ref[...] = jnp.broadcast_to(ny, ny_ref.shape)


def _khat_body(a_ref, d_ref, c_ref, s_ref, kr_ref, ki_ref):
    i = pl.program_id(1)
    tm = c_ref.shape[0]
    seq = c_ref.shape[1]
    f = i * tm + lax.broadcasted_iota(jnp.int32, (tm, 1), 0)
    w = jnp.where(f == 0, 1.0, 2.0).astype(F32) * (1.0 / (2 * seq))
    kr_ref[...] = _dot(c_ref[...], a_ref[...]) * w
    ki_ref[...] = _dot(s_ref[...], d_ref[...]) * w


def hyena_filter_taps(seq, fw1, fb1, fw2, fb2, fw3, freq, taps_dtype):
    d = fw3.shape[1] // 2
    bands_n = (fw1.shape[0] - 1) // 2
    t = jnp.linspace(0.0, 1.0, seq, dtype=F32)[:, None]
    w = (2.0 * math.pi / seq) * jnp.arange(seq, dtype=F32)[:, None]
    bands = jnp.linspace(1e-4, bands_n - 1, bands_n, dtype=F32)[None, :]
    z = jnp.concatenate([t, jnp.cos(bands * w), -jnp.sin(bands * w)], axis=-1)
    h = jnp.sin(freq * (jnp.dot(z, fw1, precision=HIGHEST) + fb1))
    h2 = jnp.sin(freq * (jnp.dot(h, fw2, precision=HIGHEST) + fb2))
    max_decay = math.log(HY_DECAY_TARGET) / HY_FAST_PCT
    min_decay = math.log(HY_DECAY_TARGET) / HY_SLOW_PCT
    deltas = jnp.abs(jnp.linspace(min_decay, max_decay, d, dtype=F32))[None, :]

    order = h2.shape[1]
    tn = min(d, 256)
    nj = d // tn
    return pl.pallas_call(
        _filt_body,
        grid=(nj,),
        in_specs=[pl.BlockSpec((seq, order), lambda j: (0, 0)),
                  pl.BlockSpec((order, tn), lambda j: (0, j)),
                  pl.BlockSpec((order, tn), lambda j: (0, nj + j)),
                  pl.BlockSpec((1, tn), lambda j: (0, j))],
        out_specs=[pl.BlockSpec((seq, tn), lambda j: (0, j)),
                   pl.BlockSpec((seq, tn), lambda j: (0, j)),
                   pl.BlockSpec((SUBLANES, tn), lambda j: (0, j))],
        out_shape=[jax.ShapeDtypeStruct((seq, d), taps_dtype), jax.ShapeDtypeStruct((seq, d), taps_dtype),
                   jax.ShapeDtypeStruct((SUBLANES, d), F32)],
        compiler_params=_cparams("parallel"),
        name="hyena_filter_taps",
    )(h2, fw3, fw3, deltas)


def hyena_filter_dft(a, dd, cmat, smat):
    seq, d = a.shape
    tm = min(seq, 512)
    tn2 = min(d, 512)
    return pl.pallas_call(
        _khat_body,
        grid=(d // tn2, seq // tm),
        in_specs=[pl.BlockSpec((seq, tn2), lambda j, i: (0, j)),
                  pl.BlockSpec((seq, tn2), lambda j, i: (0, j)),
                  pl.BlockSpec((tm, seq), lambda j, i: (i, 0)),
                  pl.BlockSpec((tm, seq), lambda j, i: (i, 0))],
        out_specs=[pl.BlockSpec((tm, tn2), lambda j, i: (i, j))] * 2,
        out_shape=[jax.ShapeDtypeStruct((seq, d), F32)] * 2,
        compiler_params=_cparams("parallel", "parallel"),
        name="hyena_filter_dft",
    )(a, dd, cmat, smat)


def _dft_fwd_body(v_ref, c_ref, s_ref, kr_ref, ki_ref, kn_ref, ya_ref, yb_ref, yn_ref):
    i = pl.program_id(2)
    v = v_ref[...]
    vr = _dot(c_ref[...], v)
    p = _dot(s_ref[...], v)
    kr = kr_ref[...]
    ki = ki_ref[...]
    ya_ref[...] = (vr * kr + p * ki).astype(BF16)
    yb_ref[...] = (p * kr - vr * ki).astype(BF16)

    @pl.when(i == 0)
    def _():
        seq = v.shape[0]
        row = lax.broadcasted_iota(jnp.int32, (seq, 1), 0)
        vl = jnp.sum(v.astype(F32) * _alt_sign(row), axis=0, keepdims=True)
        yn_ref[...] = jnp.broadcast_to(vl * kn_ref[0:1, :], yn_ref.shape)


def _dft_inv_body(ya_ref, yb_ref, c_ref, s_ref, v_ref, x0_ref, skip_ref, yn_ref, o_ref):
    i = pl.program_id(2)
    tm = c_ref.shape[0]
    acc = _dot(c_ref[...], ya_ref[...]) + _dot(s_ref[...], yb_ref[...])
    t = i * tm + lax.broadcasted_iota(jnp.int32, (tm, 1), 0)
    y = acc + _alt_sign(t) * yn_ref[0:1, :] + skip_ref[...] * v_ref[...].astype(F32)
    o_ref[...] = (y * x0_ref[...].astype(F32)).astype(BF16)


def hyena_conv(v, x0, skip, kr, ki, kn, cmat, smat):
    bsz, seq, d = v.shape
    tm = min(seq, 512)
    tn = min(d, 512)
    grid = (bsz, d // tn, seq // tm)
    full = pl.BlockSpec((None, seq, tn), lambda b, j, i: (b, 0, j))
    mat = pl.BlockSpec((tm, seq), lambda b, j, i: (i, 0))
    tile = pl.BlockSpec((None, tm, tn), lambda b, j, i: (b, i, j))
    ktile = pl.BlockSpec((tm, tn), lambda b, j, i: (i, j))
    nyq = pl.BlockSpec((None, SUBLANES, tn), lambda b, j, i: (b, 0, j))
    ya, yb, yn = pl.pallas_call(
        _dft_fwd_body,
        grid=grid,
        in_specs=[full, mat, mat, ktile, ktile, pl.BlockSpec((SUBLANES, tn), lambda b, j, i: (0, j))],
        out_specs=[tile, tile, nyq],
        out_shape=[jax.ShapeDtypeStruct((bsz, seq, d), BF16)] * 2
        + [jax.ShapeDtypeStruct((bsz, SUBLANES, d), F32)],
        compiler_params=_cparams("parallel", "parallel", "arbitrary"),
        name="hyena_dft_fwd",
    )(v, cmat, smat, kr, ki, kn)
    return pl.pallas_call(
        _dft_inv_body,
        grid=grid,
        in_specs=[full, full, mat, mat, tile, tile, pl.BlockSpec((1, tn), lambda b, j, i: (0, j)), nyq],
        out_specs=tile,
        out_shape=jax.ShapeDtypeStruct((bsz, seq, d), BF16),
        compiler_params=_cparams("parallel", "parallel", "parallel"),
        name="hyena_dft_inv",
    )(ya, yb, cmat, smat, v, x0, skip.reshape(1, d), yn)


FFT_N2 = 256
FFT_MIN_SEQ = 1024
FFT_UNROLL = 8


def _fft_matrices(seq):
    n = 2 * seq
    n2 = FFT_N2
    n1 = n // n2
    r8 = SUBLANES
    q = jnp.arange(n2 // r8, dtype=jnp.int32)[:, None, None, None]
    f1 = jnp.arange(n1, dtype=jnp.int32)[None, :, None, None]
    r = jnp.arange(r8, dtype=jnp.int32)[None, None, :, None]
    t1 = jnp.arange(n1 // 2, dtype=jnp.int32)[None, None, None, :]
    ang = ((f1 * (t1 * n2 + q * r8 + r)) % n).astype(F32) * (2.0 * math.pi / n)
    g = jnp.stack([jnp.cos(ang), -jnp.sin(ang)], axis=3)
    eye = jnp.eye(r8, dtype=F32)[None, None, :, None, None, :]
    ma = (g[..., None] * eye).reshape(n2 // r8, n1 * r8 * 2, (n1 // 2) * r8).astype(BF16)
    f2 = jnp.arange(n2, dtype=jnp.int32)[:, None]
    t2 = jnp.arange(n2, dtype=jnp.int32)[None, :]
    th = ((f2 * t2) % n2).astype(F32) * (2.0 * math.pi / n2)
    co, si = jnp.cos(th), jnp.sin(th)
    wc = jnp.stack([jnp.stack([co, si], axis=-1), jnp.stack([-si, co], axis=-1)], axis=0)
    wc = wc.reshape(2 * n2, 2 * n2).astype(BF16)
    return ma, jnp.swapaxes(ma, 1, 2), wc, wc.T


def _fft_stage_a(x_ref, ma_ref, s1):
    n1h, n_q, r8, tn = x_ref.shape
    n1 = s1.shape[0]

    def body(q, carry):
        x = x_ref[:, pl.ds(q, 1), :, :].reshape(n1h * r8, tn).astype(BF16)
        a = _dot(ma_ref[q], x).astype(BF16)
        s1[:, pl.ds(pl.multiple_of(q * 2 * r8, 2 * r8), 2 * r8), :] = a.reshape(n1, 2 * r8, tn)
        return carry

    lax.fori_loop(0, n_q, body, 0, unroll=FFT_UNROLL)


def _fft_conv_body(v_ref, x0_ref, k_ref, skip_ref, ma_ref, mat_ref, wc_ref, wci_ref, o_ref, s1, ysc):
    n1h, n_q, r8, tn = v_ref.shape
    n1 = s1.shape[0]
    n2 = s1.shape[1] // 2
    seq = n1h * n_q * r8
    _fft_stage_a(v_ref, ma_ref, s1)

    def slab(f, carry):
        y = _dot(wc_ref[...], s1[f])
        yr, yi = y[:n2], y[n2:]
        kr = k_ref[f, 0].astype(F32)
        ki = k_ref[f, 1].astype(F32)
        p = jnp.concatenate([yr * kr - yi * ki, yr * ki + yi * kr], axis=0).astype(BF16)
        s1[f] = _dot(wci_ref[...], p).astype(BF16)
        return carry

    lax.fori_loop(0, n1, slab, 0, unroll=FFT_UNROLL)

    def inv_a(q, carry):
        z = s1[:, pl.ds(pl.multiple_of(q * 2 * r8, 2 * r8), 2 * r8), :].reshape(n1 * 2 * r8, tn)
        ysc[:, pl.ds(q, 1), :, :] = _dot(mat_ref[q], z).reshape(n1h, 1, r8, tn)
        return carry

    lax.fori_loop(0, n_q, inv_a, 0, unroll=FFT_UNROLL)
    y = ysc[...].reshape(seq, tn) + skip_ref[...] * v_ref[...].reshape(seq, tn)
    o_ref[...] = (y * x0_ref[...].astype(F32)).astype(BF16)


def _fft_filter_body(a_ref, d_ref, ma_ref, wc_ref, k_ref, s1):
    n1 = s1.shape[0]
    n2 = s1.shape[1] // 2
    scale = 1.0 / (n1 * n2)
    for src_ref, part, sign in ((a_ref, 0, scale), (d_ref, 1, -scale)):
        _fft_stage_a(src_ref, ma_ref, s1)

        def slab(f, carry):
            y = _dot(wc_ref[part * n2:(part + 1) * n2, :], s1[f])
            k_ref[f, part] = (y * sign).astype(BF16)
            return carry

        lax.fori_loop(0, n1, slab, 0, unroll=FFT_UNROLL)


def hyena_conv_fft(v, x0, skip, a, dd):
    bsz, seq, d = v.shape
    n2 = FFT_N2
    n1 = 2 * seq // n2
    n_q = n2 // SUBLANES
    tn = min(d, 256)
    ma, mat, wc, wci = _fft_matrices(seq)
    const = lambda shape: pl.BlockSpec(shape, lambda *_: (0,) * len(shape), pipeline_mode=pl.Buffered(1))
    view = lambda t: t.reshape(t.shape[:-2] + (n1 // 2, n_q, SUBLANES, d))
    tap = pl.BlockSpec((n1 // 2, n_q, SUBLANES, tn), lambda j: (0, 0, 0, j))
    khat = pl.pallas_call(
        _fft_filter_body,
        grid=(d // tn,),
        in_specs=[tap, tap, const(ma.shape), const(wc.shape)],
        out_specs=pl.BlockSpec((n1, 2, n2, tn), lambda j: (0, 0, 0, j)),
        out_shape=jax.ShapeDtypeStruct((n1, 2, n2, d), BF16),
        scratch_shapes=[pltpu.VMEM((n1, 2 * n2, tn), BF16)],
        compiler_params=_cparams("parallel"),
        name="hyena_filter_fft",
    )(view(a), view(dd), ma, wc)
    return pl.pallas_call(
        _fft_conv_body,
        grid=(d // tn, bsz),
        in_specs=[pl.BlockSpec((None, n1 // 2, n_q, SUBLANES, tn), lambda j, b: (b, 0, 0, 0, j)),
                  pl.BlockSpec((None, seq, tn), lambda j, b: (b, 0, j)),
                  pl.BlockSpec((n1, 2, n2, tn), lambda j, b: (0, 0, 0, j), pipeline_mode=pl.Buffered(1)),
                  pl.BlockSpec((1, tn), lambda j, b: (0, j)),
                  const(ma.shape), const(mat.shape), const(wc.shape), const(wci.shape)],
        out_specs=pl.BlockSpec((None, seq, tn), lambda j, b: (b, 0, j)),
        out_shape=jax.ShapeDtypeStruct((bsz, seq, d), BF16),
        scratch_shapes=[pltpu.VMEM((n1, 2 * n2, tn), BF16), pltpu.VMEM((n1 // 2, n_q, SUBLANES, tn), F32)],
        compiler_params=_cparams("parallel", "arbitrary"),
        name="hyena_conv_fft",
    )(view(v), x0, khat, skip.reshape(1, d), ma, mat, wc, wci)


def _mm_res_body(x_ref, w_ref, b_ref, res_ref, gate_ref, o_ref):
    o_ref[...] = res_ref[...] + gate_ref[...] * (_dot(x_ref[...], w_ref[...]) + b_ref[...])


def mm_residual(x, w, b, res, gate):
    bsz, seq, k = x.shape
    n = w.shape[1]
    tm = min(seq, 512)
    return pl.pallas_call(
        _mm_res_body,
        grid=(bsz, seq // tm),
        in_specs=[pl.BlockSpec((None, tm, k), lambda b, i: (b, i, 0)),
                  pl.BlockSpec((k, n), lambda b, i: (0, 0)),
                  pl.BlockSpec((1, n), lambda b, i: (0, 0)),
                  pl.BlockSpec((None, tm, n), lambda b, i: (b, i, 0)),
                  pl.BlockSpec((None, 1, n), lambda b, i: (b, 0, 0))],
        out_specs=pl.BlockSpec((None, tm, n), lambda b, i: (b, i, 0)),
        out_shape=jax.ShapeDtypeStruct((bsz, seq, n), F32),
        compiler_params=_cparams("parallel", "parallel"),
        name="mm_residual",
    )(x, w, b.reshape(1, n), res, gate)


def _moe_pre_body(*refs, n_groups, n_experts, tile_offs):
    n_streams = len(tile_offs) - 1
    g_ref, wr_ref, br_ref, tok_ref, eid_ref, gate_ref = refs[3 * n_streams:]
    i = pl.program_id(0)
    for k in range(n_streams):
        x_ref, sh_ref, sc_ref = refs[3 * k:3 * k + 3]

        @pl.when(jnp.logical_and(i >= tile_offs[k], i < tile_offs[k + 1]))
        def _():
            tok = _norm_mod(x_ref[...], g_ref[...], sh_ref[...], sc_ref[...])
            _route_tokens(tok, wr_ref, br_ref, tok_ref, eid_ref, gate_ref, n_groups, n_experts)


def _route_tokens(tok, wr_ref, br_ref, tok_ref, eid_ref, gate_ref, n_groups, n_experts):
    tok_ref[...] = tok
    t_hi = tok.astype(BF16)
    t_lo = (tok - t_hi.astype(F32)).astype(BF16)
    logits = (_dot(t_hi, wr_ref[0]) + _dot(t_hi, wr_ref[1]) + _dot(t_lo, wr_ref[0])) + br_ref[...]
    lane = lax.broadcasted_iota(jnp.int32, logits.shape, 1)
    per = n_experts // n_groups
    big = jnp.int32(1 << 20)
    gmask = jnp.logical_and(lane >= n_experts, lane < n_experts + n_groups)
    gl = jnp.where(gmask, logits, NEG_BIG)
    gmax = jnp.max(gl, axis=-1, keepdims=True)
    gidx = jnp.min(jnp.where(gl == gmax, lane - n_experts, big), axis=-1, keepdims=True)
    p_top = 1.0 / jnp.sum(jnp.where(gmask, jnp.exp(gl - gmax), 0.0), axis=-1, keepdims=True)
    lo = gidx * per
    emask = jnp.logical_and(lane >= lo, lane < lo + per)
    el = jnp.where(emask, logits, NEG_BIG)
    m1 = jnp.max(el, axis=-1, keepdims=True)
    i1 = jnp.min(jnp.where(el == m1, lane, big), axis=-1, keepdims=True)
    el2 = jnp.where(lane == i1, NEG_BIG, el)
    m2 = jnp.max(el2, axis=-1, keepdims=True)
    i2 = jnp.min(jnp.where(el2 == m2, lane, big), axis=-1, keepdims=True)
    e21 = jnp.exp(m2 - m1)
    g1 = p_top / (1.0 + e21)
    g2 = g1 * e21
    eid_ref[...] = jnp.where(lane == 0, i1, jnp.where(lane == 1, i2, 0))
    gate_ref[...] = jnp.where(lane == 0, g1, jnp.where(lane == 1, g2, 0.0))


def moe_pre(streams, g, wr, br, n_groups, n_experts):
    d = streams[0][0].shape[2]
    tm = MOE_TM
    tile_offs = [0]
    in_specs, args = [], []
    for x, shift, scale in streams:
        bsz, seq, _ = x.shape
        nt = seq // tm
        n_tiles = bsz * nt
        off = tile_offs[-1]
        tile_offs.append(off + n_tiles)

        def tile(i, off=off, n_tiles=n_tiles):
            return jnp.clip(i - off, 0, n_tiles - 1)

        in_specs += [pl.BlockSpec((None, tm, d), lambda i, tile=tile, nt=nt: (tile(i) // nt, tile(i) % nt, 0)),
                     pl.BlockSpec((None, 1, d), lambda i, tile=tile, nt=nt: (tile(i) // nt, 0, 0)),
                     pl.BlockSpec((None, 1, d), lambda i, tile=tile, nt=nt: (tile(i) // nt, 0, 0))]
        args += [x, shift, scale]
    in_specs += [pl.BlockSpec((1, d), lambda i: (0, 0)),
                 pl.BlockSpec((2, d, LANES), lambda i: (0, 0, 0)),
                 pl.BlockSpec((1, LANES), lambda i: (0, 0))]
    wr_hi = wr.astype(BF16)
    wr_split = jnp.stack([wr_hi, (wr - wr_hi.astype(F32)).astype(BF16)])
    args += [g.reshape(1, d), wr_split, br]
    total = tile_offs[-1] * tm
    rout = pl.BlockSpec((tm, LANES), lambda i: (i, 0))
    tok, eid, gate = pl.pallas_call(
        functools.partial(_moe_pre_body, n_groups=n_groups, n_experts=n_experts, tile_offs=tuple(tile_offs)),
        grid=(tile_offs[-1],),
        in_specs=in_specs,
        out_specs=[pl.BlockSpec((tm, d), lambda i: (i, 0)), rout, rout],
        out_shape=[jax.ShapeDtypeStruct((total, d), F32),
                   jax.ShapeDtypeStruct((total, LANES), jnp.int32),
                   jax.ShapeDtypeStruct((total, LANES), F32)],
        compiler_params=_cparams("parallel"),
        name="moe_pre",
    )(*args)
    return tok, eid, gate, tile_offs[:-1]


def _start_row_gather(idx_ref, n_rows, stride, offset, src_hbm, dst_vmem, sem):
    def body(g, c):
        r0 = pl.multiple_of(g * SUBLANES, SUBLANES)
        dst_tile = dst_vmem.at[pl.ds(r0, SUBLANES)]
        for k in range(SUBLANES):
            pltpu.make_async_copy(src_hbm.at[pl.ds(idx_ref[stride * (r0 + k) + offset], 1)],
                                  dst_tile.at[pl.ds(k, 1)], sem).start()
        return c

    lax.fori_loop(0, n_rows // SUBLANES, body, 0, unroll=2)


def _wait_row_gather(n_rows, src_hbm, dst_vmem, sem):
    pltpu.make_async_copy(src_hbm.at[pl.ds(0, n_rows)], dst_vmem, sem).wait()


def _expert_body(bv_ref, rk_ref, pe_ref, tot_ref, src_ref, nxt_ref, tok_ref, wg_hbm, wu_hbm, wd_hbm, o_ref,
                 xbuf, xsem, wcache, stg, wsem, cnt, *, layer):
    i = pl.program_id(0)
    n = pl.num_programs(0)
    slot = i % 2
    cr, cc = stg.shape[1:]
    total = tot_ref[0]
    mats_hbm = (wg_hbm, wu_hbm, wd_hbm)

    @pl.when(i == 0)
    def _():
        cnt[0] = 0
        cnt[1] = 0

    @pl.when(jnp.logical_and(i == 0, bv_ref[0] > 0))
    def _():
        _start_row_gather(src_ref, MOE_BM, 1, 0, tok_ref, xbuf.at[0], xsem.at[0])

    @pl.when(jnp.logical_and(i + 1 < n, bv_ref[jnp.minimum(i + 1, n - 1)] > 0))
    def _():
        _start_row_gather(nxt_ref, MOE_BM, 1, 0, tok_ref, xbuf.at[1 - slot], xsem.at[1 - slot])

    def chunk_geom(c):
        q = c % MOE_NCH
        m = q // 4
        sub = q % 4
        r0 = jnp.where(m < 2, sub, sub // 2) * cr
        c0 = jnp.where(m < 2, 0, sub % 2) * cc
        return m, pl.multiple_of(r0, cr), pl.multiple_of(c0, cc)

    def issue(c):
        e = pe_ref[c // MOE_NCH]
        m, r0, c0 = chunk_geom(c)
        s = c % MOE_STAGE
        for k, w_hbm in enumerate(mats_hbm):
            @pl.when(m == k)
            def _():
                pltpu.make_async_copy(w_hbm.at[layer, e, pl.ds(r0, cr), pl.ds(c0, cc)], stg.at[s],
                                      wsem.at[s]).start()

    def cast(c):
        s = c % MOE_STAGE
        pltpu.make_async_copy(wg_hbm.at[layer, 0, pl.ds(0, cr), pl.ds(0, cc)], stg.at[s], wsem.at[s]).wait()
        wcache[(c // MOE_NCH) % 2, c % MOE_NCH] = stg[s].astype(BF16)

    valid = bv_ref[i] > 0
    rank = rk_ref[i]
    issued = cnt[0]
    done = cnt[1]
    limit = jnp.minimum(total, MOE_NCH * (rank + 2))
    need = jnp.where(valid, MOE_NCH * (rank + 1), done)

    def fill(issued, done):
        hi = jnp.minimum(limit, done + MOE_STAGE)

        def body(c, carry):
            issue(c)
            return carry

        lax.fori_loop(issued, hi, body, 0)
        return jnp.maximum(issued, hi)

    def cast_and_refill(c, issued):
        cast(c)
        more = issued < jnp.minimum(limit, c + 1 + MOE_STAGE)

        @pl.when(more)
        def _():
            issue(issued)

        return issued + more.astype(jnp.int32)

    issued = fill(issued, done)
    issued = lax.fori_loop(done, need, cast_and_refill, issued)
    done = jnp.maximum(done, need)

    @pl.when(valid)
    def _():
        ws = rank % 2
        _wait_row_gather(MOE_BM, tok_ref, xbuf.at[slot], xsem.at[slot])
        x = xbuf[slot].astype(BF16)
        gate = sum(_dot(x[:, k * cr:(k + 1) * cr], wcache[ws, k]) for k in range(4))
        up = sum(_dot(x[:, k * cr:(k + 1) * cr], wcache[ws, 4 + k]) for k in range(4))
        h = (gate * jax.nn.sigmoid(gate) * up).astype(BF16)
        for half in range(2):
            o_ref[:, half * cc:(half + 1) * cc] = sum(
                _dot(h[:, k * cr:(k + 1) * cr], wcache[ws, 8 + 2 * k + half]) for k in range(2))

    @pl.when(jnp.logical_not(valid))
    def _():
        o_ref[...] = jnp.zeros_like(o_ref)

    fetched = issued
    issued = lax.fori_loop(done, fetched, cast_and_refill, issued)
    done = jnp.maximum(done, fetched)
    last = i == n - 1
    tail = jnp.where(last, issued, done)

    def drain(c, carry):
        cast(c)
        return carry

    lax.fori_loop(done, tail, drain, 0)
    cnt[0] = issued
    cnt[1] = jnp.maximum(done, tail)


def moe_experts(tok, src_tok, block_valid, block_rank, present, n_chunks, w_gate, w_up, w_down, layer):
    d = tok.shape[1]
    n_rows = src_tok.shape[0]
    n_blocks = n_rows // MOE_BM
    dh = w_gate.shape[3]
    assert 2 * dh == d and MOE_NCH == 12
    cr, cc = d // 4, dh
    any_spec = pl.BlockSpec(memory_space=pl.ANY)
    grid_spec = pltpu.PrefetchScalarGridSpec(
        num_scalar_prefetch=4,
        grid=(n_blocks,),
        in_specs=[pl.BlockSpec((MOE_BM,), lambda i, *_: (i,), memory_space=pltpu.SMEM),
                  pl.BlockSpec((MOE_BM,), lambda i, *_: (jnp.minimum(i + 1, n_blocks - 1),),
                               memory_space=pltpu.SMEM),
                  any_spec, any_spec, any_spec, any_spec],
        out_specs=pl.BlockSpec((MOE_BM, d), lambda i, *_: (i, 0)),
        scratch_shapes=[pltpu.VMEM((2, MOE_BM, d), F32), pltpu.SemaphoreType.DMA((2,)),
                        pltpu.VMEM((2, MOE_NCH, cr, cc), BF16),
                        pltpu.VMEM((MOE_STAGE, cr, cc), F32), pltpu.SemaphoreType.DMA((MOE_STAGE,)),
                        pltpu.SMEM((2,), jnp.int32)],
    )
    return pl.pallas_call(
        functools.partial(_expert_body, layer=layer),
        grid_spec=grid_spec,
        out_shape=jax.ShapeDtypeStruct((n_rows, d), F32),
        compiler_params=_cparams("arbitrary"),
        name="moe_experts",
    )(block_valid, block_rank, present, n_chunks, src_tok, src_tok, tok, w_gate, w_up, w_down)


def _combine_body(dest_ref, nxt_ref, os_ref, gate_ref, res_ref, gt_ref, fg_ref, o_ref, buf, sem, *, final_norm):
    rows = res_ref.shape[0]
    i = pl.program_id(0)
    n = pl.num_programs(0)
    slot = i % 2

    def start(idx_ref, s):
        for k in range(TOP_K):
            _start_row_gather(idx_ref, rows, TOP_K, k, os_ref, buf.at[s, k], sem.at[s])

    @pl.when(i == 0)
    def _():
        start(dest_ref, 0)

    @pl.when(i + 1 < n)
    def _():
        start(nxt_ref, 1 - slot)

    for k in range(TOP_K):
        _wait_row_gather(rows, os_ref, buf.at[slot, k], sem.at[slot])
    gates = gate_ref[...]
    mo = gates[:, 0:1] * buf[slot, 0] + gates[:, 1:2] * buf[slot, 1]
    y = res_ref[...] + gt_ref[...] * mo
    if final_norm:
        ms = jnp.mean(y * y, axis=-1, keepdims=True)
        y = y * lax.rsqrt(ms + NORM_EPS) * fg_ref[...]
    o_ref[...] = y


def moe_combine(os, dest, gates, tile0, res, gt, final_g, final_norm):
    bsz, seq, d = res.shape
    rows = MOE_TM
    nt = seq // rows
    n = bsz * nt
    tile = pl.BlockSpec((None, rows, d), lambda i: (i // nt, i % nt, 0))
    return pl.pallas_call(
        functools.partial(_combine_body, final_norm=final_norm),
        grid=(n,),
        in_specs=[pl.BlockSpec((rows * TOP_K,), lambda i: (tile0 + i,), memory_space=pltpu.SMEM),
                  pl.BlockSpec((rows * TOP_K,), lambda i: (tile0 + jnp.minimum(i + 1, n - 1),),
                               memory_space=pltpu.SMEM),
                  pl.BlockSpec(memory_space=pl.ANY),
                  pl.BlockSpec((rows, LANES), lambda i: (tile0 + i, 0)),
                  tile,
                  pl.BlockSpec((None, 1, d), lambda i: (i // nt, 0, 0)),
                  pl.BlockSpec((1, d), lambda i: (0, 0))],
        out_specs=tile,
        out_shape=jax.ShapeDtypeStruct((bsz, seq, d), F32),
        scratch_shapes=[pltpu.VMEM((2, TOP_K, rows, d), F32), pltpu.SemaphoreType.DMA((2,))],
        compiler_params=_cparams("arbitrary"),
        name="moe_combine",
    )(dest, dest, os, gates, res, gt, final_g.reshape(1, d))


def _route_plan(eid, n_experts):
    e_flat = eid.reshape(-1)
    a = e_flat.shape[0]
    order = jnp.argsort(e_flat)
    pos = jnp.argsort(order)
    experts = jnp.arange(n_experts, dtype=jnp.int32)
    counts = jnp.sum((e_flat[:, None] == experts[None, :]).astype(jnp.int32), axis=0)
    start = jnp.cumsum(counts) - counts
    padded = (counts + MOE_BM - 1) // MOE_BM * MOE_BM
    pad_end = jnp.cumsum(padded)
    pad_start = pad_end - padded
    dest = (pad_start - start)[e_flat] + pos
    n_blocks = -(-a // MOE_BM) + n_experts
    starts = jnp.arange(n_blocks, dtype=jnp.int32) * MOE_BM
    block_expert = jnp.minimum(jnp.sum((pad_end[None, :] <= starts[:, None]).astype(jnp.int32), axis=1),
                               n_experts - 1)
    block_valid = (starts < pad_end[-1]).astype(jnp.int32)
    lane = jnp.arange(MOE_BM, dtype=jnp.int32)[None, :]
    blk_off = (starts - pad_start[block_expert])[:, None] + lane
    live = jnp.logical_and(blk_off < counts[block_expert][:, None], block_valid[:, None] > 0)
    sorted_pos = jnp.clip(start[block_expert][:, None] + blk_off, 0, a - 1)
    src_tok = jnp.where(live, order[sorted_pos] // TOP_K, 0).reshape(-1)
    has = (counts > 0).astype(jnp.int32)
    block_rank = (jnp.cumsum(has) - 1)[block_expert]
    present = jnp.argsort(1 - has)
    n_chunks = (MOE_NCH * jnp.sum(has)).reshape(1)
    i32 = lambda v: v.astype(jnp.int32)
    return i32(dest), i32(src_tok), block_valid, i32(block_rank), i32(present), i32(n_chunks)


def hier_moe(streams, norm_g, wg, bg, we, be, w_gate, w_up, w_down, layer, final_g, final_norm):
    n_groups = wg.shape[1]
    n_experts = we.shape[1]
    d = wg.shape[0]
    wr = jnp.zeros((d, LANES), F32).at[:, :n_experts].set(we).at[:, n_experts:n_experts + n_groups].set(wg)
    br = jnp.zeros((1, LANES), F32).at[0, :n_experts].set(be).at[0, n_experts:n_experts + n_groups].set(bg)
    tok, eid, gates, tile0s = moe_pre([s[:3] for s in streams], norm_g, wr, br, n_groups, n_experts)
    dest, src_tok, block_valid, block_rank, present, n_chunks = _route_plan(eid[:, :TOP_K], n_experts)
    os = moe_experts(tok, src_tok, block_valid, block_rank, present, n_chunks, w_gate, w_up, w_down, layer)
    return [moe_combine(os, dest, gates, tile0, x, gt, final_g, final_norm)
            for (x, _, _, gt), tile0 in zip(streams, tile0s)]


def _norm_mod_body(x_ref, g_ref, sh_ref, sc_ref, o_ref):
    o_ref[...] = _norm_mod(x_ref[...], g_ref[...], sh_ref[...], sc_ref[...]).astype(o_ref.dtype)


def norm_mod(x, g, shift, scale):
    bsz, seq, d = x.shape
    tm = min(seq, 512)
    mspec = pl.BlockSpec((None, 1, d), lambda b, i: (b, 0, 0))
    return pl.pallas_call(
        _norm_mod_body,
        grid=(bsz, seq // tm),
        in_specs=[pl.BlockSpec((None, tm, d), lambda b, i: (b, i, 0)),
                  pl.BlockSpec((1, d), lambda b, i: (0, 0)), mspec, mspec],
        out_specs=pl.BlockSpec((None, tm, d), lambda b, i: (b, i, 0)),
        out_shape=jax.ShapeDtypeStruct((bsz, seq, d), BF16),
        compiler_params=_cparams("parallel", "parallel"),
        name="norm_mod",
    )(x, g.reshape(1, d), shift, scale)


def _s5_arrange(h):
    bsz, t, d = h.shape
    c = t // (S5_SEGS * S5_TAU)
    h = h.reshape(bsz, S5_SEGS, c, S5_TAU, d // LANES, LANES)
    return h.transpose(2, 0, 1, 4, 3, 5).reshape(c * bsz * S5_SEGS, d * S5_TAU)


def _s5_unarrange(y, bsz):
    r, w = y.shape
    d = w // S5_TAU
    c = r // (bsz * S5_SEGS)
    y = y.reshape(c, bsz, S5_SEGS, d // LANES, S5_TAU, LANES)
    return y.transpose(1, 2, 0, 4, 3, 5).reshape(bsz, S5_SEGS * c * S5_TAU, d)


def _s5_operators(a_re, a_im, log_step, b_re, b_im, c_re, c_im):
    n_g, n_p = a_re.shape[1:]
    n_h = b_re.shape[-1]
    gpt = LANES // n_h
    n_j = n_g // gpt
    tau = S5_TAU
    assert tau * n_h == LANES and 2 * n_p == LANES
    lam_step = lax.complex(a_re, a_im) * jnp.exp(log_step)[..., None]
    lam_bar = jnp.exp(lam_step)
    b_bar = ((lam_bar - 1.0) / lax.complex(a_re, a_im))[..., None] * lax.complex(b_re, b_im)
    c_mat = lax.complex(c_re, c_im)
    ks = jnp.arange(tau + 1, dtype=F32)[None, :, None, None]
    pw = jnp.exp(lam_step[:, None] * ks)
    ein = functools.partial(jnp.einsum, precision=HIGHEST)
    inj_c, cl_c, lt = [], [], []
    tz_c = 0.0
    for d in range(2):
        pos = jnp.arange(tau) if d == 0 else jnp.arange(tau)[::-1]
        inj = (pw[d][tau - 1 - pos][..., None] * b_bar[d][None]).reshape(tau, n_j, gpt, n_p, n_h)
        inj = inj.transpose(1, 0, 2, 4, 3).reshape(n_j, tau * LANES, n_p)
        inj_c.append(jnp.concatenate([inj.real, inj.imag], axis=-1))
        cl = (c_mat[d][None] * pw[d][pos + 1][:, :, None, :]).reshape(tau, n_j, gpt, n_h, n_p)
        cl = cl.transpose(1, 2, 4, 0, 3).reshape(n_j, gpt * n_p, tau * n_h)
        cl_c.append(jnp.concatenate([cl.real, -cl.imag], axis=1))
        mk = ein('gop,kgp,gph->kgoh', c_mat[d], pw[d][:tau], b_bar[d]).real
        diff = pos[:, None] - pos[None, :]
        tz = jnp.where((diff >= 0)[:, :, None, None, None], mk[jnp.clip(diff, 0, tau - 1)], 0.0)
        tz = tz.reshape(tau, tau, n_j, gpt, n_h, n_h)
        tz_c = tz_c + tz.transpose(2, 1, 3, 5, 0, 4).reshape(n_j, tau * LANES, tau * n_h)
        lt_d = pw[d][tau].reshape(n_j, 1, gpt * n_p)
        lt.append(jnp.concatenate([lt_d.real, lt_d.imag], axis=-1))
    ws, wu, wh = s5_expand(jnp.stack(inj_c).astype(BF16), jnp.stack(cl_c).astype(BF16), tz_c.astype(BF16),
                           n_h, n_p)
    return ws, wu, wh, jnp.stack(lt).astype(F32)


def _s5_expand_body(inj_ref, cl_ref, tz_ref, ws_ref, wu_ref, wh_ref, *, n_h, n_p):
    rows = tz_ref.shape[0]
    gpt = LANES // n_h
    row = lax.broadcasted_iota(jnp.int32, (rows, LANES), 0)
    lane = lax.broadcasted_iota(jnp.int32, (rows, LANES), 1)
    sel_r = lax.broadcasted_iota(jnp.int32, (LANES, LANES), 0)
    sel_l = lax.broadcasted_iota(jnp.int32, (LANES, LANES), 1)
    grp_in = (row // n_h) % gpt
    grp_st = (row // n_p) % gpt

    def spread_out(m, t, grp_row):
        sel = jnp.logical_and(sel_r // n_h == t, sel_r % n_h == sel_l % n_h).astype(BF16)
        return jnp.where(grp_row == lane // n_h, _dot(m, sel), 0.0).astype(BF16)

    def spread_state(m, c, q, grp_row):
        sel = jnp.logical_and(sel_r // n_p == c, sel_r % n_p == sel_l % n_p).astype(BF16)
        return jnp.where(grp_row == (LANES // n_p) * q + lane // n_p, _dot(m, sel), 0.0).astype(BF16)

    w2 = 2 * gpt * n_p
    tz = tz_ref[...]
    for t in range(S5_TAU):
        wu_ref[:, t * LANES:(t + 1) * LANES] = spread_out(tz, t, grp_in)
    for d in range(2):
        cl = cl_ref[d]
        inj = inj_ref[d]
        for t in range(S5_TAU):
            wh_ref[d, :, t * LANES:(t + 1) * LANES] = spread_out(cl, t, grp_st)
        for c in range(2):
            for q in range(gpt * n_p // LANES):
                lo = d * w2 + c * gpt * n_p + q * LANES
                ws_ref[:, lo:lo + LANES] = spread_state(inj, c, q, grp_in)


def s5_expand(inj_c, cl_c, tz_c, n_h, n_p):
    n_j, rows, _ = tz_c.shape
    gpt = LANES // n_h
    w2 = 2 * gpt * n_p
    assert rows == S5_TAU * LANES == w2
    cspec = pl.BlockSpec((2, None, rows, LANES), lambda j: (0, j, 0, 0))
    return pl.pallas_call(
        functools.partial(_s5_expand_body, n_h=n_h, n_p=n_p),
        grid=(n_j,),
        in_specs=[cspec, cspec, pl.BlockSpec((None, rows, LANES), lambda j: (j, 0, 0))],
        out_specs=[pl.BlockSpec((None, rows, 2 * w2), lambda j: (j, 0, 0)),
                   pl.BlockSpec((None, rows, rows), lambda j: (j, 0, 0)),
                   pl.BlockSpec((2, None, w2, rows), lambda j: (0, j, 0, 0))],
        out_shape=[jax.ShapeDtypeStruct((n_j, rows, 2 * w2), BF16),
                   jax.ShapeDtypeStruct((n_j, rows, rows), BF16),
                   jax.ShapeDtypeStruct((2, n_j, w2, rows), BF16)],
        compiler_params=_cparams("parallel"),
        name="s5_expand",
    )(inj_c, cl_c, tz_c)


def _s5_inj_body(x_ref, w_ref, o_ref):
    o_ref[...] = _dot(x_ref[...], w_ref[...])


def s5_inject(xr, ws):
    r = xr.shape[0]
    n_j, k, n = ws.shape
    tm = r // 2 if r % 32 == 0 else r
    return pl.pallas_call(
        _s5_inj_body,
        grid=(n_j, r // tm),
        in_specs=[pl.BlockSpec((tm, k), lambda j, i: (i, j)),
                  pl.BlockSpec((None, k, n), lambda j, i: (j, 0, 0))],
        out_specs=pl.BlockSpec((tm, n), lambda j, i: (i, j)),
        out_shape=jax.ShapeDtypeStruct((r, n_j * n), F32),
        compiler_params=_cparams("parallel", "parallel"),
        name="s5_inject",
    )(xr, ws)


def _cmul(ar, ai, br, bi):
    return ar * br - ai * bi, ar * bi + ai * br


def _s5_scan_body(s_ref, lt_ref, h_ref, raw_ref, *, n_ctx, n_lat, bsz):
    d = pl.program_id(1)
    w2 = lt_ref.shape[-1]
    w = w2 // 2
    rows = bsz * S5_SEGS
    seg = lax.broadcasted_iota(jnp.int32, (rows, 1), 0) % S5_SEGS
    is_late = seg != d
    lam_r = lt_ref[:, 0:w]
    lam_i = lt_ref[:, w:w2]
    zero = jnp.zeros((rows, w), F32)
    one = (jnp.ones((1, w), F32), jnp.zeros((1, w), F32))

    def swap_segments(x):
        return jnp.where(seg == 0, pltpu.roll(x, rows - 1, axis=0), pltpu.roll(x, 1, axis=0))

    def phase(c0, n_steps, hin_r, hin_i, write):
        def chunk(k):
            return c0 + jnp.where(d == 0, k, n_steps - 1 - k)

        def step_raw(k, carry):
            hr, hi = carry
            c = chunk(k)
            raw_ref[c - c0, :, 0:w] = hr
            raw_ref[c - c0, :, w:w2] = hi
            nr, ni = _cmul(lam_r, lam_i, hr, hi)
            return nr + s_ref[c, :, 0:w], ni + s_ref[c, :, w:w2]

        er, ei = lax.fori_loop(0, n_steps, step_raw, (zero, zero))
        pr, pi = lax.fori_loop(0, n_steps, lambda k, q: _cmul(lam_r, lam_i, *q), one)
        dr, di = _cmul(pr, pi, hin_r, hin_i)
        first_r = jnp.where(is_late, 0.0, er + dr)
        first_i = jnp.where(is_late, 0.0, ei + di)
        carry_r = jnp.where(is_late, swap_segments(first_r), hin_r)
        carry_i = jnp.where(is_late, swap_segments(first_i), hin_i)
        if write:
            def step_fix(k, q):
                c = chunk(k)
                fr, fi = _cmul(q[0], q[1], carry_r, carry_i)
                h_ref[c - c0, :, 0:w] = (raw_ref[c - c0, :, 0:w] + fr).astype(h_ref.dtype)
                h_ref[c - c0, :, w:w2] = (raw_ref[c - c0, :, w:w2] + fi).astype(h_ref.dtype)
                return _cmul(lam_r, lam_i, q[0], q[1])

            lax.fori_loop(0, n_steps, step_fix, one)
        lr, li = _cmul(pr, pi, carry_r, carry_i)
        last_r = jnp.where(is_late, er + lr, 0.0)
        last_i = jnp.where(is_late, ei + li, 0.0)
        return (jnp.where(is_late, 0.0, swap_segments(last_r)), jnp.where(is_late, 0.0, swap_segments(last_i)))

    hr, hi = zero, zero
    if n_ctx:
        hr, hi = phase(0, n_ctx, hr, hi, False)
    phase(n_ctx, n_lat, hr, hi, True)


def s5_scan(s, lt, n_ctx, n_lat, bsz):
    assert S5_SEGS == 2
    n_c, rows, _ = s.shape
    n_j = lt.shape[1]
    w2 = lt.shape[-1]
    return pl.pallas_call(
        functools.partial(_s5_scan_body, n_ctx=n_ctx, n_lat=n_lat, bsz=bsz),
        grid=(n_j, 2),
        in_specs=[pl.BlockSpec((n_c, rows, w2), lambda j, d: (0, 0, 2 * j + d)),
                  pl.BlockSpec((None, None, 1, w2), lambda j, d: (d, j, 0, 0))],
        out_specs=pl.BlockSpec((None, n_lat, rows, w2), lambda j, d: (d, 0, 0, j)),
        out_shape=jax.ShapeDtypeStruct((2, n_lat, rows, n_j * w2), BF16),
        scratch_shapes=[pltpu.VMEM((max(n_ctx, n_lat), rows, w2), F32)],
        compiler_params=_cparams("parallel", "parallel"),
        name="s5_scan",
    )(s, lt)


def _s5_out_body(x_ref, hf_ref, hb_ref, wu_ref, whf_ref, whb_ref, o_ref):
    o_ref[...] = (_dot(x_ref[...], wu_ref[...]) + _dot(hf_ref[...], whf_ref[...])
                  + _dot(hb_ref[...], whb_ref[...]))


def s5_readout(xr, h, wu, wh):
    r = xr.shape[0]
    n_j, k, n = wu.shape
    w2 = wh.shape[2]
    tm = min(r, 1024)
    return pl.pallas_call(
        _s5_out_body,
        grid=(n_j, r // tm),
        in_specs=[pl.BlockSpec((tm, k), lambda j, i: (i, j)),
                  pl.BlockSpec((None, tm, w2), lambda j, i: (0, i, j)),
                  pl.BlockSpec((None, tm, w2), lambda j, i: (1, i, j)),
                  pl.BlockSpec((None, k, n), lambda j, i: (j, 0, 0)),
                  pl.BlockSpec((None, None, w2, n), lambda j, i: (0, j, 0, 0)),
                  pl.BlockSpec((None, None, w2, n), lambda j, i: (1, j, 0, 0))],
        out_specs=pl.BlockSpec((tm, n), lambda j, i: (i, j)),
        out_shape=jax.ShapeDtypeStruct((r, n_j * n), F32),
        compiler_params=_cparams("parallel", "parallel"),
        name="s5_readout",
    )(xr, h, h, wu, wh, wh)


def _gelu_tanh(x):
    return 0.5 * x * (1.0 + jnp.tanh(math.sqrt(2.0 / math.pi) * (x + 0.044715 * (x * x * x))))


def _s5_glu_body(x_ref, y_ref, g_ref, sh_ref, sc_ref, dk_ref, w1_ref, w2_ref, b1_ref, b2_ref, gt_ref, o_ref):
    x = x_ref[...]
    u = _norm_mod(x, g_ref[...], sh_ref[...], sc_ref[...])
    y = _gelu_tanh(y_ref[...] + dk_ref[...] * u).astype(BF16)
    o = (_dot(y, w1_ref[...]) + b1_ref[...]) * jax.nn.sigmoid(_dot(y, w2_ref[...]) + b2_ref[...])
    o_ref[...] = x + gt_ref[...] * o


def s5_glu(x, y, g, shift, scale, d_skip, w1, b1, w2, b2, gate):
    bsz, seq, d = x.shape
    tm = min(seq, 512)
    row = lambda a: a.reshape(1, d)
    rspec = pl.BlockSpec((1, d), lambda b, i: (0, 0))
    mspec = pl.BlockSpec((None, 1, d), lambda b, i: (b, 0, 0))
    tile = pl.BlockSpec((None, tm, d), lambda b, i: (b, i, 0))
    wspec = pl.BlockSpec((d, d), lambda b, i: (0, 0), pipeline_mode=pl.Buffered(1))
    return pl.pallas_call(
        _s5_glu_body,
        grid=(bsz, seq // tm),
        in_specs=[tile, tile, rspec, mspec, mspec, rspec, wspec, wspec, rspec, rspec, mspec],
        out_specs=tile,
        out_shape=jax.ShapeDtypeStruct((bsz, seq, d), F32),
        compiler_params=_cparams("parallel", "parallel"),
        name="s5_glu",
    )(x, y, row(g), shift, scale, row(d_skip), w1, w2, row(b1), row(b2), gate)


def s5_mix(xl, xc, g, sh_l, sc_l, sh_c, sc_c, gate_l, a_re, a_im, log_step, b_re, b_im, c_re, c_im, d_skip,
           w1, b1, w2, b2):
    bsz, seq, d = xl.shape
    hl = norm_mod(xl, g, sh_l, sc_l)
    hc = norm_mod(xc, g, sh_c, sc_c)
    xr_c = _s5_arrange(hc)
    xr_l = _s5_arrange(hl)
    rows = bsz * S5_SEGS
    n_ctx = xr_c.shape[0] // rows
    n_lat = xr_l.shape[0] // rows
    ws, wu, wh, lt = _s5_operators(a_re, a_im, log_step, b_re, b_im, c_re, c_im)
    s = s5_inject(jnp.concatenate([xr_c, xr_l], axis=0), ws)
    h = s5_scan(s.reshape(n_ctx + n_lat, rows, -1), lt, n_ctx, n_lat, bsz)
    y = s5_readout(xr_l, h.reshape(2, n_lat * rows, -1), wu, wh)
    y = _s5_unarrange(y, bsz)
    return s5_glu(xl, y, g, sh_l, sc_l, d_skip, w1.astype(BF16), b1, w2.astype(BF16), b2, gate_l)


def hyena_mix(x, g, shift, scale, gate, w_in, b_in, conv_w, conv_b, fw1, fb1, fw2, fb2, fw3, freq, skip,
              w_out, b_out):
    seq = x.shape[1]
    if seq >= FFT_MIN_SEQ and (2 * seq) % (2 * FFT_N2) == 0:
        a, dd, _ = hyena_filter_taps(seq, fw1, fb1, fw2, fb2, fw3, freq, F32)
        v, x0 = hyena_in(x, g, shift, scale, w_in, b_in, conv_w, conv_b, F32)
        yg = hyena_conv_fft(v, x0, skip, a, dd)
    else:
        cmat, smat = dft_matrices(seq)
        a, dd, kn = hyena_filter_taps(seq, fw1, fb1, fw2, fb2, fw3, freq, BF16)
        kr, ki = hyena_filter_dft(a, dd, cmat, smat)
        v, x0 = hyena_in(x, g, shift, scale, w_in, b_in, conv_w, conv_b, BF16)
        yg = hyena_conv(v, x0, skip, kr, ki, kn, cmat, smat)
    return mm_residual(yg, w_out, b_out, x, gate)


def kernel(x, c, ctx, c_ctx, ada_w, ada_b, norm_g, final_g, hy_w_in, hy_b_in, hy_conv_w, hy_conv_b, hy_fw1,
           hy_fb1, hy_fw2, hy_fb2, hy_fw3, hy_freq, hy_skip, hy_w_out, hy_b_out, s5_a_re, s5_a_im,
           s5_log_step, s5_b_re, s5_b_im, s5_c_re, s5_c_im, s5_d, s5_w1, s5_b1, s5_w2, s5_b2, moe_wg, moe_bg,
           moe_we, moe_be, moe_w_gate, moe_w_up, moe_w_down):
    bsz, _, d = x.shape
    depth = ada_w.shape[0]
    assert depth == 2 and bsz < SUBLANES
    c_all = jnp.zeros((SUBLANES, d), F32).at[:bsz].set(c).at[bsz].set(c_ctx)
    mods = ada_mod(c_all, ada_w, ada_b)

    def mod_rows(layer, k):
        lat = mods[layer, :bsz, k * d:(k + 1) * d][:, None, :]
        cx = jnp.broadcast_to(mods[layer, bsz, k * d:(k + 1) * d][None, None, :], (bsz, 1, d))
        return lat, cx

    (sh_a, csh_a), (sc_a, csc_a), (gt_a, cgt_a) = mod_rows(0, 0), mod_rows(0, 1), mod_rows(0, 2)
    (sh_f, csh_f), (sc_f, csc_f), (gt_f, cgt_f) = mod_rows(0, 3), mod_rows(0, 4), mod_rows(0, 5)
    hy = (hy_w_in[0].astype(BF16), hy_b_in[0], hy_conv_w[0], hy_conv_b[0], hy_fw1[0], hy_fb1[0], hy_fw2[0],
          hy_fb2[0], hy_fw3[0], hy_freq[0], hy_skip[0], hy_w_out[0].astype(BF16), hy_b_out[0])
    xl = hyena_mix(x, norm_g[0, 0], sh_a, sc_a, gt_a, *hy)
    xc = hyena_mix(ctx, norm_g[0, 0], csh_a, csc_a, cgt_a, *hy)
    xl, xc = hier_moe([(xl, sh_f, sc_f, gt_f), (xc, csh_f, csc_f, cgt_f)], norm_g[0, 1],
                      moe_wg[0], moe_bg[0], moe_we[0], moe_be[0], moe_w_gate, moe_w_up, moe_w_down, 0,
                      final_g, False)

    (sh_a, csh_a), (sc_a, csc_a), (gt_a, _) = mod_rows(1, 0), mod_rows(1, 1), mod_rows(1, 2)
    (sh_f, _), (sc_f, _), (gt_f, _) = mod_rows(1, 3), mod_rows(1, 4), mod_rows(1, 5)
    xl = s5_mix(xl, xc, norm_g[1, 0], sh_a, sc_a, csh_a, csc_a, gt_a, s5_a_re[0], s5_a_im[0], s5_log_step[0],
                s5_b_re[0], s5_b_im[0], s5_c_re[0], s5_c_im[0], s5_d[0], s5_w1[0], s5_b1[0], s5_w2[0], s5_b2[0])
    (out,) = hier_moe([(xl, sh_f, sc_f, gt_f)], norm_g[1, 1], moe_wg[1], moe_bg[1], moe_we[1], moe_be[1],
                      moe_w_gate, moe_w_up, moe_w_down, 1, final_g, True)
    return out
```

```python
import functools
import math

import jax
import jax.numpy as jnp
from jax import lax
from jax.experimental import pallas as pl
from jax.experimental.pallas import tpu as pltpu

F32 = jnp.float32
BF16 = jnp.bfloat16
HIGHEST = lax.Precision.HIGHEST

NORM_EPS = 1e-6
HY_DECAY_TARGET = 1e-2
HY_FAST_PCT = 0.3
HY_SLOW_PCT = 1.5
TOP_K = 2

V7X_VMEM_LIMIT_BYTES = 56 * 1024 * 1024
LANES = 128
SUBLANES = 8
S5_TAU = 8
S5_SEGS = 2
S5_SCAN_UNROLL = 8
MOE_TM = 256
MOE_BM = 256
MOE_NCH = 12
MOE_STAGE = 4
MOE_CAST_ROWS = 32
NEG_BIG = -1e30


def _cparams(*sem):
    return pltpu.CompilerParams(dimension_semantics=sem, vmem_limit_bytes=V7X_VMEM_LIMIT_BYTES)


def _norm_mod(x, g, shift, scale):
    ms = jnp.mean(x * x, axis=-1, keepdims=True)
    return (x * lax.rsqrt(ms + NORM_EPS) * g) * (1.0 + scale) + shift


def _dot(a, b):
    return jnp.dot(a, b, preferred_element_type=F32)


def _ada_body(c_ref, w_ref, b_ref, o_ref):
    x = c_ref[...]
    s = (x * jax.nn.sigmoid(x)).astype(BF16)
    o_ref[...] = _dot(s, w_ref[...].astype(BF16)) + b_ref[...]


def ada_mod(c_all, ada_w, ada_b):
    depth, d, n = ada_w.shape
    tn = min(n, 1024)
    return pl.pallas_call(
        _ada_body,
        grid=(depth, n // tn),
        in_specs=[pl.BlockSpec((SUBLANES, d), lambda l, j: (0, 0)),
                  pl.BlockSpec((None, d, tn), lambda l, j: (l, 0, j)),
                  pl.BlockSpec((None, 1, tn), lambda l, j: (l, 0, j))],
        out_specs=pl.BlockSpec((None, SUBLANES, tn), lambda l, j: (l, 0, j)),
        out_shape=jax.ShapeDtypeStruct((depth, SUBLANES, n), F32),
        compiler_params=_cparams("parallel", "parallel"),
        name="ada_mod",
    )(c_all, ada_w, ada_b.reshape(depth, 1, n))


def _hy_in_body(xp_ref, xm_ref, xn_ref, g_ref, sh_ref, sc_ref,
                w0_ref, w1_ref, w2_ref, b0_ref, b1_ref, b2_ref,
                cw0_ref, cw1_ref, cw2_ref, cb0_ref, cb1_ref, cb2_ref,
                v_ref, x0_ref):
    i = pl.program_id(2)
    ni = pl.num_programs(2)
    tm = xm_ref.shape[0]
    x = jnp.concatenate([xp_ref[...], xm_ref[...], xn_ref[...]], axis=0)
    h = _norm_mod(x, g_ref[...], sh_ref[...], sc_ref[...]).astype(BF16)
    rows = lax.broadcasted_iota(jnp.int32, (tm + 2 * SUBLANES, 1), 0)
    valid = jnp.logical_and(jnp.logical_or(rows >= SUBLANES, i > 0),
                            jnp.logical_or(rows < tm + SUBLANES, i < ni - 1))

    def part(w_ref, b_ref, cw_ref, cb_ref):
        z = jnp.where(valid, _dot(h, w_ref[...]) + b_ref[...], 0.0)
        cw = cw_ref[...]
        zp = pltpu.roll(z, 1, axis=0)[SUBLANES:tm + SUBLANES]
        zn = pltpu.roll(z, tm + 2 * SUBLANES - 1, axis=0)[SUBLANES:tm + SUBLANES]
        return zp * cw[0:1] + z[SUBLANES:tm + SUBLANES] * cw[1:2] + zn * cw[2:3] + cb_ref[...]

    x0 = part(w0_ref, b0_ref, cw0_ref, cb0_ref)
    x1 = part(w1_ref, b1_ref, cw1_ref, cb1_ref)
    v = part(w2_ref, b2_ref, cw2_ref, cb2_ref) * x1
    v_ref[...] = v.astype(v_ref.dtype)
    x0_ref[...] = x0.astype(BF16)


def hyena_in(x, g, shift, scale, w_in, b_in, conv_w, conv_b, v_dtype):
    bsz, seq, d = x.shape
    tm = min(seq, 512)
    tn = min(d, 1024)
    nj = d // tn
    r8 = tm // SUBLANES
    last8 = seq // SUBLANES - 1
    row = lambda a: a.reshape(1, -1)
    wspec = lambda k: pl.BlockSpec((d, tn), lambda j, b, i: (0, k * nj + j))
    rspec = lambda k: pl.BlockSpec((1, tn), lambda j, b, i: (0, k * nj + j))
    cspec = lambda k: pl.BlockSpec((3, tn), lambda j, b, i: (0, k * nj + j))
    mspec = pl.BlockSpec((None, 1, d), lambda j, b, i: (b, 0, 0))
    out_spec = pl.BlockSpec((None, tm, tn), lambda j, b, i: (b, i, j))
    return pl.pallas_call(
        _hy_in_body,
        grid=(nj, bsz, seq // tm),
        in_specs=[pl.BlockSpec((None, SUBLANES, d), lambda j, b, i: (b, jnp.maximum(i * r8 - 1, 0), 0)),
                  pl.BlockSpec((None, tm, d), lambda j, b, i: (b, i, 0)),
                  pl.BlockSpec((None, SUBLANES, d), lambda j, b, i: (b, jnp.minimum((i + 1) * r8, last8), 0)),
                  pl.BlockSpec((1, d), lambda j, b, i: (0, 0)), mspec, mspec,
                  wspec(0), wspec(1), wspec(2), rspec(0), rspec(1), rspec(2),
                  cspec(0), cspec(1), cspec(2), rspec(0), rspec(1), rspec(2)],
        out_specs=[out_spec, out_spec],
        out_shape=[jax.ShapeDtypeStruct((bsz, seq, d), v_dtype), jax.ShapeDtypeStruct((bsz, seq, d), BF16)],
        compiler_params=_cparams("parallel", "parallel", "parallel"),
        name="hyena_in",
    )(x, x, x, row(g), shift, scale, w_in, w_in, w_in, row(b_in), row(b_in), row(b_in),
      conv_w, conv_w, conv_w, row(conv_b), row(conv_b), row(conv_b))


def _dft_tables(seq, blk):
    n = 2 * seq
    s = jnp.arange(seq, dtype=jnp.int32)[None, :]
    fl = jnp.arange(blk, dtype=jnp.int32)[:, None]
    fh = (jnp.arange(seq // blk, dtype=jnp.int32) * blk)[:, None]
    w = 2.0 * math.pi / n
    ang_b = ((fl * s) % n).astype(F32) * w
    ang_a = ((fh * s) % n).astype(F32) * w
    return (jnp.cos(ang_a)[:, None, :], jnp.sin(ang_a)[:, None, :], jnp.cos(ang_b), jnp.sin(ang_b))


def _dft_gen_body(ca_ref, sa_ref, cb_ref, sb_ref, c_ref, s_ref):
    ca, sa, cb, sb = ca_ref[...], sa_ref[...], cb_ref[...], sb_ref[...]
    c_ref[...] = (ca * cb - sa * sb).astype(BF16)
    s_ref[...] = (sa * cb + ca * sb).astype(BF16)


def dft_matrices(seq):
    blk = min(seq, 256)
    ca, sa, cb, sb = _dft_tables(seq, blk)
    aspec = pl.BlockSpec((None, 1, seq), lambda i: (i, 0, 0))
    bspec = pl.BlockSpec((blk, seq), lambda i: (0, 0))
    ospec = pl.BlockSpec((blk, seq), lambda i: (i, 0))
    return pl.pallas_call(
        _dft_gen_body,
        grid=(seq // blk,),
        in_specs=[aspec, aspec, bspec, bspec],
        out_specs=[ospec, ospec],
        out_shape=[jax.ShapeDtypeStruct((seq, seq), BF16)] * 2,
        compiler_params=_cparams("parallel"),
        name="dft_matrices",
    )(ca, sa, cb, sb)


def _alt_sign(rows):
    return jnp.where((rows & 1) == 0, 1.0, -1.0).astype(F32)


def _filt_body(h2_ref, wf_ref, wb_ref, dl_ref, a_ref, d_ref, ny_ref):
    seq = h2_ref.shape[0]
    h2 = h2_ref[...]
    row = lax.broadcasted_iota(jnp.int32, (seq, 1), 0)
    t = row.astype(F32) * (1.0 / (seq - 1))
    win = jnp.exp(-t * dl_ref[...])
    hf = jnp.dot(h2, wf_ref[...], precision=HIGHEST, preferred_element_type=F32) * win
    hb = jnp.dot(h2, wb_ref[...], precision=HIGHEST, preferred_element_type=F32) * win
    hb = jnp.where(row == 0, 0.0, hb)
    nrm = (jnp.sum(jnp.abs(hf), axis=0, keepdims=True) + jnp.sum(jnp.abs(hb), axis=0, keepdims=True))
    inv = 1.0 / nrm
    a = (hf + hb) * inv
    a_ref[...] = a.astype(a_ref.dtype)
    d_ref[...] = ((hb - hf) * inv).astype(d_ref.dtype)
    ny = jnp.sum(a * _alt_sign(row), axis=0, keepdims=True) * (1.0 / (2 * seq))
    ny_ref[...] = jnp.broadcast_to(ny, ny_ref.shape)


def _khat_body(a_ref, d_ref, c_ref, s_ref, kr_ref, ki_ref):
    i = pl.program_id(1)
    tm = c_ref.shape[0]
    seq = c_ref.shape[1]
    f = i * tm + lax.broadcasted_iota(jnp.int32, (tm, 1), 0)
    w = jnp.where(f == 0, 1.0, 2.0).astype(F32) * (1.0 / (2 * seq))
    kr_ref[...] = _dot(c_ref[...], a_ref[...]) * w
    ki_ref[...] = _dot(s_ref[...], d_ref[...]) * w


def hyena_filter_taps(seq, fw1, fb1, fw2, fb2, fw3, freq, taps_dtype):
    d = fw3.shape[1] // 2
    bands_n = (fw1.shape[0] - 1) // 2
    t = jnp.linspace(0.0, 1.0, seq, dtype=F32)[:, None]
    w = (2.0 * math.pi / seq) * jnp.arange(seq, dtype=F32)[:, None]
    bands = jnp.linspace(1e-4, bands_n - 1, bands_n, dtype=F32)[None, :]
    z = jnp.concatenate([t, jnp.cos(bands * w), -jnp.sin(bands * w)], axis=-1)
    h = jnp.sin(freq * (jnp.dot(z, fw1, precision=HIGHEST) + fb1))
    h2 = jnp.sin(freq * (jnp.dot(h, fw2, precision=HIGHEST) + fb2))
    max_decay = math.log(HY_DECAY_TARGET) / HY_FAST_PCT
    min_decay = math.log(HY_DECAY_TARGET) / HY_SLOW_PCT
    deltas = jnp.abs(jnp.linspace(min_decay, max_decay, d, dtype=F32))[None, :]

    order = h2.shape[1]
    tn = min(d, 256)
    nj = d // tn
    return pl.pallas_call(
        _filt_body,
        grid=(nj,),
        in_specs=[pl.BlockSpec((seq, order), lambda j: (0, 0)),
                  pl.BlockSpec((order, tn), lambda j: (0, j)),
                  pl.BlockSpec((order, tn), lambda j: (0, nj + j)),
                  pl.BlockSpec((1, tn), lambda j: (0, j))],
        out_specs=[pl.BlockSpec((seq, tn), lambda j: (0, j)),
                   pl.BlockSpec((seq, tn), lambda j: (0, j)),
                   pl.BlockSpec((SUBLANES, tn), lambda j: (0, j))],
        out_shape=[jax.ShapeDtypeStruct((seq, d), taps_dtype), jax.ShapeDtypeStruct((seq, d), taps_dtype),
                   jax.ShapeDtypeStruct((SUBLANES, d), F32)],
        compiler_params=_cparams("parallel"),
        name="hyena_filter_taps",
    )(h2, fw3, fw3, deltas)


def hyena_filter_dft(a, dd, cmat, smat):
    seq, d = a.shape
    tm = min(seq, 512)
    tn2 = min(d, 512)
    return pl.pallas_call(
        _khat_body,
        grid=(d // tn2, seq // tm),
        in_specs=[pl.BlockSpec((seq, tn2), lambda j, i: (0, j)),
                  pl.BlockSpec((seq, tn2), lambda j, i: (0, j)),
                  pl.BlockSpec((tm, seq), lambda j, i: (i, 0)),
                  pl.BlockSpec((tm, seq), lambda j, i: (i, 0))],
        out_specs=[pl.BlockSpec((tm, tn2), lambda j, i: (i, j))] * 2,
        out_shape=[jax.ShapeDtypeStruct((seq, d), F32)] * 2,
        compiler_params=_cparams("parallel", "parallel"),
        name="hyena_filter_dft",
    )(a, dd, cmat, smat)


def _dft_fwd_body(v_ref, c_ref, s_ref, kr_ref, ki_ref, kn_ref, ya_ref, yb_ref, yn_ref):
    i = pl.program_id(2)
    v = v_ref[...]
    vr = _dot(c_ref[...], v)
    p = _dot(s_ref[...], v)
    kr = kr_ref[...]
    ki = ki_ref[...]
    ya_ref[...] = (vr * kr + p * ki).astype(BF16)
    yb_ref[...] = (p * kr - vr * ki).astype(BF16)

    @pl.when(i == 0)
    def _():
        seq = v.shape[0]
        row = lax.broadcasted_iota(jnp.int32, (seq, 1), 0)
        vl = jnp.sum(v.astype(F32) * _alt_sign(row), axis=0, keepdims=True)
        yn_ref[...] = jnp.broadcast_to(vl * kn_ref[0:1, :], yn_ref.shape)


def _dft_inv_body(ya_ref, yb_ref, c_ref, s_ref, v_ref, x0_ref, skip_ref, yn_ref, o_ref):
    i = pl.program_id(2)
    tm = c_ref.shape[0]
    acc = _dot(c_ref[...], ya_ref[...]) + _dot(s_ref[...], yb_ref[...])
    t = i * tm + lax.broadcasted_iota(jnp.int32, (tm, 1), 0)
    y = acc + _alt_sign(t) * yn_ref[0:1, :] + skip_ref[...] * v_ref[...].astype(F32)
    o_ref[...] = (y * x0_ref[...].astype(F32)).astype(BF16)


def hyena_conv(v, x0, skip, kr, ki, kn, cmat, smat):
    bsz, seq, d = v.shape
    tm = min(seq, 512)
    tn = min(d, 512)
    grid = (bsz, d // tn, seq // tm)
    full = pl.BlockSpec((None, seq, tn), lambda b, j, i: (b, 0, j))
    mat = pl.BlockSpec((tm, seq), lambda b, j, i: (i, 0))
    tile = pl.BlockSpec((None, tm, tn), lambda b, j, i: (b, i, j))
    ktile = pl.BlockSpec((tm, tn), lambda b, j, i: (i, j))
    nyq = pl.BlockSpec((None, SUBLANES, tn), lambda b, j, i: (b, 0, j))
    ya, yb, yn = pl.pallas_call(
        _dft_fwd_body,
        grid=grid,
        in_specs=[full, mat, mat, ktile, ktile, pl.BlockSpec((SUBLANES, tn), lambda b, j, i: (0, j))],
        out_specs=[tile, tile, nyq],
        out_shape=[jax.ShapeDtypeStruct((bsz, seq, d), BF16)] * 2
        + [jax.ShapeDtypeStruct((bsz, SUBLANES, d), F32)],
        compiler_params=_cparams("parallel", "parallel", "arbitrary"),
        name="hyena_dft_fwd",
    )(v, cmat, smat, kr, ki, kn)
    return pl.pallas_call(
        _dft_inv_body,
        grid=grid,
        in_specs=[full, full, mat, mat, tile, tile, pl.BlockSpec((1, tn), lambda b, j, i: (0, j)), nyq],
        out_specs=tile,
        out_shape=jax.ShapeDtypeStruct((bsz, seq, d), BF16),
        compiler_params=_cparams("parallel", "parallel", "parallel"),
        name="hyena_dft_inv",
    )(ya, yb, cmat, smat, v, x0, skip.reshape(1, d), yn)


FFT_N2 = 128
FFT_MIN_SEQ = 1024
FFT_UNROLL = 8


def _fft_matrices(seq):
    n = 2 * seq
    n2 = FFT_N2
    n1 = n // n2
    r8 = SUBLANES
    q = jnp.arange(n2 // r8, dtype=jnp.int32)[:, None, None, None]
    f1 = jnp.arange(n1, dtype=jnp.int32)[None, :, None, None]
    r = jnp.arange(r8, dtype=jnp.int32)[None, None, :, None]
    t1 = jnp.arange(n1 // 2, dtype=jnp.int32)[None, None, None, :]
    ang = ((f1 * (t1 * n2 + q * r8 + r)) % n).astype(F32) * (2.0 * math.pi / n)
    g = jnp.stack([jnp.cos(ang), -jnp.sin(ang)], axis=3)
    eye = jnp.eye(r8, dtype=F32)[None, None, :, None, None, :]
    ma = (g[..., None] * eye).reshape(n2 // r8, n1 * r8 * 2, (n1 // 2) * r8).astype(BF16)
    f2 = jnp.arange(n2, dtype=jnp.int32)[:, None]
    t2 = jnp.arange(n2, dtype=jnp.int32)[None, :]
    th = ((f2 * t2) % n2).astype(F32) * (2.0 * math.pi / n2)
    co, si = jnp.cos(th), jnp.sin(th)
    wc = jnp.stack([jnp.stack([co, si], axis=-1), jnp.stack([-si, co], axis=-1)], axis=0)
    wc = wc.reshape(2 * n2, 2 * n2).astype(BF16)
    return ma, jnp.swapaxes(ma, 1, 2), wc, wc.T


def _fft_stage_a(x_ref, ma_ref, s1):
    n1h, n_q, r8, tn = x_ref.shape
    n1 = s1.shape[0]

    def body(q, carry):
        x = x_ref[:, pl.ds(q, 1), :, :].reshape(n1h * r8, tn).astype(BF16)
        a = _dot(ma_ref[q], x).astype(BF16)
        s1[:, pl.ds(pl.multiple_of(q * 2 * r8, 2 * r8), 2 * r8), :] = a.reshape(n1, 2 * r8, tn)
        return carry

    lax.fori_loop(0, n_q, body, 0, unroll=FFT_UNROLL)


def _fft_conv_body(v_ref, x0_ref, k_ref, skip_ref, ma_ref, mat_ref, wc_ref, wci_ref, o_ref, s1, ysc):
    n1h, n_q, r8, tn = v_ref.shape
    n1 = s1.shape[0]
    n2 = s1.shape[1] // 2
    seq = n1h * n_q * r8
    _fft_stage_a(v_ref, ma_ref, s1)

    def slab(f, carry):
        y = _dot(wc_ref[...], s1[f])
        yr, yi = y[:n2], y[n2:]
        kr = k_ref[f, 0].astype(F32)
        ki = k_ref[f, 1].astype(F32)
        p = jnp.concatenate([yr * kr - yi * ki, yr * ki + yi * kr], axis=0).astype(BF16)
        s1[f] = _dot(wci_ref[...], p).astype(BF16)
        return carry

    lax.fori_loop(0, n1, slab, 0, unroll=2 * FFT_UNROLL)

    def inv_a(q, carry):
        z = s1[:, pl.ds(pl.multiple_of(q * 2 * r8, 2 * r8), 2 * r8), :].reshape(n1 * 2 * r8, tn)
        ysc[:, pl.ds(q, 1), :, :] = _dot(mat_ref[q], z).reshape(n1h, 1, r8, tn)
        return carry

    lax.fori_loop(0, n_q, inv_a, 0, unroll=FFT_UNROLL)
    y = ysc[...].reshape(seq, tn) + skip_ref[...] * v_ref[...].reshape(seq, tn)
    o_ref[...] = (y * x0_ref[...].astype(F32)).astype(BF16)


def _fft_filter_body(a_ref, d_ref, ma_ref, wc_ref, k_ref, s1):
    n1 = s1.shape[0]
    n2 = s1.shape[1] // 2
    scale = 1.0 / (n1 * n2)
    for src_ref, part, sign in ((a_ref, 0, scale), (d_ref, 1, -scale)):
        _fft_stage_a(src_ref, ma_ref, s1)

        def slab(f, carry):
            y = _dot(wc_ref[part * n2:(part + 1) * n2, :], s1[f])
            k_ref[f, part] = (y * sign).astype(BF16)
            return carry

        lax.fori_loop(0, n1, slab, 0, unroll=FFT_UNROLL)


def hyena_conv_fft(v, x0, skip, a, dd):
    bsz, seq, d = v.shape
    n2 = FFT_N2
    n1 = 2 * seq // n2
    n_q = n2 // SUBLANES
    tn = min(d, 256)
    ma, mat, wc, wci = _fft_matrices(seq)
    const = lambda shape: pl.BlockSpec(shape, lambda *_: (0,) * len(shape), pipeline_mode=pl.Buffered(1))
    view = lambda t: t.reshape(t.shape[:-2] + (n1 // 2, n_q, SUBLANES, d))
    tap = pl.BlockSpec((n1 // 2, n_q, SUBLANES, tn), lambda j: (0, 0, 0, j))
    khat = pl.pallas_call(
        _fft_filter_body,
        grid=(d // tn,),
        in_specs=[tap, tap, const(ma.shape), const(wc.shape)],
        out_specs=pl.BlockSpec((n1, 2, n2, tn), lambda j: (0, 0, 0, j)),
        out_shape=jax.ShapeDtypeStruct((n1, 2, n2, d), BF16),
        scratch_shapes=[pltpu.VMEM((n1, 2 * n2, tn), BF16)],
        compiler_params=_cparams("parallel"),
        name="hyena_filter_fft",
    )(view(a), view(dd), ma, wc)
    return pl.pallas_call(
        _fft_conv_body,
        grid=(d // tn, bsz),
        in_specs=[pl.BlockSpec((None, n1 // 2, n_q, SUBLANES, tn), lambda j, b: (b, 0, 0, 0, j)),
                  pl.BlockSpec((None, seq, tn), lambda j, b: (b, 0, j)),
                  pl.BlockSpec((n1, 2, n2, tn), lambda j, b: (0, 0, 0, j), pipeline_mode=pl.Buffered(1)),
                  pl.BlockSpec((1, tn), lambda j, b: (0, j)),
                  const(ma.shape), const(mat.shape), const(wc.shape), const(wci.shape)],
        out_specs=pl.BlockSpec((None, seq, tn), lambda j, b: (b, 0, j)),
        out_shape=jax.ShapeDtypeStruct((bsz, seq, d), BF16),
        scratch_shapes=[pltpu.VMEM((n1, 2 * n2, tn), BF16), pltpu.VMEM((n1 // 2, n_q, SUBLANES, tn), F32)],
        compiler_params=_cparams("parallel", "arbitrary"),
        name="hyena_conv_fft",
    )(view(v), x0, khat, skip.reshape(1, d), ma, mat, wc, wci)


def _mm_res_body(x_ref, w_ref, b_ref, res_ref, gate_ref, o_ref):
    o_ref[...] = res_ref[...] + gate_ref[...] * (_dot(x_ref[...], w_ref[...]) + b_ref[...])


def mm_residual(x, w, b, res, gate):
    bsz, seq, k = x.shape
    n = w.shape[1]
    tm = min(seq, 512)
    return pl.pallas_call(
        _mm_res_body,
        grid=(bsz, seq // tm),
        in_specs=[pl.BlockSpec((None, tm, k), lambda b, i: (b, i, 0)),
                  pl.BlockSpec((k, n), lambda b, i: (0, 0)),
                  pl.BlockSpec((1, n), lambda b, i: (0, 0)),
                  pl.BlockSpec((None, tm, n), lambda b, i: (b, i, 0)),
                  pl.BlockSpec((None, 1, n), lambda b, i: (b, 0, 0))],
        out_specs=pl.BlockSpec((None, tm, n), lambda b, i: (b, i, 0)),
        out_shape=jax.ShapeDtypeStruct((bsz, seq, n), F32),
        compiler_params=_cparams("parallel", "parallel"),
        name="mm_residual",
    )(x, w, b.reshape(1, n), res, gate)


def _moe_pre_body(*refs, n_groups, n_experts, tile_offs):
    n_streams = len(tile_offs) - 1
    g_ref, wr_ref, br_ref, tok_ref, eid_ref, gate_ref = refs[3 * n_streams:]
    i = pl.program_id(0)
    for k in range(n_streams):
        x_ref, sh_ref, sc_ref = refs[3 * k:3 * k + 3]

        @pl.when(jnp.logical_and(i >= tile_offs[k], i < tile_offs[k + 1]))
        def _():
            tok = _norm_mod(x_ref[...], g_ref[...], sh_ref[...], sc_ref[...])
            _route_tokens(tok, wr_ref, br_ref, tok_ref, eid_ref, gate_ref, n_groups, n_experts)


def _route_tokens(tok, wr_ref, br_ref, tok_ref, eid_ref, gate_ref, n_groups, n_experts):
    tok_ref[...] = tok
    t_hi = tok.astype(BF16)
    t_lo = (tok - t_hi.astype(F32)).astype(BF16)
    logits = (_dot(t_hi, wr_ref[0]) + _dot(t_hi, wr_ref[1]) + _dot(t_lo, wr_ref[0])) + br_ref[...]
    lane = lax.broadcasted_iota(jnp.int32, logits.shape, 1)
    per = n_experts // n_groups
    big = jnp.int32(1 << 20)
    gmask = jnp.logical_and(lane >= n_experts, lane < n_experts + n_groups)
    gl = jnp.where(gmask, logits, NEG_BIG)
    gmax = jnp.max(gl, axis=-1, keepdims=True)
    gidx = jnp.min(jnp.where(gl == gmax, lane - n_experts, big), axis=-1, keepdims=True)
    p_top = 1.0 / jnp.sum(jnp.where(gmask, jnp.exp(gl - gmax), 0.0), axis=-1, keepdims=True)
    lo = gidx * per
    emask = jnp.logical_and(lane >= lo, lane < lo + per)
    el = jnp.where(emask, logits, NEG_BIG)
    m1 = jnp.max(el, axis=-1, keepdims=True)
    i1 = jnp.min(jnp.where(el == m1, lane, big), axis=-1, keepdims=True)
    el2 = jnp.where(lane == i1, NEG_BIG, el)
    m2 = jnp.max(el2, axis=-1, keepdims=True)
    i2 = jnp.min(jnp.where(el2 == m2, lane, big), axis=-1, keepdims=True)
    e21 = jnp.exp(m2 - m1)
    g1 = p_top / (1.0 + e21)
    g2 = g1 * e21
    eid_ref[...] = jnp.where(lane == 0, i1, jnp.where(lane == 1, i2, 0))
    gate_ref[...] = jnp.where(lane == 0, g1, jnp.where(lane == 1, g2, 0.0))


def moe_pre(streams, g, wr, br, n_groups, n_experts):
    d = streams[0][0].shape[2]
    tm = MOE_TM
    tile_offs = [0]
    in_specs, args = [], []
    for x, shift, scale in streams:
        bsz, seq, _ = x.shape
        nt = seq // tm
        n_tiles = bsz * nt
        off = tile_offs[-1]
        tile_offs.append(off + n_tiles)

        def tile(i, off=off, n_tiles=n_tiles):
            return jnp.clip(i - off, 0, n_tiles - 1)

        in_specs += [pl.BlockSpec((None, tm, d), lambda i, tile=tile, nt=nt: (tile(i) // nt, tile(i) % nt, 0)),
                     pl.BlockSpec((None, 1, d), lambda i, tile=tile, nt=nt: (tile(i) // nt, 0, 0)),
                     pl.BlockSpec((None, 1, d), lambda i, tile=tile, nt=nt: (tile(i) // nt, 0, 0))]
        args += [x, shift, scale]
    in_specs += [pl.BlockSpec((1, d), lambda i: (0, 0)),
                 pl.BlockSpec((2, d, LANES), lambda i: (0, 0, 0)),
                 pl.BlockSpec((1, LANES), lambda i: (0, 0))]
    wr_hi = wr.astype(BF16)
    wr_split = jnp.stack([wr_hi, (wr - wr_hi.astype(F32)).astype(BF16)])
    args += [g.reshape(1, d), wr_split, br]
    total = tile_offs[-1] * tm
    rout = pl.BlockSpec((tm, LANES), lambda i: (i, 0))
    tok, eid, gate = pl.pallas_call(
        functools.partial(_moe_pre_body, n_groups=n_groups, n_experts=n_experts, tile_offs=tuple(tile_offs)),
        grid=(tile_offs[-1],),
        in_specs=in_specs,
        out_specs=[pl.BlockSpec((tm, d), lambda i: (i, 0)), rout, rout],
        out_shape=[jax.ShapeDtypeStruct((total, d), F32),
                   jax.ShapeDtypeStruct((total, LANES), jnp.int32),
                   jax.ShapeDtypeStruct((total, LANES), F32)],
        compiler_params=_cparams("parallel"),
        name="moe_pre",
    )(*args)
    return tok, eid, gate, tile_offs[:-1]


def _start_row_gather(idx_ref, n_rows, stride, offset, src_hbm, dst_vmem, sem):
    def body(g, c):
        r0 = pl.multiple_of(g * SUBLANES, SUBLANES)
        dst_tile = dst_vmem.at[pl.ds(r0, SUBLANES)]
        for k in range(SUBLANES):
            pltpu.make_async_copy(src_hbm.at[pl.ds(idx_ref[stride * (r0 + k) + offset], 1)],
                                  dst_tile.at[pl.ds(k, 1)], sem).start()
        return c

    lax.fori_loop(0, n_rows // SUBLANES, body, 0, unroll=2)


def _wait_row_gather(n_rows, src_hbm, dst_vmem, sem):
    pltpu.make_async_copy(src_hbm.at[pl.ds(0, n_rows)], dst_vmem, sem).wait()


def _expert_body(bv_ref, rk_ref, pe_ref, tot_ref, src_ref, nxt_ref, tok_ref, wg_hbm, wu_hbm, wd_hbm, o_ref,
                 xbuf, xsem, wcache, stg, wsem, cnt, *, layer):
    i = pl.program_id(0)
    n = pl.num_programs(0)
    slot = i % 2
    cr, cc = stg.shape[1:]
    total = tot_ref[0]
    mats_hbm = (wg_hbm, wu_hbm, wd_hbm)

    @pl.when(i == 0)
    def _():
        cnt[0] = 0
        cnt[1] = 0

    @pl.when(jnp.logical_and(i == 0, bv_ref[0] > 0))
    def _():
        _start_row_gather(src_ref, MOE_BM, 1, 0, tok_ref, xbuf.at[0], xsem.at[0])

    @pl.when(jnp.logical_and(i + 1 < n, bv_ref[jnp.minimum(i + 1, n - 1)] > 0))
    def _():
        _start_row_gather(nxt_ref, MOE_BM, 1, 0, tok_ref, xbuf.at[1 - slot], xsem.at[1 - slot])

    def chunk_geom(c):
        q = c % MOE_NCH
        m = q // 4
        sub = q % 4
        r0 = jnp.where(m < 2, sub, sub // 2) * cr
        c0 = jnp.where(m < 2, 0, sub % 2) * cc
        return m, pl.multiple_of(r0, cr), pl.multiple_of(c0, cc)

    def issue(c):
        e = pe_ref[c // MOE_NCH]
        m, r0, c0 = chunk_geom(c)
        s = c % MOE_STAGE
        for k, w_hbm in enumerate(mats_hbm):
            @pl.when(m == k)
            def _():
                pltpu.make_async_copy(w_hbm.at[layer, e, pl.ds(r0, cr), pl.ds(c0, cc)], stg.at[s],
                                      wsem.at[s]).start()

    def cast(c):
        s = c % MOE_STAGE
        pltpu.make_async_copy(wg_hbm.at[layer, 0, pl.ds(0, cr), pl.ds(0, cc)], stg.at[s], wsem.at[s]).wait()
        ws = (c // MOE_NCH) % 2
        q = c % MOE_NCH
        for r0 in range(0, cr, MOE_CAST_ROWS):
            wcache[ws, q, r0:r0 + MOE_CAST_ROWS, :] = stg[s, r0:r0 + MOE_CAST_ROWS, :].astype(BF16)

    valid = bv_ref[i] > 0
    rank = rk_ref[i]
    issued = cnt[0]
    done = cnt[1]
    limit = jnp.minimum(total, MOE_NCH * (rank + 2))
    need = jnp.where(valid, MOE_NCH * (rank + 1), done)

    def fill(issued, done):
        hi = jnp.minimum(limit, done + MOE_STAGE)

        def body(c, carry):
            issue(c)
            return carry

        lax.fori_loop(issued, hi, body, 0)
        return jnp.maximum(issued, hi)

    def cast_and_refill(c, issued):
        cast(c)
        more = issued < jnp.minimum(limit, c + 1 + MOE_STAGE)

        @pl.when(more)
        def _():
            issue(issued)

        return issued + more.astype(jnp.int32)

    issued = fill(issued, done)
    issued = lax.fori_loop(done, need, cast_and_refill, issued)
    done = jnp.maximum(done, need)

    @pl.when(valid)
    def _():
        ws = rank % 2
        _wait_row_gather(MOE_BM, tok_ref, xbuf.at[slot], xsem.at[slot])
        x = xbuf[slot].astype(BF16)
        gate = sum(_dot(x[:, k * cr:(k + 1) * cr], wcache[ws, k]) for k in range(4))
        up = sum(_dot(x[:, k * cr:(k + 1) * cr], wcache[ws, 4 + k]) for k in range(4))
        h = (gate * jax.nn.sigmoid(gate) * up).astype(BF16)
        for half in range(2):
            o_ref[:, half * cc:(half + 1) * cc] = sum(
                _dot(h[:, k * cr:(k + 1) * cr], wcache[ws, 8 + 2 * k + half]) for k in range(2))

    @pl.when(jnp.logical_not(valid))
    def _():
        o_ref[...] = jnp.zeros_like(o_ref)

    fetched = issued
    issued = lax.fori_loop(done, fetched, cast_and_refill, issued)
    done = jnp.maximum(done, fetched)
    last = i == n - 1
    tail = jnp.where(last, issued, done)

    def drain(c, carry):
        cast(c)
        return carry

    lax.fori_loop(done, tail, drain, 0)
    cnt[0] = issued
    cnt[1] = jnp.maximum(done, tail)


def moe_experts(tok, src_tok, block_valid, block_rank, present, n_chunks, w_gate, w_up, w_down, layer):
    d = tok.shape[1]
    n_rows = src_tok.shape[0]
    n_blocks = n_rows // MOE_BM
    dh = w_gate.shape[3]
    assert 2 * dh == d and MOE_NCH == 12
    cr, cc = d // 4, dh
    any_spec = pl.BlockSpec(memory_space=pl.ANY)
    grid_spec = pltpu.PrefetchScalarGridSpec(
        num_scalar_prefetch=4,
        grid=(n_blocks,),
        in_specs=[pl.BlockSpec((MOE_BM,), lambda i, *_: (i,), memory_space=pltpu.SMEM),
                  pl.BlockSpec((MOE_BM,), lambda i, *_: (jnp.minimum(i + 1, n_blocks - 1),),
                               memory_space=pltpu.SMEM),
                  any_spec, any_spec, any_spec, any_spec],
        out_specs=pl.BlockSpec((MOE_BM, d), lambda i, *_: (i, 0)),
        scratch_shapes=[pltpu.VMEM((2, MOE_BM, d), F32), pltpu.SemaphoreType.DMA((2,)),
                        pltpu.VMEM((2, MOE_NCH, cr, cc), BF16),
                        pltpu.VMEM((MOE_STAGE, cr, cc), F32), pltpu.SemaphoreType.DMA((MOE_STAGE,)),
                        pltpu.SMEM((2,), jnp.int32)],
    )
    return pl.pallas_call(
        functools.partial(_expert_body, layer=layer),
        grid_spec=grid_spec,
        out_shape=jax.ShapeDtypeStruct((n_rows, d), F32),
        compiler_params=_cparams("arbitrary"),
        name="moe_experts",
    )(block_valid, block_rank, present, n_chunks, src_tok, src_tok, tok, w_gate, w_up, w_down)


def _combine_body(dest_ref, nxt_ref, os_ref, gate_ref, res_ref, gt_ref, fg_ref, o_ref, buf, sem, *, final_norm):
    rows = res_ref.shape[0]
    i = pl.program_id(0)
    n = pl.num_programs(0)
    slot = i % 2

    def start(idx_ref, s):
        for k in range(TOP_K):
            _start_row_gather(idx_ref, rows, TOP_K, k, os_ref, buf.at[s, k], sem.at[s])

    @pl.when(i == 0)
    def _():
        start(dest_ref, 0)

    @pl.when(i + 1 < n)
    def _():
        start(nxt_ref, 1 - slot)

    for k in range(TOP_K):
        _wait_row_gather(rows, os_ref, buf.at[slot, k], sem.at[slot])
    gates = gate_ref[...]
    mo = gates[:, 0:1] * buf[slot, 0] + gates[:, 1:2] * buf[slot, 1]
    y = res_ref[...] + gt_ref[...] * mo
    if final_norm:
        ms = jnp.mean(y * y, axis=-1, keepdims=True)
        y = y * lax.rsqrt(ms + NORM_EPS) * fg_ref[...]
    o_ref[...] = y


def moe_combine(os, dest, gates, tile0, res, gt, final_g, final_norm):
    bsz, seq, d = res.shape
    rows = MOE_TM
    nt = seq // rows
    n = bsz * nt
    tile = pl.BlockSpec((None, rows, d), lambda i: (i // nt, i % nt, 0))
    return pl.pallas_call(
        functools.partial(_combine_body, final_norm=final_norm),
        grid=(n,),
        in_specs=[pl.BlockSpec((rows * TOP_K,), lambda i: (tile0 + i,), memory_space=pltpu.SMEM),
                  pl.BlockSpec((rows * TOP_K,), lambda i: (tile0 + jnp.minimum(i + 1, n - 1),),
                               memory_space=pltpu.SMEM),
                  pl.BlockSpec(memory_space=pl.ANY),
                  pl.BlockSpec((rows, LANES), lambda i: (tile0 + i, 0)),
                  tile,
                  pl.BlockSpec((None, 1, d), lambda i: (i // nt, 0, 0)),
                  pl.BlockSpec((1, d), lambda i: (0, 0))],
        out_specs=tile,
        out_shape=jax.ShapeDtypeStruct((bsz, seq, d), F32),
        scratch_shapes=[pltpu.VMEM((2, TOP_K, rows, d), F32), pltpu.SemaphoreType.DMA((2,))],
        compiler_params=_cparams("arbitrary"),
        name="moe_combine",
    )(dest, dest, os, gates, res, gt, final_g.reshape(1, d))


def _route_plan(eid, n_experts):
    e_flat = eid.reshape(-1)
    a = e_flat.shape[0]
    order = jnp.argsort(e_flat)
    pos = jnp.argsort(order)
    experts = jnp.arange(n_experts, dtype=jnp.int32)
    counts = jnp.sum((e_flat[:, None] == experts[None, :]).astype(jnp.int32), axis=0)
    start = jnp.cumsum(counts) - counts
    padded = (counts + MOE_BM - 1) // MOE_BM * MOE_BM
    pad_end = jnp.cumsum(padded)
    pad_start = pad_end - padded
    dest = (pad_start - start)[e_flat] + pos
    n_blocks = -(-a // MOE_BM) + n_experts
    starts = jnp.arange(n_blocks, dtype=jnp.int32) * MOE_BM
    block_expert = jnp.minimum(jnp.sum((pad_end[None, :] <= starts[:, None]).astype(jnp.int32), axis=1),
                               n_experts - 1)
    block_valid = (starts < pad_end[-1]).astype(jnp.int32)
    lane = jnp.arange(MOE_BM, dtype=jnp.int32)[None, :]
    blk_off = (starts - pad_start[block_expert])[:, None] + lane
    live = jnp.logical_and(blk_off < counts[block_expert][:, None], block_valid[:, None] > 0)
    sorted_pos = jnp.clip(start[block_expert][:, None] + blk_off, 0, a - 1)
    src_tok = jnp.where(live, order[sorted_pos] // TOP_K, 0).reshape(-1)
    has = (counts > 0).astype(jnp.int32)
    block_rank = (jnp.cumsum(has) - 1)[block_expert]
    present = jnp.argsort(1 - has)
    n_chunks = (MOE_NCH * jnp.sum(has)).reshape(1)
    i32 = lambda v: v.astype(jnp.int32)
    return i32(dest), i32(src_tok), block_valid, i32(block_rank), i32(present), i32(n_chunks)


def hier_moe(streams, norm_g, wg, bg, we, be, w_gate, w_up, w_down, layer, final_g, final_norm):
    n_groups = wg.shape[1]
    n_experts = we.shape[1]
    d = wg.shape[0]
    wr = jnp.zeros((d, LANES), F32).at[:, :n_experts].set(we).at[:, n_experts:n_experts + n_groups].set(wg)
    br = jnp.zeros((1, LANES), F32).at[0, :n_experts].set(be).at[0, n_experts:n_experts + n_groups].set(bg)
    tok, eid, gates, tile0s = moe_pre([s[:3] for s in streams], norm_g, wr, br, n_groups, n_experts)
    dest, src_tok, block_valid, block_rank, present, n_chunks = _route_plan(eid[:, :TOP_K], n_experts)
    os = moe_experts(tok, src_tok, block_valid, block_rank, present, n_chunks, w_gate, w_up, w_down, layer)
    return [moe_combine(os, dest, gates, tile0, x, gt, final_g, final_norm)
            for (x, _, _, gt), tile0 in zip(streams, tile0s)]


def _norm_mod_body(x_ref, g_ref, sh_ref, sc_ref, o_ref):
    o_ref[...] = _norm_mod(x_ref[...], g_ref[...], sh_ref[...], sc_ref[...]).astype(o_ref.dtype)


def norm_mod(x, g, shift, scale):
    bsz, seq, d = x.shape
    tm = min(seq, 512)
    mspec = pl.BlockSpec((None, 1, d), lambda b, i: (b, 0, 0))
    return pl.pallas_call(
        _norm_mod_body,
        grid=(bsz, seq // tm),
        in_specs=[pl.BlockSpec((None, tm, d), lambda b, i: (b, i, 0)),
                  pl.BlockSpec((1, d), lambda b, i: (0, 0)), mspec, mspec],
        out_specs=pl.BlockSpec((None, tm, d), lambda b, i: (b, i, 0)),
        out_shape=jax.ShapeDtypeStruct((bsz, seq, d), BF16),
        compiler_params=_cparams("parallel", "parallel"),
        name="norm_mod",
    )(x, g.reshape(1, d), shift, scale)


def _s5_arrange(h):
    bsz, t, d = h.shape
    c = t // (S5_SEGS * S5_TAU)
    h = h.reshape(bsz, S5_SEGS, c, S5_TAU, d // LANES, LANES)
    return h.transpose(2, 0, 1, 4, 3, 5).reshape(c * bsz * S5_SEGS, d * S5_TAU)


def _s5_unarrange(y, bsz):
    r, w = y.shape
    d = w // S5_TAU
    c = r // (bsz * S5_SEGS)
    y = y.reshape(c, bsz, S5_SEGS, d // LANES, S5_TAU, LANES)
    return y.transpose(1, 2, 0, 4, 3, 5).reshape(bsz, S5_SEGS * c * S5_TAU, d)


def _s5_operators(a_re, a_im, log_step, b_re, b_im, c_re, c_im):
    n_g, n_p = a_re.shape[1:]
    n_h = b_re.shape[-1]
    gpt = LANES // n_h
    n_j = n_g // gpt
    tau = S5_TAU
    assert tau * n_h == LANES and 2 * n_p == LANES
    lam_step = lax.complex(a_re, a_im) * jnp.exp(log_step)[..., None]
    lam_bar = jnp.exp(lam_step)
    b_bar = ((lam_bar - 1.0) / lax.complex(a_re, a_im))[..., None] * lax.complex(b_re, b_im)
    c_mat = lax.complex(c_re, c_im)
    ks = jnp.arange(tau + 1, dtype=F32)[None, :, None, None]
    pw = jnp.exp(lam_step[:, None] * ks)
    ein = functools.partial(jnp.einsum, precision=HIGHEST)
    inj_c, cl_c, lt = [], [], []
    tz_c = 0.0
    for d in range(2):
        pos = jnp.arange(tau) if d == 0 else jnp.arange(tau)[::-1]
        inj = (pw[d][tau - 1 - pos][..., None] * b_bar[d][None]).reshape(tau, n_j, gpt, n_p, n_h)
        inj = inj.transpose(1, 0, 2, 4, 3).reshape(n_j, tau * LANES, n_p)
        inj_c.append(jnp.concatenate([inj.real, inj.imag], axis=-1))
        cl = (c_mat[d][None] * pw[d][pos + 1][:, :, None, :]).reshape(tau, n_j, gpt, n_h, n_p)
        cl = cl.transpose(1, 2, 4, 0, 3).reshape(n_j, gpt * n_p, tau * n_h)
        cl_c.append(jnp.concatenate([cl.real, -cl.imag], axis=1))
        mk = ein('gop,kgp,gph->kgoh', c_mat[d], pw[d][:tau], b_bar[d]).real
        diff = pos[:, None] - pos[None, :]
        tz = jnp.where((diff >= 0)[:, :, None, None, None], mk[jnp.clip(diff, 0, tau - 1)], 0.0)
        tz = tz.reshape(tau, tau, n_j, gpt, n_h, n_h)
        tz_c = tz_c + tz.transpose(2, 1, 3, 5, 0, 4).reshape(n_j, tau * LANES, tau * n_h)
        lt_d = pw[d][tau].reshape(n_j, 1, gpt * n_p)
        lt.append(jnp.concatenate([lt_d.real, lt_d.imag], axis=-1))
    ws, wu, wh = s5_expand(jnp.stack(inj_c).astype(BF16), jnp.stack(cl_c).astype(BF16), tz_c.astype(BF16),
                           n_h, n_p)
    return ws, wu, wh, jnp.stack(lt).astype(F32)


def _s5_expand_body(inj_ref, cl_ref, tz_ref, ws_ref, wu_ref, wh_ref, *, n_h, n_p):
    rows = tz_ref.shape[0]
    gpt = LANES // n_h
    row = lax.broadcasted_iota(jnp.int32, (rows, LANES), 0)
    lane = lax.broadcasted_iota(jnp.int32, (rows, LANES), 1)
    sel_r = lax.broadcasted_iota(jnp.int32, (LANES, LANES), 0)
    sel_l = lax.broadcasted_iota(jnp.int32, (LANES, LANES), 1)
    grp_in = (row // n_h) % gpt
    grp_st = (row // n_p) % gpt

    def spread_out(m, t, grp_row):
        sel = jnp.logical_and(sel_r // n_h == t, sel_r % n_h == sel_l % n_h).astype(BF16)
        return jnp.where(grp_row == lane // n_h, _dot(m, sel), 0.0).astype(BF16)

    def spread_state(m, c, q, grp_row):
        sel = jnp.logical_and(sel_r // n_p == c, sel_r % n_p == sel_l % n_p).astype(BF16)
        return jnp.where(grp_row == (LANES // n_p) * q + lane // n_p, _dot(m, sel), 0.0).astype(BF16)

    w2 = 2 * gpt * n_p
    tz = tz_ref[...]
    for t in range(S5_TAU):
        wu_ref[:, t * LANES:(t + 1) * LANES] = spread_out(tz, t, grp_in)
    for d in range(2):
        cl = cl_ref[d]
        inj = inj_ref[d]
        for t in range(S5_TAU):
            wh_ref[d, :, t * LANES:(t + 1) * LANES] = spread_out(cl, t, grp_st)
        for c in range(2):
            for q in range(gpt * n_p // LANES):
                lo = d * w2 + c * gpt * n_p + q * LANES
                ws_ref[:, lo:lo + LANES] = spread_state(inj, c, q, grp_in)


def s5_expand(inj_c, cl_c, tz_c, n_h, n_p):
    n_j, rows, _ = tz_c.shape
    gpt = LANES // n_h
    w2 = 2 * gpt * n_p
    assert rows == S5_TAU * LANES == w2
    cspec = pl.BlockSpec((2, None, rows, LANES), lambda j: (0, j, 0, 0))
    return pl.pallas_call(
        functools.partial(_s5_expand_body, n_h=n_h, n_p=n_p),
        grid=(n_j,),
        in_specs=[cspec, cspec, pl.BlockSpec((None, rows, LANES), lambda j: (j, 0, 0))],
        out_specs=[pl.BlockSpec((None, rows, 2 * w2), lambda j: (j, 0, 0)),
                   pl.BlockSpec((None, rows, rows), lambda j: (j, 0, 0)),
                   pl.BlockSpec((2, None, w2, rows), lambda j: (0, j, 0, 0))],
        out_shape=[jax.ShapeDtypeStruct((n_j, rows, 2 * w2), BF16),
                   jax.ShapeDtypeStruct((n_j, rows, rows), BF16),
                   jax.ShapeDtypeStruct((2, n_j, w2, rows), BF16)],
        compiler_params=_cparams("parallel"),
        name="s5_expand",
    )(inj_c, cl_c, tz_c)


def _s5_inj_body(x_ref, w_ref, o_ref):
    o_ref[...] = _dot(x_ref[...], w_ref[...])


def s5_inject(xr, ws):
    r = xr.shape[0]
    n_j, k, n = ws.shape
    tm = r // 2 if r % 32 == 0 else r
    return pl.pallas_call(
        _s5_inj_body,
        grid=(n_j, r // tm),
        in_specs=[pl.BlockSpec((tm, k), lambda j, i: (i, j)),
                  pl.BlockSpec((None, k, n), lambda j, i: (j, 0, 0))],
        out_specs=pl.BlockSpec((tm, n), lambda j, i: (i, j)),
        out_shape=jax.ShapeDtypeStruct((r, n_j * n), F32),
        compiler_params=_cparams("parallel", "parallel"),
        name="s5_inject",
    )(xr, ws)


def _cmul(ar, ai, br, bi):
    return ar * br - ai * bi, ar * bi + ai * br


def _s5_scan_body(s_ref, lt_ref, h_ref, raw_ref, *, n_ctx, n_lat, bsz):
    d = pl.program_id(1)
    w2 = lt_ref.shape[-1]
    w = w2 // 2
    rows = bsz * S5_SEGS
    seg = lax.broadcasted_iota(jnp.int32, (rows, 1), 0) % S5_SEGS
    is_late = seg != d
    lam_r = lt_ref[:, 0:w]
    lam_i = lt_ref[:, w:w2]
    zero = jnp.zeros((rows, w), F32)
    one = (jnp.ones((1, w), F32), jnp.zeros((1, w), F32))

    def swap_segments(x):
        return jnp.where(seg == 0, pltpu.roll(x, rows - 1, axis=0), pltpu.roll(x, 1, axis=0))

    def phase(c0, n_steps, hin_r, hin_i, write):
        def chunk(k):
            return c0 + jnp.where(d == 0, k, n_steps - 1 - k)

        def step_raw(k, carry):
            hr, hi = carry
            c = chunk(k)
            raw_ref[c - c0, :, 0:w] = hr
            raw_ref[c - c0, :, w:w2] = hi
            nr, ni = _cmul(lam_r, lam_i, hr, hi)
            return nr + s_ref[c, :, 0:w], ni + s_ref[c, :, w:w2]

        er, ei = lax.fori_loop(0, n_steps, step_raw, (zero, zero), unroll=S5_SCAN_UNROLL)
        pr, pi = lax.fori_loop(0, n_steps, lambda k, q: _cmul(lam_r, lam_i, *q), one)
        dr, di = _cmul(pr, pi, hin_r, hin_i)
        first_r = jnp.where(is_late, 0.0, er + dr)
        first_i = jnp.where(is_late, 0.0, ei + di)
        carry_r = jnp.where(is_late, swap_segments(first_r), hin_r)
        carry_i = jnp.where(is_late, swap_segments(first_i), hin_i)
        if write:
            def step_fix(k, q):
                c = chunk(k)
                fr, fi = _cmul(q[0], q[1], carry_r, carry_i)
                h_ref[c - c0, :, 0:w] = (raw_ref[c - c0, :, 0:w] + fr).astype(h_ref.dtype)
                h_ref[c - c0, :, w:w2] = (raw_ref[c - c0, :, w:w2] + fi).astype(h_ref.dtype)
                return _cmul(lam_r, lam_i, q[0], q[1])

            lax.fori_loop(0, n_steps, step_fix, one, unroll=S5_SCAN_UNROLL)
        lr, li = _cmul(pr, pi, carry_r, carry_i)
        last_r = jnp.where(is_late, er + lr, 0.0)
        last_i = jnp.where(is_late, ei + li, 0.0)
        return (jnp.where(is_late, 0.0, swap_segments(last_r)), jnp.where(is_late, 0.0, swap_segments(last_i)))

    hr, hi = zero, zero
    if n_ctx:
        hr, hi = phase(0, n_ctx, hr, hi, False)
    phase(n_ctx, n_lat, hr, hi, True)


def s5_scan(s, lt, n_ctx, n_lat, bsz):
    assert S5_SEGS == 2
    n_c, rows, _ = s.shape
    n_j = lt.shape[1]
    w2 = lt.shape[-1]
    return pl.pallas_call(
        functools.partial(_s5_scan_body, n_ctx=n_ctx, n_lat=n_lat, bsz=bsz),
        grid=(n_j, 2),
        in_specs=[pl.BlockSpec((n_c, rows, w2), lambda j, d: (0, 0, 2 * j + d)),
                  pl.BlockSpec((None, None, 1, w2), lambda j, d: (d, j, 0, 0))],
        out_specs=pl.BlockSpec((None, n_lat, rows, w2), lambda j, d: (d, 0, 0, j)),
        out_shape=jax.ShapeDtypeStruct((2, n_lat, rows, n_j * w2), BF16),
        scratch_shapes=[pltpu.VMEM((max(n_ctx, n_lat), rows, w2), F32)],
        compiler_params=_cparams("parallel", "parallel"),
        name="s5_scan",
    )(s, lt)


def _s5_out_body(x_ref, hf_ref, hb_ref, wu_ref, whf_ref, whb_ref, o_ref):
    o_ref[...] = (_dot(x_ref[...], wu_ref[...]) + _dot(hf_ref[...], whf_ref[...])
                  + _dot(hb_ref[...], whb_ref[...])).astype(o_ref.dtype)


def s5_readout(xr, h, wu, wh):
    r = xr.shape[0]
    n_j, k, n = wu.shape
    w2 = wh.shape[2]
    tm = min(r, 1024)
    return pl.pallas_call(
        _s5_out_body,
        grid=(n_j, r // tm),
        in_specs=[pl.BlockSpec((tm, k), lambda j, i: (i, j)),
                  pl.BlockSpec((None, tm, w2), lambda j, i: (0, i, j)),
                  pl.BlockSpec((None, tm, w2), lambda j, i: (1, i, j)),
                  pl.BlockSpec((None, k, n), lambda j, i: (j, 0, 0)),
                  pl.BlockSpec((None, None, w2, n), lambda j, i: (0, j, 0, 0)),
                  pl.BlockSpec((None, None, w2, n), lambda j, i: (1, j, 0, 0))],
        out_specs=pl.BlockSpec((tm, n), lambda j, i: (i, j)),
        out_shape=jax.ShapeDtypeStruct((r, n_j * n), BF16),
        compiler_params=_cparams("parallel", "parallel"),
        name="s5_readout",
    )(xr, h, h, wu, wh, wh)


def _gelu_tanh(x):
    return 0.5 * x * (1.0 + jnp.tanh(math.sqrt(2.0 / math.pi) * (x + 0.044715 * (x * x * x))))


def _s5_glu_body(x_ref, y_ref, g_ref, sh_ref, sc_ref, dk_ref, w1_ref, w2_ref, b1_ref, b2_ref, gt_ref, o_ref):
    x = x_ref[...]
    u = _norm_mod(x, g_ref[...], sh_ref[...], sc_ref[...])
    y = _gelu_tanh(y_ref[...].astype(F32) + dk_ref[...] * u).astype(BF16)
    o = (_dot(y, w1_ref[...]) + b1_ref[...]) * jax.nn.sigmoid(_dot(y, w2_ref[...]) + b2_ref[...])
    o_ref[...] = x + gt_ref[...] * o


def s5_glu(x, y, g, shift, scale, d_skip, w1, b1, w2, b2, gate):
    bsz, seq, d = x.shape
    tm = min(seq, 512)
    row = lambda a: a.reshape(1, d)
    rspec = pl.BlockSpec((1, d), lambda b, i: (0, 0))
    mspec = pl.BlockSpec((None, 1, d), lambda b, i: (b, 0, 0))
    tile = pl.BlockSpec((None, tm, d), lambda b, i: (b, i, 0))
    wspec = pl.BlockSpec((d, d), lambda b, i: (0, 0), pipeline_mode=pl.Buffered(1))
    return pl.pallas_call(
        _s5_glu_body,
        grid=(bsz, seq // tm),
        in_specs=[tile, tile, rspec, mspec, mspec, rspec, wspec, wspec, rspec, rspec, mspec],
        out_specs=tile,
        out_shape=jax.ShapeDtypeStruct((bsz, seq, d), F32),
        compiler_params=_cparams("parallel", "parallel"),
        name="s5_glu",
    )(x, y, row(g), shift, scale, row(d_skip), w1, w2, row(b1), row(b2), gate)


def s5_mix(xl, xc, g, sh_l, sc_l, sh_c, sc_c, gate_l, a_re, a_im, log_step, b_re, b_im, c_re, c_im, d_skip,
           w1, b1, w2, b2):
    bsz, seq, d = xl.shape
    hl = norm_mod(xl, g, sh_l, sc_l)
    hc = norm_mod(xc, g, sh_c, sc_c)
    xr_c = _s5_arrange(hc)
    xr_l = _s5_arrange(hl)
    rows = bsz * S5_SEGS
    n_ctx = xr_c.shape[0] // rows
    n_lat = xr_l.shape[0] // rows
    ws, wu, wh, lt = _s5_operators(a_re, a_im, log_step, b_re, b_im, c_re, c_im)
    s = s5_inject(jnp.concatenate([xr_c, xr_l], axis=0), ws)
    h = s5_scan(s.reshape(n_ctx + n_lat, rows, -1), lt, n_ctx, n_lat, bsz)
    y = s5_readout(xr_l, h.reshape(2, n_lat * rows, -1), wu, wh)
    y = _s5_unarrange(y, bsz)
    return s5_glu(xl, y, g, sh_l, sc_l, d_skip, w1.astype(BF16), b1, w2.astype(BF16), b2, gate_l)


def hyena_mix(x, g, shift, scale, gate, w_in, b_in, conv_w, conv_b, fw1, fb1, fw2, fb2, fw3, freq, skip,
              w_out, b_out):
    seq = x.shape[1]
    if seq >= FFT_MIN_SEQ and (2 * seq) % (2 * FFT_N2) == 0:
        a, dd, _ = hyena_filter_taps(seq, fw1, fb1, fw2, fb2, fw3, freq, F32)
        v, x0 = hyena_in(x, g, shift, scale, w_in, b_in, conv_w, conv_b, F32)
        yg = hyena_conv_fft(v, x0, skip, a, dd)
    else:
        cmat, smat = dft_matrices(seq)
        a, dd, kn = hyena_filter_taps(seq, fw1, fb1, fw2, fb2, fw3, freq, BF16)
        kr, ki = hyena_filter_dft(a, dd, cmat, smat)
        v, x0 = hyena_in(x, g, shift, scale, w_in, b_in, conv_w, conv_b, BF16)
        yg = hyena_conv(v, x0, skip, kr, ki, kn, cmat, smat)
    return mm_residual(yg, w_out, b_out, x, gate)


def kernel(x, c, ctx, c_ctx, ada_w, ada_b, norm_g, final_g, hy_w_in, hy_b_in, hy_conv_w, hy_conv_b, hy_fw1,
           hy_fb1, hy_fw2, hy_fb2, hy_fw3, hy_freq, hy_skip, hy_w_out, hy_b_out, s5_a_re, s5_a_im,
           s5_log_step, s5_b_re, s5_b_im, s5_c_re, s5_c_im, s5_d, s5_w1, s5_b1, s5_w2, s5_b2, moe_wg, moe_bg,
           moe_we, moe_be, moe_w_gate, moe_w_up, moe_w_down):
    bsz, _, d = x.shape
    depth = ada_w.shape[0]
    assert depth == 2 and bsz < SUBLANES
    c_all = jnp.zeros((SUBLANES, d), F32).at[:bsz].set(c).at[bsz].set(c_ctx)
    mods = ada_mod(c_all, ada_w, ada_b)

    def mod_rows(layer, k):
        lat = mods[layer, :bsz, k * d:(k + 1) * d][:, None, :]
        cx = jnp.broadcast_to(mods[layer, bsz, k * d:(k + 1) * d][None, None, :], (bsz, 1, d))
        return lat, cx

    (sh_a, csh_a), (sc_a, csc_a), (gt_a, cgt_a) = mod_rows(0, 0), mod_rows(0, 1), mod_rows(0, 2)
    (sh_f, csh_f), (sc_f, csc_f), (gt_f, cgt_f) = mod_rows(0, 3), mod_rows(0, 4), mod_rows(0, 5)
    hy = (hy_w_in[0].astype(BF16), hy_b_in[0], hy_conv_w[0], hy_conv_b[0], hy_fw1[0], hy_fb1[0], hy_fw2[0],
          hy_fb2[0], hy_fw3[0], hy_freq[0], hy_skip[0], hy_w_out[0].astype(BF16), hy_b_out[0])
    xl = hyena_mix(x, norm_g[0, 0], sh_a, sc_a, gt_a, *hy)
    xc = hyena_mix(ctx, norm_g[0, 0], csh_a, csc_a, cgt_a, *hy)
    xl, xc = hier_moe([(xl, sh_f, sc_f, gt_f), (xc, csh_f, csc_f, cgt_f)], norm_g[0, 1],
                      moe_wg[0], moe_bg[0], moe_we[0], moe_be[0], moe_w_gate, moe_w_up, moe_w_down, 0,
                      final_g, False)

    (sh_a, csh_a), (sc_a, csc_a), (gt_a, _) = mod_rows(1, 0), mod_rows(1, 1), mod_rows(1, 2)
    (sh_f, _), (sc_f, _), (gt_f, _) = mod_rows(1, 3), mod_rows(1, 4), mod_rows(1, 5)
    xl = s5_mix(xl, xc, norm_g[1, 0], sh_a, sc_a, csh_a, csc_a, gt_a, s5_a_re[0], s5_a_im[0], s5_log_step[0],
                s5_b_re[0], s5_b_im[0], s5_c_re[0], s5_c_im[0], s5_d[0], s5_w1[0], s5_b1[0], s5_w2[0], s5_b2[0])
    (out,) = hier_moe([(xl, sh_f, sc_f, gt_f)], norm_g[1, 1], moe_wg[1], moe_bg[1], moe_we[1], moe_be[1],
                      moe_w_gate, moe_w_up, moe_w_down, 1, final_g, True)
    return out
```

```python
import functools
import math

import numpy as np
import jax
import jax.numpy as jnp
from jax import lax
from jax.experimental import pallas as pl
from jax.experimental.pallas import tpu as pltpu

F32 = jnp.float32
BF16 = jnp.bfloat16
HIGHEST = lax.Precision.HIGHEST

NORM_EPS = 1e-6
HY_DECAY_TARGET = 1e-2
HY_FAST_PCT = 0.3
HY_SLOW_PCT = 1.5
TOP_K = 2

V7X_VMEM_LIMIT_BYTES = 56 * 1024 * 1024
LANES = 128
SUBLANES = 8
S5_TAU = 8
S5_SEGS = 2
S5_SCAN_UNROLL = 8
MOE_TM = 256
MOE_BM = 256
MOE_NCH = 12
MOE_STAGE = 4
MOE_CAST_ROWS = 128
NEG_BIG = -1e30


def _cparams(*sem):
    return pltpu.CompilerParams(dimension_semantics=sem, vmem_limit_bytes=V7X_VMEM_LIMIT_BYTES)


def _norm_mod(x, g, shift, scale):
    ms = jnp.mean(x * x, axis=-1, keepdims=True)
    return (x * lax.rsqrt(ms + NORM_EPS) * g) * (1.0 + scale) + shift


def _dot(a, b):
    return jnp.dot(a, b, preferred_element_type=F32)


def _ada_body(c_ref, w_ref, b_ref, o_ref):
    x = c_ref[...]
    s = (x * jax.nn.sigmoid(x)).astype(BF16)
    o_ref[...] = _dot(s, w_ref[...].astype(BF16)) + b_ref[...]


def ada_mod(c_all, ada_w, ada_b):
    depth, d, n = ada_w.shape
    tn = min(n, 1024)
    return pl.pallas_call(
        _ada_body,
        grid=(depth, n // tn),
        in_specs=[pl.BlockSpec((SUBLANES, d), lambda l, j: (0, 0)),
                  pl.BlockSpec((None, d, tn), lambda l, j: (l, 0, j)),
                  pl.BlockSpec((None, 1, tn), lambda l, j: (l, 0, j))],
        out_specs=pl.BlockSpec((None, SUBLANES, tn), lambda l, j: (l, 0, j)),
        out_shape=jax.ShapeDtypeStruct((depth, SUBLANES, n), F32),
        compiler_params=_cparams("parallel", "parallel"),
        name="ada_mod",
    )(c_all, ada_w, ada_b.reshape(depth, 1, n))


def _hy_in_body(xp_ref, xm_ref, xn_ref, g_ref, sh_ref, sc_ref,
                w0_ref, w1_ref, w2_ref, b0_ref, b1_ref, b2_ref,
                cw0_ref, cw1_ref, cw2_ref, cb0_ref, cb1_ref, cb2_ref,
                v_ref, x0_ref):
    i = pl.program_id(2)
    ni = pl.num_programs(2)
    tm = xm_ref.shape[0]
    x = jnp.concatenate([xp_ref[...], xm_ref[...], xn_ref[...]], axis=0)
    h = _norm_mod(x, g_ref[...], sh_ref[...], sc_ref[...]).astype(BF16)
    rows = lax.broadcasted_iota(jnp.int32, (tm + 2 * SUBLANES, 1), 0)
    valid = jnp.logical_and(jnp.logical_or(rows >= SUBLANES, i > 0),
                            jnp.logical_or(rows < tm + SUBLANES, i < ni - 1))

    def part(w_ref, b_ref, cw_ref, cb_ref):
        z = jnp.where(valid, _dot(h, w_ref[...]) + b_ref[...], 0.0)
        cw = cw_ref[...]
        zp = pltpu.roll(z, 1, axis=0)[SUBLANES:tm + SUBLANES]
        zn = pltpu.roll(z, tm + 2 * SUBLANES - 1, axis=0)[SUBLANES:tm + SUBLANES]
        return zp * cw[0:1] + z[SUBLANES:tm + SUBLANES] * cw[1:2] + zn * cw[2:3] + cb_ref[...]

    x0 = part(w0_ref, b0_ref, cw0_ref, cb0_ref)
    x1 = part(w1_ref, b1_ref, cw1_ref, cb1_ref)
    v = part(w2_ref, b2_ref, cw2_ref, cb2_ref) * x1
    v_ref[...] = v.astype(v_ref.dtype)
    x0_ref[...] = x0.astype(BF16)


def hyena_in(x, g, shift, scale, w_in, b_in, conv_w, conv_b, v_dtype):
    bsz, seq, d = x.shape
    tm = min(seq, 512)
    tn = min(d, 1024)
    nj = d // tn
    r8 = tm // SUBLANES
    last8 = seq // SUBLANES - 1
    row = lambda a: a.reshape(1, -1)
    wspec = lambda k: pl.BlockSpec((d, tn), lambda j, b, i: (0, k * nj + j))
    rspec = lambda k: pl.BlockSpec((1, tn), lambda j, b, i: (0, k * nj + j))
    cspec = lambda k: pl.BlockSpec((3, tn), lambda j, b, i: (0, k * nj + j))
    mspec = pl.BlockSpec((None, 1, d), lambda j, b, i: (b, 0, 0))
    out_spec = pl.BlockSpec((None, tm, tn), lambda j, b, i: (b, i, j))
    return pl.pallas_call(
        _hy_in_body,
        grid=(nj, bsz, seq // tm),
        in_specs=[pl.BlockSpec((None, SUBLANES, d), lambda j, b, i: (b, jnp.maximum(i * r8 - 1, 0), 0)),
                  pl.BlockSpec((None, tm, d), lambda j, b, i: (b, i, 0)),
                  pl.BlockSpec((None, SUBLANES, d), lambda j, b, i: (b, jnp.minimum((i + 1) * r8, last8), 0)),
                  pl.BlockSpec((1, d), lambda j, b, i: (0, 0)), mspec, mspec,
                  wspec(0), wspec(1), wspec(2), rspec(0), rspec(1), rspec(2),
                  cspec(0), cspec(1), cspec(2), rspec(0), rspec(1), rspec(2)],
        out_specs=[out_spec, out_spec],
        out_shape=[jax.ShapeDtypeStruct((bsz, seq, d), v_dtype), jax.ShapeDtypeStruct((bsz, seq, d), BF16)],
        compiler_params=_cparams("parallel", "parallel", "parallel"),
        name="hyena_in",
    )(x, x, x, row(g), shift, scale, w_in, w_in, w_in, row(b_in), row(b_in), row(b_in),
      conv_w, conv_w, conv_w, row(conv_b), row(conv_b), row(conv_b))


def _dft_tables(seq, blk):
    n = 2 * seq
    s = np.arange(seq, dtype=np.int64)[None, :]
    fl = np.arange(blk, dtype=np.int64)[:, None]
    fh = (np.arange(seq // blk, dtype=np.int64) * blk)[:, None]
    w = 2.0 * math.pi / n
    ang_b = ((fl * s) % n) * w
    ang_a = ((fh * s) % n) * w
    f32 = lambda m: jnp.asarray(m.astype(np.float32))
    return (f32(np.cos(ang_a)[:, None, :]), f32(np.sin(ang_a)[:, None, :]), f32(np.cos(ang_b)), f32(np.sin(ang_b)))


def _dft_gen_body(ca_ref, sa_ref, cb_ref, sb_ref, c_ref, s_ref):
    ca, sa, cb, sb = ca_ref[...], sa_ref[...], cb_ref[...], sb_ref[...]
    c_ref[...] = (ca * cb - sa * sb).astype(BF16)
    s_ref[...] = (sa * cb + ca * sb).astype(BF16)


def dft_matrices(seq):
    blk = min(seq, 256)
    ca, sa, cb, sb = _dft_tables(seq, blk)
    aspec = pl.BlockSpec((None, 1, seq), lambda i: (i, 0, 0))
    bspec = pl.BlockSpec((blk, seq), lambda i: (0, 0))
    ospec = pl.BlockSpec((blk, seq), lambda i: (i, 0))
    return pl.pallas_call(
        _dft_gen_body,
        grid=(seq // blk,),
        in_specs=[aspec, aspec, bspec, bspec],
        out_specs=[ospec, ospec],
        out_shape=[jax.ShapeDtypeStruct((seq, seq), BF16)] * 2,
        compiler_params=_cparams("parallel"),
        name="dft_matrices",
    )(ca, sa, cb, sb)


def _alt_sign(rows):
    return jnp.where((rows & 1) == 0, 1.0, -1.0).astype(F32)


def _filt_body(h2_ref, wf_ref, wb_ref, dl_ref, a_ref, d_ref, ny_ref):
    seq = h2_ref.shape[0]
    h2 = h2_ref[...]
    row = lax.broadcasted_iota(jnp.int32, (seq, 1), 0)
    t = row.astype(F32) * (1.0 / (seq - 1))
    win = jnp.exp(-t * dl_ref[...])
    hf = jnp.dot(h2, wf_ref[...], precision=HIGHEST, preferred_element_type=F32) * win
    hb = jnp.dot(h2, wb_ref[...], precision=HIGHEST, preferred_element_type=F32) * win
    hb = jnp.where(row == 0, 0.0, hb)
    nrm = (jnp.sum(jnp.abs(hf), axis=0, keepdims=True) + jnp.sum(jnp.abs(hb), axis=0, keepdims=True))
    inv = 1.0 / nrm
    a = (hf + hb) * inv
    a_ref[...] = a.astype(a_ref.dtype)
    d_ref[...] = ((hb - hf) * inv).astype(d_ref.dtype)
    ny = jnp.sum(a * _alt_sign(row), axis=0, keepdims=True) * (1.0 / (2 * seq))
    ny_ref[...] = jnp.broadcast_to(ny, ny_ref.shape)


def _khat_body(a_ref, d_ref, c_ref, s_ref, kr_ref, ki_ref):
    i = pl.program_id(1)
    tm = c_ref.shape[0]
    seq = c_ref.shape[1]
    f = i * tm + lax.broadcasted_iota(jnp.int32, (tm, 1), 0)
    w = jnp.where(f == 0, 1.0, 2.0).astype(F32) * (1.0 / (2 * seq))
    kr_ref[...] = _dot(c_ref[...], a_ref[...]) * w
    ki_ref[...] = _dot(s_ref[...], d_ref[...]) * w


def hyena_filter_taps(seq, fw1, fb1, fw2, fb2, fw3, freq, taps_dtype):
    d = fw3.shape[1] // 2
    bands_n = (fw1.shape[0] - 1) // 2
    t = np.linspace(0.0, 1.0, seq)[:, None]
    w = (2.0 * math.pi / seq) * np.arange(seq)[:, None]
    bands = np.linspace(1e-4, bands_n - 1, bands_n)[None, :]
    z = jnp.asarray(np.concatenate([t, np.cos(bands * w), -np.sin(bands * w)], axis=-1).astype(np.float32))
    h = jnp.sin(freq * (jnp.dot(z, fw1, precision=HIGHEST) + fb1))
    h2 = jnp.sin(freq * (jnp.dot(h, fw2, precision=HIGHEST) + fb2))
    max_decay = math.log(HY_DECAY_TARGET) / HY_FAST_PCT
    min_decay = math.log(HY_DECAY_TARGET) / HY_SLOW_PCT
    deltas = jnp.abs(jnp.linspace(min_decay, max_decay, d, dtype=F32))[None, :]

    order = h2.shape[1]
    tn = min(d, 256)
    nj = d // tn
    return pl.pallas_call(
        _filt_body,
        grid=(nj,),
        in_specs=[pl.BlockSpec((seq, order), lambda j: (0, 0)),
                  pl.BlockSpec((order, tn), lambda j: (0, j)),
                  pl.BlockSpec((order, tn), lambda j: (0, nj + j)),
                  pl.BlockSpec((1, tn), lambda j: (0, j))],
        out_specs=[pl.BlockSpec((seq, tn), lambda j: (0, j)),
                   pl.BlockSpec((seq, tn), lambda j: (0, j)),
                   pl.BlockSpec((SUBLANES, tn), lambda j: (0, j))],
        out_shape=[jax.ShapeDtypeStruct((seq, d), taps_dtype), jax.ShapeDtypeStruct((seq, d), taps_dtype),
                   jax.ShapeDtypeStruct((SUBLANES, d), F32)],
        compiler_params=_cparams("parallel"),
        name="hyena_filter_taps",
    )(h2, fw3, fw3, deltas)


def hyena_filter_dft(a, dd, cmat, smat):
    seq, d = a.shape
    tm = min(seq, 512)
    tn2 = min(d, 512)
    return pl.pallas_call(
        _khat_body,
        grid=(d // tn2, seq // tm),
        in_specs=[pl.BlockSpec((seq, tn2), lambda j, i: (0, j)),
                  pl.BlockSpec((seq, tn2), lambda j, i: (0, j)),
                  pl.BlockSpec((tm, seq), lambda j, i: (i, 0)),
                  pl.BlockSpec((tm, seq), lambda j, i: (i, 0))],
        out_specs=[pl.BlockSpec((tm, tn2), lambda j, i: (i, j))] * 2,
        out_shape=[jax.ShapeDtypeStruct((seq, d), F32)] * 2,
        compiler_params=_cparams("parallel", "parallel"),
        name="hyena_filter_dft",
    )(a, dd, cmat, smat)


def _dft_fwd_body(v_ref, c_ref, s_ref, kr_ref, ki_ref, kn_ref, ya_ref, yb_ref, yn_ref):
    i = pl.program_id(2)
    v = v_ref[...]
    vr = _dot(c_ref[...], v)
    p = _dot(s_ref[...], v)
    kr = kr_ref[...]
    ki = ki_ref[...]
    ya_ref[...] = (vr * kr + p * ki).astype(BF16)
    yb_ref[...] = (p * kr - vr * ki).astype(BF16)

    @pl.when(i == 0)
    def _():
        seq = v.shape[0]
        row = lax.broadcasted_iota(jnp.int32, (seq, 1), 0)
        vl = jnp.sum(v.astype(F32) * _alt_sign(row), axis=0, keepdims=True)
        yn_ref[...] = jnp.broadcast_to(vl * kn_ref[0:1, :], yn_ref.shape)


def _dft_inv_body(ya_ref, yb_ref, c_ref, s_ref, v_ref, x0_ref, skip_ref, yn_ref, o_ref):
    i = pl.program_id(2)
    tm = c_ref.shape[0]
    acc = _dot(c_ref[...], ya_ref[...]) + _dot(s_ref[...], yb_ref[...])
    t = i * tm + lax.broadcasted_iota(jnp.int32, (tm, 1), 0)
    y = acc + _alt_sign(t) * yn_ref[0:1, :] + skip_ref[...] * v_ref[...].astype(F32)
    o_ref[...] = (y * x0_ref[...].astype(F32)).astype(BF16)


def hyena_conv(v, x0, skip, kr, ki, kn, cmat, smat):
    bsz, seq, d = v.shape
    tm = min(seq, 512)
    tn = min(d, 512)
    grid = (bsz, d // tn, seq // tm)
    full = pl.BlockSpec((None, seq, tn), lambda b, j, i: (b, 0, j))
    mat = pl.BlockSpec((tm, seq), lambda b, j, i: (i, 0))
    tile = pl.BlockSpec((None, tm, tn), lambda b, j, i: (b, i, j))
    ktile = pl.BlockSpec((tm, tn), lambda b, j, i: (i, j))
    nyq = pl.BlockSpec((None, SUBLANES, tn), lambda b, j, i: (b, 0, j))
    ya, yb, yn = pl.pallas_call(
        _dft_fwd_body,
        grid=grid,
        in_specs=[full, mat, mat, ktile, ktile, pl.BlockSpec((SUBLANES, tn), lambda b, j, i: (0, j))],
        out_specs=[tile, tile, nyq],
        out_shape=[jax.ShapeDtypeStruct((bsz, seq, d), BF16)] * 2
        + [jax.ShapeDtypeStruct((bsz, SUBLANES, d), F32)],
        compiler_params=_cparams("parallel", "parallel", "arbitrary"),
        name="hyena_dft_fwd",
    )(v, cmat, smat, kr, ki, kn)
    return pl.pallas_call(
        _dft_inv_body,
        grid=grid,
        in_specs=[full, full, mat, mat, tile, tile, pl.BlockSpec((1, tn), lambda b, j, i: (0, j)), nyq],
        out_specs=tile,
        out_shape=jax.ShapeDtypeStruct((bsz, seq, d), BF16),
        compiler_params=_cparams("parallel", "parallel", "parallel"),
        name="hyena_dft_inv",
    )(ya, yb, cmat, smat, v, x0, skip.reshape(1, d), yn)


FFT_N2 = 128
FFT_MIN_SEQ = 1024
FFT_UNROLL = 8


def _fft_matrices(seq):
    n = 2 * seq
    n2 = FFT_N2
    n1 = n // n2
    r8 = SUBLANES
    q = np.arange(n2 // r8, dtype=np.int64)[:, None, None, None]
    f1 = np.arange(n1, dtype=np.int64)[None, :, None, None]
    r = np.arange(r8, dtype=np.int64)[None, None, :, None]
    t1 = np.arange(n1 // 2, dtype=np.int64)[None, None, None, :]
    ang = ((f1 * (t1 * n2 + q * r8 + r)) % n) * (2.0 * math.pi / n)
    g = np.stack([np.cos(ang), -np.sin(ang)], axis=3)
    eye = np.eye(r8)[None, None, :, None, None, :]
    ma = (g[..., None] * eye).reshape(n2 // r8, n1 * r8 * 2, (n1 // 2) * r8).astype(np.float32)
    f2 = np.arange(n2, dtype=np.int64)[:, None]
    t2 = np.arange(n2, dtype=np.int64)[None, :]
    th = ((f2 * t2) % n2) * (2.0 * math.pi / n2)
    co, si = np.cos(th), np.sin(th)
    wc = np.stack([np.stack([co, si], axis=-1), np.stack([-si, co], axis=-1)], axis=0)
    wc = wc.reshape(2 * n2, 2 * n2).astype(np.float32)
    as_bf16 = lambda m: jnp.asarray(np.ascontiguousarray(m).astype(BF16))
    return as_bf16(ma), as_bf16(np.swapaxes(ma, 1, 2)), as_bf16(wc), as_bf16(wc.T)


def _fft_stage_a(x_ref, ma_ref, s1):
    n1h, n_q, r8, tn = x_ref.shape
    n1 = s1.shape[0]

    def body(q, carry):
        x = x_ref[:, pl.ds(q, 1), :, :].reshape(n1h * r8, tn).astype(BF16)
        a = _dot(ma_ref[q], x).astype(BF16)
        s1[:, pl.ds(pl.multiple_of(q * 2 * r8, 2 * r8), 2 * r8), :] = a.reshape(n1, 2 * r8, tn)
        return carry

    lax.fori_loop(0, n_q, body, 0, unroll=FFT_UNROLL)


def _fft_conv_body(v_ref, x0_ref, k_ref, skip_ref, ma_ref, mat_ref, wc_ref, wci_ref, o_ref, s1, ysc):
    n1h, n_q, r8, tn = v_ref.shape
    n1 = s1.shape[0]
    n2 = s1.shape[1] // 2
    seq = n1h * n_q * r8
    _fft_stage_a(v_ref, ma_ref, s1)

    def slab(f, carry):
        y = _dot(wc_ref[...], s1[f])
        yr, yi = y[:n2], y[n2:]
        kr = k_ref[f, 0].astype(F32)
        ki = k_ref[f, 1].astype(F32)
        p = jnp.concatenate([yr * kr - yi * ki, yr * ki + yi * kr], axis=0).astype(BF16)
        s1[f] = _dot(wci_ref[...], p).astype(BF16)
        return carry

    lax.fori_loop(0, n1, slab, 0, unroll=2 * FFT_UNROLL)

    def inv_a(q, carry):
        z = s1[:, pl.ds(pl.multiple_of(q * 2 * r8, 2 * r8), 2 * r8), :].reshape(n1 * 2 * r8, tn)
        ysc[:, pl.ds(q, 1), :, :] = _dot(mat_ref[q], z).reshape(n1h, 1, r8, tn)
        return carry

    lax.fori_loop(0, n_q, inv_a, 0, unroll=FFT_UNROLL)
    y = ysc[...].reshape(seq, tn) + skip_ref[...] * v_ref[...].reshape(seq, tn)
    o_ref[...] = (y * x0_ref[...].astype(F32)).astype(BF16)


def _fft_filter_body(a_ref, d_ref, ma_ref, wc_ref, k_ref, s1):
    n1 = s1.shape[0]
    n2 = s1.shape[1] // 2
    scale = 1.0 / (n1 * n2)
    for src_ref, part, sign in ((a_ref, 0, scale), (d_ref, 1, -scale)):
        _fft_stage_a(src_ref, ma_ref, s1)

        def slab(f, carry):
            y = _dot(wc_ref[part * n2:(part + 1) * n2, :], s1[f])
            k_ref[f, part] = (y * sign).astype(BF16)
            return carry

        lax.fori_loop(0, n1, slab, 0, unroll=FFT_UNROLL)


def hyena_conv_fft(v, x0, skip, a, dd):
    bsz, seq, d = v.shape
    n2 = FFT_N2
    n1 = 2 * seq // n2
    n_q = n2 // SUBLANES
    tn = min(d, 256)
    ma, mat, wc, wci = _fft_matrices(seq)
    const = lambda shape: pl.BlockSpec(shape, lambda *_: (0,) * len(shape), pipeline_mode=pl.Buffered(1))
    view = lambda t: t.reshape(t.shape[:-2] + (n1 // 2, n_q, SUBLANES, d))
    tap = pl.BlockSpec((n1 // 2, n_q, SUBLANES, tn), lambda j: (0, 0, 0, j))
    khat = pl.pallas_call(
        _fft_filter_body,
        grid=(d // tn,),
        in_specs=[tap, tap, const(ma.shape), const(wc.shape)],
        out_specs=pl.BlockSpec((n1, 2, n2, tn), lambda j: (0, 0, 0, j)),
        out_shape=jax.ShapeDtypeStruct((n1, 2, n2, d), BF16),
        scratch_shapes=[pltpu.VMEM((n1, 2 * n2, tn), BF16)],
        compiler_params=_cparams("parallel"),
        name="hyena_filter_fft",
    )(view(a), view(dd), ma, wc)
    return pl.pallas_call(
        _fft_conv_body,
        grid=(d // tn, bsz),
        in_specs=[pl.BlockSpec((None, n1 // 2, n_q, SUBLANES, tn), lambda j, b: (b, 0, 0, 0, j)),
                  pl.BlockSpec((None, seq, tn), lambda j, b: (b, 0, j)),
                  pl.BlockSpec((n1, 2, n2, tn), lambda j, b: (0, 0, 0, j), pipeline_mode=pl.Buffered(1)),
                  pl.BlockSpec((1, tn), lambda j, b: (0, j)),
                  const(ma.shape), const(mat.shape), const(wc.shape), const(wci.shape)],
        out_specs=pl.BlockSpec((None, seq, tn), lambda j, b: (b, 0, j)),
        out_shape=jax.ShapeDtypeStruct((bsz, seq, d), BF16),
        scratch_shapes=[pltpu.VMEM((n1, 2 * n2, tn), BF16), pltpu.VMEM((n1 // 2, n_q, SUBLANES, tn), F32)],
        compiler_params=_cparams("parallel", "arbitrary"),
        name="hyena_conv_fft",
    )(view(v), x0, khat, skip.reshape(1, d), ma, mat, wc, wci)


def _mm_res_body(x_ref, w_ref, b_ref, res_ref, gate_ref, o_ref):
    o_ref[...] = res_ref[...] + gate_ref[...] * (_dot(x_ref[...], w_ref[...]) + b_ref[...])


def mm_residual(x, w, b, res, gate):
    bsz, seq, k = x.shape
    n = w.shape[1]
    tm = min(seq, 512)
    return pl.pallas_call(
        _mm_res_body,
        grid=(bsz, seq // tm),
        in_specs=[pl.BlockSpec((None, tm, k), lambda b, i: (b, i, 0)),
                  pl.BlockSpec((k, n), lambda b, i: (0, 0)),
                  pl.BlockSpec((1, n), lambda b, i: (0, 0)),
                  pl.BlockSpec((None, tm, n), lambda b, i: (b, i, 0)),
                  pl.BlockSpec((None, 1, n), lambda b, i: (b, 0, 0))],
        out_specs=pl.BlockSpec((None, tm, n), lambda b, i: (b, i, 0)),
        out_shape=jax.ShapeDtypeStruct((bsz, seq, n), F32),
        compiler_params=_cparams("parallel", "parallel"),
        name="mm_residual",
    )(x, w, b.reshape(1, n), res, gate)


def _moe_pre_body(*refs, n_groups, n_experts, tile_offs):
    n_streams = len(tile_offs) - 1
    g_ref, wr_ref, br_ref, tok_ref, eid_ref, gate_ref = refs[3 * n_streams:]
    i = pl.program_id(0)
    for k in range(n_streams):
        x_ref, sh_ref, sc_ref = refs[3 * k:3 * k + 3]

        @pl.when(jnp.logical_and(i >= tile_offs[k], i < tile_offs[k + 1]))
        def _():
            tok = _norm_mod(x_ref[...], g_ref[...], sh_ref[...], sc_ref[...])
            _route_tokens(tok, wr_ref, br_ref, tok_ref, eid_ref, gate_ref, n_groups, n_experts)


def _route_tokens(tok, wr_ref, br_ref, tok_ref, eid_ref, gate_ref, n_groups, n_experts):
    tok_ref[...] = tok
    t_hi = tok.astype(BF16)
    t_lo = (tok - t_hi.astype(F32)).astype(BF16)
    logits = (_dot(t_hi, wr_ref[0]) + _dot(t_hi, wr_ref[1]) + _dot(t_lo, wr_ref[0])) + br_ref[...]
    lane = lax.broadcasted_iota(jnp.int32, logits.shape, 1)
    per = n_experts // n_groups
    big = jnp.int32(1 << 20)
    gmask = jnp.logical_and(lane >= n_experts, lane < n_experts + n_groups)
    gl = jnp.where(gmask, logits, NEG_BIG)
    gmax = jnp.max(gl, axis=-1, keepdims=True)
    gidx = jnp.min(jnp.where(gl == gmax, lane - n_experts, big), axis=-1, keepdims=True)
    p_top = 1.0 / jnp.sum(jnp.where(gmask, jnp.exp(gl - gmax), 0.0), axis=-1, keepdims=True)
    lo = gidx * per
    emask = jnp.logical_and(lane >= lo, lane < lo + per)
    el = jnp.where(emask, logits, NEG_BIG)
    m1 = jnp.max(el, axis=-1, keepdims=True)
    i1 = jnp.min(jnp.where(el == m1, lane, big), axis=-1, keepdims=True)
    el2 = jnp.where(lane == i1, NEG_BIG, el)
    m2 = jnp.max(el2, axis=-1, keepdims=True)
    i2 = jnp.min(jnp.where(el2 == m2, lane, big), axis=-1, keepdims=True)
    e21 = jnp.exp(m2 - m1)
    g1 = p_top / (1.0 + e21)
    g2 = g1 * e21
    ids = jnp.where(lane == 0, i1, jnp.where(lane == 1, i2, -1))
    eid_ref[...] = ids.T[0:SUBLANES, :]
    gate_ref[...] = jnp.where(lane == 0, g1, jnp.where(lane == 1, g2, 0.0))


def moe_pre(streams, g, wr, br, n_groups, n_experts):
    d = streams[0][0].shape[2]
    tm = MOE_TM
    tile_offs = [0]
    in_specs, args = [], []
    for x, shift, scale in streams:
        bsz, seq, _ = x.shape
        nt = seq // tm
        n_tiles = bsz * nt
        off = tile_offs[-1]
        tile_offs.append(off + n_tiles)

        def tile(i, off=off, n_tiles=n_tiles):
            return jnp.clip(i - off, 0, n_tiles - 1)

        in_specs += [pl.BlockSpec((None, tm, d), lambda i, tile=tile, nt=nt: (tile(i) // nt, tile(i) % nt, 0)),
                     pl.BlockSpec((None, 1, d), lambda i, tile=tile, nt=nt: (tile(i) // nt, 0, 0)),
                     pl.BlockSpec((None, 1, d), lambda i, tile=tile, nt=nt: (tile(i) // nt, 0, 0))]
        args += [x, shift, scale]
    in_specs += [pl.BlockSpec((1, d), lambda i: (0, 0)),
                 pl.BlockSpec((2, d, LANES), lambda i: (0, 0, 0)),
                 pl.BlockSpec((1, LANES), lambda i: (0, 0))]
    wr_hi = wr.astype(BF16)
    wr_split = jnp.stack([wr_hi, (wr - wr_hi.astype(F32)).astype(BF16)])
    args += [g.reshape(1, d), wr_split, br]
    total = tile_offs[-1] * tm
    rout = pl.BlockSpec((tm, LANES), lambda i: (i, 0))
    tok, eid, gate = pl.pallas_call(
        functools.partial(_moe_pre_body, n_groups=n_groups, n_experts=n_experts, tile_offs=tuple(tile_offs)),
        grid=(tile_offs[-1],),
        in_specs=in_specs,
        out_specs=[pl.BlockSpec((tm, d), lambda i: (i, 0)),
                   pl.BlockSpec((None, SUBLANES, tm), lambda i: (i, 0, 0)), rout],
        out_shape=[jax.ShapeDtypeStruct((total, d), F32),
                   jax.ShapeDtypeStruct((tile_offs[-1], SUBLANES, tm), jnp.int32),
                   jax.ShapeDtypeStruct((total, LANES), F32)],
        compiler_params=_cparams("parallel"),
        name="moe_pre",
    )(*args)
    return tok, eid, gate, tile_offs[:-1]


def _start_row_gather(row_index, n_rows, src_hbm, dst_vmem, sem):
    def body(g, c):
        r0 = pl.multiple_of(g * SUBLANES, SUBLANES)
        dst_tile = dst_vmem.at[pl.ds(r0, SUBLANES)]
        for k in range(SUBLANES):
            pltpu.make_async_copy(src_hbm.at[pl.ds(row_index(r0 + k), 1)], dst_tile.at[pl.ds(k, 1)], sem).start()
        return c

    lax.fori_loop(0, n_rows // SUBLANES, body, 0, unroll=2)


def _wait_row_gather(n_rows, src_hbm, dst_vmem, sem):
    pltpu.make_async_copy(src_hbm.at[pl.ds(0, n_rows)], dst_vmem, sem).wait()


def _expert_body(bv_ref, rk_ref, pe_ref, tot_ref, src_ref, nxt_ref, tok_ref, wg_hbm, wu_hbm, wd_hbm, o_ref,
                 xbuf, xsem, wcache, stg, wsem, cnt, *, layer):
    i = pl.program_id(0)
    n = pl.num_programs(0)
    slot = i % 2
    cr, cc = stg.shape[1:]
    total = tot_ref[0]
    mats_hbm = (wg_hbm, wu_hbm, wd_hbm)

    @pl.when(i == 0)
    def _():
        cnt[0] = 0
        cnt[1] = 0

    @pl.when(jnp.logical_and(i == 0, bv_ref[0] > 0))
    def _():
        _start_row_gather(lambda r: src_ref[r], MOE_BM, tok_ref, xbuf.at[0], xsem.at[0])

    @pl.when(jnp.logical_and(i + 1 < n, bv_ref[jnp.minimum(i + 1, n - 1)] > 0))
    def _():
        _start_row_gather(lambda r: nxt_ref[r], MOE_BM, tok_ref, xbuf.at[1 - slot], xsem.at[1 - slot])

    def chunk_geom(c):
        q = c % MOE_NCH
        m = q // 4
        sub = q % 4
        r0 = jnp.where(m < 2, sub, sub // 2) * cr
        c0 = jnp.where(m < 2, 0, sub % 2) * cc
        return m, pl.multiple_of(r0, cr), pl.multiple_of(c0, cc)

    def issue(c):
        e = pe_ref[c // MOE_NCH]
        m, r0, c0 = chunk_geom(c)
        s = c % MOE_STAGE
        for k, w_hbm in enumerate(mats_hbm):
            @pl.when(m == k)
            def _():
                pltpu.make_async_copy(w_hbm.at[layer, e, pl.ds(r0, cr), pl.ds(c0, cc)], stg.at[s],
                                      wsem.at[s]).start()

    def cast(c):
        s = c % MOE_STAGE
        pltpu.make_async_copy(wg_hbm.at[layer, 0, pl.ds(0, cr), pl.ds(0, cc)], stg.at[s], wsem.at[s]).wait()
        ws = (c // MOE_NCH) % 2
        q = c % MOE_NCH
        step = min(MOE_CAST_ROWS, cr)
        assert cr % step == 0

        def slab(k, carry):
            rows = pl.ds(pl.multiple_of(k * step, step), step)
            wcache[ws, q, rows, :] = stg[s, rows, :].astype(BF16)
            return carry

        lax.fori_loop(0, cr // step, slab, 0)

    valid = bv_ref[i] > 0
    rank = rk_ref[i]
    issued = cnt[0]
    done = cnt[1]
    limit = jnp.minimum(total, MOE_NCH * (rank + 2))
    need = jnp.where(valid, MOE_NCH * (rank + 1), done)

    def fill(issued, done):
        hi = jnp.minimum(limit, done + MOE_STAGE)

        def body(c, carry):
            issue(c)
            return carry

        lax.fori_loop(issued, hi, body, 0)
        return jnp.maximum(issued, hi)

    def cast_and_refill(c, issued):
        cast(c)
        more = issued < jnp.minimum(limit, c + 1 + MOE_STAGE)

        @pl.when(more)
        def _():
            issue(issued)

        return issued + more.astype(jnp.int32)

    issued = fill(issued, done)
    issued = lax.fori_loop(done, need, cast_and_refill, issued)
    done = jnp.maximum(done, need)

    @pl.when(valid)
    def _():
        ws = rank % 2
        _wait_row_gather(MOE_BM, tok_ref, xbuf.at[slot], xsem.at[slot])
        x = xbuf[slot].astype(BF16)
        gate = sum(_dot(x[:, k * cr:(k + 1) * cr], wcache[ws, k]) for k in range(4))
        up = sum(_dot(x[:, k * cr:(k + 1) * cr], wcache[ws, 4 + k]) for k in range(4))
        h = (gate * jax.nn.sigmoid(gate) * up).astype(BF16)
        for half in range(2):
            o_ref[:, half * cc:(half + 1) * cc] = sum(
                _dot(h[:, k * cr:(k + 1) * cr], wcache[ws, 8 + 2 * k + half]) for k in range(2))

    @pl.when(jnp.logical_not(valid))
    def _():
        o_ref[...] = jnp.zeros_like(o_ref)

    fetched = issued
    issued = lax.fori_loop(done, fetched, cast_and_refill, issued)
    done = jnp.maximum(done, fetched)
    last = i == n - 1
    tail = jnp.where(last, issued, done)

    def drain(c, carry):
        cast(c)
        return carry

    lax.fori_loop(done, tail, drain, 0)
    cnt[0] = issued
    cnt[1] = jnp.maximum(done, tail)


def moe_experts(tok, src_tok, block_valid, block_rank, present, n_chunks, w_gate, w_up, w_down, layer):
    d = tok.shape[1]
    n_rows = src_tok.shape[0]
    n_blocks = n_rows // MOE_BM
    dh = w_gate.shape[3]
    assert 2 * dh == d and MOE_NCH == 12
    cr, cc = d // 4, dh
    any_spec = pl.BlockSpec(memory_space=pl.ANY)
    grid_spec = pltpu.PrefetchScalarGridSpec(
        num_scalar_prefetch=4,
        grid=(n_blocks,),
        in_specs=[pl.BlockSpec((MOE_BM,), lambda i, *_: (i,), memory_space=pltpu.SMEM),
                  pl.BlockSpec((MOE_BM,), lambda i, *_: (jnp.minimum(i + 1, n_blocks - 1),),
                               memory_space=pltpu.SMEM),
                  any_spec, any_spec, any_spec, any_spec],
        out_specs=pl.BlockSpec((MOE_BM, d), lambda i, *_: (i, 0)),
        scratch_shapes=[pltpu.VMEM((2, MOE_BM, d), F32), pltpu.SemaphoreType.DMA((2,)),
                        pltpu.VMEM((2, MOE_NCH, cr, cc), BF16),
                        pltpu.VMEM((MOE_STAGE, cr, cc), F32), pltpu.SemaphoreType.DMA((MOE_STAGE,)),
                        pltpu.SMEM((2,), jnp.int32)],
    )
    return pl.pallas_call(
        functools.partial(_expert_body, layer=layer),
        grid_spec=grid_spec,
        out_shape=jax.ShapeDtypeStruct((n_rows, d), F32),
        compiler_params=_cparams("arbitrary"),
        name="moe_experts",
    )(block_valid, block_rank, present, n_chunks, src_tok, src_tok, tok, w_gate, w_up, w_down)


def _combine_body(dest_ref, nxt_ref, os_ref, gate_ref, res_ref, gt_ref, fg_ref, o_ref, buf, sem, *, final_norm):
    rows = res_ref.shape[0]
    i = pl.program_id(0)
    n = pl.num_programs(0)
    slot = i % 2

    def start(idx_ref, s):
        for k in range(TOP_K):
            _start_row_gather(lambda r, k=k: idx_ref[k, r], rows, os_ref, buf.at[s, k], sem.at[s])

    @pl.when(i == 0)
    def _():
        start(dest_ref, 0)

    @pl.when(i + 1 < n)
    def _():
        start(nxt_ref, 1 - slot)

    for k in range(TOP_K):
        _wait_row_gather(rows, os_ref, buf.at[slot, k], sem.at[slot])
    gates = gate_ref[...]
    mo = gates[:, 0:1] * buf[slot, 0] + gates[:, 1:2] * buf[slot, 1]
    y = res_ref[...] + gt_ref[...] * mo
    if final_norm:
        ms = jnp.mean(y * y, axis=-1, keepdims=True)
        y = y * lax.rsqrt(ms + NORM_EPS) * fg_ref[...]
    o_ref[...] = y


def moe_combine(os, dest, gates, tile0, res, gt, final_g, final_norm):
    bsz, seq, d = res.shape
    rows = MOE_TM
    nt = seq // rows
    n = bsz * nt
    tile = pl.BlockSpec((None, rows, d), lambda i: (i // nt, i % nt, 0))
    return pl.pallas_call(
        functools.partial(_combine_body, final_norm=final_norm),
        grid=(n,),
        in_specs=[pl.BlockSpec((None, SUBLANES, rows), lambda i: (tile0 + i, 0, 0), memory_space=pltpu.SMEM),
                  pl.BlockSpec((None, SUBLANES, rows), lambda i: (tile0 + jnp.minimum(i + 1, n - 1), 0, 0),
                               memory_space=pltpu.SMEM),
                  pl.BlockSpec(memory_space=pl.ANY),
                  pl.BlockSpec((rows, LANES), lambda i: (tile0 + i, 0)),
                  tile,
                  pl.BlockSpec((None, 1, d), lambda i: (i // nt, 0, 0)),
                  pl.BlockSpec((1, d), lambda i: (0, 0))],
        out_specs=tile,
        out_shape=jax.ShapeDtypeStruct((bsz, seq, d), F32),
        scratch_shapes=[pltpu.VMEM((2, TOP_K, rows, d), F32), pltpu.SemaphoreType.DMA((2,))],
        compiler_params=_cparams("arbitrary"),
        name="moe_combine",
    )(dest, dest, os, gates, res, gt, final_g.reshape(1, d))


def _plan_body(e_ref, dest_ref, tab_ref, *, n_experts):
    n_rows, w = e_ref.shape
    e_all = e_ref[...]
    li = lax.broadcasted_iota(jnp.int32, (w, w), 0)
    lj = lax.broadcasted_iota(jnp.int32, (w, w), 1)
    incl = (li <= lj).astype(BF16)
    ri = lax.broadcasted_iota(jnp.int32, (n_rows, n_rows), 0)
    rj = lax.broadcasted_iota(jnp.int32, (n_rows, n_rows), 1)
    before = (rj < ri).astype(BF16)
    elane = lax.broadcasted_iota(jnp.int32, (n_rows, LANES), 1)
    row_tot = jnp.zeros((n_rows, LANES), F32)
    for e in range(n_experts):
        tot = jnp.sum((e_all == e).astype(F32), axis=1, keepdims=True)
        row_tot = row_tot + jnp.where(elane == e, tot, 0.0)
    rows_before = _dot(before, row_tot.astype(BF16))
    counts = jnp.sum(row_tot, axis=0, keepdims=True).astype(jnp.int32)
    lane1 = lax.broadcasted_iota(jnp.int32, (1, LANES), 1)

    def excl_prefix(v):
        acc = v
        sh = 1
        while sh < LANES:
            acc = acc + jnp.where(lane1 >= sh, pltpu.roll(acc, sh, axis=1), 0)
            sh *= 2
        return acc - v

    start = excl_prefix(counts)
    padded = (counts + (MOE_BM - 1)) // MOE_BM * MOE_BM
    pad_start = excl_prefix(padded)
    tab = jnp.concatenate([counts, start, pad_start, pad_start + padded,
                           jnp.zeros((SUBLANES - 4, LANES), jnp.int32)], axis=0)
    tab_ref[...] = tab
    base = rows_before + pad_start.astype(F32)
    dest = jnp.zeros((n_rows, w), F32)
    for e in range(n_experts):
        hit = e_all == e
        within = _dot(hit.astype(BF16), incl)
        dest = dest + jnp.where(hit, within - 1.0 + base[:, e:e + 1], 0.0)
    dest_ref[...] = dest.astype(jnp.int32)


def _route_plan(eid, n_experts):
    n_tiles, r8, tm = eid.shape
    dest, tab = pl.pallas_call(
        functools.partial(_plan_body, n_experts=n_experts),
        out_shape=[jax.ShapeDtypeStruct((n_tiles * r8, tm), jnp.int32),
                   jax.ShapeDtypeStruct((SUBLANES, LANES), jnp.int32)],
        compiler_params=pltpu.CompilerParams(vmem_limit_bytes=V7X_VMEM_LIMIT_BYTES),
        name="moe_route_plan",
    )(eid.reshape(n_tiles * r8, tm))
    counts, start, pad_start, pad_end = (tab[k, :n_experts] for k in range(4))
    e_flat = eid[:, :TOP_K, :].reshape(-1)
    a = e_flat.shape[0]
    order = jnp.argsort(e_flat)
    tok_of = (order // (TOP_K * tm)) * tm + order % tm
    n_blocks = -(-a // MOE_BM) + n_experts
    starts = jnp.arange(n_blocks, dtype=jnp.int32) * MOE_BM
    block_expert = jnp.minimum(jnp.sum((pad_end[None, :] <= starts[:, None]).astype(jnp.int32), axis=1),
                               n_experts - 1)
    block_valid = (starts < pad_end[-1]).astype(jnp.int32)
    lane = jnp.arange(MOE_BM, dtype=jnp.int32)[None, :]
    blk_off = (starts - pad_start[block_expert])[:, None] + lane
    live = jnp.logical_and(blk_off < counts[block_expert][:, None], block_valid[:, None] > 0)
    sorted_pos = jnp.clip(start[block_expert][:, None] + blk_off, 0, a - 1)
    src_tok = jnp.where(live, tok_of[sorted_pos], 0).reshape(-1)
    has = (counts > 0).astype(jnp.int32)
    block_rank = (jnp.cumsum(has) - 1)[block_expert]
    present = jnp.argsort(1 - has)
    n_chunks = (MOE_NCH * jnp.sum(has)).reshape(1)
    i32 = lambda v: v.astype(jnp.int32)
    return (dest.reshape(n_tiles, r8, tm), i32(src_tok), block_valid, i32(block_rank), i32(present),
            i32(n_chunks))


def hier_moe(streams, norm_g, wg, bg, we, be, w_gate, w_up, w_down, layer, final_g, final_norm):
    n_groups = wg.shape[1]
    n_experts = we.shape[1]
    d = wg.shape[0]
    wr = jnp.zeros((d, LANES), F32).at[:, :n_experts].set(we).at[:, n_experts:n_experts + n_groups].set(wg)
    br = jnp.zeros((1, LANES), F32).at[0, :n_experts].set(be).at[0, n_experts:n_experts + n_groups].set(bg)
    tok, eid, gates, tile0s = moe_pre([s[:3] for s in streams], norm_g, wr, br, n_groups, n_experts)
    dest, src_tok, block_valid, block_rank, present, n_chunks = _route_plan(eid, n_experts)
    os = moe_experts(tok, src_tok, block_valid, block_rank, present, n_chunks, w_gate, w_up, w_down, layer)
    return [moe_combine(os, dest, gates, tile0, x, gt, final_g, final_norm)
            for (x, _, _, gt), tile0 in zip(streams, tile0s)]


def _norm_mod_body(x_ref, g_ref, sh_ref, sc_ref, o_ref):
    o_ref[...] = _norm_mod(x_ref[...], g_ref[...], sh_ref[...], sc_ref[...]).astype(o_ref.dtype)


def norm_mod(x, g, shift, scale):
    bsz, seq, d = x.shape
    tm = min(seq, 512)
    mspec = pl.BlockSpec((None, 1, d), lambda b, i: (b, 0, 0))
    return pl.pallas_call(
        _norm_mod_body,
        grid=(bsz, seq // tm),
        in_specs=[pl.BlockSpec((None, tm, d), lambda b, i: (b, i, 0)),
                  pl.BlockSpec((1, d), lambda b, i: (0, 0)), mspec, mspec],
        out_specs=pl.BlockSpec((None, tm, d), lambda b, i: (b, i, 0)),
        out_shape=jax.ShapeDtypeStruct((bsz, seq, d), BF16),
        compiler_params=_cparams("parallel", "parallel"),
        name="norm_mod",
    )(x, g.reshape(1, d), shift, scale)


def _s5_arrange(h):
    bsz, t, d = h.shape
    c = t // (S5_SEGS * S5_TAU)
    h = h.reshape(bsz, S5_SEGS, c, S5_TAU, d // LANES, LANES)
    return h.transpose(2, 0, 1, 4, 3, 5).reshape(c * bsz * S5_SEGS, d * S5_TAU)


def _s5_unarrange(y, bsz):
    r, w = y.shape
    d = w // S5_TAU
    c = r // (bsz * S5_SEGS)
    y = y.reshape(c, bsz, S5_SEGS, d // LANES, S5_TAU, LANES)
    return y.transpose(1, 2, 0, 4, 3, 5).reshape(bsz, S5_SEGS * c * S5_TAU, d)


def _s5_operators(a_re, a_im, log_step, b_re, b_im, c_re, c_im):
    n_g, n_p = a_re.shape[1:]
    n_h = b_re.shape[-1]
    gpt = LANES // n_h
    n_j = n_g // gpt
    tau = S5_TAU
    assert tau * n_h == LANES and 2 * n_p == LANES
    lam_step = lax.complex(a_re, a_im) * jnp.exp(log_step)[..., None]
    lam_bar = jnp.exp(lam_step)
    b_bar = ((lam_bar - 1.0) / lax.complex(a_re, a_im))[..., None] * lax.complex(b_re, b_im)
    c_mat = lax.complex(c_re, c_im)
    ks = jnp.arange(tau + 1, dtype=F32)[None, :, None, None]
    pw = jnp.exp(lam_step[:, None] * ks)
    ein = functools.partial(jnp.einsum, precision=HIGHEST)
    inj_c, cl_c, lt = [], [], []
    tz_c = 0.0
    for d in range(2):
        pos = jnp.arange(tau) if d == 0 else jnp.arange(tau)[::-1]
        inj = (pw[d][tau - 1 - pos][..., None] * b_bar[d][None]).reshape(tau, n_j, gpt, n_p, n_h)
        inj = inj.transpose(1, 0, 2, 4, 3).reshape(n_j, tau * LANES, n_p)
        inj_c.append(jnp.concatenate([inj.real, inj.imag], axis=-1))
        cl = (c_mat[d][None] * pw[d][pos + 1][:, :, None, :]).reshape(tau, n_j, gpt, n_h, n_p)
        cl = cl.transpose(1, 2, 4, 0, 3).reshape(n_j, gpt * n_p, tau * n_h)
        cl_c.append(jnp.concatenate([cl.real, -cl.imag], axis=1))
        mk = ein('gop,kgp,gph->kgoh', c_mat[d], pw[d][:tau], b_bar[d]).real
        diff = pos[:, None] - pos[None, :]
        tz = jnp.where((diff >= 0)[:, :, None, None, None], mk[jnp.clip(diff, 0, tau - 1)], 0.0)
        tz = tz.reshape(tau, tau, n_j, gpt, n_h, n_h)
        tz_c = tz_c + tz.transpose(2, 1, 3, 5, 0, 4).reshape(n_j, tau * LANES, tau * n_h)
        lt_d = pw[d][tau].reshape(n_j, 1, gpt * n_p)
        lt.append(jnp.concatenate([lt_d.real, lt_d.imag], axis=-1))
    ws, wu, wh = s5_expand(jnp.stack(inj_c).astype(BF16), jnp.stack(cl_c).astype(BF16), tz_c.astype(BF16),
                           n_h, n_p)
    return ws, wu, wh, jnp.stack(lt).astype(F32)


def _s5_expand_body(inj_ref, cl_ref, tz_ref, ws_ref, wu_ref, wh_ref, *, n_h, n_p):
    rows = tz_ref.shape[0]
    gpt = LANES // n_h
    row = lax.broadcasted_iota(jnp.int32, (rows, LANES), 0)
    lane = lax.broadcasted_iota(jnp.int32, (rows, LANES), 1)
    sel_r = lax.broadcasted_iota(jnp.int32, (LANES, LANES), 0)
    sel_l = lax.broadcasted_iota(jnp.int32, (LANES, LANES), 1)
    grp_in = (row // n_h) % gpt
    grp_st = (row // n_p) % gpt

    def spread_out(m, t, grp_row):
        sel = jnp.logical_and(sel_r // n_h == t, sel_r % n_h == sel_l % n_h).astype(BF16)
        return jnp.where(grp_row == lane // n_h, _dot(m, sel), 0.0).astype(BF16)

    def spread_state(m, c, q, grp_row):
        sel = jnp.logical_and(sel_r // n_p == c, sel_r % n_p == sel_l % n_p).astype(BF16)
        return jnp.where(grp_row == (LANES // n_p) * q + lane // n_p, _dot(m, sel), 0.0).astype(BF16)

    w2 = 2 * gpt * n_p
    tz = tz_ref[...]
    for t in range(S5_TAU):
        wu_ref[:, t * LANES:(t + 1) * LANES] = spread_out(tz, t, grp_in)
    for d in range(2):
        cl = cl_ref[d]
        inj = inj_ref[d]
        for t in range(S5_TAU):
            wh_ref[d, :, t * LANES:(t + 1) * LANES] = spread_out(cl, t, grp_st)
        for c in range(2):
            for q in range(gpt * n_p // LANES):
                lo = d * w2 + c * gpt * n_p + q * LANES
                ws_ref[:, lo:lo + LANES] = spread_state(inj, c, q, grp_in)


def s5_expand(inj_c, cl_c, tz_c, n_h, n_p):
    n_j, rows, _ = tz_c.shape
    gpt = LANES // n_h
    w2 = 2 * gpt * n_p
    assert rows == S5_TAU * LANES == w2
    cspec = pl.BlockSpec((2, None, rows, LANES), lambda j: (0, j, 0, 0))
    return pl.pallas_call(
        functools.partial(_s5_expand_body, n_h=n_h, n_p=n_p),
        grid=(n_j,),
        in_specs=[cspec, cspec, pl.BlockSpec((None, rows, LANES), lambda j: (j, 0, 0))],
        out_specs=[pl.BlockSpec((None, rows, 2 * w2), lambda j: (j, 0, 0)),
                   pl.BlockSpec((None, rows, rows), lambda j: (j, 0, 0)),
                   pl.BlockSpec((2, None, w2, rows), lambda j: (0, j, 0, 0))],
        out_shape=[jax.ShapeDtypeStruct((n_j, rows, 2 * w2), BF16),
                   jax.ShapeDtypeStruct((n_j, rows, rows), BF16),
                   jax.ShapeDtypeStruct((2, n_j, w2, rows), BF16)],
        compiler_params=_cparams("parallel"),
        name="s5_expand",
    )(inj_c, cl_c, tz_c)


def _s5_inj_body(x_ref, w_ref, o_ref):
    o_ref[...] = _dot(x_ref[...], w_ref[...])


def s5_inject(xr, ws):
    r = xr.shape[0]
    n_j, k, n = ws.shape
    tm = r // 2 if r % 32 == 0 else r
    return pl.pallas_call(
        _s5_inj_body,
        grid=(n_j, r // tm),
        in_specs=[pl.BlockSpec((tm, k), lambda j, i: (i, j)),
                  pl.BlockSpec((None, k, n), lambda j, i: (j, 0, 0))],
        out_specs=pl.BlockSpec((tm, n), lambda j, i: (i, j)),
        out_shape=jax.ShapeDtypeStruct((r, n_j * n), F32),
        compiler_params=_cparams("parallel", "parallel"),
        name="s5_inject",
    )(xr, ws)


def _cmul(ar, ai, br, bi):
    return ar * br - ai * bi, ar * bi + ai * br


def _s5_scan_body(s_ref, lt_ref, h_ref, raw_ref, *, n_ctx, n_lat, bsz):
    d = pl.program_id(1)
    w2 = lt_ref.shape[-1]
    w = w2 // 2
    rows = bsz * S5_SEGS
    seg = lax.broadcasted_iota(jnp.int32, (rows, 1), 0) % S5_SEGS
    is_late = seg != d
    lam_r = lt_ref[:, 0:w]
    lam_i = lt_ref[:, w:w2]
    zero = jnp.zeros((rows, w), F32)
    one = (jnp.ones((1, w), F32), jnp.zeros((1, w), F32))

    def swap_segments(x):
        return jnp.where(seg == 0, pltpu.roll(x, rows - 1, axis=0), pltpu.roll(x, 1, axis=0))

    def phase(c0, n_steps, hin_r, hin_i, write):
        def chunk(k):
            return c0 + jnp.where(d == 0, k, n_steps - 1 - k)

        def step_raw(k, carry):
            hr, hi = carry
            c = chunk(k)
            raw_ref[c - c0, :, 0:w] = hr
            raw_ref[c - c0, :, w:w2] = hi
            nr, ni = _cmul(lam_r, lam_i, hr, hi)
            return nr + s_ref[c, :, 0:w], ni + s_ref[c, :, w:w2]

        er, ei = lax.fori_loop(0, n_steps, step_raw, (zero, zero), unroll=S5_SCAN_UNROLL)
        pr, pi = lax.fori_loop(0, n_steps, lambda k, q: _cmul(lam_r, lam_i, *q), one)
        dr, di = _cmul(pr, pi, hin_r, hin_i)
        first_r = jnp.where(is_late, 0.0, er + dr)
        first_i = jnp.where(is_late, 0.0, ei + di)
        carry_r = jnp.where(is_late, swap_segments(first_r), hin_r)
        carry_i = jnp.where(is_late, swap_segments(first_i), hin_i)
        if write:
            def step_fix(k, q):
                c = chunk(k)
                fr, fi = _cmul(q[0], q[1], carry_r, carry_i)
                h_ref[c - c0, :, 0:w] = (raw_ref[c - c0, :, 0:w] + fr).astype(h_ref.dtype)
                h_ref[c - c0, :, w:w2] = (raw_ref[c - c0, :, w:w2] + fi).astype(h_ref.dtype)
                return _cmul(lam_r, lam_i, q[0], q[1])

            lax.fori_loop(0, n_steps, step_fix, one, unroll=S5_SCAN_UNROLL)
        lr, li = _cmul(pr, pi, carry_r, carry_i)
        last_r = jnp.where(is_late, er + lr, 0.0)
        last_i = jnp.where(is_late, ei + li, 0.0)
        return (jnp.where(is_late, 0.0, swap_segments(last_r)), jnp.where(is_late, 0.0, swap_segments(last_i)))

    hr, hi = zero, zero
    if n_ctx:
        hr, hi = phase(0, n_ctx, hr, hi, False)
    phase(n_ctx, n_lat, hr, hi, True)


def s5_scan(s, lt, n_ctx, n_lat, bsz):
    assert S5_SEGS == 2
    n_c, rows, _ = s.shape
    n_j = lt.shape[1]
    w2 = lt.shape[-1]
    return pl.pallas_call(
        functools.partial(_s5_scan_body, n_ctx=n_ctx, n_lat=n_lat, bsz=bsz),
        grid=(n_j, 2),
        in_specs=[pl.BlockSpec((n_c, rows, w2), lambda j, d: (0, 0, 2 * j + d)),
                  pl.BlockSpec((None, None, 1, w2), lambda j, d: (d, j, 0, 0))],
        out_specs=pl.BlockSpec((None, n_lat, rows, w2), lambda j, d: (d, 0, 0, j)),
        out_shape=jax.ShapeDtypeStruct((2, n_lat, rows, n_j * w2), BF16),
        scratch_shapes=[pltpu.VMEM((max(n_ctx, n_lat), rows, w2), F32)],
        compiler_params=_cparams("parallel", "parallel"),
        name="s5_scan",
    )(s, lt)


def _s5_out_body(x_ref, hf_ref, hb_ref, wu_ref, whf_ref, whb_ref, o_ref):
    o_ref[...] = (_dot(x_ref[...], wu_ref[...]) + _dot(hf_ref[...], whf_ref[...])
                  + _dot(hb_ref[...], whb_ref[...])).astype(o_ref.dtype)


def s5_readout(xr, h, wu, wh):
    r = xr.shape[0]
    n_j, k, n = wu.shape
    w2 = wh.shape[2]
    tm = min(r, 1024)
    return pl.pallas_call(
        _s5_out_body,
        grid=(n_j, r // tm),
        in_specs=[pl.BlockSpec((tm, k), lambda j, i: (i, j)),
                  pl.BlockSpec((None, tm, w2), lambda j, i: (0, i, j)),
                  pl.BlockSpec((None, tm, w2), lambda j, i: (1, i, j)),
                  pl.BlockSpec((None, k, n), lambda j, i: (j, 0, 0)),
                  pl.BlockSpec((None, None, w2, n), lambda j, i: (0, j, 0, 0)),
                  pl.BlockSpec((None, None, w2, n), lambda j, i: (1, j, 0, 0))],
        out_specs=pl.BlockSpec((tm, n), lambda j, i: (i, j)),
        out_shape=jax.ShapeDtypeStruct((r, n_j * n), BF16),
        compiler_params=_cparams("parallel", "parallel"),
        name="s5_readout",
    )(xr, h, h, wu, wh, wh)


def _gelu_tanh(x):
    return 0.5 * x * (1.0 + jnp.tanh(math.sqrt(2.0 / math.pi) * (x + 0.044715 * (x * x * x))))


def _s5_glu_body(x_ref, y_ref, g_ref, sh_ref, sc_ref, dk_ref, w1_ref, w2_ref, b1_ref, b2_ref, gt_ref, o_ref):
    x = x_ref[...]
    u = _norm_mod(x, g_ref[...], sh_ref[...], sc_ref[...])
    y = _gelu_tanh(y_ref[...].astype(F32) + dk_ref[...] * u).astype(BF16)
    o = (_dot(y, w1_ref[...]) + b1_ref[...]) * jax.nn.sigmoid(_dot(y, w2_ref[...]) + b2_ref[...])
    o_ref[...] = x + gt_ref[...] * o


def s5_glu(x, y, g, shift, scale, d_skip, w1, b1, w2, b2, gate):
    bsz, seq, d = x.shape
    tm = min(seq, 512)
    row = lambda a: a.reshape(1, d)
    rspec = pl.BlockSpec((1, d), lambda b, i: (0, 0))
    mspec = pl.BlockSpec((None, 1, d), lambda b, i: (b, 0, 0))
    tile = pl.BlockSpec((None, tm, d), lambda b, i: (b, i, 0))
    wspec = pl.BlockSpec((d, d), lambda b, i: (0, 0), pipeline_mode=pl.Buffered(1))
    return pl.pallas_call(
        _s5_glu_body,
        grid=(bsz, seq // tm),
        in_specs=[tile, tile, rspec, mspec, mspec, rspec, wspec, wspec, rspec, rspec, mspec],
        out_specs=tile,
        out_shape=jax.ShapeDtypeStruct((bsz, seq, d), F32),
        compiler_params=_cparams("parallel", "parallel"),
        name="s5_glu",
    )(x, y, row(g), shift, scale, row(d_skip), w1, w2, row(b1), row(b2), gate)


def s5_mix(xl, xc, g, sh_l, sc_l, sh_c, sc_c, gate_l, a_re, a_im, log_step, b_re, b_im, c_re, c_im, d_skip,
           w1, b1, w2, b2):
    bsz, seq, d = xl.shape
    hl = norm_mod(xl, g, sh_l, sc_l)
    hc = norm_mod(xc, g, sh_c, sc_c)
    xr_c = _s5_arrange(hc)
    xr_l = _s5_arrange(hl)
    rows = bsz * S5_SEGS
    n_ctx = xr_c.shape[0] // rows
    n_lat = xr_l.shape[0] // rows
    ws, wu, wh, lt = _s5_operators(a_re, a_im, log_step, b_re, b_im, c_re, c_im)
    s = s5_inject(jnp.concatenate([xr_c, xr_l], axis=0), ws)
    h = s5_scan(s.reshape(n_ctx + n_lat, rows, -1), lt, n_ctx, n_lat, bsz)
    y = s5_readout(xr_l, h.reshape(2, n_lat * rows, -1), wu, wh)
    y = _s5_unarrange(y, bsz)
    return s5_glu(xl, y, g, sh_l, sc_l, d_skip, w1.astype(BF16), b1, w2.astype(BF16), b2, gate_l)


def hyena_mix(x, g, shift, scale, gate, w_in, b_in, conv_w, conv_b, fw1, fb1, fw2, fb2, fw3, freq, skip,
              w_out, b_out):
    seq = x.shape[1]
    if seq >= FFT_MIN_SEQ and (2 * seq) % (2 * FFT_N2) == 0:
        a, dd, _ = hyena_filter_taps(seq, fw1, fb1, fw2, fb2, fw3, freq, F32)
        v, x0 = hyena_in(x, g, shift, scale, w_in, b_in, conv_w, conv_b, F32)
        yg = hyena_conv_fft(v, x0, skip, a, dd)
    else:
        cmat, smat = dft_matrices(seq)
        a, dd, kn = hyena_filter_taps(seq, fw1, fb1, fw2, fb2, fw3, freq, BF16)
        kr, ki = hyena_filter_dft(a, dd, cmat, smat)
        v, x0 = hyena_in(x, g, shift, scale, w_in, b_in, conv_w, conv_b, BF16)
        yg = hyena_conv(v, x0, skip, kr, ki, kn, cmat, smat)
    return mm_residual(yg, w_out, b_out, x, gate)


def kernel(x, c, ctx, c_ctx, ada_w, ada_b, norm_g, final_g, hy_w_in, hy_b_in, hy_conv_w, hy_conv_b, hy_fw1,
           hy_fb1, hy_fw2, hy_fb2, hy_fw3, hy_freq, hy_skip, hy_w_out, hy_b_out, s5_a_re, s5_a_im,
           s5_log_step, s5_b_re, s5_b_im, s5_c_re, s5_c_im, s5_d, s5_w1, s5_b1, s5_w2, s5_b2, moe_wg, moe_bg,
           moe_we, moe_be, moe_w_gate, moe_w_up, moe_w_down):
    bsz, _, d = x.shape
    depth = ada_w.shape[0]
    assert depth == 2 and bsz < SUBLANES
    c_all = jnp.zeros((SUBLANES, d), F32).at[:bsz].set(c).at[bsz].set(c_ctx)
    mods = ada_mod(c_all, ada_w, ada_b)

    def mod_rows(layer, k):
        lat = mods[layer, :bsz, k * d:(k + 1) * d][:, None, :]
        cx = jnp.broadcast_to(mods[layer, bsz, k * d:(k + 1) * d][None, None, :], (bsz, 1, d))
        return lat, cx

    (sh_a, csh_a), (sc_a, csc_a), (gt_a, cgt_a) = mod_rows(0, 0), mod_rows(0, 1), mod_rows(0, 2)
    (sh_f, csh_f), (sc_f, csc_f), (gt_f, cgt_f) = mod_rows(0, 3), mod_rows(0, 4), mod_rows(0, 5)
    hy = (hy_w_in[0].astype(BF16), hy_b_in[0], hy_conv_w[0], hy_conv_b[0], hy_fw1[0], hy_fb1[0], hy_fw2[0],
          hy_fb2[0], hy_fw3[0], hy_freq[0], hy_skip[0], hy_w_out[0].astype(BF16), hy_b_out[0])
    xl = hyena_mix(x, norm_g[0, 0], sh_a, sc_a, gt_a, *hy)
    xc = hyena_mix(ctx, norm_g[0, 0], csh_a, csc_a, cgt_a, *hy)
    xl, xc = hier_moe([(xl, sh_f, sc_f, gt_f), (xc, csh_f, csc_f, cgt_f)], norm_g[0, 1],
                      moe_wg[0], moe_bg[0], moe_we[0], moe_be[0], moe_w_gate, moe_w_up, moe_w_down, 0,
                      final_g, False)

    (sh_a, csh_a), (sc_a, csc_a), (gt_a, _) = mod_rows(1, 0), mod_rows(1, 1), mod_rows(1, 2)
    (sh_f, _), (sc_f, _), (gt_f, _) = mod_rows(1, 3), mod_rows(1, 4), mod_rows(1, 5)
    xl = s5_mix(xl, xc, norm_g[1, 0], sh_a, sc_a, csh_a, csc_a, gt_a, s5_a_re[0], s5_a_im[0], s5_log_step[0],
                s5_b_re[0], s5_b_im[0], s5_c_re[0], s5_c_im[0], s5_d[0], s5_w1[0], s5_b1[0], s5_w2[0], s5_b2[0])
    (out,) = hier_moe([(xl, sh_f, sc_f, gt_f)], norm_g[1, 1], moe_wg[1], moe_bg[1], moe_we[1], moe_be[1],
                      moe_w_gate, moe_w_up, moe_w_down, 1, final_g, True)
    return out
```

```python
import functools
import math

import numpy as np
import jax
import jax.numpy as jnp
from jax import lax
from jax.experimental import pallas as pl
from jax.experimental.pallas import tpu as pltpu

F32 = jnp.float32
BF16 = jnp.bfloat16
HIGHEST = lax.Precision.HIGHEST

NORM_EPS = 1e-6
HY_DECAY_TARGET = 1e-2
HY_FAST_PCT = 0.3
HY_SLOW_PCT = 1.5
TOP_K = 2

V7X_VMEM_LIMIT_BYTES = 56 * 1024 * 1024
LANES = 128
SUBLANES = 8
S5_TAU = 8
S5_SEGS = 2
S5_SCAN_UNROLL = 8
MOE_TM = 256
MOE_BM = 256
MOE_NCH = 12
MOE_STAGE = 4
MOE_CAST_ROWS = 128
NEG_BIG = -1e30


def _cparams(*sem):
    return pltpu.CompilerParams(dimension_semantics=sem, vmem_limit_bytes=V7X_VMEM_LIMIT_BYTES)


def _norm_mod(x, g, shift, scale):
    ms = jnp.mean(x * x, axis=-1, keepdims=True)
    return (x * lax.rsqrt(ms + NORM_EPS) * g) * (1.0 + scale) + shift


def _dot(a, b):
    return jnp.dot(a, b, preferred_element_type=F32)


def _ada_body(c_ref, w_ref, b_ref, o_ref):
    x = c_ref[...]
    s = (x * jax.nn.sigmoid(x)).astype(BF16)
    o_ref[...] = _dot(s, w_ref[...].astype(BF16)) + b_ref[...]


def ada_mod(c_all, ada_w, ada_b):
    depth, d, n = ada_w.shape
    tn = min(n, 1024)
    return pl.pallas_call(
        _ada_body,
        grid=(depth, n // tn),
        in_specs=[pl.BlockSpec((SUBLANES, d), lambda l, j: (0, 0)),
                  pl.BlockSpec((None, d, tn), lambda l, j: (l, 0, j)),
                  pl.BlockSpec((None, 1, tn), lambda l, j: (l, 0, j))],
        out_specs=pl.BlockSpec((None, SUBLANES, tn), lambda l, j: (l, 0, j)),
        out_shape=jax.ShapeDtypeStruct((depth, SUBLANES, n), F32),
        compiler_params=_cparams("parallel", "parallel"),
        name="ada_mod",
    )(c_all, ada_w, ada_b.reshape(depth, 1, n))


def _hy_in_body(xp_ref, xm_ref, xn_ref, g_ref, sh_ref, sc_ref,
                w0_ref, w1_ref, w2_ref, b0_ref, b1_ref, b2_ref,
                cw0_ref, cw1_ref, cw2_ref, cb0_ref, cb1_ref, cb2_ref,
                v_ref, x0_ref):
    i = pl.program_id(2)
    ni = pl.num_programs(2)
    tm = xm_ref.shape[0]
    x = jnp.concatenate([xp_ref[...], xm_ref[...], xn_ref[...]], axis=0)
    h = _norm_mod(x, g_ref[...], sh_ref[...], sc_ref[...]).astype(BF16)
    rows = lax.broadcasted_iota(jnp.int32, (tm + 2 * SUBLANES, 1), 0)
    valid = jnp.logical_and(jnp.logical_or(rows >= SUBLANES, i > 0),
                            jnp.logical_or(rows < tm + SUBLANES, i < ni - 1))

    def part(w_ref, b_ref, cw_ref, cb_ref):
        z = jnp.where(valid, _dot(h, w_ref[...]) + b_ref[...], 0.0)
        cw = cw_ref[...]
        zp = pltpu.roll(z, 1, axis=0)[SUBLANES:tm + SUBLANES]
        zn = pltpu.roll(z, tm + 2 * SUBLANES - 1, axis=0)[SUBLANES:tm + SUBLANES]
        return zp * cw[0:1] + z[SUBLANES:tm + SUBLANES] * cw[1:2] + zn * cw[2:3] + cb_ref[...]

    x0 = part(w0_ref, b0_ref, cw0_ref, cb0_ref)
    x1 = part(w1_ref, b1_ref, cw1_ref, cb1_ref)
    v = part(w2_ref, b2_ref, cw2_ref, cb2_ref) * x1
    v_ref[...] = v.astype(v_ref.dtype)
    x0_ref[...] = x0.astype(BF16)


def hyena_in(x, g, shift, scale, w_in, b_in, conv_w, conv_b, v_dtype):
    bsz, seq, d = x.shape
    tm = min(seq, 512)
    tn = min(d, 1024)
    nj = d // tn
    r8 = tm // SUBLANES
    last8 = seq // SUBLANES - 1
    row = lambda a: a.reshape(1, -1)
    wspec = lambda k: pl.BlockSpec((d, tn), lambda j, b, i: (0, k * nj + j))
    rspec = lambda k: pl.BlockSpec((1, tn), lambda j, b, i: (0, k * nj + j))
    cspec = lambda k: pl.BlockSpec((3, tn), lambda j, b, i: (0, k * nj + j))
    mspec = pl.BlockSpec((None, 1, d), lambda j, b, i: (b, 0, 0))
    out_spec = pl.BlockSpec((None, tm, tn), lambda j, b, i: (b, i, j))
    return pl.pallas_call(
        _hy_in_body,
        grid=(nj, bsz, seq // tm),
        in_specs=[pl.BlockSpec((None, SUBLANES, d), lambda j, b, i: (b, jnp.maximum(i * r8 - 1, 0), 0)),
                  pl.BlockSpec((None, tm, d), lambda j, b, i: (b, i, 0)),
                  pl.BlockSpec((None, SUBLANES, d), lambda j, b, i: (b, jnp.minimum((i + 1) * r8, last8), 0)),
                  pl.BlockSpec((1, d), lambda j, b, i: (0, 0)), mspec, mspec,
                  wspec(0), wspec(1), wspec(2), rspec(0), rspec(1), rspec(2),
                  cspec(0), cspec(1), cspec(2), rspec(0), rspec(1), rspec(2)],
        out_specs=[out_spec, out_spec],
        out_shape=[jax.ShapeDtypeStruct((bsz, seq, d), v_dtype), jax.ShapeDtypeStruct((bsz, seq, d), BF16)],
        compiler_params=_cparams("parallel", "parallel", "parallel"),
        name="hyena_in",
    )(x, x, x, row(g), shift, scale, w_in, w_in, w_in, row(b_in), row(b_in), row(b_in),
      conv_w, conv_w, conv_w, row(conv_b), row(conv_b), row(conv_b))


def _dft_tables(seq, blk):
    n = 2 * seq
    s = np.arange(seq, dtype=np.int64)[None, :]
    fl = np.arange(blk, dtype=np.int64)[:, None]
    fh = (np.arange(seq // blk, dtype=np.int64) * blk)[:, None]
    w = 2.0 * math.pi / n
    ang_b = ((fl * s) % n) * w
    ang_a = ((fh * s) % n) * w
    f32 = lambda m: jnp.asarray(m.astype(np.float32))
    return (f32(np.cos(ang_a)[:, None, :]), f32(np.sin(ang_a)[:, None, :]), f32(np.cos(ang_b)), f32(np.sin(ang_b)))


def _dft_gen_body(ca_ref, sa_ref, cb_ref, sb_ref, c_ref, s_ref):
    ca, sa, cb, sb = ca_ref[...], sa_ref[...], cb_ref[...], sb_ref[...]
    c_ref[...] = (ca * cb - sa * sb).astype(BF16)
    s_ref[...] = (sa * cb + ca * sb).astype(BF16)


def dft_matrices(seq):
    blk = min(seq, 256)
    ca, sa, cb, sb = _dft_tables(seq, blk)
    aspec = pl.BlockSpec((None, 1, seq), lambda i: (i, 0, 0))
    bspec = pl.BlockSpec((blk, seq), lambda i: (0, 0))
    ospec = pl.BlockSpec((blk, seq), lambda i: (i, 0))
    return pl.pallas_call(
        _dft_gen_body,
        grid=(seq // blk,),
        in_specs=[aspec, aspec, bspec, bspec],
        out_specs=[ospec, ospec],
        out_shape=[jax.ShapeDtypeStruct((seq, seq), BF16)] * 2,
        compiler_params=_cparams("parallel"),
        name="dft_matrices",
    )(ca, sa, cb, sb)


def _alt_sign(rows):
    return jnp.where((rows & 1) == 0, 1.0, -1.0).astype(F32)


def _filt_body(h2_ref, wf_ref, wb_ref, dl_ref, a_ref, d_ref, ny_ref):
    seq = h2_ref.shape[0]
    h2 = h2_ref[...]
    row = lax.broadcasted_iota(jnp.int32, (seq, 1), 0)
    t = row.astype(F32) * (1.0 / (seq - 1))
    win = jnp.exp(-t * dl_ref[...])
    hf = jnp.dot(h2, wf_ref[...], precision=HIGHEST, preferred_element_type=F32) * win
    hb = jnp.dot(h2, wb_ref[...], precision=HIGHEST, preferred_element_type=F32) * win
    hb = jnp.where(row == 0, 0.0, hb)
    nrm = (jnp.sum(jnp.abs(hf), axis=0, keepdims=True) + jnp.sum(jnp.abs(hb), axis=0, keepdims=True))
    inv = 1.0 / nrm
    a = (hf + hb) * inv
    a_ref[...] = a.astype(a_ref.dtype)
    d_ref[...] = ((hb - hf) * inv).astype(d_ref.dtype)
    ny = jnp.sum(a * _alt_sign(row), axis=0, keepdims=True) * (1.0 / (2 * seq))
    ny_ref[...] = jnp.broadcast_to(ny, ny_ref.shape)


def _khat_body(a_ref, d_ref, c_ref, s_ref, kr_ref, ki_ref):
    i = pl.program_id(1)
    tm = c_ref.shape[0]
    seq = c_ref.shape[1]
    f = i * tm + lax.broadcasted_iota(jnp.int32, (tm, 1), 0)
    w = jnp.where(f == 0, 1.0, 2.0).astype(F32) * (1.0 / (2 * seq))
    kr_ref[...] = _dot(c_ref[...], a_ref[...]) * w
    ki_ref[...] = _dot(s_ref[...], d_ref[...]) * w


def hyena_filter_taps(seq, fw1, fb1, fw2, fb2, fw3, freq, taps_dtype):
    d = fw3.shape[1] // 2
    bands_n = (fw1.shape[0] - 1) // 2
    t = np.linspace(0.0, 1.0, seq)[:, None]
    w = (2.0 * math.pi / seq) * np.arange(seq)[:, None]
    bands = np.linspace(1e-4, bands_n - 1, bands_n)[None, :]
    z = jnp.asarray(np.concatenate([t, np.cos(bands * w), -np.sin(bands * w)], axis=-1).astype(np.float32))
    h = jnp.sin(freq * (jnp.dot(z, fw1, precision=HIGHEST) + fb1))
    h2 = jnp.sin(freq * (jnp.dot(h, fw2, precision=HIGHEST) + fb2))
    max_decay = math.log(HY_DECAY_TARGET) / HY_FAST_PCT
    min_decay = math.log(HY_DECAY_TARGET) / HY_SLOW_PCT
    deltas = jnp.abs(jnp.linspace(min_decay, max_decay, d, dtype=F32))[None, :]

    order = h2.shape[1]
    tn = min(d, 256)
    nj = d // tn
    return pl.pallas_call(
        _filt_body,
        grid=(nj,),
        in_specs=[pl.BlockSpec((seq, order), lambda j: (0, 0)),
                  pl.BlockSpec((order, tn), lambda j: (0, j)),
                  pl.BlockSpec((order, tn), lambda j: (0, nj + j)),
                  pl.BlockSpec((1, tn), lambda j: (0, j))],
        out_specs=[pl.BlockSpec((seq, tn), lambda j: (0, j)),
                   pl.BlockSpec((seq, tn), lambda j: (0, j)),
                   pl.BlockSpec((SUBLANES, tn), lambda j: (0, j))],
        out_shape=[jax.ShapeDtypeStruct((seq, d), taps_dtype), jax.ShapeDtypeStruct((seq, d), taps_dtype),
                   jax.ShapeDtypeStruct((SUBLANES, d), F32)],
        compiler_params=_cparams("parallel"),
        name="hyena_filter_taps",
    )(h2, fw3, fw3, deltas)


def hyena_filter_dft(a, dd, cmat, smat):
    seq, d = a.shape
    tm = min(seq, 512)
    tn2 = min(d, 512)
    return pl.pallas_call(
        _khat_body,
        grid=(d // tn2, seq // tm),
        in_specs=[pl.BlockSpec((seq, tn2), lambda j, i: (0, j)),
                  pl.BlockSpec((seq, tn2), lambda j, i: (0, j)),
                  pl.BlockSpec((tm, seq), lambda j, i: (i, 0)),
                  pl.BlockSpec((tm, seq), lambda j, i: (i, 0))],
        out_specs=[pl.BlockSpec((tm, tn2), lambda j, i: (i, j))] * 2,
        out_shape=[jax.ShapeDtypeStruct((seq, d), F32)] * 2,
        compiler_params=_cparams("parallel", "parallel"),
        name="hyena_filter_dft",
    )(a, dd, cmat, smat)


def _dft_fwd_body(v_ref, c_ref, s_ref, kr_ref, ki_ref, kn_ref, ya_ref, yb_ref, yn_ref):
    i = pl.program_id(2)
    v = v_ref[...]
    vr = _dot(c_ref[...], v)
    p = _dot(s_ref[...], v)
    kr = kr_ref[...]
    ki = ki_ref[...]
    ya_ref[...] = (vr * kr + p * ki).astype(BF16)
    yb_ref[...] = (p * kr - vr * ki).astype(BF16)

    @pl.when(i == 0)
    def _():
        seq = v.shape[0]
        row = lax.broadcasted_iota(jnp.int32, (seq, 1), 0)
        vl = jnp.sum(v.astype(F32) * _alt_sign(row), axis=0, keepdims=True)
        yn_ref[...] = jnp.broadcast_to(vl * kn_ref[0:1, :], yn_ref.shape)


def _dft_inv_body(ya_ref, yb_ref, c_ref, s_ref, v_ref, x0_ref, skip_ref, yn_ref, o_ref):
    i = pl.program_id(2)
    tm = c_ref.shape[0]
    acc = _dot(c_ref[...], ya_ref[...]) + _dot(s_ref[...], yb_ref[...])
    t = i * tm + lax.broadcasted_iota(jnp.int32, (tm, 1), 0)
    y = acc + _alt_sign(t) * yn_ref[0:1, :] + skip_ref[...] * v_ref[...].astype(F32)
    o_ref[...] = (y * x0_ref[...].astype(F32)).astype(BF16)


def hyena_conv(v, x0, skip, kr, ki, kn, cmat, smat):
    bsz, seq, d = v.shape
    tm = min(seq, 512)
    tn = min(d, 512)
    grid = (bsz, d // tn, seq // tm)
    full = pl.BlockSpec((None, seq, tn), lambda b, j, i: (b, 0, j))
    mat = pl.BlockSpec((tm, seq), lambda b, j, i: (i, 0))
    tile = pl.BlockSpec((None, tm, tn), lambda b, j, i: (b, i, j))
    ktile = pl.BlockSpec((tm, tn), lambda b, j, i: (i, j))
    nyq = pl.BlockSpec((None, SUBLANES, tn), lambda b, j, i: (b, 0, j))
    ya, yb, yn = pl.pallas_call(
        _dft_fwd_body,
        grid=grid,
        in_specs=[full, mat, mat, ktile, ktile, pl.BlockSpec((SUBLANES, tn), lambda b, j, i: (0, j))],
        out_specs=[tile, tile, nyq],
        out_shape=[jax.ShapeDtypeStruct((bsz, seq, d), BF16)] * 2
        + [jax.ShapeDtypeStruct((bsz, SUBLANES, d), F32)],
        compiler_params=_cparams("parallel", "parallel", "arbitrary"),
        name="hyena_dft_fwd",
    )(v, cmat, smat, kr, ki, kn)
    return pl.pallas_call(
        _dft_inv_body,
        grid=grid,
        in_specs=[full, full, mat, mat, tile, tile, pl.BlockSpec((1, tn), lambda b, j, i: (0, j)), nyq],
        out_specs=tile,
        out_shape=jax.ShapeDtypeStruct((bsz, seq, d), BF16),
        compiler_params=_cparams("parallel", "parallel", "parallel"),
        name="hyena_dft_inv",
    )(ya, yb, cmat, smat, v, x0, skip.reshape(1, d), yn)


FFT_N2 = 128
FFT_MIN_SEQ = 1024
FFT_UNROLL = 8


def _fft_matrices(seq):
    n = 2 * seq
    n2 = FFT_N2
    n1 = n // n2
    r8 = SUBLANES
    q = np.arange(n2 // r8, dtype=np.int64)[:, None, None, None]
    f1 = np.arange(n1, dtype=np.int64)[None, :, None, None]
    r = np.arange(r8, dtype=np.int64)[None, None, :, None]
    t1 = np.arange(n1 // 2, dtype=np.int64)[None, None, None, :]
    ang = ((f1 * (t1 * n2 + q * r8 + r)) % n) * (2.0 * math.pi / n)
    g = np.stack([np.cos(ang), -np.sin(ang)], axis=3)
    eye = np.eye(r8)[None, None, :, None, None, :]
    ma = (g[..., None] * eye).reshape(n2 // r8, n1 * r8 * 2, (n1 // 2) * r8).astype(np.float32)
    f2 = np.arange(n2, dtype=np.int64)[:, None]
    t2 = np.arange(n2, dtype=np.int64)[None, :]
    th = ((f2 * t2) % n2) * (2.0 * math.pi / n2)
    co, si = np.cos(th), np.sin(th)
    wc = np.stack([np.stack([co, si], axis=-1), np.stack([-si, co], axis=-1)], axis=0)
    wc = wc.reshape(2 * n2, 2 * n2).astype(np.float32)
    as_bf16 = lambda m: jnp.asarray(np.ascontiguousarray(m).astype(BF16))
    return as_bf16(ma), as_bf16(np.swapaxes(ma, 1, 2)), as_bf16(wc), as_bf16(wc.T)


def _fft_stage_a(x_ref, ma_ref, s1):
    n1h, n_q, r8, tn = x_ref.shape
    n1 = s1.shape[0]

    def body(q, carry):
        x = x_ref[:, pl.ds(q, 1), :, :].reshape(n1h * r8, tn).astype(BF16)
        a = _dot(ma_ref[q], x).astype(BF16)
        s1[:, pl.ds(pl.multiple_of(q * 2 * r8, 2 * r8), 2 * r8), :] = a.reshape(n1, 2 * r8, tn)
        return carry

    lax.fori_loop(0, n_q, body, 0, unroll=FFT_UNROLL)


def _fft_conv_body(v_ref, x0_ref, k_ref, skip_ref, ma_ref, mat_ref, wc_ref, wci_ref, o_ref, s1, ysc):
    n1h, n_q, r8, tn = v_ref.shape
    n1 = s1.shape[0]
    n2 = s1.shape[1] // 2
    seq = n1h * n_q * r8
    _fft_stage_a(v_ref, ma_ref, s1)

    def slab(f, carry):
        y = _dot(wc_ref[...], s1[f])
        yr, yi = y[:n2], y[n2:]
        kr = k_ref[f, 0].astype(F32)
        ki = k_ref[f, 1].astype(F32)
        p = jnp.concatenate([yr * kr - yi * ki, yr * ki + yi * kr], axis=0).astype(BF16)
        s1[f] = _dot(wci_ref[...], p).astype(BF16)
        return carry

    lax.fori_loop(0, n1, slab, 0, unroll=2 * FFT_UNROLL)

    def inv_a(q, carry):
        z = s1[:, pl.ds(pl.multiple_of(q * 2 * r8, 2 * r8), 2 * r8), :].reshape(n1 * 2 * r8, tn)
        ysc[:, pl.ds(q, 1), :, :] = _dot(mat_ref[q], z).reshape(n1h, 1, r8, tn)
        return carry

    lax.fori_loop(0, n_q, inv_a, 0, unroll=FFT_UNROLL)
    y = ysc[...].reshape(seq, tn) + skip_ref[...] * v_ref[...].reshape(seq, tn)
    o_ref[...] = (y * x0_ref[...].astype(F32)).astype(BF16)


def _fft_filter_body(a_ref, d_ref, ma_ref, wc_ref, k_ref, s1):
    n1 = s1.shape[0]
    n2 = s1.shape[1] // 2
    scale = 1.0 / (n1 * n2)
    for src_ref, part, sign in ((a_ref, 0, scale), (d_ref, 1, -scale)):
        _fft_stage_a(src_ref, ma_ref, s1)

        def slab(f, carry):
            y = _dot(wc_ref[part * n2:(part + 1) * n2, :], s1[f])
            k_ref[f, part] = (y * sign).astype(BF16)
            return carry

        lax.fori_loop(0, n1, slab, 0, unroll=FFT_UNROLL)


def hyena_conv_fft(v, x0, skip, a, dd):
    bsz, seq, d = v.shape
    n2 = FFT_N2
    n1 = 2 * seq // n2
    n_q = n2 // SUBLANES
    tn = min(d, 256)
    ma, mat, wc, wci = _fft_matrices(seq)
    const = lambda shape: pl.BlockSpec(shape, lambda *_: (0,) * len(shape), pipeline_mode=pl.Buffered(1))
    view = lambda t: t.reshape(t.shape[:-2] + (n1 // 2, n_q, SUBLANES, d))
    tap = pl.BlockSpec((n1 // 2, n_q, SUBLANES, tn), lambda j: (0, 0, 0, j))
    khat = pl.pallas_call(
        _fft_filter_body,
        grid=(d // tn,),
        in_specs=[tap, tap, const(ma.shape), const(wc.shape)],
        out_specs=pl.BlockSpec((n1, 2, n2, tn), lambda j: (0, 0, 0, j)),
        out_shape=jax.ShapeDtypeStruct((n1, 2, n2, d), BF16),
        scratch_shapes=[pltpu.VMEM((n1, 2 * n2, tn), BF16)],
        compiler_params=_cparams("parallel"),
        name="hyena_filter_fft",
    )(view(a), view(dd), ma, wc)
    return pl.pallas_call(
        _fft_conv_body,
        grid=(d // tn, bsz),
        in_specs=[pl.BlockSpec((None, n1 // 2, n_q, SUBLANES, tn), lambda j, b: (b, 0, 0, 0, j)),
                  pl.BlockSpec((None, seq, tn), lambda j, b: (b, 0, j)),
                  pl.BlockSpec((n1, 2, n2, tn), lambda j, b: (0, 0, 0, j), pipeline_mode=pl.Buffered(1)),
                  pl.BlockSpec((1, tn), lambda j, b: (0, j)),
                  const(ma.shape), const(mat.shape), const(wc.shape), const(wci.shape)],
        out_specs=pl.BlockSpec((None, seq, tn), lambda j, b: (b, 0, j)),
        out_shape=jax.ShapeDtypeStruct((bsz, seq, d), BF16),
        scratch_shapes=[pltpu.VMEM((n1, 2 * n2, tn), BF16), pltpu.VMEM((n1 // 2, n_q, SUBLANES, tn), F32)],
        compiler_params=_cparams("parallel", "arbitrary"),
        name="hyena_conv_fft",
    )(view(v), x0, khat, skip.reshape(1, d), ma, mat, wc, wci)


def _mm_res_body(x_ref, w_ref, b_ref, res_ref, gate_ref, o_ref):
    o_ref[...] = res_ref[...] + gate_ref[...] * (_dot(x_ref[...], w_ref[...]) + b_ref[...])


def mm_residual(x, w, b, res, gate):
    bsz, seq, k = x.shape
    n = w.shape[1]
    tm = min(seq, 512)
    return pl.pallas_call(
        _mm_res_body,
        grid=(bsz, seq // tm),
        in_specs=[pl.BlockSpec((None, tm, k), lambda b, i: (b, i, 0)),
                  pl.BlockSpec((k, n), lambda b, i: (0, 0)),
                  pl.BlockSpec((1, n), lambda b, i: (0, 0)),
                  pl.BlockSpec((None, tm, n), lambda b, i: (b, i, 0)),
                  pl.BlockSpec((None, 1, n), lambda b, i: (b, 0, 0))],
        out_specs=pl.BlockSpec((None, tm, n), lambda b, i: (b, i, 0)),
        out_shape=jax.ShapeDtypeStruct((bsz, seq, n), F32),
        compiler_params=_cparams("parallel", "parallel"),
        name="mm_residual",
    )(x, w, b.reshape(1, n), res, gate)


def _moe_pre_body(*refs, n_groups, n_experts, tile_offs):
    n_streams = len(tile_offs) - 1
    g_ref, wr_ref, br_ref, tok_ref, eid_ref, gate_ref = refs[3 * n_streams:]
    i = pl.program_id(0)
    for k in range(n_streams):
        x_ref, sh_ref, sc_ref = refs[3 * k:3 * k + 3]

        @pl.when(jnp.logical_and(i >= tile_offs[k], i < tile_offs[k + 1]))
        def _():
            tok = _norm_mod(x_ref[...], g_ref[...], sh_ref[...], sc_ref[...])
            _route_tokens(tok, wr_ref, br_ref, tok_ref, eid_ref, gate_ref, n_groups, n_experts)


def _route_tokens(tok, wr_ref, br_ref, tok_ref, eid_ref, gate_ref, n_groups, n_experts):
    tok_ref[...] = tok
    t_hi = tok.astype(BF16)
    t_lo = (tok - t_hi.astype(F32)).astype(BF16)
    logits = (_dot(t_hi, wr_ref[0]) + _dot(t_hi, wr_ref[1]) + _dot(t_lo, wr_ref[0])) + br_ref[...]
    lane = lax.broadcasted_iota(jnp.int32, logits.shape, 1)
    per = n_experts // n_groups
    big = jnp.int32(1 << 20)
    gmask = jnp.logical_and(lane >= n_experts, lane < n_experts + n_groups)
    gl = jnp.where(gmask, logits, NEG_BIG)
    gmax = jnp.max(gl, axis=-1, keepdims=True)
    gidx = jnp.min(jnp.where(gl == gmax, lane - n_experts, big), axis=-1, keepdims=True)
    p_top = 1.0 / jnp.sum(jnp.where(gmask, jnp.exp(gl - gmax), 0.0), axis=-1, keepdims=True)
    lo = gidx * per
    emask = jnp.logical_and(lane >= lo, lane < lo + per)
    el = jnp.where(emask, logits, NEG_BIG)
    m1 = jnp.max(el, axis=-1, keepdims=True)
    i1 = jnp.min(jnp.where(el == m1, lane, big), axis=-1, keepdims=True)
    el2 = jnp.where(lane == i1, NEG_BIG, el)
    m2 = jnp.max(el2, axis=-1, keepdims=True)
    i2 = jnp.min(jnp.where(el2 == m2, lane, big), axis=-1, keepdims=True)
    e21 = jnp.exp(m2 - m1)
    g1 = p_top / (1.0 + e21)
    g2 = g1 * e21
    ids = jnp.where(lane == 0, i1, jnp.where(lane == 1, i2, -1))
    eid_ref[...] = ids.T[0:SUBLANES, :]
    gate_ref[...] = jnp.where(lane == 0, g1, jnp.where(lane == 1, g2, 0.0))


def moe_pre(streams, g, wr, br, n_groups, n_experts):
    d = streams[0][0].shape[2]
    tm = MOE_TM
    tile_offs = [0]
    in_specs, args = [], []
    for x, shift, scale in streams:
        bsz, seq, _ = x.shape
        nt = seq // tm
        n_tiles = bsz * nt
        off = tile_offs[-1]
        tile_offs.append(off + n_tiles)

        def tile(i, off=off, n_tiles=n_tiles):
            return jnp.clip(i - off, 0, n_tiles - 1)

        in_specs += [pl.BlockSpec((None, tm, d), lambda i, tile=tile, nt=nt: (tile(i) // nt, tile(i) % nt, 0)),
                     pl.BlockSpec((None, 1, d), lambda i, tile=tile, nt=nt: (tile(i) // nt, 0, 0)),
                     pl.BlockSpec((None, 1, d), lambda i, tile=tile, nt=nt: (tile(i) // nt, 0, 0))]
        args += [x, shift, scale]
    in_specs += [pl.BlockSpec((1, d), lambda i: (0, 0)),
                 pl.BlockSpec((2, d, LANES), lambda i: (0, 0, 0)),
                 pl.BlockSpec((1, LANES), lambda i: (0, 0))]
    wr_hi = wr.astype(BF16)
    wr_split = jnp.stack([wr_hi, (wr - wr_hi.astype(F32)).astype(BF16)])
    args += [g.reshape(1, d), wr_split, br]
    total = tile_offs[-1] * tm
    rout = pl.BlockSpec((tm, LANES), lambda i: (i, 0))
    tok, eid, gate = pl.pallas_call(
        functools.partial(_moe_pre_body, n_groups=n_groups, n_experts=n_experts, tile_offs=tuple(tile_offs)),
        grid=(tile_offs[-1],),
        in_specs=in_specs,
        out_specs=[pl.BlockSpec((tm, d), lambda i: (i, 0)),
                   pl.BlockSpec((None, SUBLANES, tm), lambda i: (i, 0, 0)), rout],
        out_shape=[jax.ShapeDtypeStruct((total, d), F32),
                   jax.ShapeDtypeStruct((tile_offs[-1], SUBLANES, tm), jnp.int32),
                   jax.ShapeDtypeStruct((total, LANES), F32)],
        compiler_params=_cparams("parallel"),
        name="moe_pre",
    )(*args)
    return tok, eid, gate, tile_offs[:-1]


def _start_row_gather(row_index, n_rows, src_hbm, dst_vmem, sem):
    def body(g, c):
        r0 = pl.multiple_of(g * SUBLANES, SUBLANES)
        dst_tile = dst_vmem.at[pl.ds(r0, SUBLANES)]
        for k in range(SUBLANES):
            pltpu.make_async_copy(src_hbm.at[pl.ds(row_index(r0 + k), 1)], dst_tile.at[pl.ds(k, 1)], sem).start()
        return c

    lax.fori_loop(0, n_rows // SUBLANES, body, 0, unroll=2)


def _wait_row_gather(n_rows, src_hbm, dst_vmem, sem):
    pltpu.make_async_copy(src_hbm.at[pl.ds(0, n_rows)], dst_vmem, sem).wait()


def _expert_body(bv_ref, rk_ref, pe_ref, tot_ref, src_ref, nxt_ref, tok_ref, wg_hbm, wu_hbm, wd_hbm, o_ref,
                 xbuf, xsem, wcache, stg, wsem, cnt, *, layer):
    i = pl.program_id(0)
    n = pl.num_programs(0)
    slot = i % 2
    cr, cc = stg.shape[1:]
    total = tot_ref[0]
    mats_hbm = (wg_hbm, wu_hbm, wd_hbm)

    @pl.when(i == 0)
    def _():
        cnt[0] = 0
        cnt[1] = 0

    @pl.when(jnp.logical_and(i == 0, bv_ref[0] > 0))
    def _():
        _start_row_gather(lambda r: src_ref[r], MOE_BM, tok_ref, xbuf.at[0], xsem.at[0])

    @pl.when(jnp.logical_and(i + 1 < n, bv_ref[jnp.minimum(i + 1, n - 1)] > 0))
    def _():
        _start_row_gather(lambda r: nxt_ref[r], MOE_BM, tok_ref, xbuf.at[1 - slot], xsem.at[1 - slot])

    def chunk_geom(c):
        q = c % MOE_NCH
        m = q // 4
        sub = q % 4
        r0 = jnp.where(m < 2, sub, sub // 2) * cr
        c0 = jnp.where(m < 2, 0, sub % 2) * cc
        return m, pl.multiple_of(r0, cr), pl.multiple_of(c0, cc)

    def issue(c):
        e = pe_ref[c // MOE_NCH]
        m, r0, c0 = chunk_geom(c)
        s = c % MOE_STAGE
        for k, w_hbm in enumerate(mats_hbm):
            @pl.when(m == k)
            def _():
                pltpu.make_async_copy(w_hbm.at[layer, e, pl.ds(r0, cr), pl.ds(c0, cc)], stg.at[s],
                                      wsem.at[s]).start()

    def cast(c):
        s = c % MOE_STAGE
        pltpu.make_async_copy(wg_hbm.at[layer, 0, pl.ds(0, cr), pl.ds(0, cc)], stg.at[s], wsem.at[s]).wait()
        ws = (c // MOE_NCH) % 2
        q = c % MOE_NCH
        step = min(MOE_CAST_ROWS, cr)
        assert cr % step == 0

        def slab(k, carry):
            rows = pl.ds(pl.multiple_of(k * step, step), step)
            wcache[ws, q, rows, :] = stg[s, rows, :].astype(BF16)
            return carry

        lax.fori_loop(0, cr // step, slab, 0)

    valid = bv_ref[i] > 0
    rank = rk_ref[i]
    issued = cnt[0]
    done = cnt[1]
    limit = jnp.minimum(total, MOE_NCH * (rank + 2))
    need = jnp.where(valid, MOE_NCH * (rank + 1), done)

    def fill(issued, done):
        hi = jnp.minimum(limit, done + MOE_STAGE)

        def body(c, carry):
            issue(c)
            return carry

        lax.fori_loop(issued, hi, body, 0)
        return jnp.maximum(issued, hi)

    def cast_and_refill(c, issued):
        cast(c)
        more = issued < jnp.minimum(limit, c + 1 + MOE_STAGE)

        @pl.when(more)
        def _():
            issue(issued)

        return issued + more.astype(jnp.int32)

    issued = fill(issued, done)
    issued = lax.fori_loop(done, need, cast_and_refill, issued)
    done = jnp.maximum(done, need)

    @pl.when(valid)
    def _():
        ws = rank % 2
        _wait_row_gather(MOE_BM, tok_ref, xbuf.at[slot], xsem.at[slot])
        x = xbuf[slot].astype(BF16)
        gate = sum(_dot(x[:, k * cr:(k + 1) * cr], wcache[ws, k]) for k in range(4))
        up = sum(_dot(x[:, k * cr:(k + 1) * cr], wcache[ws, 4 + k]) for k in range(4))
        h = (gate * jax.nn.sigmoid(gate) * up).astype(BF16)
        for half in range(2):
            o_ref[:, half * cc:(half + 1) * cc] = sum(
                _dot(h[:, k * cr:(k + 1) * cr], wcache[ws, 8 + 2 * k + half]) for k in range(2))

    @pl.when(jnp.logical_not(valid))
    def _():
        o_ref[...] = jnp.zeros_like(o_ref)

    fetched = issued
    issued = lax.fori_loop(done, fetched, cast_and_refill, issued)
    done = jnp.maximum(done, fetched)
    last = i == n - 1
    tail = jnp.where(last, issued, done)

    def drain(c, carry):
        cast(c)
        return carry

    lax.fori_loop(done, tail, drain, 0)
    cnt[0] = issued
    cnt[1] = jnp.maximum(done, tail)


def moe_experts(tok, src_tok, block_valid, block_rank, present, n_chunks, w_gate, w_up, w_down, layer):
    d = tok.shape[1]
    n_rows = src_tok.shape[0]
    n_blocks = n_rows // MOE_BM
    dh = w_gate.shape[3]
    assert 2 * dh == d and MOE_NCH == 12
    cr, cc = d // 4, dh
    any_spec = pl.BlockSpec(memory_space=pl.ANY)
    grid_spec = pltpu.PrefetchScalarGridSpec(
        num_scalar_prefetch=4,
        grid=(n_blocks,),
        in_specs=[pl.BlockSpec((MOE_BM,), lambda i, *_: (i,), memory_space=pltpu.SMEM),
                  pl.BlockSpec((MOE_BM,), lambda i, *_: (jnp.minimum(i + 1, n_blocks - 1),),
                               memory_space=pltpu.SMEM),
                  any_spec, any_spec, any_spec, any_spec],
        out_specs=pl.BlockSpec((MOE_BM, d), lambda i, *_: (i, 0)),
        scratch_shapes=[pltpu.VMEM((2, MOE_BM, d), F32), pltpu.SemaphoreType.DMA((2,)),
                        pltpu.VMEM((2, MOE_NCH, cr, cc), BF16),
                        pltpu.VMEM((MOE_STAGE, cr, cc), F32), pltpu.SemaphoreType.DMA((MOE_STAGE,)),
                        pltpu.SMEM((2,), jnp.int32)],
    )
    return pl.pallas_call(
        functools.partial(_expert_body, layer=layer),
        grid_spec=grid_spec,
        out_shape=jax.ShapeDtypeStruct((n_rows, d), F32),
        compiler_params=_cparams("arbitrary"),
        name="moe_experts",
    )(block_valid, block_rank, present, n_chunks, src_tok, src_tok, tok, w_gate, w_up, w_down)


def _combine_body(dest_ref, nxt_ref, os_ref, gate_ref, res_ref, gt_ref, fg_ref, *rest, final_norm, with_next):
    if with_next:
        ng_ref, nsh_ref, nsc_ref, o_ref, h_ref, buf, sem = rest
    else:
        o_ref, buf, sem = rest
    rows = res_ref.shape[0]
    i = pl.program_id(0)
    n = pl.num_programs(0)
    slot = i % 2

    def start(idx_ref, s):
        for k in range(TOP_K):
            _start_row_gather(lambda r, k=k: idx_ref[k * rows + r], rows, os_ref, buf.at[s, k], sem.at[s])

    @pl.when(i == 0)
    def _():
        start(dest_ref, 0)

    @pl.when(i + 1 < n)
    def _():
        start(nxt_ref, 1 - slot)

    for k in range(TOP_K):
        _wait_row_gather(rows, os_ref, buf.at[slot, k], sem.at[slot])
    gates = gate_ref[...]
    mo = gates[:, 0:1] * buf[slot, 0] + gates[:, 1:2] * buf[slot, 1]
    y = res_ref[...] + gt_ref[...] * mo
    if final_norm:
        ms = jnp.mean(y * y, axis=-1, keepdims=True)
        y = y * lax.rsqrt(ms + NORM_EPS) * fg_ref[...]
    o_ref[...] = y
    if with_next:
        h_ref[...] = _norm_mod(y, ng_ref[...], nsh_ref[...], nsc_ref[...]).astype(h_ref.dtype)


def moe_combine(os, dest, gates, tile0, res, gt, final_g, final_norm, next_mod=None):
    bsz, seq, d = res.shape
    rows = MOE_TM
    nt = seq // rows
    n = bsz * nt
    tile = pl.BlockSpec((None, rows, d), lambda i: (i // nt, i % nt, 0))
    mspec = pl.BlockSpec((None, 1, d), lambda i: (i // nt, 0, 0))
    rspec = pl.BlockSpec((1, d), lambda i: (0, 0))
    dspec = lambda step: pl.BlockSpec((SUBLANES * rows,), lambda i: (tile0 + step(i),), memory_space=pltpu.SMEM)
    in_specs = [dspec(lambda i: i), dspec(lambda i: jnp.minimum(i + 1, n - 1)),
                pl.BlockSpec(memory_space=pl.ANY),
                pl.BlockSpec((rows, LANES), lambda i: (tile0 + i, 0)),
                tile, mspec, rspec]
    args = [dest, dest, os, gates, res, gt, final_g.reshape(1, d)]
    out_specs, out_shape = [tile], [jax.ShapeDtypeStruct((bsz, seq, d), F32)]
    if next_mod is not None:
        in_specs += [rspec, mspec, mspec]
        args += [next_mod[0].reshape(1, d), next_mod[1], next_mod[2]]
        out_specs.append(tile)
        out_shape.append(jax.ShapeDtypeStruct((bsz, seq, d), BF16))
    outs = pl.pallas_call(
        functools.partial(_combine_body, final_norm=final_norm, with_next=next_mod is not None),
        grid=(n,),
        in_specs=in_specs,
        out_specs=out_specs,
        out_shape=out_shape,
        scratch_shapes=[pltpu.VMEM((2, TOP_K, rows, d), F32), pltpu.SemaphoreType.DMA((2,))],
        compiler_params=_cparams("arbitrary"),
        name="moe_combine",
    )(*args)
    return outs if next_mod is not None else outs[0]


def _plan_body(e_ref, dest_ref, tab_ref, *, n_experts):
    n_rows, w = e_ref.shape
    e_all = e_ref[...]
    li = lax.broadcasted_iota(jnp.int32, (w, w), 0)
    lj = lax.broadcasted_iota(jnp.int32, (w, w), 1)
    incl = (li <= lj).astype(BF16)
    ri = lax.broadcasted_iota(jnp.int32, (n_rows, n_rows), 0)
    rj = lax.broadcasted_iota(jnp.int32, (n_rows, n_rows), 1)
    before = (rj < ri).astype(BF16)
    elane = lax.broadcasted_iota(jnp.int32, (n_rows, LANES), 1)
    row_tot = jnp.zeros((n_rows, LANES), F32)
    for e in range(n_experts):
        tot = jnp.sum((e_all == e).astype(F32), axis=1, keepdims=True)
        row_tot = row_tot + jnp.where(elane == e, tot, 0.0)
    rows_before = _dot(before, row_tot.astype(BF16))
    counts = jnp.sum(row_tot, axis=0, keepdims=True).astype(jnp.int32)
    lane1 = lax.broadcasted_iota(jnp.int32, (1, LANES), 1)

    def excl_prefix(v):
        acc = v
        sh = 1
        while sh < LANES:
            acc = acc + jnp.where(lane1 >= sh, pltpu.roll(acc, sh, axis=1), 0)
            sh *= 2
        return acc - v

    start = excl_prefix(counts)
    padded = (counts + (MOE_BM - 1)) // MOE_BM * MOE_BM
    pad_start = excl_prefix(padded)
    tab = jnp.concatenate([counts, start, pad_start, pad_start + padded,
                           jnp.zeros((SUBLANES - 4, LANES), jnp.int32)], axis=0)
    tab_ref[...] = tab
    base = rows_before + pad_start.astype(F32)
    dest = jnp.zeros((n_rows, w), F32)
    for e in range(n_experts):
        hit = e_all == e
        within = _dot(hit.astype(BF16), incl)
        dest = dest + jnp.where(hit, within - 1.0 + base[:, e:e + 1], 0.0)
    dest_ref[...] = dest.astype(jnp.int32)


def _route_plan(eid, n_experts):
    n_tiles, r8, tm = eid.shape
    dest, tab = pl.pallas_call(
        functools.partial(_plan_body, n_experts=n_experts),
        out_shape=[jax.ShapeDtypeStruct((n_tiles * r8, tm), jnp.int32),
                   jax.ShapeDtypeStruct((SUBLANES, LANES), jnp.int32)],
        compiler_params=pltpu.CompilerParams(vmem_limit_bytes=V7X_VMEM_LIMIT_BYTES),
        name="moe_route_plan",
    )(eid.reshape(n_tiles * r8, tm))
    counts, start, pad_start, pad_end = (tab[k, :n_experts] for k in range(4))
    e_flat = eid[:, :TOP_K, :].reshape(-1)
    a = e_flat.shape[0]
    order = jnp.argsort(e_flat)
    tok_of = (order // (TOP_K * tm)) * tm + order % tm
    n_blocks = -(-a // MOE_BM) + n_experts
    starts = jnp.arange(n_blocks, dtype=jnp.int32) * MOE_BM
    block_expert = jnp.minimum(jnp.sum((pad_end[None, :] <= starts[:, None]).astype(jnp.int32), axis=1),
                               n_experts - 1)
    block_valid = (starts < pad_end[-1]).astype(jnp.int32)
    lane = jnp.arange(MOE_BM, dtype=jnp.int32)[None, :]
    blk_off = (starts - pad_start[block_expert])[:, None] + lane
    live = jnp.logical_and(blk_off < counts[block_expert][:, None], block_valid[:, None] > 0)
    sorted_pos = jnp.clip(start[block_expert][:, None] + blk_off, 0, a - 1)
    src_tok = jnp.where(live, tok_of[sorted_pos], 0).reshape(-1)
    has = (counts > 0).astype(jnp.int32)
    block_rank = (jnp.cumsum(has) - 1)[block_expert]
    present = jnp.argsort(1 - has)
    n_chunks = (MOE_NCH * jnp.sum(has)).reshape(1)
    i32 = lambda v: v.astype(jnp.int32)
    return dest.reshape(-1), i32(src_tok), block_valid, i32(block_rank), i32(present), i32(n_chunks)


def hier_moe(streams, norm_g, wg, bg, we, be, w_gate, w_up, w_down, layer, final_g, final_norm, next_mods=None):
    n_groups = wg.shape[1]
    n_experts = we.shape[1]
    d = wg.shape[0]
    wr = jnp.zeros((d, LANES), F32).at[:, :n_experts].set(we).at[:, n_experts:n_experts + n_groups].set(wg)
    br = jnp.zeros((1, LANES), F32).at[0, :n_experts].set(be).at[0, n_experts:n_experts + n_groups].set(bg)
    tok, eid, gates, tile0s = moe_pre([s[:3] for s in streams], norm_g, wr, br, n_groups, n_experts)
    dest, src_tok, block_valid, block_rank, present, n_chunks = _route_plan(eid, n_experts)
    os = moe_experts(tok, src_tok, block_valid, block_rank, present, n_chunks, w_gate, w_up, w_down, layer)
    next_mods = next_mods or [None] * len(streams)
    return [moe_combine(os, dest, gates, tile0, x, gt, final_g, final_norm, nm)
            for (x, _, _, gt), tile0, nm in zip(streams, tile0s, next_mods)]


def _norm_mod_body(x_ref, g_ref, sh_ref, sc_ref, o_ref):
    o_ref[...] = _norm_mod(x_ref[...], g_ref[...], sh_ref[...], sc_ref[...]).astype(o_ref.dtype)


def norm_mod(x, g, shift, scale):
    bsz, seq, d = x.shape
    tm = min(seq, 512)
    mspec = pl.BlockSpec((None, 1, d), lambda b, i: (b, 0, 0))
    return pl.pallas_call(
        _norm_mod_body,
        grid=(bsz, seq // tm),
        in_specs=[pl.BlockSpec((None, tm, d), lambda b, i: (b, i, 0)),
                  pl.BlockSpec((1, d), lambda b, i: (0, 0)), mspec, mspec],
        out_specs=pl.BlockSpec((None, tm, d), lambda b, i: (b, i, 0)),
        out_shape=jax.ShapeDtypeStruct((bsz, seq, d), BF16),
        compiler_params=_cparams("parallel", "parallel"),
        name="norm_mod",
    )(x, g.reshape(1, d), shift, scale)


def _s5_arrange(h):
    bsz, t, d = h.shape
    c = t // (S5_SEGS * S5_TAU)
    h = h.reshape(bsz, S5_SEGS, c, S5_TAU, d // LANES, LANES)
    return h.transpose(2, 0, 1, 4, 3, 5).reshape(c * bsz * S5_SEGS, d * S5_TAU)


def _s5_unarrange(y, bsz):
    r, w = y.shape
    d = w // S5_TAU
    c = r // (bsz * S5_SEGS)
    y = y.reshape(c, bsz, S5_SEGS, d // LANES, S5_TAU, LANES)
    return y.transpose(1, 2, 0, 4, 3, 5).reshape(bsz, S5_SEGS * c * S5_TAU, d)


def _s5_operators(a_re, a_im, log_step, b_re, b_im, c_re, c_im):
    n_g, n_p = a_re.shape[1:]
    n_h = b_re.shape[-1]
    gpt = LANES // n_h
    n_j = n_g // gpt
    tau = S5_TAU
    assert tau * n_h == LANES and 2 * n_p == LANES
    lam_step = lax.complex(a_re, a_im) * jnp.exp(log_step)[..., None]
    lam_bar = jnp.exp(lam_step)
    b_bar = ((lam_bar - 1.0) / lax.complex(a_re, a_im))[..., None] * lax.complex(b_re, b_im)
    c_mat = lax.complex(c_re, c_im)
    ks = jnp.arange(tau + 1, dtype=F32)[None, :, None, None]
    pw = jnp.exp(lam_step[:, None] * ks)
    ein = functools.partial(jnp.einsum, precision=HIGHEST)
    inj_c, cl_c, lt = [], [], []
    tz_c = 0.0
    for d in range(2):
        pos = jnp.arange(tau) if d == 0 else jnp.arange(tau)[::-1]
        inj = (pw[d][tau - 1 - pos][..., None] * b_bar[d][None]).reshape(tau, n_j, gpt, n_p, n_h)
        inj = inj.transpose(1, 0, 2, 4, 3).reshape(n_j, tau * LANES, n_p)
        inj_c.append(jnp.concatenate([inj.real, inj.imag], axis=-1))
        cl = (c_mat[d][None] * pw[d][pos + 1][:, :, None, :]).reshape(tau, n_j, gpt, n_h, n_p)
        cl = cl.transpose(1, 2, 4, 0, 3).reshape(n_j, gpt * n_p, tau * n_h)
        cl_c.append(jnp.concatenate([cl.real, -cl.imag], axis=1))
        mk = ein('gop,kgp,gph->kgoh', c_mat[d], pw[d][:tau], b_bar[d]).real
        diff = pos[:, None] - pos[None, :]
        tz = jnp.where((diff >= 0)[:, :, None, None, None], mk[jnp.clip(diff, 0, tau - 1)], 0.0)
        tz = tz.reshape(tau, tau, n_j, gpt, n_h, n_h)
        tz_c = tz_c + tz.transpose(2, 1, 3, 5, 0, 4).reshape(n_j, tau * LANES, tau * n_h)
        lt_d = pw[d][tau].reshape(n_j, 1, gpt * n_p)
        lt.append(jnp.concatenate([lt_d.real, lt_d.imag], axis=-1))
    ws, wu, wh = s5_expand(jnp.stack(inj_c).astype(BF16), jnp.stack(cl_c).astype(BF16), tz_c.astype(BF16),
                           n_h, n_p)
    return ws, wu, wh, jnp.stack(lt).astype(F32)


def _s5_expand_body(inj_ref, cl_ref, tz_ref, ws_ref, wu_ref, wh_ref, *, n_h, n_p):
    rows = tz_ref.shape[0]
    gpt = LANES // n_h
    row = lax.broadcasted_iota(jnp.int32, (rows, LANES), 0)
    lane = lax.broadcasted_iota(jnp.int32, (rows, LANES), 1)
    sel_r = lax.broadcasted_iota(jnp.int32, (LANES, LANES), 0)
    sel_l = lax.broadcasted_iota(jnp.int32, (LANES, LANES), 1)
    grp_in = (row // n_h) % gpt
    grp_st = (row // n_p) % gpt

    def spread_out(m, t, grp_row):
        sel = jnp.logical_and(sel_r // n_h == t, sel_r % n_h == sel_l % n_h).astype(BF16)
        return jnp.where(grp_row == lane // n_h, _dot(m, sel), 0.0).astype(BF16)

    def spread_state(m, c, q, grp_row):
        sel = jnp.logical_and(sel_r // n_p == c, sel_r % n_p == sel_l % n_p).astype(BF16)
        return jnp.where(grp_row == (LANES // n_p) * q + lane // n_p, _dot(m, sel), 0.0).astype(BF16)

    w2 = 2 * gpt * n_p
    tz = tz_ref[...]
    for t in range(S5_TAU):
        wu_ref[:, t * LANES:(t + 1) * LANES] = spread_out(tz, t, grp_in)
    for d in range(2):
        cl = cl_ref[d]
        inj = inj_ref[d]
        for t in range(S5_TAU):
            wh_ref[d, :, t * LANES:(t + 1) * LANES] = spread_out(cl, t, grp_st)
        for c in range(2):
            for q in range(gpt * n_p // LANES):
                lo = d * w2 + c * gpt * n_p + q * LANES
                ws_ref[:, lo:lo + LANES] = spread_state(inj, c, q, grp_in)


def s5_expand(inj_c, cl_c, tz_c, n_h, n_p):
    n_j, rows, _ = tz_c.shape
    gpt = LANES // n_h
    w2 = 2 * gpt * n_p
    assert rows == S5_TAU * LANES == w2
    cspec = pl.BlockSpec((2, None, rows, LANES), lambda j: (0, j, 0, 0))
    return pl.pallas_call(
        functools.partial(_s5_expand_body, n_h=n_h, n_p=n_p),
        grid=(n_j,),
        in_specs=[cspec, cspec, pl.BlockSpec((None, rows, LANES), lambda j: (j, 0, 0))],
        out_specs=[pl.BlockSpec((None, rows, 2 * w2), lambda j: (j, 0, 0)),
                   pl.BlockSpec((None, rows, rows), lambda j: (j, 0, 0)),
                   pl.BlockSpec((2, None, w2, rows), lambda j: (0, j, 0, 0))],
        out_shape=[jax.ShapeDtypeStruct((n_j, rows, 2 * w2), BF16),
                   jax.ShapeDtypeStruct((n_j, rows, rows), BF16),
                   jax.ShapeDtypeStruct((2, n_j, w2, rows), BF16)],
        compiler_params=_cparams("parallel"),
        name="s5_expand",
    )(inj_c, cl_c, tz_c)


def _s5_inj_body(x_ref, w_ref, o_ref):
    o_ref[...] = _dot(x_ref[...], w_ref[...])


def s5_inject(xr, ws):
    r = xr.shape[0]
    n_j, k, n = ws.shape
    tm = r // 2 if r % 32 == 0 else r
    return pl.pallas_call(
        _s5_inj_body,
        grid=(n_j, r // tm),
        in_specs=[pl.BlockSpec((tm, k), lambda j, i: (i, j)),
                  pl.BlockSpec((None, k, n), lambda j, i: (j, 0, 0))],
        out_specs=pl.BlockSpec((tm, n), lambda j, i: (i, j)),
        out_shape=jax.ShapeDtypeStruct((r, n_j * n), F32),
        compiler_params=_cparams("parallel", "parallel"),
        name="s5_inject",
    )(xr, ws)


def _cmul(ar, ai, br, bi):
    return ar * br - ai * bi, ar * bi + ai * br


def _s5_scan_body(sc_ref, sl_ref, lt_ref, h_ref, raw_ref, *, n_ctx, n_lat, bsz):
    d = pl.program_id(1)
    w2 = lt_ref.shape[-1]
    w = w2 // 2
    rows = bsz * S5_SEGS
    seg = lax.broadcasted_iota(jnp.int32, (rows, 1), 0) % S5_SEGS
    is_late = seg != d
    lam_r = lt_ref[:, 0:w]
    lam_i = lt_ref[:, w:w2]
    zero = jnp.zeros((rows, w), F32)
    one = (jnp.ones((1, w), F32), jnp.zeros((1, w), F32))

    def swap_segments(x):
        return jnp.where(seg == 0, pltpu.roll(x, rows - 1, axis=0), pltpu.roll(x, 1, axis=0))

    def phase(s_ref, n_steps, hin_r, hin_i, write):
        def chunk(k):
            return jnp.where(d == 0, k, n_steps - 1 - k)

        def step_raw(k, carry):
            hr, hi = carry
            c = chunk(k)
            raw_ref[c, :, 0:w] = hr
            raw_ref[c, :, w:w2] = hi
            nr, ni = _cmul(lam_r, lam_i, hr, hi)
            return nr + s_ref[c, :, 0:w], ni + s_ref[c, :, w:w2]

        er, ei = lax.fori_loop(0, n_steps, step_raw, (zero, zero), unroll=S5_SCAN_UNROLL)
        pr, pi = lax.fori_loop(0, n_steps, lambda k, q: _cmul(lam_r, lam_i, *q), one)
        dr, di = _cmul(pr, pi, hin_r, hin_i)
        first_r = jnp.where(is_late, 0.0, er + dr)
        first_i = jnp.where(is_late, 0.0, ei + di)
        carry_r = jnp.where(is_late, swap_segments(first_r), hin_r)
        carry_i = jnp.where(is_late, swap_segments(first_i), hin_i)
        if write:
            def step_fix(k, q):
                c = chunk(k)
                fr, fi = _cmul(q[0], q[1], carry_r, carry_i)
                h_ref[c, :, 0:w] = (raw_ref[c, :, 0:w] + fr).astype(h_ref.dtype)
                h_ref[c, :, w:w2] = (raw_ref[c, :, w:w2] + fi).astype(h_ref.dtype)
                return _cmul(lam_r, lam_i, q[0], q[1])

            lax.fori_loop(0, n_steps, step_fix, one, unroll=S5_SCAN_UNROLL)
        lr, li = _cmul(pr, pi, carry_r, carry_i)
        last_r = jnp.where(is_late, er + lr, 0.0)
        last_i = jnp.where(is_late, ei + li, 0.0)
        return (jnp.where(is_late, 0.0, swap_segments(last_r)), jnp.where(is_late, 0.0, swap_segments(last_i)))

    hr, hi = phase(sc_ref, n_ctx, zero, zero, False)
    phase(sl_ref, n_lat, hr, hi, True)


def s5_scan(s_ctx, s_lat, lt, bsz):
    assert S5_SEGS == 2
    n_ctx, rows, _ = s_ctx.shape
    n_lat = s_lat.shape[0]
    n_j = lt.shape[1]
    w2 = lt.shape[-1]
    return pl.pallas_call(
        functools.partial(_s5_scan_body, n_ctx=n_ctx, n_lat=n_lat, bsz=bsz),
        grid=(n_j, 2),
        in_specs=[pl.BlockSpec((n_ctx, rows, w2), lambda j, d: (0, 0, 2 * j + d)),
                  pl.BlockSpec((n_lat, rows, w2), lambda j, d: (0, 0, 2 * j + d)),
                  pl.BlockSpec((None, None, 1, w2), lambda j, d: (d, j, 0, 0))],
        out_specs=pl.BlockSpec((None, n_lat, rows, w2), lambda j, d: (d, 0, 0, j)),
        out_shape=jax.ShapeDtypeStruct((2, n_lat, rows, n_j * w2), BF16),
        scratch_shapes=[pltpu.VMEM((max(n_ctx, n_lat), rows, w2), F32)],
        compiler_params=_cparams("parallel", "parallel"),
        name="s5_scan",
    )(s_ctx, s_lat, lt)


def _s5_out_body(x_ref, hf_ref, hb_ref, wu_ref, whf_ref, whb_ref, o_ref):
    o_ref[...] = (_dot(x_ref[...], wu_ref[...]) + _dot(hf_ref[...], whf_ref[...])
                  + _dot(hb_ref[...], whb_ref[...])).astype(o_ref.dtype)


def s5_readout(xr, h, wu, wh):
    r = xr.shape[0]
    n_j, k, n = wu.shape
    w2 = wh.shape[2]
    tm = min(r, 1024)
    return pl.pallas_call(
        _s5_out_body,
        grid=(n_j, r // tm),
        in_specs=[pl.BlockSpec((tm, k), lambda j, i: (i, j)),
                  pl.BlockSpec((None, tm, w2), lambda j, i: (0, i, j)),
                  pl.BlockSpec((None, tm, w2), lambda j, i: (1, i, j)),
                  pl.BlockSpec((None, k, n), lambda j, i: (j, 0, 0)),
                  pl.BlockSpec((None, None, w2, n), lambda j, i: (0, j, 0, 0)),
                  pl.BlockSpec((None, None, w2, n), lambda j, i: (1, j, 0, 0))],
        out_specs=pl.BlockSpec((tm, n), lambda j, i: (i, j)),
        out_shape=jax.ShapeDtypeStruct((r, n_j * n), BF16),
        compiler_params=_cparams("parallel", "parallel"),
        name="s5_readout",
    )(xr, h, h, wu, wh, wh)


def _gelu_tanh(x):
    return 0.5 * x * (1.0 + jnp.tanh(math.sqrt(2.0 / math.pi) * (x + 0.044715 * (x * x * x))))


def _s5_glu_body(x_ref, y_ref, g_ref, sh_ref, sc_ref, dk_ref, w1_ref, w2_ref, b1_ref, b2_ref, gt_ref, o_ref):
    x = x_ref[...]
    u = _norm_mod(x, g_ref[...], sh_ref[...], sc_ref[...])
    y = _gelu_tanh(y_ref[...].astype(F32) + dk_ref[...] * u).astype(BF16)
    o = (_dot(y, w1_ref[...]) + b1_ref[...]) * jax.nn.sigmoid(_dot(y, w2_ref[...]) + b2_ref[...])
    o_ref[...] = x + gt_ref[...] * o


def s5_glu(x, y, g, shift, scale, d_skip, w1, b1, w2, b2, gate):
    bsz, seq, d = x.shape
    tm = min(seq, 512)
    row = lambda a: a.reshape(1, d)
    rspec = pl.BlockSpec((1, d), lambda b, i: (0, 0))
    mspec = pl.BlockSpec((None, 1, d), lambda b, i: (b, 0, 0))
    tile = pl.BlockSpec((None, tm, d), lambda b, i: (b, i, 0))
    wspec = pl.BlockSpec((d, d), lambda b, i: (0, 0), pipeline_mode=pl.Buffered(1))
    return pl.pallas_call(
        _s5_glu_body,
        grid=(bsz, seq // tm),
        in_specs=[tile, tile, rspec, mspec, mspec, rspec, wspec, wspec, rspec, rspec, mspec],
        out_specs=tile,
        out_shape=jax.ShapeDtypeStruct((bsz, seq, d), F32),
        compiler_params=_cparams("parallel", "parallel"),
        name="s5_glu",
    )(x, y, row(g), shift, scale, row(d_skip), w1, w2, row(b1), row(b2), gate)


def s5_mix(xl, hl, hc, g, sh_l, sc_l, gate_l, a_re, a_im, log_step, b_re, b_im, c_re, c_im, d_skip,
           w1, b1, w2, b2):
    bsz, seq, d = xl.shape
    xr_c = _s5_arrange(hc)
    xr_l = _s5_arrange(hl)
    rows = bsz * S5_SEGS
    n_ctx = xr_c.shape[0] // rows
    n_lat = xr_l.shape[0] // rows
    ws, wu, wh, lt = _s5_operators(a_re, a_im, log_step, b_re, b_im, c_re, c_im)
    s_ctx = s5_inject(xr_c, ws).reshape(n_ctx, rows, -1)
    s_lat = s5_inject(xr_l, ws).reshape(n_lat, rows, -1)
    h = s5_scan(s_ctx, s_lat, lt, bsz)
    y = s5_readout(xr_l, h.reshape(2, n_lat * rows, -1), wu, wh)
    y = _s5_unarrange(y, bsz)
    return s5_glu(xl, y, g, sh_l, sc_l, d_skip, w1.astype(BF16), b1, w2.astype(BF16), b2, gate_l)


def hyena_mix(x, g, shift, scale, gate, w_in, b_in, conv_w, conv_b, fw1, fb1, fw2, fb2, fw3, freq, skip,
              w_out, b_out):
    seq = x.shape[1]
    if seq >= FFT_MIN_SEQ and (2 * seq) % (2 * FFT_N2) == 0:
        a, dd, _ = hyena_filter_taps(seq, fw1, fb1, fw2, fb2, fw3, freq, F32)
        v, x0 = hyena_in(x, g, shift, scale, w_in, b_in, conv_w, conv_b, F32)
        yg = hyena_conv_fft(v, x0, skip, a, dd)
    else:
        cmat, smat = dft_matrices(seq)
        a, dd, kn = hyena_filter_taps(seq, fw1, fb1, fw2, fb2, fw3, freq, BF16)
        kr, ki = hyena_filter_dft(a, dd, cmat, smat)
        v, x0 = hyena_in(x, g, shift, scale, w_in, b_in, conv_w, conv_b, BF16)
        yg = hyena_conv(v, x0, skip, kr, ki, kn, cmat, smat)
    return mm_residual(yg, w_out, b_out, x, gate)


def kernel(x, c, ctx, c_ctx, ada_w, ada_b, norm_g, final_g, hy_w_in, hy_b_in, hy_conv_w, hy_conv_b, hy_fw1,
           hy_fb1, hy_fw2, hy_fb2, hy_fw3, hy_freq, hy_skip, hy_w_out, hy_b_out, s5_a_re, s5_a_im,
           s5_log_step, s5_b_re, s5_b_im, s5_c_re, s5_c_im, s5_d, s5_w1, s5_b1, s5_w2, s5_b2, moe_wg, moe_bg,
           moe_we, moe_be, moe_w_gate, moe_w_up, moe_w_down):
    bsz, _, d = x.shape
    depth = ada_w.shape[0]
    assert depth == 2 and bsz < SUBLANES
    c_all = jnp.zeros((SUBLANES, d), F32).at[:bsz].set(c).at[bsz].set(c_ctx)
    mods = ada_mod(c_all, ada_w, ada_b)

    def mod_rows(layer, k):
        lat = mods[layer, :bsz, k * d:(k + 1) * d][:, None, :]
        cx = jnp.broadcast_to(mods[layer, bsz, k * d:(k + 1) * d][None, None, :], (bsz, 1, d))
        return lat, cx

    (sh_a, csh_a), (sc_a, csc_a), (gt_a, cgt_a) = mod_rows(0, 0), mod_rows(0, 1), mod_rows(0, 2)
    (sh_f, csh_f), (sc_f, csc_f), (gt_f, cgt_f) = mod_rows(0, 3), mod_rows(0, 4), mod_rows(0, 5)
    hy = (hy_w_in[0].astype(BF16), hy_b_in[0], hy_conv_w[0], hy_conv_b[0], hy_fw1[0], hy_fb1[0], hy_fw2[0],
          hy_fb2[0], hy_fw3[0], hy_freq[0], hy_skip[0], hy_w_out[0].astype(BF16), hy_b_out[0])
    xl = hyena_mix(x, norm_g[0, 0], sh_a, sc_a, gt_a, *hy)
    xc = hyena_mix(ctx, norm_g[0, 0], csh_a, csc_a, cgt_a, *hy)
    (sh_a, csh_a), (sc_a, csc_a), (gt_a, _) = mod_rows(1, 0), mod_rows(1, 1), mod_rows(1, 2)
    (xl, hl), (_, hc) = hier_moe([(xl, sh_f, sc_f, gt_f), (xc, csh_f, csc_f, cgt_f)], norm_g[0, 1],
                                 moe_wg[0], moe_bg[0], moe_we[0], moe_be[0], moe_w_gate, moe_w_up, moe_w_down, 0,
                                 final_g, False,
                                 next_mods=[(norm_g[1, 0], sh_a, sc_a), (norm_g[1, 0], csh_a, csc_a)])
    (sh_f, _), (sc_f, _), (gt_f, _) = mod_rows(1, 3), mod_rows(1, 4), mod_rows(1, 5)
    xl = s5_mix(xl, hl, hc, norm_g[1, 0], sh_a, sc_a, gt_a, s5_a_re[0], s5_a_im[0], s5_log_step[0],
                s5_b_re[0], s5_b_im[0], s5_c_re[0], s5_c_im[0], s5_d[0], s5_w1[0], s5_b1[0], s5_w2[0], s5_b2[0])
    (out,) = hier_moe([(xl, sh_f, sc_f, gt_f)], norm_g[1, 1], moe_wg[1], moe_bg[1], moe_we[1], moe_be[1],
                      moe_w_gate, moe_w_up, moe_w_down, 1, final_g, True)
    return out
```

```python
import functools
import math

import numpy as np
import jax
import jax.numpy as jnp
from jax import lax
from jax.experimental import pallas as pl
from jax.experimental.pallas import tpu as pltpu

F32 = jnp.float32
BF16 = jnp.bfloat16
HIGHEST = lax.Precision.HIGHEST

NORM_EPS = 1e-6
HY_DECAY_TARGET = 1e-2
HY_FAST_PCT = 0.3
HY_SLOW_PCT = 1.5
TOP_K = 2

V7X_VMEM_LIMIT_BYTES = 56 * 1024 * 1024
LANES = 128
SUBLANES = 8
S5_TAU = 8
S5_SEGS = 2
S5_SCAN_UNROLL = 8
MOE_TM = 256
MOE_BM = 256
MOE_NCH = 12
MOE_STAGE = 4
MOE_CAST_ROWS = 128
MOE_WIN_ALIGN = 1024
MOE_WIN = 2 * MOE_WIN_ALIGN
NEG_BIG = -1e30


def _cparams(*sem):
    return pltpu.CompilerParams(dimension_semantics=sem, vmem_limit_bytes=V7X_VMEM_LIMIT_BYTES)


def _norm_mod(x, g, shift, scale):
    ms = jnp.mean(x * x, axis=-1, keepdims=True)
    return (x * lax.rsqrt(ms + NORM_EPS) * g) * (1.0 + scale) + shift


def _dot(a, b):
    return jnp.dot(a, b, preferred_element_type=F32)


def _ada_body(c_ref, w_ref, b_ref, o_ref):
    x = c_ref[...]
    s = (x * jax.nn.sigmoid(x)).astype(BF16)
    o_ref[...] = _dot(s, w_ref[...].astype(BF16)) + b_ref[...]


def ada_mod(c_all, ada_w, ada_b):
    depth, d, n = ada_w.shape
    tn = min(n, 1024)
    return pl.pallas_call(
        _ada_body,
        grid=(depth, n // tn),
        in_specs=[pl.BlockSpec((SUBLANES, d), lambda l, j: (0, 0)),
                  pl.BlockSpec((None, d, tn), lambda l, j: (l, 0, j)),
                  pl.BlockSpec((None, 1, tn), lambda l, j: (l, 0, j))],
        out_specs=pl.BlockSpec((None, SUBLANES, tn), lambda l, j: (l, 0, j)),
        out_shape=jax.ShapeDtypeStruct((depth, SUBLANES, n), F32),
        compiler_params=_cparams("parallel", "parallel"),
        name="ada_mod",
    )(c_all, ada_w, ada_b.reshape(depth, 1, n))


def _hy_in_body(xp_ref, xm_ref, xn_ref, g_ref, sh_ref, sc_ref,
                w0_ref, w1_ref, w2_ref, b0_ref, b1_ref, b2_ref,
                cw0_ref, cw1_ref, cw2_ref, cb0_ref, cb1_ref, cb2_ref,
                v_ref, x0_ref):
    i = pl.program_id(2)
    ni = pl.num_programs(2)
    tm = xm_ref.shape[0]
    x = jnp.concatenate([xp_ref[...], xm_ref[...], xn_ref[...]], axis=0)
    h = _norm_mod(x, g_ref[...], sh_ref[...], sc_ref[...]).astype(BF16)
    rows = lax.broadcasted_iota(jnp.int32, (tm + 2 * SUBLANES, 1), 0)
    valid = jnp.logical_and(jnp.logical_or(rows >= SUBLANES, i > 0),
                            jnp.logical_or(rows < tm + SUBLANES, i < ni - 1))

    def part(w_ref, b_ref, cw_ref, cb_ref):
        z = jnp.where(valid, _dot(h, w_ref[...]) + b_ref[...], 0.0)
        cw = cw_ref[...]
        zp = pltpu.roll(z, 1, axis=0)[SUBLANES:tm + SUBLANES]
        zn = pltpu.roll(z, tm + 2 * SUBLANES - 1, axis=0)[SUBLANES:tm + SUBLANES]
        return zp * cw[0:1] + z[SUBLANES:tm + SUBLANES] * cw[1:2] + zn * cw[2:3] + cb_ref[...]

    x0 = part(w0_ref, b0_ref, cw0_ref, cb0_ref)
    x1 = part(w1_ref, b1_ref, cw1_ref, cb1_ref)
    v = part(w2_ref, b2_ref, cw2_ref, cb2_ref) * x1
    v_ref[...] = v.astype(v_ref.dtype)
    x0_ref[...] = x0.astype(BF16)


def hyena_in(x, g, shift, scale, w_in, b_in, conv_w, conv_b, v_dtype):
    bsz, seq, d = x.shape
    tm = min(seq, 512)
    tn = min(d, 1024)
    nj = d // tn
    r8 = tm // SUBLANES
    last8 = seq // SUBLANES - 1
    row = lambda a: a.reshape(1, -1)
    wspec = lambda k: pl.BlockSpec((d, tn), lambda j, b, i: (0, k * nj + j))
    rspec = lambda k: pl.BlockSpec((1, tn), lambda j, b, i: (0, k * nj + j))
    cspec = lambda k: pl.BlockSpec((3, tn), lambda j, b, i: (0, k * nj + j))
    mspec = pl.BlockSpec((None, 1, d), lambda j, b, i: (b, 0, 0))
    out_spec = pl.BlockSpec((None, tm, tn), lambda j, b, i: (b, i, j))
    return pl.pallas_call(
        _hy_in_body,
        grid=(nj, bsz, seq // tm),
        in_specs=[pl.BlockSpec((None, SUBLANES, d), lambda j, b, i: (b, jnp.maximum(i * r8 - 1, 0), 0)),
                  pl.BlockSpec((None, tm, d), lambda j, b, i: (b, i, 0)),
                  pl.BlockSpec((None, SUBLANES, d), lambda j, b, i: (b, jnp.minimum((i + 1) * r8, last8), 0)),
                  pl.BlockSpec((1, d), lambda j, b, i: (0, 0)), mspec, mspec,
                  wspec(0), wspec(1), wspec(2), rspec(0), rspec(1), rspec(2),
                  cspec(0), cspec(1), cspec(2), rspec(0), rspec(1), rspec(2)],
        out_specs=[out_spec, out_spec],
        out_shape=[jax.ShapeDtypeStruct((bsz, seq, d), v_dtype), jax.ShapeDtypeStruct((bsz, seq, d), BF16)],
        compiler_params=_cparams("parallel", "parallel", "parallel"),
        name="hyena_in",
    )(x, x, x, row(g), shift, scale, w_in, w_in, w_in, row(b_in), row(b_in), row(b_in),
      conv_w, conv_w, conv_w, row(conv_b), row(conv_b), row(conv_b))


def _dft_tables(seq, blk):
    n = 2 * seq
    s = np.arange(seq, dtype=np.int64)[None, :]
    fl = np.arange(blk, dtype=np.int64)[:, None]
    fh = (np.arange(seq // blk, dtype=np.int64) * blk)[:, None]
    w = 2.0 * math.pi / n
    ang_b = ((fl * s) % n) * w
    ang_a = ((fh * s) % n) * w
    f32 = lambda m: jnp.asarray(m.astype(np.float32))
    return (f32(np.cos(ang_a)[:, None, :]), f32(np.sin(ang_a)[:, None, :]), f32(np.cos(ang_b)), f32(np.sin(ang_b)))


def _dft_gen_body(ca_ref, sa_ref, cb_ref, sb_ref, c_ref, s_ref):
    ca, sa, cb, sb = ca_ref[...], sa_ref[...], cb_ref[...], sb_ref[...]
    c_ref[...] = (ca * cb - sa * sb).astype(BF16)
    s_ref[...] = (sa * cb + ca * sb).astype(BF16)


def dft_matrices(seq):
    blk = min(seq, 256)
    ca, sa, cb, sb = _dft_tables(seq, blk)
    aspec = pl.BlockSpec((None, 1, seq), lambda i: (i, 0, 0))
    bspec = pl.BlockSpec((blk, seq), lambda i: (0, 0))
    ospec = pl.BlockSpec((blk, seq), lambda i: (i, 0))
    return pl.pallas_call(
        _dft_gen_body,
        grid=(seq // blk,),
        in_specs=[aspec, aspec, bspec, bspec],
        out_specs=[ospec, ospec],
        out_shape=[jax.ShapeDtypeStruct((seq, seq), BF16)] * 2,
        compiler_params=_cparams("parallel"),
        name="dft_matrices",
    )(ca, sa, cb, sb)


def _alt_sign(rows):
    return jnp.where((rows & 1) == 0, 1.0, -1.0).astype(F32)


def _filt_body(h2_ref, wf_ref, wb_ref, dl_ref, a_ref, d_ref, ny_ref):
    seq = h2_ref.shape[0]
    h2 = h2_ref[...]
    row = lax.broadcasted_iota(jnp.int32, (seq, 1), 0)
    t = row.astype(F32) * (1.0 / (seq - 1))
    win = jnp.exp(-t * dl_ref[...])
    hf = jnp.dot(h2, wf_ref[...], precision=HIGHEST, preferred_element_type=F32) * win
    hb = jnp.dot(h2, wb_ref[...], precision=HIGHEST, preferred_element_type=F32) * win
    hb = jnp.where(row == 0, 0.0, hb)
    nrm = (jnp.sum(jnp.abs(hf), axis=0, keepdims=True) + jnp.sum(jnp.abs(hb), axis=0, keepdims=True))
    inv = 1.0 / nrm
    a = (hf + hb) * inv
    a_ref[...] = a.astype(a_ref.dtype)
    d_ref[...] = ((hb - hf) * inv).astype(d_ref.dtype)
    ny = jnp.sum(a * _alt_sign(row), axis=0, keepdims=True) * (1.0 / (2 * seq))
    ny_ref[...] = jnp.broadcast_to(ny, ny_ref.shape)


def _khat_body(a_ref, d_ref, c_ref, s_ref, kr_ref, ki_ref):
    i = pl.program_id(1)
    tm = c_ref.shape[0]
    seq = c_ref.shape[1]
    f = i * tm + lax.broadcasted_iota(jnp.int32, (tm, 1), 0)
    w = jnp.where(f == 0, 1.0, 2.0).astype(F32) * (1.0 / (2 * seq))
    kr_ref[...] = _dot(c_ref[...], a_ref[...]) * w
    ki_ref[...] = _dot(s_ref[...], d_ref[...]) * w


def hyena_filter_taps(seq, fw1, fb1, fw2, fb2, fw3, freq, taps_dtype):
    d = fw3.shape[1] // 2
    bands_n = (fw1.shape[0] - 1) // 2
    t = np.linspace(0.0, 1.0, seq)[:, None]
    w = (2.0 * math.pi / seq) * np.arange(seq)[:, None]
    bands = np.linspace(1e-4, bands_n - 1, bands_n)[None, :]
    z = jnp.asarray(np.concatenate([t, np.cos(bands * w), -np.sin(bands * w)], axis=-1).astype(np.float32))
    h = jnp.sin(freq * (jnp.dot(z, fw1, precision=HIGHEST) + fb1))
    h2 = jnp.sin(freq * (jnp.dot(h, fw2, precision=HIGHEST) + fb2))
    max_decay = math.log(HY_DECAY_TARGET) / HY_FAST_PCT
    min_decay = math.log(HY_DECAY_TARGET) / HY_SLOW_PCT
    deltas = jnp.abs(jnp.linspace(min_decay, max_decay, d, dtype=F32))[None, :]

    order = h2.shape[1]
    tn = min(d, 256)
    nj = d // tn
    return pl.pallas_call(
        _filt_body,
        grid=(nj,),
        in_specs=[pl.BlockSpec((seq, order), lambda j: (0, 0)),
                  pl.BlockSpec((order, tn), lambda j: (0, j)),
                  pl.BlockSpec((order, tn), lambda j: (0, nj + j)),
                  pl.BlockSpec((1, tn), lambda j: (0, j))],
        out_specs=[pl.BlockSpec((seq, tn), lambda j: (0, j)),
                   pl.BlockSpec((seq, tn), lambda j: (0, j)),
                   pl.BlockSpec((SUBLANES, tn), lambda j: (0, j))],
        out_shape=[jax.ShapeDtypeStruct((seq, d), taps_dtype), jax.ShapeDtypeStruct((seq, d), taps_dtype),
                   jax.ShapeDtypeStruct((SUBLANES, d), F32)],
        compiler_params=_cparams("parallel"),
        name="hyena_filter_taps",
    )(h2, fw3, fw3, deltas)


def hyena_filter_dft(a, dd, cmat, smat):
    seq, d = a.shape
    tm = min(seq, 512)
    tn2 = min(d, 512)
    return pl.pallas_call(
        _khat_body,
        grid=(d // tn2, seq // tm),
        in_specs=[pl.BlockSpec((seq, tn2), lambda j, i: (0, j)),
                  pl.BlockSpec((seq, tn2), lambda j, i: (0, j)),
                  pl.BlockSpec((tm, seq), lambda j, i: (i, 0)),
                  pl.BlockSpec((tm, seq), lambda j, i: (i, 0))],
        out_specs=[pl.BlockSpec((tm, tn2), lambda j, i: (i, j))] * 2,
        out_shape=[jax.ShapeDtypeStruct((seq, d), F32)] * 2,
        compiler_params=_cparams("parallel", "parallel"),
        name="hyena_filter_dft",
    )(a, dd, cmat, smat)


def _dft_fwd_body(v_ref, c_ref, s_ref, kr_ref, ki_ref, kn_ref, ya_ref, yb_ref, yn_ref):
    i = pl.program_id(2)
    v = v_ref[...]
    vr = _dot(c_ref[...], v)
    p = _dot(s_ref[...], v)
    kr = kr_ref[...]
    ki = ki_ref[...]
    ya_ref[...] = (vr * kr + p * ki).astype(BF16)
    yb_ref[...] = (p * kr - vr * ki).astype(BF16)

    @pl.when(i == 0)
    def _():
        seq = v.shape[0]
        row = lax.broadcasted_iota(jnp.int32, (seq, 1), 0)
        vl = jnp.sum(v.astype(F32) * _alt_sign(row), axis=0, keepdims=True)
        yn_ref[...] = jnp.broadcast_to(vl * kn_ref[0:1, :], yn_ref.shape)


def _dft_inv_body(ya_ref, yb_ref, c_ref, s_ref, v_ref, x0_ref, skip_ref, yn_ref, o_ref):
    i = pl.program_id(2)
    tm = c_ref.shape[0]
    acc = _dot(c_ref[...], ya_ref[...]) + _dot(s_ref[...], yb_ref[...])
    t = i * tm + lax.broadcasted_iota(jnp.int32, (tm, 1), 0)
    y = acc + _alt_sign(t) * yn_ref[0:1, :] + skip_ref[...] * v_ref[...].astype(F32)
    o_ref[...] = (y * x0_ref[...].astype(F32)).astype(BF16)


def hyena_conv(v, x0, skip, kr, ki, kn, cmat, smat):
    bsz, seq, d = v.shape
    tm = min(seq, 512)
    tn = min(d, 512)
    grid = (bsz, d // tn, seq // tm)
    full = pl.BlockSpec((None, seq, tn), lambda b, j, i: (b, 0, j))
    mat = pl.BlockSpec((tm, seq), lambda b, j, i: (i, 0))
    tile = pl.BlockSpec((None, tm, tn), lambda b, j, i: (b, i, j))
    ktile = pl.BlockSpec((tm, tn), lambda b, j, i: (i, j))
    nyq = pl.BlockSpec((None, SUBLANES, tn), lambda b, j, i: (b, 0, j))
    ya, yb, yn = pl.pallas_call(
        _dft_fwd_body,
        grid=grid,
        in_specs=[full, mat, mat, ktile, ktile, pl.BlockSpec((SUBLANES, tn), lambda b, j, i: (0, j))],
        out_specs=[tile, tile, nyq],
        out_shape=[jax.ShapeDtypeStruct((bsz, seq, d), BF16)] * 2
        + [jax.ShapeDtypeStruct((bsz, SUBLANES, d), F32)],
        compiler_params=_cparams("parallel", "parallel", "arbitrary"),
        name="hyena_dft_fwd",
    )(v, cmat, smat, kr, ki, kn)
    return pl.pallas_call(
        _dft_inv_body,
        grid=grid,
        in_specs=[full, full, mat, mat, tile, tile, pl.BlockSpec((1, tn), lambda b, j, i: (0, j)), nyq],
        out_specs=tile,
        out_shape=jax.ShapeDtypeStruct((bsz, seq, d), BF16),
        compiler_params=_cparams("parallel", "parallel", "parallel"),
        name="hyena_dft_inv",
    )(ya, yb, cmat, smat, v, x0, skip.reshape(1, d), yn)


FFT_N2 = 128
FFT_MIN_SEQ = 1024
FFT_UNROLL = 8


def _fft_matrices(seq):
    n = 2 * seq
    n2 = FFT_N2
    n1 = n // n2
    r8 = SUBLANES
    q = np.arange(n2 // r8, dtype=np.int64)[:, None, None, None]
    f1 = np.arange(n1, dtype=np.int64)[None, :, None, None]
    r = np.arange(r8, dtype=np.int64)[None, None, :, None]
    t1 = np.arange(n1 // 2, dtype=np.int64)[None, None, None, :]
    ang = ((f1 * (t1 * n2 + q * r8 + r)) % n) * (2.0 * math.pi / n)
    g = np.stack([np.cos(ang), -np.sin(ang)], axis=3)
    eye = np.eye(r8)[None, None, :, None, None, :]
    ma = (g[..., None] * eye).reshape(n2 // r8, n1 * r8 * 2, (n1 // 2) * r8).astype(np.float32)
    f2 = np.arange(n2, dtype=np.int64)[:, None]
    t2 = np.arange(n2, dtype=np.int64)[None, :]
    th = ((f2 * t2) % n2) * (2.0 * math.pi / n2)
    co, si = np.cos(th), np.sin(th)
    wc = np.stack([np.stack([co, si], axis=-1), np.stack([-si, co], axis=-1)], axis=0)
    wc = wc.reshape(2 * n2, 2 * n2).astype(np.float32)
    as_bf16 = lambda m: jnp.asarray(np.ascontiguousarray(m).astype(BF16))
    return as_bf16(ma), as_bf16(np.swapaxes(ma, 1, 2)), as_bf16(wc), as_bf16(wc.T)


def _fft_stage_a(x_ref, ma_ref, s1):
    n1h, n_q, r8, tn = x_ref.shape
    n1 = s1.shape[0]

    def body(q, carry):
        x = x_ref[:, pl.ds(q, 1), :, :].reshape(n1h * r8, tn).astype(BF16)
        a = _dot(ma_ref[q], x).astype(BF16)
        s1[:, pl.ds(pl.multiple_of(q * 2 * r8, 2 * r8), 2 * r8), :] = a.reshape(n1, 2 * r8, tn)
        return carry

    lax.fori_loop(0, n_q, body, 0, unroll=FFT_UNROLL)


def _fft_conv_body(v_ref, x0_ref, k_ref, skip_ref, ma_ref, mat_ref, wc_ref, wci_ref, o_ref, s1, ysc):
    n1h, n_q, r8, tn = v_ref.shape
    n1 = s1.shape[0]
    n2 = s1.shape[1] // 2
    seq = n1h * n_q * r8
    _fft_stage_a(v_ref, ma_ref, s1)

    def slab(f, carry):
        y = _dot(wc_ref[...], s1[f])
        yr, yi = y[:n2], y[n2:]
        kr = k_ref[f, 0].astype(F32)
        ki = k_ref[f, 1].astype(F32)
        p = jnp.concatenate([yr * kr - yi * ki, yr * ki + yi * kr], axis=0).astype(BF16)
        s1[f] = _dot(wci_ref[...], p).astype(BF16)
        return carry

    lax.fori_loop(0, n1, slab, 0, unroll=2 * FFT_UNROLL)

    def inv_a(q, carry):
        z = s1[:, pl.ds(pl.multiple_of(q * 2 * r8, 2 * r8), 2 * r8), :].reshape(n1 * 2 * r8, tn)
        ysc[:, pl.ds(q, 1), :, :] = _dot(mat_ref[q], z).reshape(n1h, 1, r8, tn)
        return carry

    lax.fori_loop(0, n_q, inv_a, 0, unroll=FFT_UNROLL)
    y = ysc[...].reshape(seq, tn) + skip_ref[...] * v_ref[...].reshape(seq, tn)
    o_ref[...] = (y * x0_ref[...].astype(F32)).astype(BF16)


def _fft_filter_body(a_ref, d_ref, ma_ref, wc_ref, k_ref, s1):
    n1 = s1.shape[0]
    n2 = s1.shape[1] // 2
    scale = 1.0 / (n1 * n2)
    for src_ref, part, sign in ((a_ref, 0, scale), (d_ref, 1, -scale)):
        _fft_stage_a(src_ref, ma_ref, s1)

        def slab(f, carry):
            y = _dot(wc_ref[part * n2:(part + 1) * n2, :], s1[f])
            k_ref[f, part] = (y * sign).astype(BF16)
            return carry

        lax.fori_loop(0, n1, slab, 0, unroll=FFT_UNROLL)


def hyena_conv_fft(v, x0, skip, a, dd):
    bsz, seq, d = v.shape
    n2 = FFT_N2
    n1 = 2 * seq // n2
    n_q = n2 // SUBLANES
    tn = min(d, 256)
    ma, mat, wc, wci = _fft_matrices(seq)
    const = lambda shape: pl.BlockSpec(shape, lambda *_: (0,) * len(shape), pipeline_mode=pl.Buffered(1))
    view = lambda t: t.reshape(t.shape[:-2] + (n1 // 2, n_q, SUBLANES, d))
    tap = pl.BlockSpec((n1 // 2, n_q, SUBLANES, tn), lambda j: (0, 0, 0, j))
    khat = pl.pallas_call(
        _fft_filter_body,
        grid=(d // tn,),
        in_specs=[tap, tap, const(ma.shape), const(wc.shape)],
        out_specs=pl.BlockSpec((n1, 2, n2, tn), lambda j: (0, 0, 0, j)),
        out_shape=jax.ShapeDtypeStruct((n1, 2, n2, d), BF16),
        scratch_shapes=[pltpu.VMEM((n1, 2 * n2, tn), BF16)],
        compiler_params=_cparams("parallel"),
        name="hyena_filter_fft",
    )(view(a), view(dd), ma, wc)
    return pl.pallas_call(
        _fft_conv_body,
        grid=(d // tn, bsz),
        in_specs=[pl.BlockSpec((None, n1 // 2, n_q, SUBLANES, tn), lambda j, b: (b, 0, 0, 0, j)),
                  pl.BlockSpec((None, seq, tn), lambda j, b: (b, 0, j)),
                  pl.BlockSpec((n1, 2, n2, tn), lambda j, b: (0, 0, 0, j), pipeline_mode=pl.Buffered(1)),
                  pl.BlockSpec((1, tn), lambda j, b: (0, j)),
                  const(ma.shape), const(mat.shape), const(wc.shape), const(wci.shape)],
        out_specs=pl.BlockSpec((None, seq, tn), lambda j, b: (b, 0, j)),
        out_shape=jax.ShapeDtypeStruct((bsz, seq, d), BF16),
        scratch_shapes=[pltpu.VMEM((n1, 2 * n2, tn), BF16), pltpu.VMEM((n1 // 2, n_q, SUBLANES, tn), F32)],
        compiler_params=_cparams("parallel", "arbitrary"),
        name="hyena_conv_fft",
    )(view(v), x0, khat, skip.reshape(1, d), ma, mat, wc, wci)


def _mm_res_body(x_ref, w_ref, b_ref, res_ref, gate_ref, o_ref):
    o_ref[...] = res_ref[...] + gate_ref[...] * (_dot(x_ref[...], w_ref[...]) + b_ref[...])


def mm_residual(x, w, b, res, gate):
    bsz, seq, k = x.shape
    n = w.shape[1]
    tm = min(seq, 512)
    return pl.pallas_call(
        _mm_res_body,
        grid=(bsz, seq // tm),
        in_specs=[pl.BlockSpec((None, tm, k), lambda b, i: (b, i, 0)),
                  pl.BlockSpec((k, n), lambda b, i: (0, 0)),
                  pl.BlockSpec((1, n), lambda b, i: (0, 0)),
                  pl.BlockSpec((None, tm, n), lambda b, i: (b, i, 0)),
                  pl.BlockSpec((None, 1, n), lambda b, i: (b, 0, 0))],
        out_specs=pl.BlockSpec((None, tm, n), lambda b, i: (b, i, 0)),
        out_shape=jax.ShapeDtypeStruct((bsz, seq, n), F32),
        compiler_params=_cparams("parallel", "parallel"),
        name="mm_residual",
    )(x, w, b.reshape(1, n), res, gate)


def _moe_pre_body(*refs, n_groups, n_experts, tile_offs):
    n_streams = len(tile_offs) - 1
    g_ref, wr_ref, br_ref, tok_ref, eid_ref, gate_ref = refs[3 * n_streams:]
    i = pl.program_id(0)
    for k in range(n_streams):
        x_ref, sh_ref, sc_ref = refs[3 * k:3 * k + 3]

        @pl.when(jnp.logical_and(i >= tile_offs[k], i < tile_offs[k + 1]))
        def _():
            tok = _norm_mod(x_ref[...], g_ref[...], sh_ref[...], sc_ref[...])
            _route_tokens(tok, wr_ref, br_ref, tok_ref, eid_ref, gate_ref, n_groups, n_experts)


def _route_tokens(tok, wr_ref, br_ref, tok_ref, eid_ref, gate_ref, n_groups, n_experts):
    tok_ref[...] = tok
    t_hi = tok.astype(BF16)
    t_lo = (tok - t_hi.astype(F32)).astype(BF16)
    logits = (_dot(t_hi, wr_ref[0]) + _dot(t_hi, wr_ref[1]) + _dot(t_lo, wr_ref[0])) + br_ref[...]
    lane = lax.broadcasted_iota(jnp.int32, logits.shape, 1)
    per = n_experts // n_groups
    big = jnp.int32(1 << 20)
    gmask = jnp.logical_and(lane >= n_experts, lane < n_experts + n_groups)
    gl = jnp.where(gmask, logits, NEG_BIG)
    gmax = jnp.max(gl, axis=-1, keepdims=True)
    gidx = jnp.min(jnp.where(gl == gmax, lane - n_experts, big), axis=-1, keepdims=True)
    p_top = 1.0 / jnp.sum(jnp.where(gmask, jnp.exp(gl - gmax), 0.0), axis=-1, keepdims=True)
    lo = gidx * per
    emask = jnp.logical_and(lane >= lo, lane < lo + per)
    el = jnp.where(emask, logits, NEG_BIG)
    m1 = jnp.max(el, axis=-1, keepdims=True)
    i1 = jnp.min(jnp.where(el == m1, lane, big), axis=-1, keepdims=True)
    el2 = jnp.where(lane == i1, NEG_BIG, el)
    m2 = jnp.max(el2, axis=-1, keepdims=True)
    i2 = jnp.min(jnp.where(el2 == m2, lane, big), axis=-1, keepdims=True)
    e21 = jnp.exp(m2 - m1)
    g1 = p_top / (1.0 + e21)
    g2 = g1 * e21
    ids = jnp.where(lane == 0, i1, jnp.where(lane == 1, i2, -1))
    eid_ref[...] = ids.T[0:SUBLANES, :]
    gate_ref[...] = jnp.where(lane == 0, g1, jnp.where(lane == 1, g2, 0.0))


def moe_pre(streams, g, wr, br, n_groups, n_experts):
    d = streams[0][0].shape[2]
    tm = MOE_TM
    tile_offs = [0]
    in_specs, args = [], []
    for x, shift, scale in streams:
        bsz, seq, _ = x.shape
        nt = seq // tm
        n_tiles = bsz * nt
        off = tile_offs[-1]
        tile_offs.append(off + n_tiles)

        def tile(i, off=off, n_tiles=n_tiles):
            return jnp.clip(i - off, 0, n_tiles - 1)

        in_specs += [pl.BlockSpec((None, tm, d), lambda i, tile=tile, nt=nt: (tile(i) // nt, tile(i) % nt, 0)),
                     pl.BlockSpec((None, 1, d), lambda i, tile=tile, nt=nt: (tile(i) // nt, 0, 0)),
                     pl.BlockSpec((None, 1, d), lambda i, tile=tile, nt=nt: (tile(i) // nt, 0, 0))]
        args += [x, shift, scale]
    in_specs += [pl.BlockSpec((1, d), lambda i: (0, 0)),
                 pl.BlockSpec((2, d, LANES), lambda i: (0, 0, 0)),
                 pl.BlockSpec((1, LANES), lambda i: (0, 0))]
    wr_hi = wr.astype(BF16)
    wr_split = jnp.stack([wr_hi, (wr - wr_hi.astype(F32)).astype(BF16)])
    args += [g.reshape(1, d), wr_split, br]
    total = tile_offs[-1] * tm
    rout = pl.BlockSpec((tm, LANES), lambda i: (i, 0))
    tok, eid, gate = pl.pallas_call(
        functools.partial(_moe_pre_body, n_groups=n_groups, n_experts=n_experts, tile_offs=tuple(tile_offs)),
        grid=(tile_offs[-1],),
        in_specs=in_specs,
        out_specs=[pl.BlockSpec((tm, d), lambda i: (i, 0)),
                   pl.BlockSpec((None, SUBLANES, tm), lambda i: (i, 0, 0)), rout],
        out_shape=[jax.ShapeDtypeStruct((total, d), F32),
                   jax.ShapeDtypeStruct((tile_offs[-1], SUBLANES, tm), jnp.int32),
                   jax.ShapeDtypeStruct((total, LANES), F32)],
        compiler_params=_cparams("parallel"),
        name="moe_pre",
    )(*args)
    return tok, eid, gate, tile_offs[:-1]


def _start_row_gather(row_index, n_rows, src_hbm, dst_vmem, sem):
    def body(g, c):
        r0 = pl.multiple_of(g * SUBLANES, SUBLANES)
        dst_tile = dst_vmem.at[pl.ds(r0, SUBLANES)]
        for k in range(SUBLANES):
            pltpu.make_async_copy(src_hbm.at[pl.ds(row_index(r0 + k), 1)], dst_tile.at[pl.ds(k, 1)], sem).start()
        return c

    lax.fori_loop(0, n_rows // SUBLANES, body, 0, unroll=2)


def _wait_row_gather(n_rows, src_hbm, dst_vmem, sem):
    pltpu.make_async_copy(src_hbm.at[pl.ds(0, n_rows)], dst_vmem, sem).wait()


def _expert_body(bv_ref, rk_ref, pe_ref, tot_ref, p0_ref, ts_hbm, tok_ref, wg_hbm, wu_hbm, wd_hbm, o_ref,
                 xbuf, xsem, win, isem, wcache, stg, wsem, cnt, *, layer):
    i = pl.program_id(0)
    n = pl.num_programs(0)
    slot = i % 2
    cr, cc = stg.shape[1:]
    total = tot_ref[0]
    mats_hbm = (wg_hbm, wu_hbm, wd_hbm)

    @pl.when(i == 0)
    def _():
        cnt[0] = 0
        cnt[1] = 0

    def live(b):
        return jnp.logical_and(b < n, bv_ref[jnp.minimum(b, n - 1)] > 0)

    def window(b):
        s = b % 2
        lo = pl.multiple_of((p0_ref[jnp.minimum(b, n - 1)] // MOE_WIN_ALIGN) * MOE_WIN_ALIGN, MOE_WIN_ALIGN)
        return pltpu.make_async_copy(ts_hbm.at[pl.ds(lo, MOE_WIN)],
                                     win.at[pl.ds(pl.multiple_of(s * MOE_WIN, MOE_WIN), MOE_WIN)], isem.at[s])

    def gather(b):
        s = b % 2
        base = s * MOE_WIN + p0_ref[jnp.minimum(b, n - 1)] % MOE_WIN_ALIGN
        _start_row_gather(lambda r: win[base + r], MOE_BM, tok_ref, xbuf.at[s], xsem.at[s])

    @pl.when(jnp.logical_and(i == 0, live(0)))
    def _():
        window(0).start()
        window(0).wait()
        gather(0)

    @pl.when(jnp.logical_and(i == 0, live(1)))
    def _():
        window(1).start()

    @pl.when(live(i + 1))
    def _():
        window(i + 1).wait()
        gather(i + 1)

    @pl.when(live(i + 2))
    def _():
        window(i + 2).start()

    def chunk_geom(c):
        q = c % MOE_NCH
        m = q // 4
        sub = q % 4
        r0 = jnp.where(m < 2, sub, sub // 2) * cr
        c0 = jnp.where(m < 2, 0, sub % 2) * cc
        return m, pl.multiple_of(r0, cr), pl.multiple_of(c0, cc)

    def issue(c):
        e = pe_ref[c // MOE_NCH]
        m, r0, c0 = chunk_geom(c)
        s = c % MOE_STAGE
        for k, w_hbm in enumerate(mats_hbm):
            @pl.when(m == k)
            def _():
                pltpu.make_async_copy(w_hbm.at[layer, e, pl.ds(r0, cr), pl.ds(c0, cc)], stg.at[s],
                                      wsem.at[s]).start()

    def cast(c):
        s = c % MOE_STAGE
        pltpu.make_async_copy(wg_hbm.at[layer, 0, pl.ds(0, cr), pl.ds(0, cc)], stg.at[s], wsem.at[s]).wait()
        ws = (c // MOE_NCH) % 2
        q = c % MOE_NCH
        step = min(MOE_CAST_ROWS, cr)
        assert cr % step == 0

        def slab(k, carry):
            rows = pl.ds(pl.multiple_of(k * step, step), step)
            wcache[ws, q, rows, :] = stg[s, rows, :].astype(BF16)
            return carry

        lax.fori_loop(0, cr // step, slab, 0)

    valid = bv_ref[i] > 0
    rank = rk_ref[i]
    issued = cnt[0]
    done = cnt[1]
    limit = jnp.minimum(total, MOE_NCH * (rank + 2))
    need = jnp.where(valid, MOE_NCH * (rank + 1), done)

    def fill(issued, done):
        hi = jnp.minimum(limit, done + MOE_STAGE)

        def body(c, carry):
            issue(c)
            return carry

        lax.fori_loop(issued, hi, body, 0)
        return jnp.maximum(issued, hi)

    def cast_and_refill(c, issued):
        cast(c)
        more = issued < jnp.minimum(limit, c + 1 + MOE_STAGE)

        @pl.when(more)
        def _():
            issue(issued)

        return issued + more.astype(jnp.int32)

    issued = fill(issued, done)
    issued = lax.fori_loop(done, need, cast_and_refill, issued)
    done = jnp.maximum(done, need)

    @pl.when(valid)
    def _():
        ws = rank % 2
        _wait_row_gather(MOE_BM, tok_ref, xbuf.at[slot], xsem.at[slot])
        x = xbuf[slot].astype(BF16)
        gate = sum(_dot(x[:, k * cr:(k + 1) * cr], wcache[ws, k]) for k in range(4))
        up = sum(_dot(x[:, k * cr:(k + 1) * cr], wcache[ws, 4 + k]) for k in range(4))
        h = (gate * jax.nn.sigmoid(gate) * up).astype(BF16)
        for half in range(2):
            o_ref[:, half * cc:(half + 1) * cc] = sum(
                _dot(h[:, k * cr:(k + 1) * cr], wcache[ws, 8 + 2 * k + half]) for k in range(2))

    @pl.when(jnp.logical_not(valid))
    def _():
        o_ref[...] = jnp.zeros_like(o_ref)

    fetched = issued
    issued = lax.fori_loop(done, fetched, cast_and_refill, issued)
    done = jnp.maximum(done, fetched)
    last = i == n - 1
    tail = jnp.where(last, issued, done)

    def drain(c, carry):
        cast(c)
        return carry

    lax.fori_loop(done, tail, drain, 0)
    cnt[0] = issued
    cnt[1] = jnp.maximum(done, tail)


def moe_experts(tok, tok_sorted, block_p0, block_valid, block_rank, present, n_chunks, w_gate, w_up, w_down,
                layer):
    d = tok.shape[1]
    n_blocks = block_valid.shape[0]
    n_rows = n_blocks * MOE_BM
    dh = w_gate.shape[3]
    assert 2 * dh == d and MOE_NCH == 12
    cr, cc = d // 4, dh
    any_spec = pl.BlockSpec(memory_space=pl.ANY)
    grid_spec = pltpu.PrefetchScalarGridSpec(
        num_scalar_prefetch=5,
        grid=(n_blocks,),
        in_specs=[any_spec, any_spec, any_spec, any_spec, any_spec],
        out_specs=pl.BlockSpec((MOE_BM, d), lambda i, *_: (i, 0)),
        scratch_shapes=[pltpu.VMEM((2, MOE_BM, d), F32), pltpu.SemaphoreType.DMA((2,)),
                        pltpu.SMEM((2 * MOE_WIN,), jnp.int32), pltpu.SemaphoreType.DMA((2,)),
                        pltpu.VMEM((2, MOE_NCH, cr, cc), BF16),
                        pltpu.VMEM((MOE_STAGE, cr, cc), F32), pltpu.SemaphoreType.DMA((MOE_STAGE,)),
                        pltpu.SMEM((2,), jnp.int32)],
    )
    return pl.pallas_call(
        functools.partial(_expert_body, layer=layer),
        grid_spec=grid_spec,
        out_shape=jax.ShapeDtypeStruct((n_rows, d), F32),
        compiler_params=_cparams("arbitrary"),
        name="moe_experts",
    )(block_valid, block_rank, present, n_chunks, block_p0, tok_sorted, tok, w_gate, w_up, w_down)


def _combine_body(dest_ref, nxt_ref, os_ref, gate_ref, res_ref, gt_ref, fg_ref, *rest, final_norm, with_next):
    if with_next:
        ng_ref, nsh_ref, nsc_ref, o_ref, h_ref, buf, sem = rest
    else:
        o_ref, buf, sem = rest
    rows = res_ref.shape[0]
    i = pl.program_id(0)
    n = pl.num_programs(0)
    slot = i % 2

    def start(idx_ref, s):
        for k in range(TOP_K):
            _start_row_gather(lambda r, k=k: idx_ref[k * rows + r], rows, os_ref, buf.at[s, k], sem.at[s])

    @pl.when(i == 0)
    def _():
        start(dest_ref, 0)

    @pl.when(i + 1 < n)
    def _():
        start(nxt_ref, 1 - slot)

    for k in range(TOP_K):
        _wait_row_gather(rows, os_ref, buf.at[slot, k], sem.at[slot])
    gates = gate_ref[...]
    mo = gates[:, 0:1] * buf[slot, 0] + gates[:, 1:2] * buf[slot, 1]
    y = res_ref[...] + gt_ref[...] * mo
    if final_norm:
        ms = jnp.mean(y * y, axis=-1, keepdims=True)
        y = y * lax.rsqrt(ms + NORM_EPS) * fg_ref[...]
    o_ref[...] = y
    if with_next:
        h_ref[...] = _norm_mod(y, ng_ref[...], nsh_ref[...], nsc_ref[...]).astype(h_ref.dtype)


def moe_combine(os, dest, gates, tile0, res, gt, final_g, final_norm, next_mod=None):
    bsz, seq, d = res.shape
    rows = MOE_TM
    nt = seq // rows
    n = bsz * nt
    tile = pl.BlockSpec((None, rows, d), lambda i: (i // nt, i % nt, 0))
    mspec = pl.BlockSpec((None, 1, d), lambda i: (i // nt, 0, 0))
    rspec = pl.BlockSpec((1, d), lambda i: (0, 0))
    dspec = lambda step: pl.BlockSpec((SUBLANES * rows,), lambda i: (tile0 + step(i),), memory_space=pltpu.SMEM)
    in_specs = [dspec(lambda i: i), dspec(lambda i: jnp.minimum(i + 1, n - 1)),
                pl.BlockSpec(memory_space=pl.ANY),
                pl.BlockSpec((rows, LANES), lambda i: (tile0 + i, 0)),
                tile, mspec, rspec]
    args = [dest, dest, os, gates, res, gt, final_g.reshape(1, d)]
    out_specs, out_shape = [tile], [jax.ShapeDtypeStruct((bsz, seq, d), F32)]
    if next_mod is not None:
        in_specs += [rspec, mspec, mspec]
        args += [next_mod[0].reshape(1, d), next_mod[1], next_mod[2]]
        out_specs.append(tile)
        out_shape.append(jax.ShapeDtypeStruct((bsz, seq, d), BF16))
    outs = pl.pallas_call(
        functools.partial(_combine_body, final_norm=final_norm, with_next=next_mod is not None),
        grid=(n,),
        in_specs=in_specs,
        out_specs=out_specs,
        out_shape=out_shape,
        scratch_shapes=[pltpu.VMEM((2, TOP_K, rows, d), F32), pltpu.SemaphoreType.DMA((2,))],
        compiler_params=_cparams("arbitrary"),
        name="moe_combine",
    )(*args)
    return outs if next_mod is not None else outs[0]


def _plan_body(e_ref, dest_ref, tab_ref, *, n_experts):
    n_rows, w = e_ref.shape
    e_all = e_ref[...]
    li = lax.broadcasted_iota(jnp.int32, (w, w), 0)
    lj = lax.broadcasted_iota(jnp.int32, (w, w), 1)
    incl = (li <= lj).astype(BF16)
    ri = lax.broadcasted_iota(jnp.int32, (n_rows, n_rows), 0)
    rj = lax.broadcasted_iota(jnp.int32, (n_rows, n_rows), 1)
    before = (rj < ri).astype(BF16)
    elane = lax.broadcasted_iota(jnp.int32, (n_rows, LANES), 1)
    row_tot = jnp.zeros((n_rows, LANES), F32)
    for e in range(n_experts):
        tot = jnp.sum((e_all == e).astype(F32), axis=1, keepdims=True)
        row_tot = row_tot + jnp.where(elane == e, tot, 0.0)
    rows_before = _dot(before, row_tot.astype(BF16))
    counts = jnp.sum(row_tot, axis=0, keepdims=True).astype(jnp.int32)
    lane1 = lax.broadcasted_iota(jnp.int32, (1, LANES), 1)

    def excl_prefix(v):
        acc = v
        sh = 1
        while sh < LANES:
            acc = acc + jnp.where(lane1 >= sh, pltpu.roll(acc, sh, axis=1), 0)
            sh *= 2
        return acc - v

    start = excl_prefix(counts)
    padded = (counts + (MOE_BM - 1)) // MOE_BM * MOE_BM
    pad_start = excl_prefix(padded)
    tab = jnp.concatenate([counts, start, pad_start, pad_start + padded,
                           jnp.zeros((SUBLANES - 4, LANES), jnp.int32)], axis=0)
    tab_ref[...] = tab
    base = rows_before + pad_start.astype(F32)
    dest = jnp.zeros((n_rows, w), F32)
    for e in range(n_experts):
        hit = e_all == e
        within = _dot(hit.astype(BF16), incl)
        dest = dest + jnp.where(hit, within - 1.0 + base[:, e:e + 1], 0.0)
    dest_ref[...] = dest.astype(jnp.int32)


def _route_plan(eid, n_experts):
    n_tiles, r8, tm = eid.shape
    dest, tab = pl.pallas_call(
        functools.partial(_plan_body, n_experts=n_experts),
        out_shape=[jax.ShapeDtypeStruct((n_tiles * r8, tm), jnp.int32),
                   jax.ShapeDtypeStruct((SUBLANES, LANES), jnp.int32)],
        compiler_params=pltpu.CompilerParams(vmem_limit_bytes=V7X_VMEM_LIMIT_BYTES),
        name="moe_route_plan",
    )(eid.reshape(n_tiles * r8, tm))
    counts, start, pad_start, pad_end = (tab[k, :n_experts] for k in range(4))
    e_flat = eid[:, :TOP_K, :].reshape(-1)
    a = e_flat.shape[0]
    order = jnp.argsort(e_flat)
    tok_of = (order // (TOP_K * tm)) * tm + order % tm
    slack = -a % MOE_WIN_ALIGN + MOE_WIN
    tok_sorted = jnp.concatenate([tok_of.astype(jnp.int32), jnp.zeros((slack,), jnp.int32)])
    n_blocks = -(-a // MOE_BM) + n_experts
    starts = jnp.arange(n_blocks, dtype=jnp.int32) * MOE_BM
    block_expert = jnp.minimum(jnp.sum((pad_end[None, :] <= starts[:, None]).astype(jnp.int32), axis=1),
                               n_experts - 1)
    block_valid = (starts < pad_end[-1]).astype(jnp.int32)
    block_p0 = jnp.where(block_valid > 0, start[block_expert] + starts - pad_start[block_expert], 0)
    has = (counts > 0).astype(jnp.int32)
    block_rank = (jnp.cumsum(has) - 1)[block_expert]
    present = jnp.argsort(1 - has)
    n_chunks = (MOE_NCH * jnp.sum(has)).reshape(1)
    i32 = lambda v: v.astype(jnp.int32)
    return (dest.reshape(-1), tok_sorted, i32(block_p0), block_valid, i32(block_rank), i32(present),
            i32(n_chunks))


def hier_moe(streams, norm_g, wg, bg, we, be, w_gate, w_up, w_down, layer, final_g, final_norm, next_mods=None):
    n_groups = wg.shape[1]
    n_experts = we.shape[1]
    d = wg.shape[0]
    wr = jnp.zeros((d, LANES), F32).at[:, :n_experts].set(we).at[:, n_experts:n_experts + n_groups].set(wg)
    br = jnp.zeros((1, LANES), F32).at[0, :n_experts].set(be).at[0, n_experts:n_experts + n_groups].set(bg)
    tok, eid, gates, tile0s = moe_pre([s[:3] for s in streams], norm_g, wr, br, n_groups, n_experts)
    dest, tok_sorted, block_p0, block_valid, block_rank, present, n_chunks = _route_plan(eid, n_experts)
    os = moe_experts(tok, tok_sorted, block_p0, block_valid, block_rank, present, n_chunks, w_gate, w_up, w_down,
                     layer)
    next_mods = next_mods or [None] * len(streams)
    return [moe_combine(os, dest, gates, tile0, x, gt, final_g, final_norm, nm)
            for (x, _, _, gt), tile0, nm in zip(streams, tile0s, next_mods)]


def _norm_mod_body(x_ref, g_ref, sh_ref, sc_ref, o_ref):
    o_ref[...] = _norm_mod(x_ref[...], g_ref[...], sh_ref[...], sc_ref[...]).astype(o_ref.dtype)


def norm_mod(x, g, shift, scale):
    bsz, seq, d = x.shape
    tm = min(seq, 512)
    mspec = pl.BlockSpec((None, 1, d), lambda b, i: (b, 0, 0))
    return pl.pallas_call(
        _norm_mod_body,
        grid=(bsz, seq // tm),
        in_specs=[pl.BlockSpec((None, tm, d), lambda b, i: (b, i, 0)),
                  pl.BlockSpec((1, d), lambda b, i: (0, 0)), mspec, mspec],
        out_specs=pl.BlockSpec((None, tm, d), lambda b, i: (b, i, 0)),
        out_shape=jax.ShapeDtypeStruct((bsz, seq, d), BF16),
        compiler_params=_cparams("parallel", "parallel"),
        name="norm_mod",
    )(x, g.reshape(1, d), shift, scale)


def _s5_arrange(h):
    bsz, t, d = h.shape
    c = t // (S5_SEGS * S5_TAU)
    h = h.reshape(bsz, S5_SEGS, c, S5_TAU, d // LANES, LANES)
    return h.transpose(2, 0, 1, 4, 3, 5).reshape(c * bsz * S5_SEGS, d * S5_TAU)


def _s5_unarrange(y, bsz):
    r, w = y.shape
    d = w // S5_TAU
    c = r // (bsz * S5_SEGS)
    y = y.reshape(c, bsz, S5_SEGS, d // LANES, S5_TAU, LANES)
    return y.transpose(1, 2, 0, 4, 3, 5).reshape(bsz, S5_SEGS * c * S5_TAU, d)


def _s5_operators(a_re, a_im, log_step, b_re, b_im, c_re, c_im):
    n_g, n_p = a_re.shape[1:]
    n_h = b_re.shape[-1]
    gpt = LANES // n_h
    n_j = n_g // gpt
    tau = S5_TAU
    assert tau * n_h == LANES and 2 * n_p == LANES
    lam_step = lax.complex(a_re, a_im) * jnp.exp(log_step)[..., None]
    lam_bar = jnp.exp(lam_step)
    b_bar = ((lam_bar - 1.0) / lax.complex(a_re, a_im))[..., None] * lax.complex(b_re, b_im)
    c_mat = lax.complex(c_re, c_im)
    ks = jnp.arange(tau + 1, dtype=F32)[None, :, None, None]
    pw = jnp.exp(lam_step[:, None] * ks)
    ein = functools.partial(jnp.einsum, precision=HIGHEST)
    inj_c, cl_c, lt = [], [], []
    tz_c = 0.0
    for d in range(2):
        pos = jnp.arange(tau) if d == 0 else jnp.arange(tau)[::-1]
        inj = (pw[d][tau - 1 - pos][..., None] * b_bar[d][None]).reshape(tau, n_j, gpt, n_p, n_h)
        inj = inj.transpose(1, 0, 2, 4, 3).reshape(n_j, tau * LANES, n_p)
        inj_c.append(jnp.concatenate([inj.real, inj.imag], axis=-1))
        cl = (c_mat[d][None] * pw[d][pos + 1][:, :, None, :]).reshape(tau, n_j, gpt, n_h, n_p)
        cl = cl.transpose(1, 2, 4, 0, 3).reshape(n_j, gpt * n_p, tau * n_h)
        cl_c.append(jnp.concatenate([cl.real, -cl.imag], axis=1))
        mk = ein('gop,kgp,gph->kgoh', c_mat[d], pw[d][:tau], b_bar[d]).real
        diff = pos[:, None] - pos[None, :]
        tz = jnp.where((diff >= 0)[:, :, None, None, None], mk[jnp.clip(diff, 0, tau - 1)], 0.0)
        tz = tz.reshape(tau, tau, n_j, gpt, n_h, n_h)
        tz_c = tz_c + tz.transpose(2, 1, 3, 5, 0, 4).reshape(n_j, tau * LANES, tau * n_h)
        lt_d = pw[d][tau].reshape(n_j, 1, gpt * n_p)
        lt.append(jnp.concatenate([lt_d.real, lt_d.imag], axis=-1))
    ws, wu, wh = s5_expand(jnp.stack(inj_c).astype(BF16), jnp.stack(cl_c).astype(BF16), tz_c.astype(BF16),
                           n_h, n_p)
    return ws, wu, wh, jnp.stack(lt).astype(F32)


def _s5_expand_body(inj_ref, cl_ref, tz_ref, ws_ref, wu_ref, wh_ref, *, n_h, n_p):
    rows = tz_ref.shape[0]
    gpt = LANES // n_h
    row = lax.broadcasted_iota(jnp.int32, (rows, LANES), 0)
    lane = lax.broadcasted_iota(jnp.int32, (rows, LANES), 1)
    sel_r = lax.broadcasted_iota(jnp.int32, (LANES, LANES), 0)
    sel_l = lax.broadcasted_iota(jnp.int32, (LANES, LANES), 1)
    grp_in = (row // n_h) % gpt
    grp_st = (row // n_p) % gpt

    def spread_out(m, t, grp_row):
        sel = jnp.logical_and(sel_r // n_h == t, sel_r % n_h == sel_l % n_h).astype(BF16)
        return jnp.where(grp_row == lane // n_h, _dot(m, sel), 0.0).astype(BF16)

    def spread_state(m, c, q, grp_row):
        sel = jnp.logical_and(sel_r // n_p == c, sel_r % n_p == sel_l % n_p).astype(BF16)
        return jnp.where(grp_row == (LANES // n_p) * q + lane // n_p, _dot(m, sel), 0.0).astype(BF16)

    w2 = 2 * gpt * n_p
    tz = tz_ref[...]
    for t in range(S5_TAU):
        wu_ref[:, t * LANES:(t + 1) * LANES] = spread_out(tz, t, grp_in)
    for d in range(2):
        cl = cl_ref[d]
        inj = inj_ref[d]
        for t in range(S5_TAU):
            wh_ref[d, :, t * LANES:(t + 1) * LANES] = spread_out(cl, t, grp_st)
        for c in range(2):
            for q in range(gpt * n_p // LANES):
                lo = d * w2 + c * gpt * n_p + q * LANES
                ws_ref[:, lo:lo + LANES] = spread_state(inj, c, q, grp_in)


def s5_expand(inj_c, cl_c, tz_c, n_h, n_p):
    n_j, rows, _ = tz_c.shape
    gpt = LANES // n_h
    w2 = 2 * gpt * n_p
    assert rows == S5_TAU * LANES == w2
    cspec = pl.BlockSpec((2, None, rows, LANES), lambda j: (0, j, 0, 0))
    return pl.pallas_call(
        functools.partial(_s5_expand_body, n_h=n_h, n_p=n_p),
        grid=(n_j,),
        in_specs=[cspec, cspec, pl.BlockSpec((None, rows, LANES), lambda j: (j, 0, 0))],
        out_specs=[pl.BlockSpec((None, rows, 2 * w2), lambda j: (j, 0, 0)),
                   pl.BlockSpec((None, rows, rows), lambda j: (j, 0, 0)),
                   pl.BlockSpec((2, None, w2, rows), lambda j: (0, j, 0, 0))],
        out_shape=[jax.ShapeDtypeStruct((n_j, rows, 2 * w2), BF16),
                   jax.ShapeDtypeStruct((n_j, rows, rows), BF16),
                   jax.ShapeDtypeStruct((2, n_j, w2, rows), BF16)],
        compiler_params=_cparams("parallel"),
        name="s5_expand",
    )(inj_c, cl_c, tz_c)


def _s5_inj_body(xc_ref, xl_ref, w_ref, oc_ref, ol_ref):
    w = w_ref[...]
    ol_ref[...] = _dot(xl_ref[...], w)

    @pl.when(pl.program_id(1) == 0)
    def _():
        oc_ref[...] = _dot(xc_ref[...], w)


def s5_inject(xr_c, xr_l, ws):
    r_c, r_l = xr_c.shape[0], xr_l.shape[0]
    n_j, k, n = ws.shape
    tm = r_l // 2 if r_l % 32 == 0 else r_l
    return pl.pallas_call(
        _s5_inj_body,
        grid=(n_j, r_l // tm),
        in_specs=[pl.BlockSpec((r_c, k), lambda j, i: (0, j)),
                  pl.BlockSpec((tm, k), lambda j, i: (i, j)),
                  pl.BlockSpec((None, k, n), lambda j, i: (j, 0, 0))],
        out_specs=[pl.BlockSpec((r_c, n), lambda j, i: (0, j)),
                   pl.BlockSpec((tm, n), lambda j, i: (i, j))],
        out_shape=[jax.ShapeDtypeStruct((r_c, n_j * n), F32), jax.ShapeDtypeStruct((r_l, n_j * n), F32)],
        compiler_params=_cparams("parallel", "arbitrary"),
        name="s5_inject",
    )(xr_c, xr_l, ws)


def _cmul(ar, ai, br, bi):
    return ar * br - ai * bi, ar * bi + ai * br


def _s5_scan_body(sc_ref, sl_ref, lt_ref, h_ref, raw_ref, *, n_ctx, n_lat, bsz):
    d = pl.program_id(1)
    w2 = lt_ref.shape[-1]
    w = w2 // 2
    rows = bsz * S5_SEGS
    seg = lax.broadcasted_iota(jnp.int32, (rows, 1), 0) % S5_SEGS
    is_late = seg != d
    lam_r = lt_ref[:, 0:w]
    lam_i = lt_ref[:, w:w2]
    zero = jnp.zeros((rows, w), F32)
    one = (jnp.ones((1, w), F32), jnp.zeros((1, w), F32))

    def swap_segments(x):
        return jnp.where(seg == 0, pltpu.roll(x, rows - 1, axis=0), pltpu.roll(x, 1, axis=0))

    def phase(s_ref, n_steps, hin_r, hin_i, write):
        def chunk(k):
            return jnp.where(d == 0, k, n_steps - 1 - k)

        def step_raw(k, carry):
            hr, hi = carry
            c = chunk(k)
            raw_ref[c, :, 0:w] = hr
            raw_ref[c, :, w:w2] = hi
            nr, ni = _cmul(lam_r, lam_i, hr, hi)
            return nr + s_ref[c, :, 0:w], ni + s_ref[c, :, w:w2]

        er, ei = lax.fori_loop(0, n_steps, step_raw, (zero, zero), unroll=S5_SCAN_UNROLL)
        pr, pi = lax.fori_loop(0, n_steps, lambda k, q: _cmul(lam_r, lam_i, *q), one)
        dr, di = _cmul(pr, pi, hin_r, hin_i)
        first_r = jnp.where(is_late, 0.0, er + dr)
        first_i = jnp.where(is_late, 0.0, ei + di)
        carry_r = jnp.where(is_late, swap_segments(first_r), hin_r)
        carry_i = jnp.where(is_late, swap_segments(first_i), hin_i)
        if write:
            def step_fix(k, q):
                c = chunk(k)
                fr, fi = _cmul(q[0], q[1], carry_r, carry_i)
                h_ref[c, :, 0:w] = (raw_ref[c, :, 0:w] + fr).astype(h_ref.dtype)
                h_ref[c, :, w:w2] = (raw_ref[c, :, w:w2] + fi).astype(h_ref.dtype)
                return _cmul(lam_r, lam_i, q[0], q[1])

            lax.fori_loop(0, n_steps, step_fix, one, unroll=S5_SCAN_UNROLL)
        lr, li = _cmul(pr, pi, carry_r, carry_i)
        last_r = jnp.where(is_late, er + lr, 0.0)
        last_i = jnp.where(is_late, ei + li, 0.0)
        return (jnp.where(is_late, 0.0, swap_segments(last_r)), jnp.where(is_late, 0.0, swap_segments(last_i)))

    hr, hi = phase(sc_ref, n_ctx, zero, zero, False)
    phase(sl_ref, n_lat, hr, hi, True)


def s5_scan(s_ctx, s_lat, lt, bsz):
    assert S5_SEGS == 2
    n_ctx, rows, _ = s_ctx.shape
    n_lat = s_lat.shape[0]
    n_j = lt.shape[1]
    w2 = lt.shape[-1]
    return pl.pallas_call(
        functools.partial(_s5_scan_body, n_ctx=n_ctx, n_lat=n_lat, bsz=bsz),
        grid=(n_j, 2),
        in_specs=[pl.BlockSpec((n_ctx, rows, w2), lambda j, d: (0, 0, 2 * j + d)),
                  pl.BlockSpec((n_lat, rows, w2), lambda j, d: (0, 0, 2 * j + d)),
                  pl.BlockSpec((None, None, 1, w2), lambda j, d: (d, j, 0, 0))],
        out_specs=pl.BlockSpec((None, n_lat, rows, w2), lambda j, d: (d, 0, 0, j)),
        out_shape=jax.ShapeDtypeStruct((2, n_lat, rows, n_j * w2), BF16),
        scratch_shapes=[pltpu.VMEM((max(n_ctx, n_lat), rows, w2), F32)],
        compiler_params=_cparams("parallel", "parallel"),
        name="s5_scan",
    )(s_ctx, s_lat, lt)


def _s5_out_body(x_ref, hf_ref, hb_ref, wu_ref, whf_ref, whb_ref, o_ref):
    o_ref[...] = (_dot(x_ref[...], wu_ref[...]) + _dot(hf_ref[...], whf_ref[...])
                  + _dot(hb_ref[...], whb_ref[...])).astype(o_ref.dtype)


def s5_readout(xr, h, wu, wh):
    r = xr.shape[0]
    n_j, k, n = wu.shape
    w2 = wh.shape[2]
    tm = min(r, 1024)
    return pl.pallas_call(
        _s5_out_body,
        grid=(n_j, r // tm),
        in_specs=[pl.BlockSpec((tm, k), lambda j, i: (i, j)),
                  pl.BlockSpec((None, tm, w2), lambda j, i: (0, i, j)),
                  pl.BlockSpec((None, tm, w2), lambda j, i: (1, i, j)),
                  pl.BlockSpec((None, k, n), lambda j, i: (j, 0, 0)),
                  pl.BlockSpec((None, None, w2, n), lambda j, i: (0, j, 0, 0)),
                  pl.BlockSpec((None, None, w2, n), lambda j, i: (1, j, 0, 0))],
        out_specs=pl.BlockSpec((tm, n), lambda j, i: (i, j)),
        out_shape=jax.ShapeDtypeStruct((r, n_j * n), BF16),
        compiler_params=_cparams("parallel", "parallel"),
        name="s5_readout",
    )(xr, h, h, wu, wh, wh)


def _gelu_tanh(x):
    return 0.5 * x * (1.0 + jnp.tanh(math.sqrt(2.0 / math.pi) * (x + 0.044715 * (x * x * x))))


def _s5_glu_body(x_ref, y_ref, g_ref, sh_ref, sc_ref, dk_ref, w1_ref, w2_ref, b1_ref, b2_ref, gt_ref, o_ref):
    x = x_ref[...]
    u = _norm_mod(x, g_ref[...], sh_ref[...], sc_ref[...])
    y = _gelu_tanh(y_ref[...].astype(F32) + dk_ref[...] * u).astype(BF16)
    o = (_dot(y, w1_ref[...]) + b1_ref[...]) * jax.nn.sigmoid(_dot(y, w2_ref[...]) + b2_ref[...])
    o_ref[...] = x + gt_ref[...] * o


def s5_glu(x, y, g, shift, scale, d_skip, w1, b1, w2, b2, gate):
    bsz, seq, d = x.shape
    tm = min(seq, 512)
    row = lambda a: a.reshape(1, d)
    rspec = pl.BlockSpec((1, d), lambda b, i: (0, 0))
    mspec = pl.BlockSpec((None, 1, d), lambda b, i: (b, 0, 0))
    tile = pl.BlockSpec((None, tm, d), lambda b, i: (b, i, 0))
    wspec = pl.BlockSpec((d, d), lambda b, i: (0, 0), pipeline_mode=pl.Buffered(1))
    return pl.pallas_call(
        _s5_glu_body,
        grid=(bsz, seq // tm),
        in_specs=[tile, tile, rspec, mspec, mspec, rspec, wspec, wspec, rspec, rspec, mspec],
        out_specs=tile,
        out_shape=jax.ShapeDtypeStruct((bsz, seq, d), F32),
        compiler_params=_cparams("parallel", "parallel"),
        name="s5_glu",
    )(x, y, row(g), shift, scale, row(d_skip), w1, w2, row(b1), row(b2), gate)


def s5_mix(xl, hl, hc, g, sh_l, sc_l, gate_l, a_re, a_im, log_step, b_re, b_im, c_re, c_im, d_skip,
           w1, b1, w2, b2):
    bsz, seq, d = xl.shape
    xr_c = _s5_arrange(hc)
    xr_l = _s5_arrange(hl)
    rows = bsz * S5_SEGS
    n_ctx = xr_c.shape[0] // rows
    n_lat = xr_l.shape[0] // rows
    ws, wu, wh, lt = _s5_operators(a_re, a_im, log_step, b_re, b_im, c_re, c_im)
    s_ctx, s_lat = s5_inject(xr_c, xr_l, ws)
    h = s5_scan(s_ctx.reshape(n_ctx, rows, -1), s_lat.reshape(n_lat, rows, -1), lt, bsz)
    y = s5_readout(xr_l, h.reshape(2, n_lat * rows, -1), wu, wh)
    y = _s5_unarrange(y, bsz)
    return s5_glu(xl, y, g, sh_l, sc_l, d_skip, w1.astype(BF16), b1, w2.astype(BF16), b2, gate_l)


def hyena_mix(x, g, shift, scale, gate, w_in, b_in, conv_w, conv_b, fw1, fb1, fw2, fb2, fw3, freq, skip,
              w_out, b_out):
    seq = x.shape[1]
    if seq >= FFT_MIN_SEQ and (2 * seq) % (2 * FFT_N2) == 0:
        a, dd, _ = hyena_filter_taps(seq, fw1, fb1, fw2, fb2, fw3, freq, F32)
        v, x0 = hyena_in(x, g, shift, scale, w_in, b_in, conv_w, conv_b, F32)
        yg = hyena_conv_fft(v, x0, skip, a, dd)
    else:
        cmat, smat = dft_matrices(seq)
        a, dd, kn = hyena_filter_taps(seq, fw1, fb1, fw2, fb2, fw3, freq, BF16)
        kr, ki = hyena_filter_dft(a, dd, cmat, smat)
        v, x0 = hyena_in(x, g, shift, scale, w_in, b_in, conv_w, conv_b, BF16)
        yg = hyena_conv(v, x0, skip, kr, ki, kn, cmat, smat)
    return mm_residual(yg, w_out, b_out, x, gate)


def kernel(x, c, ctx, c_ctx, ada_w, ada_b, norm_g, final_g, hy_w_in, hy_b_in, hy_conv_w, hy_conv_b, hy_fw1,
           hy_fb1, hy_fw2, hy_fb2, hy_fw3, hy_freq, hy_skip, hy_w_out, hy_b_out, s5_a_re, s5_a_im,
           s5_log_step, s5_b_re, s5_b_im, s5_c_re, s5_c_im, s5_d, s5_w1, s5_b1, s5_w2, s5_b2, moe_wg, moe_bg,
           moe_we, moe_be, moe_w_gate, moe_w_up, moe_w_down):
    bsz, _, d = x.shape
    depth = ada_w.shape[0]
    assert depth == 2 and bsz < SUBLANES
    c_all = jnp.zeros((SUBLANES, d), F32).at[:bsz].set(c).at[bsz].set(c_ctx)
    mods = ada_mod(c_all, ada_w, ada_b)

    def mod_rows(layer, k):
        lat = mods[layer, :bsz, k * d:(k + 1) * d][:, None, :]
        cx = jnp.broadcast_to(mods[layer, bsz, k * d:(k + 1) * d][None, None, :], (bsz, 1, d))
        return lat, cx

    (sh_a, csh_a), (sc_a, csc_a), (gt_a, cgt_a) = mod_rows(0, 0), mod_rows(0, 1), mod_rows(0, 2)
    (sh_f, csh_f), (sc_f, csc_f), (gt_f, cgt_f) = mod_rows(0, 3), mod_rows(0, 4), mod_rows(0, 5)
    hy = (hy_w_in[0].astype(BF16), hy_b_in[0], hy_conv_w[0], hy_conv_b[0], hy_fw1[0], hy_fb1[0], hy_fw2[0],
          hy_fb2[0], hy_fw3[0], hy_freq[0], hy_skip[0], hy_w_out[0].astype(BF16), hy_b_out[0])
    xl = hyena_mix(x, norm_g[0, 0], sh_a, sc_a, gt_a, *hy)
    xc = hyena_mix(ctx, norm_g[0, 0], csh_a, csc_a, cgt_a, *hy)
    (sh_a, csh_a), (sc_a, csc_a), (gt_a, _) = mod_rows(1, 0), mod_rows(1, 1), mod_rows(1, 2)
    (xl, hl), (_, hc) = hier_moe([(xl, sh_f, sc_f, gt_f), (xc, csh_f, csc_f, cgt_f)], norm_g[0, 1],
                                 moe_wg[0], moe_bg[0], moe_we[0], moe_be[0], moe_w_gate, moe_w_up, moe_w_down, 0,
                                 final_g, False,
                                 next_mods=[(norm_g[1, 0], sh_a, sc_a), (norm_g[1, 0], csh_a, csc_a)])
    (sh_f, _), (sc_f, _), (gt_f, _) = mod_rows(1, 3), mod_rows(1, 4), mod_rows(1, 5)
    xl = s5_mix(xl, hl, hc, norm_g[1, 0], sh_a, sc_a, gt_a, s5_a_re[0], s5_a_im[0], s5_log_step[0],
                s5_b_re[0], s5_b_im[0], s5_c_re[0], s5_c_im[0], s5_d[0], s5_w1[0], s5_b1[0], s5_w2[0], s5_b2[0])
    (out,) = hier_moe([(xl, sh_f, sc_f, gt_f)], norm_g[1, 1], moe_wg[1], moe_bg[1], moe_we[1], moe_be[1],
                      moe_w_gate, moe_w_up, moe_w_down, 1, final_g, True)
    return out
```

```python
import functools
import math

import numpy as np
import jax
import jax.numpy as jnp
from jax import lax
from jax.experimental import pallas as pl
from jax.experimental.pallas import tpu as pltpu

F32 = jnp.float32
BF16 = jnp.bfloat16
HIGHEST = lax.Precision.HIGHEST

NORM_EPS = 1e-6
HY_DECAY_TARGET = 1e-2
HY_FAST_PCT = 0.3
HY_SLOW_PCT = 1.5
TOP_K = 2

V7X_VMEM_LIMIT_BYTES = 56 * 1024 * 1024
LANES = 128
SUBLANES = 8
S5_TAU = 8
S5_SEGS = 2
S5_SCAN_UNROLL = 8
MOE_TM = 256
MOE_BM = 256
MOE_NCH = 12
MOE_STAGE = 4
MOE_CAST_ROWS = 128
MOE_WIN_ALIGN = 1024
MOE_WIN = 2 * MOE_WIN_ALIGN
NEG_BIG = -1e30


def _cparams(*sem):
    return pltpu.CompilerParams(dimension_semantics=sem, vmem_limit_bytes=V7X_VMEM_LIMIT_BYTES)


def _norm_mod(x, g, shift, scale):
    ms = jnp.mean(x * x, axis=-1, keepdims=True)
    return (x * lax.rsqrt(ms + NORM_EPS) * g) * (1.0 + scale) + shift


def _dot(a, b):
    return jnp.dot(a, b, preferred_element_type=F32)


def _ada_body(c_ref, w_ref, b_ref, o_ref):
    x = c_ref[...]
    s = (x * jax.nn.sigmoid(x)).astype(BF16)
    o_ref[...] = _dot(s, w_ref[...].astype(BF16)) + b_ref[...]


def ada_mod(c_all, ada_w, ada_b):
    depth, d, n = ada_w.shape
    tn = min(n, 1024)
    return pl.pallas_call(
        _ada_body,
        grid=(depth, n // tn),
        in_specs=[pl.BlockSpec((SUBLANES, d), lambda l, j: (0, 0)),
                  pl.BlockSpec((None, d, tn), lambda l, j: (l, 0, j)),
                  pl.BlockSpec((None, 1, tn), lambda l, j: (l, 0, j))],
        out_specs=pl.BlockSpec((None, SUBLANES, tn), lambda l, j: (l, 0, j)),
        out_shape=jax.ShapeDtypeStruct((depth, SUBLANES, n), F32),
        compiler_params=_cparams("parallel", "parallel"),
        name="ada_mod",
    )(c_all, ada_w, ada_b.reshape(depth, 1, n))


def _hy_in_body(xp_ref, xm_ref, xn_ref, g_ref, sh_ref, sc_ref,
                w0_ref, w1_ref, w2_ref, b0_ref, b1_ref, b2_ref,
                cw0_ref, cw1_ref, cw2_ref, cb0_ref, cb1_ref, cb2_ref,
                v_ref, x0_ref):
    i = pl.program_id(2)
    ni = pl.num_programs(2)
    tm = xm_ref.shape[0]
    x = jnp.concatenate([xp_ref[...], xm_ref[...], xn_ref[...]], axis=0)
    h = _norm_mod(x, g_ref[...], sh_ref[...], sc_ref[...]).astype(BF16)
    rows = lax.broadcasted_iota(jnp.int32, (tm + 2 * SUBLANES, 1), 0)
    valid = jnp.logical_and(jnp.logical_or(rows >= SUBLANES, i > 0),
                            jnp.logical_or(rows < tm + SUBLANES, i < ni - 1))

    def part(w_ref, b_ref, cw_ref, cb_ref):
        z = jnp.where(valid, _dot(h, w_ref[...]) + b_ref[...], 0.0)
        cw = cw_ref[...]
        zp = pltpu.roll(z, 1, axis=0)[SUBLANES:tm + SUBLANES]
        zn = pltpu.roll(z, tm + 2 * SUBLANES - 1, axis=0)[SUBLANES:tm + SUBLANES]
        return zp * cw[0:1] + z[SUBLANES:tm + SUBLANES] * cw[1:2] + zn * cw[2:3] + cb_ref[...]

    x0 = part(w0_ref, b0_ref, cw0_ref, cb0_ref)
    x1 = part(w1_ref, b1_ref, cw1_ref, cb1_ref)
    v = part(w2_ref, b2_ref, cw2_ref, cb2_ref) * x1
    v_ref[...] = v.astype(v_ref.dtype)
    x0_ref[...] = x0.astype(BF16)


def hyena_in(x, g, shift, scale, w_in, b_in, conv_w, conv_b, v_dtype):
    bsz, seq, d = x.shape
    tm = min(seq, 512)
    tn = min(d, 1024)
    nj = d // tn
    r8 = tm // SUBLANES
    last8 = seq // SUBLANES - 1
    row = lambda a: a.reshape(1, -1)
    wspec = lambda k: pl.BlockSpec((d, tn), lambda j, b, i: (0, k * nj + j))
    rspec = lambda k: pl.BlockSpec((1, tn), lambda j, b, i: (0, k * nj + j))
    cspec = lambda k: pl.BlockSpec((3, tn), lambda j, b, i: (0, k * nj + j))
    mspec = pl.BlockSpec((None, 1, d), lambda j, b, i: (b, 0, 0))
    out_spec = pl.BlockSpec((None, tm, tn), lambda j, b, i: (b, i, j))
    return pl.pallas_call(
        _hy_in_body,
        grid=(nj, bsz, seq // tm),
        in_specs=[pl.BlockSpec((None, SUBLANES, d), lambda j, b, i: (b, jnp.maximum(i * r8 - 1, 0), 0)),
                  pl.BlockSpec((None, tm, d), lambda j, b, i: (b, i, 0)),
                  pl.BlockSpec((None, SUBLANES, d), lambda j, b, i: (b, jnp.minimum((i + 1) * r8, last8), 0)),
                  pl.BlockSpec((1, d), lambda j, b, i: (0, 0)), mspec, mspec,
                  wspec(0), wspec(1), wspec(2), rspec(0), rspec(1), rspec(2),
                  cspec(0), cspec(1), cspec(2), rspec(0), rspec(1), rspec(2)],
        out_specs=[out_spec, out_spec],
        out_shape=[jax.ShapeDtypeStruct((bsz, seq, d), v_dtype), jax.ShapeDtypeStruct((bsz, seq, d), BF16)],
        compiler_params=_cparams("parallel", "parallel", "parallel"),
        name="hyena_in",
    )(x, x, x, row(g), shift, scale, w_in, w_in, w_in, row(b_in), row(b_in), row(b_in),
      conv_w, conv_w, conv_w, row(conv_b), row(conv_b), row(conv_b))


def _dft_tables(seq, blk):
    n = 2 * seq
    s = np.arange(seq, dtype=np.int64)[None, :]
    fl = np.arange(blk, dtype=np.int64)[:, None]
    fh = (np.arange(seq // blk, dtype=np.int64) * blk)[:, None]
    w = 2.0 * math.pi / n
    ang_b = ((fl * s) % n) * w
    ang_a = ((fh * s) % n) * w
    f32 = lambda m: jnp.asarray(m.astype(np.float32))
    return (f32(np.cos(ang_a)[:, None, :]), f32(np.sin(ang_a)[:, None, :]), f32(np.cos(ang_b)), f32(np.sin(ang_b)))


def _dft_gen_body(ca_ref, sa_ref, cb_ref, sb_ref, c_ref, s_ref):
    ca, sa, cb, sb = ca_ref[...], sa_ref[...], cb_ref[...], sb_ref[...]
    c_ref[...] = (ca * cb - sa * sb).astype(BF16)
    s_ref[...] = (sa * cb + ca * sb).astype(BF16)


def dft_matrices(seq):
    blk = min(seq, 256)
    ca, sa, cb, sb = _dft_tables(seq, blk)
    aspec = pl.BlockSpec((None, 1, seq), lambda i: (i, 0, 0))
    bspec = pl.BlockSpec((blk, seq), lambda i: (0, 0))
    ospec = pl.BlockSpec((blk, seq), lambda i: (i, 0))
    return pl.pallas_call(
        _dft_gen_body,
        grid=(seq // blk,),
        in_specs=[aspec, aspec, bspec, bspec],
        out_specs=[ospec, ospec],
        out_shape=[jax.ShapeDtypeStruct((seq, seq), BF16)] * 2,
        compiler_params=_cparams("parallel"),
        name="dft_matrices",
    )(ca, sa, cb, sb)


def _alt_sign(rows):
    return jnp.where((rows & 1) == 0, 1.0, -1.0).astype(F32)


def _filt_body(h2_ref, wf_ref, wb_ref, dl_ref, a_ref, d_ref, ny_ref):
    seq = h2_ref.shape[0]
    h2 = h2_ref[...]
    row = lax.broadcasted_iota(jnp.int32, (seq, 1), 0)
    t = row.astype(F32) * (1.0 / (seq - 1))
    win = jnp.exp(-t * dl_ref[...])
    hf = jnp.dot(h2, wf_ref[...], precision=HIGHEST, preferred_element_type=F32) * win
    hb = jnp.dot(h2, wb_ref[...], precision=HIGHEST, preferred_element_type=F32) * win
    hb = jnp.where(row == 0, 0.0, hb)
    nrm = (jnp.sum(jnp.abs(hf), axis=0, keepdims=True) + jnp.sum(jnp.abs(hb), axis=0, keepdims=True))
    inv = 1.0 / nrm
    a = (hf + hb) * inv
    a_ref[...] = a.astype(a_ref.dtype)
    d_ref[...] = ((hb - hf) * inv).astype(d_ref.dtype)
    ny = jnp.sum(a * _alt_sign(row), axis=0, keepdims=True) * (1.0 / (2 * seq))
    ny_ref[...] = jnp.broadcast_to(ny, ny_ref.shape)


def _khat_body(a_ref, d_ref, c_ref, s_ref, kr_ref, ki_ref):
    i = pl.program_id(1)
    tm = c_ref.shape[0]
    seq = c_ref.shape[1]
    f = i * tm + lax.broadcasted_iota(jnp.int32, (tm, 1), 0)
    w = jnp.where(f == 0, 1.0, 2.0).astype(F32) * (1.0 / (2 * seq))
    kr_ref[...] = _dot(c_ref[...], a_ref[...]) * w
    ki_ref[...] = _dot(s_ref[...], d_ref[...]) * w


def hyena_filter_taps(seq, fw1, fb1, fw2, fb2, fw3, freq, taps_dtype):
    d = fw3.shape[1] // 2
    bands_n = (fw1.shape[0] - 1) // 2
    t = np.linspace(0.0, 1.0, seq)[:, None]
    w = (2.0 * math.pi / seq) * np.arange(seq)[:, None]
    bands = np.linspace(1e-4, bands_n - 1, bands_n)[None, :]
    z = jnp.asarray(np.concatenate([t, np.cos(bands * w), -np.sin(bands * w)], axis=-1).astype(np.float32))
    h = jnp.sin(freq * (jnp.dot(z, fw1, precision=HIGHEST) + fb1))
    h2 = jnp.sin(freq * (jnp.dot(h, fw2, precision=HIGHEST) + fb2))
    max_decay = math.log(HY_DECAY_TARGET) / HY_FAST_PCT
    min_decay = math.log(HY_DECAY_TARGET) / HY_SLOW_PCT
    deltas = jnp.abs(jnp.linspace(min_decay, max_decay, d, dtype=F32))[None, :]

    order = h2.shape[1]
    tn = min(d, 256)
    nj = d // tn
    return pl.pallas_call(
        _filt_body,
        grid=(nj,),
        in_specs=[pl.BlockSpec((seq, order), lambda j: (0, 0)),
                  pl.BlockSpec((order, tn), lambda j: (0, j)),
                  pl.BlockSpec((order, tn), lambda j: (0, nj + j)),
                  pl.BlockSpec((1, tn), lambda j: (0, j))],
        out_specs=[pl.BlockSpec((seq, tn), lambda j: (0, j)),
                   pl.BlockSpec((seq, tn), lambda j: (0, j)),
                   pl.BlockSpec((SUBLANES, tn), lambda j: (0, j))],
        out_shape=[jax.ShapeDtypeStruct((seq, d), taps_dtype), jax.ShapeDtypeStruct((seq, d), taps_dtype),
                   jax.ShapeDtypeStruct((SUBLANES, d), F32)],
        compiler_params=_cparams("parallel"),
        name="hyena_filter_taps",
    )(h2, fw3, fw3, deltas)


def hyena_filter_dft(a, dd, cmat, smat):
    seq, d = a.shape
    tm = min(seq, 512)
    tn2 = min(d, 512)
    return pl.pallas_call(
        _khat_body,
        grid=(d // tn2, seq // tm),
        in_specs=[pl.BlockSpec((seq, tn2), lambda j, i: (0, j)),
                  pl.BlockSpec((seq, tn2), lambda j, i: (0, j)),
                  pl.BlockSpec((tm, seq), lambda j, i: (i, 0)),
                  pl.BlockSpec((tm, seq), lambda j, i: (i, 0))],
        out_specs=[pl.BlockSpec((tm, tn2), lambda j, i: (i, j))] * 2,
        out_shape=[jax.ShapeDtypeStruct((seq, d), F32)] * 2,
        compiler_params=_cparams("parallel", "parallel"),
        name="hyena_filter_dft",
    )(a, dd, cmat, smat)


def _dft_fwd_body(v_ref, c_ref, s_ref, kr_ref, ki_ref, kn_ref, ya_ref, yb_ref, yn_ref):
    i = pl.program_id(2)
    v = v_ref[...]
    vr = _dot(c_ref[...], v)
    p = _dot(s_ref[...], v)
    kr = kr_ref[...]
    ki = ki_ref[...]
    ya_ref[...] = (vr * kr + p * ki).astype(BF16)
    yb_ref[...] = (p * kr - vr * ki).astype(BF16)

    @pl.when(i == 0)
    def _():
        seq = v.shape[0]
        row = lax.broadcasted_iota(jnp.int32, (seq, 1), 0)
        vl = jnp.sum(v.astype(F32) * _alt_sign(row), axis=0, keepdims=True)
        yn_ref[...] = jnp.broadcast_to(vl * kn_ref[0:1, :], yn_ref.shape)


def _dft_inv_body(ya_ref, yb_ref, c_ref, s_ref, v_ref, x0_ref, skip_ref, yn_ref, o_ref):
    i = pl.program_id(2)
    tm = c_ref.shape[0]
    acc = _dot(c_ref[...], ya_ref[...]) + _dot(s_ref[...], yb_ref[...])
    t = i * tm + lax.broadcasted_iota(jnp.int32, (tm, 1), 0)
    y = acc + _alt_sign(t) * yn_ref[0:1, :] + skip_ref[...] * v_ref[...].astype(F32)
    o_ref[...] = (y * x0_ref[...].astype(F32)).astype(BF16)


def hyena_conv(v, x0, skip, kr, ki, kn, cmat, smat):
    bsz, seq, d = v.shape
    tm = min(seq, 512)
    tn = min(d, 512)
    grid = (bsz, d // tn, seq // tm)
    full = pl.BlockSpec((None, seq, tn), lambda b, j, i: (b, 0, j))
    mat = pl.BlockSpec((tm, seq), lambda b, j, i: (i, 0))
    tile = pl.BlockSpec((None, tm, tn), lambda b, j, i: (b, i, j))
    ktile = pl.BlockSpec((tm, tn), lambda b, j, i: (i, j))
    nyq = pl.BlockSpec((None, SUBLANES, tn), lambda b, j, i: (b, 0, j))
    ya, yb, yn = pl.pallas_call(
        _dft_fwd_body,
        grid=grid,
        in_specs=[full, mat, mat, ktile, ktile, pl.BlockSpec((SUBLANES, tn), lambda b, j, i: (0, j))],
        out_specs=[tile, tile, nyq],
        out_shape=[jax.ShapeDtypeStruct((bsz, seq, d), BF16)] * 2
        + [jax.ShapeDtypeStruct((bsz, SUBLANES, d), F32)],
        compiler_params=_cparams("parallel", "parallel", "arbitrary"),
        name="hyena_dft_fwd",
    )(v, cmat, smat, kr, ki, kn)
    return pl.pallas_call(
        _dft_inv_body,
        grid=grid,
        in_specs=[full, full, mat, mat, tile, tile, pl.BlockSpec((1, tn), lambda b, j, i: (0, j)), nyq],
        out_specs=tile,
        out_shape=jax.ShapeDtypeStruct((bsz, seq, d), BF16),
        compiler_params=_cparams("parallel", "parallel", "parallel"),
        name="hyena_dft_inv",
    )(ya, yb, cmat, smat, v, x0, skip.reshape(1, d), yn)


FFT_N2 = 128
FFT_MIN_SEQ = 1024
FFT_UNROLL = 8


def _fft_matrices(seq):
    n = 2 * seq
    n2 = FFT_N2
    n1 = n // n2
    r8 = SUBLANES
    q = np.arange(n2 // r8, dtype=np.int64)[:, None, None, None]
    f1 = np.arange(n1, dtype=np.int64)[None, :, None, None]
    r = np.arange(r8, dtype=np.int64)[None, None, :, None]
    t1 = np.arange(n1 // 2, dtype=np.int64)[None, None, None, :]
    ang = ((f1 * (t1 * n2 + q * r8 + r)) % n) * (2.0 * math.pi / n)
    g = np.stack([np.cos(ang), -np.sin(ang)], axis=3)
    eye = np.eye(r8)[None, None, :, None, None, :]
    ma = (g[..., None] * eye).reshape(n2 // r8, n1 * r8 * 2, (n1 // 2) * r8).astype(np.float32)
    f2 = np.arange(n2, dtype=np.int64)[:, None]
    t2 = np.arange(n2, dtype=np.int64)[None, :]
    th = ((f2 * t2) % n2) * (2.0 * math.pi / n2)
    co, si = np.cos(th), np.sin(th)
    wc = np.stack([np.stack([co, si], axis=-1), np.stack([-si, co], axis=-1)], axis=0)
    wc = wc.reshape(2 * n2, 2 * n2).astype(np.float32)
    as_bf16 = lambda m: jnp.asarray(np.ascontiguousarray(m).astype(BF16))
    return as_bf16(ma), as_bf16(np.swapaxes(ma, 1, 2)), as_bf16(wc), as_bf16(wc.T)


def _fft_stage_a(x_ref, ma_ref, s1):
    n1h, n_q, r8, tn = x_ref.shape
    n1 = s1.shape[0]

    def body(q, carry):
        x = x_ref[:, pl.ds(q, 1), :, :].reshape(n1h * r8, tn).astype(BF16)
        a = _dot(ma_ref[q], x).astype(BF16)
        s1[:, pl.ds(pl.multiple_of(q * 2 * r8, 2 * r8), 2 * r8), :] = a.reshape(n1, 2 * r8, tn)
        return carry

    lax.fori_loop(0, n_q, body, 0, unroll=FFT_UNROLL)


def _fft_conv_body(v_ref, x0_ref, k_ref, skip_ref, ma_ref, mat_ref, wc_ref, wci_ref, o_ref, s1, ysc):
    n1h, n_q, r8, tn = v_ref.shape
    n1 = s1.shape[0]
    n2 = s1.shape[1] // 2
    seq = n1h * n_q * r8
    _fft_stage_a(v_ref, ma_ref, s1)

    def slab(f, carry):
        y = _dot(wc_ref[...], s1[f])
        yr, yi = y[:n2], y[n2:]
        kr = k_ref[f, 0].astype(F32)
        ki = k_ref[f, 1].astype(F32)
        p = jnp.concatenate([yr * kr - yi * ki, yr * ki + yi * kr], axis=0).astype(BF16)
        s1[f] = _dot(wci_ref[...], p).astype(BF16)
        return carry

    lax.fori_loop(0, n1, slab, 0, unroll=2 * FFT_UNROLL)

    def inv_a(q, carry):
        z = s1[:, pl.ds(pl.multiple_of(q * 2 * r8, 2 * r8), 2 * r8), :].reshape(n1 * 2 * r8, tn)
        ysc[:, pl.ds(q, 1), :, :] = _dot(mat_ref[q], z).reshape(n1h, 1, r8, tn)
        return carry

    lax.fori_loop(0, n_q, inv_a, 0, unroll=FFT_UNROLL)
    y = ysc[...].reshape(seq, tn) + skip_ref[...] * v_ref[...].reshape(seq, tn)
    o_ref[...] = (y * x0_ref[...].astype(F32)).astype(BF16)


def _fft_filter_body(a_ref, d_ref, ma_ref, wc_ref, k_ref, s1):
    n1 = s1.shape[0]
    n2 = s1.shape[1] // 2
    scale = 1.0 / (n1 * n2)
    for src_ref, part, sign in ((a_ref, 0, scale), (d_ref, 1, -scale)):
        _fft_stage_a(src_ref, ma_ref, s1)

        def slab(f, carry):
            y = _dot(wc_ref[part * n2:(part + 1) * n2, :], s1[f])
            k_ref[f, part] = (y * sign).astype(BF16)
            return carry

        lax.fori_loop(0, n1, slab, 0, unroll=FFT_UNROLL)


def hyena_conv_fft(v, x0, skip, a, dd):
    bsz, seq, d = v.shape
    n2 = FFT_N2
    n1 = 2 * seq // n2
    n_q = n2 // SUBLANES
    tn = min(d, 256)
    ma, mat, wc, wci = _fft_matrices(seq)
    const = lambda shape: pl.BlockSpec(shape, lambda *_: (0,) * len(shape), pipeline_mode=pl.Buffered(1))
    view = lambda t: t.reshape(t.shape[:-2] + (n1 // 2, n_q, SUBLANES, d))
    tap = pl.BlockSpec((n1 // 2, n_q, SUBLANES, tn), lambda j: (0, 0, 0, j))
    khat = pl.pallas_call(
        _fft_filter_body,
        grid=(d // tn,),
        in_specs=[tap, tap, const(ma.shape), const(wc.shape)],
        out_specs=pl.BlockSpec((n1, 2, n2, tn), lambda j: (0, 0, 0, j)),
        out_shape=jax.ShapeDtypeStruct((n1, 2, n2, d), BF16),
        scratch_shapes=[pltpu.VMEM((n1, 2 * n2, tn), BF16)],
        compiler_params=_cparams("parallel"),
        name="hyena_filter_fft",
    )(view(a), view(dd), ma, wc)
    return pl.pallas_call(
        _fft_conv_body,
        grid=(d // tn, bsz),
        in_specs=[pl.BlockSpec((None, n1 // 2, n_q, SUBLANES, tn), lambda j, b: (b, 0, 0, 0, j)),
                  pl.BlockSpec((None, seq, tn), lambda j, b: (b, 0, j)),
                  pl.BlockSpec((n1, 2, n2, tn), lambda j, b: (0, 0, 0, j), pipeline_mode=pl.Buffered(1)),
                  pl.BlockSpec((1, tn), lambda j, b: (0, j)),
                  const(ma.shape), const(mat.shape), const(wc.shape), const(wci.shape)],
        out_specs=pl.BlockSpec((None, seq, tn), lambda j, b: (b, 0, j)),
        out_shape=jax.ShapeDtypeStruct((bsz, seq, d), BF16),
        scratch_shapes=[pltpu.VMEM((n1, 2 * n2, tn), BF16), pltpu.VMEM((n1 // 2, n_q, SUBLANES, tn), F32)],
        compiler_params=_cparams("parallel", "arbitrary"),
        name="hyena_conv_fft",
    )(view(v), x0, khat, skip.reshape(1, d), ma, mat, wc, wci)


def _mm_res_body(x_ref, w_ref, b_ref, res_ref, gate_ref, o_ref):
    o_ref[...] = res_ref[...] + gate_ref[...] * (_dot(x_ref[...], w_ref[...]) + b_ref[...])


def mm_residual(x, w, b, res, gate):
    bsz, seq, k = x.shape
    n = w.shape[1]
    tm = min(seq, 512)
    return pl.pallas_call(
        _mm_res_body,
        grid=(bsz, seq // tm),
        in_specs=[pl.BlockSpec((None, tm, k), lambda b, i: (b, i, 0)),
                  pl.BlockSpec((k, n), lambda b, i: (0, 0)),
                  pl.BlockSpec((1, n), lambda b, i: (0, 0)),
                  pl.BlockSpec((None, tm, n), lambda b, i: (b, i, 0)),
                  pl.BlockSpec((None, 1, n), lambda b, i: (b, 0, 0))],
        out_specs=pl.BlockSpec((None, tm, n), lambda b, i: (b, i, 0)),
        out_shape=jax.ShapeDtypeStruct((bsz, seq, n), F32),
        compiler_params=_cparams("parallel", "parallel"),
        name="mm_residual",
    )(x, w, b.reshape(1, n), res, gate)


def _moe_pre_body(*refs, n_groups, n_experts, tile_offs):
    n_streams = len(tile_offs) - 1
    g_ref, wr_ref, br_ref, tok_ref, eid_ref, gate_ref = refs[3 * n_streams:]
    i = pl.program_id(0)
    for k in range(n_streams):
        x_ref, sh_ref, sc_ref = refs[3 * k:3 * k + 3]

        @pl.when(jnp.logical_and(i >= tile_offs[k], i < tile_offs[k + 1]))
        def _():
            tok = _norm_mod(x_ref[...], g_ref[...], sh_ref[...], sc_ref[...])
            _route_tokens(tok, wr_ref, br_ref, tok_ref, eid_ref, gate_ref, n_groups, n_experts)


def _route_tokens(tok, wr_ref, br_ref, tok_ref, eid_ref, gate_ref, n_groups, n_experts):
    tok_ref[...] = tok
    t_hi = tok.astype(BF16)
    t_lo = (tok - t_hi.astype(F32)).astype(BF16)
    logits = (_dot(t_hi, wr_ref[0]) + _dot(t_hi, wr_ref[1]) + _dot(t_lo, wr_ref[0])) + br_ref[...]
    lane = lax.broadcasted_iota(jnp.int32, logits.shape, 1)
    per = n_experts // n_groups
    big = jnp.int32(1 << 20)
    gmask = jnp.logical_and(lane >= n_experts, lane < n_experts + n_groups)
    gl = jnp.where(gmask, logits, NEG_BIG)
    gmax = jnp.max(gl, axis=-1, keepdims=True)
    gidx = jnp.min(jnp.where(gl == gmax, lane - n_experts, big), axis=-1, keepdims=True)
    p_top = 1.0 / jnp.sum(jnp.where(gmask, jnp.exp(gl - gmax), 0.0), axis=-1, keepdims=True)
    lo = gidx * per
    emask = jnp.logical_and(lane >= lo, lane < lo + per)
    el = jnp.where(emask, logits, NEG_BIG)
    m1 = jnp.max(el, axis=-1, keepdims=True)
    i1 = jnp.min(jnp.where(el == m1, lane, big), axis=-1, keepdims=True)
    el2 = jnp.where(lane == i1, NEG_BIG, el)
    m2 = jnp.max(el2, axis=-1, keepdims=True)
    i2 = jnp.min(jnp.where(el2 == m2, lane, big), axis=-1, keepdims=True)
    e21 = jnp.exp(m2 - m1)
    g1 = p_top / (1.0 + e21)
    g2 = g1 * e21
    ids = jnp.where(lane == 0, i1, jnp.where(lane == 1, i2, -1))
    eid_ref[...] = ids.T[0:SUBLANES, :]
    gate_ref[...] = jnp.where(lane == 0, g1, jnp.where(lane == 1, g2, 0.0))


def moe_pre(streams, g, wr, br, n_groups, n_experts):
    d = streams[0][0].shape[2]
    tm = MOE_TM
    tile_offs = [0]
    in_specs, args = [], []
    for x, shift, scale in streams:
        bsz, seq, _ = x.shape
        nt = seq // tm
        n_tiles = bsz * nt
        off = tile_offs[-1]
        tile_offs.append(off + n_tiles)

        def tile(i, off=off, n_tiles=n_tiles):
            return jnp.clip(i - off, 0, n_tiles - 1)

        in_specs += [pl.BlockSpec((None, tm, d), lambda i, tile=tile, nt=nt: (tile(i) // nt, tile(i) % nt, 0)),
                     pl.BlockSpec((None, 1, d), lambda i, tile=tile, nt=nt: (tile(i) // nt, 0, 0)),
                     pl.BlockSpec((None, 1, d), lambda i, tile=tile, nt=nt: (tile(i) // nt, 0, 0))]
        args += [x, shift, scale]
    in_specs += [pl.BlockSpec((1, d), lambda i: (0, 0)),
                 pl.BlockSpec((2, d, LANES), lambda i: (0, 0, 0)),
                 pl.BlockSpec((1, LANES), lambda i: (0, 0))]
    wr_hi = wr.astype(BF16)
    wr_split = jnp.stack([wr_hi, (wr - wr_hi.astype(F32)).astype(BF16)])
    args += [g.reshape(1, d), wr_split, br]
    total = tile_offs[-1] * tm
    rout = pl.BlockSpec((tm, LANES), lambda i: (i, 0))
    tok, eid, gate = pl.pallas_call(
        functools.partial(_moe_pre_body, n_groups=n_groups, n_experts=n_experts, tile_offs=tuple(tile_offs)),
        grid=(tile_offs[-1],),
        in_specs=in_specs,
        out_specs=[pl.BlockSpec((tm, d), lambda i: (i, 0)),
                   pl.BlockSpec((None, SUBLANES, tm), lambda i: (i, 0, 0)), rout],
        out_shape=[jax.ShapeDtypeStruct((total, d), F32),
                   jax.ShapeDtypeStruct((tile_offs[-1], SUBLANES, tm), jnp.int32),
                   jax.ShapeDtypeStruct((total, LANES), F32)],
        compiler_params=_cparams("parallel"),
        name="moe_pre",
    )(*args)
    return tok, eid, gate, tile_offs[:-1]


def _start_row_gather(row_index, n_rows, src_hbm, dst_vmem, sem):
    def body(g, c):
        r0 = pl.multiple_of(g * SUBLANES, SUBLANES)
        dst_tile = dst_vmem.at[pl.ds(r0, SUBLANES)]
        for k in range(SUBLANES):
            pltpu.make_async_copy(src_hbm.at[pl.ds(row_index(r0 + k), 1)], dst_tile.at[pl.ds(k, 1)], sem).start()
        return c

    lax.fori_loop(0, n_rows // SUBLANES, body, 0, unroll=2)


def _wait_row_gather(n_rows, src_hbm, dst_vmem, sem):
    pltpu.make_async_copy(src_hbm.at[pl.ds(0, n_rows)], dst_vmem, sem).wait()


def _expert_body(bv_ref, rk_ref, pe_ref, tot_ref, p0_ref, ts_hbm, tok_ref, wg_hbm, wu_hbm, wd_hbm, o_ref,
                 xbuf, xsem, win, isem, wcache, stg, wsem, cnt, *, layer):
    i = pl.program_id(0)
    n = pl.num_programs(0)
    slot = i % 2
    cr, cc = stg.shape[1:]
    total = tot_ref[0]
    mats_hbm = (wg_hbm, wu_hbm, wd_hbm)

    @pl.when(i == 0)
    def _():
        cnt[0] = 0
        cnt[1] = 0

    def live(b):
        return jnp.logical_and(b < n, bv_ref[jnp.minimum(b, n - 1)] > 0)

    def window(b):
        s = b % 2
        lo = pl.multiple_of((p0_ref[jnp.minimum(b, n - 1)] // MOE_WIN_ALIGN) * MOE_WIN_ALIGN, MOE_WIN_ALIGN)
        return pltpu.make_async_copy(ts_hbm.at[pl.ds(lo, MOE_WIN)],
                                     win.at[pl.ds(pl.multiple_of(s * MOE_WIN, MOE_WIN), MOE_WIN)], isem.at[s])

    def gather(b):
        s = b % 2
        base = s * MOE_WIN + p0_ref[jnp.minimum(b, n - 1)] % MOE_WIN_ALIGN
        _start_row_gather(lambda r: win[base + r], MOE_BM, tok_ref, xbuf.at[s], xsem.at[s])

    @pl.when(jnp.logical_and(i == 0, live(0)))
    def _():
        window(0).start()
        window(0).wait()
        gather(0)

    @pl.when(jnp.logical_and(i == 0, live(1)))
    def _():
        window(1).start()

    @pl.when(live(i + 1))
    def _():
        window(i + 1).wait()
        gather(i + 1)

    @pl.when(live(i + 2))
    def _():
        window(i + 2).start()

    def chunk_geom(c):
        q = c % MOE_NCH
        m = q // 4
        sub = q % 4
        r0 = jnp.where(m < 2, sub, sub // 2) * cr
        c0 = jnp.where(m < 2, 0, sub % 2) * cc
        return m, pl.multiple_of(r0, cr), pl.multiple_of(c0, cc)

    def issue(c):
        e = pe_ref[c // MOE_NCH]
        m, r0, c0 = chunk_geom(c)
        s = c % MOE_STAGE
        for k, w_hbm in enumerate(mats_hbm):
            @pl.when(m == k)
            def _():
                pltpu.make_async_copy(w_hbm.at[layer, e, pl.ds(r0, cr), pl.ds(c0, cc)], stg.at[s],
                                      wsem.at[s]).start()

    def cast(c):
        s = c % MOE_STAGE
        pltpu.make_async_copy(wg_hbm.at[layer, 0, pl.ds(0, cr), pl.ds(0, cc)], stg.at[s], wsem.at[s]).wait()
        ws = (c // MOE_NCH) % 2
        q = c % MOE_NCH
        step = min(MOE_CAST_ROWS, cr)
        assert cr % step == 0

        def slab(k, carry):
            rows = pl.ds(pl.multiple_of(k * step, step), step)
            wcache[ws, q, rows, :] = stg[s, rows, :].astype(BF16)
            return carry

        lax.fori_loop(0, cr // step, slab, 0)

    valid = bv_ref[i] > 0
    rank = rk_ref[i]
    issued = cnt[0]
    done = cnt[1]
    limit = jnp.minimum(total, MOE_NCH * (rank + 2))
    need = jnp.where(valid, MOE_NCH * (rank + 1), done)

    def fill(issued, done):
        hi = jnp.minimum(limit, done + MOE_STAGE)

        def body(c, carry):
            issue(c)
            return carry

        lax.fori_loop(issued, hi, body, 0)
        return jnp.maximum(issued, hi)

    def cast_and_refill(c, issued):
        cast(c)
        more = issued < jnp.minimum(limit, c + 1 + MOE_STAGE)

        @pl.when(more)
        def _():
            issue(issued)

        return issued + more.astype(jnp.int32)

    issued = fill(issued, done)
    issued = lax.fori_loop(done, need, cast_and_refill, issued)
    done = jnp.maximum(done, need)

    @pl.when(valid)
    def _():
        ws = rank % 2
        _wait_row_gather(MOE_BM, tok_ref, xbuf.at[slot], xsem.at[slot])
        x = xbuf[slot].astype(BF16)
        gate = sum(_dot(x[:, k * cr:(k + 1) * cr], wcache[ws, k]) for k in range(4))
        up = sum(_dot(x[:, k * cr:(k + 1) * cr], wcache[ws, 4 + k]) for k in range(4))
        h = (gate * jax.nn.sigmoid(gate) * up).astype(BF16)
        for half in range(2):
            o_ref[:, half * cc:(half + 1) * cc] = sum(
                _dot(h[:, k * cr:(k + 1) * cr], wcache[ws, 8 + 2 * k + half]) for k in range(2))

    @pl.when(jnp.logical_not(valid))
    def _():
        o_ref[...] = jnp.zeros_like(o_ref)

    fetched = issued
    issued = lax.fori_loop(done, fetched, cast_and_refill, issued)
    done = jnp.maximum(done, fetched)
    last = i == n - 1
    tail = jnp.where(last, issued, done)

    def drain(c, carry):
        cast(c)
        return carry

    lax.fori_loop(done, tail, drain, 0)
    cnt[0] = issued
    cnt[1] = jnp.maximum(done, tail)


def moe_experts(tok, tok_sorted, block_p0, block_valid, block_rank, present, n_chunks, w_gate, w_up, w_down,
                layer):
    d = tok.shape[1]
    n_blocks = block_valid.shape[0]
    n_rows = n_blocks * MOE_BM
    dh = w_gate.shape[3]
    assert 2 * dh == d and MOE_NCH == 12
    cr, cc = d // 4, dh
    any_spec = pl.BlockSpec(memory_space=pl.ANY)
    grid_spec = pltpu.PrefetchScalarGridSpec(
        num_scalar_prefetch=5,
        grid=(n_blocks,),
        in_specs=[any_spec, any_spec, any_spec, any_spec, any_spec],
        out_specs=pl.BlockSpec((MOE_BM, d), lambda i, *_: (i, 0)),
        scratch_shapes=[pltpu.VMEM((2, MOE_BM, d), F32), pltpu.SemaphoreType.DMA((2,)),
                        pltpu.SMEM((2 * MOE_WIN,), jnp.int32), pltpu.SemaphoreType.DMA((2,)),
                        pltpu.VMEM((2, MOE_NCH, cr, cc), BF16),
                        pltpu.VMEM((MOE_STAGE, cr, cc), F32), pltpu.SemaphoreType.DMA((MOE_STAGE,)),
                        pltpu.SMEM((2,), jnp.int32)],
    )
    return pl.pallas_call(
        functools.partial(_expert_body, layer=layer),
        grid_spec=grid_spec,
        out_shape=jax.ShapeDtypeStruct((n_rows, d), F32),
        compiler_params=_cparams("arbitrary"),
        name="moe_experts",
    )(block_valid, block_rank, present, n_chunks, block_p0, tok_sorted, tok, w_gate, w_up, w_down)


def _combine_body(dest_ref, nxt_ref, os_ref, gate_ref, res_ref, gt_ref, fg_ref, *rest, final_norm, with_next):
    if with_next:
        ng_ref, nsh_ref, nsc_ref, o_ref, h_ref, buf, sem = rest
    else:
        o_ref, buf, sem = rest
    rows = res_ref.shape[0]
    i = pl.program_id(0)
    n = pl.num_programs(0)
    slot = i % 2

    def start(idx_ref, s):
        for k in range(TOP_K):
            _start_row_gather(lambda r, k=k: idx_ref[k * rows + r], rows, os_ref, buf.at[s, k], sem.at[s])

    @pl.when(i == 0)
    def _():
        start(dest_ref, 0)

    @pl.when(i + 1 < n)
    def _():
        start(nxt_ref, 1 - slot)

    for k in range(TOP_K):
        _wait_row_gather(rows, os_ref, buf.at[slot, k], sem.at[slot])
    gates = gate_ref[...]
    mo = gates[:, 0:1] * buf[slot, 0] + gates[:, 1:2] * buf[slot, 1]
    y = res_ref[...] + gt_ref[...] * mo
    if final_norm:
        ms = jnp.mean(y * y, axis=-1, keepdims=True)
        y = y * lax.rsqrt(ms + NORM_EPS) * fg_ref[...]
    o_ref[...] = y
    if with_next:
        h_ref[...] = _norm_mod(y, ng_ref[...], nsh_ref[...], nsc_ref[...]).astype(h_ref.dtype)


def moe_combine(os, dest, gates, tile0, res, gt, final_g, final_norm, next_mod=None):
    bsz, seq, d = res.shape
    rows = MOE_TM
    nt = seq // rows
    n = bsz * nt
    tile = pl.BlockSpec((None, rows, d), lambda i: (i // nt, i % nt, 0))
    mspec = pl.BlockSpec((None, 1, d), lambda i: (i // nt, 0, 0))
    rspec = pl.BlockSpec((1, d), lambda i: (0, 0))
    dspec = lambda step: pl.BlockSpec((SUBLANES * rows,), lambda i: (tile0 + step(i),), memory_space=pltpu.SMEM)
    in_specs = [dspec(lambda i: i), dspec(lambda i: jnp.minimum(i + 1, n - 1)),
                pl.BlockSpec(memory_space=pl.ANY),
                pl.BlockSpec((rows, LANES), lambda i: (tile0 + i, 0)),
                tile, mspec, rspec]
    args = [dest, dest, os, gates, res, gt, final_g.reshape(1, d)]
    out_specs, out_shape = [tile], [jax.ShapeDtypeStruct((bsz, seq, d), F32)]
    if next_mod is not None:
        in_specs += [rspec, mspec, mspec]
        args += [next_mod[0].reshape(1, d), next_mod[1], next_mod[2]]
        out_specs.append(tile)
        out_shape.append(jax.ShapeDtypeStruct((bsz, seq, d), BF16))
    outs = pl.pallas_call(
        functools.partial(_combine_body, final_norm=final_norm, with_next=next_mod is not None),
        grid=(n,),
        in_specs=in_specs,
        out_specs=out_specs,
        out_shape=out_shape,
        scratch_shapes=[pltpu.VMEM((2, TOP_K, rows, d), F32), pltpu.SemaphoreType.DMA((2,))],
        compiler_params=_cparams("arbitrary"),
        name="moe_combine",
    )(*args)
    return outs if next_mod is not None else outs[0]


def _plan_body(e_ref, dest_ref, tab_ref, blk_ref, present_ref, *, n_experts):
    n_rows, w = e_ref.shape
    e_all = e_ref[...]
    li = lax.broadcasted_iota(jnp.int32, (w, w), 0)
    lj = lax.broadcasted_iota(jnp.int32, (w, w), 1)
    incl = (li <= lj).astype(BF16)
    ri = lax.broadcasted_iota(jnp.int32, (n_rows, n_rows), 0)
    rj = lax.broadcasted_iota(jnp.int32, (n_rows, n_rows), 1)
    before = (rj < ri).astype(BF16)
    elane = lax.broadcasted_iota(jnp.int32, (n_rows, LANES), 1)
    row_tot = jnp.zeros((n_rows, LANES), F32)
    for e in range(n_experts):
        tot = jnp.sum((e_all == e).astype(F32), axis=1, keepdims=True)
        row_tot = row_tot + jnp.where(elane == e, tot, 0.0)
    rows_before = _dot(before, row_tot.astype(BF16))
    counts = jnp.sum(row_tot, axis=0, keepdims=True).astype(jnp.int32)
    lane1 = lax.broadcasted_iota(jnp.int32, (1, LANES), 1)

    def excl_prefix(v):
        acc = v
        sh = 1
        while sh < LANES:
            acc = acc + jnp.where(lane1 >= sh, pltpu.roll(acc, sh, axis=1), 0)
            sh *= 2
        return acc - v

    start = excl_prefix(counts)
    padded = (counts + (MOE_BM - 1)) // MOE_BM * MOE_BM
    pad_start = excl_prefix(padded)
    pad_end = pad_start + padded
    has = (counts > 0).astype(jnp.int32)
    rank = excl_prefix(has)
    n_chunks = MOE_NCH * jnp.sum(has.astype(F32), axis=1, keepdims=True).astype(jnp.int32)
    tab_ref[...] = jnp.concatenate([counts, start, pad_start, pad_end, jnp.broadcast_to(n_chunks, (1, LANES)),
                                    jnp.zeros((SUBLANES - 5, LANES), jnp.int32)], axis=0)
    nb = blk_ref.shape[0]
    f32 = lambda v: v.astype(F32)
    lane_b = lax.broadcasted_iota(jnp.int32, (nb, LANES), 1)
    first_row = lax.broadcasted_iota(jnp.int32, (nb, 1), 0) * MOE_BM
    is_expert = lane_b < n_experts
    expert = jnp.sum(f32(jnp.logical_and(is_expert, pad_end <= first_row)), axis=1, keepdims=True)
    expert = jnp.minimum(expert.astype(jnp.int32), n_experts - 1)
    mine = lane_b == expert
    pick = lambda v: jnp.sum(jnp.where(mine, f32(v), 0.0), axis=1, keepdims=True).astype(jnp.int32)
    live = first_row < jnp.max(f32(pad_end), axis=1, keepdims=True).astype(jnp.int32)
    p0 = jnp.where(live, pick(start) - pick(pad_start) + first_row, 0)
    blk_ref[...] = jnp.where(lane_b == 0, live.astype(jnp.int32),
                             jnp.where(lane_b == 1, pick(rank), jnp.where(lane_b == 2, p0, 0)))
    row_r = lax.broadcasted_iota(jnp.int32, (LANES, LANES), 0)
    lane_e = lax.broadcasted_iota(jnp.int32, (LANES, LANES), 1)
    hit_r = jnp.logical_and(has > 0, rank == row_r)
    present = jnp.sum(jnp.where(hit_r, f32(lane_e), 0.0), axis=1, keepdims=True).astype(jnp.int32)
    present_ref[...] = jnp.broadcast_to(present, (LANES, LANES))
    base = rows_before + pad_start.astype(F32)
    dest = jnp.zeros((n_rows, w), F32)
    for e in range(n_experts):
        hit = e_all == e
        within = _dot(hit.astype(BF16), incl)
        dest = dest + jnp.where(hit, within - 1.0 + base[:, e:e + 1], 0.0)
    dest_ref[...] = dest.astype(jnp.int32)


def _route_plan(eid, n_experts):
    n_tiles, r8, tm = eid.shape
    a = n_tiles * TOP_K * tm
    n_blocks = -(-a // MOE_BM) + n_experts
    assert (tm & (tm - 1)) == 0 and TOP_K == 2
    nb_pad = -(-n_blocks // SUBLANES) * SUBLANES
    dest, tab, blk, present = pl.pallas_call(
        functools.partial(_plan_body, n_experts=n_experts),
        out_shape=[jax.ShapeDtypeStruct((n_tiles * r8, tm), jnp.int32),
                   jax.ShapeDtypeStruct((SUBLANES, LANES), jnp.int32),
                   jax.ShapeDtypeStruct((nb_pad, LANES), jnp.int32),
                   jax.ShapeDtypeStruct((LANES, LANES), jnp.int32)],
        compiler_params=pltpu.CompilerParams(vmem_limit_bytes=V7X_VMEM_LIMIT_BYTES),
        name="moe_route_plan",
    )(eid.reshape(n_tiles * r8, tm))
    e_flat = eid[:, :TOP_K, :].reshape(-1)
    order = jnp.argsort(e_flat).astype(jnp.int32)
    shift = tm.bit_length() - 1
    tok_of = ((order >> (shift + 1)) << shift) | (order & (tm - 1))
    slack = -a % MOE_WIN_ALIGN + MOE_WIN
    tok_sorted = jnp.concatenate([tok_of, jnp.zeros((slack,), jnp.int32)])
    blk = blk[:n_blocks]
    return dest.reshape(-1), tok_sorted, blk[:, 2], blk[:, 0], blk[:, 1], present[:n_experts, 0], tab[4, :1]


def hier_moe(streams, norm_g, wg, bg, we, be, w_gate, w_up, w_down, layer, final_g, final_norm, next_mods=None):
    n_groups = wg.shape[1]
    n_experts = we.shape[1]
    d = wg.shape[0]
    wr = jnp.zeros((d, LANES), F32).at[:, :n_experts].set(we).at[:, n_experts:n_experts + n_groups].set(wg)
    br = jnp.zeros((1, LANES), F32).at[0, :n_experts].set(be).at[0, n_experts:n_experts + n_groups].set(bg)
    tok, eid, gates, tile0s = moe_pre([s[:3] for s in streams], norm_g, wr, br, n_groups, n_experts)
    dest, tok_sorted, block_p0, block_valid, block_rank, present, n_chunks = _route_plan(eid, n_experts)
    os = moe_experts(tok, tok_sorted, block_p0, block_valid, block_rank, present, n_chunks, w_gate, w_up, w_down,
                     layer)
    next_mods = next_mods or [None] * len(streams)
    return [moe_combine(os, dest, gates, tile0, x, gt, final_g, final_norm, nm)
            for (x, _, _, gt), tile0, nm in zip(streams, tile0s, next_mods)]


def _norm_mod_body(x_ref, g_ref, sh_ref, sc_ref, o_ref):
    o_ref[...] = _norm_mod(x_ref[...], g_ref[...], sh_ref[...], sc_ref[...]).astype(o_ref.dtype)


def norm_mod(x, g, shift, scale):
    bsz, seq, d = x.shape
    tm = min(seq, 512)
    mspec = pl.BlockSpec((None, 1, d), lambda b, i: (b, 0, 0))
    return pl.pallas_call(
        _norm_mod_body,
        grid=(bsz, seq // tm),
        in_specs=[pl.BlockSpec((None, tm, d), lambda b, i: (b, i, 0)),
                  pl.BlockSpec((1, d), lambda b, i: (0, 0)), mspec, mspec],
        out_specs=pl.BlockSpec((None, tm, d), lambda b, i: (b, i, 0)),
        out_shape=jax.ShapeDtypeStruct((bsz, seq, d), BF16),
        compiler_params=_cparams("parallel", "parallel"),
        name="norm_mod",
    )(x, g.reshape(1, d), shift, scale)


def _s5_arrange(h):
    bsz, t, d = h.shape
    c = t // (S5_SEGS * S5_TAU)
    h = h.reshape(bsz, S5_SEGS, c, S5_TAU, d // LANES, LANES)
    return h.transpose(2, 0, 1, 4, 3, 5).reshape(c * bsz * S5_SEGS, d * S5_TAU)


def _s5_unarrange(y, bsz):
    r, w = y.shape
    d = w // S5_TAU
    c = r // (bsz * S5_SEGS)
    y = y.reshape(c, bsz, S5_SEGS, d // LANES, S5_TAU, LANES)
    return y.transpose(1, 2, 0, 4, 3, 5).reshape(bsz, S5_SEGS * c * S5_TAU, d)


def _s5_operators(a_re, a_im, log_step, b_re, b_im, c_re, c_im):
    n_g, n_p = a_re.shape[1:]
    n_h = b_re.shape[-1]
    gpt = LANES // n_h
    n_j = n_g // gpt
    tau = S5_TAU
    assert tau * n_h == LANES and 2 * n_p == LANES
    lam_step = lax.complex(a_re, a_im) * jnp.exp(log_step)[..., None]
    lam_bar = jnp.exp(lam_step)
    b_bar = ((lam_bar - 1.0) / lax.complex(a_re, a_im))[..., None] * lax.complex(b_re, b_im)
    c_mat = lax.complex(c_re, c_im)
    ks = jnp.arange(tau + 1, dtype=F32)[None, :, None, None]
    pw = jnp.exp(lam_step[:, None] * ks)
    ein = functools.partial(jnp.einsum, precision=HIGHEST)
    inj_c, cl_c, lt = [], [], []
    tz_c = 0.0
    for d in range(2):
        pos = jnp.arange(tau) if d == 0 else jnp.arange(tau)[::-1]
        inj = (pw[d][tau - 1 - pos][..., None] * b_bar[d][None]).reshape(tau, n_j, gpt, n_p, n_h)
        inj = inj.transpose(1, 0, 2, 4, 3).reshape(n_j, tau * LANES, n_p)
        inj_c.append(jnp.concatenate([inj.real, inj.imag], axis=-1))
        cl = (c_mat[d][None] * pw[d][pos + 1][:, :, None, :]).reshape(tau, n_j, gpt, n_h, n_p)
        cl = cl.transpose(1, 2, 4, 0, 3).reshape(n_j, gpt * n_p, tau * n_h)
        cl_c.append(jnp.concatenate([cl.real, -cl.imag], axis=1))
        mk = ein('gop,kgp,gph->kgoh', c_mat[d], pw[d][:tau], b_bar[d]).real
        diff = pos[:, None] - pos[None, :]
        tz = jnp.where((diff >= 0)[:, :, None, None, None], mk[jnp.clip(diff, 0, tau - 1)], 0.0)
        tz = tz.reshape(tau, tau, n_j, gpt, n_h, n_h)
        tz_c = tz_c + tz.transpose(2, 1, 3, 5, 0, 4).reshape(n_j, tau * LANES, tau * n_h)
        lt_d = pw[d][tau].reshape(n_j, 1, gpt * n_p)
        lt.append(jnp.concatenate([lt_d.real, lt_d.imag], axis=-1))
    ws, wu, wh = s5_expand(jnp.stack(inj_c).astype(BF16), jnp.stack(cl_c).astype(BF16), tz_c.astype(BF16),
                           n_h, n_p)
    return ws, wu, wh, jnp.stack(lt).astype(F32)


def _s5_expand_body(inj_ref, cl_ref, tz_ref, ws_ref, wu_ref, wh_ref, *, n_h, n_p):
    rows = tz_ref.shape[0]
    gpt = LANES // n_h
    row = lax.broadcasted_iota(jnp.int32, (rows, LANES), 0)
    lane = lax.broadcasted_iota(jnp.int32, (rows, LANES), 1)
    sel_r = lax.broadcasted_iota(jnp.int32, (LANES, LANES), 0)
    sel_l = lax.broadcasted_iota(jnp.int32, (LANES, LANES), 1)
    grp_in = (row // n_h) % gpt
    grp_st = (row // n_p) % gpt

    def spread_out(m, t, grp_row):
        sel = jnp.logical_and(sel_r // n_h == t, sel_r % n_h == sel_l % n_h).astype(BF16)
        return jnp.where(grp_row == lane // n_h, _dot(m, sel), 0.0).astype(BF16)

    def spread_state(m, c, q, grp_row):
        sel = jnp.logical_and(sel_r // n_p == c, sel_r % n_p == sel_l % n_p).astype(BF16)
        return jnp.where(grp_row == (LANES // n_p) * q + lane // n_p, _dot(m, sel), 0.0).astype(BF16)

    w2 = 2 * gpt * n_p
    tz = tz_ref[...]
    for t in range(S5_TAU):
        wu_ref[:, t * LANES:(t + 1) * LANES] = spread_out(tz, t, grp_in)
    for d in range(2):
        cl = cl_ref[d]
        inj = inj_ref[d]
        for t in range(S5_TAU):
            wh_ref[d, :, t * LANES:(t + 1) * LANES] = spread_out(cl, t, grp_st)
        for c in range(2):
            for q in range(gpt * n_p // LANES):
                lo = d * w2 + c * gpt * n_p + q * LANES
                ws_ref[:, lo:lo + LANES] = spread_state(inj, c, q, grp_in)


def s5_expand(inj_c, cl_c, tz_c, n_h, n_p):
    n_j, rows, _ = tz_c.shape
    gpt = LANES // n_h
    w2 = 2 * gpt * n_p
    assert rows == S5_TAU * LANES == w2
    cspec = pl.BlockSpec((2, None, rows, LANES), lambda j: (0, j, 0, 0))
    return pl.pallas_call(
        functools.partial(_s5_expand_body, n_h=n_h, n_p=n_p),
        grid=(n_j,),
        in_specs=[cspec, cspec, pl.BlockSpec((None, rows, LANES), lambda j: (j, 0, 0))],
        out_specs=[pl.BlockSpec((None, rows, 2 * w2), lambda j: (j, 0, 0)),
                   pl.BlockSpec((None, rows, rows), lambda j: (j, 0, 0)),
                   pl.BlockSpec((2, None, w2, rows), lambda j: (0, j, 0, 0))],
        out_shape=[jax.ShapeDtypeStruct((n_j, rows, 2 * w2), BF16),
                   jax.ShapeDtypeStruct((n_j, rows, rows), BF16),
                   jax.ShapeDtypeStruct((2, n_j, w2, rows), BF16)],
        compiler_params=_cparams("parallel"),
        name="s5_expand",
    )(inj_c, cl_c, tz_c)


def _s5_inj_body(xc_ref, xl_ref, w_ref, oc_ref, ol_ref):
    w = w_ref[...]
    ol_ref[...] = _dot(xl_ref[...], w)

    @pl.when(pl.program_id(1) == 0)
    def _():
        oc_ref[...] = _dot(xc_ref[...], w)


def s5_inject(xr_c, xr_l, ws):
    r_c, r_l = xr_c.shape[0], xr_l.shape[0]
    n_j, k, n = ws.shape
    tm = r_l // 2 if r_l % 32 == 0 else r_l
    return pl.pallas_call(
        _s5_inj_body,
        grid=(n_j, r_l // tm),
        in_specs=[pl.BlockSpec((r_c, k), lambda j, i: (0, j)),
                  pl.BlockSpec((tm, k), lambda j, i: (i, j)),
                  pl.BlockSpec((None, k, n), lambda j, i: (j, 0, 0))],
        out_specs=[pl.BlockSpec((r_c, n), lambda j, i: (0, j)),
                   pl.BlockSpec((tm, n), lambda j, i: (i, j))],
        out_shape=[jax.ShapeDtypeStruct((r_c, n_j * n), F32), jax.ShapeDtypeStruct((r_l, n_j * n), F32)],
        compiler_params=_cparams("parallel", "arbitrary"),
        name="s5_inject",
    )(xr_c, xr_l, ws)


def _cmul(ar, ai, br, bi):
    return ar * br - ai * bi, ar * bi + ai * br


def _s5_scan_body(sc_ref, sl_ref, lt_ref, h_ref, raw_ref, *, n_ctx, n_lat, bsz):
    d = pl.program_id(1)
    w2 = lt_ref.shape[-1]
    w = w2 // 2
    rows = bsz * S5_SEGS
    seg = lax.broadcasted_iota(jnp.int32, (rows, 1), 0) % S5_SEGS
    is_late = seg != d
    lam_r = lt_ref[:, 0:w]
    lam_i = lt_ref[:, w:w2]
    zero = jnp.zeros((rows, w), F32)
    one = (jnp.ones((1, w), F32), jnp.zeros((1, w), F32))

    def swap_segments(x):
        return jnp.where(seg == 0, pltpu.roll(x, rows - 1, axis=0), pltpu.roll(x, 1, axis=0))

    def phase(s_ref, n_steps, hin_r, hin_i, write):
        def chunk(k):
            return jnp.where(d == 0, k, n_steps - 1 - k)

        def step_raw(k, carry):
            hr, hi = carry
            c = chunk(k)
            raw_ref[c, :, 0:w] = hr
            raw_ref[c, :, w:w2] = hi
            nr, ni = _cmul(lam_r, lam_i, hr, hi)
            return nr + s_ref[c, :, 0:w], ni + s_ref[c, :, w:w2]

        er, ei = lax.fori_loop(0, n_steps, step_raw, (zero, zero), unroll=S5_SCAN_UNROLL)
        pr, pi = lax.fori_loop(0, n_steps, lambda k, q: _cmul(lam_r, lam_i, *q), one)
        dr, di = _cmul(pr, pi, hin_r, hin_i)
        first_r = jnp.where(is_late, 0.0, er + dr)
        first_i = jnp.where(is_late, 0.0, ei + di)
        carry_r = jnp.where(is_late, swap_segments(first_r), hin_r)
        carry_i = jnp.where(is_late, swap_segments(first_i), hin_i)
        if write:
            def step_fix(k, q):
                c = chunk(k)
                fr, fi = _cmul(q[0], q[1], carry_r, carry_i)
                h_ref[c, :, 0:w] = (raw_ref[c, :, 0:w] + fr).astype(h_ref.dtype)
                h_ref[c, :, w:w2] = (raw_ref[c, :, w:w2] + fi).astype(h_ref.dtype)
                return _cmul(lam_r, lam_i, q[0], q[1])

            lax.fori_loop(0, n_steps, step_fix, one, unroll=S5_SCAN_UNROLL)
        lr, li = _cmul(pr, pi, carry_r, carry_i)
        last_r = jnp.where(is_late, er + lr, 0.0)
        last_i = jnp.where(is_late, ei + li, 0.0)
        return (jnp.where(is_late, 0.0, swap_segments(last_r)), jnp.where(is_late, 0.0, swap_segments(last_i)))

    hr, hi = phase(sc_ref, n_ctx, zero, zero, False)
    phase(sl_ref, n_lat, hr, hi, True)


def s5_scan(s_ctx, s_lat, lt, bsz):
    assert S5_SEGS == 2
    n_ctx, rows, _ = s_ctx.shape
    n_lat = s_lat.shape[0]
    n_j = lt.shape[1]
    w2 = lt.shape[-1]
    return pl.pallas_call(
        functools.partial(_s5_scan_body, n_ctx=n_ctx, n_lat=n_lat, bsz=bsz),
        grid=(n_j, 2),
        in_specs=[pl.BlockSpec((n_ctx, rows, w2), lambda j, d: (0, 0, 2 * j + d)),
                  pl.BlockSpec((n_lat, rows, w2), lambda j, d: (0, 0, 2 * j + d)),
                  pl.BlockSpec((None, None, 1, w2), lambda j, d: (d, j, 0, 0))],
        out_specs=pl.BlockSpec((None, n_lat, rows, w2), lambda j, d: (d, 0, 0, j)),
        out_shape=jax.ShapeDtypeStruct((2, n_lat, rows, n_j * w2), BF16),
        scratch_shapes=[pltpu.VMEM((max(n_ctx, n_lat), rows, w2), F32)],
        compiler_params=_cparams("parallel", "parallel"),
        name="s5_scan",
    )(s_ctx, s_lat, lt)


def _s5_out_body(x_ref, hf_ref, hb_ref, wu_ref, whf_ref, whb_ref, o_ref):
    o_ref[...] = (_dot(x_ref[...], wu_ref[...]) + _dot(hf_ref[...], whf_ref[...])
                  + _dot(hb_ref[...], whb_ref[...])).astype(o_ref.dtype)


def s5_readout(xr, h, wu, wh):
    r = xr.shape[0]
    n_j, k, n = wu.shape
    w2 = wh.shape[2]
    tm = min(r, 1024)
    return pl.pallas_call(
        _s5_out_body,
        grid=(n_j, r // tm),
        in_specs=[pl.BlockSpec((tm, k), lambda j, i: (i, j)),
                  pl.BlockSpec((None, tm, w2), lambda j, i: (0, i, j)),
                  pl.BlockSpec((None, tm, w2), lambda j, i: (1, i, j)),
                  pl.BlockSpec((None, k, n), lambda j, i: (j, 0, 0)),
                  pl.BlockSpec((None, None, w2, n), lambda j, i: (0, j, 0, 0)),
                  pl.BlockSpec((None, None, w2, n), lambda j, i: (1, j, 0, 0))],
        out_specs=pl.BlockSpec((tm, n), lambda j, i: (i, j)),
        out_shape=jax.ShapeDtypeStruct((r, n_j * n), BF16),
        compiler_params=_cparams("parallel", "parallel"),
        name="s5_readout",
    )(xr, h, h, wu, wh, wh)


def _gelu_tanh(x):
    return 0.5 * x * (1.0 + jnp.tanh(math.sqrt(2.0 / math.pi) * (x + 0.044715 * (x * x * x))))


def _s5_glu_body(x_ref, y_ref, g_ref, sh_ref, sc_ref, dk_ref, w1_ref, w2_ref, b1_ref, b2_ref, gt_ref, o_ref):
    x = x_ref[...]
    u = _norm_mod(x, g_ref[...], sh_ref[...], sc_ref[...])
    y = _gelu_tanh(y_ref[...].astype(F32) + dk_ref[...] * u).astype(BF16)
    o = (_dot(y, w1_ref[...]) + b1_ref[...]) * jax.nn.sigmoid(_dot(y, w2_ref[...]) + b2_ref[...])
    o_ref[...] = x + gt_ref[...] * o


def s5_glu(x, y, g, shift, scale, d_skip, w1, b1, w2, b2, gate):
    bsz, seq, d = x.shape
    tm = min(seq, 512)
    row = lambda a: a.reshape(1, d)
    rspec = pl.BlockSpec((1, d), lambda b, i: (0, 0))
    mspec = pl.BlockSpec((None, 1, d), lambda b, i: (b, 0, 0))
    tile = pl.BlockSpec((None, tm, d), lambda b, i: (b, i, 0))
    wspec = pl.BlockSpec((d, d), lambda b, i: (0, 0), pipeline_mode=pl.Buffered(1))
    return pl.pallas_call(
        _s5_glu_body,
        grid=(bsz, seq // tm),
        in_specs=[tile, tile, rspec, mspec, mspec, rspec, wspec, wspec, rspec, rspec, mspec],
        out_specs=tile,
        out_shape=jax.ShapeDtypeStruct((bsz, seq, d), F32),
        compiler_params=_cparams("parallel", "parallel"),
        name="s5_glu",
    )(x, y, row(g), shift, scale, row(d_skip), w1, w2, row(b1), row(b2), gate)


def s5_mix(xl, hl, hc, g, sh_l, sc_l, gate_l, a_re, a_im, log_step, b_re, b_im, c_re, c_im, d_skip,
           w1, b1, w2, b2):
    bsz, seq, d = xl.shape
    xr_c = _s5_arrange(hc)
    xr_l = _s5_arrange(hl)
    rows = bsz * S5_SEGS
    n_ctx = xr_c.shape[0] // rows
    n_lat = xr_l.shape[0] // rows
    ws, wu, wh, lt = _s5_operators(a_re, a_im, log_step, b_re, b_im, c_re, c_im)
    s_ctx, s_lat = s5_inject(xr_c, xr_l, ws)
    h = s5_scan(s_ctx.reshape(n_ctx, rows, -1), s_lat.reshape(n_lat, rows, -1), lt, bsz)
    y = s5_readout(xr_l, h.reshape(2, n_lat * rows, -1), wu, wh)
    y = _s5_unarrange(y, bsz)
    return s5_glu(xl, y, g, sh_l, sc_l, d_skip, w1.astype(BF16), b1, w2.astype(BF16), b2, gate_l)


def hyena_mix(x, g, shift, scale, gate, w_in, b_in, conv_w, conv_b, fw1, fb1, fw2, fb2, fw3, freq, skip,
              w_out, b_out):
    seq = x.shape[1]
    if seq >= FFT_MIN_SEQ and (2 * seq) % (2 * FFT_N2) == 0:
        a, dd, _ = hyena_filter_taps(seq, fw1, fb1, fw2, fb2, fw3, freq, F32)
        v, x0 = hyena_in(x, g, shift, scale, w_in, b_in, conv_w, conv_b, F32)
        yg = hyena_conv_fft(v, x0, skip, a, dd)
    else:
        cmat, smat = dft_matrices(seq)
        a, dd, kn = hyena_filter_taps(seq, fw1, fb1, fw2, fb2, fw3, freq, BF16)
        kr, ki = hyena_filter_dft(a, dd, cmat, smat)
        v, x0 = hyena_in(x, g, shift, scale, w_in, b_in, conv_w, conv_b, BF16)
        yg = hyena_conv(v, x0, skip, kr, ki, kn, cmat, smat)
    return mm_residual(yg, w_out, b_out, x, gate)


def kernel(x, c, ctx, c_ctx, ada_w, ada_b, norm_g, final_g, hy_w_in, hy_b_in, hy_conv_w, hy_conv_b, hy_fw1,
           hy_fb1, hy_fw2, hy_fb2, hy_fw3, hy_freq, hy_skip, hy_w_out, hy_b_out, s5_a_re, s5_a_im,
           s5_log_step, s5_b_re, s5_b_im, s5_c_re, s5_c_im, s5_d, s5_w1, s5_b1, s5_w2, s5_b2, moe_wg, moe_bg,
           moe_we, moe_be, moe_w_gate, moe_w_up, moe_w_down):
    bsz, _, d = x.shape
    depth = ada_w.shape[0]
    assert depth == 2 and bsz < SUBLANES
    c_all = jnp.zeros((SUBLANES, d), F32).at[:bsz].set(c).at[bsz].set(c_ctx)
    mods = ada_mod(c_all, ada_w, ada_b)

    def mod_rows(layer, k):
        lat = mods[layer, :bsz, k * d:(k + 1) * d][:, None, :]
        cx = jnp.broadcast_to(mods[layer, bsz, k * d:(k + 1) * d][None, None, :], (bsz, 1, d))
        return lat, cx

    (sh_a, csh_a), (sc_a, csc_a), (gt_a, cgt_a) = mod_rows(0, 0), mod_rows(0, 1), mod_rows(0, 2)
    (sh_f, csh_f), (sc_f, csc_f), (gt_f, cgt_f) = mod_rows(0, 3), mod_rows(0, 4), mod_rows(0, 5)
    hy = (hy_w_in[0].astype(BF16), hy_b_in[0], hy_conv_w[0], hy_conv_b[0], hy_fw1[0], hy_fb1[0], hy_fw2[0],
          hy_fb2[0], hy_fw3[0], hy_freq[0], hy_skip[0], hy_w_out[0].astype(BF16), hy_b_out[0])
    xl = hyena_mix(x, norm_g[0, 0], sh_a, sc_a, gt_a, *hy)
    xc = hyena_mix(ctx, norm_g[0, 0], csh_a, csc_a, cgt_a, *hy)
    (sh_a, csh_a), (sc_a, csc_a), (gt_a, _) = mod_rows(1, 0), mod_rows(1, 1), mod_rows(1, 2)
    (xl, hl), (_, hc) = hier_moe([(xl, sh_f, sc_f, gt_f), (xc, csh_f, csc_f, cgt_f)], norm_g[0, 1],
                                 moe_wg[0], moe_bg[0], moe_we[0], moe_be[0], moe_w_gate, moe_w_up, moe_w_down, 0,
                                 final_g, False,
                                 next_mods=[(norm_g[1, 0], sh_a, sc_a), (norm_g[1, 0], csh_a, csc_a)])
    (sh_f, _), (sc_f, _), (gt_f, _) = mod_rows(1, 3), mod_rows(1, 4), mod_rows(1, 5)
    xl = s5_mix(xl, hl, hc, norm_g[1, 0], sh_a, sc_a, gt_a, s5_a_re[0], s5_a_im[0], s5_log_step[0],
                s5_b_re[0], s5_b_im[0], s5_c_re[0], s5_c_im[0], s5_d[0], s5_w1[0], s5_b1[0], s5_w2[0], s5_b2[0])
    (out,) = hier_moe([(xl, sh_f, sc_f, gt_f)], norm_g[1, 1], moe_wg[1], moe_bg[1], moe_we[1], moe_be[1],
                      moe_w_gate, moe_w_up, moe_w_down, 1, final_g, True)
    return out
```

```python
import functools
import math

import numpy as np
import jax
import jax.numpy as jnp
from jax import lax
from jax.experimental import pallas as pl
from jax.experimental.pallas import tpu as pltpu

F32 = jnp.float32
BF16 = jnp.bfloat16
HIGHEST = lax.Precision.HIGHEST

NORM_EPS = 1e-6
HY_DECAY_TARGET = 1e-2
HY_FAST_PCT = 0.3
HY_SLOW_PCT = 1.5
TOP_K = 2

V7X_VMEM_LIMIT_BYTES = 56 * 1024 * 1024
LANES = 128
SUBLANES = 8
S5_TAU = 8
S5_SEGS = 2
S5_SCAN_UNROLL = 8
MOE_TM = 256
MOE_BM = 256
MOE_NCH = 12
MOE_STAGE = 4
MOE_CAST_ROWS = 128
MOE_WIN_ALIGN = 1024
MOE_WIN = 2 * MOE_WIN_ALIGN
NEG_BIG = -1e30


def _cparams(*sem):
    return pltpu.CompilerParams(dimension_semantics=sem, vmem_limit_bytes=V7X_VMEM_LIMIT_BYTES)


def _norm_mod(x, g, shift, scale):
    ms = jnp.mean(x * x, axis=-1, keepdims=True)
    return (x * lax.rsqrt(ms + NORM_EPS) * g) * (1.0 + scale) + shift


def _dot(a, b):
    return jnp.dot(a, b, preferred_element_type=F32)


def _ada_body(c_ref, w_ref, b_ref, o_ref):
    x = c_ref[...]
    s = (x * jax.nn.sigmoid(x)).astype(BF16)
    o_ref[...] = _dot(s, w_ref[...].astype(BF16)) + b_ref[...]


def ada_mod(c_all, ada_w, ada_b):
    depth, d, n = ada_w.shape
    tn = min(n, 1024)
    return pl.pallas_call(
        _ada_body,
        grid=(depth, n // tn),
        in_specs=[pl.BlockSpec((SUBLANES, d), lambda l, j: (0, 0)),
                  pl.BlockSpec((None, d, tn), lambda l, j: (l, 0, j)),
                  pl.BlockSpec((None, 1, tn), lambda l, j: (l, 0, j))],
        out_specs=pl.BlockSpec((None, SUBLANES, tn), lambda l, j: (l, 0, j)),
        out_shape=jax.ShapeDtypeStruct((depth, SUBLANES, n), F32),
        compiler_params=_cparams("parallel", "parallel"),
        name="ada_mod",
    )(c_all, ada_w, ada_b.reshape(depth, 1, n))


def _hy_in_body(xp_ref, xm_ref, xn_ref, g_ref, sh_ref, sc_ref,
                w0_ref, w1_ref, w2_ref, b0_ref, b1_ref, b2_ref,
                cw0_ref, cw1_ref, cw2_ref, cb0_ref, cb1_ref, cb2_ref,
                v_ref, x0_ref):
    i = pl.program_id(2)
    ni = pl.num_programs(2)
    tm = xm_ref.shape[0]
    x = jnp.concatenate([xp_ref[...], xm_ref[...], xn_ref[...]], axis=0)
    h = _norm_mod(x, g_ref[...], sh_ref[...], sc_ref[...]).astype(BF16)
    rows = lax.broadcasted_iota(jnp.int32, (tm + 2 * SUBLANES, 1), 0)
    valid = jnp.logical_and(jnp.logical_or(rows >= SUBLANES, i > 0),
                            jnp.logical_or(rows < tm + SUBLANES, i < ni - 1))

    def part(w_ref, b_ref, cw_ref, cb_ref):
        z = jnp.where(valid, _dot(h, w_ref[...]) + b_ref[...], 0.0)
        cw = cw_ref[...]
        zp = pltpu.roll(z, 1, axis=0)[SUBLANES:tm + SUBLANES]
        zn = pltpu.roll(z, tm + 2 * SUBLANES - 1, axis=0)[SUBLANES:tm + SUBLANES]
        return zp * cw[0:1] + z[SUBLANES:tm + SUBLANES] * cw[1:2] + zn * cw[2:3] + cb_ref[...]

    x0 = part(w0_ref, b0_ref, cw0_ref, cb0_ref)
    x1 = part(w1_ref, b1_ref, cw1_ref, cb1_ref)
    v = part(w2_ref, b2_ref, cw2_ref, cb2_ref) * x1
    v_ref[...] = v.astype(v_ref.dtype)
    x0_ref[...] = x0.astype(BF16)


def hyena_in(x, g, shift, scale, w_in, b_in, conv_w, conv_b, v_dtype):
    bsz, seq, d = x.shape
    tm = min(seq, 512)
    tn = min(d, 1024)
    nj = d // tn
    r8 = tm // SUBLANES
    last8 = seq // SUBLANES - 1
    row = lambda a: a.reshape(1, -1)
    wspec = lambda k: pl.BlockSpec((d, tn), lambda j, b, i: (0, k * nj + j))
    rspec = lambda k: pl.BlockSpec((1, tn), lambda j, b, i: (0, k * nj + j))
    cspec = lambda k: pl.BlockSpec((3, tn), lambda j, b, i: (0, k * nj + j))
    mspec = pl.BlockSpec((None, 1, d), lambda j, b, i: (b, 0, 0))
    out_spec = pl.BlockSpec((None, tm, tn), lambda j, b, i: (b, i, j))
    return pl.pallas_call(
        _hy_in_body,
        grid=(nj, bsz, seq // tm),
        in_specs=[pl.BlockSpec((None, SUBLANES, d), lambda j, b, i: (b, jnp.maximum(i * r8 - 1, 0), 0)),
                  pl.BlockSpec((None, tm, d), lambda j, b, i: (b, i, 0)),
                  pl.BlockSpec((None, SUBLANES, d), lambda j, b, i: (b, jnp.minimum((i + 1) * r8, last8), 0)),
                  pl.BlockSpec((1, d), lambda j, b, i: (0, 0)), mspec, mspec,
                  wspec(0), wspec(1), wspec(2), rspec(0), rspec(1), rspec(2),
                  cspec(0), cspec(1), cspec(2), rspec(0), rspec(1), rspec(2)],
        out_specs=[out_spec, out_spec],
        out_shape=[jax.ShapeDtypeStruct((bsz, seq, d), v_dtype), jax.ShapeDtypeStruct((bsz, seq, d), BF16)],
        compiler_params=_cparams("parallel", "parallel", "parallel"),
        name="hyena_in",
    )(x, x, x, row(g), shift, scale, w_in, w_in, w_in, row(b_in), row(b_in), row(b_in),
      conv_w, conv_w, conv_w, row(conv_b), row(conv_b), row(conv_b))


def _dft_tables(seq, blk):
    n = 2 * seq
    s = np.arange(seq, dtype=np.int64)[None, :]
    fl = np.arange(blk, dtype=np.int64)[:, None]
    fh = (np.arange(seq // blk, dtype=np.int64) * blk)[:, None]
    w = 2.0 * math.pi / n
    ang_b = ((fl * s) % n) * w
    ang_a = ((fh * s) % n) * w
    f32 = lambda m: jnp.asarray(m.astype(np.float32))
    return (f32(np.cos(ang_a)[:, None, :]), f32(np.sin(ang_a)[:, None, :]), f32(np.cos(ang_b)), f32(np.sin(ang_b)))


def _dft_gen_body(ca_ref, sa_ref, cb_ref, sb_ref, c_ref, s_ref):
    ca, sa, cb, sb = ca_ref[...], sa_ref[...], cb_ref[...], sb_ref[...]
    c_ref[...] = (ca * cb - sa * sb).astype(BF16)
    s_ref[...] = (sa * cb + ca * sb).astype(BF16)


def dft_matrices(seq):
    blk = min(seq, 256)
    ca, sa, cb, sb = _dft_tables(seq, blk)
    aspec = pl.BlockSpec((None, 1, seq), lambda i: (i, 0, 0))
    bspec = pl.BlockSpec((blk, seq), lambda i: (0, 0))
    ospec = pl.BlockSpec((blk, seq), lambda i: (i, 0))
    return pl.pallas_call(
        _dft_gen_body,
        grid=(seq // blk,),
        in_specs=[aspec, aspec, bspec, bspec],
        out_specs=[ospec, ospec],
        out_shape=[jax.ShapeDtypeStruct((seq, seq), BF16)] * 2,
        compiler_params=_cparams("parallel"),
        name="dft_matrices",
    )(ca, sa, cb, sb)


def _alt_sign(rows):
    return jnp.where((rows & 1) == 0, 1.0, -1.0).astype(F32)


def _filt_body(h2_ref, wf_ref, wb_ref, dl_ref, a_ref, d_ref, ny_ref):
    seq = h2_ref.shape[0]
    h2 = h2_ref[...]
    row = lax.broadcasted_iota(jnp.int32, (seq, 1), 0)
    t = row.astype(F32) * (1.0 / (seq - 1))
    win = jnp.exp(-t * dl_ref[...])
    hf = jnp.dot(h2, wf_ref[...], precision=HIGHEST, preferred_element_type=F32) * win
    hb = jnp.dot(h2, wb_ref[...], precision=HIGHEST, preferred_element_type=F32) * win
    hb = jnp.where(row == 0, 0.0, hb)
    nrm = (jnp.sum(jnp.abs(hf), axis=0, keepdims=True) + jnp.sum(jnp.abs(hb), axis=0, keepdims=True))
    inv = 1.0 / nrm
    a = (hf + hb) * inv
    a_ref[...] = a.astype(a_ref.dtype)
    d_ref[...] = ((hb - hf) * inv).astype(d_ref.dtype)
    ny = jnp.sum(a * _alt_sign(row), axis=0, keepdims=True) * (1.0 / (2 * seq))
    ny_ref[...] = jnp.broadcast_to(ny, ny_ref.shape)


def _khat_body(a_ref, d_ref, c_ref, s_ref, kr_ref, ki_ref):
    i = pl.program_id(1)
    tm = c_ref.shape[0]
    seq = c_ref.shape[1]
    f = i * tm + lax.broadcasted_iota(jnp.int32, (tm, 1), 0)
    w = jnp.where(f == 0, 1.0, 2.0).astype(F32) * (1.0 / (2 * seq))
    kr_ref[...] = _dot(c_ref[...], a_ref[...]) * w
    ki_ref[...] = _dot(s_ref[...], d_ref[...]) * w


def hyena_filter_taps(seq, fw1, fb1, fw2, fb2, fw3, freq, taps_dtype):
    d = fw3.shape[1] // 2
    bands_n = (fw1.shape[0] - 1) // 2
    t = np.linspace(0.0, 1.0, seq)[:, None]
    w = (2.0 * math.pi / seq) * np.arange(seq)[:, None]
    bands = np.linspace(1e-4, bands_n - 1, bands_n)[None, :]
    z = jnp.asarray(np.concatenate([t, np.cos(bands * w), -np.sin(bands * w)], axis=-1).astype(np.float32))
    h = jnp.sin(freq * (jnp.dot(z, fw1, precision=HIGHEST) + fb1))
    h2 = jnp.sin(freq * (jnp.dot(h, fw2, precision=HIGHEST) + fb2))
    max_decay = math.log(HY_DECAY_TARGET) / HY_FAST_PCT
    min_decay = math.log(HY_DECAY_TARGET) / HY_SLOW_PCT
    deltas = jnp.abs(jnp.linspace(min_decay, max_decay, d, dtype=F32))[None, :]

    order = h2.shape[1]
    tn = min(d, 256)
    nj = d // tn
    return pl.pallas_call(
        _filt_body,
        grid=(nj,),
        in_specs=[pl.BlockSpec((seq, order), lambda j: (0, 0)),
                  pl.BlockSpec((order, tn), lambda j: (0, j)),
                  pl.BlockSpec((order, tn), lambda j: (0, nj + j)),
                  pl.BlockSpec((1, tn), lambda j: (0, j))],
        out_specs=[pl.BlockSpec((seq, tn), lambda j: (0, j)),
                   pl.BlockSpec((seq, tn), lambda j: (0, j)),
                   pl.BlockSpec((SUBLANES, tn), lambda j: (0, j))],
        out_shape=[jax.ShapeDtypeStruct((seq, d), taps_dtype), jax.ShapeDtypeStruct((seq, d), taps_dtype),
                   jax.ShapeDtypeStruct((SUBLANES, d), F32)],
        compiler_params=_cparams("parallel"),
        name="hyena_filter_taps",
    )(h2, fw3, fw3, deltas)


def hyena_filter_dft(a, dd, cmat, smat):
    seq, d = a.shape
    tm = min(seq, 512)
    tn2 = min(d, 512)
    return pl.pallas_call(
        _khat_body,
        grid=(d // tn2, seq // tm),
        in_specs=[pl.BlockSpec((seq, tn2), lambda j, i: (0, j)),
                  pl.BlockSpec((seq, tn2), lambda j, i: (0, j)),
                  pl.BlockSpec((tm, seq), lambda j, i: (i, 0)),
                  pl.BlockSpec((tm, seq), lambda j, i: (i, 0))],
        out_specs=[pl.BlockSpec((tm, tn2), lambda j, i: (i, j))] * 2,
        out_shape=[jax.ShapeDtypeStruct((seq, d), F32)] * 2,
        compiler_params=_cparams("parallel", "parallel"),
        name="hyena_filter_dft",
    )(a, dd, cmat, smat)


def _dft_fwd_body(v_ref, c_ref, s_ref, kr_ref, ki_ref, kn_ref, ya_ref, yb_ref, yn_ref):
    i = pl.program_id(2)
    v = v_ref[...]
    vr = _dot(c_ref[...], v)
    p = _dot(s_ref[...], v)
    kr = kr_ref[...]
    ki = ki_ref[...]
    ya_ref[...] = (vr * kr + p * ki).astype(BF16)
    yb_ref[...] = (p * kr - vr * ki).astype(BF16)

    @pl.when(i == 0)
    def _():
        seq = v.shape[0]
        row = lax.broadcasted_iota(jnp.int32, (seq, 1), 0)
        vl = jnp.sum(v.astype(F32) * _alt_sign(row), axis=0, keepdims=True)
        yn_ref[...] = jnp.broadcast_to(vl * kn_ref[0:1, :], yn_ref.shape)


def _dft_inv_body(ya_ref, yb_ref, c_ref, s_ref, v_ref, x0_ref, skip_ref, yn_ref, o_ref):
    i = pl.program_id(2)
    tm = c_ref.shape[0]
    acc = _dot(c_ref[...], ya_ref[...]) + _dot(s_ref[...], yb_ref[...])
    t = i * tm + lax.broadcasted_iota(jnp.int32, (tm, 1), 0)
    y = acc + _alt_sign(t) * yn_ref[0:1, :] + skip_ref[...] * v_ref[...].astype(F32)
    o_ref[...] = (y * x0_ref[...].astype(F32)).astype(BF16)


def hyena_conv(v, x0, skip, kr, ki, kn, cmat, smat):
    bsz, seq, d = v.shape
    tm = min(seq, 512)
    tn = min(d, 512)
    grid = (bsz, d // tn, seq // tm)
    full = pl.BlockSpec((None, seq, tn), lambda b, j, i: (b, 0, j))
    mat = pl.BlockSpec((tm, seq), lambda b, j, i: (i, 0))
    tile = pl.BlockSpec((None, tm, tn), lambda b, j, i: (b, i, j))
    ktile = pl.BlockSpec((tm, tn), lambda b, j, i: (i, j))
    nyq = pl.BlockSpec((None, SUBLANES, tn), lambda b, j, i: (b, 0, j))
    ya, yb, yn = pl.pallas_call(
        _dft_fwd_body,
        grid=grid,
        in_specs=[full, mat, mat, ktile, ktile, pl.BlockSpec((SUBLANES, tn), lambda b, j, i: (0, j))],
        out_specs=[tile, tile, nyq],
        out_shape=[jax.ShapeDtypeStruct((bsz, seq, d), BF16)] * 2
        + [jax.ShapeDtypeStruct((bsz, SUBLANES, d), F32)],
        compiler_params=_cparams("parallel", "parallel", "arbitrary"),
        name="hyena_dft_fwd",
    )(v, cmat, smat, kr, ki, kn)
    return pl.pallas_call(
        _dft_inv_body,
        grid=grid,
        in_specs=[full, full, mat, mat, tile, tile, pl.BlockSpec((1, tn), lambda b, j, i: (0, j)), nyq],
        out_specs=tile,
        out_shape=jax.ShapeDtypeStruct((bsz, seq, d), BF16),
        compiler_params=_cparams("parallel", "parallel", "parallel"),
        name="hyena_dft_inv",
    )(ya, yb, cmat, smat, v, x0, skip.reshape(1, d), yn)


FFT_N2 = 128
FFT_MIN_SEQ = 1024
FFT_UNROLL = 8


def _fft_matrices(seq):
    n = 2 * seq
    n2 = FFT_N2
    n1 = n // n2
    r8 = SUBLANES
    q = np.arange(n2 // r8, dtype=np.int64)[:, None, None, None]
    f1 = np.arange(n1, dtype=np.int64)[None, :, None, None]
    r = np.arange(r8, dtype=np.int64)[None, None, :, None]
    t1 = np.arange(n1 // 2, dtype=np.int64)[None, None, None, :]
    ang = ((f1 * (t1 * n2 + q * r8 + r)) % n) * (2.0 * math.pi / n)
    g = np.stack([np.cos(ang), -np.sin(ang)], axis=3)
    eye = np.eye(r8)[None, None, :, None, None, :]
    ma = (g[..., None] * eye).reshape(n2 // r8, n1 * r8 * 2, (n1 // 2) * r8).astype(np.float32)
    f2 = np.arange(n2, dtype=np.int64)[:, None]
    t2 = np.arange(n2, dtype=np.int64)[None, :]
    th = ((f2 * t2) % n2) * (2.0 * math.pi / n2)
    co, si = np.cos(th), np.sin(th)
    wc = np.stack([np.stack([co, si], axis=-1), np.stack([-si, co], axis=-1)], axis=0)
    wc = wc.reshape(2 * n2, 2 * n2).astype(np.float32)
    as_bf16 = lambda m: jnp.asarray(np.ascontiguousarray(m).astype(BF16))
    return as_bf16(ma), as_bf16(np.swapaxes(ma, 1, 2)), as_bf16(wc), as_bf16(wc.T)


def _fft_stage_a(x_ref, ma_ref, s1):
    n1h, n_q, r8, tn = x_ref.shape
    n1 = s1.shape[0]

    def body(q, carry):
        x = x_ref[:, pl.ds(q, 1), :, :].reshape(n1h * r8, tn).astype(BF16)
        a = _dot(ma_ref[q], x).astype(BF16)
        s1[:, pl.ds(pl.multiple_of(q * 2 * r8, 2 * r8), 2 * r8), :] = a.reshape(n1, 2 * r8, tn)
        return carry

    lax.fori_loop(0, n_q, body, 0, unroll=FFT_UNROLL)


def _fft_conv_body(v_ref, x0_ref, k_ref, skip_ref, ma_ref, mat_ref, wc_ref, wci_ref, o_ref, s1, ysc):
    n1h, n_q, r8, tn = v_ref.shape
    n1 = s1.shape[0]
    n2 = s1.shape[1] // 2
    seq = n1h * n_q * r8
    _fft_stage_a(v_ref, ma_ref, s1)

    def slab(f, carry):
        y = _dot(wc_ref[...], s1[f])
        yr, yi = y[:n2], y[n2:]
        kr = k_ref[f, 0].astype(F32)
        ki = k_ref[f, 1].astype(F32)
        p = jnp.concatenate([yr * kr - yi * ki, yr * ki + yi * kr], axis=0).astype(BF16)
        s1[f] = _dot(wci_ref[...], p).astype(BF16)
        return carry

    lax.fori_loop(0, n1, slab, 0, unroll=2 * FFT_UNROLL)

    def inv_a(q, carry):
        z = s1[:, pl.ds(pl.multiple_of(q * 2 * r8, 2 * r8), 2 * r8), :].reshape(n1 * 2 * r8, tn)
        ysc[:, pl.ds(q, 1), :, :] = _dot(mat_ref[q], z).reshape(n1h, 1, r8, tn)
        return carry

    lax.fori_loop(0, n_q, inv_a, 0, unroll=FFT_UNROLL)
    y = ysc[...].reshape(seq, tn) + skip_ref[...] * v_ref[...].reshape(seq, tn)
    o_ref[...] = (y * x0_ref[...].astype(F32)).astype(BF16)


def _fft_filter_body(a_ref, d_ref, ma_ref, wc_ref, k_ref, s1):
    n1 = s1.shape[0]
    n2 = s1.shape[1] // 2
    scale = 1.0 / (n1 * n2)
    for src_ref, part, sign in ((a_ref, 0, scale), (d_ref, 1, -scale)):
        _fft_stage_a(src_ref, ma_ref, s1)

        def slab(f, carry):
            y = _dot(wc_ref[part * n2:(part + 1) * n2, :], s1[f])
            k_ref[f, part] = (y * sign).astype(BF16)
            return carry

        lax.fori_loop(0, n1, slab, 0, unroll=FFT_UNROLL)


def hyena_conv_fft(v, x0, skip, a, dd):
    bsz, seq, d = v.shape
    n2 = FFT_N2
    n1 = 2 * seq // n2
    n_q = n2 // SUBLANES
    tn = min(d, 256)
    ma, mat, wc, wci = _fft_matrices(seq)
    const = lambda shape: pl.BlockSpec(shape, lambda *_: (0,) * len(shape), pipeline_mode=pl.Buffered(1))
    view = lambda t: t.reshape(t.shape[:-2] + (n1 // 2, n_q, SUBLANES, d))
    tap = pl.BlockSpec((n1 // 2, n_q, SUBLANES, tn), lambda j: (0, 0, 0, j))
    khat = pl.pallas_call(
        _fft_filter_body,
        grid=(d // tn,),
        in_specs=[tap, tap, const(ma.shape), const(wc.shape)],
        out_specs=pl.BlockSpec((n1, 2, n2, tn), lambda j: (0, 0, 0, j)),
        out_shape=jax.ShapeDtypeStruct((n1, 2, n2, d), BF16),
        scratch_shapes=[pltpu.VMEM((n1, 2 * n2, tn), BF16)],
        compiler_params=_cparams("parallel"),
        name="hyena_filter_fft",
    )(view(a), view(dd), ma, wc)
    return pl.pallas_call(
        _fft_conv_body,
        grid=(d // tn, bsz),
        in_specs=[pl.BlockSpec((None, n1 // 2, n_q, SUBLANES, tn), lambda j, b: (b, 0, 0, 0, j)),
                  pl.BlockSpec((None, seq, tn), lambda j, b: (b, 0, j)),
                  pl.BlockSpec((n1, 2, n2, tn), lambda j, b: (0, 0, 0, j), pipeline_mode=pl.Buffered(1)),
                  pl.BlockSpec((1, tn), lambda j, b: (0, j)),
                  const(ma.shape), const(mat.shape), const(wc.shape), const(wci.shape)],
        out_specs=pl.BlockSpec((None, seq, tn), lambda j, b: (b, 0, j)),
        out_shape=jax.ShapeDtypeStruct((bsz, seq, d), BF16),
        scratch_shapes=[pltpu.VMEM((n1, 2 * n2, tn), BF16), pltpu.VMEM((n1 // 2, n_q, SUBLANES, tn), F32)],
        compiler_params=_cparams("parallel", "arbitrary"),
        name="hyena_conv_fft",
    )(view(v), x0, khat, skip.reshape(1, d), ma, mat, wc, wci)


def _mm_res_body(x_ref, w_ref, b_ref, res_ref, gate_ref, o_ref):
    o_ref[...] = res_ref[...] + gate_ref[...] * (_dot(x_ref[...], w_ref[...]) + b_ref[...])


def mm_residual(x, w, b, res, gate):
    bsz, seq, k = x.shape
    n = w.shape[1]
    tm = min(seq, 512)
    return pl.pallas_call(
        _mm_res_body,
        grid=(bsz, seq // tm),
        in_specs=[pl.BlockSpec((None, tm, k), lambda b, i: (b, i, 0)),
                  pl.BlockSpec((k, n), lambda b, i: (0, 0)),
                  pl.BlockSpec((1, n), lambda b, i: (0, 0)),
                  pl.BlockSpec((None, tm, n), lambda b, i: (b, i, 0)),
                  pl.BlockSpec((None, 1, n), lambda b, i: (b, 0, 0))],
        out_specs=pl.BlockSpec((None, tm, n), lambda b, i: (b, i, 0)),
        out_shape=jax.ShapeDtypeStruct((bsz, seq, n), F32),
        compiler_params=_cparams("parallel", "parallel"),
        name="mm_residual",
    )(x, w, b.reshape(1, n), res, gate)


def _moe_pre_body(*refs, n_groups, n_experts, tile_offs):
    n_streams = len(tile_offs) - 1
    g_ref, wr_ref, br_ref, tok_ref, eid_ref, gate_ref = refs[3 * n_streams:]
    i = pl.program_id(0)
    for k in range(n_streams):
        x_ref, sh_ref, sc_ref = refs[3 * k:3 * k + 3]

        @pl.when(jnp.logical_and(i >= tile_offs[k], i < tile_offs[k + 1]))
        def _():
            tok = _norm_mod(x_ref[...], g_ref[...], sh_ref[...], sc_ref[...])
            _route_tokens(tok, wr_ref, br_ref, tok_ref, eid_ref, gate_ref, n_groups, n_experts)


def _route_tokens(tok, wr_ref, br_ref, tok_ref, eid_ref, gate_ref, n_groups, n_experts):
    tok_ref[...] = tok
    t_hi = tok.astype(BF16)
    t_lo = (tok - t_hi.astype(F32)).astype(BF16)
    logits = (_dot(t_hi, wr_ref[0]) + _dot(t_hi, wr_ref[1]) + _dot(t_lo, wr_ref[0])) + br_ref[...]
    lane = lax.broadcasted_iota(jnp.int32, logits.shape, 1)
    per = n_experts // n_groups
    big = jnp.int32(1 << 20)
    gmask = jnp.logical_and(lane >= n_experts, lane < n_experts + n_groups)
    gl = jnp.where(gmask, logits, NEG_BIG)
    gmax = jnp.max(gl, axis=-1, keepdims=True)
    gidx = jnp.min(jnp.where(gl == gmax, lane - n_experts, big), axis=-1, keepdims=True)
    p_top = 1.0 / jnp.sum(jnp.where(gmask, jnp.exp(gl - gmax), 0.0), axis=-1, keepdims=True)
    lo = gidx * per
    emask = jnp.logical_and(lane >= lo, lane < lo + per)
    el = jnp.where(emask, logits, NEG_BIG)
    m1 = jnp.max(el, axis=-1, keepdims=True)
    i1 = jnp.min(jnp.where(el == m1, lane, big), axis=-1, keepdims=True)
    el2 = jnp.where(lane == i1, NEG_BIG, el)
    m2 = jnp.max(el2, axis=-1, keepdims=True)
    i2 = jnp.min(jnp.where(el2 == m2, lane, big), axis=-1, keepdims=True)
    e21 = jnp.exp(m2 - m1)
    g1 = p_top / (1.0 + e21)
    g2 = g1 * e21
    ids = jnp.where(lane == 0, i1, jnp.where(lane == 1, i2, -1))
    ids_t = ids.T
    for h in range(eid_ref.shape[0]):
        eid_ref[h] = ids_t[0:SUBLANES, h * MOE_TM:(h + 1) * MOE_TM]
    gate_ref[...] = jnp.where(lane == 0, g1, jnp.where(lane == 1, g2, 0.0))


def _split_bf16(w):
    hi = w.astype(BF16)
    return jnp.stack([hi, (w - hi.astype(F32)).astype(BF16)])


def _router_weights(wg, bg, we, be):
    pad = LANES - we.shape[1] - wg.shape[1]
    wr = jnp.pad(jnp.concatenate([we, wg], axis=1), ((0, 0), (0, pad)))
    br = jnp.pad(jnp.concatenate([be, bg]), (0, pad)).reshape(1, LANES)
    return wr, br


def moe_pre(streams, g, wr, br, n_groups, n_experts):
    d = streams[0][0].shape[2]
    tm = MOE_TM
    tile_offs = [0]
    in_specs, args = [], []
    for x, shift, scale in streams:
        bsz, seq, _ = x.shape
        nt = seq // tm
        n_tiles = bsz * nt
        off = tile_offs[-1]
        tile_offs.append(off + n_tiles)

        def tile(i, off=off, n_tiles=n_tiles):
            return jnp.clip(i - off, 0, n_tiles - 1)

        in_specs += [pl.BlockSpec((None, tm, d), lambda i, tile=tile, nt=nt: (tile(i) // nt, tile(i) % nt, 0)),
                     pl.BlockSpec((None, 1, d), lambda i, tile=tile, nt=nt: (tile(i) // nt, 0, 0)),
                     pl.BlockSpec((None, 1, d), lambda i, tile=tile, nt=nt: (tile(i) // nt, 0, 0))]
        args += [x, shift, scale]
    in_specs += [pl.BlockSpec((1, d), lambda i: (0, 0)),
                 pl.BlockSpec((2, d, LANES), lambda i: (0, 0, 0)),
                 pl.BlockSpec((1, LANES), lambda i: (0, 0))]
    args += [g.reshape(1, d), _split_bf16(wr), br]
    total = tile_offs[-1] * tm
    rout = pl.BlockSpec((tm, LANES), lambda i: (i, 0))
    tok, eid, gate = pl.pallas_call(
        functools.partial(_moe_pre_body, n_groups=n_groups, n_experts=n_experts, tile_offs=tuple(tile_offs)),
        grid=(tile_offs[-1],),
        in_specs=in_specs,
        out_specs=[pl.BlockSpec((tm, d), lambda i: (i, 0)),
                   pl.BlockSpec((1, SUBLANES, tm), lambda i: (i, 0, 0)), rout],
        out_shape=[jax.ShapeDtypeStruct((total, d), F32),
                   jax.ShapeDtypeStruct((tile_offs[-1], SUBLANES, tm), jnp.int32),
                   jax.ShapeDtypeStruct((total, LANES), F32)],
        compiler_params=_cparams("parallel"),
        name="moe_pre",
    )(*args)
    return tok, eid, gate, tile_offs[:-1]


def _start_row_gather(row_index, n_rows, src_hbm, dst_vmem, sem):
    def body(g, c):
        r0 = pl.multiple_of(g * SUBLANES, SUBLANES)
        dst_tile = dst_vmem.at[pl.ds(r0, SUBLANES)]
        for k in range(SUBLANES):
            pltpu.make_async_copy(src_hbm.at[pl.ds(row_index(r0 + k), 1)], dst_tile.at[pl.ds(k, 1)], sem).start()
        return c

    lax.fori_loop(0, n_rows // SUBLANES, body, 0, unroll=2)


def _wait_row_gather(n_rows, src_hbm, dst_vmem, sem):
    pltpu.make_async_copy(src_hbm.at[pl.ds(0, n_rows)], dst_vmem, sem).wait()


def _expert_body(bv_ref, rk_ref, pe_ref, tot_ref, p0_ref, ts_hbm, tok_ref, wg_hbm, wu_hbm, wd_hbm, o_ref,
                 xbuf, xsem, win, isem, wcache, stg, wsem, cnt, *, layer):
    i = pl.program_id(0)
    n = pl.num_programs(0)
    slot = i % 2
    cr, cc = stg.shape[1:]
    total = tot_ref[0]
    mats_hbm = (wg_hbm, wu_hbm, wd_hbm)

    @pl.when(i == 0)
    def _():
        cnt[0] = 0
        cnt[1] = 0

    def live(b):
        return jnp.logical_and(b < n, bv_ref[jnp.minimum(b, n - 1)] > 0)

    def window(b):
        s = b % 2
        lo = pl.multiple_of((p0_ref[jnp.minimum(b, n - 1)] // MOE_WIN_ALIGN) * MOE_WIN_ALIGN, MOE_WIN_ALIGN)
        return pltpu.make_async_copy(ts_hbm.at[pl.ds(lo, MOE_WIN)],
                                     win.at[pl.ds(pl.multiple_of(s * MOE_WIN, MOE_WIN), MOE_WIN)], isem.at[s])

    def gather(b):
        s = b % 2
        base = s * MOE_WIN + p0_ref[jnp.minimum(b, n - 1)] % MOE_WIN_ALIGN
        _start_row_gather(lambda r: win[base + r], MOE_BM, tok_ref, xbuf.at[s], xsem.at[s])

    @pl.when(jnp.logical_and(i == 0, live(0)))
    def _():
        window(0).start()
        window(0).wait()
        gather(0)

    @pl.when(jnp.logical_and(i == 0, live(1)))
    def _():
        window(1).start()

    @pl.when(live(i + 1))
    def _():
        window(i + 1).wait()
        gather(i + 1)

    @pl.when(live(i + 2))
    def _():
        window(i + 2).start()

    def chunk_geom(c):
        q = c % MOE_NCH
        m = q // 4
        sub = q % 4
        r0 = jnp.where(m < 2, sub, sub // 2) * cr
        c0 = jnp.where(m < 2, 0, sub % 2) * cc
        return m, pl.multiple_of(r0, cr), pl.multiple_of(c0, cc)

    def issue(c):
        e = pe_ref[c // MOE_NCH]
        m, r0, c0 = chunk_geom(c)
        s = c % MOE_STAGE
        for k, w_hbm in enumerate(mats_hbm):
            @pl.when(m == k)
            def _():
                pltpu.make_async_copy(w_hbm.at[layer, e, pl.ds(r0, cr), pl.ds(c0, cc)], stg.at[s],
                                      wsem.at[s]).start()

    def cast(c):
        s = c % MOE_STAGE
        pltpu.make_async_copy(wg_hbm.at[layer, 0, pl.ds(0, cr), pl.ds(0, cc)], stg.at[s], wsem.at[s]).wait()
        ws = (c // MOE_NCH) % 2
        q = c % MOE_NCH
        step = min(MOE_CAST_ROWS, cr)
        assert cr % step == 0

        def slab(k, carry):
            rows = pl.ds(pl.multiple_of(k * step, step), step)
            wcache[ws, q, rows, :] = stg[s, rows, :].astype(BF16)
            return carry

        lax.fori_loop(0, cr // step, slab, 0)

    valid = bv_ref[i] > 0
    rank = rk_ref[i]
    issued = cnt[0]
    done = cnt[1]
    limit = jnp.minimum(total, MOE_NCH * (rank + 2))
    need = jnp.where(valid, MOE_NCH * (rank + 1), done)

    def fill(issued, done):
        hi = jnp.minimum(limit, done + MOE_STAGE)

        def body(c, carry):
            issue(c)
            return carry

        lax.fori_loop(issued, hi, body, 0)
        return jnp.maximum(issued, hi)

    def cast_and_refill(c, issued):
        cast(c)
        more = issued < jnp.minimum(limit, c + 1 + MOE_STAGE)

        @pl.when(more)
        def _():
            issue(issued)

        return issued + more.astype(jnp.int32)

    issued = fill(issued, done)
    issued = lax.fori_loop(done, need, cast_and_refill, issued)
    done = jnp.maximum(done, need)

    @pl.when(valid)
    def _():
        ws = rank % 2
        _wait_row_gather(MOE_BM, tok_ref, xbuf.at[slot], xsem.at[slot])
        x = xbuf[slot].astype(BF16)
        gate = sum(_dot(x[:, k * cr:(k + 1) * cr], wcache[ws, k]) for k in range(4))
        up = sum(_dot(x[:, k * cr:(k + 1) * cr], wcache[ws, 4 + k]) for k in range(4))
        h = (gate * jax.nn.sigmoid(gate) * up).astype(BF16)
        for half in range(2):
            o_ref[:, half * cc:(half + 1) * cc] = sum(
                _dot(h[:, k * cr:(k + 1) * cr], wcache[ws, 8 + 2 * k + half]) for k in range(2))

    @pl.when(jnp.logical_not(valid))
    def _():
        o_ref[...] = jnp.zeros_like(o_ref)

    fetched = issued
    issued = lax.fori_loop(done, fetched, cast_and_refill, issued)
    done = jnp.maximum(done, fetched)
    last = i == n - 1
    tail = jnp.where(last, issued, done)

    def drain(c, carry):
        cast(c)
        return carry

    lax.fori_loop(done, tail, drain, 0)
    cnt[0] = issued
    cnt[1] = jnp.maximum(done, tail)


def moe_experts(tok, tok_sorted, block_p0, block_valid, block_rank, present, n_chunks, w_gate, w_up, w_down,
                layer):
    d = tok.shape[1]
    n_blocks = block_valid.shape[0]
    n_rows = n_blocks * MOE_BM
    dh = w_gate.shape[3]
    assert 2 * dh == d and MOE_NCH == 12
    cr, cc = d // 4, dh
    any_spec = pl.BlockSpec(memory_space=pl.ANY)
    grid_spec = pltpu.PrefetchScalarGridSpec(
        num_scalar_prefetch=5,
        grid=(n_blocks,),
        in_specs=[any_spec, any_spec, any_spec, any_spec, any_spec],
        out_specs=pl.BlockSpec((MOE_BM, d), lambda i, *_: (i, 0)),
        scratch_shapes=[pltpu.VMEM((2, MOE_BM, d), F32), pltpu.SemaphoreType.DMA((2,)),
                        pltpu.SMEM((2 * MOE_WIN,), jnp.int32), pltpu.SemaphoreType.DMA((2,)),
                        pltpu.VMEM((2, MOE_NCH, cr, cc), BF16),
                        pltpu.VMEM((MOE_STAGE, cr, cc), F32), pltpu.SemaphoreType.DMA((MOE_STAGE,)),
                        pltpu.SMEM((2,), jnp.int32)],
    )
    return pl.pallas_call(
        functools.partial(_expert_body, layer=layer),
        grid_spec=grid_spec,
        out_shape=jax.ShapeDtypeStruct((n_rows, d), F32),
        compiler_params=_cparams("arbitrary"),
        name="moe_experts",
    )(block_valid, block_rank, present, n_chunks, block_p0, tok_sorted, tok, w_gate, w_up, w_down)


def _combine_body(dest_ref, nxt_ref, os_ref, gate_ref, res_ref, gt_ref, fg_ref, *rest, final_norm, with_next):
    if with_next:
        ng_ref, nsh_ref, nsc_ref, o_ref, h_ref, buf, sem = rest
    else:
        o_ref, buf, sem = rest
    rows = res_ref.shape[0]
    i = pl.program_id(0)
    n = pl.num_programs(0)
    slot = i % 2

    def start(idx_ref, s):
        for k in range(TOP_K):
            _start_row_gather(lambda r, k=k: idx_ref[k * rows + r], rows, os_ref, buf.at[s, k], sem.at[s])

    @pl.when(i == 0)
    def _():
        start(dest_ref, 0)

    @pl.when(i + 1 < n)
    def _():
        start(nxt_ref, 1 - slot)

    for k in range(TOP_K):
        _wait_row_gather(rows, os_ref, buf.at[slot, k], sem.at[slot])
    gates = gate_ref[...]
    mo = gates[:, 0:1] * buf[slot, 0] + gates[:, 1:2] * buf[slot, 1]
    y = res_ref[...] + gt_ref[...] * mo
    if final_norm:
        ms = jnp.mean(y * y, axis=-1, keepdims=True)
        y = y * lax.rsqrt(ms + NORM_EPS) * fg_ref[...]
    o_ref[...] = y
    if with_next:
        h_ref[...] = _norm_mod(y, ng_ref[...], nsh_ref[...], nsc_ref[...]).astype(h_ref.dtype)


def moe_combine(os, dest, gates, tile0, res, gt, final_g, final_norm, next_mod=None):
    bsz, seq, d = res.shape
    rows = MOE_TM
    nt = seq // rows
    n = bsz * nt
    tile = pl.BlockSpec((None, rows, d), lambda i: (i // nt, i % nt, 0))
    mspec = pl.BlockSpec((None, 1, d), lambda i: (i // nt, 0, 0))
    rspec = pl.BlockSpec((1, d), lambda i: (0, 0))
    dspec = lambda step: pl.BlockSpec((SUBLANES * rows,), lambda i: (tile0 + step(i),), memory_space=pltpu.SMEM)
    in_specs = [dspec(lambda i: i), dspec(lambda i: jnp.minimum(i + 1, n - 1)),
                pl.BlockSpec(memory_space=pl.ANY),
                pl.BlockSpec((rows, LANES), lambda i: (tile0 + i, 0)),
                tile, mspec, rspec]
    args = [dest, dest, os, gates, res, gt, final_g.reshape(1, d)]
    out_specs, out_shape = [tile], [jax.ShapeDtypeStruct((bsz, seq, d), F32)]
    if next_mod is not None:
        in_specs += [rspec, mspec, mspec]
        args += [next_mod[0].reshape(1, d), next_mod[1], next_mod[2]]
        out_specs.append(tile)
        out_shape.append(jax.ShapeDtypeStruct((bsz, seq, d), BF16))
    outs = pl.pallas_call(
        functools.partial(_combine_body, final_norm=final_norm, with_next=next_mod is not None),
        grid=(n,),
        in_specs=in_specs,
        out_specs=out_specs,
        out_shape=out_shape,
        scratch_shapes=[pltpu.VMEM((2, TOP_K, rows, d), F32), pltpu.SemaphoreType.DMA((2,))],
        compiler_params=_cparams("arbitrary"),
        name="moe_combine",
    )(*args)
    return outs if next_mod is not None else outs[0]


def _plan_body(e_ref, dest_ref, tab_ref, blk_ref, present_ref, *, n_experts):
    n_rows, w = e_ref.shape
    e_all = e_ref[...]
    li = lax.broadcasted_iota(jnp.int32, (w, w), 0)
    lj = lax.broadcasted_iota(jnp.int32, (w, w), 1)
    incl = (li <= lj).astype(BF16)
    ri = lax.broadcasted_iota(jnp.int32, (n_rows, n_rows), 0)
    rj = lax.broadcasted_iota(jnp.int32, (n_rows, n_rows), 1)
    before = (rj < ri).astype(BF16)
    elane = lax.broadcasted_iota(jnp.int32, (n_rows, LANES), 1)
    row_tot = jnp.zeros((n_rows, LANES), F32)
    for e in range(n_experts):
        tot = jnp.sum((e_all == e).astype(F32), axis=1, keepdims=True)
        row_tot = row_tot + jnp.where(elane == e, tot, 0.0)
    rows_before = _dot(before, row_tot.astype(BF16))
    counts = jnp.sum(row_tot, axis=0, keepdims=True).astype(jnp.int32)
    lane1 = lax.broadcasted_iota(jnp.int32, (1, LANES), 1)

    def excl_prefix(v):
        acc = v
        sh = 1
        while sh < LANES:
            acc = acc + jnp.where(lane1 >= sh, pltpu.roll(acc, sh, axis=1), 0)
            sh *= 2
        return acc - v

    start = excl_prefix(counts)
    padded = (counts + (MOE_BM - 1)) // MOE_BM * MOE_BM
    pad_start = excl_prefix(padded)
    pad_end = pad_start + padded
    has = (counts > 0).astype(jnp.int32)
    rank = excl_prefix(has)
    n_chunks = MOE_NCH * jnp.sum(has.astype(F32), axis=1, keepdims=True).astype(jnp.int32)
    tab_ref[...] = jnp.concatenate([counts, start, pad_start, pad_end, jnp.broadcast_to(n_chunks, (1, LANES)),
                                    jnp.zeros((SUBLANES - 5, LANES), jnp.int32)], axis=0)
    nb = blk_ref.shape[0]
    f32 = lambda v: v.astype(F32)
    lane_b = lax.broadcasted_iota(jnp.int32, (nb, LANES), 1)
    first_row = lax.broadcasted_iota(jnp.int32, (nb, 1), 0) * MOE_BM
    is_expert = lane_b < n_experts
    expert = jnp.sum(f32(jnp.logical_and(is_expert, pad_end <= first_row)), axis=1, keepdims=True)
    expert = jnp.minimum(expert.astype(jnp.int32), n_experts - 1)
    mine = lane_b == expert
    pick = lambda v: jnp.sum(jnp.where(mine, f32(v), 0.0), axis=1, keepdims=True).astype(jnp.int32)
    live = first_row < jnp.max(f32(pad_end), axis=1, keepdims=True).astype(jnp.int32)
    p0 = jnp.where(live, pick(start) - pick(pad_start) + first_row, 0)
    blk_ref[...] = jnp.where(lane_b == 0, live.astype(jnp.int32),
                             jnp.where(lane_b == 1, pick(rank), jnp.where(lane_b == 2, p0, 0)))
    row_r = lax.broadcasted_iota(jnp.int32, (LANES, LANES), 0)
    lane_e = lax.broadcasted_iota(jnp.int32, (LANES, LANES), 1)
    hit_r = jnp.logical_and(has > 0, rank == row_r)
    present = jnp.sum(jnp.where(hit_r, f32(lane_e), 0.0), axis=1, keepdims=True).astype(jnp.int32)
    present_ref[...] = jnp.broadcast_to(present, (LANES, LANES))
    base = rows_before + pad_start.astype(F32)
    dest = jnp.zeros((n_rows, w), F32)
    for e in range(n_experts):
        hit = e_all == e
        within = _dot(hit.astype(BF16), incl)
        dest = dest + jnp.where(hit, within - 1.0 + base[:, e:e + 1], 0.0)
    dest_ref[...] = dest.astype(jnp.int32)


def _route_plan(eid, n_experts):
    n_tiles, r8, tm = eid.shape
    a = n_tiles * TOP_K * tm
    n_blocks = -(-a // MOE_BM) + n_experts
    assert (tm & (tm - 1)) == 0 and TOP_K == 2
    nb_pad = -(-n_blocks // SUBLANES) * SUBLANES
    dest, tab, blk, present = pl.pallas_call(
        functools.partial(_plan_body, n_experts=n_experts),
        out_shape=[jax.ShapeDtypeStruct((n_tiles * r8, tm), jnp.int32),
                   jax.ShapeDtypeStruct((SUBLANES, LANES), jnp.int32),
                   jax.ShapeDtypeStruct((nb_pad, LANES), jnp.int32),
                   jax.ShapeDtypeStruct((LANES, LANES), jnp.int32)],
        compiler_params=pltpu.CompilerParams(vmem_limit_bytes=V7X_VMEM_LIMIT_BYTES),
        name="moe_route_plan",
    )(eid.reshape(n_tiles * r8, tm))
    e_flat = eid[:, :TOP_K, :].reshape(-1)
    order = jnp.argsort(e_flat).astype(jnp.int32)
    shift = tm.bit_length() - 1
    tok_of = ((order >> (shift + 1)) << shift) | (order & (tm - 1))
    slack = -a % MOE_WIN_ALIGN + MOE_WIN
    tok_sorted = jnp.concatenate([tok_of, jnp.zeros((slack,), jnp.int32)])
    blk = blk[:n_blocks]
    return dest.reshape(-1), tok_sorted, blk[:, 2], blk[:, 0], blk[:, 1], present[:n_experts, 0], tab[4, :1]


def hier_moe(streams, norm_g, wg, bg, we, be, w_gate, w_up, w_down, layer, final_g, final_norm, next_mods=None,
             routed=None):
    n_groups = wg.shape[1]
    n_experts = we.shape[1]
    if routed is None:
        wr, br = _router_weights(wg, bg, we, be)
        tok, eid, gates, tile0s = moe_pre([s[:3] for s in streams], norm_g, wr, br, n_groups, n_experts)
    else:
        (tok, eid, gates), tile0s = routed, [0]
    dest, tok_sorted, block_p0, block_valid, block_rank, present, n_chunks = _route_plan(eid, n_experts)
    os = moe_experts(tok, tok_sorted, block_p0, block_valid, block_rank, present, n_chunks, w_gate, w_up, w_down,
                     layer)
    next_mods = next_mods or [None] * len(streams)
    return [moe_combine(os, dest, gates, tile0, x, gt, final_g, final_norm, nm)
            for (x, _, _, gt), tile0, nm in zip(streams, tile0s, next_mods)]


def _norm_mod_body(x_ref, g_ref, sh_ref, sc_ref, o_ref):
    o_ref[...] = _norm_mod(x_ref[...], g_ref[...], sh_ref[...], sc_ref[...]).astype(o_ref.dtype)


def norm_mod(x, g, shift, scale):
    bsz, seq, d = x.shape
    tm = min(seq, 512)
    mspec = pl.BlockSpec((None, 1, d), lambda b, i: (b, 0, 0))
    return pl.pallas_call(
        _norm_mod_body,
        grid=(bsz, seq // tm),
        in_specs=[pl.BlockSpec((None, tm, d), lambda b, i: (b, i, 0)),
                  pl.BlockSpec((1, d), lambda b, i: (0, 0)), mspec, mspec],
        out_specs=pl.BlockSpec((None, tm, d), lambda b, i: (b, i, 0)),
        out_shape=jax.ShapeDtypeStruct((bsz, seq, d), BF16),
        compiler_params=_cparams("parallel", "parallel"),
        name="norm_mod",
    )(x, g.reshape(1, d), shift, scale)


def _s5_arrange(h):
    bsz, t, d = h.shape
    c = t // (S5_SEGS * S5_TAU)
    h = h.reshape(bsz, S5_SEGS, c, S5_TAU, d // LANES, LANES)
    return h.transpose(2, 0, 1, 4, 3, 5).reshape(c * bsz * S5_SEGS, d * S5_TAU)


def _s5_unarrange(y, bsz):
    r, w = y.shape
    d = w // S5_TAU
    c = r // (bsz * S5_SEGS)
    y = y.reshape(c, bsz, S5_SEGS, d // LANES, S5_TAU, LANES)
    return y.transpose(1, 2, 0, 4, 3, 5).reshape(bsz, S5_SEGS * c * S5_TAU, d)


def _s5_operators(a_re, a_im, log_step, b_re, b_im, c_re, c_im):
    n_g, n_p = a_re.shape[1:]
    n_h = b_re.shape[-1]
    gpt = LANES // n_h
    n_j = n_g // gpt
    tau = S5_TAU
    assert tau * n_h == LANES and 2 * n_p == LANES
    lam_step = lax.complex(a_re, a_im) * jnp.exp(log_step)[..., None]
    lam_bar = jnp.exp(lam_step)
    b_bar = ((lam_bar - 1.0) / lax.complex(a_re, a_im))[..., None] * lax.complex(b_re, b_im)
    c_mat = lax.complex(c_re, c_im)
    ks = jnp.arange(tau + 1, dtype=F32)[None, :, None, None]
    pw = jnp.exp(lam_step[:, None] * ks)
    ein = functools.partial(jnp.einsum, precision=HIGHEST)
    inj_c, cl_c, lt = [], [], []
    tz_c = 0.0
    for d in range(2):
        pos = jnp.arange(tau) if d == 0 else jnp.arange(tau)[::-1]
        inj = (pw[d][tau - 1 - pos][..., None] * b_bar[d][None]).reshape(tau, n_j, gpt, n_p, n_h)
        inj = inj.transpose(1, 0, 2, 4, 3).reshape(n_j, tau * LANES, n_p)
        inj_c.append(jnp.concatenate([inj.real, inj.imag], axis=-1))
        cl = (c_mat[d][None] * pw[d][pos + 1][:, :, None, :]).reshape(tau, n_j, gpt, n_h, n_p)
        cl = cl.transpose(1, 2, 4, 0, 3).reshape(n_j, gpt * n_p, tau * n_h)
        cl_c.append(jnp.concatenate([cl.real, -cl.imag], axis=1))
        mk = ein('gop,kgp,gph->kgoh', c_mat[d], pw[d][:tau], b_bar[d]).real
        diff = pos[:, None] - pos[None, :]
        tz = jnp.where((diff >= 0)[:, :, None, None, None], mk[jnp.clip(diff, 0, tau - 1)], 0.0)
        tz = tz.reshape(tau, tau, n_j, gpt, n_h, n_h)
        tz_c = tz_c + tz.transpose(2, 1, 3, 5, 0, 4).reshape(n_j, tau * LANES, tau * n_h)
        lt_d = pw[d][tau].reshape(n_j, 1, gpt * n_p)
        lt.append(jnp.concatenate([lt_d.real, lt_d.imag], axis=-1))
    ws, wu, wh = s5_expand(jnp.stack(inj_c).astype(BF16), jnp.stack(cl_c).astype(BF16), tz_c.astype(BF16),
                           n_h, n_p)
    return ws, wu, wh, jnp.stack(lt).astype(F32)


def _s5_expand_body(inj_ref, cl_ref, tz_ref, ws_ref, wu_ref, wh_ref, *, n_h, n_p):
    rows = tz_ref.shape[0]
    gpt = LANES // n_h
    row = lax.broadcasted_iota(jnp.int32, (rows, LANES), 0)
    lane = lax.broadcasted_iota(jnp.int32, (rows, LANES), 1)
    sel_r = lax.broadcasted_iota(jnp.int32, (LANES, LANES), 0)
    sel_l = lax.broadcasted_iota(jnp.int32, (LANES, LANES), 1)
    grp_in = (row // n_h) % gpt
    grp_st = (row // n_p) % gpt

    def spread_out(m, t, grp_row):
        sel = jnp.logical_and(sel_r // n_h == t, sel_r % n_h == sel_l % n_h).astype(BF16)
        return jnp.where(grp_row == lane // n_h, _dot(m, sel), 0.0).astype(BF16)

    def spread_state(m, c, q, grp_row):
        sel = jnp.logical_and(sel_r // n_p == c, sel_r % n_p == sel_l % n_p).astype(BF16)
        return jnp.where(grp_row == (LANES // n_p) * q + lane // n_p, _dot(m, sel), 0.0).astype(BF16)

    w2 = 2 * gpt * n_p
    tz = tz_ref[...]
    for t in range(S5_TAU):
        wu_ref[:, t * LANES:(t + 1) * LANES] = spread_out(tz, t, grp_in)
    for d in range(2):
        cl = cl_ref[d]
        inj = inj_ref[d]
        for t in range(S5_TAU):
            wh_ref[d, :, t * LANES:(t + 1) * LANES] = spread_out(cl, t, grp_st)
        for c in range(2):
            for q in range(gpt * n_p // LANES):
                lo = d * w2 + c * gpt * n_p + q * LANES
                ws_ref[:, lo:lo + LANES] = spread_state(inj, c, q, grp_in)


def s5_expand(inj_c, cl_c, tz_c, n_h, n_p):
    n_j, rows, _ = tz_c.shape
    gpt = LANES // n_h
    w2 = 2 * gpt * n_p
    assert rows == S5_TAU * LANES == w2
    cspec = pl.BlockSpec((2, None, rows, LANES), lambda j: (0, j, 0, 0))
    return pl.pallas_call(
        functools.partial(_s5_expand_body, n_h=n_h, n_p=n_p),
        grid=(n_j,),
        in_specs=[cspec, cspec, pl.BlockSpec((None, rows, LANES), lambda j: (j, 0, 0))],
        out_specs=[pl.BlockSpec((None, rows, 2 * w2), lambda j: (j, 0, 0)),
                   pl.BlockSpec((None, rows, rows), lambda j: (j, 0, 0)),
                   pl.BlockSpec((2, None, w2, rows), lambda j: (0, j, 0, 0))],
        out_shape=[jax.ShapeDtypeStruct((n_j, rows, 2 * w2), BF16),
                   jax.ShapeDtypeStruct((n_j, rows, rows), BF16),
                   jax.ShapeDtypeStruct((2, n_j, w2, rows), BF16)],
        compiler_params=_cparams("parallel"),
        name="s5_expand",
    )(inj_c, cl_c, tz_c)


def _s5_inj_body(xc_ref, xl_ref, w_ref, oc_ref, ol_ref):
    w = w_ref[...]
    ol_ref[...] = _dot(xl_ref[...], w)

    @pl.when(pl.program_id(1) == 0)
    def _():
        oc_ref[...] = _dot(xc_ref[...], w)


def s5_inject(xr_c, xr_l, ws):
    r_c, r_l = xr_c.shape[0], xr_l.shape[0]
    n_j, k, n = ws.shape
    tm = r_l // 2 if r_l % 32 == 0 else r_l
    return pl.pallas_call(
        _s5_inj_body,
        grid=(n_j, r_l // tm),
        in_specs=[pl.BlockSpec((r_c, k), lambda j, i: (0, j)),
                  pl.BlockSpec((tm, k), lambda j, i: (i, j)),
                  pl.BlockSpec((None, k, n), lambda j, i: (j, 0, 0))],
        out_specs=[pl.BlockSpec((r_c, n), lambda j, i: (0, j)),
                   pl.BlockSpec((tm, n), lambda j, i: (i, j))],
        out_shape=[jax.ShapeDtypeStruct((r_c, n_j * n), F32), jax.ShapeDtypeStruct((r_l, n_j * n), F32)],
        compiler_params=_cparams("parallel", "arbitrary"),
        name="s5_inject",
    )(xr_c, xr_l, ws)


def _cmul(ar, ai, br, bi):
    return ar * br - ai * bi, ar * bi + ai * br


def _s5_scan_body(sc_ref, sl_ref, lt_ref, h_ref, raw_ref, *, n_ctx, n_lat, bsz):
    d = pl.program_id(1)
    w2 = lt_ref.shape[-1]
    w = w2 // 2
    rows = bsz * S5_SEGS
    seg = lax.broadcasted_iota(jnp.int32, (rows, 1), 0) % S5_SEGS
    is_late = seg != d
    lam_r = lt_ref[:, 0:w]
    lam_i = lt_ref[:, w:w2]
    zero = jnp.zeros((rows, w), F32)
    one = (jnp.ones((1, w), F32), jnp.zeros((1, w), F32))

    def swap_segments(x):
        return jnp.where(seg == 0, pltpu.roll(x, rows - 1, axis=0), pltpu.roll(x, 1, axis=0))

    def phase(s_ref, n_steps, hin_r, hin_i, write):
        def chunk(k):
            return jnp.where(d == 0, k, n_steps - 1 - k)

        def step_raw(k, carry):
            hr, hi = carry
            c = chunk(k)
            raw_ref[c, :, 0:w] = hr
            raw_ref[c, :, w:w2] = hi
            nr, ni = _cmul(lam_r, lam_i, hr, hi)
            return nr + s_ref[c, :, 0:w], ni + s_ref[c, :, w:w2]

        er, ei = lax.fori_loop(0, n_steps, step_raw, (zero, zero), unroll=S5_SCAN_UNROLL)
        pr, pi = lax.fori_loop(0, n_steps, lambda k, q: _cmul(lam_r, lam_i, *q), one)
        dr, di = _cmul(pr, pi, hin_r, hin_i)
        first_r = jnp.where(is_late, 0.0, er + dr)
        first_i = jnp.where(is_late, 0.0, ei + di)
        carry_r = jnp.where(is_late, swap_segments(first_r), hin_r)
        carry_i = jnp.where(is_late, swap_segments(first_i), hin_i)
        if write:
            def step_fix(k, q):
                c = chunk(k)
                fr, fi = _cmul(q[0], q[1], carry_r, carry_i)
                h_ref[c, :, 0:w] = (raw_ref[c, :, 0:w] + fr).astype(h_ref.dtype)
                h_ref[c, :, w:w2] = (raw_ref[c, :, w:w2] + fi).astype(h_ref.dtype)
                return _cmul(lam_r, lam_i, q[0], q[1])

            lax.fori_loop(0, n_steps, step_fix, one, unroll=S5_SCAN_UNROLL)
        lr, li = _cmul(pr, pi, carry_r, carry_i)
        last_r = jnp.where(is_late, er + lr, 0.0)
        last_i = jnp.where(is_late, ei + li, 0.0)
        return (jnp.where(is_late, 0.0, swap_segments(last_r)), jnp.where(is_late, 0.0, swap_segments(last_i)))

    hr, hi = phase(sc_ref, n_ctx, zero, zero, False)
    phase(sl_ref, n_lat, hr, hi, True)


def s5_scan(s_ctx, s_lat, lt, bsz):
    assert S5_SEGS == 2
    n_ctx, rows, _ = s_ctx.shape
    n_lat = s_lat.shape[0]
    n_j = lt.shape[1]
    w2 = lt.shape[-1]
    return pl.pallas_call(
        functools.partial(_s5_scan_body, n_ctx=n_ctx, n_lat=n_lat, bsz=bsz),
        grid=(n_j, 2),
        in_specs=[pl.BlockSpec((n_ctx, rows, w2), lambda j, d: (0, 0, 2 * j + d)),
                  pl.BlockSpec((n_lat, rows, w2), lambda j, d: (0, 0, 2 * j + d)),
                  pl.BlockSpec((None, None, 1, w2), lambda j, d: (d, j, 0, 0))],
        out_specs=pl.BlockSpec((None, n_lat, rows, w2), lambda j, d: (d, 0, 0, j)),
        out_shape=jax.ShapeDtypeStruct((2, n_lat, rows, n_j * w2), BF16),
        scratch_shapes=[pltpu.VMEM((max(n_ctx, n_lat), rows, w2), F32)],
        compiler_params=_cparams("parallel", "parallel"),
        name="s5_scan",
    )(s_ctx, s_lat, lt)


def _s5_out_body(x_ref, hf_ref, hb_ref, wu_ref, whf_ref, whb_ref, o_ref):
    o_ref[...] = (_dot(x_ref[...], wu_ref[...]) + _dot(hf_ref[...], whf_ref[...])
                  + _dot(hb_ref[...], whb_ref[...])).astype(o_ref.dtype)


def s5_readout(xr, h, wu, wh):
    r = xr.shape[0]
    n_j, k, n = wu.shape
    w2 = wh.shape[2]
    tm = min(r, 1024)
    return pl.pallas_call(
        _s5_out_body,
        grid=(n_j, r // tm),
        in_specs=[pl.BlockSpec((tm, k), lambda j, i: (i, j)),
                  pl.BlockSpec((None, tm, w2), lambda j, i: (0, i, j)),
                  pl.BlockSpec((None, tm, w2), lambda j, i: (1, i, j)),
                  pl.BlockSpec((None, k, n), lambda j, i: (j, 0, 0)),
                  pl.BlockSpec((None, None, w2, n), lambda j, i: (0, j, 0, 0)),
                  pl.BlockSpec((None, None, w2, n), lambda j, i: (1, j, 0, 0))],
        out_specs=pl.BlockSpec((tm, n), lambda j, i: (i, j)),
        out_shape=jax.ShapeDtypeStruct((r, n_j * n), BF16),
        compiler_params=_cparams("parallel", "parallel"),
        name="s5_readout",
    )(xr, h, h, wu, wh, wh)


def _gelu_tanh(x):
    return 0.5 * x * (1.0 + jnp.tanh(math.sqrt(2.0 / math.pi) * (x + 0.044715 * (x * x * x))))


def _s5_glu_body(x_ref, y_ref, g_ref, sh_ref, sc_ref, dk_ref, w1_ref, w2_ref, b1_ref, b2_ref, gt_ref,
                 mg_ref, msh_ref, msc_ref, wr_ref, br_ref, o_ref, tok_ref, eid_ref, gate_ref, *, n_groups, n_experts):
    x = x_ref[...]
    u = _norm_mod(x, g_ref[...], sh_ref[...], sc_ref[...])
    y = _gelu_tanh(y_ref[...].astype(F32) + dk_ref[...] * u).astype(BF16)
    o = (_dot(y, w1_ref[...]) + b1_ref[...]) * jax.nn.sigmoid(_dot(y, w2_ref[...]) + b2_ref[...])
    xn = x + gt_ref[...] * o
    o_ref[...] = xn
    tok = _norm_mod(xn, mg_ref[...], msh_ref[...], msc_ref[...])
    _route_tokens(tok, wr_ref, br_ref, tok_ref, eid_ref, gate_ref, n_groups, n_experts)


def s5_glu(x, y, g, shift, scale, d_skip, w1, b1, w2, b2, gate, moe_g, moe_shift, moe_scale, wr, br, n_groups,
           n_experts):
    bsz, seq, d = x.shape
    tm = min(seq, 512)
    nt = seq // tm
    sub = tm // MOE_TM
    row = lambda a: a.reshape(1, d)
    rspec = pl.BlockSpec((1, d), lambda b, i: (0, 0))
    mspec = pl.BlockSpec((None, 1, d), lambda b, i: (b, 0, 0))
    tile = pl.BlockSpec((None, tm, d), lambda b, i: (b, i, 0))
    wspec = pl.BlockSpec((d, d), lambda b, i: (0, 0), pipeline_mode=pl.Buffered(1))
    return pl.pallas_call(
        functools.partial(_s5_glu_body, n_groups=n_groups, n_experts=n_experts),
        grid=(bsz, nt),
        in_specs=[tile, tile, rspec, mspec, mspec, rspec, wspec, wspec, rspec, rspec, mspec,
                  rspec, mspec, mspec,
                  pl.BlockSpec((2, d, LANES), lambda b, i: (0, 0, 0)),
                  pl.BlockSpec((1, LANES), lambda b, i: (0, 0))],
        out_specs=[tile,
                   pl.BlockSpec((tm, d), lambda b, i: (b * nt + i, 0)),
                   pl.BlockSpec((sub, SUBLANES, MOE_TM), lambda b, i: (b * nt + i, 0, 0)),
                   pl.BlockSpec((tm, LANES), lambda b, i: (b * nt + i, 0))],
        out_shape=[jax.ShapeDtypeStruct((bsz, seq, d), F32),
                   jax.ShapeDtypeStruct((bsz * seq, d), F32),
                   jax.ShapeDtypeStruct((bsz * seq // MOE_TM, SUBLANES, MOE_TM), jnp.int32),
                   jax.ShapeDtypeStruct((bsz * seq, LANES), F32)],
        compiler_params=_cparams("parallel", "parallel"),
        name="s5_glu",
    )(x, y, row(g), shift, scale, row(d_skip), w1, w2, row(b1), row(b2), gate,
      row(moe_g), moe_shift, moe_scale, _split_bf16(wr), br)


def s5_mix(xl, hl, hc, g, sh_l, sc_l, gate_l, a_re, a_im, log_step, b_re, b_im, c_re, c_im, d_skip,
           w1, b1, w2, b2, moe_mod, router):
    bsz, seq, d = xl.shape
    xr_c = _s5_arrange(hc)
    xr_l = _s5_arrange(hl)
    rows = bsz * S5_SEGS
    n_ctx = xr_c.shape[0] // rows
    n_lat = xr_l.shape[0] // rows
    ws, wu, wh, lt = _s5_operators(a_re, a_im, log_step, b_re, b_im, c_re, c_im)
    s_ctx, s_lat = s5_inject(xr_c, xr_l, ws)
    h = s5_scan(s_ctx.reshape(n_ctx, rows, -1), s_lat.reshape(n_lat, rows, -1), lt, bsz)
    y = s5_readout(xr_l, h.reshape(2, n_lat * rows, -1), wu, wh)
    y = _s5_unarrange(y, bsz)
    wr, br = _router_weights(*router)
    xl, tok, eid, gates = s5_glu(xl, y, g, sh_l, sc_l, d_skip, w1.astype(BF16), b1, w2.astype(BF16), b2, gate_l,
                                 *moe_mod, wr, br, router[0].shape[1], router[2].shape[1])
    return xl, (tok, eid, gates)


def hyena_mix(x, g, shift, scale, gate, w_in, b_in, conv_w, conv_b, fw1, fb1, fw2, fb2, fw3, freq, skip,
              w_out, b_out):
    seq = x.shape[1]
    if seq >= FFT_MIN_SEQ and (2 * seq) % (2 * FFT_N2) == 0:
        a, dd, _ = hyena_filter_taps(seq, fw1, fb1, fw2, fb2, fw3, freq, F32)
        v, x0 = hyena_in(x, g, shift, scale, w_in, b_in, conv_w, conv_b, F32)
        yg = hyena_conv_fft(v, x0, skip, a, dd)
    else:
        cmat, smat = dft_matrices(seq)
        a, dd, kn = hyena_filter_taps(seq, fw1, fb1, fw2, fb2, fw3, freq, BF16)
        kr, ki = hyena_filter_dft(a, dd, cmat, smat)
        v, x0 = hyena_in(x, g, shift, scale, w_in, b_in, conv_w, conv_b, BF16)
        yg = hyena_conv(v, x0, skip, kr, ki, kn, cmat, smat)
    return mm_residual(yg, w_out, b_out, x, gate)


def kernel(x, c, ctx, c_ctx, ada_w, ada_b, norm_g, final_g, hy_w_in, hy_b_in, hy_conv_w, hy_conv_b, hy_fw1,
           hy_fb1, hy_fw2, hy_fb2, hy_fw3, hy_freq, hy_skip, hy_w_out, hy_b_out, s5_a_re, s5_a_im,
           s5_log_step, s5_b_re, s5_b_im, s5_c_re, s5_c_im, s5_d, s5_w1, s5_b1, s5_w2, s5_b2, moe_wg, moe_bg,
           moe_we, moe_be, moe_w_gate, moe_w_up, moe_w_down):
    bsz, _, d = x.shape
    depth = ada_w.shape[0]
    assert depth == 2 and bsz < SUBLANES
    c_all = jnp.concatenate([c, c_ctx[None, :], jnp.zeros((SUBLANES - bsz - 1, d), F32)], axis=0)
    mods = ada_mod(c_all, ada_w, ada_b)

    def mod_rows(layer, k):
        lat = mods[layer, :bsz, k * d:(k + 1) * d][:, None, :]
        cx = jnp.broadcast_to(mods[layer, bsz, k * d:(k + 1) * d][None, None, :], (bsz, 1, d))
        return lat, cx

    (sh_a, csh_a), (sc_a, csc_a), (gt_a, cgt_a) = mod_rows(0, 0), mod_rows(0, 1), mod_rows(0, 2)
    (sh_f, csh_f), (sc_f, csc_f), (gt_f, cgt_f) = mod_rows(0, 3), mod_rows(0, 4), mod_rows(0, 5)
    hy = (hy_w_in[0].astype(BF16), hy_b_in[0], hy_conv_w[0], hy_conv_b[0], hy_fw1[0], hy_fb1[0], hy_fw2[0],
          hy_fb2[0], hy_fw3[0], hy_freq[0], hy_skip[0], hy_w_out[0].astype(BF16), hy_b_out[0])
    xl = hyena_mix(x, norm_g[0, 0], sh_a, sc_a, gt_a, *hy)
    xc = hyena_mix(ctx, norm_g[0, 0], csh_a, csc_a, cgt_a, *hy)
    (sh_a, csh_a), (sc_a, csc_a), (gt_a, _) = mod_rows(1, 0), mod_rows(1, 1), mod_rows(1, 2)
    (xl, hl), (_, hc) = hier_moe([(xl, sh_f, sc_f, gt_f), (xc, csh_f, csc_f, cgt_f)], norm_g[0, 1],
                                 moe_wg[0], moe_bg[0], moe_we[0], moe_be[0], moe_w_gate, moe_w_up, moe_w_down, 0,
                                 final_g, False,
                                 next_mods=[(norm_g[1, 0], sh_a, sc_a), (norm_g[1, 0], csh_a, csc_a)])
    (sh_f, _), (sc_f, _), (gt_f, _) = mod_rows(1, 3), mod_rows(1, 4), mod_rows(1, 5)
    router = (moe_wg[1], moe_bg[1], moe_we[1], moe_be[1])
    xl, routed = s5_mix(xl, hl, hc, norm_g[1, 0], sh_a, sc_a, gt_a, s5_a_re[0], s5_a_im[0], s5_log_step[0],
                        s5_b_re[0], s5_b_im[0], s5_c_re[0], s5_c_im[0], s5_d[0], s5_w1[0], s5_b1[0], s5_w2[0],
                        s5_b2[0], (norm_g[1, 1], sh_f, sc_f), router)
    (out,) = hier_moe([(xl, sh_f, sc_f, gt_f)], norm_g[1, 1], *router, moe_w_gate, moe_w_up, moe_w_down, 1,
                      final_g, True, routed=routed)
    return out
```

```python
import functools
import math

import numpy as np
import jax
import jax.numpy as jnp
from jax import lax
from jax.experimental import pallas as pl
from jax.experimental.pallas import tpu as pltpu

F32 = jnp.float32
BF16 = jnp.bfloat16
HIGHEST = lax.Precision.HIGHEST

NORM_EPS = 1e-6
HY_DECAY_TARGET = 1e-2
HY_FAST_PCT = 0.3
HY_SLOW_PCT = 1.5
TOP_K = 2

V7X_VMEM_LIMIT_BYTES = 56 * 1024 * 1024
LANES = 128
SUBLANES = 8
S5_TAU = 8
S5_SEGS = 2
S5_SCAN_UNROLL = 8
MOE_TM = 256
MOE_BM = 256
MOE_NCH = 12
MOE_STAGE = 6
MOE_CAST_ROWS = 128
MOE_WIN_ALIGN = 1024
MOE_WIN = 2 * MOE_WIN_ALIGN
NEG_BIG = -1e30


def _cparams(*sem):
    return pltpu.CompilerParams(dimension_semantics=sem, vmem_limit_bytes=V7X_VMEM_LIMIT_BYTES)


def _norm_mod(x, g, shift, scale):
    ms = jnp.mean(x * x, axis=-1, keepdims=True)
    return (x * lax.rsqrt(ms + NORM_EPS) * g) * (1.0 + scale) + shift


def _dot(a, b):
    return jnp.dot(a, b, preferred_element_type=F32)


def _ada_body(c_ref, w_ref, b_ref, o_ref):
    x = c_ref[...]
    s = (x * jax.nn.sigmoid(x)).astype(BF16)
    o_ref[...] = _dot(s, w_ref[...].astype(BF16)) + b_ref[...]


def ada_mod(c_all, ada_w, ada_b):
    depth, d, n = ada_w.shape
    tn = min(n, 1024)
    return pl.pallas_call(
        _ada_body,
        grid=(depth, n // tn),
        in_specs=[pl.BlockSpec((SUBLANES, d), lambda l, j: (0, 0)),
                  pl.BlockSpec((None, d, tn), lambda l, j: (l, 0, j)),
                  pl.BlockSpec((None, 1, tn), lambda l, j: (l, 0, j))],
        out_specs=pl.BlockSpec((None, SUBLANES, tn), lambda l, j: (l, 0, j)),
        out_shape=jax.ShapeDtypeStruct((depth, SUBLANES, n), F32),
        compiler_params=_cparams("parallel", "parallel"),
        name="ada_mod",
    )(c_all, ada_w, ada_b.reshape(depth, 1, n))


def _hy_in_body(xp_ref, xm_ref, xn_ref, g_ref, sh_ref, sc_ref,
                w0_ref, w1_ref, w2_ref, b0_ref, b1_ref, b2_ref,
                cw0_ref, cw1_ref, cw2_ref, cb0_ref, cb1_ref, cb2_ref,
                v_ref, x0_ref):
    i = pl.program_id(2)
    ni = pl.num_programs(2)
    tm = xm_ref.shape[0]
    x = jnp.concatenate([xp_ref[...], xm_ref[...], xn_ref[...]], axis=0)
    h = _norm_mod(x, g_ref[...], sh_ref[...], sc_ref[...]).astype(BF16)
    rows = lax.broadcasted_iota(jnp.int32, (tm + 2 * SUBLANES, 1), 0)
    valid = jnp.logical_and(jnp.logical_or(rows >= SUBLANES, i > 0),
                            jnp.logical_or(rows < tm + SUBLANES, i < ni - 1))

    def part(w_ref, b_ref, cw_ref, cb_ref):
        z = jnp.where(valid, _dot(h, w_ref[...]) + b_ref[...], 0.0)
        cw = cw_ref[...]
        zp = pltpu.roll(z, 1, axis=0)[SUBLANES:tm + SUBLANES]
        zn = pltpu.roll(z, tm + 2 * SUBLANES - 1, axis=0)[SUBLANES:tm + SUBLANES]
        return zp * cw[0:1] + z[SUBLANES:tm + SUBLANES] * cw[1:2] + zn * cw[2:3] + cb_ref[...]

    x0 = part(w0_ref, b0_ref, cw0_ref, cb0_ref)
    x1 = part(w1_ref, b1_ref, cw1_ref, cb1_ref)
    v = part(w2_ref, b2_ref, cw2_ref, cb2_ref) * x1
    v_ref[...] = v.astype(v_ref.dtype)
    x0_ref[...] = x0.astype(BF16)


def hyena_in(x, g, shift, scale, w_in, b_in, conv_w, conv_b, v_dtype):
    bsz, seq, d = x.shape
    tm = min(seq, 512)
    tn = min(d, 1024)
    nj = d // tn
    r8 = tm // SUBLANES
    last8 = seq // SUBLANES - 1
    row = lambda a: a.reshape(1, -1)
    wspec = lambda k: pl.BlockSpec((d, tn), lambda j, b, i: (0, k * nj + j))
    rspec = lambda k: pl.BlockSpec((1, tn), lambda j, b, i: (0, k * nj + j))
    cspec = lambda k: pl.BlockSpec((3, tn), lambda j, b, i: (0, k * nj + j))
    mspec = pl.BlockSpec((None, 1, d), lambda j, b, i: (b, 0, 0))
    out_spec = pl.BlockSpec((None, tm, tn), lambda j, b, i: (b, i, j))
    return pl.pallas_call(
        _hy_in_body,
        grid=(nj, bsz, seq // tm),
        in_specs=[pl.BlockSpec((None, SUBLANES, d), lambda j, b, i: (b, jnp.maximum(i * r8 - 1, 0), 0)),
                  pl.BlockSpec((None, tm, d), lambda j, b, i: (b, i, 0)),
                  pl.BlockSpec((None, SUBLANES, d), lambda j, b, i: (b, jnp.minimum((i + 1) * r8, last8), 0)),
                  pl.BlockSpec((1, d), lambda j, b, i: (0, 0)), mspec, mspec,
                  wspec(0), wspec(1), wspec(2), rspec(0), rspec(1), rspec(2),
                  cspec(0), cspec(1), cspec(2), rspec(0), rspec(1), rspec(2)],
        out_specs=[out_spec, out_spec],
        out_shape=[jax.ShapeDtypeStruct((bsz, seq, d), v_dtype), jax.ShapeDtypeStruct((bsz, seq, d), BF16)],
        compiler_params=_cparams("parallel", "parallel", "parallel"),
        name="hyena_in",
    )(x, x, x, row(g), shift, scale, w_in, w_in, w_in, row(b_in), row(b_in), row(b_in),
      conv_w, conv_w, conv_w, row(conv_b), row(conv_b), row(conv_b))


def _dft_tables(seq, blk):
    n = 2 * seq
    s = np.arange(seq, dtype=np.int64)[None, :]
    fl = np.arange(blk, dtype=np.int64)[:, None]
    fh = (np.arange(seq // blk, dtype=np.int64) * blk)[:, None]
    w = 2.0 * math.pi / n
    ang_b = ((fl * s) % n) * w
    ang_a = ((fh * s) % n) * w
    f32 = lambda m: jnp.asarray(m.astype(np.float32))
    return (f32(np.cos(ang_a)[:, None, :]), f32(np.sin(ang_a)[:, None, :]), f32(np.cos(ang_b)), f32(np.sin(ang_b)))


def _dft_gen_body(ca_ref, sa_ref, cb_ref, sb_ref, c_ref, s_ref):
    ca, sa, cb, sb = ca_ref[...], sa_ref[...], cb_ref[...], sb_ref[...]
    c_ref[...] = (ca * cb - sa * sb).astype(BF16)
    s_ref[...] = (sa * cb + ca * sb).astype(BF16)


def dft_matrices(seq):
    blk = min(seq, 256)
    ca, sa, cb, sb = _dft_tables(seq, blk)
    aspec = pl.BlockSpec((None, 1, seq), lambda i: (i, 0, 0))
    bspec = pl.BlockSpec((blk, seq), lambda i: (0, 0))
    ospec = pl.BlockSpec((blk, seq), lambda i: (i, 0))
    return pl.pallas_call(
        _dft_gen_body,
        grid=(seq // blk,),
        in_specs=[aspec, aspec, bspec, bspec],
        out_specs=[ospec, ospec],
        out_shape=[jax.ShapeDtypeStruct((seq, seq), BF16)] * 2,
        compiler_params=_cparams("parallel"),
        name="dft_matrices",
    )(ca, sa, cb, sb)


def _alt_sign(rows):
    return jnp.where((rows & 1) == 0, 1.0, -1.0).astype(F32)


def _filt_body(h2_ref, wf_ref, wb_ref, dl_ref, a_ref, d_ref, ny_ref):
    seq = h2_ref.shape[0]
    h2 = h2_ref[...]
    row = lax.broadcasted_iota(jnp.int32, (seq, 1), 0)
    t = row.astype(F32) * (1.0 / (seq - 1))
    win = jnp.exp(-t * dl_ref[...])
    h_hi = h2.astype(BF16)
    h_lo = (h2 - h_hi.astype(F32)).astype(BF16)
    dot3 = lambda w_ref: _dot(h_hi, w_ref[0]) + _dot(h_hi, w_ref[1]) + _dot(h_lo, w_ref[0])
    hf = dot3(wf_ref) * win
    hb = dot3(wb_ref) * win
    hb = jnp.where(row == 0, 0.0, hb)
    nrm = (jnp.sum(jnp.abs(hf), axis=0, keepdims=True) + jnp.sum(jnp.abs(hb), axis=0, keepdims=True))
    inv = 1.0 / nrm
    a = (hf + hb) * inv
    a_ref[...] = a.astype(a_ref.dtype)
    d_ref[...] = ((hb - hf) * inv).astype(d_ref.dtype)
    ny = jnp.sum(a * _alt_sign(row), axis=0, keepdims=True) * (1.0 / (2 * seq))
    ny_ref[...] = jnp.broadcast_to(ny, ny_ref.shape)


def _khat_body(a_ref, d_ref, c_ref, s_ref, kr_ref, ki_ref):
    i = pl.program_id(1)
    tm = c_ref.shape[0]
    seq = c_ref.shape[1]
    f = i * tm + lax.broadcasted_iota(jnp.int32, (tm, 1), 0)
    w = jnp.where(f == 0, 1.0, 2.0).astype(F32) * (1.0 / (2 * seq))
    kr_ref[...] = _dot(c_ref[...], a_ref[...]) * w
    ki_ref[...] = _dot(s_ref[...], d_ref[...]) * w


def hyena_filter_taps(seq, fw1, fb1, fw2, fb2, fw3, freq, taps_dtype):
    d = fw3.shape[1] // 2
    bands_n = (fw1.shape[0] - 1) // 2
    t = np.linspace(0.0, 1.0, seq)[:, None]
    w = (2.0 * math.pi / seq) * np.arange(seq)[:, None]
    bands = np.linspace(1e-4, bands_n - 1, bands_n)[None, :]
    z = jnp.asarray(np.concatenate([t, np.cos(bands * w), -np.sin(bands * w)], axis=-1).astype(np.float32))
    h = jnp.sin(freq * (jnp.dot(z, fw1, precision=HIGHEST) + fb1))
    h2 = jnp.sin(freq * (jnp.dot(h, fw2, precision=HIGHEST) + fb2))
    max_decay = math.log(HY_DECAY_TARGET) / HY_FAST_PCT
    min_decay = math.log(HY_DECAY_TARGET) / HY_SLOW_PCT
    deltas = jnp.abs(jnp.linspace(min_decay, max_decay, d, dtype=F32))[None, :]

    order = h2.shape[1]
    fw3_split = _split_bf16(fw3)
    tn = min(d, 256)
    nj = d // tn
    return pl.pallas_call(
        _filt_body,
        grid=(nj,),
        in_specs=[pl.BlockSpec((seq, order), lambda j: (0, 0)),
                  pl.BlockSpec((2, order, tn), lambda j: (0, 0, j)),
                  pl.BlockSpec((2, order, tn), lambda j: (0, 0, nj + j)),
                  pl.BlockSpec((1, tn), lambda j: (0, j))],
        out_specs=[pl.BlockSpec((seq, tn), lambda j: (0, j)),
                   pl.BlockSpec((seq, tn), lambda j: (0, j)),
                   pl.BlockSpec((SUBLANES, tn), lambda j: (0, j))],
        out_shape=[jax.ShapeDtypeStruct((seq, d), taps_dtype), jax.ShapeDtypeStruct((seq, d), taps_dtype),
                   jax.ShapeDtypeStruct((SUBLANES, d), F32)],
        compiler_params=_cparams("parallel"),
        name="hyena_filter_taps",
    )(h2, fw3_split, fw3_split, deltas)


def hyena_filter_dft(a, dd, cmat, smat):
    seq, d = a.shape
    tm = min(seq, 512)
    tn2 = min(d, 512)
    return pl.pallas_call(
        _khat_body,
        grid=(d // tn2, seq // tm),
        in_specs=[pl.BlockSpec((seq, tn2), lambda j, i: (0, j)),
                  pl.BlockSpec((seq, tn2), lambda j, i: (0, j)),
                  pl.BlockSpec((tm, seq), lambda j, i: (i, 0)),
                  pl.BlockSpec((tm, seq), lambda j, i: (i, 0))],
        out_specs=[pl.BlockSpec((tm, tn2), lambda j, i: (i, j))] * 2,
        out_shape=[jax.ShapeDtypeStruct((seq, d), F32)] * 2,
        compiler_params=_cparams("parallel", "parallel"),
        name="hyena_filter_dft",
    )(a, dd, cmat, smat)


def _dft_fwd_body(v_ref, c_ref, s_ref, kr_ref, ki_ref, kn_ref, ya_ref, yb_ref, yn_ref):
    i = pl.program_id(2)
    v = v_ref[...]
    vr = _dot(c_ref[...], v)
    p = _dot(s_ref[...], v)
    kr = kr_ref[...]
    ki = ki_ref[...]
    ya_ref[...] = (vr * kr + p * ki).astype(BF16)
    yb_ref[...] = (p * kr - vr * ki).astype(BF16)

    @pl.when(i == 0)
    def _():
        seq = v.shape[0]
        row = lax.broadcasted_iota(jnp.int32, (seq, 1), 0)
        vl = jnp.sum(v.astype(F32) * _alt_sign(row), axis=0, keepdims=True)
        yn_ref[...] = jnp.broadcast_to(vl * kn_ref[0:1, :], yn_ref.shape)


def _dft_inv_body(ya_ref, yb_ref, c_ref, s_ref, v_ref, x0_ref, skip_ref, yn_ref, o_ref):
    i = pl.program_id(2)
    tm = c_ref.shape[0]
    acc = _dot(c_ref[...], ya_ref[...]) + _dot(s_ref[...], yb_ref[...])
    t = i * tm + lax.broadcasted_iota(jnp.int32, (tm, 1), 0)
    y = acc + _alt_sign(t) * yn_ref[0:1, :] + skip_ref[...] * v_ref[...].astype(F32)
    o_ref[...] = (y * x0_ref[...].astype(F32)).astype(BF16)


def hyena_conv(v, x0, skip, kr, ki, kn, cmat, smat):
    bsz, seq, d = v.shape
    tm = min(seq, 512)
    tn = min(d, 512)
    grid = (bsz, d // tn, seq // tm)
    full = pl.BlockSpec((None, seq, tn), lambda b, j, i: (b, 0, j))
    mat = pl.BlockSpec((tm, seq), lambda b, j, i: (i, 0))
    tile = pl.BlockSpec((None, tm, tn), lambda b, j, i: (b, i, j))
    ktile = pl.BlockSpec((tm, tn), lambda b, j, i: (i, j))
    nyq = pl.BlockSpec((None, SUBLANES, tn), lambda b, j, i: (b, 0, j))
    ya, yb, yn = pl.pallas_call(
        _dft_fwd_body,
        grid=grid,
        in_specs=[full, mat, mat, ktile, ktile, pl.BlockSpec((SUBLANES, tn), lambda b, j, i: (0, j))],
        out_specs=[tile, tile, nyq],
        out_shape=[jax.ShapeDtypeStruct((bsz, seq, d), BF16)] * 2
        + [jax.ShapeDtypeStruct((bsz, SUBLANES, d), F32)],
        compiler_params=_cparams("parallel", "parallel", "arbitrary"),
        name="hyena_dft_fwd",
    )(v, cmat, smat, kr, ki, kn)
    return pl.pallas_call(
        _dft_inv_body,
        grid=grid,
        in_specs=[full, full, mat, mat, tile, tile, pl.BlockSpec((1, tn), lambda b, j, i: (0, j)), nyq],
        out_specs=tile,
        out_shape=jax.ShapeDtypeStruct((bsz, seq, d), BF16),
        compiler_params=_cparams("parallel", "parallel", "parallel"),
        name="hyena_dft_inv",
    )(ya, yb, cmat, smat, v, x0, skip.reshape(1, d), yn)


FFT_N2 = 128
FFT_MIN_SEQ = 1024
FFT_UNROLL = 8


def _fft_matrices(seq):
    n = 2 * seq
    n2 = FFT_N2
    n1 = n // n2
    r8 = SUBLANES
    q = np.arange(n2 // r8, dtype=np.int64)[:, None, None, None]
    f1 = np.arange(n1, dtype=np.int64)[None, :, None, None]
    r = np.arange(r8, dtype=np.int64)[None, None, :, None]
    t1 = np.arange(n1 // 2, dtype=np.int64)[None, None, None, :]
    ang = ((f1 * (t1 * n2 + q * r8 + r)) % n) * (2.0 * math.pi / n)
    g = np.stack([np.cos(ang), -np.sin(ang)], axis=3)
    eye = np.eye(r8)[None, None, :, None, None, :]
    ma = (g[..., None] * eye).reshape(n2 // r8, n1 * r8 * 2, (n1 // 2) * r8).astype(np.float32)
    f2 = np.arange(n2, dtype=np.int64)[:, None]
    t2 = np.arange(n2, dtype=np.int64)[None, :]
    th = ((f2 * t2) % n2) * (2.0 * math.pi / n2)
    co, si = np.cos(th), np.sin(th)
    wc = np.stack([np.stack([co, si], axis=-1), np.stack([-si, co], axis=-1)], axis=0)
    wc = wc.reshape(2 * n2, 2 * n2).astype(np.float32)
    as_bf16 = lambda m: jnp.asarray(np.ascontiguousarray(m).astype(BF16))
    return as_bf16(ma), as_bf16(np.swapaxes(ma, 1, 2)), as_bf16(wc), as_bf16(wc.T)


def _fft_stage_a(x_ref, ma_ref, s1):
    n1h, n_q, r8, tn = x_ref.shape
    n1 = s1.shape[0]

    def body(q, carry):
        x = x_ref[:, pl.ds(q, 1), :, :].reshape(n1h * r8, tn).astype(BF16)
        a = _dot(ma_ref[q], x).astype(BF16)
        s1[:, pl.ds(pl.multiple_of(q * 2 * r8, 2 * r8), 2 * r8), :] = a.reshape(n1, 2 * r8, tn)
        return carry

    lax.fori_loop(0, n_q, body, 0, unroll=FFT_UNROLL)


def _fft_conv_body(v_ref, x0_ref, k_ref, skip_ref, ma_ref, mat_ref, wc_ref, wci_ref, o_ref, s1, ysc):
    n1h, n_q, r8, tn = v_ref.shape
    n1 = s1.shape[0]
    n2 = s1.shape[1] // 2
    seq = n1h * n_q * r8
    _fft_stage_a(v_ref, ma_ref, s1)

    def slab(f, carry):
        y = _dot(wc_ref[...], s1[f])
        yr, yi = y[:n2], y[n2:]
        kr = k_ref[f, 0].astype(F32)
        ki = k_ref[f, 1].astype(F32)
        p = jnp.concatenate([yr * kr - yi * ki, yr * ki + yi * kr], axis=0).astype(BF16)
        s1[f] = _dot(wci_ref[...], p).astype(BF16)
        return carry

    lax.fori_loop(0, n1, slab, 0, unroll=2 * FFT_UNROLL)

    def inv_a(q, carry):
        z = s1[:, pl.ds(pl.multiple_of(q * 2 * r8, 2 * r8), 2 * r8), :].reshape(n1 * 2 * r8, tn)
        ysc[:, pl.ds(q, 1), :, :] = _dot(mat_ref[q], z).reshape(n1h, 1, r8, tn)
        return carry

    lax.fori_loop(0, n_q, inv_a, 0, unroll=FFT_UNROLL)
    y = ysc[...].reshape(seq, tn) + skip_ref[...] * v_ref[...].reshape(seq, tn)
    o_ref[...] = (y * x0_ref[...].astype(F32)).astype(BF16)


def _fft_filter_body(a_ref, d_ref, ma_ref, wc_ref, k_ref, s1):
    n1 = s1.shape[0]
    n2 = s1.shape[1] // 2
    scale = 1.0 / (n1 * n2)
    for src_ref, part, sign in ((a_ref, 0, scale), (d_ref, 1, -scale)):
        _fft_stage_a(src_ref, ma_ref, s1)

        def slab(f, carry):
            y = _dot(wc_ref[part * n2:(part + 1) * n2, :], s1[f])
            k_ref[f, part] = (y * sign).astype(BF16)
            return carry

        lax.fori_loop(0, n1, slab, 0, unroll=FFT_UNROLL)


def hyena_conv_fft(v, x0, skip, a, dd):
    bsz, seq, d = v.shape
    n2 = FFT_N2
    n1 = 2 * seq // n2
    n_q = n2 // SUBLANES
    tn = min(d, 256)
    ma, mat, wc, wci = _fft_matrices(seq)
    const = lambda shape: pl.BlockSpec(shape, lambda *_: (0,) * len(shape), pipeline_mode=pl.Buffered(1))
    view = lambda t: t.reshape(t.shape[:-2] + (n1 // 2, n_q, SUBLANES, d))
    tap = pl.BlockSpec((n1 // 2, n_q, SUBLANES, tn), lambda j: (0, 0, 0, j))
    khat = pl.pallas_call(
        _fft_filter_body,
        grid=(d // tn,),
        in_specs=[tap, tap, const(ma.shape), const(wc.shape)],
        out_specs=pl.BlockSpec((n1, 2, n2, tn), lambda j: (0, 0, 0, j)),
        out_shape=jax.ShapeDtypeStruct((n1, 2, n2, d), BF16),
        scratch_shapes=[pltpu.VMEM((n1, 2 * n2, tn), BF16)],
        compiler_params=_cparams("parallel"),
        name="hyena_filter_fft",
    )(view(a), view(dd), ma, wc)
    return pl.pallas_call(
        _fft_conv_body,
        grid=(d // tn, bsz),
        in_specs=[pl.BlockSpec((None, n1 // 2, n_q, SUBLANES, tn), lambda j, b: (b, 0, 0, 0, j)),
                  pl.BlockSpec((None, seq, tn), lambda j, b: (b, 0, j)),
                  pl.BlockSpec((n1, 2, n2, tn), lambda j, b: (0, 0, 0, j), pipeline_mode=pl.Buffered(1)),
                  pl.BlockSpec((1, tn), lambda j, b: (0, j)),
                  const(ma.shape), const(mat.shape), const(wc.shape), const(wci.shape)],
        out_specs=pl.BlockSpec((None, seq, tn), lambda j, b: (b, 0, j)),
        out_shape=jax.ShapeDtypeStruct((bsz, seq, d), BF16),
        scratch_shapes=[pltpu.VMEM((n1, 2 * n2, tn), BF16), pltpu.VMEM((n1 // 2, n_q, SUBLANES, tn), F32)],
        compiler_params=_cparams("parallel", "arbitrary"),
        name="hyena_conv_fft",
    )(view(v), x0, khat, skip.reshape(1, d), ma, mat, wc, wci)


def _mm_res_body(x_ref, w_ref, b_ref, res_ref, gate_ref, o_ref):
    o_ref[...] = res_ref[...] + gate_ref[...] * (_dot(x_ref[...], w_ref[...]) + b_ref[...])


def mm_residual(x, w, b, res, gate):
    bsz, seq, k = x.shape
    n = w.shape[1]
    tm = min(seq, 512)
    return pl.pallas_call(
        _mm_res_body,
        grid=(bsz, seq // tm),
        in_specs=[pl.BlockSpec((None, tm, k), lambda b, i: (b, i, 0)),
                  pl.BlockSpec((k, n), lambda b, i: (0, 0)),
                  pl.BlockSpec((1, n), lambda b, i: (0, 0)),
                  pl.BlockSpec((None, tm, n), lambda b, i: (b, i, 0)),
                  pl.BlockSpec((None, 1, n), lambda b, i: (b, 0, 0))],
        out_specs=pl.BlockSpec((None, tm, n), lambda b, i: (b, i, 0)),
        out_shape=jax.ShapeDtypeStruct((bsz, seq, n), F32),
        compiler_params=_cparams("parallel", "parallel"),
        name="mm_residual",
    )(x, w, b.reshape(1, n), res, gate)


def _moe_pre_body(*refs, n_groups, n_experts, tile_offs):
    n_streams = len(tile_offs) - 1
    g_ref, wr_ref, br_ref, tok_ref, eid_ref, gate_ref = refs[3 * n_streams:]
    i = pl.program_id(0)
    for k in range(n_streams):
        x_ref, sh_ref, sc_ref = refs[3 * k:3 * k + 3]

        @pl.when(jnp.logical_and(i >= tile_offs[k], i < tile_offs[k + 1]))
        def _():
            tok = _norm_mod(x_ref[...], g_ref[...], sh_ref[...], sc_ref[...])
            _route_tokens(tok, wr_ref, br_ref, tok_ref, eid_ref, gate_ref, n_groups, n_experts)


def _route_tokens(tok, wr_ref, br_ref, tok_ref, eid_ref, gate_ref, n_groups, n_experts):
    tok_ref[...] = tok
    t_hi = tok.astype(BF16)
    t_lo = (tok - t_hi.astype(F32)).astype(BF16)
    logits = (_dot(t_hi, wr_ref[0]) + _dot(t_hi, wr_ref[1]) + _dot(t_lo, wr_ref[0])) + br_ref[...]
    lane = lax.broadcasted_iota(jnp.int32, logits.shape, 1)
    per = n_experts // n_groups
    big = jnp.int32(1 << 20)
    gmask = jnp.logical_and(lane >= n_experts, lane < n_experts + n_groups)
    gl = jnp.where(gmask, logits, NEG_BIG)
    gmax = jnp.max(gl, axis=-1, keepdims=True)
    gidx = jnp.min(jnp.where(gl == gmax, lane - n_experts, big), axis=-1, keepdims=True)
    p_top = 1.0 / jnp.sum(jnp.where(gmask, jnp.exp(gl - gmax), 0.0), axis=-1, keepdims=True)
    lo = gidx * per
    emask = jnp.logical_and(lane >= lo, lane < lo + per)
    el = jnp.where(emask, logits, NEG_BIG)
    m1 = jnp.max(el, axis=-1, keepdims=True)
    i1 = jnp.min(jnp.where(el == m1, lane, big), axis=-1, keepdims=True)
    el2 = jnp.where(lane == i1, NEG_BIG, el)
    m2 = jnp.max(el2, axis=-1, keepdims=True)
    i2 = jnp.min(jnp.where(el2 == m2, lane, big), axis=-1, keepdims=True)
    e21 = jnp.exp(m2 - m1)
    g1 = p_top / (1.0 + e21)
    g2 = g1 * e21
    ids = jnp.where(lane == 0, i1, jnp.where(lane == 1, i2, -1))
    ids_t = ids.T
    for h in range(eid_ref.shape[0]):
        eid_ref[h] = ids_t[0:SUBLANES, h * MOE_TM:(h + 1) * MOE_TM]
    gate_ref[...] = jnp.where(lane == 0, g1, jnp.where(lane == 1, g2, 0.0))


def _split_bf16(w):
    hi = w.astype(BF16)
    return jnp.stack([hi, (w - hi.astype(F32)).astype(BF16)])


def _router_weights(wg, bg, we, be):
    pad = LANES - we.shape[1] - wg.shape[1]
    wr = jnp.pad(jnp.concatenate([we, wg], axis=1), ((0, 0), (0, pad)))
    br = jnp.pad(jnp.concatenate([be, bg]), (0, pad)).reshape(1, LANES)
    return wr, br


def moe_pre(streams, g, wr, br, n_groups, n_experts):
    d = streams[0][0].shape[2]
    tm = MOE_TM
    tile_offs = [0]
    in_specs, args = [], []
    for x, shift, scale in streams:
        bsz, seq, _ = x.shape
        nt = seq // tm
        n_tiles = bsz * nt
        off = tile_offs[-1]
        tile_offs.append(off + n_tiles)

        def tile(i, off=off, n_tiles=n_tiles):
            return jnp.clip(i - off, 0, n_tiles - 1)

        in_specs += [pl.BlockSpec((None, tm, d), lambda i, tile=tile, nt=nt: (tile(i) // nt, tile(i) % nt, 0)),
                     pl.BlockSpec((None, 1, d), lambda i, tile=tile, nt=nt: (tile(i) // nt, 0, 0)),
                     pl.BlockSpec((None, 1, d), lambda i, tile=tile, nt=nt: (tile(i) // nt, 0, 0))]
        args += [x, shift, scale]
    in_specs += [pl.BlockSpec((1, d), lambda i: (0, 0)),
                 pl.BlockSpec((2, d, LANES), lambda i: (0, 0, 0)),
                 pl.BlockSpec((1, LANES), lambda i: (0, 0))]
    args += [g.reshape(1, d), _split_bf16(wr), br]
    total = tile_offs[-1] * tm
    rout = pl.BlockSpec((tm, LANES), lambda i: (i, 0))
    tok, eid, gate = pl.pallas_call(
        functools.partial(_moe_pre_body, n_groups=n_groups, n_experts=n_experts, tile_offs=tuple(tile_offs)),
        grid=(tile_offs[-1],),
        in_specs=in_specs,
        out_specs=[pl.BlockSpec((tm, d), lambda i: (i, 0)),
                   pl.BlockSpec((1, SUBLANES, tm), lambda i: (i, 0, 0)), rout],
        out_shape=[jax.ShapeDtypeStruct((total, d), F32),
                   jax.ShapeDtypeStruct((tile_offs[-1], SUBLANES, tm), jnp.int32),
                   jax.ShapeDtypeStruct((total, LANES), F32)],
        compiler_params=_cparams("parallel"),
        name="moe_pre",
    )(*args)
    return tok, eid, gate, tile_offs[:-1]


def _start_row_gather(row_index, n_rows, src_hbm, dst_vmem, sem):
    def body(g, c):
        r0 = pl.multiple_of(g * SUBLANES, SUBLANES)
        dst_tile = dst_vmem.at[pl.ds(r0, SUBLANES)]
        for k in range(SUBLANES):
            pltpu.make_async_copy(src_hbm.at[pl.ds(row_index(r0 + k), 1)], dst_tile.at[pl.ds(k, 1)], sem).start()
        return c

    lax.fori_loop(0, n_rows // SUBLANES, body, 0, unroll=2)


def _wait_row_gather(n_rows, src_hbm, dst_vmem, sem):
    pltpu.make_async_copy(src_hbm.at[pl.ds(0, n_rows)], dst_vmem, sem).wait()


def _expert_body(bv_ref, rk_ref, pe_ref, tot_ref, p0_ref, ts_hbm, tok_ref, wg_hbm, wu_hbm, wd_hbm, o_ref,
                 xbuf, xsem, win, isem, wcache, stg, wsem, cnt, *, layer):
    i = pl.program_id(0)
    n = pl.num_programs(0)
    slot = i % 2
    cr, cc = stg.shape[1:]
    total = tot_ref[0]
    mats_hbm = (wg_hbm, wu_hbm, wd_hbm)

    @pl.when(i == 0)
    def _():
        cnt[0] = 0
        cnt[1] = 0

    def live(b):
        return jnp.logical_and(b < n, bv_ref[jnp.minimum(b, n - 1)] > 0)

    def window(b):
        s = b % 2
        lo = pl.multiple_of((p0_ref[jnp.minimum(b, n - 1)] // MOE_WIN_ALIGN) * MOE_WIN_ALIGN, MOE_WIN_ALIGN)
        return pltpu.make_async_copy(ts_hbm.at[pl.ds(lo, MOE_WIN)],
                                     win.at[pl.ds(pl.multiple_of(s * MOE_WIN, MOE_WIN), MOE_WIN)], isem.at[s])

    def gather(b):
        s = b % 2
        base = s * MOE_WIN + p0_ref[jnp.minimum(b, n - 1)] % MOE_WIN_ALIGN
        _start_row_gather(lambda r: win[base + r], MOE_BM, tok_ref, xbuf.at[s], xsem.at[s])

    @pl.when(jnp.logical_and(i == 0, live(0)))
    def _():
        window(0).start()
        window(0).wait()
        gather(0)

    @pl.when(jnp.logical_and(i == 0, live(1)))
    def _():
        window(1).start()

    @pl.when(live(i + 1))
    def _():
        window(i + 1).wait()
        gather(i + 1)

    @pl.when(live(i + 2))
    def _():
        window(i + 2).start()

    def chunk_geom(c):
        q = c % MOE_NCH
        m = q // 4
        sub = q % 4
        r0 = jnp.where(m < 2, sub, sub // 2) * cr
        c0 = jnp.where(m < 2, 0, sub % 2) * cc
        return m, pl.multiple_of(r0, cr), pl.multiple_of(c0, cc)

    def issue(c):
        e = pe_ref[c // MOE_NCH]
        m, r0, c0 = chunk_geom(c)
        s = c % MOE_STAGE
        for k, w_hbm in enumerate(mats_hbm):
            @pl.when(m == k)
            def _():
                pltpu.make_async_copy(w_hbm.at[layer, e, pl.ds(r0, cr), pl.ds(c0, cc)], stg.at[s],
                                      wsem.at[s]).start()

    def cast(c):
        s = c % MOE_STAGE
        pltpu.make_async_copy(wg_hbm.at[layer, 0, pl.ds(0, cr), pl.ds(0, cc)], stg.at[s], wsem.at[s]).wait()
        ws = (c // MOE_NCH) % 2
        q = c % MOE_NCH
        step = min(MOE_CAST_ROWS, cr)
        assert cr % step == 0

        def slab(k, carry):
            rows = pl.ds(pl.multiple_of(k * step, step), step)
            wcache[ws, q, rows, :] = stg[s, rows, :].astype(BF16)
            return carry

        lax.fori_loop(0, cr // step, slab, 0)

    valid = bv_ref[i] > 0
    rank = rk_ref[i]
    issued = cnt[0]
    done = cnt[1]
    limit = jnp.minimum(total, MOE_NCH * (rank + 2))
    need = jnp.where(valid, MOE_NCH * (rank + 1), done)

    def fill(issued, done):
        hi = jnp.minimum(limit, done + MOE_STAGE)

        def body(c, carry):
            issue(c)
            return carry

        lax.fori_loop(issued, hi, body, 0)
        return jnp.maximum(issued, hi)

    def cast_and_refill(c, issued):
        cast(c)
        more = issued < jnp.minimum(limit, c + 1 + MOE_STAGE)

        @pl.when(more)
        def _():
            issue(issued)

        return issued + more.astype(jnp.int32)

    issued = fill(issued, done)
    issued = lax.fori_loop(done, need, cast_and_refill, issued)
    done = jnp.maximum(done, need)

    @pl.when(valid)
    def _():
        ws = rank % 2
        _wait_row_gather(MOE_BM, tok_ref, xbuf.at[slot], xsem.at[slot])
        x = xbuf[slot].astype(BF16)
        gate = sum(_dot(x[:, k * cr:(k + 1) * cr], wcache[ws, k]) for k in range(4))
        up = sum(_dot(x[:, k * cr:(k + 1) * cr], wcache[ws, 4 + k]) for k in range(4))
        h = (gate * jax.nn.sigmoid(gate) * up).astype(BF16)
        for half in range(2):
            o_ref[:, half * cc:(half + 1) * cc] = sum(
                _dot(h[:, k * cr:(k + 1) * cr], wcache[ws, 8 + 2 * k + half]) for k in range(2))

    @pl.when(jnp.logical_not(valid))
    def _():
        o_ref[...] = jnp.zeros_like(o_ref)

    fetched = issued
    issued = lax.fori_loop(done, fetched, cast_and_refill, issued)
    done = jnp.maximum(done, fetched)
    last = i == n - 1
    tail = jnp.where(last, issued, done)

    def drain(c, carry):
        cast(c)
        return carry

    lax.fori_loop(done, tail, drain, 0)
    cnt[0] = issued
    cnt[1] = jnp.maximum(done, tail)


def moe_experts(tok, tok_sorted, block_p0, block_valid, block_rank, present, n_chunks, w_gate, w_up, w_down,
                layer):
    d = tok.shape[1]
    n_blocks = block_valid.shape[0]
    n_rows = n_blocks * MOE_BM
    dh = w_gate.shape[3]
    assert 2 * dh == d and MOE_NCH == 12
    cr, cc = d // 4, dh
    any_spec = pl.BlockSpec(memory_space=pl.ANY)
    grid_spec = pltpu.PrefetchScalarGridSpec(
        num_scalar_prefetch=5,
        grid=(n_blocks,),
        in_specs=[any_spec, any_spec, any_spec, any_spec, any_spec],
        out_specs=pl.BlockSpec((MOE_BM, d), lambda i, *_: (i, 0)),
        scratch_shapes=[pltpu.VMEM((2, MOE_BM, d), F32), pltpu.SemaphoreType.DMA((2,)),
                        pltpu.SMEM((2 * MOE_WIN,), jnp.int32), pltpu.SemaphoreType.DMA((2,)),
                        pltpu.VMEM((2, MOE_NCH, cr, cc), BF16),
                        pltpu.VMEM((MOE_STAGE, cr, cc), F32), pltpu.SemaphoreType.DMA((MOE_STAGE,)),
                        pltpu.SMEM((2,), jnp.int32)],
    )
    return pl.pallas_call(
        functools.partial(_expert_body, layer=layer),
        grid_spec=grid_spec,
        out_shape=jax.ShapeDtypeStruct((n_rows, d), F32),
        compiler_params=_cparams("arbitrary"),
        name="moe_experts",
    )(block_valid, block_rank, present, n_chunks, block_p0, tok_sorted, tok, w_gate, w_up, w_down)


def _combine_body(dest_ref, nxt_ref, os_ref, gate_ref, res_ref, gt_ref, fg_ref, *rest, final_norm, with_next):
    if with_next:
        ng_ref, nsh_ref, nsc_ref, o_ref, h_ref, buf, sem = rest
    else:
        o_ref, buf, sem = rest
    rows = res_ref.shape[0]
    i = pl.program_id(0)
    n = pl.num_programs(0)
    slot = i % 2

    def start(idx_ref, s):
        for k in range(TOP_K):
            _start_row_gather(lambda r, k=k: idx_ref[k * rows + r], rows, os_ref, buf.at[s, k], sem.at[s])

    @pl.when(i == 0)
    def _():
        start(dest_ref, 0)

    @pl.when(i + 1 < n)
    def _():
        start(nxt_ref, 1 - slot)

    for k in range(TOP_K):
        _wait_row_gather(rows, os_ref, buf.at[slot, k], sem.at[slot])
    gates = gate_ref[...]
    mo = gates[:, 0:1] * buf[slot, 0] + gates[:, 1:2] * buf[slot, 1]
    y = res_ref[...] + gt_ref[...] * mo
    if final_norm:
        ms = jnp.mean(y * y, axis=-1, keepdims=True)
        y = y * lax.rsqrt(ms + NORM_EPS) * fg_ref[...]
    o_ref[...] = y
    if with_next:
        h_ref[...] = _norm_mod(y, ng_ref[...], nsh_ref[...], nsc_ref[...]).astype(h_ref.dtype)


def moe_combine(os, dest, gates, tile0, res, gt, final_g, final_norm, next_mod=None):
    bsz, seq, d = res.shape
    rows = MOE_TM
    nt = seq // rows
    n = bsz * nt
    tile = pl.BlockSpec((None, rows, d), lambda i: (i // nt, i % nt, 0))
    mspec = pl.BlockSpec((None, 1, d), lambda i: (i // nt, 0, 0))
    rspec = pl.BlockSpec((1, d), lambda i: (0, 0))
    dspec = lambda step: pl.BlockSpec((SUBLANES * rows,), lambda i: (tile0 + step(i),), memory_space=pltpu.SMEM)
    in_specs = [dspec(lambda i: i), dspec(lambda i: jnp.minimum(i + 1, n - 1)),
                pl.BlockSpec(memory_space=pl.ANY),
                pl.BlockSpec((rows, LANES), lambda i: (tile0 + i, 0)),
                tile, mspec, rspec]
    args = [dest, dest, os, gates, res, gt, final_g.reshape(1, d)]
    out_specs, out_shape = [tile], [jax.ShapeDtypeStruct((bsz, seq, d), F32)]
    if next_mod is not None:
        in_specs += [rspec, mspec, mspec]
        args += [next_mod[0].reshape(1, d), next_mod[1], next_mod[2]]
        out_specs.append(tile)
        out_shape.append(jax.ShapeDtypeStruct((bsz, seq, d), BF16))
    outs = pl.pallas_call(
        functools.partial(_combine_body, final_norm=final_norm, with_next=next_mod is not None),
        grid=(n,),
        in_specs=in_specs,
        out_specs=out_specs,
        out_shape=out_shape,
        scratch_shapes=[pltpu.VMEM((2, TOP_K, rows, d), F32), pltpu.SemaphoreType.DMA((2,))],
        compiler_params=_cparams("arbitrary"),
        name="moe_combine",
    )(*args)
    return outs if next_mod is not None else outs[0]


def _plan_body(e_ref, dest_ref, tab_ref, blk_ref, present_ref, *, n_experts):
    n_rows, w = e_ref.shape
    e_all = e_ref[...]
    li = lax.broadcasted_iota(jnp.int32, (w, w), 0)
    lj = lax.broadcasted_iota(jnp.int32, (w, w), 1)
    incl = (li <= lj).astype(BF16)
    ri = lax.broadcasted_iota(jnp.int32, (n_rows, n_rows), 0)
    rj = lax.broadcasted_iota(jnp.int32, (n_rows, n_rows), 1)
    before = (rj < ri).astype(BF16)
    elane = lax.broadcasted_iota(jnp.int32, (n_rows, LANES), 1)
    row_tot = jnp.zeros((n_rows, LANES), F32)
    for e in range(n_experts):
        tot = jnp.sum((e_all == e).astype(F32), axis=1, keepdims=True)
        row_tot = row_tot + jnp.where(elane == e, tot, 0.0)
    rows_before = _dot(before, row_tot.astype(BF16))
    counts = jnp.sum(row_tot, axis=0, keepdims=True).astype(jnp.int32)
    lane1 = lax.broadcasted_iota(jnp.int32, (1, LANES), 1)

    def excl_prefix(v):
        acc = v
        sh = 1
        while sh < LANES:
            acc = acc + jnp.where(lane1 >= sh, pltpu.roll(acc, sh, axis=1), 0)
            sh *= 2
        return acc - v

    start = excl_prefix(counts)
    padded = (counts + (MOE_BM - 1)) // MOE_BM * MOE_BM
    pad_start = excl_prefix(padded)
    pad_end = pad_start + padded
    has = (counts > 0).astype(jnp.int32)
    rank = excl_prefix(has)
    n_chunks = MOE_NCH * jnp.sum(has.astype(F32), axis=1, keepdims=True).astype(jnp.int32)
    tab_ref[...] = jnp.concatenate([counts, start, pad_start, pad_end, jnp.broadcast_to(n_chunks, (1, LANES)),
                                    jnp.zeros((SUBLANES - 5, LANES), jnp.int32)], axis=0)
    nb = blk_ref.shape[0]
    f32 = lambda v: v.astype(F32)
    lane_b = lax.broadcasted_iota(jnp.int32, (nb, LANES), 1)
    first_row = lax.broadcasted_iota(jnp.int32, (nb, 1), 0) * MOE_BM
    is_expert = lane_b < n_experts
    expert = jnp.sum(f32(jnp.logical_and(is_expert, pad_end <= first_row)), axis=1, keepdims=True)
    expert = jnp.minimum(expert.astype(jnp.int32), n_experts - 1)
    mine = lane_b == expert
    pick = lambda v: jnp.sum(jnp.where(mine, f32(v), 0.0), axis=1, keepdims=True).astype(jnp.int32)
    live = first_row < jnp.max(f32(pad_end), axis=1, keepdims=True).astype(jnp.int32)
    p0 = jnp.where(live, pick(start) - pick(pad_start) + first_row, 0)
    blk_ref[...] = jnp.where(lane_b == 0, live.astype(jnp.int32),
                             jnp.where(lane_b == 1, pick(rank), jnp.where(lane_b == 2, p0, 0)))
    row_r = lax.broadcasted_iota(jnp.int32, (LANES, LANES), 0)
    lane_e = lax.broadcasted_iota(jnp.int32, (LANES, LANES), 1)
    hit_r = jnp.logical_and(has > 0, rank == row_r)
    present = jnp.sum(jnp.where(hit_r, f32(lane_e), 0.0), axis=1, keepdims=True).astype(jnp.int32)
    present_ref[...] = jnp.broadcast_to(present, (LANES, LANES))
    base = rows_before + pad_start.astype(F32)
    dest = jnp.zeros((n_rows, w), F32)
    for e in range(n_experts):
        hit = e_all == e
        within = _dot(hit.astype(BF16), incl)
        dest = dest + jnp.where(hit, within - 1.0 + base[:, e:e + 1], 0.0)
    dest_ref[...] = dest.astype(jnp.int32)


def _route_plan(eid, n_experts):
    n_tiles, r8, tm = eid.shape
    a = n_tiles * TOP_K * tm
    n_blocks = -(-a // MOE_BM) + n_experts
    assert (tm & (tm - 1)) == 0 and TOP_K == 2
    nb_pad = -(-n_blocks // SUBLANES) * SUBLANES
    dest, tab, blk, present = pl.pallas_call(
        functools.partial(_plan_body, n_experts=n_experts),
        out_shape=[jax.ShapeDtypeStruct((n_tiles * r8, tm), jnp.int32),
                   jax.ShapeDtypeStruct((SUBLANES, LANES), jnp.int32),
                   jax.ShapeDtypeStruct((nb_pad, LANES), jnp.int32),
                   jax.ShapeDtypeStruct((LANES, LANES), jnp.int32)],
        compiler_params=pltpu.CompilerParams(vmem_limit_bytes=V7X_VMEM_LIMIT_BYTES),
        name="moe_route_plan",
    )(eid.reshape(n_tiles * r8, tm))
    e_flat = eid[:, :TOP_K, :].reshape(-1)
    order = jnp.argsort(e_flat).astype(jnp.int32)
    shift = tm.bit_length() - 1
    tok_of = ((order >> (shift + 1)) << shift) | (order & (tm - 1))
    slack = -a % MOE_WIN_ALIGN + MOE_WIN
    tok_sorted = jnp.concatenate([tok_of, jnp.zeros((slack,), jnp.int32)])
    blk = blk[:n_blocks]
    return dest.reshape(-1), tok_sorted, blk[:, 2], blk[:, 0], blk[:, 1], present[:n_experts, 0], tab[4, :1]


def hier_moe(streams, norm_g, wg, bg, we, be, w_gate, w_up, w_down, layer, final_g, final_norm, next_mods=None,
             routed=None):
    n_groups = wg.shape[1]
    n_experts = we.shape[1]
    if routed is None:
        wr, br = _router_weights(wg, bg, we, be)
        tok, eid, gates, tile0s = moe_pre([s[:3] for s in streams], norm_g, wr, br, n_groups, n_experts)
    else:
        (tok, eid, gates), tile0s = routed, [0]
    dest, tok_sorted, block_p0, block_valid, block_rank, present, n_chunks = _route_plan(eid, n_experts)
    os = moe_experts(tok, tok_sorted, block_p0, block_valid, block_rank, present, n_chunks, w_gate, w_up, w_down,
                     layer)
    next_mods = next_mods or [None] * len(streams)
    return [moe_combine(os, dest, gates, tile0, x, gt, final_g, final_norm, nm)
            for (x, _, _, gt), tile0, nm in zip(streams, tile0s, next_mods)]


def _norm_mod_body(x_ref, g_ref, sh_ref, sc_ref, o_ref):
    o_ref[...] = _norm_mod(x_ref[...], g_ref[...], sh_ref[...], sc_ref[...]).astype(o_ref.dtype)


def norm_mod(x, g, shift, scale):
    bsz, seq, d = x.shape
    tm = min(seq, 512)
    mspec = pl.BlockSpec((None, 1, d), lambda b, i: (b, 0, 0))
    return pl.pallas_call(
        _norm_mod_body,
        grid=(bsz, seq // tm),
        in_specs=[pl.BlockSpec((None, tm, d), lambda b, i: (b, i, 0)),
                  pl.BlockSpec((1, d), lambda b, i: (0, 0)), mspec, mspec],
        out_specs=pl.BlockSpec((None, tm, d), lambda b, i: (b, i, 0)),
        out_shape=jax.ShapeDtypeStruct((bsz, seq, d), BF16),
        compiler_params=_cparams("parallel", "parallel"),
        name="norm_mod",
    )(x, g.reshape(1, d), shift, scale)


def _s5_arrange(h):
    bsz, t, d = h.shape
    c = t // (S5_SEGS * S5_TAU)
    h = h.reshape(bsz, S5_SEGS, c, S5_TAU, d // LANES, LANES)
    return h.transpose(2, 0, 1, 4, 3, 5).reshape(c * bsz * S5_SEGS, d * S5_TAU)


def _s5_unarrange(y, bsz):
    r, w = y.shape
    d = w // S5_TAU
    c = r // (bsz * S5_SEGS)
    y = y.reshape(c, bsz, S5_SEGS, d // LANES, S5_TAU, LANES)
    return y.transpose(1, 2, 0, 4, 3, 5).reshape(bsz, S5_SEGS * c * S5_TAU, d)


def _s5_operators(a_re, a_im, log_step, b_re, b_im, c_re, c_im):
    n_g, n_p = a_re.shape[1:]
    n_h = b_re.shape[-1]
    gpt = LANES // n_h
    n_j = n_g // gpt
    tau = S5_TAU
    assert tau * n_h == LANES and 2 * n_p == LANES
    lam_step = lax.complex(a_re, a_im) * jnp.exp(log_step)[..., None]
    lam_bar = jnp.exp(lam_step)
    b_bar = ((lam_bar - 1.0) / lax.complex(a_re, a_im))[..., None] * lax.complex(b_re, b_im)
    c_mat = lax.complex(c_re, c_im)
    ks = jnp.arange(tau + 1, dtype=F32)[None, :, None, None]
    pw = jnp.exp(lam_step[:, None] * ks)
    ein = functools.partial(jnp.einsum, precision=HIGHEST)
    inj_c, cl_c, lt = [], [], []
    tz_c = 0.0
    for d in range(2):
        pos = jnp.arange(tau) if d == 0 else jnp.arange(tau)[::-1]
        inj = (pw[d][tau - 1 - pos][..., None] * b_bar[d][None]).reshape(tau, n_j, gpt, n_p, n_h)
        inj = inj.transpose(1, 0, 2, 4, 3).reshape(n_j, tau * LANES, n_p)
        inj_c.append(jnp.concatenate([inj.real, inj.imag], axis=-1))
        cl = (c_mat[d][None] * pw[d][pos + 1][:, :, None, :]).reshape(tau, n_j, gpt, n_h, n_p)
        cl = cl.transpose(1, 2, 4, 0, 3).reshape(n_j, gpt * n_p, tau * n_h)
        cl_c.append(jnp.concatenate([cl.real, -cl.imag], axis=1))
        mk = ein('gop,kgp,gph->kgoh', c_mat[d], pw[d][:tau], b_bar[d]).real
        diff = pos[:, None] - pos[None, :]
        tz = jnp.where((diff >= 0)[:, :, None, None, None], mk[jnp.clip(diff, 0, tau - 1)], 0.0)
        tz = tz.reshape(tau, tau, n_j, gpt, n_h, n_h)
        tz_c = tz_c + tz.transpose(2, 1, 3, 5, 0, 4).reshape(n_j, tau * LANES, tau * n_h)
        lt_d = pw[d][tau].reshape(n_j, 1, gpt * n_p)
        lt.append(jnp.concatenate([lt_d.real, lt_d.imag], axis=-1))
    ws, wu, wh = s5_expand(jnp.stack(inj_c).astype(BF16), jnp.stack(cl_c).astype(BF16), tz_c.astype(BF16),
                           n_h, n_p)
    return ws, wu, wh, jnp.stack(lt).astype(F32)


def _s5_expand_body(inj_ref, cl_ref, tz_ref, ws_ref, wu_ref, wh_ref, *, n_h, n_p):
    rows = tz_ref.shape[0]
    gpt = LANES // n_h
    row = lax.broadcasted_iota(jnp.int32, (rows, LANES), 0)
    lane = lax.broadcasted_iota(jnp.int32, (rows, LANES), 1)
    sel_r = lax.broadcasted_iota(jnp.int32, (LANES, LANES), 0)
    sel_l = lax.broadcasted_iota(jnp.int32, (LANES, LANES), 1)
    grp_in = (row // n_h) % gpt
    grp_st = (row // n_p) % gpt

    def spread_out(m, t, grp_row):
        sel = jnp.logical_and(sel_r // n_h == t, sel_r % n_h == sel_l % n_h).astype(BF16)
        return jnp.where(grp_row == lane // n_h, _dot(m, sel), 0.0).astype(BF16)

    def spread_state(m, c, q, grp_row):
        sel = jnp.logical_and(sel_r // n_p == c, sel_r % n_p == sel_l % n_p).astype(BF16)
        return jnp.where(grp_row == (LANES // n_p) * q + lane // n_p, _dot(m, sel), 0.0).astype(BF16)

    w2 = 2 * gpt * n_p
    tz = tz_ref[...]
    for t in range(S5_TAU):
        wu_ref[:, t * LANES:(t + 1) * LANES] = spread_out(tz, t, grp_in)
    for d in range(2):
        cl = cl_ref[d]
        inj = inj_ref[d]
        for t in range(S5_TAU):
            wh_ref[d, :, t * LANES:(t + 1) * LANES] = spread_out(cl, t, grp_st)
        for c in range(2):
            for q in range(gpt * n_p // LANES):
                lo = d * w2 + c * gpt * n_p + q * LANES
                ws_ref[:, lo:lo + LANES] = spread_state(inj, c, q, grp_in)


def s5_expand(inj_c, cl_c, tz_c, n_h, n_p):
    n_j, rows, _ = tz_c.shape
    gpt = LANES // n_h
    w2 = 2 * gpt * n_p
    assert rows == S5_TAU * LANES == w2
    cspec = pl.BlockSpec((2, None, rows, LANES), lambda j: (0, j, 0, 0))
    return pl.pallas_call(
        functools.partial(_s5_expand_body, n_h=n_h, n_p=n_p),
        grid=(n_j,),
        in_specs=[cspec, cspec, pl.BlockSpec((None, rows, LANES), lambda j: (j, 0, 0))],
        out_specs=[pl.BlockSpec((None, rows, 2 * w2), lambda j: (j, 0, 0)),
                   pl.BlockSpec((None, rows, rows), lambda j: (j, 0, 0)),
                   pl.BlockSpec((2, None, w2, rows), lambda j: (0, j, 0, 0))],
        out_shape=[jax.ShapeDtypeStruct((n_j, rows, 2 * w2), BF16),
                   jax.ShapeDtypeStruct((n_j, rows, rows), BF16),
                   jax.ShapeDtypeStruct((2, n_j, w2, rows), BF16)],
        compiler_params=_cparams("parallel"),
        name="s5_expand",
    )(inj_c, cl_c, tz_c)


def _s5_inj_body(xc_ref, xl_ref, w_ref, oc_ref, ol_ref):
    w = w_ref[...]
    ol_ref[...] = _dot(xl_ref[...], w)

    @pl.when(pl.program_id(1) == 0)
    def _():
        oc_ref[...] = _dot(xc_ref[...], w)


def s5_inject(xr_c, xr_l, ws):
    r_c, r_l = xr_c.shape[0], xr_l.shape[0]
    n_j, k, n = ws.shape
    tm = r_l // 2 if r_l % 32 == 0 else r_l
    return pl.pallas_call(
        _s5_inj_body,
        grid=(n_j, r_l // tm),
        in_specs=[pl.BlockSpec((r_c, k), lambda j, i: (0, j)),
                  pl.BlockSpec((tm, k), lambda j, i: (i, j)),
                  pl.BlockSpec((None, k, n), lambda j, i: (j, 0, 0))],
        out_specs=[pl.BlockSpec((r_c, n), lambda j, i: (0, j)),
                   pl.BlockSpec((tm, n), lambda j, i: (i, j))],
        out_shape=[jax.ShapeDtypeStruct((r_c, n_j * n), F32), jax.ShapeDtypeStruct((r_l, n_j * n), F32)],
        compiler_params=_cparams("parallel", "arbitrary"),
        name="s5_inject",
    )(xr_c, xr_l, ws)


def _cmul(ar, ai, br, bi):
    return ar * br - ai * bi, ar * bi + ai * br


def _s5_scan_body(sc_ref, sl_ref, lt_ref, h_ref, raw_ref, *, n_ctx, n_lat, bsz):
    d = pl.program_id(1)
    w2 = lt_ref.shape[-1]
    w = w2 // 2
    rows = bsz * S5_SEGS
    seg = lax.broadcasted_iota(jnp.int32, (rows, 1), 0) % S5_SEGS
    is_late = seg != d
    lam_r = lt_ref[:, 0:w]
    lam_i = lt_ref[:, w:w2]
    zero = jnp.zeros((rows, w), F32)
    one = (jnp.ones((1, w), F32), jnp.zeros((1, w), F32))

    def swap_segments(x):
        return jnp.where(seg == 0, pltpu.roll(x, rows - 1, axis=0), pltpu.roll(x, 1, axis=0))

    def phase(s_ref, n_steps, hin_r, hin_i, write):
        def chunk(k):
            return jnp.where(d == 0, k, n_steps - 1 - k)

        def step_raw(k, carry):
            hr, hi = carry
            c = chunk(k)
            raw_ref[c, :, 0:w] = hr
            raw_ref[c, :, w:w2] = hi
            nr, ni = _cmul(lam_r, lam_i, hr, hi)
            return nr + s_ref[c, :, 0:w], ni + s_ref[c, :, w:w2]

        er, ei = lax.fori_loop(0, n_steps, step_raw, (zero, zero), unroll=S5_SCAN_UNROLL)
        pr, pi = lax.fori_loop(0, n_steps, lambda k, q: _cmul(lam_r, lam_i, *q), one)
        dr, di = _cmul(pr, pi, hin_r, hin_i)
        first_r = jnp.where(is_late, 0.0, er + dr)
        first_i = jnp.where(is_late, 0.0, ei + di)
        carry_r = jnp.where(is_late, swap_segments(first_r), hin_r)
        carry_i = jnp.where(is_late, swap_segments(first_i), hin_i)
        if write:
            def step_fix(k, q):
                c = chunk(k)
                fr, fi = _cmul(q[0], q[1], carry_r, carry_i)
                h_ref[c, :, 0:w] = (raw_ref[c, :, 0:w] + fr).astype(h_ref.dtype)
                h_ref[c, :, w:w2] = (raw_ref[c, :, w:w2] + fi).astype(h_ref.dtype)
                return _cmul(lam_r, lam_i, q[0], q[1])

            lax.fori_loop(0, n_steps, step_fix, one, unroll=S5_SCAN_UNROLL)
        lr, li = _cmul(pr, pi, carry_r, carry_i)
        last_r = jnp.where(is_late, er + lr, 0.0)
        last_i = jnp.where(is_late, ei + li, 0.0)
        return (jnp.where(is_late, 0.0, swap_segments(last_r)), jnp.where(is_late, 0.0, swap_segments(last_i)))

    hr, hi = phase(sc_ref, n_ctx, zero, zero, False)
    phase(sl_ref, n_lat, hr, hi, True)


def s5_scan(s_ctx, s_lat, lt, bsz):
    assert S5_SEGS == 2
    n_ctx, rows, _ = s_ctx.shape
    n_lat = s_lat.shape[0]
    n_j = lt.shape[1]
    w2 = lt.shape[-1]
    return pl.pallas_call(
        functools.partial(_s5_scan_body, n_ctx=n_ctx, n_lat=n_lat, bsz=bsz),
        grid=(n_j, 2),
        in_specs=[pl.BlockSpec((n_ctx, rows, w2), lambda j, d: (0, 0, 2 * j + d)),
                  pl.BlockSpec((n_lat, rows, w2), lambda j, d: (0, 0, 2 * j + d)),
                  pl.BlockSpec((None, None, 1, w2), lambda j, d: (d, j, 0, 0))],
        out_specs=pl.BlockSpec((None, n_lat, rows, w2), lambda j, d: (d, 0, 0, j)),
        out_shape=jax.ShapeDtypeStruct((2, n_lat, rows, n_j * w2), BF16),
        scratch_shapes=[pltpu.VMEM((max(n_ctx, n_lat), rows, w2), F32)],
        compiler_params=_cparams("parallel", "parallel"),
        name="s5_scan",
    )(s_ctx, s_lat, lt)


def _s5_out_body(x_ref, hf_ref, hb_ref, wu_ref, whf_ref, whb_ref, o_ref):
    o_ref[...] = (_dot(x_ref[...], wu_ref[...]) + _dot(hf_ref[...], whf_ref[...])
                  + _dot(hb_ref[...], whb_ref[...])).astype(o_ref.dtype)


def s5_readout(xr, h, wu, wh):
    r = xr.shape[0]
    n_j, k, n = wu.shape
    w2 = wh.shape[2]
    tm = min(r, 1024)
    return pl.pallas_call(
        _s5_out_body,
        grid=(n_j, r // tm),
        in_specs=[pl.BlockSpec((tm, k), lambda j, i: (i, j)),
                  pl.BlockSpec((None, tm, w2), lambda j, i: (0, i, j)),
                  pl.BlockSpec((None, tm, w2), lambda j, i: (1, i, j)),
                  pl.BlockSpec((None, k, n), lambda j, i: (j, 0, 0)),
                  pl.BlockSpec((None, None, w2, n), lambda j, i: (0, j, 0, 0)),
                  pl.BlockSpec((None, None, w2, n), lambda j, i: (1, j, 0, 0))],
        out_specs=pl.BlockSpec((tm, n), lambda j, i: (i, j)),
        out_shape=jax.ShapeDtypeStruct((r, n_j * n), BF16),
        compiler_params=_cparams("parallel", "parallel"),
        name="s5_readout",
    )(xr, h, h, wu, wh, wh)


def _gelu_tanh(x):
    return 0.5 * x * (1.0 + jnp.tanh(math.sqrt(2.0 / math.pi) * (x + 0.044715 * (x * x * x))))


def _s5_glu_body(x_ref, y_ref, g_ref, sh_ref, sc_ref, dk_ref, w1_ref, w2_ref, b1_ref, b2_ref, gt_ref,
                 mg_ref, msh_ref, msc_ref, wr_ref, br_ref, o_ref, tok_ref, eid_ref, gate_ref, *, n_groups, n_experts):
    x = x_ref[...]
    u = _norm_mod(x, g_ref[...], sh_ref[...], sc_ref[...])
    y = _gelu_tanh(y_ref[...].astype(F32) + dk_ref[...] * u).astype(BF16)
    o = (_dot(y, w1_ref[...]) + b1_ref[...]) * jax.nn.sigmoid(_dot(y, w2_ref[...]) + b2_ref[...])
    xn = x + gt_ref[...] * o
    o_ref[...] = xn
    tok = _norm_mod(xn, mg_ref[...], msh_ref[...], msc_ref[...])
    _route_tokens(tok, wr_ref, br_ref, tok_ref, eid_ref, gate_ref, n_groups, n_experts)


def s5_glu(x, y, g, shift, scale, d_skip, w1, b1, w2, b2, gate, moe_g, moe_shift, moe_scale, wr, br, n_groups,
           n_experts):
    bsz, seq, d = x.shape
    tm = min(seq, 512)
    nt = seq // tm
    sub = tm // MOE_TM
    row = lambda a: a.reshape(1, d)
    rspec = pl.BlockSpec((1, d), lambda b, i: (0, 0))
    mspec = pl.BlockSpec((None, 1, d), lambda b, i: (b, 0, 0))
    tile = pl.BlockSpec((None, tm, d), lambda b, i: (b, i, 0))
    wspec = pl.BlockSpec((d, d), lambda b, i: (0, 0), pipeline_mode=pl.Buffered(1))
    return pl.pallas_call(
        functools.partial(_s5_glu_body, n_groups=n_groups, n_experts=n_experts),
        grid=(bsz, nt),
        in_specs=[tile, tile, rspec, mspec, mspec, rspec, wspec, wspec, rspec, rspec, mspec,
                  rspec, mspec, mspec,
                  pl.BlockSpec((2, d, LANES), lambda b, i: (0, 0, 0)),
                  pl.BlockSpec((1, LANES), lambda b, i: (0, 0))],
        out_specs=[tile,
                   pl.BlockSpec((tm, d), lambda b, i: (b * nt + i, 0)),
                   pl.BlockSpec((sub, SUBLANES, MOE_TM), lambda b, i: (b * nt + i, 0, 0)),
                   pl.BlockSpec((tm, LANES), lambda b, i: (b * nt + i, 0))],
        out_shape=[jax.ShapeDtypeStruct((bsz, seq, d), F32),
                   jax.ShapeDtypeStruct((bsz * seq, d), F32),
                   jax.ShapeDtypeStruct((bsz * seq // MOE_TM, SUBLANES, MOE_TM), jnp.int32),
                   jax.ShapeDtypeStruct((bsz * seq, LANES), F32)],
        compiler_params=_cparams("parallel", "parallel"),
        name="s5_glu",
    )(x, y, row(g), shift, scale, row(d_skip), w1, w2, row(b1), row(b2), gate,
      row(moe_g), moe_shift, moe_scale, _split_bf16(wr), br)


def s5_mix(xl, hl, hc, g, sh_l, sc_l, gate_l, a_re, a_im, log_step, b_re, b_im, c_re, c_im, d_skip,
           w1, b1, w2, b2, moe_mod, router):
    bsz, seq, d = xl.shape
    xr_c = _s5_arrange(hc)
    xr_l = _s5_arrange(hl)
    rows = bsz * S5_SEGS
    n_ctx = xr_c.shape[0] // rows
    n_lat = xr_l.shape[0] // rows
    ws, wu, wh, lt = _s5_operators(a_re, a_im, log_step, b_re, b_im, c_re, c_im)
    s_ctx, s_lat = s5_inject(xr_c, xr_l, ws)
    h = s5_scan(s_ctx.reshape(n_ctx, rows, -1), s_lat.reshape(n_lat, rows, -1), lt, bsz)
    y = s5_readout(xr_l, h.reshape(2, n_lat * rows, -1), wu, wh)
    y = _s5_unarrange(y, bsz)
    wr, br = _router_weights(*router)
    xl, tok, eid, gates = s5_glu(xl, y, g, sh_l, sc_l, d_skip, w1.astype(BF16), b1, w2.astype(BF16), b2, gate_l,
                                 *moe_mod, wr, br, router[0].shape[1], router[2].shape[1])
    return xl, (tok, eid, gates)


def hyena_mix(x, g, shift, scale, gate, w_in, b_in, conv_w, conv_b, fw1, fb1, fw2, fb2, fw3, freq, skip,
              w_out, b_out):
    seq = x.shape[1]
    if seq >= FFT_MIN_SEQ and (2 * seq) % (2 * FFT_N2) == 0:
        a, dd, _ = hyena_filter_taps(seq, fw1, fb1, fw2, fb2, fw3, freq, F32)
        v, x0 = hyena_in(x, g, shift, scale, w_in, b_in, conv_w, conv_b, F32)
        yg = hyena_conv_fft(v, x0, skip, a, dd)
    else:
        cmat, smat = dft_matrices(seq)
        a, dd, kn = hyena_filter_taps(seq, fw1, fb1, fw2, fb2, fw3, freq, BF16)
        kr, ki = hyena_filter_dft(a, dd, cmat, smat)
        v, x0 = hyena_in(x, g, shift, scale, w_in, b_in, conv_w, conv_b, BF16)
        yg = hyena_conv(v, x0, skip, kr, ki, kn, cmat, smat)
    return mm_residual(yg, w_out, b_out, x, gate)


def kernel(x, c, ctx, c_ctx, ada_w, ada_b, norm_g, final_g, hy_w_in, hy_b_in, hy_conv_w, hy_conv_b, hy_fw1,
           hy_fb1, hy_fw2, hy_fb2, hy_fw3, hy_freq, hy_skip, hy_w_out, hy_b_out, s5_a_re, s5_a_im,
           s5_log_step, s5_b_re, s5_b_im, s5_c_re, s5_c_im, s5_d, s5_w1, s5_b1, s5_w2, s5_b2, moe_wg, moe_bg,
           moe_we, moe_be, moe_w_gate, moe_w_up, moe_w_down):
    bsz, _, d = x.shape
    depth = ada_w.shape[0]
    assert depth == 2 and bsz < SUBLANES
    c_all = jnp.concatenate([c, c_ctx[None, :], jnp.zeros((SUBLANES - bsz - 1, d), F32)], axis=0)
    mods = ada_mod(c_all, ada_w, ada_b)

    def mod_rows(layer, k):
        lat = mods[layer, :bsz, k * d:(k + 1) * d][:, None, :]
        cx = jnp.broadcast_to(mods[layer, bsz, k * d:(k + 1) * d][None, None, :], (bsz, 1, d))
        return lat, cx

    (sh_a, csh_a), (sc_a, csc_a), (gt_a, cgt_a) = mod_rows(0, 0), mod_rows(0, 1), mod_rows(0, 2)
    (sh_f, csh_f), (sc_f, csc_f), (gt_f, cgt_f) = mod_rows(0, 3), mod_rows(0, 4), mod_rows(0, 5)
    hy = (hy_w_in[0].astype(BF16), hy_b_in[0], hy_conv_w[0], hy_conv_b[0], hy_fw1[0], hy_fb1[0], hy_fw2[0],
          hy_fb2[0], hy_fw3[0], hy_freq[0], hy_skip[0], hy_w_out[0].astype(BF16), hy_b_out[0])
    xl = hyena_mix(x, norm_g[0, 0], sh_a, sc_a, gt_a, *hy)
    xc = hyena_mix(ctx, norm_g[0, 0], csh_a, csc_a, cgt_a, *hy)
    (sh_a, csh_a), (sc_a, csc_a), (gt_a, _) = mod_rows(1, 0), mod_rows(1, 1), mod_rows(1, 2)
    (xl, hl), (_, hc) = hier_moe([(xl, sh_f, sc_f, gt_f), (xc, csh_f, csc_f, cgt_f)], norm_g[0, 1],
                                 moe_wg[0], moe_bg[0], moe_we[0], moe_be[0], moe_w_gate, moe_w_up, moe_w_down, 0,
                                 final_g, False,
                                 next_mods=[(norm_g[1, 0], sh_a, sc_a), (norm_g[1, 0], csh_a, csc_a)])
    (sh_f, _), (sc_f, _), (gt_f, _) = mod_rows(1, 3), mod_rows(1, 4), mod_rows(1, 5)
    router = (moe_wg[1], moe_bg[1], moe_we[1], moe_be[1])
    xl, routed = s5_mix(xl, hl, hc, norm_g[1, 0], sh_a, sc_a, gt_a, s5_a_re[0], s5_a_im[0], s5_log_step[0],
                        s5_b_re[0], s5_b_im[0], s5_c_re[0], s5_c_im[0], s5_d[0], s5_w1[0], s5_b1[0], s5_w2[0],
                        s5_b2[0], (norm_g[1, 1], sh_f, sc_f), router)
    (out,) = hier_moe([(xl, sh_f, sc_f, gt_f)], norm_g[1, 1], *router, moe_w_gate, moe_w_up, moe_w_down, 1,
                      final_g, True, routed=routed)
    return out
```

```python
import functools
import math

import numpy as np
import jax
import jax.numpy as jnp
from jax import lax
from jax.experimental import pallas as pl
from jax.experimental.pallas import tpu as pltpu

F32 = jnp.float32
BF16 = jnp.bfloat16
HIGHEST = lax.Precision.HIGHEST

NORM_EPS = 1e-6
HY_DECAY_TARGET = 1e-2
HY_FAST_PCT = 0.3
HY_SLOW_PCT = 1.5
TOP_K = 2

V7X_VMEM_LIMIT_BYTES = 56 * 1024 * 1024
LANES = 128
SUBLANES = 8
S5_TAU = 8
S5_SEGS = 2
S5_SCAN_UNROLL = 8
MOE_TM = 256
MOE_BM = 256
MOE_NCH = 12
MOE_STAGE = 4
MOE_CAST_ROWS = 128
MOE_WIN_ALIGN = 1024
MOE_WIN = 2 * MOE_WIN_ALIGN
NEG_BIG = -1e30


def _cparams(*sem):
    return pltpu.CompilerParams(dimension_semantics=sem, vmem_limit_bytes=V7X_VMEM_LIMIT_BYTES)


def _norm_mod(x, g, shift, scale):
    ms = jnp.mean(x * x, axis=-1, keepdims=True)
    return (x * lax.rsqrt(ms + NORM_EPS) * g) * (1.0 + scale) + shift


def _dot(a, b):
    return jnp.dot(a, b, preferred_element_type=F32)


def _ada_body(c_ref, w_ref, b_ref, o_ref):
    x = c_ref[...]
    s = (x * jax.nn.sigmoid(x)).astype(BF16)
    o_ref[...] = _dot(s, w_ref[...].astype(BF16)) + b_ref[...]


def ada_mod(c_all, ada_w, ada_b):
    depth, d, n = ada_w.shape
    tn = min(n, 1024)
    return pl.pallas_call(
        _ada_body,
        grid=(depth, n // tn),
        in_specs=[pl.BlockSpec((SUBLANES, d), lambda l, j: (0, 0)),
                  pl.BlockSpec((None, d, tn), lambda l, j: (l, 0, j)),
                  pl.BlockSpec((None, 1, tn), lambda l, j: (l, 0, j))],
        out_specs=pl.BlockSpec((None, SUBLANES, tn), lambda l, j: (l, 0, j)),
        out_shape=jax.ShapeDtypeStruct((depth, SUBLANES, n), F32),
        compiler_params=_cparams("parallel", "parallel"),
        name="ada_mod",
    )(c_all, ada_w, ada_b.reshape(depth, 1, n))


def _hy_in_body(xp_ref, xm_ref, xn_ref, g_ref, sh_ref, sc_ref,
                w0_ref, w1_ref, w2_ref, b0_ref, b1_ref, b2_ref,
                cw0_ref, cw1_ref, cw2_ref, cb0_ref, cb1_ref, cb2_ref,
                v_ref, x0_ref):
    i = pl.program_id(2)
    ni = pl.num_programs(2)
    tm = xm_ref.shape[0]
    x = jnp.concatenate([xp_ref[...], xm_ref[...], xn_ref[...]], axis=0)
    h = _norm_mod(x, g_ref[...], sh_ref[...], sc_ref[...]).astype(BF16)
    rows = lax.broadcasted_iota(jnp.int32, (tm + 2 * SUBLANES, 1), 0)
    valid = jnp.logical_and(jnp.logical_or(rows >= SUBLANES, i > 0),
                            jnp.logical_or(rows < tm + SUBLANES, i < ni - 1))

    def part(w_ref, b_ref, cw_ref, cb_ref):
        z = jnp.where(valid, _dot(h, w_ref[...]) + b_ref[...], 0.0)
        cw = cw_ref[...]
        zp = pltpu.roll(z, 1, axis=0)[SUBLANES:tm + SUBLANES]
        zn = pltpu.roll(z, tm + 2 * SUBLANES - 1, axis=0)[SUBLANES:tm + SUBLANES]
        return zp * cw[0:1] + z[SUBLANES:tm + SUBLANES] * cw[1:2] + zn * cw[2:3] + cb_ref[...]

    x0 = part(w0_ref, b0_ref, cw0_ref, cb0_ref)
    x1 = part(w1_ref, b1_ref, cw1_ref, cb1_ref)
    v = part(w2_ref, b2_ref, cw2_ref, cb2_ref) * x1
    v_ref[...] = v.astype(v_ref.dtype)
    x0_ref[...] = x0.astype(BF16)


def hyena_in(x, g, shift, scale, w_in, b_in, conv_w, conv_b, v_dtype):
    bsz, seq, d = x.shape
    tm = min(seq, 512)
    tn = min(d, 1024)
    nj = d // tn
    r8 = tm // SUBLANES
    last8 = seq // SUBLANES - 1
    row = lambda a: a.reshape(1, -1)
    wspec = lambda k: pl.BlockSpec((d, tn), lambda j, b, i: (0, k * nj + j))
    rspec = lambda k: pl.BlockSpec((1, tn), lambda j, b, i: (0, k * nj + j))
    cspec = lambda k: pl.BlockSpec((3, tn), lambda j, b, i: (0, k * nj + j))
    mspec = pl.BlockSpec((None, 1, d), lambda j, b, i: (b, 0, 0))
    out_spec = pl.BlockSpec((None, tm, tn), lambda j, b, i: (b, i, j))
    return pl.pallas_call(
        _hy_in_body,
        grid=(nj, bsz, seq // tm),
        in_specs=[pl.BlockSpec((None, SUBLANES, d), lambda j, b, i: (b, jnp.maximum(i * r8 - 1, 0), 0)),
                  pl.BlockSpec((None, tm, d), lambda j, b, i: (b, i, 0)),
                  pl.BlockSpec((None, SUBLANES, d), lambda j, b, i: (b, jnp.minimum((i + 1) * r8, last8), 0)),
                  pl.BlockSpec((1, d), lambda j, b, i: (0, 0)), mspec, mspec,
                  wspec(0), wspec(1), wspec(2), rspec(0), rspec(1), rspec(2),
                  cspec(0), cspec(1), cspec(2), rspec(0), rspec(1), rspec(2)],
        out_specs=[out_spec, out_spec],
        out_shape=[jax.ShapeDtypeStruct((bsz, seq, d), v_dtype), jax.ShapeDtypeStruct((bsz, seq, d), BF16)],
        compiler_params=_cparams("parallel", "parallel", "parallel"),
        name="hyena_in",
    )(x, x, x, row(g), shift, scale, w_in, w_in, w_in, row(b_in), row(b_in), row(b_in),
      conv_w, conv_w, conv_w, row(conv_b), row(conv_b), row(conv_b))


def _dft_tables(seq, blk):
    n = 2 * seq
    s = np.arange(seq, dtype=np.int64)[None, :]
    fl = np.arange(blk, dtype=np.int64)[:, None]
    fh = (np.arange(seq // blk, dtype=np.int64) * blk)[:, None]
    w = 2.0 * math.pi / n
    ang_b = ((fl * s) % n) * w
    ang_a = ((fh * s) % n) * w
    f32 = lambda m: jnp.asarray(m.astype(np.float32))
    return (f32(np.cos(ang_a)[:, None, :]), f32(np.sin(ang_a)[:, None, :]), f32(np.cos(ang_b)), f32(np.sin(ang_b)))


def _dft_gen_body(ca_ref, sa_ref, cb_ref, sb_ref, c_ref, s_ref):
    ca, sa, cb, sb = ca_ref[...], sa_ref[...], cb_ref[...], sb_ref[...]
    c_ref[...] = (ca * cb - sa * sb).astype(BF16)
    s_ref[...] = (sa * cb + ca * sb).astype(BF16)


def dft_matrices(seq):
    blk = min(seq, 256)
    ca, sa, cb, sb = _dft_tables(seq, blk)
    aspec = pl.BlockSpec((None, 1, seq), lambda i: (i, 0, 0))
    bspec = pl.BlockSpec((blk, seq), lambda i: (0, 0))
    ospec = pl.BlockSpec((blk, seq), lambda i: (i, 0))
    return pl.pallas_call(
        _dft_gen_body,
        grid=(seq // blk,),
        in_specs=[aspec, aspec, bspec, bspec],
        out_specs=[ospec, ospec],
        out_shape=[jax.ShapeDtypeStruct((seq, seq), BF16)] * 2,
        compiler_params=_cparams("parallel"),
        name="dft_matrices",
    )(ca, sa, cb, sb)


def _alt_sign(rows):
    return jnp.where((rows & 1) == 0, 1.0, -1.0).astype(F32)


def _filt_body(h2_ref, wf_ref, wb_ref, dl_ref, a_ref, d_ref, ny_ref):
    seq = h2_ref.shape[0]
    h2 = h2_ref[...]
    row = lax.broadcasted_iota(jnp.int32, (seq, 1), 0)
    t = row.astype(F32) * (1.0 / (seq - 1))
    win = jnp.exp(-t * dl_ref[...])
    h_hi = h2.astype(BF16)
    h_lo = (h2 - h_hi.astype(F32)).astype(BF16)
    dot3 = lambda w_ref: _dot(h_hi, w_ref[0]) + _dot(h_hi, w_ref[1]) + _dot(h_lo, w_ref[0])
    hf = dot3(wf_ref) * win
    hb = dot3(wb_ref) * win
    hb = jnp.where(row == 0, 0.0, hb)
    nrm = (jnp.sum(jnp.abs(hf), axis=0, keepdims=True) + jnp.sum(jnp.abs(hb), axis=0, keepdims=True))
    inv = 1.0 / nrm
    a = (hf + hb) * inv
    a_ref[...] = a.astype(a_ref.dtype)
    d_ref[...] = ((hb - hf) * inv).astype(d_ref.dtype)
    ny = jnp.sum(a * _alt_sign(row), axis=0, keepdims=True) * (1.0 / (2 * seq))
    ny_ref[...] = jnp.broadcast_to(ny, ny_ref.shape)


def _khat_body(a_ref, d_ref, c_ref, s_ref, kr_ref, ki_ref):
    i = pl.program_id(1)
    tm = c_ref.shape[0]
    seq = c_ref.shape[1]
    f = i * tm + lax.broadcasted_iota(jnp.int32, (tm, 1), 0)
    w = jnp.where(f == 0, 1.0, 2.0).astype(F32) * (1.0 / (2 * seq))
    kr_ref[...] = _dot(c_ref[...], a_ref[...]) * w
    ki_ref[...] = _dot(s_ref[...], d_ref[...]) * w


def hyena_filter_taps(seq, fw1, fb1, fw2, fb2, fw3, freq, taps_dtype):
    d = fw3.shape[1] // 2
    bands_n = (fw1.shape[0] - 1) // 2
    t = np.linspace(0.0, 1.0, seq)[:, None]
    w = (2.0 * math.pi / seq) * np.arange(seq)[:, None]
    bands = np.linspace(1e-4, bands_n - 1, bands_n)[None, :]
    z = jnp.asarray(np.concatenate([t, np.cos(bands * w), -np.sin(bands * w)], axis=-1).astype(np.float32))
    h = jnp.sin(freq * (jnp.dot(z, fw1, precision=HIGHEST) + fb1))
    h2 = jnp.sin(freq * (jnp.dot(h, fw2, precision=HIGHEST) + fb2))
    max_decay = math.log(HY_DECAY_TARGET) / HY_FAST_PCT
    min_decay = math.log(HY_DECAY_TARGET) / HY_SLOW_PCT
    deltas = jnp.abs(jnp.linspace(min_decay, max_decay, d, dtype=F32))[None, :]

    order = h2.shape[1]
    fw3_split = _split_bf16(fw3)
    tn = min(d, 256)
    nj = d // tn
    return pl.pallas_call(
        _filt_body,
        grid=(nj,),
        in_specs=[pl.BlockSpec((seq, order), lambda j: (0, 0)),
                  pl.BlockSpec((2, order, tn), lambda j: (0, 0, j)),
                  pl.BlockSpec((2, order, tn), lambda j: (0, 0, nj + j)),
                  pl.BlockSpec((1, tn), lambda j: (0, j))],
        out_specs=[pl.BlockSpec((seq, tn), lambda j: (0, j)),
                   pl.BlockSpec((seq, tn), lambda j: (0, j)),
                   pl.BlockSpec((SUBLANES, tn), lambda j: (0, j))],
        out_shape=[jax.ShapeDtypeStruct((seq, d), taps_dtype), jax.ShapeDtypeStruct((seq, d), taps_dtype),
                   jax.ShapeDtypeStruct((SUBLANES, d), F32)],
        compiler_params=_cparams("parallel"),
        name="hyena_filter_taps",
    )(h2, fw3_split, fw3_split, deltas)


def hyena_filter_dft(a, dd, cmat, smat):
    seq, d = a.shape
    tm = min(seq, 512)
    tn2 = min(d, 512)
    return pl.pallas_call(
        _khat_body,
        grid=(d // tn2, seq // tm),
        in_specs=[pl.BlockSpec((seq, tn2), lambda j, i: (0, j)),
                  pl.BlockSpec((seq, tn2), lambda j, i: (0, j)),
                  pl.BlockSpec((tm, seq), lambda j, i: (i, 0)),
                  pl.BlockSpec((tm, seq), lambda j, i: (i, 0))],
        out_specs=[pl.BlockSpec((tm, tn2), lambda j, i: (i, j))] * 2,
        out_shape=[jax.ShapeDtypeStruct((seq, d), F32)] * 2,
        compiler_params=_cparams("parallel", "parallel"),
        name="hyena_filter_dft",
    )(a, dd, cmat, smat)


def _dft_fwd_body(v_ref, c_ref, s_ref, kr_ref, ki_ref, kn_ref, ya_ref, yb_ref, yn_ref):
    i = pl.program_id(2)
    v = v_ref[...]
    vr = _dot(c_ref[...], v)
    p = _dot(s_ref[...], v)
    kr = kr_ref[...]
    ki = ki_ref[...]
    ya_ref[...] = (vr * kr + p * ki).astype(BF16)
    yb_ref[...] = (p * kr - vr * ki).astype(BF16)

    @pl.when(i == 0)
    def _():
        seq = v.shape[0]
        row = lax.broadcasted_iota(jnp.int32, (seq, 1), 0)
        vl = jnp.sum(v.astype(F32) * _alt_sign(row), axis=0, keepdims=True)
        yn_ref[...] = jnp.broadcast_to(vl * kn_ref[0:1, :], yn_ref.shape)


def _dft_inv_body(ya_ref, yb_ref, c_ref, s_ref, v_ref, x0_ref, skip_ref, yn_ref, o_ref):
    i = pl.program_id(2)
    tm = c_ref.shape[0]
    acc = _dot(c_ref[...], ya_ref[...]) + _dot(s_ref[...], yb_ref[...])
    t = i * tm + lax.broadcasted_iota(jnp.int32, (tm, 1), 0)
    y = acc + _alt_sign(t) * yn_ref[0:1, :] + skip_ref[...] * v_ref[...].astype(F32)
    o_ref[...] = (y * x0_ref[...].astype(F32)).astype(BF16)


def hyena_conv(v, x0, skip, kr, ki, kn, cmat, smat):
    bsz, seq, d = v.shape
    tm = min(seq, 512)
    tn = min(d, 512)
    grid = (bsz, d // tn, seq // tm)
    full = pl.BlockSpec((None, seq, tn), lambda b, j, i: (b, 0, j))
    mat = pl.BlockSpec((tm, seq), lambda b, j, i: (i, 0))
    tile = pl.BlockSpec((None, tm, tn), lambda b, j, i: (b, i, j))
    ktile = pl.BlockSpec((tm, tn), lambda b, j, i: (i, j))
    nyq = pl.BlockSpec((None, SUBLANES, tn), lambda b, j, i: (b, 0, j))
    ya, yb, yn = pl.pallas_call(
        _dft_fwd_body,
        grid=grid,
        in_specs=[full, mat, mat, ktile, ktile, pl.BlockSpec((SUBLANES, tn), lambda b, j, i: (0, j))],
        out_specs=[tile, tile, nyq],
        out_shape=[jax.ShapeDtypeStruct((bsz, seq, d), BF16)] * 2
        + [jax.ShapeDtypeStruct((bsz, SUBLANES, d), F32)],
        compiler_params=_cparams("parallel", "parallel", "arbitrary"),
        name="hyena_dft_fwd",
    )(v, cmat, smat, kr, ki, kn)
    return pl.pallas_call(
        _dft_inv_body,
        grid=grid,
        in_specs=[full, full, mat, mat, tile, tile, pl.BlockSpec((1, tn), lambda b, j, i: (0, j)), nyq],
        out_specs=tile,
        out_shape=jax.ShapeDtypeStruct((bsz, seq, d), BF16),
        compiler_params=_cparams("parallel", "parallel", "parallel"),
        name="hyena_dft_inv",
    )(ya, yb, cmat, smat, v, x0, skip.reshape(1, d), yn)


FFT_N2 = 128
FFT_MIN_SEQ = 1024
FFT_UNROLL = 8


def _fft_matrices(seq):
    n = 2 * seq
    n2 = FFT_N2
    n1 = n // n2
    r8 = SUBLANES
    q = np.arange(n2 // r8, dtype=np.int64)[:, None, None, None]
    f1 = np.arange(n1, dtype=np.int64)[None, :, None, None]
    r = np.arange(r8, dtype=np.int64)[None, None, :, None]
    t1 = np.arange(n1 // 2, dtype=np.int64)[None, None, None, :]
    ang = ((f1 * (t1 * n2 + q * r8 + r)) % n) * (2.0 * math.pi / n)
    g = np.stack([np.cos(ang), -np.sin(ang)], axis=3)
    eye = np.eye(r8)[None, None, :, None, None, :]
    ma = (g[..., None] * eye).reshape(n2 // r8, n1 * r8 * 2, (n1 // 2) * r8).astype(np.float32)
    f2 = np.arange(n2, dtype=np.int64)[:, None]
    t2 = np.arange(n2, dtype=np.int64)[None, :]
    th = ((f2 * t2) % n2) * (2.0 * math.pi / n2)
    co, si = np.cos(th), np.sin(th)
    wc = np.stack([np.stack([co, si], axis=-1), np.stack([-si, co], axis=-1)], axis=0)
    wc = wc.reshape(2 * n2, 2 * n2).astype(np.float32)
    as_bf16 = lambda m: jnp.asarray(np.ascontiguousarray(m).astype(BF16))
    return as_bf16(ma), as_bf16(np.swapaxes(ma, 1, 2)), as_bf16(wc), as_bf16(wc.T)


def _fft_stage_a(x_ref, ma_ref, s1):
    n1h, n_q, r8, tn = x_ref.shape
    n1 = s1.shape[0]

    def body(q, carry):
        x = x_ref[:, pl.ds(q, 1), :, :].reshape(n1h * r8, tn).astype(BF16)
        a = _dot(ma_ref[q], x).astype(BF16)
        s1[:, pl.ds(pl.multiple_of(q * 2 * r8, 2 * r8), 2 * r8), :] = a.reshape(n1, 2 * r8, tn)
        return carry

    lax.fori_loop(0, n_q, body, 0, unroll=FFT_UNROLL)


def _fft_conv_body(v_ref, x0_ref, k_ref, skip_ref, ma_ref, mat_ref, wc_ref, wci_ref, o_ref, s1, ysc):
    n1h, n_q, r8, tn = v_ref.shape
    n1 = s1.shape[0]
    n2 = s1.shape[1] // 2
    seq = n1h * n_q * r8
    _fft_stage_a(v_ref, ma_ref, s1)

    def slab(f, carry):
        y = _dot(wc_ref[...], s1[f])
        yr, yi = y[:n2], y[n2:]
        kr = k_ref[f, 0].astype(F32)
        ki = k_ref[f, 1].astype(F32)
        p = jnp.concatenate([yr * kr - yi * ki, yr * ki + yi * kr], axis=0).astype(BF16)
        s1[f] = _dot(wci_ref[...], p).astype(BF16)
        return carry

    lax.fori_loop(0, n1, slab, 0, unroll=2 * FFT_UNROLL)

    def inv_a(q, carry):
        z = s1[:, pl.ds(pl.multiple_of(q * 2 * r8, 2 * r8), 2 * r8), :].reshape(n1 * 2 * r8, tn)
        ysc[:, pl.ds(q, 1), :, :] = _dot(mat_ref[q], z).reshape(n1h, 1, r8, tn)
        return carry

    lax.fori_loop(0, n_q, inv_a, 0, unroll=FFT_UNROLL)
    y = ysc[...].reshape(seq, tn) + skip_ref[...] * v_ref[...].reshape(seq, tn)
    o_ref[...] = (y * x0_ref[...].astype(F32)).astype(BF16)


def _fft_filter_body(a_ref, d_ref, ma_ref, wc_ref, k_ref, s1):
    n1 = s1.shape[0]
    n2 = s1.shape[1] // 2
    scale = 1.0 / (n1 * n2)
    for src_ref, part, sign in ((a_ref, 0, scale), (d_ref, 1, -scale)):
        _fft_stage_a(src_ref, ma_ref, s1)

        def slab(f, carry):
            y = _dot(wc_ref[part * n2:(part + 1) * n2, :], s1[f])
            k_ref[f, part] = (y * sign).astype(BF16)
            return carry

        lax.fori_loop(0, n1, slab, 0, unroll=FFT_UNROLL)


def hyena_conv_fft(v, x0, skip, a, dd):
    bsz, seq, d = v.shape
    n2 = FFT_N2
    n1 = 2 * seq // n2
    n_q = n2 // SUBLANES
    tn = min(d, 256)
    ma, mat, wc, wci = _fft_matrices(seq)
    const = lambda shape: pl.BlockSpec(shape, lambda *_: (0,) * len(shape), pipeline_mode=pl.Buffered(1))
    view = lambda t: t.reshape(t.shape[:-2] + (n1 // 2, n_q, SUBLANES, d))
    tap = pl.BlockSpec((n1 // 2, n_q, SUBLANES, tn), lambda j: (0, 0, 0, j))
    khat = pl.pallas_call(
        _fft_filter_body,
        grid=(d // tn,),
        in_specs=[tap, tap, const(ma.shape), const(wc.shape)],
        out_specs=pl.BlockSpec((n1, 2, n2, tn), lambda j: (0, 0, 0, j)),
        out_shape=jax.ShapeDtypeStruct((n1, 2, n2, d), BF16),
        scratch_shapes=[pltpu.VMEM((n1, 2 * n2, tn), BF16)],
        compiler_params=_cparams("parallel"),
        name="hyena_filter_fft",
    )(view(a), view(dd), ma, wc)
    return pl.pallas_call(
        _fft_conv_body,
        grid=(d // tn, bsz),
        in_specs=[pl.BlockSpec((None, n1 // 2, n_q, SUBLANES, tn), lambda j, b: (b, 0, 0, 0, j)),
                  pl.BlockSpec((None, seq, tn), lambda j, b: (b, 0, j)),
                  pl.BlockSpec((n1, 2, n2, tn), lambda j, b: (0, 0, 0, j), pipeline_mode=pl.Buffered(1)),
                  pl.BlockSpec((1, tn), lambda j, b: (0, j)),
                  const(ma.shape), const(mat.shape), const(wc.shape), const(wci.shape)],
        out_specs=pl.BlockSpec((None, seq, tn), lambda j, b: (b, 0, j)),
        out_shape=jax.ShapeDtypeStruct((bsz, seq, d), BF16),
        scratch_shapes=[pltpu.VMEM((n1, 2 * n2, tn), BF16), pltpu.VMEM((n1 // 2, n_q, SUBLANES, tn), F32)],
        compiler_params=_cparams("parallel", "arbitrary"),
        name="hyena_conv_fft",
    )(view(v), x0, khat, skip.reshape(1, d), ma, mat, wc, wci)


def _mm_res_body(x_ref, w_ref, b_ref, res_ref, gate_ref, o_ref):
    o_ref[...] = res_ref[...] + gate_ref[...] * (_dot(x_ref[...], w_ref[...]) + b_ref[...])


def mm_residual(x, w, b, res, gate):
    bsz, seq, k = x.shape
    n = w.shape[1]
    tm = min(seq, 512)
    return pl.pallas_call(
        _mm_res_body,
        grid=(bsz, seq // tm),
        in_specs=[pl.BlockSpec((None, tm, k), lambda b, i: (b, i, 0)),
                  pl.BlockSpec((k, n), lambda b, i: (0, 0)),
                  pl.BlockSpec((1, n), lambda b, i: (0, 0)),
                  pl.BlockSpec((None, tm, n), lambda b, i: (b, i, 0)),
                  pl.BlockSpec((None, 1, n), lambda b, i: (b, 0, 0))],
        out_specs=pl.BlockSpec((None, tm, n), lambda b, i: (b, i, 0)),
        out_shape=jax.ShapeDtypeStruct((bsz, seq, n), F32),
        compiler_params=_cparams("parallel", "parallel"),
        name="mm_residual",
    )(x, w, b.reshape(1, n), res, gate)


def _moe_pre_body(*refs, n_groups, n_experts, tile_offs):
    n_streams = len(tile_offs) - 1
    g_ref, wr_ref, br_ref, tok_ref, eid_ref, gate_ref = refs[3 * n_streams:]
    i = pl.program_id(0)
    for k in range(n_streams):
        x_ref, sh_ref, sc_ref = refs[3 * k:3 * k + 3]

        @pl.when(jnp.logical_and(i >= tile_offs[k], i < tile_offs[k + 1]))
        def _():
            tok = _norm_mod(x_ref[...], g_ref[...], sh_ref[...], sc_ref[...])
            _route_tokens(tok, wr_ref, br_ref, tok_ref, eid_ref, gate_ref, n_groups, n_experts)


def _route_tokens(tok, wr_ref, br_ref, tok_ref, eid_ref, gate_ref, n_groups, n_experts):
    tok_ref[...] = tok
    t_hi = tok.astype(BF16)
    t_lo = (tok - t_hi.astype(F32)).astype(BF16)
    logits = (_dot(t_hi, wr_ref[0]) + _dot(t_hi, wr_ref[1]) + _dot(t_lo, wr_ref[0])) + br_ref[...]
    lane = lax.broadcasted_iota(jnp.int32, logits.shape, 1)
    per = n_experts // n_groups
    big = jnp.int32(1 << 20)
    gmask = jnp.logical_and(lane >= n_experts, lane < n_experts + n_groups)
    gl = jnp.where(gmask, logits, NEG_BIG)
    gmax = jnp.max(gl, axis=-1, keepdims=True)
    gidx = jnp.min(jnp.where(gl == gmax, lane - n_experts, big), axis=-1, keepdims=True)
    p_top = 1.0 / jnp.sum(jnp.where(gmask, jnp.exp(gl - gmax), 0.0), axis=-1, keepdims=True)
    lo = gidx * per
    emask = jnp.logical_and(lane >= lo, lane < lo + per)
    el = jnp.where(emask, logits, NEG_BIG)
    m1 = jnp.max(el, axis=-1, keepdims=True)
    i1 = jnp.min(jnp.where(el == m1, lane, big), axis=-1, keepdims=True)
    el2 = jnp.where(lane == i1, NEG_BIG, el)
    m2 = jnp.max(el2, axis=-1, keepdims=True)
    i2 = jnp.min(jnp.where(el2 == m2, lane, big), axis=-1, keepdims=True)
    e21 = jnp.exp(m2 - m1)
    g1 = p_top / (1.0 + e21)
    g2 = g1 * e21
    ids = jnp.where(lane == 0, i1, jnp.where(lane == 1, i2, -1))
    ids_t = ids.T
    for h in range(eid_ref.shape[0]):
        eid_ref[h] = ids_t[0:SUBLANES, h * MOE_TM:(h + 1) * MOE_TM]
    gate_ref[...] = jnp.where(lane == 0, g1, jnp.where(lane == 1, g2, 0.0))


def _split_bf16(w):
    hi = w.astype(BF16)
    return jnp.stack([hi, (w - hi.astype(F32)).astype(BF16)])


def _router_weights(wg, bg, we, be):
    pad = LANES - we.shape[1] - wg.shape[1]
    wr = jnp.pad(jnp.concatenate([we, wg], axis=1), ((0, 0), (0, pad)))
    br = jnp.pad(jnp.concatenate([be, bg]), (0, pad)).reshape(1, LANES)
    return wr, br


def moe_pre(streams, g, wr, br, n_groups, n_experts):
    d = streams[0][0].shape[2]
    tm = MOE_TM
    tile_offs = [0]
    in_specs, args = [], []
    for x, shift, scale in streams:
        bsz, seq, _ = x.shape
        nt = seq // tm
        n_tiles = bsz * nt
        off = tile_offs[-1]
        tile_offs.append(off + n_tiles)

        def tile(i, off=off, n_tiles=n_tiles):
            return jnp.clip(i - off, 0, n_tiles - 1)

        in_specs += [pl.BlockSpec((None, tm, d), lambda i, tile=tile, nt=nt: (tile(i) // nt, tile(i) % nt, 0)),
                     pl.BlockSpec((None, 1, d), lambda i, tile=tile, nt=nt: (tile(i) // nt, 0, 0)),
                     pl.BlockSpec((None, 1, d), lambda i, tile=tile, nt=nt: (tile(i) // nt, 0, 0))]
        args += [x, shift, scale]
    in_specs += [pl.BlockSpec((1, d), lambda i: (0, 0)),
                 pl.BlockSpec((2, d, LANES), lambda i: (0, 0, 0)),
                 pl.BlockSpec((1, LANES), lambda i: (0, 0))]
    args += [g.reshape(1, d), _split_bf16(wr), br]
    total = tile_offs[-1] * tm
    rout = pl.BlockSpec((tm, LANES), lambda i: (i, 0))
    tok, eid, gate = pl.pallas_call(
        functools.partial(_moe_pre_body, n_groups=n_groups, n_experts=n_experts, tile_offs=tuple(tile_offs)),
        grid=(tile_offs[-1],),
        in_specs=in_specs,
        out_specs=[pl.BlockSpec((tm, d), lambda i: (i, 0)),
                   pl.BlockSpec((1, SUBLANES, tm), lambda i: (i, 0, 0)), rout],
        out_shape=[jax.ShapeDtypeStruct((total, d), F32),
                   jax.ShapeDtypeStruct((tile_offs[-1], SUBLANES, tm), jnp.int32),
                   jax.ShapeDtypeStruct((total, LANES), F32)],
        compiler_params=_cparams("parallel"),
        name="moe_pre",
    )(*args)
    return tok, eid, gate, tile_offs[:-1]


def _start_row_gather(row_index, n_rows, src_hbm, dst_vmem, sem):
    def body(g, c):
        r0 = pl.multiple_of(g * SUBLANES, SUBLANES)
        dst_tile = dst_vmem.at[pl.ds(r0, SUBLANES)]
        for k in range(SUBLANES):
            pltpu.make_async_copy(src_hbm.at[pl.ds(row_index(r0 + k), 1)], dst_tile.at[pl.ds(k, 1)], sem).start()
        return c

    lax.fori_loop(0, n_rows // SUBLANES, body, 0, unroll=2)


def _wait_row_gather(n_rows, src_hbm, dst_vmem, sem):
    pltpu.make_async_copy(src_hbm.at[pl.ds(0, n_rows)], dst_vmem, sem).wait()


def _expert_body(bv_ref, rk_ref, pe_ref, tot_ref, p0_ref, ts_hbm, tok_ref, wg_hbm, wu_hbm, wd_hbm, o_ref,
                 xbuf, xsem, win, isem, wcache, stg, wsem, cnt, *, layer):
    i = pl.program_id(0)
    n = pl.num_programs(0)
    slot = i % 2
    cr, cc = stg.shape[1:]
    total = tot_ref[0]
    mats_hbm = (wg_hbm, wu_hbm, wd_hbm)

    @pl.when(i == 0)
    def _():
        cnt[0] = 0
        cnt[1] = 0

    def live(b):
        return jnp.logical_and(b < n, bv_ref[jnp.minimum(b, n - 1)] > 0)

    def window(b):
        s = b % 2
        lo = pl.multiple_of((p0_ref[jnp.minimum(b, n - 1)] // MOE_WIN_ALIGN) * MOE_WIN_ALIGN, MOE_WIN_ALIGN)
        return pltpu.make_async_copy(ts_hbm.at[pl.ds(lo, MOE_WIN)],
                                     win.at[pl.ds(pl.multiple_of(s * MOE_WIN, MOE_WIN), MOE_WIN)], isem.at[s])

    def gather(b):
        s = b % 2
        base = s * MOE_WIN + p0_ref[jnp.minimum(b, n - 1)] % MOE_WIN_ALIGN
        _start_row_gather(lambda r: win[base + r], MOE_BM, tok_ref, xbuf.at[s], xsem.at[s])

    @pl.when(jnp.logical_and(i == 0, live(0)))
    def _():
        window(0).start()
        window(0).wait()
        gather(0)

    @pl.when(jnp.logical_and(i == 0, live(1)))
    def _():
        window(1).start()

    @pl.when(live(i + 1))
    def _():
        window(i + 1).wait()
        gather(i + 1)

    @pl.when(live(i + 2))
    def _():
        window(i + 2).start()

    def chunk_geom(c):
        q = c % MOE_NCH
        m = q // 4
        sub = q % 4
        r0 = jnp.where(m < 2, sub, sub // 2) * cr
        c0 = jnp.where(m < 2, 0, sub % 2) * cc
        return m, pl.multiple_of(r0, cr), pl.multiple_of(c0, cc)

    def issue(c):
        e = pe_ref[c // MOE_NCH]
        m, r0, c0 = chunk_geom(c)
        s = c % MOE_STAGE
        for k, w_hbm in enumerate(mats_hbm):
            @pl.when(m == k)
            def _():
                pltpu.make_async_copy(w_hbm.at[layer, e, pl.ds(r0, cr), pl.ds(c0, cc)], stg.at[s],
                                      wsem.at[s]).start()

    def cast(c):
        s = c % MOE_STAGE
        pltpu.make_async_copy(wg_hbm.at[layer, 0, pl.ds(0, cr), pl.ds(0, cc)], stg.at[s], wsem.at[s]).wait()
        ws = (c // MOE_NCH) % 2
        q = c % MOE_NCH
        step = min(MOE_CAST_ROWS, cr)
        assert cr % step == 0

        def slab(k, carry):
            rows = pl.ds(pl.multiple_of(k * step, step), step)
            wcache[ws, q, rows, :] = stg[s, rows, :].astype(BF16)
            return carry

        lax.fori_loop(0, cr // step, slab, 0)

    valid = bv_ref[i] > 0
    rank = rk_ref[i]
    issued = cnt[0]
    done = cnt[1]
    limit = jnp.minimum(total, MOE_NCH * (rank + 2))
    need = jnp.where(valid, MOE_NCH * (rank + 1), done)

    def fill(issued, done):
        hi = jnp.minimum(limit, done + MOE_STAGE)

        def body(c, carry):
            issue(c)
            return carry

        lax.fori_loop(issued, hi, body, 0)
        return jnp.maximum(issued, hi)

    def cast_and_refill(c, issued):
        cast(c)
        more = issued < jnp.minimum(limit, c + 1 + MOE_STAGE)

        @pl.when(more)
        def _():
            issue(issued)

        return issued + more.astype(jnp.int32)

    issued = fill(issued, done)
    issued = lax.fori_loop(done, need, cast_and_refill, issued)
    done = jnp.maximum(done, need)

    @pl.when(valid)
    def _():
        ws = rank % 2
        _wait_row_gather(MOE_BM, tok_ref, xbuf.at[slot], xsem.at[slot])
        x = xbuf[slot].astype(BF16)
        gate = sum(_dot(x[:, k * cr:(k + 1) * cr], wcache[ws, k]) for k in range(4))
        up = sum(_dot(x[:, k * cr:(k + 1) * cr], wcache[ws, 4 + k]) for k in range(4))
        h = (gate * jax.nn.sigmoid(gate) * up).astype(BF16)
        for half in range(2):
            o_ref[:, half * cc:(half + 1) * cc] = sum(
                _dot(h[:, k * cr:(k + 1) * cr], wcache[ws, 8 + 2 * k + half]) for k in range(2))

    @pl.when(jnp.logical_not(valid))
    def _():
        o_ref[...] = jnp.zeros_like(o_ref)

    fetched = issued
    issued = lax.fori_loop(done, fetched, cast_and_refill, issued)
    done = jnp.maximum(done, fetched)
    last = i == n - 1
    tail = jnp.where(last, issued, done)

    def drain(c, carry):
        cast(c)
        return carry

    lax.fori_loop(done, tail, drain, 0)
    cnt[0] = issued
    cnt[1] = jnp.maximum(done, tail)


def moe_experts(tok, tok_sorted, block_p0, block_valid, block_rank, present, n_chunks, w_gate, w_up, w_down,
                layer):
    d = tok.shape[1]
    n_blocks = block_valid.shape[0]
    n_rows = n_blocks * MOE_BM
    dh = w_gate.shape[3]
    assert 2 * dh == d and MOE_NCH == 12
    cr, cc = d // 4, dh
    any_spec = pl.BlockSpec(memory_space=pl.ANY)
    grid_spec = pltpu.PrefetchScalarGridSpec(
        num_scalar_prefetch=5,
        grid=(n_blocks,),
        in_specs=[any_spec, any_spec, any_spec, any_spec, any_spec],
        out_specs=pl.BlockSpec((MOE_BM, d), lambda i, *_: (i, 0)),
        scratch_shapes=[pltpu.VMEM((2, MOE_BM, d), F32), pltpu.SemaphoreType.DMA((2,)),
                        pltpu.SMEM((2 * MOE_WIN,), jnp.int32), pltpu.SemaphoreType.DMA((2,)),
                        pltpu.VMEM((2, MOE_NCH, cr, cc), BF16),
                        pltpu.VMEM((MOE_STAGE, cr, cc), F32), pltpu.SemaphoreType.DMA((MOE_STAGE,)),
                        pltpu.SMEM((2,), jnp.int32)],
    )
    return pl.pallas_call(
        functools.partial(_expert_body, layer=layer),
        grid_spec=grid_spec,
        out_shape=jax.ShapeDtypeStruct((n_rows, d), F32),
        compiler_params=_cparams("arbitrary"),
        name="moe_experts",
    )(block_valid, block_rank, present, n_chunks, block_p0, tok_sorted, tok, w_gate, w_up, w_down)


def _combine_body(dest_ref, nxt_ref, os_ref, gate_ref, res_ref, gt_ref, fg_ref, *rest, final_norm, with_next):
    if with_next:
        ng_ref, nsh_ref, nsc_ref, o_ref, h_ref, buf, sem = rest
    else:
        o_ref, buf, sem = rest
    rows = res_ref.shape[0]
    i = pl.program_id(0)
    n = pl.num_programs(0)
    slot = i % 2

    def start(idx_ref, s):
        for k in range(TOP_K):
            _start_row_gather(lambda r, k=k: idx_ref[k * rows + r], rows, os_ref, buf.at[s, k], sem.at[s])

    @pl.when(i == 0)
    def _():
        start(dest_ref, 0)

    @pl.when(i + 1 < n)
    def _():
        start(nxt_ref, 1 - slot)

    for k in range(TOP_K):
        _wait_row_gather(rows, os_ref, buf.at[slot, k], sem.at[slot])
    gates = gate_ref[...]
    mo = gates[:, 0:1] * buf[slot, 0] + gates[:, 1:2] * buf[slot, 1]
    y = res_ref[...] + gt_ref[...] * mo
    if final_norm:
        ms = jnp.mean(y * y, axis=-1, keepdims=True)
        y = y * lax.rsqrt(ms + NORM_EPS) * fg_ref[...]
    o_ref[...] = y
    if with_next:
        h_ref[...] = _norm_mod(y, ng_ref[...], nsh_ref[...], nsc_ref[...]).astype(h_ref.dtype)


def moe_combine(os, dest, gates, tile0, res, gt, final_g, final_norm, next_mod=None):
    bsz, seq, d = res.shape
    rows = MOE_TM
    nt = seq // rows
    n = bsz * nt
    tile = pl.BlockSpec((None, rows, d), lambda i: (i // nt, i % nt, 0))
    mspec = pl.BlockSpec((None, 1, d), lambda i: (i // nt, 0, 0))
    rspec = pl.BlockSpec((1, d), lambda i: (0, 0))
    dspec = lambda step: pl.BlockSpec((SUBLANES * rows,), lambda i: (tile0 + step(i),), memory_space=pltpu.SMEM)
    in_specs = [dspec(lambda i: i), dspec(lambda i: jnp.minimum(i + 1, n - 1)),
                pl.BlockSpec(memory_space=pl.ANY),
                pl.BlockSpec((rows, LANES), lambda i: (tile0 + i, 0)),
                tile, mspec, rspec]
    args = [dest, dest, os, gates, res, gt, final_g.reshape(1, d)]
    out_specs, out_shape = [tile], [jax.ShapeDtypeStruct((bsz, seq, d), F32)]
    if next_mod is not None:
        in_specs += [rspec, mspec, mspec]
        args += [next_mod[0].reshape(1, d), next_mod[1], next_mod[2]]
        out_specs.append(tile)
        out_shape.append(jax.ShapeDtypeStruct((bsz, seq, d), BF16))
    outs = pl.pallas_call(
        functools.partial(_combine_body, final_norm=final_norm, with_next=next_mod is not None),
        grid=(n,),
        in_specs=in_specs,
        out_specs=out_specs,
        out_shape=out_shape,
        scratch_shapes=[pltpu.VMEM((2, TOP_K, rows, d), F32), pltpu.SemaphoreType.DMA((2,))],
        compiler_params=_cparams("arbitrary"),
        name="moe_combine",
    )(*args)
    return outs if next_mod is not None else outs[0]


def _plan_body(e_ref, dest_ref, tab_ref, blk_ref, present_ref, *, n_experts):
    n_rows, w = e_ref.shape
    e_all = e_ref[...]
    li = lax.broadcasted_iota(jnp.int32, (w, w), 0)
    lj = lax.broadcasted_iota(jnp.int32, (w, w), 1)
    incl = (li <= lj).astype(BF16)
    ri = lax.broadcasted_iota(jnp.int32, (n_rows, n_rows), 0)
    rj = lax.broadcasted_iota(jnp.int32, (n_rows, n_rows), 1)
    before = (rj < ri).astype(BF16)
    elane = lax.broadcasted_iota(jnp.int32, (n_rows, LANES), 1)
    row_tot = jnp.zeros((n_rows, LANES), F32)
    for e in range(n_experts):
        tot = jnp.sum((e_all == e).astype(F32), axis=1, keepdims=True)
        row_tot = row_tot + jnp.where(elane == e, tot, 0.0)
    rows_before = _dot(before, row_tot.astype(BF16))
    counts = jnp.sum(row_tot, axis=0, keepdims=True).astype(jnp.int32)
    lane1 = lax.broadcasted_iota(jnp.int32, (1, LANES), 1)

    def excl_prefix(v):
        acc = v
        sh = 1
        while sh < LANES:
            acc = acc + jnp.where(lane1 >= sh, pltpu.roll(acc, sh, axis=1), 0)
            sh *= 2
        return acc - v

    start = excl_prefix(counts)
    padded = (counts + (MOE_BM - 1)) // MOE_BM * MOE_BM
    pad_start = excl_prefix(padded)
    pad_end = pad_start + padded
    has = (counts > 0).astype(jnp.int32)
    rank = excl_prefix(has)
    n_chunks = MOE_NCH * jnp.sum(has.astype(F32), axis=1, keepdims=True).astype(jnp.int32)
    tab_ref[...] = jnp.concatenate([counts, start, pad_start, pad_end, jnp.broadcast_to(n_chunks, (1, LANES)),
                                    jnp.zeros((SUBLANES - 5, LANES), jnp.int32)], axis=0)
    nb = blk_ref.shape[0]
    f32 = lambda v: v.astype(F32)
    lane_b = lax.broadcasted_iota(jnp.int32, (nb, LANES), 1)
    first_row = lax.broadcasted_iota(jnp.int32, (nb, 1), 0) * MOE_BM
    is_expert = lane_b < n_experts
    expert = jnp.sum(f32(jnp.logical_and(is_expert, pad_end <= first_row)), axis=1, keepdims=True)
    expert = jnp.minimum(expert.astype(jnp.int32), n_experts - 1)
    mine = lane_b == expert
    pick = lambda v: jnp.sum(jnp.where(mine, f32(v), 0.0), axis=1, keepdims=True).astype(jnp.int32)
    live = first_row < jnp.max(f32(pad_end), axis=1, keepdims=True).astype(jnp.int32)
    p0 = jnp.where(live, pick(start) - pick(pad_start) + first_row, 0)
    blk_ref[...] = jnp.where(lane_b == 0, live.astype(jnp.int32),
                             jnp.where(lane_b == 1, pick(rank), jnp.where(lane_b == 2, p0, 0)))
    row_r = lax.broadcasted_iota(jnp.int32, (LANES, LANES), 0)
    lane_e = lax.broadcasted_iota(jnp.int32, (LANES, LANES), 1)
    hit_r = jnp.logical_and(has > 0, rank == row_r)
    present = jnp.sum(jnp.where(hit_r, f32(lane_e), 0.0), axis=1, keepdims=True).astype(jnp.int32)
    present_ref[...] = jnp.broadcast_to(present, (LANES, LANES))
    base = rows_before + pad_start.astype(F32)
    dest = jnp.zeros((n_rows, w), F32)
    for e in range(n_experts):
        hit = e_all == e
        within = _dot(hit.astype(BF16), incl)
        dest = dest + jnp.where(hit, within - 1.0 + base[:, e:e + 1], 0.0)
    dest_ref[...] = dest.astype(jnp.int32)


def _route_plan(eid, n_experts):
    n_tiles, r8, tm = eid.shape
    a = n_tiles * TOP_K * tm
    n_blocks = -(-a // MOE_BM) + n_experts
    assert (tm & (tm - 1)) == 0 and TOP_K == 2
    nb_pad = -(-n_blocks // SUBLANES) * SUBLANES
    dest, tab, blk, present = pl.pallas_call(
        functools.partial(_plan_body, n_experts=n_experts),
        out_shape=[jax.ShapeDtypeStruct((n_tiles * r8, tm), jnp.int32),
                   jax.ShapeDtypeStruct((SUBLANES, LANES), jnp.int32),
                   jax.ShapeDtypeStruct((nb_pad, LANES), jnp.int32),
                   jax.ShapeDtypeStruct((LANES, LANES), jnp.int32)],
        compiler_params=pltpu.CompilerParams(vmem_limit_bytes=V7X_VMEM_LIMIT_BYTES),
        name="moe_route_plan",
    )(eid.reshape(n_tiles * r8, tm))
    e_flat = eid[:, :TOP_K, :].reshape(-1)
    order = jnp.argsort(e_flat).astype(jnp.int32)
    shift = tm.bit_length() - 1
    tok_of = ((order >> (shift + 1)) << shift) | (order & (tm - 1))
    slack = -a % MOE_WIN_ALIGN + MOE_WIN
    tok_sorted = jnp.concatenate([tok_of, jnp.zeros((slack,), jnp.int32)])
    blk = blk[:n_blocks]
    return dest.reshape(-1), tok_sorted, blk[:, 2], blk[:, 0], blk[:, 1], present[:n_experts, 0], tab[4, :1]


def hier_moe(streams, norm_g, wg, bg, we, be, w_gate, w_up, w_down, layer, final_g, final_norm, next_mods=None,
             routed=None):
    n_groups = wg.shape[1]
    n_experts = we.shape[1]
    if routed is None:
        wr, br = _router_weights(wg, bg, we, be)
        tok, eid, gates, tile0s = moe_pre([s[:3] for s in streams], norm_g, wr, br, n_groups, n_experts)
    else:
        (tok, eid, gates), tile0s = routed, [0]
    dest, tok_sorted, block_p0, block_valid, block_rank, present, n_chunks = _route_plan(eid, n_experts)
    os = moe_experts(tok, tok_sorted, block_p0, block_valid, block_rank, present, n_chunks, w_gate, w_up, w_down,
                     layer)
    next_mods = next_mods or [None] * len(streams)
    return [moe_combine(os, dest, gates, tile0, x, gt, final_g, final_norm, nm)
            for (x, _, _, gt), tile0, nm in zip(streams, tile0s, next_mods)]


def _s5_arrange(h):
    bsz, t, d = h.shape
    c = t // (S5_SEGS * S5_TAU)
    h = h.reshape(bsz, S5_SEGS, c, S5_TAU, d // LANES, LANES)
    return h.transpose(2, 0, 1, 4, 3, 5).reshape(c * bsz * S5_SEGS, d * S5_TAU)


def _s5_unarrange(y, bsz):
    r, w = y.shape
    d = w // S5_TAU
    c = r // (bsz * S5_SEGS)
    y = y.reshape(c, bsz, S5_SEGS, d // LANES, S5_TAU, LANES)
    return y.transpose(1, 2, 0, 4, 3, 5).reshape(bsz, S5_SEGS * c * S5_TAU, d)


def _s5_operators(a_re, a_im, log_step, b_re, b_im, c_re, c_im):
    n_g, n_p = a_re.shape[1:]
    n_h = b_re.shape[-1]
    gpt = LANES // n_h
    n_j = n_g // gpt
    tau = S5_TAU
    assert tau * n_h == LANES and 2 * n_p == LANES
    lam_step = lax.complex(a_re, a_im) * jnp.exp(log_step)[..., None]
    lam_bar = jnp.exp(lam_step)
    b_bar = ((lam_bar - 1.0) / lax.complex(a_re, a_im))[..., None] * lax.complex(b_re, b_im)
    c_mat = lax.complex(c_re, c_im)
    ks = jnp.arange(tau + 1, dtype=F32)[None, :, None, None]
    pw = jnp.exp(lam_step[:, None] * ks)
    ein = functools.partial(jnp.einsum, precision=HIGHEST)
    inj_c, cl_c, lt = [], [], []
    tz_c = 0.0
    for d in range(2):
        pos = jnp.arange(tau) if d == 0 else jnp.arange(tau)[::-1]
        inj = (pw[d][tau - 1 - pos][..., None] * b_bar[d][None]).reshape(tau, n_j, gpt, n_p, n_h)
        inj = inj.transpose(1, 0, 2, 4, 3).reshape(n_j, tau * LANES, n_p)
        inj_c.append(jnp.concatenate([inj.real, inj.imag], axis=-1))
        cl = (c_mat[d][None] * pw[d][pos + 1][:, :, None, :]).reshape(tau, n_j, gpt, n_h, n_p)
        cl = cl.transpose(1, 2, 4, 0, 3).reshape(n_j, gpt * n_p, tau * n_h)
        cl_c.append(jnp.concatenate([cl.real, -cl.imag], axis=1))
        mk = ein('gop,kgp,gph->kgoh', c_mat[d], pw[d][:tau], b_bar[d]).real
        diff = pos[:, None] - pos[None, :]
        tz = jnp.where((diff >= 0)[:, :, None, None, None], mk[jnp.clip(diff, 0, tau - 1)], 0.0)
        tz = tz.reshape(tau, tau, n_j, gpt, n_h, n_h)
        tz_c = tz_c + tz.transpose(2, 1, 3, 5, 0, 4).reshape(n_j, tau * LANES, tau * n_h)
        lt_d = pw[d][tau].reshape(n_j, 1, gpt * n_p)
        lt.append(jnp.concatenate([lt_d.real, lt_d.imag], axis=-1))
    ws, wu, wh = s5_expand(jnp.stack(inj_c).astype(BF16), jnp.stack(cl_c).astype(BF16), tz_c.astype(BF16),
                           n_h, n_p)
    return ws, wu, wh, jnp.stack(lt).astype(F32)


def _s5_expand_body(inj_ref, cl_ref, tz_ref, ws_ref, wu_ref, wh_ref, *, n_h, n_p):
    rows = tz_ref.shape[0]
    gpt = LANES // n_h
    row = lax.broadcasted_iota(jnp.int32, (rows, LANES), 0)
    lane = lax.broadcasted_iota(jnp.int32, (rows, LANES), 1)
    sel_r = lax.broadcasted_iota(jnp.int32, (LANES, LANES), 0)
    sel_l = lax.broadcasted_iota(jnp.int32, (LANES, LANES), 1)
    grp_in = (row // n_h) % gpt
    grp_st = (row // n_p) % gpt

    def spread_out(m, t, grp_row):
        sel = jnp.logical_and(sel_r // n_h == t, sel_r % n_h == sel_l % n_h).astype(BF16)
        return jnp.where(grp_row == lane // n_h, _dot(m, sel), 0.0).astype(BF16)

    def spread_state(m, c, q, grp_row):
        sel = jnp.logical_and(sel_r // n_p == c, sel_r % n_p == sel_l % n_p).astype(BF16)
        return jnp.where(grp_row == (LANES // n_p) * q + lane // n_p, _dot(m, sel), 0.0).astype(BF16)

    w2 = 2 * gpt * n_p
    tz = tz_ref[...]
    for t in range(S5_TAU):
        wu_ref[:, t * LANES:(t + 1) * LANES] = spread_out(tz, t, grp_in)
    for d in range(2):
        cl = cl_ref[d]
        inj = inj_ref[d]
        for t in range(S5_TAU):
            wh_ref[d, :, t * LANES:(t + 1) * LANES] = spread_out(cl, t, grp_st)
        for c in range(2):
            for q in range(gpt * n_p // LANES):
                lo = d * w2 + c * gpt * n_p + q * LANES
                ws_ref[:, lo:lo + LANES] = spread_state(inj, c, q, grp_in)


def s5_expand(inj_c, cl_c, tz_c, n_h, n_p):
    n_j, rows, _ = tz_c.shape
    gpt = LANES // n_h
    w2 = 2 * gpt * n_p
    assert rows == S5_TAU * LANES == w2
    cspec = pl.BlockSpec((2, None, rows, LANES), lambda j: (0, j, 0, 0))
    return pl.pallas_call(
        functools.partial(_s5_expand_body, n_h=n_h, n_p=n_p),
        grid=(n_j,),
        in_specs=[cspec, cspec, pl.BlockSpec((None, rows, LANES), lambda j: (j, 0, 0))],
        out_specs=[pl.BlockSpec((None, rows, 2 * w2), lambda j: (j, 0, 0)),
                   pl.BlockSpec((None, rows, rows), lambda j: (j, 0, 0)),
                   pl.BlockSpec((2, None, w2, rows), lambda j: (0, j, 0, 0))],
        out_shape=[jax.ShapeDtypeStruct((n_j, rows, 2 * w2), BF16),
                   jax.ShapeDtypeStruct((n_j, rows, rows), BF16),
                   jax.ShapeDtypeStruct((2, n_j, w2, rows), BF16)],
        compiler_params=_cparams("parallel"),
        name="s5_expand",
    )(inj_c, cl_c, tz_c)


def _s5_inj_body(xc_ref, xl_ref, w_ref, oc_ref, ol_ref):
    w = w_ref[...]
    ol_ref[...] = _dot(xl_ref[...], w)

    @pl.when(pl.program_id(1) == 0)
    def _():
        oc_ref[...] = _dot(xc_ref[...], w)


def s5_inject(xr_c, xr_l, ws):
    r_c, r_l = xr_c.shape[0], xr_l.shape[0]
    n_j, k, n = ws.shape
    tm = r_l // 2 if r_l % 32 == 0 else r_l
    return pl.pallas_call(
        _s5_inj_body,
        grid=(n_j, r_l // tm),
        in_specs=[pl.BlockSpec((r_c, k), lambda j, i: (0, j)),
                  pl.BlockSpec((tm, k), lambda j, i: (i, j)),
                  pl.BlockSpec((None, k, n), lambda j, i: (j, 0, 0))],
        out_specs=[pl.BlockSpec((r_c, n), lambda j, i: (0, j)),
                   pl.BlockSpec((tm, n), lambda j, i: (i, j))],
        out_shape=[jax.ShapeDtypeStruct((r_c, n_j * n), F32), jax.ShapeDtypeStruct((r_l, n_j * n), F32)],
        compiler_params=_cparams("parallel", "arbitrary"),
        name="s5_inject",
    )(xr_c, xr_l, ws)


def _cmul(ar, ai, br, bi):
    return ar * br - ai * bi, ar * bi + ai * br


def _s5_scan_body(sc_ref, sl_ref, lt_ref, h_ref, raw_ref, *, n_ctx, n_lat, bsz):
    d = pl.program_id(1)
    w2 = lt_ref.shape[-1]
    w = w2 // 2
    rows = bsz * S5_SEGS
    seg = lax.broadcasted_iota(jnp.int32, (rows, 1), 0) % S5_SEGS
    is_late = seg != d
    lam_r = lt_ref[:, 0:w]
    lam_i = lt_ref[:, w:w2]
    zero = jnp.zeros((rows, w), F32)
    one = (jnp.ones((1, w), F32), jnp.zeros((1, w), F32))

    def swap_segments(x):
        return jnp.where(seg == 0, pltpu.roll(x, rows - 1, axis=0), pltpu.roll(x, 1, axis=0))

    def phase(s_ref, n_steps, hin_r, hin_i, write):
        def chunk(k):
            return jnp.where(d == 0, k, n_steps - 1 - k)

        def step_raw(k, carry):
            hr, hi = carry
            c = chunk(k)
            raw_ref[c, :, 0:w] = hr
            raw_ref[c, :, w:w2] = hi
            nr, ni = _cmul(lam_r, lam_i, hr, hi)
            return nr + s_ref[c, :, 0:w], ni + s_ref[c, :, w:w2]

        er, ei = lax.fori_loop(0, n_steps, step_raw, (zero, zero), unroll=S5_SCAN_UNROLL)
        pr, pi = lax.fori_loop(0, n_steps, lambda k, q: _cmul(lam_r, lam_i, *q), one)
        dr, di = _cmul(pr, pi, hin_r, hin_i)
        first_r = jnp.where(is_late, 0.0, er + dr)
        first_i = jnp.where(is_late, 0.0, ei + di)
        carry_r = jnp.where(is_late, swap_segments(first_r), hin_r)
        carry_i = jnp.where(is_late, swap_segments(first_i), hin_i)
        if write:
            def step_fix(k, q):
                c = chunk(k)
                fr, fi = _cmul(q[0], q[1], carry_r, carry_i)
                h_ref[c, :, 0:w] = (raw_ref[c, :, 0:w] + fr).astype(h_ref.dtype)
                h_ref[c, :, w:w2] = (raw_ref[c, :, w:w2] + fi).astype(h_ref.dtype)
                return _cmul(lam_r, lam_i, q[0], q[1])

            lax.fori_loop(0, n_steps, step_fix, one, unroll=S5_SCAN_UNROLL)
        lr, li = _cmul(pr, pi, carry_r, carry_i)
        last_r = jnp.where(is_late, er + lr, 0.0)
        last_i = jnp.where(is_late, ei + li, 0.0)
        return (jnp.where(is_late, 0.0, swap_segments(last_r)), jnp.where(is_late, 0.0, swap_segments(last_i)))

    hr, hi = phase(sc_ref, n_ctx, zero, zero, False)
    phase(sl_ref, n_lat, hr, hi, True)


def s5_scan(s_ctx, s_lat, lt, bsz):
    assert S5_SEGS == 2
    n_ctx, rows, _ = s_ctx.shape
    n_lat = s_lat.shape[0]
    n_j = lt.shape[1]
    w2 = lt.shape[-1]
    return pl.pallas_call(
        functools.partial(_s5_scan_body, n_ctx=n_ctx, n_lat=n_lat, bsz=bsz),
        grid=(n_j, 2),
        in_specs=[pl.BlockSpec((n_ctx, rows, w2), lambda j, d: (0, 0, 2 * j + d)),
                  pl.BlockSpec((n_lat, rows, w2), lambda j, d: (0, 0, 2 * j + d)),
                  pl.BlockSpec((None, None, 1, w2), lambda j, d: (d, j, 0, 0))],
        out_specs=pl.BlockSpec((None, n_lat, rows, w2), lambda j, d: (d, 0, 0, j)),
        out_shape=jax.ShapeDtypeStruct((2, n_lat, rows, n_j * w2), BF16),
        scratch_shapes=[pltpu.VMEM((max(n_ctx, n_lat), rows, w2), F32)],
        compiler_params=_cparams("parallel", "parallel"),
        name="s5_scan",
    )(s_ctx, s_lat, lt)


def _s5_out_body(x_ref, hf_ref, hb_ref, wu_ref, whf_ref, whb_ref, o_ref):
    o_ref[...] = (_dot(x_ref[...], wu_ref[...]) + _dot(hf_ref[...], whf_ref[...])
                  + _dot(hb_ref[...], whb_ref[...])).astype(o_ref.dtype)


def s5_readout(xr, h, wu, wh):
    r = xr.shape[0]
    n_j, k, n = wu.shape
    w2 = wh.shape[2]
    tm = min(r, 1024)
    return pl.pallas_call(
        _s5_out_body,
        grid=(n_j, r // tm),
        in_specs=[pl.BlockSpec((tm, k), lambda j, i: (i, j)),
                  pl.BlockSpec((None, tm, w2), lambda j, i: (0, i, j)),
                  pl.BlockSpec((None, tm, w2), lambda j, i: (1, i, j)),
                  pl.BlockSpec((None, k, n), lambda j, i: (j, 0, 0)),
                  pl.BlockSpec((None, None, w2, n), lambda j, i: (0, j, 0, 0)),
                  pl.BlockSpec((None, None, w2, n), lambda j, i: (1, j, 0, 0))],
        out_specs=pl.BlockSpec((tm, n), lambda j, i: (i, j)),
        out_shape=jax.ShapeDtypeStruct((r, n_j * n), BF16),
        compiler_params=_cparams("parallel", "parallel"),
        name="s5_readout",
    )(xr, h, h, wu, wh, wh)


def _gelu_tanh(x):
    return 0.5 * x * (1.0 + jnp.tanh(math.sqrt(2.0 / math.pi) * (x + 0.044715 * (x * x * x))))


def _s5_glu_body(x_ref, y_ref, g_ref, sh_ref, sc_ref, dk_ref, w1_ref, w2_ref, b1_ref, b2_ref, gt_ref,
                 mg_ref, msh_ref, msc_ref, wr_ref, br_ref, o_ref, tok_ref, eid_ref, gate_ref, *, n_groups, n_experts):
    x = x_ref[...]
    u = _norm_mod(x, g_ref[...], sh_ref[...], sc_ref[...])
    y = _gelu_tanh(y_ref[...].astype(F32) + dk_ref[...] * u).astype(BF16)
    o = (_dot(y, w1_ref[...]) + b1_ref[...]) * jax.nn.sigmoid(_dot(y, w2_ref[...]) + b2_ref[...])
    xn = x + gt_ref[...] * o
    o_ref[...] = xn
    tok = _norm_mod(xn, mg_ref[...], msh_ref[...], msc_ref[...])
    _route_tokens(tok, wr_ref, br_ref, tok_ref, eid_ref, gate_ref, n_groups, n_experts)


def s5_glu(x, y, g, shift, scale, d_skip, w1, b1, w2, b2, gate, moe_g, moe_shift, moe_scale, wr, br, n_groups,
           n_experts):
    bsz, seq, d = x.shape
    tm = min(seq, 512)
    nt = seq // tm
    sub = tm // MOE_TM
    row = lambda a: a.reshape(1, d)
    rspec = pl.BlockSpec((1, d), lambda b, i: (0, 0))
    mspec = pl.BlockSpec((None, 1, d), lambda b, i: (b, 0, 0))
    tile = pl.BlockSpec((None, tm, d), lambda b, i: (b, i, 0))
    wspec = pl.BlockSpec((d, d), lambda b, i: (0, 0), pipeline_mode=pl.Buffered(1))
    return pl.pallas_call(
        functools.partial(_s5_glu_body, n_groups=n_groups, n_experts=n_experts),
        grid=(bsz, nt),
        in_specs=[tile, tile, rspec, mspec, mspec, rspec, wspec, wspec, rspec, rspec, mspec,
                  rspec, mspec, mspec,
                  pl.BlockSpec((2, d, LANES), lambda b, i: (0, 0, 0)),
                  pl.BlockSpec((1, LANES), lambda b, i: (0, 0))],
        out_specs=[tile,
                   pl.BlockSpec((tm, d), lambda b, i: (b * nt + i, 0)),
                   pl.BlockSpec((sub, SUBLANES, MOE_TM), lambda b, i: (b * nt + i, 0, 0)),
                   pl.BlockSpec((tm, LANES), lambda b, i: (b * nt + i, 0))],
        out_shape=[jax.ShapeDtypeStruct((bsz, seq, d), F32),
                   jax.ShapeDtypeStruct((bsz * seq, d), F32),
                   jax.ShapeDtypeStruct((bsz * seq // MOE_TM, SUBLANES, MOE_TM), jnp.int32),
                   jax.ShapeDtypeStruct((bsz * seq, LANES), F32)],
        compiler_params=_cparams("parallel", "parallel"),
        name="s5_glu",
    )(x, y, row(g), shift, scale, row(d_skip), w1, w2, row(b1), row(b2), gate,
      row(moe_g), moe_shift, moe_scale, _split_bf16(wr), br)


def s5_mix(xl, hl, hc, g, sh_l, sc_l, gate_l, a_re, a_im, log_step, b_re, b_im, c_re, c_im, d_skip,
           w1, b1, w2, b2, moe_mod, router):
    bsz, seq, d = xl.shape
    xr_c = _s5_arrange(hc)
    xr_l = _s5_arrange(hl)
    rows = bsz * S5_SEGS
    n_ctx = xr_c.shape[0] // rows
    n_lat = xr_l.shape[0] // rows
    ws, wu, wh, lt = _s5_operators(a_re, a_im, log_step, b_re, b_im, c_re, c_im)
    s_ctx, s_lat = s5_inject(xr_c, xr_l, ws)
    h = s5_scan(s_ctx.reshape(n_ctx, rows, -1), s_lat.reshape(n_lat, rows, -1), lt, bsz)
    y = s5_readout(xr_l, h.reshape(2, n_lat * rows, -1), wu, wh)
    y = _s5_unarrange(y, bsz)
    wr, br = _router_weights(*router)
    xl, tok, eid, gates = s5_glu(xl, y, g, sh_l, sc_l, d_skip, w1.astype(BF16), b1, w2.astype(BF16), b2, gate_l,
                                 *moe_mod, wr, br, router[0].shape[1], router[2].shape[1])
    return xl, (tok, eid, gates)


def hyena_mix(x, g, shift, scale, gate, w_in, b_in, conv_w, conv_b, fw1, fb1, fw2, fb2, fw3, freq, skip,
              w_out, b_out):
    seq = x.shape[1]
    if seq >= FFT_MIN_SEQ and (2 * seq) % (2 * FFT_N2) == 0:
        a, dd, _ = hyena_filter_taps(seq, fw1, fb1, fw2, fb2, fw3, freq, F32)
        v, x0 = hyena_in(x, g, shift, scale, w_in, b_in, conv_w, conv_b, F32)
        yg = hyena_conv_fft(v, x0, skip, a, dd)
    else:
        cmat, smat = dft_matrices(seq)
        a, dd, kn = hyena_filter_taps(seq, fw1, fb1, fw2, fb2, fw3, freq, BF16)
        kr, ki = hyena_filter_dft(a, dd, cmat, smat)
        v, x0 = hyena_in(x, g, shift, scale, w_in, b_in, conv_w, conv_b, BF16)
        yg = hyena_conv(v, x0, skip, kr, ki, kn, cmat, smat)
    return mm_residual(yg, w_out, b_out, x, gate)


def kernel(x, c, ctx, c_ctx, ada_w, ada_b, norm_g, final_g, hy_w_in, hy_b_in, hy_conv_w, hy_conv_b, hy_fw1,
           hy_fb1, hy_fw2, hy_fb2, hy_fw3, hy_freq, hy_skip, hy_w_out, hy_b_out, s5_a_re, s5_a_im,
           s5_log_step, s5_b_re, s5_b_im, s5_c_re, s5_c_im, s5_d, s5_w1, s5_b1, s5_w2, s5_b2, moe_wg, moe_bg,
           moe_we, moe_be, moe_w_gate, moe_w_up, moe_w_down):
    bsz, _, d = x.shape
    depth = ada_w.shape[0]
    assert depth == 2 and bsz < SUBLANES
    c_all = jnp.concatenate([c, c_ctx[None, :], jnp.zeros((SUBLANES - bsz - 1, d), F32)], axis=0)
    mods = ada_mod(c_all, ada_w, ada_b)

    def mod_rows(layer, k):
        lat = mods[layer, :bsz, k * d:(k + 1) * d][:, None, :]
        cx = jnp.broadcast_to(mods[layer, bsz, k * d:(k + 1) * d][None, None, :], (bsz, 1, d))
        return lat, cx

    (sh_a, csh_a), (sc_a, csc_a), (gt_a, cgt_a) = mod_rows(0, 0), mod_rows(0, 1), mod_rows(0, 2)
    (sh_f, csh_f), (sc_f, csc_f), (gt_f, cgt_f) = mod_rows(0, 3), mod_rows(0, 4), mod_rows(0, 5)
    hy = (hy_w_in[0].astype(BF16), hy_b_in[0], hy_conv_w[0], hy_conv_b[0], hy_fw1[0], hy_fb1[0], hy_fw2[0],
          hy_fb2[0], hy_fw3[0], hy_freq[0], hy_skip[0], hy_w_out[0].astype(BF16), hy_b_out[0])
    xl = hyena_mix(x, norm_g[0, 0], sh_a, sc_a, gt_a, *hy)
    xc = hyena_mix(ctx, norm_g[0, 0], csh_a, csc_a, cgt_a, *hy)
    (sh_a, csh_a), (sc_a, csc_a), (gt_a, _) = mod_rows(1, 0), mod_rows(1, 1), mod_rows(1, 2)
    (xl, hl), (_, hc) = hier_moe([(xl, sh_f, sc_f, gt_f), (xc, csh_f, csc_f, cgt_f)], norm_g[0, 1],
                                 moe_wg[0], moe_bg[0], moe_we[0], moe_be[0], moe_w_gate, moe_w_up, moe_w_down, 0,
                                 final_g, False,
                                 next_mods=[(norm_g[1, 0], sh_a, sc_a), (norm_g[1, 0], csh_a, csc_a)])
    (sh_f, _), (sc_f, _), (gt_f, _) = mod_rows(1, 3), mod_rows(1, 4), mod_rows(1, 5)
    router = (moe_wg[1], moe_bg[1], moe_we[1], moe_be[1])
    xl, routed = s5_mix(xl, hl, hc, norm_g[1, 0], sh_a, sc_a, gt_a, s5_a_re[0], s5_a_im[0], s5_log_step[0],
                        s5_b_re[0], s5_b_im[0], s5_c_re[0], s5_c_im[0], s5_d[0], s5_w1[0], s5_b1[0], s5_w2[0],
                        s5_b2[0], (norm_g[1, 1], sh_f, sc_f), router)
    (out,) = hier_moe([(xl, sh_f, sc_f, gt_f)], norm_g[1, 1], *router, moe_w_gate, moe_w_up, moe_w_down, 1,
                      final_g, True, routed=routed)
    return out
```

```python
import functools
import math

import numpy as np
import jax
import jax.numpy as jnp
from jax import lax
from jax.experimental import pallas as pl
from jax.experimental.pallas import tpu as pltpu

F32 = jnp.float32
BF16 = jnp.bfloat16
HIGHEST = lax.Precision.HIGHEST

NORM_EPS = 1e-6
HY_DECAY_TARGET = 1e-2
HY_FAST_PCT = 0.3
HY_SLOW_PCT = 1.5
TOP_K = 2

V7X_VMEM_LIMIT_BYTES = 56 * 1024 * 1024
LANES = 128
SUBLANES = 8
S5_TAU = 8
S5_SEGS = 2
S5_SCAN_UNROLL = 8
MOE_TM = 256
MOE_BM = 256
MOE_NCH = 12
MOE_STAGE = 4
MOE_CAST_ROWS = 128
MOE_WIN_ALIGN = 1024
MOE_WIN = 2 * MOE_WIN_ALIGN
NEG_BIG = -1e30


def _cparams(*sem):
    return pltpu.CompilerParams(dimension_semantics=sem, vmem_limit_bytes=V7X_VMEM_LIMIT_BYTES)


def _norm_mod(x, g, shift, scale):
    ms = jnp.mean(x * x, axis=-1, keepdims=True)
    return (x * lax.rsqrt(ms + NORM_EPS) * g) * (1.0 + scale) + shift


def _dot(a, b):
    return jnp.dot(a, b, preferred_element_type=F32)


def _ada_body(c_ref, w_ref, b_ref, o_ref):
    x = c_ref[...]
    s = (x * jax.nn.sigmoid(x)).astype(BF16)
    o_ref[...] = _dot(s, w_ref[...].astype(BF16)) + b_ref[...]


def ada_mod(c_all, ada_w, ada_b):
    depth, d, n = ada_w.shape
    tn = min(n, 1024)
    return pl.pallas_call(
        _ada_body,
        grid=(depth, n // tn),
        in_specs=[pl.BlockSpec((SUBLANES, d), lambda l, j: (0, 0)),
                  pl.BlockSpec((None, d, tn), lambda l, j: (l, 0, j)),
                  pl.BlockSpec((None, 1, tn), lambda l, j: (l, 0, j))],
        out_specs=pl.BlockSpec((None, SUBLANES, tn), lambda l, j: (l, 0, j)),
        out_shape=jax.ShapeDtypeStruct((depth, SUBLANES, n), F32),
        compiler_params=_cparams("parallel", "parallel"),
        name="ada_mod",
    )(c_all, ada_w, ada_b.reshape(depth, 1, n))


def _hy_in_body(xp_ref, xm_ref, xn_ref, g_ref, sh_ref, sc_ref,
                w0_ref, w1_ref, w2_ref, b0_ref, b1_ref, b2_ref,
                cw0_ref, cw1_ref, cw2_ref, cb0_ref, cb1_ref, cb2_ref,
                v_ref, x0_ref):
    i = pl.program_id(2)
    ni = pl.num_programs(2)
    tm = xm_ref.shape[0]
    x = jnp.concatenate([xp_ref[...], xm_ref[...], xn_ref[...]], axis=0)
    h = _norm_mod(x, g_ref[...], sh_ref[...], sc_ref[...]).astype(BF16)
    rows = lax.broadcasted_iota(jnp.int32, (tm + 2 * SUBLANES, 1), 0)
    valid = jnp.logical_and(jnp.logical_or(rows >= SUBLANES, i > 0),
                            jnp.logical_or(rows < tm + SUBLANES, i < ni - 1))

    def part(w_ref, b_ref, cw_ref, cb_ref):
        z = jnp.where(valid, _dot(h, w_ref[...]) + b_ref[...], 0.0)
        cw = cw_ref[...]
        zp = pltpu.roll(z, 1, axis=0)[SUBLANES:tm + SUBLANES]
        zn = pltpu.roll(z, tm + 2 * SUBLANES - 1, axis=0)[SUBLANES:tm + SUBLANES]
        return zp * cw[0:1] + z[SUBLANES:tm + SUBLANES] * cw[1:2] + zn * cw[2:3] + cb_ref[...]

    x0 = part(w0_ref, b0_ref, cw0_ref, cb0_ref)
    x1 = part(w1_ref, b1_ref, cw1_ref, cb1_ref)
    v = part(w2_ref, b2_ref, cw2_ref, cb2_ref) * x1
    v_ref[...] = v.astype(v_ref.dtype)
    x0_ref[...] = x0.astype(BF16)


def hyena_in(x, g, shift, scale, w_in, b_in, conv_w, conv_b, v_dtype):
    bsz, seq, d = x.shape
    tm = min(seq, 512)
    tn = min(d, 1024)
    nj = d // tn
    r8 = tm // SUBLANES
    last8 = seq // SUBLANES - 1
    row = lambda a: a.reshape(1, -1)
    wspec = lambda k: pl.BlockSpec((d, tn), lambda j, b, i: (0, k * nj + j))
    rspec = lambda k: pl.BlockSpec((1, tn), lambda j, b, i: (0, k * nj + j))
    cspec = lambda k: pl.BlockSpec((3, tn), lambda j, b, i: (0, k * nj + j))
    mspec = pl.BlockSpec((None, 1, d), lambda j, b, i: (b, 0, 0))
    out_spec = pl.BlockSpec((None, tm, tn), lambda j, b, i: (b, i, j))
    return pl.pallas_call(
        _hy_in_body,
        grid=(nj, bsz, seq // tm),
        in_specs=[pl.BlockSpec((None, SUBLANES, d), lambda j, b, i: (b, jnp.maximum(i * r8 - 1, 0), 0)),
                  pl.BlockSpec((None, tm, d), lambda j, b, i: (b, i, 0)),
                  pl.BlockSpec((None, SUBLANES, d), lambda j, b, i: (b, jnp.minimum((i + 1) * r8, last8), 0)),
                  pl.BlockSpec((1, d), lambda j, b, i: (0, 0)), mspec, mspec,
                  wspec(0), wspec(1), wspec(2), rspec(0), rspec(1), rspec(2),
                  cspec(0), cspec(1), cspec(2), rspec(0), rspec(1), rspec(2)],
        out_specs=[out_spec, out_spec],
        out_shape=[jax.ShapeDtypeStruct((bsz, seq, d), v_dtype), jax.ShapeDtypeStruct((bsz, seq, d), BF16)],
        compiler_params=_cparams("parallel", "parallel", "parallel"),
        name="hyena_in",
    )(x, x, x, row(g), shift, scale, w_in, w_in, w_in, row(b_in), row(b_in), row(b_in),
      conv_w, conv_w, conv_w, row(conv_b), row(conv_b), row(conv_b))


def _dft_tables(seq, blk):
    n = 2 * seq
    s = np.arange(seq, dtype=np.int64)[None, :]
    fl = np.arange(blk, dtype=np.int64)[:, None]
    fh = (np.arange(seq // blk, dtype=np.int64) * blk)[:, None]
    w = 2.0 * math.pi / n
    ang_b = ((fl * s) % n) * w
    ang_a = ((fh * s) % n) * w
    f32 = lambda m: jnp.asarray(m.astype(np.float32))
    return (f32(np.cos(ang_a)[:, None, :]), f32(np.sin(ang_a)[:, None, :]), f32(np.cos(ang_b)), f32(np.sin(ang_b)))


def _dft_gen_body(ca_ref, sa_ref, cb_ref, sb_ref, c_ref, s_ref):
    ca, sa, cb, sb = ca_ref[...], sa_ref[...], cb_ref[...], sb_ref[...]
    c_ref[...] = (ca * cb - sa * sb).astype(BF16)
    s_ref[...] = (sa * cb + ca * sb).astype(BF16)


def dft_matrices(seq):
    blk = min(seq, 256)
    ca, sa, cb, sb = _dft_tables(seq, blk)
    aspec = pl.BlockSpec((None, 1, seq), lambda i: (i, 0, 0))
    bspec = pl.BlockSpec((blk, seq), lambda i: (0, 0))
    ospec = pl.BlockSpec((blk, seq), lambda i: (i, 0))
    return pl.pallas_call(
        _dft_gen_body,
        grid=(seq // blk,),
        in_specs=[aspec, aspec, bspec, bspec],
        out_specs=[ospec, ospec],
        out_shape=[jax.ShapeDtypeStruct((seq, seq), BF16)] * 2,
        compiler_params=_cparams("parallel"),
        name="dft_matrices",
    )(ca, sa, cb, sb)


def _alt_sign(rows):
    return jnp.where((rows & 1) == 0, 1.0, -1.0).astype(F32)


def _filt_body(h2_ref, wf_ref, wb_ref, dl_ref, a_ref, d_ref, ny_ref):
    seq = h2_ref.shape[0]
    h2 = h2_ref[...]
    row = lax.broadcasted_iota(jnp.int32, (seq, 1), 0)
    t = row.astype(F32) * (1.0 / (seq - 1))
    win = jnp.exp(-t * dl_ref[...])
    h_hi = h2.astype(BF16)
    h_lo = (h2 - h_hi.astype(F32)).astype(BF16)
    dot3 = lambda w_ref: _dot(h_hi, w_ref[0]) + _dot(h_hi, w_ref[1]) + _dot(h_lo, w_ref[0])
    hf = dot3(wf_ref) * win
    hb = dot3(wb_ref) * win
    hb = jnp.where(row == 0, 0.0, hb)
    nrm = (jnp.sum(jnp.abs(hf), axis=0, keepdims=True) + jnp.sum(jnp.abs(hb), axis=0, keepdims=True))
    inv = 1.0 / nrm
    a = (hf + hb) * inv
    a_ref[...] = a.astype(a_ref.dtype)
    d_ref[...] = ((hb - hf) * inv).astype(d_ref.dtype)
    ny = jnp.sum(a * _alt_sign(row), axis=0, keepdims=True) * (1.0 / (2 * seq))
    ny_ref[...] = jnp.broadcast_to(ny, ny_ref.shape)


def _khat_body(a_ref, d_ref, c_ref, s_ref, kr_ref, ki_ref):
    i = pl.program_id(1)
    tm = c_ref.shape[0]
    seq = c_ref.shape[1]
    f = i * tm + lax.broadcasted_iota(jnp.int32, (tm, 1), 0)
    w = jnp.where(f == 0, 1.0, 2.0).astype(F32) * (1.0 / (2 * seq))
    kr_ref[...] = _dot(c_ref[...], a_ref[...]) * w
    ki_ref[...] = _dot(s_ref[...], d_ref[...]) * w


def hyena_filter_taps(seq, fw1, fb1, fw2, fb2, fw3, freq, taps_dtype):
    d = fw3.shape[1] // 2
    bands_n = (fw1.shape[0] - 1) // 2
    t = np.linspace(0.0, 1.0, seq)[:, None]
    w = (2.0 * math.pi / seq) * np.arange(seq)[:, None]
    bands = np.linspace(1e-4, bands_n - 1, bands_n)[None, :]
    z = jnp.asarray(np.concatenate([t, np.cos(bands * w), -np.sin(bands * w)], axis=-1).astype(np.float32))
    h = jnp.sin(freq * (jnp.dot(z, fw1, precision=HIGHEST) + fb1))
    h2 = jnp.sin(freq * (jnp.dot(h, fw2, precision=HIGHEST) + fb2))
    max_decay = math.log(HY_DECAY_TARGET) / HY_FAST_PCT
    min_decay = math.log(HY_DECAY_TARGET) / HY_SLOW_PCT
    deltas = jnp.abs(jnp.linspace(min_decay, max_decay, d, dtype=F32))[None, :]

    order = h2.shape[1]
    fw3_split = _split_bf16(fw3)
    tn = min(d, 256)
    nj = d // tn
    return pl.pallas_call(
        _filt_body,
        grid=(nj,),
        in_specs=[pl.BlockSpec((seq, order), lambda j: (0, 0)),
                  pl.BlockSpec((2, order, tn), lambda j: (0, 0, j)),
                  pl.BlockSpec((2, order, tn), lambda j: (0, 0, nj + j)),
                  pl.BlockSpec((1, tn), lambda j: (0, j))],
        out_specs=[pl.BlockSpec((seq, tn), lambda j: (0, j)),
                   pl.BlockSpec((seq, tn), lambda j: (0, j)),
                   pl.BlockSpec((SUBLANES, tn), lambda j: (0, j))],
        out_shape=[jax.ShapeDtypeStruct((seq, d), taps_dtype), jax.ShapeDtypeStruct((seq, d), taps_dtype),
                   jax.ShapeDtypeStruct((SUBLANES, d), F32)],
        compiler_params=_cparams("parallel"),
        name="hyena_filter_taps",
    )(h2, fw3_split, fw3_split, deltas)


def hyena_filter_dft(a, dd, cmat, smat):
    seq, d = a.shape
    tm = min(seq, 512)
    tn2 = min(d, 512)
    return pl.pallas_call(
        _khat_body,
        grid=(d // tn2, seq // tm),
        in_specs=[pl.BlockSpec((seq, tn2), lambda j, i: (0, j)),
                  pl.BlockSpec((seq, tn2), lambda j, i: (0, j)),
                  pl.BlockSpec((tm, seq), lambda j, i: (i, 0)),
                  pl.BlockSpec((tm, seq), lambda j, i: (i, 0))],
        out_specs=[pl.BlockSpec((tm, tn2), lambda j, i: (i, j))] * 2,
        out_shape=[jax.ShapeDtypeStruct((seq, d), F32)] * 2,
        compiler_params=_cparams("parallel", "parallel"),
        name="hyena_filter_dft",
    )(a, dd, cmat, smat)


def _dft_fwd_body(v_ref, c_ref, s_ref, kr_ref, ki_ref, kn_ref, ya_ref, yb_ref, yn_ref):
    i = pl.program_id(2)
    v = v_ref[...]
    vr = _dot(c_ref[...], v)
    p = _dot(s_ref[...], v)
    kr = kr_ref[...]
    ki = ki_ref[...]
    ya_ref[...] = (vr * kr + p * ki).astype(BF16)
    yb_ref[...] = (p * kr - vr * ki).astype(BF16)

    @pl.when(i == 0)
    def _():
        seq = v.shape[0]
        row = lax.broadcasted_iota(jnp.int32, (seq, 1), 0)
        vl = jnp.sum(v.astype(F32) * _alt_sign(row), axis=0, keepdims=True)
        yn_ref[...] = jnp.broadcast_to(vl * kn_ref[0:1, :], yn_ref.shape)


def _dft_inv_body(ya_ref, yb_ref, c_ref, s_ref, v_ref, x0_ref, skip_ref, yn_ref, o_ref):
    i = pl.program_id(2)
    tm = c_ref.shape[0]
    acc = _dot(c_ref[...], ya_ref[...]) + _dot(s_ref[...], yb_ref[...])
    t = i * tm + lax.broadcasted_iota(jnp.int32, (tm, 1), 0)
    y = acc + _alt_sign(t) * yn_ref[0:1, :] + skip_ref[...] * v_ref[...].astype(F32)
    o_ref[...] = (y * x0_ref[...].astype(F32)).astype(BF16)


def hyena_conv(v, x0, skip, kr, ki, kn, cmat, smat):
    bsz, seq, d = v.shape
    tm = min(seq, 512)
    tn = min(d, 512)
    grid = (bsz, d // tn, seq // tm)
    full = pl.BlockSpec((None, seq, tn), lambda b, j, i: (b, 0, j))
    mat = pl.BlockSpec((tm, seq), lambda b, j, i: (i, 0))
    tile = pl.BlockSpec((None, tm, tn), lambda b, j, i: (b, i, j))
    ktile = pl.BlockSpec((tm, tn), lambda b, j, i: (i, j))
    nyq = pl.BlockSpec((None, SUBLANES, tn), lambda b, j, i: (b, 0, j))
    ya, yb, yn = pl.pallas_call(
        _dft_fwd_body,
        grid=grid,
        in_specs=[full, mat, mat, ktile, ktile, pl.BlockSpec((SUBLANES, tn), lambda b, j, i: (0, j))],
        out_specs=[tile, tile, nyq],
        out_shape=[jax.ShapeDtypeStruct((bsz, seq, d), BF16)] * 2
        + [jax.ShapeDtypeStruct((bsz, SUBLANES, d), F32)],
        compiler_params=_cparams("parallel", "parallel", "arbitrary"),
        name="hyena_dft_fwd",
    )(v, cmat, smat, kr, ki, kn)
    return pl.pallas_call(
        _dft_inv_body,
        grid=grid,
        in_specs=[full, full, mat, mat, tile, tile, pl.BlockSpec((1, tn), lambda b, j, i: (0, j)), nyq],
        out_specs=tile,
        out_shape=jax.ShapeDtypeStruct((bsz, seq, d), BF16),
        compiler_params=_cparams("parallel", "parallel", "parallel"),
        name="hyena_dft_inv",
    )(ya, yb, cmat, smat, v, x0, skip.reshape(1, d), yn)


FFT_N2 = 128
FFT_MIN_SEQ = 1024
FFT_UNROLL = 8


def _unroll_for(trips, cap):
    return max(u for u in range(1, cap + 1) if trips % u == 0)


def _fft_matrices(seq):
    n = 2 * seq
    n2 = FFT_N2
    n1 = n // n2
    r8 = SUBLANES
    q = np.arange(n2 // r8, dtype=np.int64)[:, None, None, None]
    nf = n1 // 2 + 1
    f1 = np.arange(nf, dtype=np.int64)[None, :, None, None]
    r = np.arange(r8, dtype=np.int64)[None, None, :, None]
    t1 = np.arange(n1 // 2, dtype=np.int64)[None, None, None, :]
    ang = ((f1 * (t1 * n2 + q * r8 + r)) % n) * (2.0 * math.pi / n)
    g = np.stack([np.cos(ang), -np.sin(ang)], axis=3)
    eye = np.eye(r8)[None, None, :, None, None, :]
    ma = (g[..., None] * eye).reshape(n2 // r8, nf * r8 * 2, (n1 // 2) * r8).astype(np.float32)
    f2 = np.arange(n2, dtype=np.int64)[:, None]
    t2 = np.arange(n2, dtype=np.int64)[None, :]
    th = ((f2 * t2) % n2) * (2.0 * math.pi / n2)
    co, si = np.cos(th), np.sin(th)
    wc = np.stack([np.stack([co, si], axis=-1), np.stack([-si, co], axis=-1)], axis=0)
    wc = wc.reshape(2 * n2, 2 * n2).astype(np.float32)
    as_bf16 = lambda m: jnp.asarray(np.ascontiguousarray(m).astype(BF16))
    return as_bf16(ma), as_bf16(np.swapaxes(ma, 1, 2)), as_bf16(wc), as_bf16(wc.T)


def _fft_stage_a(x_ref, ma_ref, s1):
    n1h, n_q, r8, tn = x_ref.shape
    nf = s1.shape[0]

    def body(q, carry):
        x = x_ref[:, pl.ds(q, 1), :, :].reshape(n1h * r8, tn).astype(BF16)
        a = _dot(ma_ref[q], x).astype(BF16)
        s1[:, pl.ds(pl.multiple_of(q * 2 * r8, 2 * r8), 2 * r8), :] = a.reshape(nf, 2 * r8, tn)
        return carry

    lax.fori_loop(0, n_q, body, 0, unroll=FFT_UNROLL)


def _fft_conv_body(v_ref, x0_ref, k_ref, skip_ref, ma_ref, mat_ref, wc_ref, wci_ref, o_ref, s1, ysc):
    n1h, n_q, r8, tn = v_ref.shape
    nf = s1.shape[0]
    n2 = s1.shape[1] // 2
    seq = n1h * n_q * r8
    _fft_stage_a(v_ref, ma_ref, s1)

    def slab(f, carry):
        y = _dot(wc_ref[...], s1[f])
        yr, yi = y[:n2], y[n2:]
        kr = k_ref[f, 0].astype(F32)
        ki = k_ref[f, 1].astype(F32)
        p = jnp.concatenate([yr * kr - yi * ki, yr * ki + yi * kr], axis=0).astype(BF16)
        s1[f] = _dot(wci_ref[...], p).astype(BF16)
        return carry

    lax.fori_loop(0, nf, slab, 0, unroll=_unroll_for(nf, 2 * FFT_UNROLL))

    def inv_a(q, carry):
        z = s1[:, pl.ds(pl.multiple_of(q * 2 * r8, 2 * r8), 2 * r8), :].reshape(nf * 2 * r8, tn)
        ysc[:, pl.ds(q, 1), :, :] = _dot(mat_ref[q], z).reshape(n1h, 1, r8, tn)
        return carry

    lax.fori_loop(0, n_q, inv_a, 0, unroll=FFT_UNROLL)
    y = ysc[...].reshape(seq, tn) + skip_ref[...] * v_ref[...].reshape(seq, tn)
    o_ref[...] = (y * x0_ref[...].astype(F32)).astype(BF16)


def _fft_filter_body(a_ref, d_ref, ma_ref, wc_ref, k_ref, s1):
    nf = s1.shape[0]
    n2 = s1.shape[1] // 2
    scale = 1.0 / (2 * (nf - 1) * n2)
    for src_ref, part, sign in ((a_ref, 0, scale), (d_ref, 1, -scale)):
        _fft_stage_a(src_ref, ma_ref, s1)

        def slab(f, carry):
            y = _dot(wc_ref[part * n2:(part + 1) * n2, :], s1[f])
            mirrored = jnp.logical_and(f > 0, f < nf - 1)
            k_ref[f, part] = (y * (sign * jnp.where(mirrored, 2.0, 1.0))).astype(BF16)
            return carry

        lax.fori_loop(0, nf, slab, 0, unroll=_unroll_for(nf, FFT_UNROLL))


def hyena_conv_fft(v, x0, skip, a, dd):
    bsz, seq, d = v.shape
    n2 = FFT_N2
    n1 = 2 * seq // n2
    nf = n1 // 2 + 1
    n_q = n2 // SUBLANES
    tn = min(d, 256)
    ma, mat, wc, wci = _fft_matrices(seq)
    const = lambda shape: pl.BlockSpec(shape, lambda *_: (0,) * len(shape), pipeline_mode=pl.Buffered(1))
    view = lambda t: t.reshape(t.shape[:-2] + (n1 // 2, n_q, SUBLANES, d))
    tap = pl.BlockSpec((n1 // 2, n_q, SUBLANES, tn), lambda j: (0, 0, 0, j))
    khat = pl.pallas_call(
        _fft_filter_body,
        grid=(d // tn,),
        in_specs=[tap, tap, const(ma.shape), const(wc.shape)],
        out_specs=pl.BlockSpec((nf, 2, n2, tn), lambda j: (0, 0, 0, j)),
        out_shape=jax.ShapeDtypeStruct((nf, 2, n2, d), BF16),
        scratch_shapes=[pltpu.VMEM((nf, 2 * n2, tn), BF16)],
        compiler_params=_cparams("parallel"),
        name="hyena_filter_fft",
    )(view(a), view(dd), ma, wc)
    return pl.pallas_call(
        _fft_conv_body,
        grid=(d // tn, bsz),
        in_specs=[pl.BlockSpec((None, n1 // 2, n_q, SUBLANES, tn), lambda j, b: (b, 0, 0, 0, j)),
                  pl.BlockSpec((None, seq, tn), lambda j, b: (b, 0, j)),
                  pl.BlockSpec((nf, 2, n2, tn), lambda j, b: (0, 0, 0, j), pipeline_mode=pl.Buffered(1)),
                  pl.BlockSpec((1, tn), lambda j, b: (0, j)),
                  const(ma.shape), const(mat.shape), const(wc.shape), const(wci.shape)],
        out_specs=pl.BlockSpec((None, seq, tn), lambda j, b: (b, 0, j)),
        out_shape=jax.ShapeDtypeStruct((bsz, seq, d), BF16),
        scratch_shapes=[pltpu.VMEM((nf, 2 * n2, tn), BF16), pltpu.VMEM((n1 // 2, n_q, SUBLANES, tn), F32)],
        compiler_params=_cparams("parallel", "arbitrary"),
        name="hyena_conv_fft",
    )(view(v), x0, khat, skip.reshape(1, d), ma, mat, wc, wci)


def _mm_res_body(x_ref, w_ref, b_ref, res_ref, gate_ref, o_ref):
    o_ref[...] = res_ref[...] + gate_ref[...] * (_dot(x_ref[...], w_ref[...]) + b_ref[...])


def mm_residual(x, w, b, res, gate):
    bsz, seq, k = x.shape
    n = w.shape[1]
    tm = min(seq, 512)
    return pl.pallas_call(
        _mm_res_body,
        grid=(bsz, seq // tm),
        in_specs=[pl.BlockSpec((None, tm, k), lambda b, i: (b, i, 0)),
                  pl.BlockSpec((k, n), lambda b, i: (0, 0)),
                  pl.BlockSpec((1, n), lambda b, i: (0, 0)),
                  pl.BlockSpec((None, tm, n), lambda b, i: (b, i, 0)),
                  pl.BlockSpec((None, 1, n), lambda b, i: (b, 0, 0))],
        out_specs=pl.BlockSpec((None, tm, n), lambda b, i: (b, i, 0)),
        out_shape=jax.ShapeDtypeStruct((bsz, seq, n), F32),
        compiler_params=_cparams("parallel", "parallel"),
        name="mm_residual",
    )(x, w, b.reshape(1, n), res, gate)


def _moe_pre_body(*refs, n_groups, n_experts, tile_offs):
    n_streams = len(tile_offs) - 1
    g_ref, wr_ref, br_ref, tok_ref, eid_ref, gate_ref = refs[3 * n_streams:]
    i = pl.program_id(0)
    for k in range(n_streams):
        x_ref, sh_ref, sc_ref = refs[3 * k:3 * k + 3]

        @pl.when(jnp.logical_and(i >= tile_offs[k], i < tile_offs[k + 1]))
        def _():
            tok = _norm_mod(x_ref[...], g_ref[...], sh_ref[...], sc_ref[...])
            _route_tokens(tok, wr_ref, br_ref, tok_ref, eid_ref, gate_ref, n_groups, n_experts)


def _route_tokens(tok, wr_ref, br_ref, tok_ref, eid_ref, gate_ref, n_groups, n_experts):
    tok_ref[...] = tok
    t_hi = tok.astype(BF16)
    t_lo = (tok - t_hi.astype(F32)).astype(BF16)
    logits = (_dot(t_hi, wr_ref[0]) + _dot(t_hi, wr_ref[1]) + _dot(t_lo, wr_ref[0])) + br_ref[...]
    lane = lax.broadcasted_iota(jnp.int32, logits.shape, 1)
    per = n_experts // n_groups
    big = jnp.int32(1 << 20)
    gmask = jnp.logical_and(lane >= n_experts, lane < n_experts + n_groups)
    gl = jnp.where(gmask, logits, NEG_BIG)
    gmax = jnp.max(gl, axis=-1, keepdims=True)
    gidx = jnp.min(jnp.where(gl == gmax, lane - n_experts, big), axis=-1, keepdims=True)
    p_top = 1.0 / jnp.sum(jnp.where(gmask, jnp.exp(gl - gmax), 0.0), axis=-1, keepdims=True)
    lo = gidx * per
    emask = jnp.logical_and(lane >= lo, lane < lo + per)
    el = jnp.where(emask, logits, NEG_BIG)
    m1 = jnp.max(el, axis=-1, keepdims=True)
    i1 = jnp.min(jnp.where(el == m1, lane, big), axis=-1, keepdims=True)
    el2 = jnp.where(lane == i1, NEG_BIG, el)
    m2 = jnp.max(el2, axis=-1, keepdims=True)
    i2 = jnp.min(jnp.where(el2 == m2, lane, big), axis=-1, keepdims=True)
    e21 = jnp.exp(m2 - m1)
    g1 = p_top / (1.0 + e21)
    g2 = g1 * e21
    ids = jnp.where(lane == 0, i1, jnp.where(lane == 1, i2, -1))
    ids_t = ids.T
    for h in range(eid_ref.shape[0]):
        eid_ref[h] = ids_t[0:SUBLANES, h * MOE_TM:(h + 1) * MOE_TM]
    gate_ref[...] = jnp.where(lane == 0, g1, jnp.where(lane == 1, g2, 0.0))


def _split_bf16(w):
    hi = w.astype(BF16)
    return jnp.stack([hi, (w - hi.astype(F32)).astype(BF16)])


def _router_weights(wg, bg, we, be):
    pad = LANES - we.shape[1] - wg.shape[1]
    wr = jnp.pad(jnp.concatenate([we, wg], axis=1), ((0, 0), (0, pad)))
    br = jnp.pad(jnp.concatenate([be, bg]), (0, pad)).reshape(1, LANES)
    return wr, br


def moe_pre(streams, g, wr, br, n_groups, n_experts):
    d = streams[0][0].shape[2]
    tm = MOE_TM
    tile_offs = [0]
    in_specs, args = [], []
    for x, shift, scale in streams:
        bsz, seq, _ = x.shape
        nt = seq // tm
        n_tiles = bsz * nt
        off = tile_offs[-1]
        tile_offs.append(off + n_tiles)

        def tile(i, off=off, n_tiles=n_tiles):
            return jnp.clip(i - off, 0, n_tiles - 1)

        in_specs += [pl.BlockSpec((None, tm, d), lambda i, tile=tile, nt=nt: (tile(i) // nt, tile(i) % nt, 0)),
                     pl.BlockSpec((None, 1, d), lambda i, tile=tile, nt=nt: (tile(i) // nt, 0, 0)),
                     pl.BlockSpec((None, 1, d), lambda i, tile=tile, nt=nt: (tile(i) // nt, 0, 0))]
        args += [x, shift, scale]
    in_specs += [pl.BlockSpec((1, d), lambda i: (0, 0)),
                 pl.BlockSpec((2, d, LANES), lambda i: (0, 0, 0)),
                 pl.BlockSpec((1, LANES), lambda i: (0, 0))]
    args += [g.reshape(1, d), _split_bf16(wr), br]
    total = tile_offs[-1] * tm
    rout = pl.BlockSpec((tm, LANES), lambda i: (i, 0))
    tok, eid, gate = pl.pallas_call(
        functools.partial(_moe_pre_body, n_groups=n_groups, n_experts=n_experts, tile_offs=tuple(tile_offs)),
        grid=(tile_offs[-1],),
        in_specs=in_specs,
        out_specs=[pl.BlockSpec((tm, d), lambda i: (i, 0)),
                   pl.BlockSpec((1, SUBLANES, tm), lambda i: (i, 0, 0)), rout],
        out_shape=[jax.ShapeDtypeStruct((total, d), F32),
                   jax.ShapeDtypeStruct((tile_offs[-1], SUBLANES, tm), jnp.int32),
                   jax.ShapeDtypeStruct((total, LANES), F32)],
        compiler_params=_cparams("parallel"),
        name="moe_pre",
    )(*args)
    return tok, eid, gate, tile_offs[:-1]


def _start_row_gather(row_index, n_rows, src_hbm, dst_vmem, sem):
    def body(g, c):
        r0 = pl.multiple_of(g * SUBLANES, SUBLANES)
        dst_tile = dst_vmem.at[pl.ds(r0, SUBLANES)]
        for k in range(SUBLANES):
            pltpu.make_async_copy(src_hbm.at[pl.ds(row_index(r0 + k), 1)], dst_tile.at[pl.ds(k, 1)], sem).start()
        return c

    lax.fori_loop(0, n_rows // SUBLANES, body, 0, unroll=2)


def _wait_row_gather(n_rows, src_hbm, dst_vmem, sem):
    pltpu.make_async_copy(src_hbm.at[pl.ds(0, n_rows)], dst_vmem, sem).wait()


def _expert_body(bv_ref, rk_ref, pe_ref, tot_ref, p0_ref, ts_hbm, tok_ref, wg_hbm, wu_hbm, wd_hbm, o_ref,
                 xbuf, xsem, win, isem, wcache, stg, wsem, cnt, *, layer):
    i = pl.program_id(0)
    n = pl.num_programs(0)
    slot = i % 2
    cr, cc = stg.shape[1:]
    total = tot_ref[0]
    mats_hbm = (wg_hbm, wu_hbm, wd_hbm)

    @pl.when(i == 0)
    def _():
        cnt[0] = 0
        cnt[1] = 0

    def live(b):
        return jnp.logical_and(b < n, bv_ref[jnp.minimum(b, n - 1)] > 0)

    def window(b):
        s = b % 2
        lo = pl.multiple_of((p0_ref[jnp.minimum(b, n - 1)] // MOE_WIN_ALIGN) * MOE_WIN_ALIGN, MOE_WIN_ALIGN)
        return pltpu.make_async_copy(ts_hbm.at[pl.ds(lo, MOE_WIN)],
                                     win.at[pl.ds(pl.multiple_of(s * MOE_WIN, MOE_WIN), MOE_WIN)], isem.at[s])

    def gather(b):
        s = b % 2
        base = s * MOE_WIN + p0_ref[jnp.minimum(b, n - 1)] % MOE_WIN_ALIGN
        _start_row_gather(lambda r: win[base + r], MOE_BM, tok_ref, xbuf.at[s], xsem.at[s])

    @pl.when(jnp.logical_and(i == 0, live(0)))
    def _():
        window(0).start()
        window(0).wait()
        gather(0)

    @pl.when(jnp.logical_and(i == 0, live(1)))
    def _():
        window(1).start()

    @pl.when(live(i + 1))
    def _():
        window(i + 1).wait()
        gather(i + 1)

    @pl.when(live(i + 2))
    def _():
        window(i + 2).start()

    def chunk_geom(c):
        q = c % MOE_NCH
        m = q // 4
        sub = q % 4
        r0 = jnp.where(m < 2, sub, sub // 2) * cr
        c0 = jnp.where(m < 2, 0, sub % 2) * cc
        return m, pl.multiple_of(r0, cr), pl.multiple_of(c0, cc)

    def issue(c):
        e = pe_ref[c // MOE_NCH]
        m, r0, c0 = chunk_geom(c)
        s = c % MOE_STAGE
        for k, w_hbm in enumerate(mats_hbm):
            @pl.when(m == k)
            def _():
                pltpu.make_async_copy(w_hbm.at[layer, e, pl.ds(r0, cr), pl.ds(c0, cc)], stg.at[s],
                                      wsem.at[s]).start()

    def cast(c):
        s = c % MOE_STAGE
        pltpu.make_async_copy(wg_hbm.at[layer, 0, pl.ds(0, cr), pl.ds(0, cc)], stg.at[s], wsem.at[s]).wait()
        ws = (c // MOE_NCH) % 2
        q = c % MOE_NCH
        step = min(MOE_CAST_ROWS, cr)
        assert cr % step == 0

        def slab(k, carry):
            rows = pl.ds(pl.multiple_of(k * step, step), step)
            wcache[ws, q, rows, :] = stg[s, rows, :].astype(BF16)
            return carry

        lax.fori_loop(0, cr // step, slab, 0)

    valid = bv_ref[i] > 0
    rank = rk_ref[i]
    issued = cnt[0]
    done = cnt[1]
    limit = jnp.minimum(total, MOE_NCH * (rank + 2))
    need = jnp.where(valid, MOE_NCH * (rank + 1), done)

    def fill(issued, done):
        hi = jnp.minimum(limit, done + MOE_STAGE)

        def body(c, carry):
            issue(c)
            return carry

        lax.fori_loop(issued, hi, body, 0)
        return jnp.maximum(issued, hi)

    def cast_and_refill(c, issued):
        cast(c)
        more = issued < jnp.minimum(limit, c + 1 + MOE_STAGE)

        @pl.when(more)
        def _():
            issue(issued)

        return issued + more.astype(jnp.int32)

    issued = fill(issued, done)
    issued = lax.fori_loop(done, need, cast_and_refill, issued)
    done = jnp.maximum(done, need)

    @pl.when(valid)
    def _():
        ws = rank % 2
        _wait_row_gather(MOE_BM, tok_ref, xbuf.at[slot], xsem.at[slot])
        x = xbuf[slot].astype(BF16)
        gate = sum(_dot(x[:, k * cr:(k + 1) * cr], wcache[ws, k]) for k in range(4))
        up = sum(_dot(x[:, k * cr:(k + 1) * cr], wcache[ws, 4 + k]) for k in range(4))
        h = (gate * jax.nn.sigmoid(gate) * up).astype(BF16)
        for half in range(2):
            o_ref[:, half * cc:(half + 1) * cc] = sum(
                _dot(h[:, k * cr:(k + 1) * cr], wcache[ws, 8 + 2 * k + half]) for k in range(2))

    @pl.when(jnp.logical_not(valid))
    def _():
        o_ref[...] = jnp.zeros_like(o_ref)

    fetched = issued
    issued = lax.fori_loop(done, fetched, cast_and_refill, issued)
    done = jnp.maximum(done, fetched)
    last = i == n - 1
    tail = jnp.where(last, issued, done)

    def drain(c, carry):
        cast(c)
        return carry

    lax.fori_loop(done, tail, drain, 0)
    cnt[0] = issued
    cnt[1] = jnp.maximum(done, tail)


def moe_experts(tok, tok_sorted, block_p0, block_valid, block_rank, present, n_chunks, w_gate, w_up, w_down,
                layer):
    d = tok.shape[1]
    n_blocks = block_valid.shape[0]
    n_rows = n_blocks * MOE_BM
    dh = w_gate.shape[3]
    assert 2 * dh == d and MOE_NCH == 12
    cr, cc = d // 4, dh
    any_spec = pl.BlockSpec(memory_space=pl.ANY)
    grid_spec = pltpu.PrefetchScalarGridSpec(
        num_scalar_prefetch=5,
        grid=(n_blocks,),
        in_specs=[any_spec, any_spec, any_spec, any_spec, any_spec],
        out_specs=pl.BlockSpec((MOE_BM, d), lambda i, *_: (i, 0)),
        scratch_shapes=[pltpu.VMEM((2, MOE_BM, d), F32), pltpu.SemaphoreType.DMA((2,)),
                        pltpu.SMEM((2 * MOE_WIN,), jnp.int32), pltpu.SemaphoreType.DMA((2,)),
                        pltpu.VMEM((2, MOE_NCH, cr, cc), BF16),
                        pltpu.VMEM((MOE_STAGE, cr, cc), F32), pltpu.SemaphoreType.DMA((MOE_STAGE,)),
                        pltpu.SMEM((2,), jnp.int32)],
    )
    return pl.pallas_call(
        functools.partial(_expert_body, layer=layer),
        grid_spec=grid_spec,
        out_shape=jax.ShapeDtypeStruct((n_rows, d), F32),
        compiler_params=_cparams("arbitrary"),
        name="moe_experts",
    )(block_valid, block_rank, present, n_chunks, block_p0, tok_sorted, tok, w_gate, w_up, w_down)


def _combine_body(dest_ref, nxt_ref, os_ref, gate_ref, res_ref, gt_ref, fg_ref, *rest, final_norm, with_next):
    if with_next:
        ng_ref, nsh_ref, nsc_ref, o_ref, h_ref, buf, sem = rest
    else:
        o_ref, buf, sem = rest
    rows = res_ref.shape[0]
    i = pl.program_id(0)
    n = pl.num_programs(0)
    slot = i % 2

    def start(idx_ref, s):
        for k in range(TOP_K):
            _start_row_gather(lambda r, k=k: idx_ref[k * rows + r], rows, os_ref, buf.at[s, k], sem.at[s])

    @pl.when(i == 0)
    def _():
        start(dest_ref, 0)

    @pl.when(i + 1 < n)
    def _():
        start(nxt_ref, 1 - slot)

    for k in range(TOP_K):
        _wait_row_gather(rows, os_ref, buf.at[slot, k], sem.at[slot])
    gates = gate_ref[...]
    mo = gates[:, 0:1] * buf[slot, 0] + gates[:, 1:2] * buf[slot, 1]
    y = res_ref[...] + gt_ref[...] * mo
    if final_norm:
        ms = jnp.mean(y * y, axis=-1, keepdims=True)
        y = y * lax.rsqrt(ms + NORM_EPS) * fg_ref[...]
    o_ref[...] = y
    if with_next:
        h_ref[...] = _norm_mod(y, ng_ref[...], nsh_ref[...], nsc_ref[...]).astype(h_ref.dtype)


def moe_combine(os, dest, gates, tile0, res, gt, final_g, final_norm, next_mod=None):
    bsz, seq, d = res.shape
    rows = MOE_TM
    nt = seq // rows
    n = bsz * nt
    tile = pl.BlockSpec((None, rows, d), lambda i: (i // nt, i % nt, 0))
    mspec = pl.BlockSpec((None, 1, d), lambda i: (i // nt, 0, 0))
    rspec = pl.BlockSpec((1, d), lambda i: (0, 0))
    dspec = lambda step: pl.BlockSpec((SUBLANES * rows,), lambda i: (tile0 + step(i),), memory_space=pltpu.SMEM)
    in_specs = [dspec(lambda i: i), dspec(lambda i: jnp.minimum(i + 1, n - 1)),
                pl.BlockSpec(memory_space=pl.ANY),
                pl.BlockSpec((rows, LANES), lambda i: (tile0 + i, 0)),
                tile, mspec, rspec]
    args = [dest, dest, os, gates, res, gt, final_g.reshape(1, d)]
    out_specs, out_shape = [tile], [jax.ShapeDtypeStruct((bsz, seq, d), F32)]
    if next_mod is not None:
        in_specs += [rspec, mspec, mspec]
        args += [next_mod[0].reshape(1, d), next_mod[1], next_mod[2]]
        out_specs.append(tile)
        out_shape.append(jax.ShapeDtypeStruct((bsz, seq, d), BF16))
    outs = pl.pallas_call(
        functools.partial(_combine_body, final_norm=final_norm, with_next=next_mod is not None),
        grid=(n,),
        in_specs=in_specs,
        out_specs=out_specs,
        out_shape=out_shape,
        scratch_shapes=[pltpu.VMEM((2, TOP_K, rows, d), F32), pltpu.SemaphoreType.DMA((2,))],
        compiler_params=_cparams("arbitrary"),
        name="moe_combine",
    )(*args)
    return outs if next_mod is not None else outs[0]


def _plan_body(e_ref, dest_ref, tab_ref, blk_ref, present_ref, *, n_experts):
    n_rows, w = e_ref.shape
    e_all = e_ref[...]
    li = lax.broadcasted_iota(jnp.int32, (w, w), 0)
    lj = lax.broadcasted_iota(jnp.int32, (w, w), 1)
    incl = (li <= lj).astype(BF16)
    ri = lax.broadcasted_iota(jnp.int32, (n_rows, n_rows), 0)
    rj = lax.broadcasted_iota(jnp.int32, (n_rows, n_rows), 1)
    before = (rj < ri).astype(BF16)
    elane = lax.broadcasted_iota(jnp.int32, (n_rows, LANES), 1)
    row_tot = jnp.zeros((n_rows, LANES), F32)
    for e in range(n_experts):
        tot = jnp.sum((e_all == e).astype(F32), axis=1, keepdims=True)
        row_tot = row_tot + jnp.where(elane == e, tot, 0.0)
    rows_before = _dot(before, row_tot.astype(BF16))
    counts = jnp.sum(row_tot, axis=0, keepdims=True).astype(jnp.int32)
    lane1 = lax.broadcasted_iota(jnp.int32, (1, LANES), 1)

    def excl_prefix(v):
        acc = v
        sh = 1
        while sh < LANES:
            acc = acc + jnp.where(lane1 >= sh, pltpu.roll(acc, sh, axis=1), 0)
            sh *= 2
        return acc - v

    start = excl_prefix(counts)
    padded = (counts + (MOE_BM - 1)) // MOE_BM * MOE_BM
    pad_start = excl_prefix(padded)
    pad_end = pad_start + padded
    has = (counts > 0).astype(jnp.int32)
    rank = excl_prefix(has)
    n_chunks = MOE_NCH * jnp.sum(has.astype(F32), axis=1, keepdims=True).astype(jnp.int32)
    tab_ref[...] = jnp.concatenate([counts, start, pad_start, pad_end, jnp.broadcast_to(n_chunks, (1, LANES)),
                                    jnp.zeros((SUBLANES - 5, LANES), jnp.int32)], axis=0)
    nb = blk_ref.shape[0]
    f32 = lambda v: v.astype(F32)
    lane_b = lax.broadcasted_iota(jnp.int32, (nb, LANES), 1)
    first_row = lax.broadcasted_iota(jnp.int32, (nb, 1), 0) * MOE_BM
    is_expert = lane_b < n_experts
    expert = jnp.sum(f32(jnp.logical_and(is_expert, pad_end <= first_row)), axis=1, keepdims=True)
    expert = jnp.minimum(expert.astype(jnp.int32), n_experts - 1)
    mine = lane_b == expert
    pick = lambda v: jnp.sum(jnp.where(mine, f32(v), 0.0), axis=1, keepdims=True).astype(jnp.int32)
    live = first_row < jnp.max(f32(pad_end), axis=1, keepdims=True).astype(jnp.int32)
    p0 = jnp.where(live, pick(start) - pick(pad_start) + first_row, 0)
    blk_ref[...] = jnp.where(lane_b == 0, live.astype(jnp.int32),
                             jnp.where(lane_b == 1, pick(rank), jnp.where(lane_b == 2, p0, 0)))
    row_r = lax.broadcasted_iota(jnp.int32, (LANES, LANES), 0)
    lane_e = lax.broadcasted_iota(jnp.int32, (LANES, LANES), 1)
    hit_r = jnp.logical_and(has > 0, rank == row_r)
    present = jnp.sum(jnp.where(hit_r, f32(lane_e), 0.0), axis=1, keepdims=True).astype(jnp.int32)
    present_ref[...] = jnp.broadcast_to(present, (LANES, LANES))
    base = rows_before + pad_start.astype(F32)
    dest = jnp.zeros((n_rows, w), F32)
    for e in range(n_experts):
        hit = e_all == e
        within = _dot(hit.astype(BF16), incl)
        dest = dest + jnp.where(hit, within - 1.0 + base[:, e:e + 1], 0.0)
    dest_ref[...] = dest.astype(jnp.int32)


def _route_plan(eid, n_experts):
    n_tiles, r8, tm = eid.shape
    a = n_tiles * TOP_K * tm
    n_blocks = -(-a // MOE_BM) + n_experts
    assert (tm & (tm - 1)) == 0 and TOP_K == 2
    nb_pad = -(-n_blocks // SUBLANES) * SUBLANES
    dest, tab, blk, present = pl.pallas_call(
        functools.partial(_plan_body, n_experts=n_experts),
        out_shape=[jax.ShapeDtypeStruct((n_tiles * r8, tm), jnp.int32),
                   jax.ShapeDtypeStruct((SUBLANES, LANES), jnp.int32),
                   jax.ShapeDtypeStruct((nb_pad, LANES), jnp.int32),
                   jax.ShapeDtypeStruct((LANES, LANES), jnp.int32)],
        compiler_params=pltpu.CompilerParams(vmem_limit_bytes=V7X_VMEM_LIMIT_BYTES),
        name="moe_route_plan",
    )(eid.reshape(n_tiles * r8, tm))
    e_flat = eid[:, :TOP_K, :].reshape(-1)
    order = jnp.argsort(e_flat).astype(jnp.int32)
    shift = tm.bit_length() - 1
    tok_of = ((order >> (shift + 1)) << shift) | (order & (tm - 1))
    slack = -a % MOE_WIN_ALIGN + MOE_WIN
    tok_sorted = jnp.concatenate([tok_of, jnp.zeros((slack,), jnp.int32)])
    blk = blk[:n_blocks]
    return dest.reshape(-1), tok_sorted, blk[:, 2], blk[:, 0], blk[:, 1], present[:n_experts, 0], tab[4, :1]


def hier_moe(streams, norm_g, wg, bg, we, be, w_gate, w_up, w_down, layer, final_g, final_norm, next_mods=None,
             routed=None):
    n_groups = wg.shape[1]
    n_experts = we.shape[1]
    if routed is None:
        wr, br = _router_weights(wg, bg, we, be)
        tok, eid, gates, tile0s = moe_pre([s[:3] for s in streams], norm_g, wr, br, n_groups, n_experts)
    else:
        (tok, eid, gates), tile0s = routed, [0]
    dest, tok_sorted, block_p0, block_valid, block_rank, present, n_chunks = _route_plan(eid, n_experts)
    os = moe_experts(tok, tok_sorted, block_p0, block_valid, block_rank, present, n_chunks, w_gate, w_up, w_down,
                     layer)
    next_mods = next_mods or [None] * len(streams)
    return [moe_combine(os, dest, gates, tile0, x, gt, final_g, final_norm, nm)
            for (x, _, _, gt), tile0, nm in zip(streams, tile0s, next_mods)]


def _s5_arrange(h):
    bsz, t, d = h.shape
    c = t // (S5_SEGS * S5_TAU)
    h = h.reshape(bsz, S5_SEGS, c, S5_TAU, d // LANES, LANES)
    return h.transpose(2, 0, 1, 4, 3, 5).reshape(c * bsz * S5_SEGS, d * S5_TAU)


def _s5_unarrange(y, bsz):
    r, w = y.shape
    d = w // S5_TAU
    c = r // (bsz * S5_SEGS)
    y = y.reshape(c, bsz, S5_SEGS, d // LANES, S5_TAU, LANES)
    return y.transpose(1, 2, 0, 4, 3, 5).reshape(bsz, S5_SEGS * c * S5_TAU, d)


def _s5_operators(a_re, a_im, log_step, b_re, b_im, c_re, c_im):
    n_g, n_p = a_re.shape[1:]
    n_h = b_re.shape[-1]
    gpt = LANES // n_h
    n_j = n_g // gpt
    tau = S5_TAU
    assert tau * n_h == LANES and 2 * n_p == LANES
    lam_step = lax.complex(a_re, a_im) * jnp.exp(log_step)[..., None]
    lam_bar = jnp.exp(lam_step)
    b_bar = ((lam_bar - 1.0) / lax.complex(a_re, a_im))[..., None] * lax.complex(b_re, b_im)
    c_mat = lax.complex(c_re, c_im)
    ks = jnp.arange(tau + 1, dtype=F32)[None, :, None, None]
    pw = jnp.exp(lam_step[:, None] * ks)
    ein = functools.partial(jnp.einsum, precision=HIGHEST)
    inj_c, cl_c, lt = [], [], []
    tz_c = 0.0
    for d in range(2):
        pos = jnp.arange(tau) if d == 0 else jnp.arange(tau)[::-1]
        inj = (pw[d][tau - 1 - pos][..., None] * b_bar[d][None]).reshape(tau, n_j, gpt, n_p, n_h)
        inj = inj.transpose(1, 0, 2, 4, 3).reshape(n_j, tau * LANES, n_p)
        inj_c.append(jnp.concatenate([inj.real, inj.imag], axis=-1))
        cl = (c_mat[d][None] * pw[d][pos + 1][:, :, None, :]).reshape(tau, n_j, gpt, n_h, n_p)
        cl = cl.transpose(1, 2, 4, 0, 3).reshape(n_j, gpt * n_p, tau * n_h)
        cl_c.append(jnp.concatenate([cl.real, -cl.imag], axis=1))
        mk = ein('gop,kgp,gph->kgoh', c_mat[d], pw[d][:tau], b_bar[d]).real
        diff = pos[:, None] - pos[None, :]
        tz = jnp.where((diff >= 0)[:, :, None, None, None], mk[jnp.clip(diff, 0, tau - 1)], 0.0)
        tz = tz.reshape(tau, tau, n_j, gpt, n_h, n_h)
        tz_c = tz_c + tz.transpose(2, 1, 3, 5, 0, 4).reshape(n_j, tau * LANES, tau * n_h)
        lt_d = pw[d][tau].reshape(n_j, 1, gpt * n_p)
        lt.append(jnp.concatenate([lt_d.real, lt_d.imag], axis=-1))
    ws, wu, wh = s5_expand(jnp.stack(inj_c).astype(BF16), jnp.stack(cl_c).astype(BF16), tz_c.astype(BF16),
                           n_h, n_p)
    return ws, wu, wh, jnp.stack(lt).astype(F32)


def _s5_expand_body(inj_ref, cl_ref, tz_ref, ws_ref, wu_ref, wh_ref, *, n_h, n_p):
    rows = tz_ref.shape[0]
    gpt = LANES // n_h
    row = lax.broadcasted_iota(jnp.int32, (rows, LANES), 0)
    lane = lax.broadcasted_iota(jnp.int32, (rows, LANES), 1)
    sel_r = lax.broadcasted_iota(jnp.int32, (LANES, LANES), 0)
    sel_l = lax.broadcasted_iota(jnp.int32, (LANES, LANES), 1)
    grp_in = (row // n_h) % gpt
    grp_st = (row // n_p) % gpt

    def spread_out(m, t, grp_row):
        sel = jnp.logical_and(sel_r // n_h == t, sel_r % n_h == sel_l % n_h).astype(BF16)
        return jnp.where(grp_row == lane // n_h, _dot(m, sel), 0.0).astype(BF16)

    def spread_state(m, c, q, grp_row):
        sel = jnp.logical_and(sel_r // n_p == c, sel_r % n_p == sel_l % n_p).astype(BF16)
        return jnp.where(grp_row == (LANES // n_p) * q + lane // n_p, _dot(m, sel), 0.0).astype(BF16)

    w2 = 2 * gpt * n_p
    tz = tz_ref[...]
    for t in range(S5_TAU):
        wu_ref[:, t * LANES:(t + 1) * LANES] = spread_out(tz, t, grp_in)
    for d in range(2):
        cl = cl_ref[d]
        inj = inj_ref[d]
        for t in range(S5_TAU):
            wh_ref[d, :, t * LANES:(t + 1) * LANES] = spread_out(cl, t, grp_st)
        for c in range(2):
            for q in range(gpt * n_p // LANES):
                lo = d * w2 + c * gpt * n_p + q * LANES
                ws_ref[:, lo:lo + LANES] = spread_state(inj, c, q, grp_in)


def s5_expand(inj_c, cl_c, tz_c, n_h, n_p):
    n_j, rows, _ = tz_c.shape
    gpt = LANES // n_h
    w2 = 2 * gpt * n_p
    assert rows == S5_TAU * LANES == w2
    cspec = pl.BlockSpec((2, None, rows, LANES), lambda j: (0, j, 0, 0))
    return pl.pallas_call(
        functools.partial(_s5_expand_body, n_h=n_h, n_p=n_p),
        grid=(n_j,),
        in_specs=[cspec, cspec, pl.BlockSpec((None, rows, LANES), lambda j: (j, 0, 0))],
        out_specs=[pl.BlockSpec((None, rows, 2 * w2), lambda j: (j, 0, 0)),
                   pl.BlockSpec((None, rows, rows), lambda j: (j, 0, 0)),
                   pl.BlockSpec((2, None, w2, rows), lambda j: (0, j, 0, 0))],
        out_shape=[jax.ShapeDtypeStruct((n_j, rows, 2 * w2), BF16),
                   jax.ShapeDtypeStruct((n_j, rows, rows), BF16),
                   jax.ShapeDtypeStruct((2, n_j, w2, rows), BF16)],
        compiler_params=_cparams("parallel"),
        name="s5_expand",
    )(inj_c, cl_c, tz_c)


def _s5_inj_body(xc_ref, xl_ref, w_ref, oc_ref, ol_ref):
    w = w_ref[...]
    ol_ref[...] = _dot(xl_ref[...], w)

    @pl.when(pl.program_id(1) == 0)
    def _():
        oc_ref[...] = _dot(xc_ref[...], w)


def s5_inject(xr_c, xr_l, ws):
    r_c, r_l = xr_c.shape[0], xr_l.shape[0]
    n_j, k, n = ws.shape
    tm = r_l // 2 if r_l % 32 == 0 else r_l
    return pl.pallas_call(
        _s5_inj_body,
        grid=(n_j, r_l // tm),
        in_specs=[pl.BlockSpec((r_c, k), lambda j, i: (0, j)),
                  pl.BlockSpec((tm, k), lambda j, i: (i, j)),
                  pl.BlockSpec((None, k, n), lambda j, i: (j, 0, 0))],
        out_specs=[pl.BlockSpec((r_c, n), lambda j, i: (0, j)),
                   pl.BlockSpec((tm, n), lambda j, i: (i, j))],
        out_shape=[jax.ShapeDtypeStruct((r_c, n_j * n), F32), jax.ShapeDtypeStruct((r_l, n_j * n), F32)],
        compiler_params=_cparams("parallel", "arbitrary"),
        name="s5_inject",
    )(xr_c, xr_l, ws)


def _cmul(ar, ai, br, bi):
    return ar * br - ai * bi, ar * bi + ai * br


def _s5_scan_body(sc_ref, sl_ref, lt_ref, h_ref, raw_ref, *, n_ctx, n_lat, bsz):
    d = pl.program_id(1)
    w2 = lt_ref.shape[-1]
    w = w2 // 2
    rows = bsz * S5_SEGS
    seg = lax.broadcasted_iota(jnp.int32, (rows, 1), 0) % S5_SEGS
    is_late = seg != d
    lam_r = lt_ref[:, 0:w]
    lam_i = lt_ref[:, w:w2]
    zero = jnp.zeros((rows, w), F32)
    one = (jnp.ones((1, w), F32), jnp.zeros((1, w), F32))

    def swap_segments(x):
        return jnp.where(seg == 0, pltpu.roll(x, rows - 1, axis=0), pltpu.roll(x, 1, axis=0))

    def phase(s_ref, n_steps, hin_r, hin_i, write):
        def chunk(k):
            return jnp.where(d == 0, k, n_steps - 1 - k)

        def step_raw(k, carry):
            hr, hi = carry
            c = chunk(k)
            raw_ref[c, :, 0:w] = hr
            raw_ref[c, :, w:w2] = hi
            nr, ni = _cmul(lam_r, lam_i, hr, hi)
            return nr + s_ref[c, :, 0:w], ni + s_ref[c, :, w:w2]

        er, ei = lax.fori_loop(0, n_steps, step_raw, (zero, zero), unroll=S5_SCAN_UNROLL)
        pr, pi = lax.fori_loop(0, n_steps, lambda k, q: _cmul(lam_r, lam_i, *q), one)
        dr, di = _cmul(pr, pi, hin_r, hin_i)
        first_r = jnp.where(is_late, 0.0, er + dr)
        first_i = jnp.where(is_late, 0.0, ei + di)
        carry_r = jnp.where(is_late, swap_segments(first_r), hin_r)
        carry_i = jnp.where(is_late, swap_segments(first_i), hin_i)
        if write:
            def step_fix(k, q):
                c = chunk(k)
                fr, fi = _cmul(q[0], q[1], carry_r, carry_i)
                h_ref[c, :, 0:w] = (raw_ref[c, :, 0:w] + fr).astype(h_ref.dtype)
                h_ref[c, :, w:w2] = (raw_ref[c, :, w:w2] + fi).astype(h_ref.dtype)
                return _cmul(lam_r, lam_i, q[0], q[1])

            lax.fori_loop(0, n_steps, step_fix, one, unroll=S5_SCAN_UNROLL)
        lr, li = _cmul(pr, pi, carry_r, carry_i)
        last_r = jnp.where(is_late, er + lr, 0.0)
        last_i = jnp.where(is_late, ei + li, 0.0)
        return (jnp.where(is_late, 0.0, swap_segments(last_r)), jnp.where(is_late, 0.0, swap_segments(last_i)))

    hr, hi = phase(sc_ref, n_ctx, zero, zero, False)
    phase(sl_ref, n_lat, hr, hi, True)


def s5_scan(s_ctx, s_lat, lt, bsz):
    assert S5_SEGS == 2
    n_ctx, rows, _ = s_ctx.shape
    n_lat = s_lat.shape[0]
    n_j = lt.shape[1]
    w2 = lt.shape[-1]
    return pl.pallas_call(
        functools.partial(_s5_scan_body, n_ctx=n_ctx, n_lat=n_lat, bsz=bsz),
        grid=(n_j, 2),
        in_specs=[pl.BlockSpec((n_ctx, rows, w2), lambda j, d: (0, 0, 2 * j + d)),
                  pl.BlockSpec((n_lat, rows, w2), lambda j, d: (0, 0, 2 * j + d)),
                  pl.BlockSpec((None, None, 1, w2), lambda j, d: (d, j, 0, 0))],
        out_specs=pl.BlockSpec((None, n_lat, rows, w2), lambda j, d: (d, 0, 0, j)),
        out_shape=jax.ShapeDtypeStruct((2, n_lat, rows, n_j * w2), BF16),
        scratch_shapes=[pltpu.VMEM((max(n_ctx, n_lat), rows, w2), F32)],
        compiler_params=_cparams("parallel", "parallel"),
        name="s5_scan",
    )(s_ctx, s_lat, lt)


def _s5_out_body(x_ref, hf_ref, hb_ref, wu_ref, whf_ref, whb_ref, o_ref):
    o_ref[...] = (_dot(x_ref[...], wu_ref[...]) + _dot(hf_ref[...], whf_ref[...])
                  + _dot(hb_ref[...], whb_ref[...])).astype(o_ref.dtype)


def s5_readout(xr, h, wu, wh):
    r = xr.shape[0]
    n_j, k, n = wu.shape
    w2 = wh.shape[2]
    tm = min(r, 1024)
    return pl.pallas_call(
        _s5_out_body,
        grid=(n_j, r // tm),
        in_specs=[pl.BlockSpec((tm, k), lambda j, i: (i, j)),
                  pl.BlockSpec((None, tm, w2), lambda j, i: (0, i, j)),
                  pl.BlockSpec((None, tm, w2), lambda j, i: (1, i, j)),
                  pl.BlockSpec((None, k, n), lambda j, i: (j, 0, 0)),
                  pl.BlockSpec((None, None, w2, n), lambda j, i: (0, j, 0, 0)),
                  pl.BlockSpec((None, None, w2, n), lambda j, i: (1, j, 0, 0))],
        out_specs=pl.BlockSpec((tm, n), lambda j, i: (i, j)),
        out_shape=jax.ShapeDtypeStruct((r, n_j * n), BF16),
        compiler_params=_cparams("parallel", "parallel"),
        name="s5_readout",
    )(xr, h, h, wu, wh, wh)


def _gelu_tanh(x):
    return 0.5 * x * (1.0 + jnp.tanh(math.sqrt(2.0 / math.pi) * (x + 0.044715 * (x * x * x))))


def _s5_glu_body(x_ref, y_ref, g_ref, sh_ref, sc_ref, dk_ref, w1_ref, w2_ref, b1_ref, b2_ref, gt_ref,
                 mg_ref, msh_ref, msc_ref, wr_ref, br_ref, o_ref, tok_ref, eid_ref, gate_ref, *, n_groups, n_experts):
    x = x_ref[...]
    u = _norm_mod(x, g_ref[...], sh_ref[...], sc_ref[...])
    y = _gelu_tanh(y_ref[...].astype(F32) + dk_ref[...] * u).astype(BF16)
    o = (_dot(y, w1_ref[...]) + b1_ref[...]) * jax.nn.sigmoid(_dot(y, w2_ref[...]) + b2_ref[...])
    xn = x + gt_ref[...] * o
    o_ref[...] = xn
    tok = _norm_mod(xn, mg_ref[...], msh_ref[...], msc_ref[...])
    _route_tokens(tok, wr_ref, br_ref, tok_ref, eid_ref, gate_ref, n_groups, n_experts)


def s5_glu(x, y, g, shift, scale, d_skip, w1, b1, w2, b2, gate, moe_g, moe_shift, moe_scale, wr, br, n_groups,
           n_experts):
    bsz, seq, d = x.shape
    tm = min(seq, 512)
    nt = seq // tm
    sub = tm // MOE_TM
    row = lambda a: a.reshape(1, d)
    rspec = pl.BlockSpec((1, d), lambda b, i: (0, 0))
    mspec = pl.BlockSpec((None, 1, d), lambda b, i: (b, 0, 0))
    tile = pl.BlockSpec((None, tm, d), lambda b, i: (b, i, 0))
    wspec = pl.BlockSpec((d, d), lambda b, i: (0, 0), pipeline_mode=pl.Buffered(1))
    return pl.pallas_call(
        functools.partial(_s5_glu_body, n_groups=n_groups, n_experts=n_experts),
        grid=(bsz, nt),
        in_specs=[tile, tile, rspec, mspec, mspec, rspec, wspec, wspec, rspec, rspec, mspec,
                  rspec, mspec, mspec,
                  pl.BlockSpec((2, d, LANES), lambda b, i: (0, 0, 0)),
                  pl.BlockSpec((1, LANES), lambda b, i: (0, 0))],
        out_specs=[tile,
                   pl.BlockSpec((tm, d), lambda b, i: (b * nt + i, 0)),
                   pl.BlockSpec((sub, SUBLANES, MOE_TM), lambda b, i: (b * nt + i, 0, 0)),
                   pl.BlockSpec((tm, LANES), lambda b, i: (b * nt + i, 0))],
        out_shape=[jax.ShapeDtypeStruct((bsz, seq, d), F32),
                   jax.ShapeDtypeStruct((bsz * seq, d), F32),
                   jax.ShapeDtypeStruct((bsz * seq // MOE_TM, SUBLANES, MOE_TM), jnp.int32),
                   jax.ShapeDtypeStruct((bsz * seq, LANES), F32)],
        compiler_params=_cparams("parallel", "parallel"),
        name="s5_glu",
    )(x, y, row(g), shift, scale, row(d_skip), w1, w2, row(b1), row(b2), gate,
      row(moe_g), moe_shift, moe_scale, _split_bf16(wr), br)


def s5_mix(xl, hl, hc, g, sh_l, sc_l, gate_l, a_re, a_im, log_step, b_re, b_im, c_re, c_im, d_skip,
           w1, b1, w2, b2, moe_mod, router):
    bsz, seq, d = xl.shape
    xr_c = _s5_arrange(hc)
    xr_l = _s5_arrange(hl)
    rows = bsz * S5_SEGS
    n_ctx = xr_c.shape[0] // rows
    n_lat = xr_l.shape[0] // rows
    ws, wu, wh, lt = _s5_operators(a_re, a_im, log_step, b_re, b_im, c_re, c_im)
    s_ctx, s_lat = s5_inject(xr_c, xr_l, ws)
    h = s5_scan(s_ctx.reshape(n_ctx, rows, -1), s_lat.reshape(n_lat, rows, -1), lt, bsz)
    y = s5_readout(xr_l, h.reshape(2, n_lat * rows, -1), wu, wh)
    y = _s5_unarrange(y, bsz)
    wr, br = _router_weights(*router)
    xl, tok, eid, gates = s5_glu(xl, y, g, sh_l, sc_l, d_skip, w1.astype(BF16), b1, w2.astype(BF16), b2, gate_l,
                                 *moe_mod, wr, br, router[0].shape[1], router[2].shape[1])
    return xl, (tok, eid, gates)


def hyena_mix(x, g, shift, scale, gate, w_in, b_in, conv_w, conv_b, fw1, fb1, fw2, fb2, fw3, freq, skip,
              w_out, b_out):
    seq = x.shape[1]
    if seq >= FFT_MIN_SEQ and (2 * seq) % (2 * FFT_N2) == 0:
        a, dd, _ = hyena_filter_taps(seq, fw1, fb1, fw2, fb2, fw3, freq, F32)
        v, x0 = hyena_in(x, g, shift, scale, w_in, b_in, conv_w, conv_b, F32)
        yg = hyena_conv_fft(v, x0, skip, a, dd)
    else:
        cmat, smat = dft_matrices(seq)
        a, dd, kn = hyena_filter_taps(seq, fw1, fb1, fw2, fb2, fw3, freq, BF16)
        kr, ki = hyena_filter_dft(a, dd, cmat, smat)
        v, x0 = hyena_in(x, g, shift, scale, w_in, b_in, conv_w, conv_b, BF16)
        yg = hyena_conv(v, x0, skip, kr, ki, kn, cmat, smat)
    return mm_residual(yg, w_out, b_out, x, gate)


def kernel(x, c, ctx, c_ctx, ada_w, ada_b, norm_g, final_g, hy_w_in, hy_b_in, hy_conv_w, hy_conv_b, hy_fw1,
           hy_fb1, hy_fw2, hy_fb2, hy_fw3, hy_freq, hy_skip, hy_w_out, hy_b_out, s5_a_re, s5_a_im,
           s5_log_step, s5_b_re, s5_b_im, s5_c_re, s5_c_im, s5_d, s5_w1, s5_b1, s5_w2, s5_b2, moe_wg, moe_bg,
           moe_we, moe_be, moe_w_gate, moe_w_up, moe_w_down):
    bsz, _, d = x.shape
    depth = ada_w.shape[0]
    assert depth == 2 and bsz < SUBLANES
    c_all = jnp.concatenate([c, c_ctx[None, :], jnp.zeros((SUBLANES - bsz - 1, d), F32)], axis=0)
    mods = ada_mod(c_all, ada_w, ada_b)

    def mod_rows(layer, k):
        lat = mods[layer, :bsz, k * d:(k + 1) * d][:, None, :]
        cx = jnp.broadcast_to(mods[layer, bsz, k * d:(k + 1) * d][None, None, :], (bsz, 1, d))
        return lat, cx

    (sh_a, csh_a), (sc_a, csc_a), (gt_a, cgt_a) = mod_rows(0, 0), mod_rows(0, 1), mod_rows(0, 2)
    (sh_f, csh_f), (sc_f, csc_f), (gt_f, cgt_f) = mod_rows(0, 3), mod_rows(0, 4), mod_rows(0, 5)
    hy = (hy_w_in[0].astype(BF16), hy_b_in[0], hy_conv_w[0], hy_conv_b[0], hy_fw1[0], hy_fb1[0], hy_fw2[0],
          hy_fb2[0], hy_fw3[0], hy_freq[0], hy_skip[0], hy_w_out[0].astype(BF16), hy_b_out[0])
    xl = hyena_mix(x, norm_g[0, 0], sh_a, sc_a, gt_a, *hy)
    xc = hyena_mix(ctx, norm_g[0, 0], csh_a, csc_a, cgt_a, *hy)
    (sh_a, csh_a), (sc_a, csc_a), (gt_a, _) = mod_rows(1, 0), mod_rows(1, 1), mod_rows(1, 2)
    (xl, hl), (_, hc) = hier_moe([(xl, sh_f, sc_f, gt_f), (xc, csh_f, csc_f, cgt_f)], norm_g[0, 1],
                                 moe_wg[0], moe_bg[0], moe_we[0], moe_be[0], moe_w_gate, moe_w_up, moe_w_down, 0,
                                 final_g, False,
                                 next_mods=[(norm_g[1, 0], sh_a, sc_a), (norm_g[1, 0], csh_a, csc_a)])
    (sh_f, _), (sc_f, _), (gt_f, _) = mod_rows(1, 3), mod_rows(1, 4), mod_rows(1, 5)
    router = (moe_wg[1], moe_bg[1], moe_we[1], moe_be[1])
    xl, routed = s5_mix(xl, hl, hc, norm_g[1, 0], sh_a, sc_a, gt_a, s5_a_re[0], s5_a_im[0], s5_log_step[0],
                        s5_b_re[0], s5_b_im[0], s5_c_re[0], s5_c_im[0], s5_d[0], s5_w1[0], s5_b1[0], s5_w2[0],
                        s5_b2[0], (norm_g[1, 1], sh_f, sc_f), router)
    (out,) = hier_moe([(xl, sh_f, sc_f, gt_f)], norm_g[1, 1], *router, moe_w_gate, moe_w_up, moe_w_down, 1,
                      final_g, True, routed=routed)
    return out
```

```python
import functools
import math

import numpy as np
import jax
import jax.numpy as jnp
from jax import lax
from jax.experimental import pallas as pl
from jax.experimental.pallas import tpu as pltpu

F32 = jnp.float32
BF16 = jnp.bfloat16
HIGHEST = lax.Precision.HIGHEST

NORM_EPS = 1e-6
HY_DECAY_TARGET = 1e-2
HY_FAST_PCT = 0.3
HY_SLOW_PCT = 1.5
TOP_K = 2

V7X_VMEM_LIMIT_BYTES = 56 * 1024 * 1024
LANES = 128
SUBLANES = 8
S5_TAU = 8
S5_SEGS = 2
S5_SCAN_UNROLL = 8
MOE_TM = 256
MOE_BM = 256
MOE_NCH = 12
MOE_STAGE = 4
MOE_CAST_ROWS = 128
MOE_WIN_ALIGN = 1024
MOE_WIN = 2 * MOE_WIN_ALIGN
NEG_BIG = -1e30


def _cparams(*sem):
    return pltpu.CompilerParams(dimension_semantics=sem, vmem_limit_bytes=V7X_VMEM_LIMIT_BYTES)


def _norm_mod(x, g, shift, scale):
    ms = jnp.mean(x * x, axis=-1, keepdims=True)
    return (x * lax.rsqrt(ms + NORM_EPS) * g) * (1.0 + scale) + shift


def _dot(a, b):
    return jnp.dot(a, b, preferred_element_type=F32)


def _ada_body(c_ref, w_ref, b_ref, o_ref):
    x = c_ref[...]
    s = (x * jax.nn.sigmoid(x)).astype(BF16)
    o_ref[...] = _dot(s, w_ref[...].astype(BF16)) + b_ref[...]


def ada_mod(c_all, ada_w, ada_b):
    depth, d, n = ada_w.shape
    tn = min(n, 1024)
    return pl.pallas_call(
        _ada_body,
        grid=(depth, n // tn),
        in_specs=[pl.BlockSpec((SUBLANES, d), lambda l, j: (0, 0)),
                  pl.BlockSpec((None, d, tn), lambda l, j: (l, 0, j)),
                  pl.BlockSpec((None, 1, tn), lambda l, j: (l, 0, j))],
        out_specs=pl.BlockSpec((None, SUBLANES, tn), lambda l, j: (l, 0, j)),
        out_shape=jax.ShapeDtypeStruct((depth, SUBLANES, n), F32),
        compiler_params=_cparams("parallel", "parallel"),
        name="ada_mod",
    )(c_all, ada_w, ada_b.reshape(depth, 1, n))


def _hy_in_body(xp_ref, xm_ref, xn_ref, g_ref, sh_ref, sc_ref,
                w0_ref, w1_ref, w2_ref, b0_ref, b1_ref, b2_ref,
                cw0_ref, cw1_ref, cw2_ref, cb0_ref, cb1_ref, cb2_ref,
                v_ref, x0_ref):
    i = pl.program_id(2)
    ni = pl.num_programs(2)
    tm = xm_ref.shape[0]
    x = jnp.concatenate([xp_ref[...], xm_ref[...], xn_ref[...]], axis=0)
    h = _norm_mod(x, g_ref[...], sh_ref[...], sc_ref[...]).astype(BF16)
    rows = lax.broadcasted_iota(jnp.int32, (tm + 2 * SUBLANES, 1), 0)
    valid = jnp.logical_and(jnp.logical_or(rows >= SUBLANES, i > 0),
                            jnp.logical_or(rows < tm + SUBLANES, i < ni - 1))

    def part(w_ref, b_ref, cw_ref, cb_ref):
        z = jnp.where(valid, _dot(h, w_ref[...]) + b_ref[...], 0.0)
        cw = cw_ref[...]
        zp = pltpu.roll(z, 1, axis=0)[SUBLANES:tm + SUBLANES]
        zn = pltpu.roll(z, tm + 2 * SUBLANES - 1, axis=0)[SUBLANES:tm + SUBLANES]
        return zp * cw[0:1] + z[SUBLANES:tm + SUBLANES] * cw[1:2] + zn * cw[2:3] + cb_ref[...]

    x0 = part(w0_ref, b0_ref, cw0_ref, cb0_ref)
    x1 = part(w1_ref, b1_ref, cw1_ref, cb1_ref)
    v = part(w2_ref, b2_ref, cw2_ref, cb2_ref) * x1
    v_ref[...] = v.astype(v_ref.dtype)
    x0_ref[...] = x0.astype(BF16)


def hyena_in(x, g, shift, scale, w_in, b_in, conv_w, conv_b, v_dtype):
    bsz, seq, d = x.shape
    tm = min(seq, 512)
    tn = min(d, 1024)
    nj = d // tn
    r8 = tm // SUBLANES
    last8 = seq // SUBLANES - 1
    row = lambda a: a.reshape(1, -1)
    wspec = lambda k: pl.BlockSpec((d, tn), lambda j, b, i: (0, k * nj + j))
    rspec = lambda k: pl.BlockSpec((1, tn), lambda j, b, i: (0, k * nj + j))
    cspec = lambda k: pl.BlockSpec((3, tn), lambda j, b, i: (0, k * nj + j))
    mspec = pl.BlockSpec((None, 1, d), lambda j, b, i: (b, 0, 0))
    out_spec = pl.BlockSpec((None, tm, tn), lambda j, b, i: (b, i, j))
    return pl.pallas_call(
        _hy_in_body,
        grid=(nj, bsz, seq // tm),
        in_specs=[pl.BlockSpec((None, SUBLANES, d), lambda j, b, i: (b, jnp.maximum(i * r8 - 1, 0), 0)),
                  pl.BlockSpec((None, tm, d), lambda j, b, i: (b, i, 0)),
                  pl.BlockSpec((None, SUBLANES, d), lambda j, b, i: (b, jnp.minimum((i + 1) * r8, last8), 0)),
                  pl.BlockSpec((1, d), lambda j, b, i: (0, 0)), mspec, mspec,
                  wspec(0), wspec(1), wspec(2), rspec(0), rspec(1), rspec(2),
                  cspec(0), cspec(1), cspec(2), rspec(0), rspec(1), rspec(2)],
        out_specs=[out_spec, out_spec],
        out_shape=[jax.ShapeDtypeStruct((bsz, seq, d), v_dtype), jax.ShapeDtypeStruct((bsz, seq, d), BF16)],
        compiler_params=_cparams("parallel", "parallel", "parallel"),
        name="hyena_in",
    )(x, x, x, row(g), shift, scale, w_in, w_in, w_in, row(b_in), row(b_in), row(b_in),
      conv_w, conv_w, conv_w, row(conv_b), row(conv_b), row(conv_b))


def _dft_tables(seq, blk):
    n = 2 * seq
    s = np.arange(seq, dtype=np.int64)[None, :]
    fl = np.arange(blk, dtype=np.int64)[:, None]
    fh = (np.arange(seq // blk, dtype=np.int64) * blk)[:, None]
    w = 2.0 * math.pi / n
    ang_b = ((fl * s) % n) * w
    ang_a = ((fh * s) % n) * w
    f32 = lambda m: jnp.asarray(m.astype(np.float32))
    return (f32(np.cos(ang_a)[:, None, :]), f32(np.sin(ang_a)[:, None, :]), f32(np.cos(ang_b)), f32(np.sin(ang_b)))


def _dft_gen_body(ca_ref, sa_ref, cb_ref, sb_ref, c_ref, s_ref):
    ca, sa, cb, sb = ca_ref[...], sa_ref[...], cb_ref[...], sb_ref[...]
    c_ref[...] = (ca * cb - sa * sb).astype(BF16)
    s_ref[...] = (sa * cb + ca * sb).astype(BF16)


def dft_matrices(seq):
    blk = min(seq, 256)
    ca, sa, cb, sb = _dft_tables(seq, blk)
    aspec = pl.BlockSpec((None, 1, seq), lambda i: (i, 0, 0))
    bspec = pl.BlockSpec((blk, seq), lambda i: (0, 0))
    ospec = pl.BlockSpec((blk, seq), lambda i: (i, 0))
    return pl.pallas_call(
        _dft_gen_body,
        grid=(seq // blk,),
        in_specs=[aspec, aspec, bspec, bspec],
        out_specs=[ospec, ospec],
        out_shape=[jax.ShapeDtypeStruct((seq, seq), BF16)] * 2,
        compiler_params=_cparams("parallel"),
        name="dft_matrices",
    )(ca, sa, cb, sb)


def _alt_sign(rows):
    return jnp.where((rows & 1) == 0, 1.0, -1.0).astype(F32)


def _filt_body(h2_ref, wf_ref, wb_ref, dl_ref, a_ref, d_ref, ny_ref):
    seq = h2_ref.shape[0]
    h2 = h2_ref[...]
    row = lax.broadcasted_iota(jnp.int32, (seq, 1), 0)
    t = row.astype(F32) * (1.0 / (seq - 1))
    win = jnp.exp(-t * dl_ref[...])
    h_hi = h2.astype(BF16)
    h_lo = (h2 - h_hi.astype(F32)).astype(BF16)
    dot3 = lambda w_ref: _dot(h_hi, w_ref[0]) + _dot(h_hi, w_ref[1]) + _dot(h_lo, w_ref[0])
    hf = dot3(wf_ref) * win
    hb = dot3(wb_ref) * win
    hb = jnp.where(row == 0, 0.0, hb)
    nrm = (jnp.sum(jnp.abs(hf), axis=0, keepdims=True) + jnp.sum(jnp.abs(hb), axis=0, keepdims=True))
    inv = 1.0 / nrm
    a = (hf + hb) * inv
    a_ref[...] = a.astype(a_ref.dtype)
    d_ref[...] = ((hb - hf) * inv).astype(d_ref.dtype)
    ny = jnp.sum(a * _alt_sign(row), axis=0, keepdims=True) * (1.0 / (2 * seq))
    ny_ref[...] = jnp.broadcast_to(ny, ny_ref.shape)


def _khat_body(a_ref, d_ref, c_ref, s_ref, kr_ref, ki_ref):
    i = pl.program_id(1)
    tm = c_ref.shape[0]
    seq = c_ref.shape[1]
    f = i * tm + lax.broadcasted_iota(jnp.int32, (tm, 1), 0)
    w = jnp.where(f == 0, 1.0, 2.0).astype(F32) * (1.0 / (2 * seq))
    kr_ref[...] = _dot(c_ref[...], a_ref[...]) * w
    ki_ref[...] = _dot(s_ref[...], d_ref[...]) * w


def hyena_filter_taps(seq, fw1, fb1, fw2, fb2, fw3, freq, taps_dtype):
    d = fw3.shape[1] // 2
    bands_n = (fw1.shape[0] - 1) // 2
    t = np.linspace(0.0, 1.0, seq)[:, None]
    w = (2.0 * math.pi / seq) * np.arange(seq)[:, None]
    bands = np.linspace(1e-4, bands_n - 1, bands_n)[None, :]
    z = jnp.asarray(np.concatenate([t, np.cos(bands * w), -np.sin(bands * w)], axis=-1).astype(np.float32))
    h = jnp.sin(freq * (jnp.dot(z, fw1, precision=HIGHEST) + fb1))
    h2 = jnp.sin(freq * (jnp.dot(h, fw2, precision=HIGHEST) + fb2))
    max_decay = math.log(HY_DECAY_TARGET) / HY_FAST_PCT
    min_decay = math.log(HY_DECAY_TARGET) / HY_SLOW_PCT
    deltas = jnp.abs(jnp.linspace(min_decay, max_decay, d, dtype=F32))[None, :]

    order = h2.shape[1]
    fw3_split = _split_bf16(fw3)
    tn = min(d, 256)
    nj = d // tn
    return pl.pallas_call(
        _filt_body,
        grid=(nj,),
        in_specs=[pl.BlockSpec((seq, order), lambda j: (0, 0)),
                  pl.BlockSpec((2, order, tn), lambda j: (0, 0, j)),
                  pl.BlockSpec((2, order, tn), lambda j: (0, 0, nj + j)),
                  pl.BlockSpec((1, tn), lambda j: (0, j))],
        out_specs=[pl.BlockSpec((seq, tn), lambda j: (0, j)),
                   pl.BlockSpec((seq, tn), lambda j: (0, j)),
                   pl.BlockSpec((SUBLANES, tn), lambda j: (0, j))],
        out_shape=[jax.ShapeDtypeStruct((seq, d), taps_dtype), jax.ShapeDtypeStruct((seq, d), taps_dtype),
                   jax.ShapeDtypeStruct((SUBLANES, d), F32)],
        compiler_params=_cparams("parallel"),
        name="hyena_filter_taps",
    )(h2, fw3_split, fw3_split, deltas)


def hyena_filter_dft(a, dd, cmat, smat):
    seq, d = a.shape
    tm = min(seq, 512)
    tn2 = min(d, 512)
    return pl.pallas_call(
        _khat_body,
        grid=(d // tn2, seq // tm),
        in_specs=[pl.BlockSpec((seq, tn2), lambda j, i: (0, j)),
                  pl.BlockSpec((seq, tn2), lambda j, i: (0, j)),
                  pl.BlockSpec((tm, seq), lambda j, i: (i, 0)),
                  pl.BlockSpec((tm, seq), lambda j, i: (i, 0))],
        out_specs=[pl.BlockSpec((tm, tn2), lambda j, i: (i, j))] * 2,
        out_shape=[jax.ShapeDtypeStruct((seq, d), F32)] * 2,
        compiler_params=_cparams("parallel", "parallel"),
        name="hyena_filter_dft",
    )(a, dd, cmat, smat)


def _dft_fwd_body(v_ref, c_ref, s_ref, kr_ref, ki_ref, kn_ref, ya_ref, yb_ref, yn_ref):
    i = pl.program_id(2)
    v = v_ref[...]
    vr = _dot(c_ref[...], v)
    p = _dot(s_ref[...], v)
    kr = kr_ref[...]
    ki = ki_ref[...]
    ya_ref[...] = (vr * kr + p * ki).astype(BF16)
    yb_ref[...] = (p * kr - vr * ki).astype(BF16)

    @pl.when(i == 0)
    def _():
        seq = v.shape[0]
        row = lax.broadcasted_iota(jnp.int32, (seq, 1), 0)
        vl = jnp.sum(v.astype(F32) * _alt_sign(row), axis=0, keepdims=True)
        yn_ref[...] = jnp.broadcast_to(vl * kn_ref[0:1, :], yn_ref.shape)


def _dft_inv_body(ya_ref, yb_ref, c_ref, s_ref, v_ref, x0_ref, skip_ref, yn_ref, o_ref):
    i = pl.program_id(2)
    tm = c_ref.shape[0]
    acc = _dot(c_ref[...], ya_ref[...]) + _dot(s_ref[...], yb_ref[...])
    t = i * tm + lax.broadcasted_iota(jnp.int32, (tm, 1), 0)
    y = acc + _alt_sign(t) * yn_ref[0:1, :] + skip_ref[...] * v_ref[...].astype(F32)
    o_ref[...] = (y * x0_ref[...].astype(F32)).astype(BF16)


def hyena_conv(v, x0, skip, kr, ki, kn, cmat, smat):
    bsz, seq, d = v.shape
    tm = min(seq, 512)
    tn = min(d, 512)
    grid = (bsz, d // tn, seq // tm)
    full = pl.BlockSpec((None, seq, tn), lambda b, j, i: (b, 0, j))
    mat = pl.BlockSpec((tm, seq), lambda b, j, i: (i, 0))
    tile = pl.BlockSpec((None, tm, tn), lambda b, j, i: (b, i, j))
    ktile = pl.BlockSpec((tm, tn), lambda b, j, i: (i, j))
    nyq = pl.BlockSpec((None, SUBLANES, tn), lambda b, j, i: (b, 0, j))
    ya, yb, yn = pl.pallas_call(
        _dft_fwd_body,
        grid=grid,
        in_specs=[full, mat, mat, ktile, ktile, pl.BlockSpec((SUBLANES, tn), lambda b, j, i: (0, j))],
        out_specs=[tile, tile, nyq],
        out_shape=[jax.ShapeDtypeStruct((bsz, seq, d), BF16)] * 2
        + [jax.ShapeDtypeStruct((bsz, SUBLANES, d), F32)],
        compiler_params=_cparams("parallel", "parallel", "arbitrary"),
        name="hyena_dft_fwd",
    )(v, cmat, smat, kr, ki, kn)
    return pl.pallas_call(
        _dft_inv_body,
        grid=grid,
        in_specs=[full, full, mat, mat, tile, tile, pl.BlockSpec((1, tn), lambda b, j, i: (0, j)), nyq],
        out_specs=tile,
        out_shape=jax.ShapeDtypeStruct((bsz, seq, d), BF16),
        compiler_params=_cparams("parallel", "parallel", "parallel"),
        name="hyena_dft_inv",
    )(ya, yb, cmat, smat, v, x0, skip.reshape(1, d), yn)


FFT_N2 = 128
FFT_MIN_SEQ = 1024
FFT_UNROLL = 8


def _unroll_for(trips, cap):
    return max(u for u in range(1, cap + 1) if trips % u == 0)


def _fft_matrices(seq):
    n = 2 * seq
    n2 = FFT_N2
    n1 = n // n2
    r8 = SUBLANES
    q = np.arange(n2 // r8, dtype=np.int64)[:, None, None, None]
    nf = n1 // 2 + 1
    f1 = np.arange(nf, dtype=np.int64)[None, :, None, None]
    r = np.arange(r8, dtype=np.int64)[None, None, :, None]
    t1 = np.arange(n1 // 2, dtype=np.int64)[None, None, None, :]
    ang = ((f1 * (t1 * n2 + q * r8 + r)) % n) * (2.0 * math.pi / n)
    g = np.stack([np.cos(ang), -np.sin(ang)], axis=3)
    eye = np.eye(r8)[None, None, :, None, None, :]
    ma = (g[..., None] * eye).reshape(n2 // r8, nf * r8 * 2, (n1 // 2) * r8).astype(np.float32)
    f2 = np.arange(n2, dtype=np.int64)[:, None]
    t2 = np.arange(n2, dtype=np.int64)[None, :]
    th = ((f2 * t2) % n2) * (2.0 * math.pi / n2)
    co, si = np.cos(th), np.sin(th)
    wc = np.stack([np.stack([co, si], axis=-1), np.stack([-si, co], axis=-1)], axis=0)
    wc = wc.reshape(2 * n2, 2 * n2).astype(np.float32)
    as_bf16 = lambda m: jnp.asarray(np.ascontiguousarray(m).astype(BF16))
    return as_bf16(ma), as_bf16(np.swapaxes(ma, 1, 2)), as_bf16(wc), as_bf16(wc.T)


def _fft_stage_a(x_ref, ma_ref, s1):
    n1h, n_q, r8, tn = x_ref.shape
    nf = s1.shape[0]

    def body(q, carry):
        x = x_ref[:, pl.ds(q, 1), :, :].reshape(n1h * r8, tn).astype(BF16)
        a = _dot(ma_ref[q], x).astype(BF16)
        s1[:, pl.ds(pl.multiple_of(q * 2 * r8, 2 * r8), 2 * r8), :] = a.reshape(nf, 2 * r8, tn)
        return carry

    lax.fori_loop(0, n_q, body, 0, unroll=FFT_UNROLL)


def _fft_conv_body(v_ref, x0_ref, k_ref, skip_ref, ma_ref, mat_ref, wc_ref, wci_ref, o_ref, s1, ysc):
    n1h, n_q, r8, tn = v_ref.shape
    nf = s1.shape[0]
    n2 = s1.shape[1] // 2
    seq = n1h * n_q * r8
    _fft_stage_a(v_ref, ma_ref, s1)

    def slab(f, carry):
        y = _dot(wc_ref[...], s1[f])
        yr, yi = y[:n2], y[n2:]
        kr = k_ref[f, 0].astype(F32)
        ki = k_ref[f, 1].astype(F32)
        p = jnp.concatenate([yr * kr - yi * ki, yr * ki + yi * kr], axis=0).astype(BF16)
        s1[f] = _dot(wci_ref[...], p).astype(BF16)
        return carry

    lax.fori_loop(0, nf, slab, 0, unroll=_unroll_for(nf, 2 * FFT_UNROLL))

    def inv_a(q, carry):
        z = s1[:, pl.ds(pl.multiple_of(q * 2 * r8, 2 * r8), 2 * r8), :].reshape(nf * 2 * r8, tn)
        ysc[:, pl.ds(q, 1), :, :] = _dot(mat_ref[q], z).reshape(n1h, 1, r8, tn)
        return carry

    lax.fori_loop(0, n_q, inv_a, 0, unroll=FFT_UNROLL)
    y = ysc[...].reshape(seq, tn) + skip_ref[...] * v_ref[...].reshape(seq, tn)
    o_ref[...] = (y * x0_ref[...].astype(F32)).astype(BF16)


def _fft_filter_body(a_ref, d_ref, ma_ref, wc_ref, k_ref, s1):
    nf = s1.shape[0]
    n2 = s1.shape[1] // 2
    scale = 1.0 / (2 * (nf - 1) * n2)
    for src_ref, part, sign in ((a_ref, 0, scale), (d_ref, 1, -scale)):
        _fft_stage_a(src_ref, ma_ref, s1)

        def slab(f, carry):
            y = _dot(wc_ref[part * n2:(part + 1) * n2, :], s1[f])
            mirrored = jnp.logical_and(f > 0, f < nf - 1)
            k_ref[f, part] = (y * (sign * jnp.where(mirrored, 2.0, 1.0))).astype(BF16)
            return carry

        lax.fori_loop(0, nf, slab, 0, unroll=_unroll_for(nf, FFT_UNROLL))


def hyena_conv_fft(v, x0, skip, a, dd):
    bsz, seq, d = v.shape
    n2 = FFT_N2
    n1 = 2 * seq // n2
    nf = n1 // 2 + 1
    n_q = n2 // SUBLANES
    tn = min(d, 256)
    ma, mat, wc, wci = _fft_matrices(seq)
    const = lambda shape: pl.BlockSpec(shape, lambda *_: (0,) * len(shape), pipeline_mode=pl.Buffered(1))
    view = lambda t: t.reshape(t.shape[:-2] + (n1 // 2, n_q, SUBLANES, d))
    tap = pl.BlockSpec((n1 // 2, n_q, SUBLANES, tn), lambda j: (0, 0, 0, j))
    khat = pl.pallas_call(
        _fft_filter_body,
        grid=(d // tn,),
        in_specs=[tap, tap, const(ma.shape), const(wc.shape)],
        out_specs=pl.BlockSpec((nf, 2, n2, tn), lambda j: (0, 0, 0, j)),
        out_shape=jax.ShapeDtypeStruct((nf, 2, n2, d), BF16),
        scratch_shapes=[pltpu.VMEM((nf, 2 * n2, tn), BF16)],
        compiler_params=_cparams("parallel"),
        name="hyena_filter_fft",
    )(view(a), view(dd), ma, wc)
    return pl.pallas_call(
        _fft_conv_body,
        grid=(d // tn, bsz),
        in_specs=[pl.BlockSpec((None, n1 // 2, n_q, SUBLANES, tn), lambda j, b: (b, 0, 0, 0, j)),
                  pl.BlockSpec((None, seq, tn), lambda j, b: (b, 0, j)),
                  pl.BlockSpec((nf, 2, n2, tn), lambda j, b: (0, 0, 0, j), pipeline_mode=pl.Buffered(1)),
                  pl.BlockSpec((1, tn), lambda j, b: (0, j)),
                  const(ma.shape), const(mat.shape), const(wc.shape), const(wci.shape)],
        out_specs=pl.BlockSpec((None, seq, tn), lambda j, b: (b, 0, j)),
        out_shape=jax.ShapeDtypeStruct((bsz, seq, d), BF16),
        scratch_shapes=[pltpu.VMEM((nf, 2 * n2, tn), BF16), pltpu.VMEM((n1 // 2, n_q, SUBLANES, tn), F32)],
        compiler_params=_cparams("parallel", "arbitrary"),
        name="hyena_conv_fft",
    )(view(v), x0, khat, skip.reshape(1, d), ma, mat, wc, wci)


def _mm_res_body(x_ref, w_ref, b_ref, res_ref, gate_ref, o_ref):
    o_ref[...] = res_ref[...] + gate_ref[...] * (_dot(x_ref[...], w_ref[...]) + b_ref[...])


def mm_residual(x, w, b, res, gate):
    bsz, seq, k = x.shape
    n = w.shape[1]
    tm = min(seq, 512)
    return pl.pallas_call(
        _mm_res_body,
        grid=(bsz, seq // tm),
        in_specs=[pl.BlockSpec((None, tm, k), lambda b, i: (b, i, 0)),
                  pl.BlockSpec((k, n), lambda b, i: (0, 0)),
                  pl.BlockSpec((1, n), lambda b, i: (0, 0)),
                  pl.BlockSpec((None, tm, n), lambda b, i: (b, i, 0)),
                  pl.BlockSpec((None, 1, n), lambda b, i: (b, 0, 0))],
        out_specs=pl.BlockSpec((None, tm, n), lambda b, i: (b, i, 0)),
        out_shape=jax.ShapeDtypeStruct((bsz, seq, n), F32),
        compiler_params=_cparams("parallel", "parallel"),
        name="mm_residual",
    )(x, w, b.reshape(1, n), res, gate)


def _moe_pre_body(*refs, n_groups, n_experts, tile_offs):
    n_streams = len(tile_offs) - 1
    g_ref, wr_ref, br_ref, tok_ref, eid_ref, gate_ref = refs[3 * n_streams:]
    i = pl.program_id(0)
    for k in range(n_streams):
        x_ref, sh_ref, sc_ref = refs[3 * k:3 * k + 3]

        @pl.when(jnp.logical_and(i >= tile_offs[k], i < tile_offs[k + 1]))
        def _():
            tok = _norm_mod(x_ref[...], g_ref[...], sh_ref[...], sc_ref[...])
            _route_tokens(tok, wr_ref, br_ref, tok_ref, eid_ref, gate_ref, n_groups, n_experts)


def _route_tokens(tok, wr_ref, br_ref, tok_ref, eid_ref, gate_ref, n_groups, n_experts):
    tok_ref[...] = tok
    t_hi = tok.astype(BF16)
    t_lo = (tok - t_hi.astype(F32)).astype(BF16)
    logits = (_dot(t_hi, wr_ref[0]) + _dot(t_hi, wr_ref[1]) + _dot(t_lo, wr_ref[0])) + br_ref[...]
    lane = lax.broadcasted_iota(jnp.int32, logits.shape, 1)
    per = n_experts // n_groups
    big = jnp.int32(1 << 20)
    gmask = jnp.logical_and(lane >= n_experts, lane < n_experts + n_groups)
    gl = jnp.where(gmask, logits, NEG_BIG)
    gmax = jnp.max(gl, axis=-1, keepdims=True)
    gidx = jnp.min(jnp.where(gl == gmax, lane - n_experts, big), axis=-1, keepdims=True)
    p_top = 1.0 / jnp.sum(jnp.where(gmask, jnp.exp(gl - gmax), 0.0), axis=-1, keepdims=True)
    lo = gidx * per
    emask = jnp.logical_and(lane >= lo, lane < lo + per)
    el = jnp.where(emask, logits, NEG_BIG)
    m1 = jnp.max(el, axis=-1, keepdims=True)
    i1 = jnp.min(jnp.where(el == m1, lane, big), axis=-1, keepdims=True)
    el2 = jnp.where(lane == i1, NEG_BIG, el)
    m2 = jnp.max(el2, axis=-1, keepdims=True)
    i2 = jnp.min(jnp.where(el2 == m2, lane, big), axis=-1, keepdims=True)
    e21 = jnp.exp(m2 - m1)
    g1 = p_top / (1.0 + e21)
    g2 = g1 * e21
    ids = jnp.where(lane == 0, i1, jnp.where(lane == 1, i2, -1))
    ids_t = ids.T
    for h in range(eid_ref.shape[0]):
        eid_ref[h] = ids_t[0:SUBLANES, h * MOE_TM:(h + 1) * MOE_TM]
    gate_ref[...] = jnp.where(lane == 0, g1, jnp.where(lane == 1, g2, 0.0))


def _split_bf16(w):
    hi = w.astype(BF16)
    return jnp.stack([hi, (w - hi.astype(F32)).astype(BF16)])


def _router_weights(wg, bg, we, be):
    pad = LANES - we.shape[1] - wg.shape[1]
    wr = jnp.pad(jnp.concatenate([we, wg], axis=1), ((0, 0), (0, pad)))
    br = jnp.pad(jnp.concatenate([be, bg]), (0, pad)).reshape(1, LANES)
    return wr, br


def moe_pre(streams, g, wr, br, n_groups, n_experts):
    d = streams[0][0].shape[2]
    tm = MOE_TM
    tile_offs = [0]
    in_specs, args = [], []
    for x, shift, scale in streams:
        bsz, seq, _ = x.shape
        nt = seq // tm
        n_tiles = bsz * nt
        off = tile_offs[-1]
        tile_offs.append(off + n_tiles)

        def tile(i, off=off, n_tiles=n_tiles):
            return jnp.clip(i - off, 0, n_tiles - 1)

        in_specs += [pl.BlockSpec((None, tm, d), lambda i, tile=tile, nt=nt: (tile(i) // nt, tile(i) % nt, 0)),
                     pl.BlockSpec((None, 1, d), lambda i, tile=tile, nt=nt: (tile(i) // nt, 0, 0)),
                     pl.BlockSpec((None, 1, d), lambda i, tile=tile, nt=nt: (tile(i) // nt, 0, 0))]
        args += [x, shift, scale]
    in_specs += [pl.BlockSpec((1, d), lambda i: (0, 0)),
                 pl.BlockSpec((2, d, LANES), lambda i: (0, 0, 0)),
                 pl.BlockSpec((1, LANES), lambda i: (0, 0))]
    args += [g.reshape(1, d), _split_bf16(wr), br]
    total = tile_offs[-1] * tm
    rout = pl.BlockSpec((tm, LANES), lambda i: (i, 0))
    tok, eid, gate = pl.pallas_call(
        functools.partial(_moe_pre_body, n_groups=n_groups, n_experts=n_experts, tile_offs=tuple(tile_offs)),
        grid=(tile_offs[-1],),
        in_specs=in_specs,
        out_specs=[pl.BlockSpec((tm, d), lambda i: (i, 0)),
                   pl.BlockSpec((1, SUBLANES, tm), lambda i: (i, 0, 0)), rout],
        out_shape=[jax.ShapeDtypeStruct((total, d), F32),
                   jax.ShapeDtypeStruct((tile_offs[-1], SUBLANES, tm), jnp.int32),
                   jax.ShapeDtypeStruct((total, LANES), F32)],
        compiler_params=_cparams("parallel"),
        name="moe_pre",
    )(*args)
    return tok, eid, gate, tile_offs[:-1]


def _start_row_gather(row_index, n_rows, src_hbm, dst_vmem, sem):
    def body(g, c):
        r0 = pl.multiple_of(g * SUBLANES, SUBLANES)
        dst_tile = dst_vmem.at[pl.ds(r0, SUBLANES)]
        for k in range(SUBLANES):
            pltpu.make_async_copy(src_hbm.at[pl.ds(row_index(r0 + k), 1)], dst_tile.at[pl.ds(k, 1)], sem).start()
        return c

    lax.fori_loop(0, n_rows // SUBLANES, body, 0, unroll=2)


def _wait_row_gather(n_rows, src_hbm, dst_vmem, sem):
    pltpu.make_async_copy(src_hbm.at[pl.ds(0, n_rows)], dst_vmem, sem).wait()


def _expert_body(bv_ref, rk_ref, pe_ref, tot_ref, p0_ref, ts_hbm, tok_ref, wg_hbm, wu_hbm, wd_hbm, o_ref,
                 xbuf, xsem, win, isem, wcache, stg, wsem, cnt, *, layer):
    i = pl.program_id(0)
    n = pl.num_programs(0)
    slot = i % 2
    cr, cc = stg.shape[1:]
    total = tot_ref[0]
    mats_hbm = (wg_hbm, wu_hbm, wd_hbm)

    @pl.when(i == 0)
    def _():
        cnt[0] = 0
        cnt[1] = 0

    def live(b):
        return jnp.logical_and(b < n, bv_ref[jnp.minimum(b, n - 1)] > 0)

    def window(b):
        s = b % 2
        lo = pl.multiple_of((p0_ref[jnp.minimum(b, n - 1)] // MOE_WIN_ALIGN) * MOE_WIN_ALIGN, MOE_WIN_ALIGN)
        return pltpu.make_async_copy(ts_hbm.at[pl.ds(lo, MOE_WIN)],
                                     win.at[pl.ds(pl.multiple_of(s * MOE_WIN, MOE_WIN), MOE_WIN)], isem.at[s])

    def gather(b):
        s = b % 2
        base = s * MOE_WIN + p0_ref[jnp.minimum(b, n - 1)] % MOE_WIN_ALIGN
        _start_row_gather(lambda r: win[base + r], MOE_BM, tok_ref, xbuf.at[s], xsem.at[s])

    @pl.when(jnp.logical_and(i == 0, live(0)))
    def _():
        window(0).start()
        window(0).wait()
        gather(0)

    @pl.when(jnp.logical_and(i == 0, live(1)))
    def _():
        window(1).start()

    @pl.when(live(i + 1))
    def _():
        window(i + 1).wait()
        gather(i + 1)

    @pl.when(live(i + 2))
    def _():
        window(i + 2).start()

    def chunk_geom(c):
        q = c % MOE_NCH
        m = q // 4
        sub = q % 4
        r0 = jnp.where(m < 2, sub, sub // 2) * cr
        c0 = jnp.where(m < 2, 0, sub % 2) * cc
        return m, pl.multiple_of(r0, cr), pl.multiple_of(c0, cc)

    def issue(c):
        e = pe_ref[c // MOE_NCH]
        m, r0, c0 = chunk_geom(c)
        s = c % MOE_STAGE
        for k, w_hbm in enumerate(mats_hbm):
            @pl.when(m == k)
            def _():
                pltpu.make_async_copy(w_hbm.at[layer, e, pl.ds(r0, cr), pl.ds(c0, cc)], stg.at[s],
                                      wsem.at[s]).start()

    def cast(c):
        s = c % MOE_STAGE
        pltpu.make_async_copy(wg_hbm.at[layer, 0, pl.ds(0, cr), pl.ds(0, cc)], stg.at[s], wsem.at[s]).wait()
        ws = (c // MOE_NCH) % 2
        q = c % MOE_NCH
        step = min(MOE_CAST_ROWS, cr)
        assert cr % step == 0

        def slab(k, carry):
            rows = pl.ds(pl.multiple_of(k * step, step), step)
            wcache[ws, q, rows, :] = stg[s, rows, :].astype(BF16)
            return carry

        lax.fori_loop(0, cr // step, slab, 0)

    valid = bv_ref[i] > 0
    rank = rk_ref[i]
    issued = cnt[0]
    done = cnt[1]
    limit = jnp.minimum(total, MOE_NCH * (rank + 2))
    need = jnp.where(valid, MOE_NCH * (rank + 1), done)

    def fill(issued, done):
        hi = jnp.minimum(limit, done + MOE_STAGE)

        def body(c, carry):
            issue(c)
            return carry

        lax.fori_loop(issued, hi, body, 0)
        return jnp.maximum(issued, hi)

    def cast_and_refill(c, issued):
        cast(c)
        more = issued < jnp.minimum(limit, c + 1 + MOE_STAGE)

        @pl.when(more)
        def _():
            issue(issued)

        return issued + more.astype(jnp.int32)

    issued = fill(issued, done)
    issued = lax.fori_loop(done, need, cast_and_refill, issued)
    done = jnp.maximum(done, need)

    @pl.when(valid)
    def _():
        ws = rank % 2
        _wait_row_gather(MOE_BM, tok_ref, xbuf.at[slot], xsem.at[slot])
        x = xbuf[slot].astype(BF16)
        gate = sum(_dot(x[:, k * cr:(k + 1) * cr], wcache[ws, k]) for k in range(4))
        up = sum(_dot(x[:, k * cr:(k + 1) * cr], wcache[ws, 4 + k]) for k in range(4))
        h = (gate * jax.nn.sigmoid(gate) * up).astype(BF16)
        for half in range(2):
            o_ref[:, half * cc:(half + 1) * cc] = sum(
                _dot(h[:, k * cr:(k + 1) * cr], wcache[ws, 8 + 2 * k + half]) for k in range(2))

    @pl.when(jnp.logical_not(valid))
    def _():
        o_ref[...] = jnp.zeros_like(o_ref)

    fetched = issued
    issued = lax.fori_loop(done, fetched, cast_and_refill, issued)
    done = jnp.maximum(done, fetched)
    last = i == n - 1
    tail = jnp.where(last, issued, done)

    def drain(c, carry):
        cast(c)
        return carry

    lax.fori_loop(done, tail, drain, 0)
    cnt[0] = issued
    cnt[1] = jnp.maximum(done, tail)


def moe_experts(tok, tok_sorted, block_p0, block_valid, block_rank, present, n_chunks, w_gate, w_up, w_down,
                layer):
    d = tok.shape[1]
    n_blocks = block_valid.shape[0]
    n_rows = n_blocks * MOE_BM
    dh = w_gate.shape[3]
    assert 2 * dh == d and MOE_NCH == 12
    cr, cc = d // 4, dh
    any_spec = pl.BlockSpec(memory_space=pl.ANY)
    grid_spec = pltpu.PrefetchScalarGridSpec(
        num_scalar_prefetch=5,
        grid=(n_blocks,),
        in_specs=[any_spec, any_spec, any_spec, any_spec, any_spec],
        out_specs=pl.BlockSpec((MOE_BM, d), lambda i, *_: (i, 0)),
        scratch_shapes=[pltpu.VMEM((2, MOE_BM, d), F32), pltpu.SemaphoreType.DMA((2,)),
                        pltpu.SMEM((2 * MOE_WIN,), jnp.int32), pltpu.SemaphoreType.DMA((2,)),
                        pltpu.VMEM((2, MOE_NCH, cr, cc), BF16),
                        pltpu.VMEM((MOE_STAGE, cr, cc), F32), pltpu.SemaphoreType.DMA((MOE_STAGE,)),
                        pltpu.SMEM((2,), jnp.int32)],
    )
    return pl.pallas_call(
        functools.partial(_expert_body, layer=layer),
        grid_spec=grid_spec,
        out_shape=jax.ShapeDtypeStruct((n_rows, d), F32),
        compiler_params=_cparams("arbitrary"),
        name="moe_experts",
    )(block_valid, block_rank, present, n_chunks, block_p0, tok_sorted, tok, w_gate, w_up, w_down)


def _combine_body(dest_ref, nxt_ref, os_ref, gate_ref, res_ref, gt_ref, fg_ref, *rest, final_norm, with_next):
    if with_next:
        ng_ref, nsh_ref, nsc_ref, o_ref, h_ref, buf, sem = rest
    else:
        o_ref, buf, sem = rest
    rows = res_ref.shape[0]
    i = pl.program_id(0)
    n = pl.num_programs(0)
    slot = i % 2

    def start(idx_ref, s):
        for k in range(TOP_K):
            _start_row_gather(lambda r, k=k: idx_ref[k * rows + r], rows, os_ref, buf.at[s, k], sem.at[s])

    @pl.when(i == 0)
    def _():
        start(dest_ref, 0)

    @pl.when(i + 1 < n)
    def _():
        start(nxt_ref, 1 - slot)

    for k in range(TOP_K):
        _wait_row_gather(rows, os_ref, buf.at[slot, k], sem.at[slot])
    gates = gate_ref[...]
    mo = gates[:, 0:1] * buf[slot, 0] + gates[:, 1:2] * buf[slot, 1]
    y = res_ref[...] + gt_ref[...] * mo
    if final_norm:
        ms = jnp.mean(y * y, axis=-1, keepdims=True)
        y = y * lax.rsqrt(ms + NORM_EPS) * fg_ref[...]
    o_ref[...] = y
    if with_next:
        h_ref[...] = _norm_mod(y, ng_ref[...], nsh_ref[...], nsc_ref[...]).astype(h_ref.dtype)


def moe_combine(os, dest, gates, tile0, res, gt, final_g, final_norm, next_mod=None):
    bsz, seq, d = res.shape
    rows = MOE_TM
    nt = seq // rows
    n = bsz * nt
    tile = pl.BlockSpec((None, rows, d), lambda i: (i // nt, i % nt, 0))
    mspec = pl.BlockSpec((None, 1, d), lambda i: (i // nt, 0, 0))
    rspec = pl.BlockSpec((1, d), lambda i: (0, 0))
    dspec = lambda step: pl.BlockSpec((SUBLANES * rows,), lambda i: (tile0 + step(i),), memory_space=pltpu.SMEM)
    in_specs = [dspec(lambda i: i), dspec(lambda i: jnp.minimum(i + 1, n - 1)),
                pl.BlockSpec(memory_space=pl.ANY),
                pl.BlockSpec((rows, LANES), lambda i: (tile0 + i, 0)),
                tile, mspec, rspec]
    args = [dest, dest, os, gates, res, gt, final_g.reshape(1, d)]
    out_specs, out_shape = [tile], [jax.ShapeDtypeStruct((bsz, seq, d), F32)]
    if next_mod is not None:
        in_specs += [rspec, mspec, mspec]
        args += [next_mod[0].reshape(1, d), next_mod[1], next_mod[2]]
        out_specs.append(tile)
        out_shape.append(jax.ShapeDtypeStruct((bsz, seq, d), BF16))
    outs = pl.pallas_call(
        functools.partial(_combine_body, final_norm=final_norm, with_next=next_mod is not None),
        grid=(n,),
        in_specs=in_specs,
        out_specs=out_specs,
        out_shape=out_shape,
        scratch_shapes=[pltpu.VMEM((2, TOP_K, rows, d), F32), pltpu.SemaphoreType.DMA((2,))],
        compiler_params=_cparams("arbitrary"),
        name="moe_combine",
    )(*args)
    return outs if next_mod is not None else outs[0]


def _plan_body(e_ref, dest_ref, tab_ref, blk_ref, present_ref, *, n_experts):
    n_rows, w = e_ref.shape
    e_all = e_ref[...]
    li = lax.broadcasted_iota(jnp.int32, (w, w), 0)
    lj = lax.broadcasted_iota(jnp.int32, (w, w), 1)
    incl = (li <= lj).astype(BF16)
    ri = lax.broadcasted_iota(jnp.int32, (n_rows, n_rows), 0)
    rj = lax.broadcasted_iota(jnp.int32, (n_rows, n_rows), 1)
    before = (rj < ri).astype(BF16)
    elane = lax.broadcasted_iota(jnp.int32, (n_rows, LANES), 1)
    row_tot = jnp.zeros((n_rows, LANES), F32)
    for e in range(n_experts):
        tot = jnp.sum((e_all == e).astype(F32), axis=1, keepdims=True)
        row_tot = row_tot + jnp.where(elane == e, tot, 0.0)
    rows_before = _dot(before, row_tot.astype(BF16))
    counts = jnp.sum(row_tot, axis=0, keepdims=True).astype(jnp.int32)
    lane1 = lax.broadcasted_iota(jnp.int32, (1, LANES), 1)

    def excl_prefix(v):
        acc = v
        sh = 1
        while sh < LANES:
            acc = acc + jnp.where(lane1 >= sh, pltpu.roll(acc, sh, axis=1), 0)
            sh *= 2
        return acc - v

    start = excl_prefix(counts)
    padded = (counts + (MOE_BM - 1)) // MOE_BM * MOE_BM
    pad_start = excl_prefix(padded)
    pad_end = pad_start + padded
    has = (counts > 0).astype(jnp.int32)
    rank = excl_prefix(has)
    n_chunks = MOE_NCH * jnp.sum(has.astype(F32), axis=1, keepdims=True).astype(jnp.int32)
    tab_ref[...] = jnp.concatenate([counts, start, pad_start, pad_end, jnp.broadcast_to(n_chunks, (1, LANES)),
                                    jnp.zeros((SUBLANES - 5, LANES), jnp.int32)], axis=0)
    nb = blk_ref.shape[0]
    f32 = lambda v: v.astype(F32)
    lane_b = lax.broadcasted_iota(jnp.int32, (nb, LANES), 1)
    first_row = lax.broadcasted_iota(jnp.int32, (nb, 1), 0) * MOE_BM
    is_expert = lane_b < n_experts
    expert = jnp.sum(f32(jnp.logical_and(is_expert, pad_end <= first_row)), axis=1, keepdims=True)
    expert = jnp.minimum(expert.astype(jnp.int32), n_experts - 1)
    mine = lane_b == expert
    pick = lambda v: jnp.sum(jnp.where(mine, f32(v), 0.0), axis=1, keepdims=True).astype(jnp.int32)
    live = first_row < jnp.max(f32(pad_end), axis=1, keepdims=True).astype(jnp.int32)
    p0 = jnp.where(live, pick(start) - pick(pad_start) + first_row, 0)
    blk_ref[...] = jnp.where(lane_b == 0, live.astype(jnp.int32),
                             jnp.where(lane_b == 1, pick(rank), jnp.where(lane_b == 2, p0, 0)))
    row_r = lax.broadcasted_iota(jnp.int32, (LANES, LANES), 0)
    lane_e = lax.broadcasted_iota(jnp.int32, (LANES, LANES), 1)
    hit_r = jnp.logical_and(has > 0, rank == row_r)
    present = jnp.sum(jnp.where(hit_r, f32(lane_e), 0.0), axis=1, keepdims=True).astype(jnp.int32)
    present_ref[...] = jnp.broadcast_to(present, (LANES, LANES))
    base = rows_before + pad_start.astype(F32)
    dest = jnp.zeros((n_rows, w), F32)
    for e in range(n_experts):
        hit = e_all == e
        within = _dot(hit.astype(BF16), incl)
        dest = dest + jnp.where(hit, within - 1.0 + base[:, e:e + 1], 0.0)
    dest_ref[...] = dest.astype(jnp.int32)


def _route_plan(eid, n_experts):
    n_tiles, r8, tm = eid.shape
    a = n_tiles * TOP_K * tm
    n_blocks = -(-a // MOE_BM) + n_experts
    assert (tm & (tm - 1)) == 0 and TOP_K == 2
    nb_pad = -(-n_blocks // SUBLANES) * SUBLANES
    dest, tab, blk, present = pl.pallas_call(
        functools.partial(_plan_body, n_experts=n_experts),
        out_shape=[jax.ShapeDtypeStruct((n_tiles * r8, tm), jnp.int32),
                   jax.ShapeDtypeStruct((SUBLANES, LANES), jnp.int32),
                   jax.ShapeDtypeStruct((nb_pad, LANES), jnp.int32),
                   jax.ShapeDtypeStruct((LANES, LANES), jnp.int32)],
        compiler_params=pltpu.CompilerParams(vmem_limit_bytes=V7X_VMEM_LIMIT_BYTES),
        name="moe_route_plan",
    )(eid.reshape(n_tiles * r8, tm))
    e_flat = eid[:, :TOP_K, :].reshape(-1)
    order = jnp.argsort(e_flat).astype(jnp.int32)
    shift = tm.bit_length() - 1
    tok_of = ((order >> (shift + 1)) << shift) | (order & (tm - 1))
    slack = -a % MOE_WIN_ALIGN + MOE_WIN
    tok_sorted = jnp.concatenate([tok_of, jnp.zeros((slack,), jnp.int32)])
    blk = blk[:n_blocks]
    return dest.reshape(-1), tok_sorted, blk[:, 2], blk[:, 0], blk[:, 1], present[:n_experts, 0], tab[4, :1]


def hier_moe(streams, norm_g, wg, bg, we, be, w_gate, w_up, w_down, layer, final_g, final_norm, next_mods=None,
             routed=None):
    n_groups = wg.shape[1]
    n_experts = we.shape[1]
    if routed is None:
        wr, br = _router_weights(wg, bg, we, be)
        tok, eid, gates, tile0s = moe_pre([s[:3] for s in streams], norm_g, wr, br, n_groups, n_experts)
    else:
        (tok, eid, gates), tile0s = routed, [0]
    dest, tok_sorted, block_p0, block_valid, block_rank, present, n_chunks = _route_plan(eid, n_experts)
    os = moe_experts(tok, tok_sorted, block_p0, block_valid, block_rank, present, n_chunks, w_gate, w_up, w_down,
                     layer)
    next_mods = next_mods or [None] * len(streams)
    return [moe_combine(os, dest, gates, tile0, x, gt, final_g, final_norm, nm)
            for (x, _, _, gt), tile0, nm in zip(streams, tile0s, next_mods)]


def _s5_arrange(h):
    bsz, t, d = h.shape
    c = t // (S5_SEGS * S5_TAU)
    h = h.reshape(bsz, S5_SEGS, c, S5_TAU, d // LANES, LANES)
    return h.transpose(2, 0, 1, 4, 3, 5).reshape(c * bsz * S5_SEGS, d * S5_TAU)


def _s5_unarrange(y, bsz):
    r, w = y.shape
    d = w // S5_TAU
    c = r // (bsz * S5_SEGS)
    y = y.reshape(c, bsz, S5_SEGS, d // LANES, S5_TAU, LANES)
    return y.transpose(1, 2, 0, 4, 3, 5).reshape(bsz, S5_SEGS * c * S5_TAU, d)


def _s5_operators(a_re, a_im, log_step, b_re, b_im, c_re, c_im):
    n_g, n_p = a_re.shape[1:]
    n_h = b_re.shape[-1]
    gpt = LANES // n_h
    n_j = n_g // gpt
    tau = S5_TAU
    assert tau * n_h == LANES and 2 * n_p == LANES
    lam_step = lax.complex(a_re, a_im) * jnp.exp(log_step)[..., None]
    lam_bar = jnp.exp(lam_step)
    b_bar = ((lam_bar - 1.0) / lax.complex(a_re, a_im))[..., None] * lax.complex(b_re, b_im)
    c_mat = lax.complex(c_re, c_im)
    ks = jnp.arange(tau + 1, dtype=F32)[None, :, None, None]
    pw = jnp.exp(lam_step[:, None] * ks)
    ein = functools.partial(jnp.einsum, precision=HIGHEST)
    inj_c, cl_c, lt = [], [], []
    tz_c = 0.0
    for d in range(2):
        pos = np.arange(tau) if d == 0 else np.arange(tau)[::-1]
        powers = lambda ks: jnp.exp(lam_step[d][None] * jnp.asarray(ks, F32)[:, None, None])
        inj = (powers(tau - 1 - pos)[..., None] * b_bar[d][None]).reshape(tau, n_j, gpt, n_p, n_h)
        inj = inj.transpose(1, 0, 2, 4, 3).reshape(n_j, tau * LANES, n_p)
        inj_c.append(jnp.concatenate([inj.real, inj.imag], axis=-1))
        cl = (c_mat[d][None] * powers(pos + 1)[:, :, None, :]).reshape(tau, n_j, gpt, n_h, n_p)
        cl = cl.transpose(1, 2, 4, 0, 3).reshape(n_j, gpt * n_p, tau * n_h)
        cl_c.append(jnp.concatenate([cl.real, -cl.imag], axis=1))
        mk = ein('gop,kgp,gph->kgoh', c_mat[d], pw[d][:tau], b_bar[d]).real
        diff = pos[:, None] - pos[None, :]
        tz = jnp.where((diff >= 0)[:, :, None, None, None], mk[jnp.clip(diff, 0, tau - 1)], 0.0)
        tz = tz.reshape(tau, tau, n_j, gpt, n_h, n_h)
        tz_c = tz_c + tz.transpose(2, 1, 3, 5, 0, 4).reshape(n_j, tau * LANES, tau * n_h)
        lt_d = pw[d][tau].reshape(n_j, 1, gpt * n_p)
        lt.append(jnp.concatenate([lt_d.real, lt_d.imag], axis=-1))
    ws, wu, wh = s5_expand(jnp.stack(inj_c).astype(BF16), jnp.stack(cl_c).astype(BF16), tz_c.astype(BF16),
                           n_h, n_p)
    return ws, wu, wh, jnp.stack(lt).astype(F32)


def _s5_expand_body(inj_ref, cl_ref, tz_ref, ws_ref, wu_ref, wh_ref, *, n_h, n_p):
    rows = tz_ref.shape[0]
    gpt = LANES // n_h
    row = lax.broadcasted_iota(jnp.int32, (rows, LANES), 0)
    lane = lax.broadcasted_iota(jnp.int32, (rows, LANES), 1)
    sel_r = lax.broadcasted_iota(jnp.int32, (LANES, LANES), 0)
    sel_l = lax.broadcasted_iota(jnp.int32, (LANES, LANES), 1)
    grp_in = (row // n_h) % gpt
    grp_st = (row // n_p) % gpt

    def spread_out(m, t, grp_row):
        sel = jnp.logical_and(sel_r // n_h == t, sel_r % n_h == sel_l % n_h).astype(BF16)
        return jnp.where(grp_row == lane // n_h, _dot(m, sel), 0.0).astype(BF16)

    def spread_state(m, c, q, grp_row):
        sel = jnp.logical_and(sel_r // n_p == c, sel_r % n_p == sel_l % n_p).astype(BF16)
        return jnp.where(grp_row == (LANES // n_p) * q + lane // n_p, _dot(m, sel), 0.0).astype(BF16)

    w2 = 2 * gpt * n_p
    tz = tz_ref[...]
    for t in range(S5_TAU):
        wu_ref[:, t * LANES:(t + 1) * LANES] = spread_out(tz, t, grp_in)
    for d in range(2):
        cl = cl_ref[d]
        inj = inj_ref[d]
        for t in range(S5_TAU):
            wh_ref[d, :, t * LANES:(t + 1) * LANES] = spread_out(cl, t, grp_st)
        for c in range(2):
            for q in range(gpt * n_p // LANES):
                lo = d * w2 + c * gpt * n_p + q * LANES
                ws_ref[:, lo:lo + LANES] = spread_state(inj, c, q, grp_in)


def s5_expand(inj_c, cl_c, tz_c, n_h, n_p):
    n_j, rows, _ = tz_c.shape
    gpt = LANES // n_h
    w2 = 2 * gpt * n_p
    assert rows == S5_TAU * LANES == w2
    cspec = pl.BlockSpec((2, None, rows, LANES), lambda j: (0, j, 0, 0))
    return pl.pallas_call(
        functools.partial(_s5_expand_body, n_h=n_h, n_p=n_p),
        grid=(n_j,),
        in_specs=[cspec, cspec, pl.BlockSpec((None, rows, LANES), lambda j: (j, 0, 0))],
        out_specs=[pl.BlockSpec((None, rows, 2 * w2), lambda j: (j, 0, 0)),
                   pl.BlockSpec((None, rows, rows), lambda j: (j, 0, 0)),
                   pl.BlockSpec((2, None, w2, rows), lambda j: (0, j, 0, 0))],
        out_shape=[jax.ShapeDtypeStruct((n_j, rows, 2 * w2), BF16),
                   jax.ShapeDtypeStruct((n_j, rows, rows), BF16),
                   jax.ShapeDtypeStruct((2, n_j, w2, rows), BF16)],
        compiler_params=_cparams("parallel"),
        name="s5_expand",
    )(inj_c, cl_c, tz_c)


def _s5_inj_body(xc_ref, xl_ref, w_ref, oc_ref, ol_ref):
    w = w_ref[...]
    ol_ref[...] = _dot(xl_ref[...], w)

    @pl.when(pl.program_id(1) == 0)
    def _():
        oc_ref[...] = _dot(xc_ref[...], w)


def s5_inject(xr_c, xr_l, ws):
    r_c, r_l = xr_c.shape[0], xr_l.shape[0]
    n_j, k, n = ws.shape
    tm = r_l // 2 if r_l % 32 == 0 else r_l
    return pl.pallas_call(
        _s5_inj_body,
        grid=(n_j, r_l // tm),
        in_specs=[pl.BlockSpec((r_c, k), lambda j, i: (0, j)),
                  pl.BlockSpec((tm, k), lambda j, i: (i, j)),
                  pl.BlockSpec((None, k, n), lambda j, i: (j, 0, 0))],
        out_specs=[pl.BlockSpec((r_c, n), lambda j, i: (0, j)),
                   pl.BlockSpec((tm, n), lambda j, i: (i, j))],
        out_shape=[jax.ShapeDtypeStruct((r_c, n_j * n), F32), jax.ShapeDtypeStruct((r_l, n_j * n), F32)],
        compiler_params=_cparams("parallel", "arbitrary"),
        name="s5_inject",
    )(xr_c, xr_l, ws)


def _cmul(ar, ai, br, bi):
    return ar * br - ai * bi, ar * bi + ai * br


def _s5_scan_body(sc_ref, sl_ref, lt_ref, h_ref, raw_ref, *, n_ctx, n_lat, bsz):
    d = pl.program_id(1)
    w2 = lt_ref.shape[-1]
    w = w2 // 2
    rows = bsz * S5_SEGS
    seg = lax.broadcasted_iota(jnp.int32, (rows, 1), 0) % S5_SEGS
    is_late = seg != d
    lam_r = lt_ref[:, 0:w]
    lam_i = lt_ref[:, w:w2]
    zero = jnp.zeros((rows, w), F32)
    one = (jnp.ones((1, w), F32), jnp.zeros((1, w), F32))

    def swap_segments(x):
        return jnp.where(seg == 0, pltpu.roll(x, rows - 1, axis=0), pltpu.roll(x, 1, axis=0))

    def phase(s_ref, n_steps, hin_r, hin_i, write):
        def chunk(k):
            return jnp.where(d == 0, k, n_steps - 1 - k)

        def step_raw(k, carry):
            hr, hi = carry
            c = chunk(k)
            raw_ref[c, :, 0:w] = hr
            raw_ref[c, :, w:w2] = hi
            nr, ni = _cmul(lam_r, lam_i, hr, hi)
            return nr + s_ref[c, :, 0:w], ni + s_ref[c, :, w:w2]

        er, ei = lax.fori_loop(0, n_steps, step_raw, (zero, zero), unroll=S5_SCAN_UNROLL)
        pr, pi = lax.fori_loop(0, n_steps, lambda k, q: _cmul(lam_r, lam_i, *q), one)
        dr, di = _cmul(pr, pi, hin_r, hin_i)
        first_r = jnp.where(is_late, 0.0, er + dr)
        first_i = jnp.where(is_late, 0.0, ei + di)
        carry_r = jnp.where(is_late, swap_segments(first_r), hin_r)
        carry_i = jnp.where(is_late, swap_segments(first_i), hin_i)
        if write:
            def step_fix(k, q):
                c = chunk(k)
                fr, fi = _cmul(q[0], q[1], carry_r, carry_i)
                h_ref[c, :, 0:w] = (raw_ref[c, :, 0:w] + fr).astype(h_ref.dtype)
                h_ref[c, :, w:w2] = (raw_ref[c, :, w:w2] + fi).astype(h_ref.dtype)
                return _cmul(lam_r, lam_i, q[0], q[1])

            lax.fori_loop(0, n_steps, step_fix, one, unroll=S5_SCAN_UNROLL)
        lr, li = _cmul(pr, pi, carry_r, carry_i)
        last_r = jnp.where(is_late, er + lr, 0.0)
        last_i = jnp.where(is_late, ei + li, 0.0)
        return (jnp.where(is_late, 0.0, swap_segments(last_r)), jnp.where(is_late, 0.0, swap_segments(last_i)))

    hr, hi = phase(sc_ref, n_ctx, zero, zero, False)
    phase(sl_ref, n_lat, hr, hi, True)


def s5_scan(s_ctx, s_lat, lt, bsz):
    assert S5_SEGS == 2
    n_ctx, rows, _ = s_ctx.shape
    n_lat = s_lat.shape[0]
    n_j = lt.shape[1]
    w2 = lt.shape[-1]
    return pl.pallas_call(
        functools.partial(_s5_scan_body, n_ctx=n_ctx, n_lat=n_lat, bsz=bsz),
        grid=(n_j, 2),
        in_specs=[pl.BlockSpec((n_ctx, rows, w2), lambda j, d: (0, 0, 2 * j + d)),
                  pl.BlockSpec((n_lat, rows, w2), lambda j, d: (0, 0, 2 * j + d)),
                  pl.BlockSpec((None, None, 1, w2), lambda j, d: (d, j, 0, 0))],
        out_specs=pl.BlockSpec((None, n_lat, rows, w2), lambda j, d: (d, 0, 0, j)),
        out_shape=jax.ShapeDtypeStruct((2, n_lat, rows, n_j * w2), BF16),
        scratch_shapes=[pltpu.VMEM((max(n_ctx, n_lat), rows, w2), F32)],
        compiler_params=_cparams("parallel", "parallel"),
        name="s5_scan",
    )(s_ctx, s_lat, lt)


def _s5_out_body(x_ref, hf_ref, hb_ref, wu_ref, whf_ref, whb_ref, o_ref):
    o_ref[...] = (_dot(x_ref[...], wu_ref[...]) + _dot(hf_ref[...], whf_ref[...])
                  + _dot(hb_ref[...], whb_ref[...])).astype(o_ref.dtype)


def s5_readout(xr, h, wu, wh):
    r = xr.shape[0]
    n_j, k, n = wu.shape
    w2 = wh.shape[2]
    tm = min(r, 1024)
    return pl.pallas_call(
        _s5_out_body,
        grid=(n_j, r // tm),
        in_specs=[pl.BlockSpec((tm, k), lambda j, i: (i, j)),
                  pl.BlockSpec((None, tm, w2), lambda j, i: (0, i, j)),
                  pl.BlockSpec((None, tm, w2), lambda j, i: (1, i, j)),
                  pl.BlockSpec((None, k, n), lambda j, i: (j, 0, 0)),
                  pl.BlockSpec((None, None, w2, n), lambda j, i: (0, j, 0, 0)),
                  pl.BlockSpec((None, None, w2, n), lambda j, i: (1, j, 0, 0))],
        out_specs=pl.BlockSpec((tm, n), lambda j, i: (i, j)),
        out_shape=jax.ShapeDtypeStruct((r, n_j * n), BF16),
        compiler_params=_cparams("parallel", "parallel"),
        name="s5_readout",
    )(xr, h, h, wu, wh, wh)


def _gelu_tanh(x):
    return 0.5 * x * (1.0 + jnp.tanh(math.sqrt(2.0 / math.pi) * (x + 0.044715 * (x * x * x))))


def _s5_glu_body(x_ref, y_ref, g_ref, sh_ref, sc_ref, dk_ref, w1_ref, w2_ref, b1_ref, b2_ref, gt_ref,
                 mg_ref, msh_ref, msc_ref, wr_ref, br_ref, o_ref, tok_ref, eid_ref, gate_ref, *, n_groups, n_experts):
    x = x_ref[...]
    u = _norm_mod(x, g_ref[...], sh_ref[...], sc_ref[...])
    y = _gelu_tanh(y_ref[...].astype(F32) + dk_ref[...] * u).astype(BF16)
    o = (_dot(y, w1_ref[...]) + b1_ref[...]) * jax.nn.sigmoid(_dot(y, w2_ref[...]) + b2_ref[...])
    xn = x + gt_ref[...] * o
    o_ref[...] = xn
    tok = _norm_mod(xn, mg_ref[...], msh_ref[...], msc_ref[...])
    _route_tokens(tok, wr_ref, br_ref, tok_ref, eid_ref, gate_ref, n_groups, n_experts)


def s5_glu(x, y, g, shift, scale, d_skip, w1, b1, w2, b2, gate, moe_g, moe_shift, moe_scale, wr, br, n_groups,
           n_experts):
    bsz, seq, d = x.shape
    tm = min(seq, 512)
    nt = seq // tm
    sub = tm // MOE_TM
    row = lambda a: a.reshape(1, d)
    rspec = pl.BlockSpec((1, d), lambda b, i: (0, 0))
    mspec = pl.BlockSpec((None, 1, d), lambda b, i: (b, 0, 0))
    tile = pl.BlockSpec((None, tm, d), lambda b, i: (b, i, 0))
    wspec = pl.BlockSpec((d, d), lambda b, i: (0, 0), pipeline_mode=pl.Buffered(1))
    return pl.pallas_call(
        functools.partial(_s5_glu_body, n_groups=n_groups, n_experts=n_experts),
        grid=(bsz, nt),
        in_specs=[tile, tile, rspec, mspec, mspec, rspec, wspec, wspec, rspec, rspec, mspec,
                  rspec, mspec, mspec,
                  pl.BlockSpec((2, d, LANES), lambda b, i: (0, 0, 0)),
                  pl.BlockSpec((1, LANES), lambda b, i: (0, 0))],
        out_specs=[tile,
                   pl.BlockSpec((tm, d), lambda b, i: (b * nt + i, 0)),
                   pl.BlockSpec((sub, SUBLANES, MOE_TM), lambda b, i: (b * nt + i, 0, 0)),
                   pl.BlockSpec((tm, LANES), lambda b, i: (b * nt + i, 0))],
        out_shape=[jax.ShapeDtypeStruct((bsz, seq, d), F32),
                   jax.ShapeDtypeStruct((bsz * seq, d), F32),
                   jax.ShapeDtypeStruct((bsz * seq // MOE_TM, SUBLANES, MOE_TM), jnp.int32),
                   jax.ShapeDtypeStruct((bsz * seq, LANES), F32)],
        compiler_params=_cparams("parallel", "parallel"),
        name="s5_glu",
    )(x, y, row(g), shift, scale, row(d_skip), w1, w2, row(b1), row(b2), gate,
      row(moe_g), moe_shift, moe_scale, _split_bf16(wr), br)


def s5_mix(xl, hl, hc, g, sh_l, sc_l, gate_l, a_re, a_im, log_step, b_re, b_im, c_re, c_im, d_skip,
           w1, b1, w2, b2, moe_mod, router):
    bsz, seq, d = xl.shape
    xr_c = _s5_arrange(hc)
    xr_l = _s5_arrange(hl)
    rows = bsz * S5_SEGS
    n_ctx = xr_c.shape[0] // rows
    n_lat = xr_l.shape[0] // rows
    ws, wu, wh, lt = _s5_operators(a_re, a_im, log_step, b_re, b_im, c_re, c_im)
    s_ctx, s_lat = s5_inject(xr_c, xr_l, ws)
    h = s5_scan(s_ctx.reshape(n_ctx, rows, -1), s_lat.reshape(n_lat, rows, -1), lt, bsz)
    y = s5_readout(xr_l, h.reshape(2, n_lat * rows, -1), wu, wh)
    y = _s5_unarrange(y, bsz)
    wr, br = _router_weights(*router)
    xl, tok, eid, gates = s5_glu(xl, y, g, sh_l, sc_l, d_skip, w1.astype(BF16), b1, w2.astype(BF16), b2, gate_l,
                                 *moe_mod, wr, br, router[0].shape[1], router[2].shape[1])
    return xl, (tok, eid, gates)


def hyena_mix(x, g, shift, scale, gate, w_in, b_in, conv_w, conv_b, fw1, fb1, fw2, fb2, fw3, freq, skip,
              w_out, b_out):
    seq = x.shape[1]
    if seq >= FFT_MIN_SEQ and (2 * seq) % (2 * FFT_N2) == 0:
        a, dd, _ = hyena_filter_taps(seq, fw1, fb1, fw2, fb2, fw3, freq, F32)
        v, x0 = hyena_in(x, g, shift, scale, w_in, b_in, conv_w, conv_b, F32)
        yg = hyena_conv_fft(v, x0, skip, a, dd)
    else:
        cmat, smat = dft_matrices(seq)
        a, dd, kn = hyena_filter_taps(seq, fw1, fb1, fw2, fb2, fw3, freq, BF16)
        kr, ki = hyena_filter_dft(a, dd, cmat, smat)
        v, x0 = hyena_in(x, g, shift, scale, w_in, b_in, conv_w, conv_b, BF16)
        yg = hyena_conv(v, x0, skip, kr, ki, kn, cmat, smat)
    return mm_residual(yg, w_out, b_out, x, gate)


def kernel(x, c, ctx, c_ctx, ada_w, ada_b, norm_g, final_g, hy_w_in, hy_b_in, hy_conv_w, hy_conv_b, hy_fw1,
           hy_fb1, hy_fw2, hy_fb2, hy_fw3, hy_freq, hy_skip, hy_w_out, hy_b_out, s5_a_re, s5_a_im,
           s5_log_step, s5_b_re, s5_b_im, s5_c_re, s5_c_im, s5_d, s5_w1, s5_b1, s5_w2, s5_b2, moe_wg, moe_bg,
           moe_we, moe_be, moe_w_gate, moe_w_up, moe_w_down):
    bsz, _, d = x.shape
    depth = ada_w.shape[0]
    assert depth == 2 and bsz < SUBLANES
    c_all = jnp.concatenate([c, c_ctx[None, :], jnp.zeros((SUBLANES - bsz - 1, d), F32)], axis=0)
    mods = ada_mod(c_all, ada_w, ada_b)

    def mod_rows(layer, k):
        lat = mods[layer, :bsz, k * d:(k + 1) * d][:, None, :]
        cx = jnp.broadcast_to(mods[layer, bsz, k * d:(k + 1) * d][None, None, :], (bsz, 1, d))
        return lat, cx

    (sh_a, csh_a), (sc_a, csc_a), (gt_a, cgt_a) = mod_rows(0, 0), mod_rows(0, 1), mod_rows(0, 2)
    (sh_f, csh_f), (sc_f, csc_f), (gt_f, cgt_f) = mod_rows(0, 3), mod_rows(0, 4), mod_rows(0, 5)
    hy = (hy_w_in[0].astype(BF16), hy_b_in[0], hy_conv_w[0], hy_conv_b[0], hy_fw1[0], hy_fb1[0], hy_fw2[0],
          hy_fb2[0], hy_fw3[0], hy_freq[0], hy_skip[0], hy_w_out[0].astype(BF16), hy_b_out[0])
    xl = hyena_mix(x, norm_g[0, 0], sh_a, sc_a, gt_a, *hy)
    xc = hyena_mix(ctx, norm_g[0, 0], csh_a, csc_a, cgt_a, *hy)
    (sh_a, csh_a), (sc_a, csc_a), (gt_a, _) = mod_rows(1, 0), mod_rows(1, 1), mod_rows(1, 2)
    (xl, hl), (_, hc) = hier_moe([(xl, sh_f, sc_f, gt_f), (xc, csh_f, csc_f, cgt_f)], norm_g[0, 1],
                                 moe_wg[0], moe_bg[0], moe_we[0], moe_be[0], moe_w_gate, moe_w_up, moe_w_down, 0,
                                 final_g, False,
                                 next_mods=[(norm_g[1, 0], sh_a, sc_a), (norm_g[1, 0], csh_a, csc_a)])
    (sh_f, _), (sc_f, _), (gt_f, _) = mod_rows(1, 3), mod_rows(1, 4), mod_rows(1, 5)
    router = (moe_wg[1], moe_bg[1], moe_we[1], moe_be[1])
    xl, routed = s5_mix(xl, hl, hc, norm_g[1, 0], sh_a, sc_a, gt_a, s5_a_re[0], s5_a_im[0], s5_log_step[0],
                        s5_b_re[0], s5_b_im[0], s5_c_re[0], s5_c_im[0], s5_d[0], s5_w1[0], s5_b1[0], s5_w2[0],
                        s5_b2[0], (norm_g[1, 1], sh_f, sc_f), router)
    (out,) = hier_moe([(xl, sh_f, sc_f, gt_f)], norm_g[1, 1], *router, moe_w_gate, moe_w_up, moe_w_down, 1,
                      final_g, True, routed=routed)
    return out
```

```python
import functools
import math

import numpy as np
import jax
import jax.numpy as jnp
from jax import lax
from jax.experimental import pallas as pl
from jax.experimental.pallas import tpu as pltpu

F32 = jnp.float32
BF16 = jnp.bfloat16
HIGHEST = lax.Precision.HIGHEST

NORM_EPS = 1e-6
HY_DECAY_TARGET = 1e-2
HY_FAST_PCT = 0.3
HY_SLOW_PCT = 1.5
TOP_K = 2

V7X_VMEM_LIMIT_BYTES = 56 * 1024 * 1024
LANES = 128
SUBLANES = 8
S5_TAU = 8
S5_SEGS = 2
S5_SCAN_UNROLL = 8
MOE_TM = 256
MOE_BM = 256
MOE_NCH = 12
MOE_STAGE = 4
MOE_CAST_ROWS = 128
MOE_WIN_ALIGN = 1024
MOE_WIN = 2 * MOE_WIN_ALIGN
NEG_BIG = -1e30


def _cparams(*sem):
    return pltpu.CompilerParams(dimension_semantics=sem, vmem_limit_bytes=V7X_VMEM_LIMIT_BYTES)


def _norm_mod(x, g, shift, scale):
    ms = jnp.mean(x * x, axis=-1, keepdims=True)
    return (x * lax.rsqrt(ms + NORM_EPS) * g) * (1.0 + scale) + shift


def _dot(a, b):
    return jnp.dot(a, b, preferred_element_type=F32)


def _ada_body(c_ref, w_ref, b_ref, o_ref):
    x = c_ref[...]
    s = (x * jax.nn.sigmoid(x)).astype(BF16)
    o_ref[...] = _dot(s, w_ref[...].astype(BF16)) + b_ref[...]


def ada_mod(c_all, ada_w, ada_b):
    depth, d, n = ada_w.shape
    tn = min(n, 1024)
    return pl.pallas_call(
        _ada_body,
        grid=(depth, n // tn),
        in_specs=[pl.BlockSpec((SUBLANES, d), lambda l, j: (0, 0)),
                  pl.BlockSpec((None, d, tn), lambda l, j: (l, 0, j)),
                  pl.BlockSpec((None, 1, tn), lambda l, j: (l, 0, j))],
        out_specs=pl.BlockSpec((None, SUBLANES, tn), lambda l, j: (l, 0, j)),
        out_shape=jax.ShapeDtypeStruct((depth, SUBLANES, n), F32),
        compiler_params=_cparams("parallel", "parallel"),
        name="ada_mod",
    )(c_all, ada_w, ada_b.reshape(depth, 1, n))


def _hy_in_body(xp_ref, xm_ref, xn_ref, g_ref, sh_ref, sc_ref,
                w0_ref, w1_ref, w2_ref, b0_ref, b1_ref, b2_ref,
                cw0_ref, cw1_ref, cw2_ref, cb0_ref, cb1_ref, cb2_ref,
                v_ref, x0_ref):
    i = pl.program_id(2)
    ni = pl.num_programs(2)
    tm = xm_ref.shape[0]
    x = jnp.concatenate([xp_ref[...], xm_ref[...], xn_ref[...]], axis=0)
    h = _norm_mod(x, g_ref[...], sh_ref[...], sc_ref[...]).astype(BF16)
    rows = lax.broadcasted_iota(jnp.int32, (tm + 2 * SUBLANES, 1), 0)
    valid = jnp.logical_and(jnp.logical_or(rows >= SUBLANES, i > 0),
                            jnp.logical_or(rows < tm + SUBLANES, i < ni - 1))

    def part(w_ref, b_ref, cw_ref, cb_ref):
        z = jnp.where(valid, _dot(h, w_ref[...]) + b_ref[...], 0.0)
        cw = cw_ref[...]
        zp = pltpu.roll(z, 1, axis=0)[SUBLANES:tm + SUBLANES]
        zn = pltpu.roll(z, tm + 2 * SUBLANES - 1, axis=0)[SUBLANES:tm + SUBLANES]
        return zp * cw[0:1] + z[SUBLANES:tm + SUBLANES] * cw[1:2] + zn * cw[2:3] + cb_ref[...]

    x0 = part(w0_ref, b0_ref, cw0_ref, cb0_ref)
    x1 = part(w1_ref, b1_ref, cw1_ref, cb1_ref)
    v = part(w2_ref, b2_ref, cw2_ref, cb2_ref) * x1
    v_ref[...] = v.astype(v_ref.dtype)
    x0_ref[...] = x0.astype(BF16)


def hyena_in(x, g, shift, scale, w_in, b_in, conv_w, conv_b, v_dtype):
    bsz, seq, d = x.shape
    tm = min(seq, 512)
    tn = min(d, 1024)
    nj = d // tn
    r8 = tm // SUBLANES
    last8 = seq // SUBLANES - 1
    row = lambda a: a.reshape(1, -1)
    wspec = lambda k: pl.BlockSpec((d, tn), lambda j, b, i: (0, k * nj + j))
    rspec = lambda k: pl.BlockSpec((1, tn), lambda j, b, i: (0, k * nj + j))
    cspec = lambda k: pl.BlockSpec((3, tn), lambda j, b, i: (0, k * nj + j))
    mspec = pl.BlockSpec((None, 1, d), lambda j, b, i: (b, 0, 0))
    out_spec = pl.BlockSpec((None, tm, tn), lambda j, b, i: (b, i, j))
    return pl.pallas_call(
        _hy_in_body,
        grid=(nj, bsz, seq // tm),
        in_specs=[pl.BlockSpec((None, SUBLANES, d), lambda j, b, i: (b, jnp.maximum(i * r8 - 1, 0), 0)),
                  pl.BlockSpec((None, tm, d), lambda j, b, i: (b, i, 0)),
                  pl.BlockSpec((None, SUBLANES, d), lambda j, b, i: (b, jnp.minimum((i + 1) * r8, last8), 0)),
                  pl.BlockSpec((1, d), lambda j, b, i: (0, 0)), mspec, mspec,
                  wspec(0), wspec(1), wspec(2), rspec(0), rspec(1), rspec(2),
                  cspec(0), cspec(1), cspec(2), rspec(0), rspec(1), rspec(2)],
        out_specs=[out_spec, out_spec],
        out_shape=[jax.ShapeDtypeStruct((bsz, seq, d), v_dtype), jax.ShapeDtypeStruct((bsz, seq, d), BF16)],
        compiler_params=_cparams("parallel", "parallel", "parallel"),
        name="hyena_in",
    )(x, x, x, row(g), shift, scale, w_in, w_in, w_in, row(b_in), row(b_in), row(b_in),
      conv_w, conv_w, conv_w, row(conv_b), row(conv_b), row(conv_b))


def _dft_tables(seq, blk):
    n = 2 * seq
    s = np.arange(seq, dtype=np.int64)[None, :]
    fl = np.arange(blk, dtype=np.int64)[:, None]
    fh = (np.arange(seq // blk, dtype=np.int64) * blk)[:, None]
    w = 2.0 * math.pi / n
    ang_b = ((fl * s) % n) * w
    ang_a = ((fh * s) % n) * w
    f32 = lambda m: jnp.asarray(m.astype(np.float32))
    return (f32(np.cos(ang_a)[:, None, :]), f32(np.sin(ang_a)[:, None, :]), f32(np.cos(ang_b)), f32(np.sin(ang_b)))


def _dft_gen_body(ca_ref, sa_ref, cb_ref, sb_ref, c_ref, s_ref):
    ca, sa, cb, sb = ca_ref[...], sa_ref[...], cb_ref[...], sb_ref[...]
    c_ref[...] = (ca * cb - sa * sb).astype(BF16)
    s_ref[...] = (sa * cb + ca * sb).astype(BF16)


def dft_matrices(seq):
    blk = min(seq, 256)
    ca, sa, cb, sb = _dft_tables(seq, blk)
    aspec = pl.BlockSpec((None, 1, seq), lambda i: (i, 0, 0))
    bspec = pl.BlockSpec((blk, seq), lambda i: (0, 0))
    ospec = pl.BlockSpec((blk, seq), lambda i: (i, 0))
    return pl.pallas_call(
        _dft_gen_body,
        grid=(seq // blk,),
        in_specs=[aspec, aspec, bspec, bspec],
        out_specs=[ospec, ospec],
        out_shape=[jax.ShapeDtypeStruct((seq, seq), BF16)] * 2,
        compiler_params=_cparams("parallel"),
        name="dft_matrices",
    )(ca, sa, cb, sb)


def _alt_sign(rows):
    return jnp.where((rows & 1) == 0, 1.0, -1.0).astype(F32)


def _filt_body(h2_ref, wf_ref, wb_ref, dl_ref, a_ref, d_ref, ny_ref):
    seq = h2_ref.shape[0]
    h2 = h2_ref[...]
    row = lax.broadcasted_iota(jnp.int32, (seq, 1), 0)
    t = row.astype(F32) * (1.0 / (seq - 1))
    win = jnp.exp(-t * dl_ref[...])
    h_hi = h2.astype(BF16)
    h_lo = (h2 - h_hi.astype(F32)).astype(BF16)
    dot3 = lambda w_ref: _dot(h_hi, w_ref[0]) + _dot(h_hi, w_ref[1]) + _dot(h_lo, w_ref[0])
    hf = dot3(wf_ref) * win
    hb = dot3(wb_ref) * win
    hb = jnp.where(row == 0, 0.0, hb)
    nrm = (jnp.sum(jnp.abs(hf), axis=0, keepdims=True) + jnp.sum(jnp.abs(hb), axis=0, keepdims=True))
    inv = 1.0 / nrm
    a = (hf + hb) * inv
    a_ref[...] = a.astype(a_ref.dtype)
    d_ref[...] = ((hb - hf) * inv).astype(d_ref.dtype)
    ny = jnp.sum(a * _alt_sign(row), axis=0, keepdims=True) * (1.0 / (2 * seq))
    ny_ref[...] = jnp.broadcast_to(ny, ny_ref.shape)


def _khat_body(a_ref, d_ref, c_ref, s_ref, kr_ref, ki_ref):
    i = pl.program_id(1)
    tm = c_ref.shape[0]
    seq = c_ref.shape[1]
    f = i * tm + lax.broadcasted_iota(jnp.int32, (tm, 1), 0)
    w = jnp.where(f == 0, 1.0, 2.0).astype(F32) * (1.0 / (2 * seq))
    kr_ref[...] = _dot(c_ref[...], a_ref[...]) * w
    ki_ref[...] = _dot(s_ref[...], d_ref[...]) * w


def hyena_filter_taps(seq, fw1, fb1, fw2, fb2, fw3, freq, taps_dtype):
    d = fw3.shape[1] // 2
    bands_n = (fw1.shape[0] - 1) // 2
    t = np.linspace(0.0, 1.0, seq)[:, None]
    w = (2.0 * math.pi / seq) * np.arange(seq)[:, None]
    bands = np.linspace(1e-4, bands_n - 1, bands_n)[None, :]
    z = jnp.asarray(np.concatenate([t, np.cos(bands * w), -np.sin(bands * w)], axis=-1).astype(np.float32))
    h = jnp.sin(freq * (jnp.dot(z, fw1, precision=HIGHEST) + fb1))
    h2 = jnp.sin(freq * (jnp.dot(h, fw2, precision=HIGHEST) + fb2))
    max_decay = math.log(HY_DECAY_TARGET) / HY_FAST_PCT
    min_decay = math.log(HY_DECAY_TARGET) / HY_SLOW_PCT
    deltas = jnp.abs(jnp.linspace(min_decay, max_decay, d, dtype=F32))[None, :]

    order = h2.shape[1]
    fw3_split = _split_bf16(fw3)
    tn = min(d, 256)
    nj = d // tn
    return pl.pallas_call(
        _filt_body,
        grid=(nj,),
        in_specs=[pl.BlockSpec((seq, order), lambda j: (0, 0)),
                  pl.BlockSpec((2, order, tn), lambda j: (0, 0, j)),
                  pl.BlockSpec((2, order, tn), lambda j: (0, 0, nj + j)),
                  pl.BlockSpec((1, tn), lambda j: (0, j))],
        out_specs=[pl.BlockSpec((seq, tn), lambda j: (0, j)),
                   pl.BlockSpec((seq, tn), lambda j: (0, j)),
                   pl.BlockSpec((SUBLANES, tn), lambda j: (0, j))],
        out_shape=[jax.ShapeDtypeStruct((seq, d), taps_dtype), jax.ShapeDtypeStruct((seq, d), taps_dtype),
                   jax.ShapeDtypeStruct((SUBLANES, d), F32)],
        compiler_params=_cparams("parallel"),
        name="hyena_filter_taps",
    )(h2, fw3_split, fw3_split, deltas)


def hyena_filter_dft(a, dd, cmat, smat):
    seq, d = a.shape
    tm = min(seq, 512)
    tn2 = min(d, 512)
    return pl.pallas_call(
        _khat_body,
        grid=(d // tn2, seq // tm),
        in_specs=[pl.BlockSpec((seq, tn2), lambda j, i: (0, j)),
                  pl.BlockSpec((seq, tn2), lambda j, i: (0, j)),
                  pl.BlockSpec((tm, seq), lambda j, i: (i, 0)),
                  pl.BlockSpec((tm, seq), lambda j, i: (i, 0))],
        out_specs=[pl.BlockSpec((tm, tn2), lambda j, i: (i, j))] * 2,
        out_shape=[jax.ShapeDtypeStruct((seq, d), F32)] * 2,
        compiler_params=_cparams("parallel", "parallel"),
        name="hyena_filter_dft",
    )(a, dd, cmat, smat)


def _dft_fwd_body(v_ref, c_ref, s_ref, kr_ref, ki_ref, kn_ref, ya_ref, yb_ref, yn_ref):
    i = pl.program_id(2)
    v = v_ref[...]
    vr = _dot(c_ref[...], v)
    p = _dot(s_ref[...], v)
    kr = kr_ref[...]
    ki = ki_ref[...]
    ya_ref[...] = (vr * kr + p * ki).astype(BF16)
    yb_ref[...] = (p * kr - vr * ki).astype(BF16)

    @pl.when(i == 0)
    def _():
        seq = v.shape[0]
        row = lax.broadcasted_iota(jnp.int32, (seq, 1), 0)
        vl = jnp.sum(v.astype(F32) * _alt_sign(row), axis=0, keepdims=True)
        yn_ref[...] = jnp.broadcast_to(vl * kn_ref[0:1, :], yn_ref.shape)


def _dft_inv_body(ya_ref, yb_ref, c_ref, s_ref, v_ref, x0_ref, skip_ref, yn_ref, o_ref):
    i = pl.program_id(2)
    tm = c_ref.shape[0]
    acc = _dot(c_ref[...], ya_ref[...]) + _dot(s_ref[...], yb_ref[...])
    t = i * tm + lax.broadcasted_iota(jnp.int32, (tm, 1), 0)
    y = acc + _alt_sign(t) * yn_ref[0:1, :] + skip_ref[...] * v_ref[...].astype(F32)
    o_ref[...] = (y * x0_ref[...].astype(F32)).astype(BF16)


def hyena_conv(v, x0, skip, kr, ki, kn, cmat, smat):
    bsz, seq, d = v.shape
    tm = min(seq, 512)
    tn = min(d, 512)
    grid = (bsz, d // tn, seq // tm)
    full = pl.BlockSpec((None, seq, tn), lambda b, j, i: (b, 0, j))
    mat = pl.BlockSpec((tm, seq), lambda b, j, i: (i, 0))
    tile = pl.BlockSpec((None, tm, tn), lambda b, j, i: (b, i, j))
    ktile = pl.BlockSpec((tm, tn), lambda b, j, i: (i, j))
    nyq = pl.BlockSpec((None, SUBLANES, tn), lambda b, j, i: (b, 0, j))
    ya, yb, yn = pl.pallas_call(
        _dft_fwd_body,
        grid=grid,
        in_specs=[full, mat, mat, ktile, ktile, pl.BlockSpec((SUBLANES, tn), lambda b, j, i: (0, j))],
        out_specs=[tile, tile, nyq],
        out_shape=[jax.ShapeDtypeStruct((bsz, seq, d), BF16)] * 2
        + [jax.ShapeDtypeStruct((bsz, SUBLANES, d), F32)],
        compiler_params=_cparams("parallel", "parallel", "arbitrary"),
        name="hyena_dft_fwd",
    )(v, cmat, smat, kr, ki, kn)
    return pl.pallas_call(
        _dft_inv_body,
        grid=grid,
        in_specs=[full, full, mat, mat, tile, tile, pl.BlockSpec((1, tn), lambda b, j, i: (0, j)), nyq],
        out_specs=tile,
        out_shape=jax.ShapeDtypeStruct((bsz, seq, d), BF16),
        compiler_params=_cparams("parallel", "parallel", "parallel"),
        name="hyena_dft_inv",
    )(ya, yb, cmat, smat, v, x0, skip.reshape(1, d), yn)


FFT_N2 = 128
FFT_MIN_SEQ = 1024
FFT_UNROLL = 8


def _unroll_for(trips, cap):
    return max(u for u in range(1, cap + 1) if trips % u == 0)


def _fft_matrices(seq):
    n = 2 * seq
    n2 = FFT_N2
    n1 = n // n2
    r8 = SUBLANES
    q = np.arange(n2 // r8, dtype=np.int64)[:, None, None, None]
    nf = n1 // 2 + 1
    f1 = np.arange(nf, dtype=np.int64)[None, :, None, None]
    r = np.arange(r8, dtype=np.int64)[None, None, :, None]
    t1 = np.arange(n1 // 2, dtype=np.int64)[None, None, None, :]
    ang = ((f1 * (t1 * n2 + q * r8 + r)) % n) * (2.0 * math.pi / n)
    g = np.stack([np.cos(ang), -np.sin(ang)], axis=3)
    eye = np.eye(r8)[None, None, :, None, None, :]
    ma = (g[..., None] * eye).reshape(n2 // r8, nf * r8 * 2, (n1 // 2) * r8).astype(np.float32)
    f2 = np.arange(n2, dtype=np.int64)[:, None]
    t2 = np.arange(n2, dtype=np.int64)[None, :]
    th = ((f2 * t2) % n2) * (2.0 * math.pi / n2)
    co, si = np.cos(th), np.sin(th)
    wc = np.stack([np.stack([co, si], axis=-1), np.stack([-si, co], axis=-1)], axis=0)
    wc = wc.reshape(2 * n2, 2 * n2).astype(np.float32)
    as_bf16 = lambda m: jnp.asarray(np.ascontiguousarray(m).astype(BF16))
    return as_bf16(ma), as_bf16(np.swapaxes(ma, 1, 2)), as_bf16(wc), as_bf16(wc.T)


def _fft_stage_a(x_ref, ma_ref, s1):
    n1h, n_q, r8, tn = x_ref.shape
    nf = s1.shape[0]

    def body(q, carry):
        x = x_ref[:, pl.ds(q, 1), :, :].reshape(n1h * r8, tn).astype(BF16)
        a = _dot(ma_ref[q], x).astype(BF16)
        s1[:, pl.ds(pl.multiple_of(q * 2 * r8, 2 * r8), 2 * r8), :] = a.reshape(nf, 2 * r8, tn)
        return carry

    lax.fori_loop(0, n_q, body, 0, unroll=FFT_UNROLL)


def _fft_conv_body(v_ref, x0_ref, k_ref, skip_ref, ma_ref, mat_ref, wc_ref, wci_ref, o_ref, s1, ysc):
    n1h, n_q, r8, tn = v_ref.shape
    nf = s1.shape[0]
    n2 = s1.shape[1] // 2
    seq = n1h * n_q * r8
    _fft_stage_a(v_ref, ma_ref, s1)

    def slab(f, carry):
        y = _dot(wc_ref[...], s1[f])
        yr, yi = y[:n2], y[n2:]
        kr = k_ref[f, 0].astype(F32)
        ki = k_ref[f, 1].astype(F32)
        p = jnp.concatenate([yr * kr - yi * ki, yr * ki + yi * kr], axis=0).astype(BF16)
        s1[f] = _dot(wci_ref[...], p).astype(BF16)
        return carry

    lax.fori_loop(0, nf, slab, 0, unroll=_unroll_for(nf, 2 * FFT_UNROLL))

    def inv_a(q, carry):
        z = s1[:, pl.ds(pl.multiple_of(q * 2 * r8, 2 * r8), 2 * r8), :].reshape(nf * 2 * r8, tn)
        ysc[:, pl.ds(q, 1), :, :] = _dot(mat_ref[q], z).reshape(n1h, 1, r8, tn)
        return carry

    lax.fori_loop(0, n_q, inv_a, 0, unroll=FFT_UNROLL)
    y = ysc[...].reshape(seq, tn) + skip_ref[...] * v_ref[...].reshape(seq, tn)
    o_ref[...] = (y * x0_ref[...].astype(F32)).astype(BF16)


def _fft_filter_body(a_ref, d_ref, ma_ref, wc_ref, k_ref, s1):
    nf = s1.shape[0]
    n2 = s1.shape[1] // 2
    scale = 1.0 / (2 * (nf - 1) * n2)
    for src_ref, part, sign in ((a_ref, 0, scale), (d_ref, 1, -scale)):
        _fft_stage_a(src_ref, ma_ref, s1)

        def slab(f, carry):
            y = _dot(wc_ref[part * n2:(part + 1) * n2, :], s1[f])
            mirrored = jnp.logical_and(f > 0, f < nf - 1)
            k_ref[f, part] = (y * (sign * jnp.where(mirrored, 2.0, 1.0))).astype(BF16)
            return carry

        lax.fori_loop(0, nf, slab, 0, unroll=_unroll_for(nf, FFT_UNROLL))


def hyena_conv_fft(v, x0, skip, a, dd):
    bsz, seq, d = v.shape
    n2 = FFT_N2
    n1 = 2 * seq // n2
    nf = n1 // 2 + 1
    n_q = n2 // SUBLANES
    tn = min(d, 256)
    ma, mat, wc, wci = _fft_matrices(seq)
    const = lambda shape: pl.BlockSpec(shape, lambda *_: (0,) * len(shape), pipeline_mode=pl.Buffered(1))
    view = lambda t: t.reshape(t.shape[:-2] + (n1 // 2, n_q, SUBLANES, d))
    tap = pl.BlockSpec((n1 // 2, n_q, SUBLANES, tn), lambda j: (0, 0, 0, j))
    khat = pl.pallas_call(
        _fft_filter_body,
        grid=(d // tn,),
        in_specs=[tap, tap, const(ma.shape), const(wc.shape)],
        out_specs=pl.BlockSpec((nf, 2, n2, tn), lambda j: (0, 0, 0, j)),
        out_shape=jax.ShapeDtypeStruct((nf, 2, n2, d), BF16),
        scratch_shapes=[pltpu.VMEM((nf, 2 * n2, tn), BF16)],
        compiler_params=_cparams("parallel"),
        name="hyena_filter_fft",
    )(view(a), view(dd), ma, wc)
    return pl.pallas_call(
        _fft_conv_body,
        grid=(d // tn, bsz),
        in_specs=[pl.BlockSpec((None, n1 // 2, n_q, SUBLANES, tn), lambda j, b: (b, 0, 0, 0, j)),
                  pl.BlockSpec((None, seq, tn), lambda j, b: (b, 0, j)),
                  pl.BlockSpec((nf, 2, n2, tn), lambda j, b: (0, 0, 0, j), pipeline_mode=pl.Buffered(1)),
                  pl.BlockSpec((1, tn), lambda j, b: (0, j)),
                  const(ma.shape), const(mat.shape), const(wc.shape), const(wci.shape)],
        out_specs=pl.BlockSpec((None, seq, tn), lambda j, b: (b, 0, j)),
        out_shape=jax.ShapeDtypeStruct((bsz, seq, d), BF16),
        scratch_shapes=[pltpu.VMEM((nf, 2 * n2, tn), BF16), pltpu.VMEM((n1 // 2, n_q, SUBLANES, tn), F32)],
        compiler_params=_cparams("parallel", "arbitrary"),
        name="hyena_conv_fft",
    )(view(v), x0, khat, skip.reshape(1, d), ma, mat, wc, wci)


def _mm_res_body(x_ref, w_ref, b_ref, res_ref, gate_ref, o_ref):
    o_ref[...] = res_ref[...] + gate_ref[...] * (_dot(x_ref[...], w_ref[...]) + b_ref[...])


def mm_residual(x, w, b, res, gate):
    bsz, seq, k = x.shape
    n = w.shape[1]
    tm = min(seq, 512)
    return pl.pallas_call(
        _mm_res_body,
        grid=(bsz, seq // tm),
        in_specs=[pl.BlockSpec((None, tm, k), lambda b, i: (b, i, 0)),
                  pl.BlockSpec((k, n), lambda b, i: (0, 0)),
                  pl.BlockSpec((1, n), lambda b, i: (0, 0)),
                  pl.BlockSpec((None, tm, n), lambda b, i: (b, i, 0)),
                  pl.BlockSpec((None, 1, n), lambda b, i: (b, 0, 0))],
        out_specs=pl.BlockSpec((None, tm, n), lambda b, i: (b, i, 0)),
        out_shape=jax.ShapeDtypeStruct((bsz, seq, n), F32),
        compiler_params=_cparams("parallel", "parallel"),
        name="mm_residual",
    )(x, w, b.reshape(1, n), res, gate)


def _moe_pre_body(*refs, n_groups, n_experts, tile_offs):
    n_streams = len(tile_offs) - 1
    g_ref, wr_ref, br_ref, tok_ref, eid_ref, gate_ref = refs[3 * n_streams:]
    i = pl.program_id(0)
    for k in range(n_streams):
        x_ref, sh_ref, sc_ref = refs[3 * k:3 * k + 3]

        @pl.when(jnp.logical_and(i >= tile_offs[k], i < tile_offs[k + 1]))
        def _():
            tok = _norm_mod(x_ref[...], g_ref[...], sh_ref[...], sc_ref[...])
            _route_tokens(tok, wr_ref, br_ref, tok_ref, eid_ref, gate_ref, n_groups, n_experts)


def _route_tokens(tok, wr_ref, br_ref, tok_ref, eid_ref, gate_ref, n_groups, n_experts):
    tok_ref[...] = tok
    t_hi = tok.astype(BF16)
    t_lo = (tok - t_hi.astype(F32)).astype(BF16)
    logits = (_dot(t_hi, wr_ref[0]) + _dot(t_hi, wr_ref[1]) + _dot(t_lo, wr_ref[0])) + br_ref[...]
    lane = lax.broadcasted_iota(jnp.int32, logits.shape, 1)
    per = n_experts // n_groups
    big = jnp.int32(1 << 20)
    gmask = jnp.logical_and(lane >= n_experts, lane < n_experts + n_groups)
    gl = jnp.where(gmask, logits, NEG_BIG)
    gmax = jnp.max(gl, axis=-1, keepdims=True)
    gidx = jnp.min(jnp.where(gl == gmax, lane - n_experts, big), axis=-1, keepdims=True)
    p_top = 1.0 / jnp.sum(jnp.where(gmask, jnp.exp(gl - gmax), 0.0), axis=-1, keepdims=True)
    lo = gidx * per
    emask = jnp.logical_and(lane >= lo, lane < lo + per)
    el = jnp.where(emask, logits, NEG_BIG)
    m1 = jnp.max(el, axis=-1, keepdims=True)
    i1 = jnp.min(jnp.where(el == m1, lane, big), axis=-1, keepdims=True)
    el2 = jnp.where(lane == i1, NEG_BIG, el)
    m2 = jnp.max(el2, axis=-1, keepdims=True)
    i2 = jnp.min(jnp.where(el2 == m2, lane, big), axis=-1, keepdims=True)
    e21 = jnp.exp(m2 - m1)
    g1 = p_top / (1.0 + e21)
    g2 = g1 * e21
    ids = jnp.where(lane == 0, i1, jnp.where(lane == 1, i2, -1))
    ids_t = ids.T
    for h in range(eid_ref.shape[0]):
        eid_ref[h] = ids_t[0:SUBLANES, h * MOE_TM:(h + 1) * MOE_TM]
    gate_ref[...] = jnp.where(lane == 0, g1, jnp.where(lane == 1, g2, 0.0))


def _split_bf16(w):
    hi = w.astype(BF16)
    return jnp.stack([hi, (w - hi.astype(F32)).astype(BF16)])


def _router_weights(wg, bg, we, be):
    pad = LANES - we.shape[1] - wg.shape[1]
    wr = jnp.pad(jnp.concatenate([we, wg], axis=1), ((0, 0), (0, pad)))
    br = jnp.pad(jnp.concatenate([be, bg]), (0, pad)).reshape(1, LANES)
    return wr, br


def moe_pre(streams, g, wr, br, n_groups, n_experts):
    d = streams[0][0].shape[2]
    tm = MOE_TM
    tile_offs = [0]
    in_specs, args = [], []
    for x, shift, scale in streams:
        bsz, seq, _ = x.shape
        nt = seq // tm
        n_tiles = bsz * nt
        off = tile_offs[-1]
        tile_offs.append(off + n_tiles)

        def tile(i, off=off, n_tiles=n_tiles):
            return jnp.clip(i - off, 0, n_tiles - 1)

        in_specs += [pl.BlockSpec((None, tm, d), lambda i, tile=tile, nt=nt: (tile(i) // nt, tile(i) % nt, 0)),
                     pl.BlockSpec((None, 1, d), lambda i, tile=tile, nt=nt: (tile(i) // nt, 0, 0)),
                     pl.BlockSpec((None, 1, d), lambda i, tile=tile, nt=nt: (tile(i) // nt, 0, 0))]
        args += [x, shift, scale]
    in_specs += [pl.BlockSpec((1, d), lambda i: (0, 0)),
                 pl.BlockSpec((2, d, LANES), lambda i: (0, 0, 0)),
                 pl.BlockSpec((1, LANES), lambda i: (0, 0))]
    args += [g.reshape(1, d), _split_bf16(wr), br]
    total = tile_offs[-1] * tm
    rout = pl.BlockSpec((tm, LANES), lambda i: (i, 0))
    tok, eid, gate = pl.pallas_call(
        functools.partial(_moe_pre_body, n_groups=n_groups, n_experts=n_experts, tile_offs=tuple(tile_offs)),
        grid=(tile_offs[-1],),
        in_specs=in_specs,
        out_specs=[pl.BlockSpec((tm, d), lambda i: (i, 0)),
                   pl.BlockSpec((1, SUBLANES, tm), lambda i: (i, 0, 0)), rout],
        out_shape=[jax.ShapeDtypeStruct((total, d), F32),
                   jax.ShapeDtypeStruct((tile_offs[-1], SUBLANES, tm), jnp.int32),
                   jax.ShapeDtypeStruct((total, LANES), F32)],
        compiler_params=_cparams("parallel"),
        name="moe_pre",
    )(*args)
    return tok, eid, gate, tile_offs[:-1]


def _start_row_gather(row_index, n_rows, src_hbm, dst_vmem, sem, priorities=(0,)):
    def body(g, c):
        r0 = pl.multiple_of(g * SUBLANES, SUBLANES)
        dst_tile = dst_vmem.at[pl.ds(r0, SUBLANES)]
        for k in range(SUBLANES):
            pltpu.make_async_copy(src_hbm.at[pl.ds(row_index(r0 + k), 1)], dst_tile.at[pl.ds(k, 1)],
                                  sem).start(priority=priorities[k % len(priorities)])
        return c

    lax.fori_loop(0, n_rows // SUBLANES, body, 0, unroll=2)


def _wait_row_gather(n_rows, src_hbm, dst_vmem, sem):
    pltpu.make_async_copy(src_hbm.at[pl.ds(0, n_rows)], dst_vmem, sem).wait()


def _expert_body(bv_ref, rk_ref, pe_ref, tot_ref, p0_ref, ts_hbm, tok_ref, wg_hbm, wu_hbm, wd_hbm, o_ref,
                 xbuf, xsem, win, isem, wcache, stg, wsem, cnt, *, layer):
    i = pl.program_id(0)
    n = pl.num_programs(0)
    slot = i % 2
    cr, cc = stg.shape[1:]
    total = tot_ref[0]
    mats_hbm = (wg_hbm, wu_hbm, wd_hbm)

    @pl.when(i == 0)
    def _():
        cnt[0] = 0
        cnt[1] = 0

    def live(b):
        return jnp.logical_and(b < n, bv_ref[jnp.minimum(b, n - 1)] > 0)

    def window(b):
        s = b % 2
        lo = pl.multiple_of((p0_ref[jnp.minimum(b, n - 1)] // MOE_WIN_ALIGN) * MOE_WIN_ALIGN, MOE_WIN_ALIGN)
        return pltpu.make_async_copy(ts_hbm.at[pl.ds(lo, MOE_WIN)],
                                     win.at[pl.ds(pl.multiple_of(s * MOE_WIN, MOE_WIN), MOE_WIN)], isem.at[s])

    def gather(b):
        s = b % 2
        base = s * MOE_WIN + p0_ref[jnp.minimum(b, n - 1)] % MOE_WIN_ALIGN
        _start_row_gather(lambda r: win[base + r], MOE_BM, tok_ref, xbuf.at[s], xsem.at[s])

    @pl.when(jnp.logical_and(i == 0, live(0)))
    def _():
        window(0).start()
        window(0).wait()
        gather(0)

    @pl.when(jnp.logical_and(i == 0, live(1)))
    def _():
        window(1).start()

    @pl.when(live(i + 1))
    def _():
        window(i + 1).wait()
        gather(i + 1)

    @pl.when(live(i + 2))
    def _():
        window(i + 2).start()

    def chunk_geom(c):
        q = c % MOE_NCH
        m = q // 4
        sub = q % 4
        r0 = jnp.where(m < 2, sub, sub // 2) * cr
        c0 = jnp.where(m < 2, 0, sub % 2) * cc
        return m, pl.multiple_of(r0, cr), pl.multiple_of(c0, cc)

    def issue(c):
        e = pe_ref[c // MOE_NCH]
        m, r0, c0 = chunk_geom(c)
        s = c % MOE_STAGE
        for k, w_hbm in enumerate(mats_hbm):
            @pl.when(m == k)
            def _():
                pltpu.make_async_copy(w_hbm.at[layer, e, pl.ds(r0, cr), pl.ds(c0, cc)], stg.at[s],
                                      wsem.at[s]).start(priority=1)

    def cast(c):
        s = c % MOE_STAGE
        pltpu.make_async_copy(wg_hbm.at[layer, 0, pl.ds(0, cr), pl.ds(0, cc)], stg.at[s], wsem.at[s]).wait()
        ws = (c // MOE_NCH) % 2
        q = c % MOE_NCH
        step = min(MOE_CAST_ROWS, cr)
        assert cr % step == 0

        def slab(k, carry):
            rows = pl.ds(pl.multiple_of(k * step, step), step)
            wcache[ws, q, rows, :] = stg[s, rows, :].astype(BF16)
            return carry

        lax.fori_loop(0, cr // step, slab, 0)

    valid = bv_ref[i] > 0
    rank = rk_ref[i]
    issued = cnt[0]
    done = cnt[1]
    limit = jnp.minimum(total, MOE_NCH * (rank + 2))
    need = jnp.where(valid, MOE_NCH * (rank + 1), done)

    def fill(issued, done):
        hi = jnp.minimum(limit, done + MOE_STAGE)

        def body(c, carry):
            issue(c)
            return carry

        lax.fori_loop(issued, hi, body, 0)
        return jnp.maximum(issued, hi)

    def cast_and_refill(c, issued):
        cast(c)
        more = issued < jnp.minimum(limit, c + 1 + MOE_STAGE)

        @pl.when(more)
        def _():
            issue(issued)

        return issued + more.astype(jnp.int32)

    issued = fill(issued, done)
    issued = lax.fori_loop(done, need, cast_and_refill, issued)
    done = jnp.maximum(done, need)

    @pl.when(valid)
    def _():
        ws = rank % 2
        _wait_row_gather(MOE_BM, tok_ref, xbuf.at[slot], xsem.at[slot])
        x = xbuf[slot].astype(BF16)
        gate = sum(_dot(x[:, k * cr:(k + 1) * cr], wcache[ws, k]) for k in range(4))
        up = sum(_dot(x[:, k * cr:(k + 1) * cr], wcache[ws, 4 + k]) for k in range(4))
        h = (gate * jax.nn.sigmoid(gate) * up).astype(BF16)
        for half in range(2):
            o_ref[:, half * cc:(half + 1) * cc] = sum(
                _dot(h[:, k * cr:(k + 1) * cr], wcache[ws, 8 + 2 * k + half]) for k in range(2))

    @pl.when(jnp.logical_not(valid))
    def _():
        o_ref[...] = jnp.zeros_like(o_ref)

    fetched = issued
    issued = lax.fori_loop(done, fetched, cast_and_refill, issued)
    done = jnp.maximum(done, fetched)
    last = i == n - 1
    tail = jnp.where(last, issued, done)

    def drain(c, carry):
        cast(c)
        return carry

    lax.fori_loop(done, tail, drain, 0)
    cnt[0] = issued
    cnt[1] = jnp.maximum(done, tail)


def moe_experts(tok, tok_sorted, block_p0, block_valid, block_rank, present, n_chunks, w_gate, w_up, w_down,
                layer):
    d = tok.shape[1]
    n_blocks = block_valid.shape[0]
    n_rows = n_blocks * MOE_BM
    dh = w_gate.shape[3]
    assert 2 * dh == d and MOE_NCH == 12
    cr, cc = d // 4, dh
    any_spec = pl.BlockSpec(memory_space=pl.ANY)
    grid_spec = pltpu.PrefetchScalarGridSpec(
        num_scalar_prefetch=5,
        grid=(n_blocks,),
        in_specs=[any_spec, any_spec, any_spec, any_spec, any_spec],
        out_specs=pl.BlockSpec((MOE_BM, d), lambda i, *_: (i, 0)),
        scratch_shapes=[pltpu.VMEM((2, MOE_BM, d), F32), pltpu.SemaphoreType.DMA((2,)),
                        pltpu.SMEM((2 * MOE_WIN,), jnp.int32), pltpu.SemaphoreType.DMA((2,)),
                        pltpu.VMEM((2, MOE_NCH, cr, cc), BF16),
                        pltpu.VMEM((MOE_STAGE, cr, cc), F32), pltpu.SemaphoreType.DMA((MOE_STAGE,)),
                        pltpu.SMEM((2,), jnp.int32)],
    )
    return pl.pallas_call(
        functools.partial(_expert_body, layer=layer),
        grid_spec=grid_spec,
        out_shape=jax.ShapeDtypeStruct((n_rows, d), F32),
        compiler_params=_cparams("arbitrary"),
        name="moe_experts",
    )(block_valid, block_rank, present, n_chunks, block_p0, tok_sorted, tok, w_gate, w_up, w_down)


def _combine_body(dest_ref, nxt_ref, os_ref, gate_ref, res_ref, gt_ref, fg_ref, *rest, final_norm, with_next):
    if with_next:
        ng_ref, nsh_ref, nsc_ref, o_ref, h_ref, buf, sem = rest
    else:
        o_ref, buf, sem = rest
    rows = res_ref.shape[0]
    i = pl.program_id(0)
    n = pl.num_programs(0)
    slot = i % 2

    def start(idx_ref, s):
        for k in range(TOP_K):
            _start_row_gather(lambda r, k=k: idx_ref[k * rows + r], rows, os_ref, buf.at[s, k], sem.at[s],
                              priorities=(0, 1))

    @pl.when(i == 0)
    def _():
        start(dest_ref, 0)

    @pl.when(i + 1 < n)
    def _():
        start(nxt_ref, 1 - slot)

    for k in range(TOP_K):
        _wait_row_gather(rows, os_ref, buf.at[slot, k], sem.at[slot])
    gates = gate_ref[...]
    mo = gates[:, 0:1] * buf[slot, 0] + gates[:, 1:2] * buf[slot, 1]
    y = res_ref[...] + gt_ref[...] * mo
    if final_norm:
        ms = jnp.mean(y * y, axis=-1, keepdims=True)
        y = y * lax.rsqrt(ms + NORM_EPS) * fg_ref[...]
    o_ref[...] = y
    if with_next:
        h_ref[...] = _norm_mod(y, ng_ref[...], nsh_ref[...], nsc_ref[...]).astype(h_ref.dtype)


def moe_combine(os, dest, gates, tile0, res, gt, final_g, final_norm, next_mod=None):
    bsz, seq, d = res.shape
    rows = MOE_TM
    nt = seq // rows
    n = bsz * nt
    tile = pl.BlockSpec((None, rows, d), lambda i: (i // nt, i % nt, 0))
    mspec = pl.BlockSpec((None, 1, d), lambda i: (i // nt, 0, 0))
    rspec = pl.BlockSpec((1, d), lambda i: (0, 0))
    dspec = lambda step: pl.BlockSpec((SUBLANES * rows,), lambda i: (tile0 + step(i),), memory_space=pltpu.SMEM)
    in_specs = [dspec(lambda i: i), dspec(lambda i: jnp.minimum(i + 1, n - 1)),
                pl.BlockSpec(memory_space=pl.ANY),
                pl.BlockSpec((rows, LANES), lambda i: (tile0 + i, 0)),
                tile, mspec, rspec]
    args = [dest, dest, os, gates, res, gt, final_g.reshape(1, d)]
    out_specs, out_shape = [tile], [jax.ShapeDtypeStruct((bsz, seq, d), F32)]
    if next_mod is not None:
        in_specs += [rspec, mspec, mspec]
        args += [next_mod[0].reshape(1, d), next_mod[1], next_mod[2]]
        out_specs.append(tile)
        out_shape.append(jax.ShapeDtypeStruct((bsz, seq, d), BF16))
    outs = pl.pallas_call(
        functools.partial(_combine_body, final_norm=final_norm, with_next=next_mod is not None),
        grid=(n,),
        in_specs=in_specs,
        out_specs=out_specs,
        out_shape=out_shape,
        scratch_shapes=[pltpu.VMEM((2, TOP_K, rows, d), F32), pltpu.SemaphoreType.DMA((2,))],
        compiler_params=_cparams("arbitrary"),
        name="moe_combine",
    )(*args)
    return outs if next_mod is not None else outs[0]


def _plan_body(e_ref, dest_ref, tab_ref, blk_ref, present_ref, *, n_experts):
    n_rows, w = e_ref.shape
    e_all = e_ref[...]
    li = lax.broadcasted_iota(jnp.int32, (w, w), 0)
    lj = lax.broadcasted_iota(jnp.int32, (w, w), 1)
    incl = (li <= lj).astype(BF16)
    ri = lax.broadcasted_iota(jnp.int32, (n_rows, n_rows), 0)
    rj = lax.broadcasted_iota(jnp.int32, (n_rows, n_rows), 1)
    before = (rj < ri).astype(BF16)
    elane = lax.broadcasted_iota(jnp.int32, (n_rows, LANES), 1)
    row_tot = jnp.zeros((n_rows, LANES), F32)
    for e in range(n_experts):
        tot = jnp.sum((e_all == e).astype(F32), axis=1, keepdims=True)
        row_tot = row_tot + jnp.where(elane == e, tot, 0.0)
    rows_before = _dot(before, row_tot.astype(BF16))
    counts = jnp.sum(row_tot, axis=0, keepdims=True).astype(jnp.int32)
    lane1 = lax.broadcasted_iota(jnp.int32, (1, LANES), 1)

    def excl_prefix(v):
        acc = v
        sh = 1
        while sh < LANES:
            acc = acc + jnp.where(lane1 >= sh, pltpu.roll(acc, sh, axis=1), 0)
            sh *= 2
        return acc - v

    start = excl_prefix(counts)
    padded = (counts + (MOE_BM - 1)) // MOE_BM * MOE_BM
    pad_start = excl_prefix(padded)
    pad_end = pad_start + padded
    has = (counts > 0).astype(jnp.int32)
    rank = excl_prefix(has)
    n_chunks = MOE_NCH * jnp.sum(has.astype(F32), axis=1, keepdims=True).astype(jnp.int32)
    tab_ref[...] = jnp.concatenate([counts, start, pad_start, pad_end, jnp.broadcast_to(n_chunks, (1, LANES)),
                                    jnp.zeros((SUBLANES - 5, LANES), jnp.int32)], axis=0)
    nb = blk_ref.shape[0]
    f32 = lambda v: v.astype(F32)
    lane_b = lax.broadcasted_iota(jnp.int32, (nb, LANES), 1)
    first_row = lax.broadcasted_iota(jnp.int32, (nb, 1), 0) * MOE_BM
    is_expert = lane_b < n_experts
    expert = jnp.sum(f32(jnp.logical_and(is_expert, pad_end <= first_row)), axis=1, keepdims=True)
    expert = jnp.minimum(expert.astype(jnp.int32), n_experts - 1)
    mine = lane_b == expert
    pick = lambda v: jnp.sum(jnp.where(mine, f32(v), 0.0), axis=1, keepdims=True).astype(jnp.int32)
    live = first_row < jnp.max(f32(pad_end), axis=1, keepdims=True).astype(jnp.int32)
    p0 = jnp.where(live, pick(start) - pick(pad_start) + first_row, 0)
    blk_ref[...] = jnp.where(lane_b == 0, live.astype(jnp.int32),
                             jnp.where(lane_b == 1, pick(rank), jnp.where(lane_b == 2, p0, 0)))
    row_r = lax.broadcasted_iota(jnp.int32, (LANES, LANES), 0)
    lane_e = lax.broadcasted_iota(jnp.int32, (LANES, LANES), 1)
    hit_r = jnp.logical_and(has > 0, rank == row_r)
    present = jnp.sum(jnp.where(hit_r, f32(lane_e), 0.0), axis=1, keepdims=True).astype(jnp.int32)
    present_ref[...] = jnp.broadcast_to(present, (LANES, LANES))
    base = rows_before + pad_start.astype(F32)
    dest = jnp.zeros((n_rows, w), F32)
    for e in range(n_experts):
        hit = e_all == e
        within = _dot(hit.astype(BF16), incl)
        dest = dest + jnp.where(hit, within - 1.0 + base[:, e:e + 1], 0.0)
    dest_ref[...] = dest.astype(jnp.int32)


def _route_plan(eid, n_experts):
    n_tiles, r8, tm = eid.shape
    a = n_tiles * TOP_K * tm
    n_blocks = -(-a // MOE_BM) + n_experts
    assert (tm & (tm - 1)) == 0 and TOP_K == 2
    nb_pad = -(-n_blocks // SUBLANES) * SUBLANES
    dest, tab, blk, present = pl.pallas_call(
        functools.partial(_plan_body, n_experts=n_experts),
        out_shape=[jax.ShapeDtypeStruct((n_tiles * r8, tm), jnp.int32),
                   jax.ShapeDtypeStruct((SUBLANES, LANES), jnp.int32),
                   jax.ShapeDtypeStruct((nb_pad, LANES), jnp.int32),
                   jax.ShapeDtypeStruct((LANES, LANES), jnp.int32)],
        compiler_params=pltpu.CompilerParams(vmem_limit_bytes=V7X_VMEM_LIMIT_BYTES),
        name="moe_route_plan",
    )(eid.reshape(n_tiles * r8, tm))
    e_flat = eid[:, :TOP_K, :].reshape(-1)
    order = jnp.argsort(e_flat).astype(jnp.int32)
    shift = tm.bit_length() - 1
    tok_of = ((order >> (shift + 1)) << shift) | (order & (tm - 1))
    slack = -a % MOE_WIN_ALIGN + MOE_WIN
    tok_sorted = jnp.concatenate([tok_of, jnp.zeros((slack,), jnp.int32)])
    blk = blk[:n_blocks]
    return dest.reshape(-1), tok_sorted, blk[:, 2], blk[:, 0], blk[:, 1], present[:n_experts, 0], tab[4, :1]


def hier_moe(streams, norm_g, wg, bg, we, be, w_gate, w_up, w_down, layer, final_g, final_norm, next_mods=None,
             routed=None):
    n_groups = wg.shape[1]
    n_experts = we.shape[1]
    if routed is None:
        wr, br = _router_weights(wg, bg, we, be)
        tok, eid, gates, tile0s = moe_pre([s[:3] for s in streams], norm_g, wr, br, n_groups, n_experts)
    else:
        (tok, eid, gates), tile0s = routed, [0]
    dest, tok_sorted, block_p0, block_valid, block_rank, present, n_chunks = _route_plan(eid, n_experts)
    os = moe_experts(tok, tok_sorted, block_p0, block_valid, block_rank, present, n_chunks, w_gate, w_up, w_down,
                     layer)
    next_mods = next_mods or [None] * len(streams)
    return [moe_combine(os, dest, gates, tile0, x, gt, final_g, final_norm, nm)
            for (x, _, _, gt), tile0, nm in zip(streams, tile0s, next_mods)]


def _s5_arrange(h):
    bsz, t, d = h.shape
    c = t // (S5_SEGS * S5_TAU)
    h = h.reshape(bsz, S5_SEGS, c, S5_TAU, d // LANES, LANES)
    return h.transpose(2, 0, 1, 4, 3, 5).reshape(c * bsz * S5_SEGS, d * S5_TAU)


def _s5_unarrange(y, bsz):
    r, w = y.shape
    d = w // S5_TAU
    c = r // (bsz * S5_SEGS)
    y = y.reshape(c, bsz, S5_SEGS, d // LANES, S5_TAU, LANES)
    return y.transpose(1, 2, 0, 4, 3, 5).reshape(bsz, S5_SEGS * c * S5_TAU, d)


def _s5_operators(a_re, a_im, log_step, b_re, b_im, c_re, c_im):
    n_g, n_p = a_re.shape[1:]
    n_h = b_re.shape[-1]
    gpt = LANES // n_h
    n_j = n_g // gpt
    tau = S5_TAU
    assert tau * n_h == LANES and 2 * n_p == LANES
    lam_step = lax.complex(a_re, a_im) * jnp.exp(log_step)[..., None]
    lam_bar = jnp.exp(lam_step)
    b_bar = ((lam_bar - 1.0) / lax.complex(a_re, a_im))[..., None] * lax.complex(b_re, b_im)
    c_mat = lax.complex(c_re, c_im)
    ks = jnp.arange(tau + 1, dtype=F32)[None, :, None, None]
    pw = jnp.exp(lam_step[:, None] * ks)
    ein = functools.partial(jnp.einsum, precision=HIGHEST)
    inj_c, cl_c, lt = [], [], []
    tz_c = 0.0
    for d in range(2):
        pos = np.arange(tau) if d == 0 else np.arange(tau)[::-1]
        powers = lambda ks: jnp.exp(lam_step[d][None] * jnp.asarray(ks, F32)[:, None, None])
        inj = (powers(tau - 1 - pos)[..., None] * b_bar[d][None]).reshape(tau, n_j, gpt, n_p, n_h)
        inj = inj.transpose(1, 0, 2, 4, 3).reshape(n_j, tau * LANES, n_p)
        inj_c.append(jnp.concatenate([inj.real, inj.imag], axis=-1))
        cl = (c_mat[d][None] * powers(pos + 1)[:, :, None, :]).reshape(tau, n_j, gpt, n_h, n_p)
        cl = cl.transpose(1, 2, 4, 0, 3).reshape(n_j, gpt * n_p, tau * n_h)
        cl_c.append(jnp.concatenate([cl.real, -cl.imag], axis=1))
        mk = ein('gop,kgp,gph->kgoh', c_mat[d], pw[d][:tau], b_bar[d]).real
        diff = pos[:, None] - pos[None, :]
        tz = jnp.where((diff >= 0)[:, :, None, None, None], mk[jnp.clip(diff, 0, tau - 1)], 0.0)
        tz = tz.reshape(tau, tau, n_j, gpt, n_h, n_h)
        tz_c = tz_c + tz.transpose(2, 1, 3, 5, 0, 4).reshape(n_j, tau * LANES, tau * n_h)
        lt_d = pw[d][tau].reshape(n_j, 1, gpt * n_p)
        lt.append(jnp.concatenate([lt_d.real, lt_d.imag], axis=-1))
    ws, wu, wh = s5_expand(jnp.stack(inj_c).astype(BF16), jnp.stack(cl_c).astype(BF16), tz_c.astype(BF16),
                           n_h, n_p)
    return ws, wu, wh, jnp.stack(lt).astype(F32)


def _s5_expand_body(inj_ref, cl_ref, tz_ref, ws_ref, wu_ref, wh_ref, *, n_h, n_p):
    rows = tz_ref.shape[0]
    gpt = LANES // n_h
    row = lax.broadcasted_iota(jnp.int32, (rows, LANES), 0)
    lane = lax.broadcasted_iota(jnp.int32, (rows, LANES), 1)
    sel_r = lax.broadcasted_iota(jnp.int32, (LANES, LANES), 0)
    sel_l = lax.broadcasted_iota(jnp.int32, (LANES, LANES), 1)
    grp_in = (row // n_h) % gpt
    grp_st = (row // n_p) % gpt

    def spread_out(m, t, grp_row):
        sel = jnp.logical_and(sel_r // n_h == t, sel_r % n_h == sel_l % n_h).astype(BF16)
        return jnp.where(grp_row == lane // n_h, _dot(m, sel), 0.0).astype(BF16)

    def spread_state(m, c, q, grp_row):
        sel = jnp.logical_and(sel_r // n_p == c, sel_r % n_p == sel_l % n_p).astype(BF16)
        return jnp.where(grp_row == (LANES // n_p) * q + lane // n_p, _dot(m, sel), 0.0).astype(BF16)

    w2 = 2 * gpt * n_p
    tz = tz_ref[...]
    for t in range(S5_TAU):
        wu_ref[:, t * LANES:(t + 1) * LANES] = spread_out(tz, t, grp_in)
    for d in range(2):
        cl = cl_ref[d]
        inj = inj_ref[d]
        for t in range(S5_TAU):
            wh_ref[d, :, t * LANES:(t + 1) * LANES] = spread_out(cl, t, grp_st)
        for c in range(2):
            for q in range(gpt * n_p // LANES):
                lo = d * w2 + c * gpt * n_p + q * LANES
                ws_ref[:, lo:lo + LANES] = spread_state(inj, c, q, grp_in)


def s5_expand(inj_c, cl_c, tz_c, n_h, n_p):
    n_j, rows, _ = tz_c.shape
    gpt = LANES // n_h
    w2 = 2 * gpt * n_p
    assert rows == S5_TAU * LANES == w2
    cspec = pl.BlockSpec((2, None, rows, LANES), lambda j: (0, j, 0, 0))
    return pl.pallas_call(
        functools.partial(_s5_expand_body, n_h=n_h, n_p=n_p),
        grid=(n_j,),
        in_specs=[cspec, cspec, pl.BlockSpec((None, rows, LANES), lambda j: (j, 0, 0))],
        out_specs=[pl.BlockSpec((None, rows, 2 * w2), lambda j: (j, 0, 0)),
                   pl.BlockSpec((None, rows, rows), lambda j: (j, 0, 0)),
                   pl.BlockSpec((2, None, w2, rows), lambda j: (0, j, 0, 0))],
        out_shape=[jax.ShapeDtypeStruct((n_j, rows, 2 * w2), BF16),
                   jax.ShapeDtypeStruct((n_j, rows, rows), BF16),
                   jax.ShapeDtypeStruct((2, n_j, w2, rows), BF16)],
        compiler_params=_cparams("parallel"),
        name="s5_expand",
    )(inj_c, cl_c, tz_c)


def _s5_inj_body(xc_ref, xl_ref, w_ref, oc_ref, ol_ref):
    w = w_ref[...]
    ol_ref[...] = _dot(xl_ref[...], w)

    @pl.when(pl.program_id(1) == 0)
    def _():
        oc_ref[...] = _dot(xc_ref[...], w)


def s5_inject(xr_c, xr_l, ws):
    r_c, r_l = xr_c.shape[0], xr_l.shape[0]
    n_j, k, n = ws.shape
    tm = r_l // 2 if r_l % 32 == 0 else r_l
    return pl.pallas_call(
        _s5_inj_body,
        grid=(n_j, r_l // tm),
        in_specs=[pl.BlockSpec((r_c, k), lambda j, i: (0, j)),
                  pl.BlockSpec((tm, k), lambda j, i: (i, j)),
                  pl.BlockSpec((None, k, n), lambda j, i: (j, 0, 0))],
        out_specs=[pl.BlockSpec((r_c, n), lambda j, i: (0, j)),
                   pl.BlockSpec((tm, n), lambda j, i: (i, j))],
        out_shape=[jax.ShapeDtypeStruct((r_c, n_j * n), F32), jax.ShapeDtypeStruct((r_l, n_j * n), F32)],
        compiler_params=_cparams("parallel", "arbitrary"),
        name="s5_inject",
    )(xr_c, xr_l, ws)


def _cmul(ar, ai, br, bi):
    return ar * br - ai * bi, ar * bi + ai * br


def _s5_scan_body(sc_ref, sl_ref, lt_ref, h_ref, raw_ref, *, n_ctx, n_lat, bsz):
    d = pl.program_id(1)
    w2 = lt_ref.shape[-1]
    w = w2 // 2
    rows = bsz * S5_SEGS
    seg = lax.broadcasted_iota(jnp.int32, (rows, 1), 0) % S5_SEGS
    is_late = seg != d
    lam_r = lt_ref[:, 0:w]
    lam_i = lt_ref[:, w:w2]
    zero = jnp.zeros((rows, w), F32)
    one = (jnp.ones((1, w), F32), jnp.zeros((1, w), F32))

    def swap_segments(x):
        return jnp.where(seg == 0, pltpu.roll(x, rows - 1, axis=0), pltpu.roll(x, 1, axis=0))

    def phase(s_ref, n_steps, hin_r, hin_i, write):
        def chunk(k):
            return jnp.where(d == 0, k, n_steps - 1 - k)

        def step_raw(k, carry):
            hr, hi = carry
            c = chunk(k)
            raw_ref[c, :, 0:w] = hr
            raw_ref[c, :, w:w2] = hi
            nr, ni = _cmul(lam_r, lam_i, hr, hi)
            return nr + s_ref[c, :, 0:w], ni + s_ref[c, :, w:w2]

        er, ei = lax.fori_loop(0, n_steps, step_raw, (zero, zero), unroll=S5_SCAN_UNROLL)
        pr, pi = lax.fori_loop(0, n_steps, lambda k, q: _cmul(lam_r, lam_i, *q), one)
        dr, di = _cmul(pr, pi, hin_r, hin_i)
        first_r = jnp.where(is_late, 0.0, er + dr)
        first_i = jnp.where(is_late, 0.0, ei + di)
        carry_r = jnp.where(is_late, swap_segments(first_r), hin_r)
        carry_i = jnp.where(is_late, swap_segments(first_i), hin_i)
        if write:
            def step_fix(k, q):
                c = chunk(k)
                fr, fi = _cmul(q[0], q[1], carry_r, carry_i)
                h_ref[c, :, 0:w] = (raw_ref[c, :, 0:w] + fr).astype(h_ref.dtype)
                h_ref[c, :, w:w2] = (raw_ref[c, :, w:w2] + fi).astype(h_ref.dtype)
                return _cmul(lam_r, lam_i, q[0], q[1])

            lax.fori_loop(0, n_steps, step_fix, one, unroll=S5_SCAN_UNROLL)
        lr, li = _cmul(pr, pi, carry_r, carry_i)
        last_r = jnp.where(is_late, er + lr, 0.0)
        last_i = jnp.where(is_late, ei + li, 0.0)
        return (jnp.where(is_late, 0.0, swap_segments(last_r)), jnp.where(is_late, 0.0, swap_segments(last_i)))

    hr, hi = phase(sc_ref, n_ctx, zero, zero, False)
    phase(sl_ref, n_lat, hr, hi, True)


def s5_scan(s_ctx, s_lat, lt, bsz):
    assert S5_SEGS == 2
    n_ctx, rows, _ = s_ctx.shape
    n_lat = s_lat.shape[0]
    n_j = lt.shape[1]
    w2 = lt.shape[-1]
    return pl.pallas_call(
        functools.partial(_s5_scan_body, n_ctx=n_ctx, n_lat=n_lat, bsz=bsz),
        grid=(n_j, 2),
        in_specs=[pl.BlockSpec((n_ctx, rows, w2), lambda j, d: (0, 0, 2 * j + d)),
                  pl.BlockSpec((n_lat, rows, w2), lambda j, d: (0, 0, 2 * j + d)),
                  pl.BlockSpec((None, None, 1, w2), lambda j, d: (d, j, 0, 0))],
        out_specs=pl.BlockSpec((None, n_lat, rows, w2), lambda j, d: (d, 0, 0, j)),
        out_shape=jax.ShapeDtypeStruct((2, n_lat, rows, n_j * w2), BF16),
        scratch_shapes=[pltpu.VMEM((max(n_ctx, n_lat), rows, w2), F32)],
        compiler_params=_cparams("parallel", "parallel"),
        name="s5_scan",
    )(s_ctx, s_lat, lt)


def _s5_out_body(x_ref, hf_ref, hb_ref, wu_ref, whf_ref, whb_ref, o_ref):
    o_ref[...] = (_dot(x_ref[...], wu_ref[...]) + _dot(hf_ref[...], whf_ref[...])
                  + _dot(hb_ref[...], whb_ref[...])).astype(o_ref.dtype)


def s5_readout(xr, h, wu, wh):
    r = xr.shape[0]
    n_j, k, n = wu.shape
    w2 = wh.shape[2]
    tm = min(r, 1024)
    return pl.pallas_call(
        _s5_out_body,
        grid=(n_j, r // tm),
        in_specs=[pl.BlockSpec((tm, k), lambda j, i: (i, j)),
                  pl.BlockSpec((None, tm, w2), lambda j, i: (0, i, j)),
                  pl.BlockSpec((None, tm, w2), lambda j, i: (1, i, j)),
                  pl.BlockSpec((None, k, n), lambda j, i: (j, 0, 0)),
                  pl.BlockSpec((None, None, w2, n), lambda j, i: (0, j, 0, 0)),
                  pl.BlockSpec((None, None, w2, n), lambda j, i: (1, j, 0, 0))],
        out_specs=pl.BlockSpec((tm, n), lambda j, i: (i, j)),
        out_shape=jax.ShapeDtypeStruct((r, n_j * n), BF16),
        compiler_params=_cparams("parallel", "parallel"),
        name="s5_readout",
    )(xr, h, h, wu, wh, wh)


def _gelu_tanh(x):
    return 0.5 * x * (1.0 + jnp.tanh(math.sqrt(2.0 / math.pi) * (x + 0.044715 * (x * x * x))))


def _s5_glu_body(x_ref, y_ref, g_ref, sh_ref, sc_ref, dk_ref, w1_ref, w2_ref, b1_ref, b2_ref, gt_ref,
                 mg_ref, msh_ref, msc_ref, wr_ref, br_ref, o_ref, tok_ref, eid_ref, gate_ref, *, n_groups, n_experts):
    x = x_ref[...]
    u = _norm_mod(x, g_ref[...], sh_ref[...], sc_ref[...])
    y = _gelu_tanh(y_ref[...].astype(F32) + dk_ref[...] * u).astype(BF16)
    o = (_dot(y, w1_ref[...]) + b1_ref[...]) * jax.nn.sigmoid(_dot(y, w2_ref[...]) + b2_ref[...])
    xn = x + gt_ref[...] * o
    o_ref[...] = xn
    tok = _norm_mod(xn, mg_ref[...], msh_ref[...], msc_ref[...])
    _route_tokens(tok, wr_ref, br_ref, tok_ref, eid_ref, gate_ref, n_groups, n_experts)


def s5_glu(x, y, g, shift, scale, d_skip, w1, b1, w2, b2, gate, moe_g, moe_shift, moe_scale, wr, br, n_groups,
           n_experts):
    bsz, seq, d = x.shape
    tm = min(seq, 512)
    nt = seq // tm
    sub = tm // MOE_TM
    row = lambda a: a.reshape(1, d)
    rspec = pl.BlockSpec((1, d), lambda b, i: (0, 0))
    mspec = pl.BlockSpec((None, 1, d), lambda b, i: (b, 0, 0))
    tile = pl.BlockSpec((None, tm, d), lambda b, i: (b, i, 0))
    wspec = pl.BlockSpec((d, d), lambda b, i: (0, 0), pipeline_mode=pl.Buffered(1))
    return pl.pallas_call(
        functools.partial(_s5_glu_body, n_groups=n_groups, n_experts=n_experts),
        grid=(bsz, nt),
        in_specs=[tile, tile, rspec, mspec, mspec, rspec, wspec, wspec, rspec, rspec, mspec,
                  rspec, mspec, mspec,
                  pl.BlockSpec((2, d, LANES), lambda b, i: (0, 0, 0)),
                  pl.BlockSpec((1, LANES), lambda b, i: (0, 0))],
        out_specs=[tile,
                   pl.BlockSpec((tm, d), lambda b, i: (b * nt + i, 0)),
                   pl.BlockSpec((sub, SUBLANES, MOE_TM), lambda b, i: (b * nt + i, 0, 0)),
                   pl.BlockSpec((tm, LANES), lambda b, i: (b * nt + i, 0))],
        out_shape=[jax.ShapeDtypeStruct((bsz, seq, d), F32),
                   jax.ShapeDtypeStruct((bsz * seq, d), F32),
                   jax.ShapeDtypeStruct((bsz * seq // MOE_TM, SUBLANES, MOE_TM), jnp.int32),
                   jax.ShapeDtypeStruct((bsz * seq, LANES), F32)],
        compiler_params=_cparams("parallel", "parallel"),
        name="s5_glu",
    )(x, y, row(g), shift, scale, row(d_skip), w1, w2, row(b1), row(b2), gate,
      row(moe_g), moe_shift, moe_scale, _split_bf16(wr), br)


def s5_mix(xl, hl, hc, g, sh_l, sc_l, gate_l, a_re, a_im, log_step, b_re, b_im, c_re, c_im, d_skip,
           w1, b1, w2, b2, moe_mod, router):
    bsz, seq, d = xl.shape
    xr_c = _s5_arrange(hc)
    xr_l = _s5_arrange(hl)
    rows = bsz * S5_SEGS
    n_ctx = xr_c.shape[0] // rows
    n_lat = xr_l.shape[0] // rows
    ws, wu, wh, lt = _s5_operators(a_re, a_im, log_step, b_re, b_im, c_re, c_im)
    s_ctx, s_lat = s5_inject(xr_c, xr_l, ws)
    h = s5_scan(s_ctx.reshape(n_ctx, rows, -1), s_lat.reshape(n_lat, rows, -1), lt, bsz)
    y = s5_readout(xr_l, h.reshape(2, n_lat * rows, -1), wu, wh)
    y = _s5_unarrange(y, bsz)
    wr, br = _router_weights(*router)
    xl, tok, eid, gates = s5_glu(xl, y, g, sh_l, sc_l, d_skip, w1.astype(BF16), b1, w2.astype(BF16), b2, gate_l,
                                 *moe_mod, wr, br, router[0].shape[1], router[2].shape[1])
    return xl, (tok, eid, gates)


def hyena_mix(x, g, shift, scale, gate, w_in, b_in, conv_w, conv_b, fw1, fb1, fw2, fb2, fw3, freq, skip,
              w_out, b_out):
    seq = x.shape[1]
    if seq >= FFT_MIN_SEQ and (2 * seq) % (2 * FFT_N2) == 0:
        a, dd, _ = hyena_filter_taps(seq, fw1, fb1, fw2, fb2, fw3, freq, F32)
        v, x0 = hyena_in(x, g, shift, scale, w_in, b_in, conv_w, conv_b, F32)
        yg = hyena_conv_fft(v, x0, skip, a, dd)
    else:
        cmat, smat = dft_matrices(seq)
        a, dd, kn = hyena_filter_taps(seq, fw1, fb1, fw2, fb2, fw3, freq, BF16)
        kr, ki = hyena_filter_dft(a, dd, cmat, smat)
        v, x0 = hyena_in(x, g, shift, scale, w_in, b_in, conv_w, conv_b, BF16)
        yg = hyena_conv(v, x0, skip, kr, ki, kn, cmat, smat)
    return mm_residual(yg, w_out, b_out, x, gate)


def kernel(x, c, ctx, c_ctx, ada_w, ada_b, norm_g, final_g, hy_w_in, hy_b_in, hy_conv_w, hy_conv_b, hy_fw1,
           hy_fb1, hy_fw2, hy_fb2, hy_fw3, hy_freq, hy_skip, hy_w_out, hy_b_out, s5_a_re, s5_a_im,
           s5_log_step, s5_b_re, s5_b_im, s5_c_re, s5_c_im, s5_d, s5_w1, s5_b1, s5_w2, s5_b2, moe_wg, moe_bg,
           moe_we, moe_be, moe_w_gate, moe_w_up, moe_w_down):
    bsz, _, d = x.shape
    depth = ada_w.shape[0]
    assert depth == 2 and bsz < SUBLANES
    c_all = jnp.concatenate([c, c_ctx[None, :], jnp.zeros((SUBLANES - bsz - 1, d), F32)], axis=0)
    mods = ada_mod(c_all, ada_w, ada_b)

    def mod_rows(layer, k):
        lat = mods[layer, :bsz, k * d:(k + 1) * d][:, None, :]
        cx = jnp.broadcast_to(mods[layer, bsz, k * d:(k + 1) * d][None, None, :], (bsz, 1, d))
        return lat, cx

    (sh_a, csh_a), (sc_a, csc_a), (gt_a, cgt_a) = mod_rows(0, 0), mod_rows(0, 1), mod_rows(0, 2)
    (sh_f, csh_f), (sc_f, csc_f), (gt_f, cgt_f) = mod_rows(0, 3), mod_rows(0, 4), mod_rows(0, 5)
    hy = (hy_w_in[0].astype(BF16), hy_b_in[0], hy_conv_w[0], hy_conv_b[0], hy_fw1[0], hy_fb1[0], hy_fw2[0],
          hy_fb2[0], hy_fw3[0], hy_freq[0], hy_skip[0], hy_w_out[0].astype(BF16), hy_b_out[0])
    xl = hyena_mix(x, norm_g[0, 0], sh_a, sc_a, gt_a, *hy)
    xc = hyena_mix(ctx, norm_g[0, 0], csh_a, csc_a, cgt_a, *hy)
    (sh_a, csh_a), (sc_a, csc_a), (gt_a, _) = mod_rows(1, 0), mod_rows(1, 1), mod_rows(1, 2)
    (xl, hl), (_, hc) = hier_moe([(xl, sh_f, sc_f, gt_f), (xc, csh_f, csc_f, cgt_f)], norm_g[0, 1],
                                 moe_wg[0], moe_bg[0], moe_we[0], moe_be[0], moe_w_gate, moe_w_up, moe_w_down, 0,
                                 final_g, False,
                                 next_mods=[(norm_g[1, 0], sh_a, sc_a), (norm_g[1, 0], csh_a, csc_a)])
    (sh_f, _), (sc_f, _), (gt_f, _) = mod_rows(1, 3), mod_rows(1, 4), mod_rows(1, 5)
    router = (moe_wg[1], moe_bg[1], moe_we[1], moe_be[1])
    xl, routed = s5_mix(xl, hl, hc, norm_g[1, 0], sh_a, sc_a, gt_a, s5_a_re[0], s5_a_im[0], s5_log_step[0],
                        s5_b_re[0], s5_b_im[0], s5_c_re[0], s5_c_im[0], s5_d[0], s5_w1[0], s5_b1[0], s5_w2[0],
                        s5_b2[0], (norm_g[1, 1], sh_f, sc_f), router)
    (out,) = hier_moe([(xl, sh_f, sc_f, gt_f)], norm_g[1, 1], *router, moe_w_gate, moe_w_up, moe_w_down, 1,
                      final_g, True, routed=routed)
    return out
```

```python
import functools
import math

import numpy as np
import jax
import jax.numpy as jnp
from jax import lax
from jax.experimental import pallas as pl
from jax.experimental.pallas import tpu as pltpu

F32 = jnp.float32
BF16 = jnp.bfloat16
HIGHEST = lax.Precision.HIGHEST

NORM_EPS = 1e-6
HY_DECAY_TARGET = 1e-2
HY_FAST_PCT = 0.3
HY_SLOW_PCT = 1.5
TOP_K = 2

V7X_VMEM_LIMIT_BYTES = 56 * 1024 * 1024
LANES = 128
SUBLANES = 8
S5_TAU = 8
S5_SEGS = 2
S5_SCAN_UNROLL = 8
MOE_TM = 256
MOE_BM = 256
MOE_NCH = 12
MOE_STAGE = 4
MOE_CAST_ROWS = 128
MOE_WIN_ALIGN = 1024
MOE_WIN = 2 * MOE_WIN_ALIGN
NEG_BIG = -1e30


def _cparams(*sem):
    return pltpu.CompilerParams(dimension_semantics=sem, vmem_limit_bytes=V7X_VMEM_LIMIT_BYTES)


def _norm_mod(x, g, shift, scale):
    ms = jnp.mean(x * x, axis=-1, keepdims=True)
    return (x * lax.rsqrt(ms + NORM_EPS) * g) * (1.0 + scale) + shift


def _dot(a, b):
    return jnp.dot(a, b, preferred_element_type=F32)


def _ada_body(c_ref, w_ref, b_ref, o_ref):
    x = c_ref[...]
    s = (x * jax.nn.sigmoid(x)).astype(BF16)
    o_ref[...] = _dot(s, w_ref[...].astype(BF16)) + b_ref[...]


def ada_mod(c_all, ada_w, ada_b):
    depth, d, n = ada_w.shape
    tn = min(n, 1024)
    return pl.pallas_call(
        _ada_body,
        grid=(depth, n // tn),
        in_specs=[pl.BlockSpec((SUBLANES, d), lambda l, j: (0, 0)),
                  pl.BlockSpec((None, d, tn), lambda l, j: (l, 0, j)),
                  pl.BlockSpec((None, 1, tn), lambda l, j: (l, 0, j))],
        out_specs=pl.BlockSpec((None, SUBLANES, tn), lambda l, j: (l, 0, j)),
        out_shape=jax.ShapeDtypeStruct((depth, SUBLANES, n), F32),
        compiler_params=_cparams("parallel", "parallel"),
        name="ada_mod",
    )(c_all, ada_w, ada_b.reshape(depth, 1, n))


def _hy_in_body(xp_ref, xm_ref, xn_ref, g_ref, sh_ref, sc_ref,
                w0_ref, w1_ref, w2_ref, b0_ref, b1_ref, b2_ref,
                cw0_ref, cw1_ref, cw2_ref, cb0_ref, cb1_ref, cb2_ref,
                v_ref, x0_ref):
    i = pl.program_id(2)
    ni = pl.num_programs(2)
    tm = xm_ref.shape[0]
    x = jnp.concatenate([xp_ref[...], xm_ref[...], xn_ref[...]], axis=0)
    h = _norm_mod(x, g_ref[...], sh_ref[...], sc_ref[...]).astype(BF16)
    rows = lax.broadcasted_iota(jnp.int32, (tm + 2 * SUBLANES, 1), 0)
    valid = jnp.logical_and(jnp.logical_or(rows >= SUBLANES, i > 0),
                            jnp.logical_or(rows < tm + SUBLANES, i < ni - 1))

    def part(w_ref, b_ref, cw_ref, cb_ref):
        z = jnp.where(valid, _dot(h, w_ref[...]) + b_ref[...], 0.0)
        cw = cw_ref[...]
        zp = pltpu.roll(z, 1, axis=0)[SUBLANES:tm + SUBLANES]
        zn = pltpu.roll(z, tm + 2 * SUBLANES - 1, axis=0)[SUBLANES:tm + SUBLANES]
        return zp * cw[0:1] + z[SUBLANES:tm + SUBLANES] * cw[1:2] + zn * cw[2:3] + cb_ref[...]

    x0 = part(w0_ref, b0_ref, cw0_ref, cb0_ref)
    x1 = part(w1_ref, b1_ref, cw1_ref, cb1_ref)
    v = part(w2_ref, b2_ref, cw2_ref, cb2_ref) * x1
    v_ref[...] = v.astype(v_ref.dtype)
    x0_ref[...] = x0.astype(BF16)


def hyena_in(x, g, shift, scale, w_in, b_in, conv_w, conv_b, v_dtype):
    bsz, seq, d = x.shape
    tm = min(seq, 512)
    tn = min(d, 1024)
    nj = d // tn
    r8 = tm // SUBLANES
    last8 = seq // SUBLANES - 1
    row = lambda a: a.reshape(1, -1)
    wspec = lambda k: pl.BlockSpec((d, tn), lambda j, b, i: (0, k * nj + j))
    rspec = lambda k: pl.BlockSpec((1, tn), lambda j, b, i: (0, k * nj + j))
    cspec = lambda k: pl.BlockSpec((3, tn), lambda j, b, i: (0, k * nj + j))
    mspec = pl.BlockSpec((None, 1, d), lambda j, b, i: (b, 0, 0))
    out_spec = pl.BlockSpec((None, tm, tn), lambda j, b, i: (b, i, j))
    return pl.pallas_call(
        _hy_in_body,
        grid=(nj, bsz, seq // tm),
        in_specs=[pl.BlockSpec((None, SUBLANES, d), lambda j, b, i: (b, jnp.maximum(i * r8 - 1, 0), 0)),
                  pl.BlockSpec((None, tm, d), lambda j, b, i: (b, i, 0)),
                  pl.BlockSpec((None, SUBLANES, d), lambda j, b, i: (b, jnp.minimum((i + 1) * r8, last8), 0)),
                  pl.BlockSpec((1, d), lambda j, b, i: (0, 0)), mspec, mspec,
                  wspec(0), wspec(1), wspec(2), rspec(0), rspec(1), rspec(2),
                  cspec(0), cspec(1), cspec(2), rspec(0), rspec(1), rspec(2)],
        out_specs=[out_spec, out_spec],
        out_shape=[jax.ShapeDtypeStruct((bsz, seq, d), v_dtype), jax.ShapeDtypeStruct((bsz, seq, d), BF16)],
        compiler_params=_cparams("parallel", "parallel", "parallel"),
        name="hyena_in",
    )(x, x, x, row(g), shift, scale, w_in, w_in, w_in, row(b_in), row(b_in), row(b_in),
      conv_w, conv_w, conv_w, row(conv_b), row(conv_b), row(conv_b))


def _dft_tables(seq, blk):
    n = 2 * seq
    s = np.arange(seq, dtype=np.int64)[None, :]
    fl = np.arange(blk, dtype=np.int64)[:, None]
    fh = (np.arange(seq // blk, dtype=np.int64) * blk)[:, None]
    w = 2.0 * math.pi / n
    ang_b = ((fl * s) % n) * w
    ang_a = ((fh * s) % n) * w
    f32 = lambda m: jnp.asarray(m.astype(np.float32))
    return (f32(np.cos(ang_a)[:, None, :]), f32(np.sin(ang_a)[:, None, :]), f32(np.cos(ang_b)), f32(np.sin(ang_b)))


def _dft_gen_body(ca_ref, sa_ref, cb_ref, sb_ref, c_ref, s_ref):
    ca, sa, cb, sb = ca_ref[...], sa_ref[...], cb_ref[...], sb_ref[...]
    c_ref[...] = (ca * cb - sa * sb).astype(BF16)
    s_ref[...] = (sa * cb + ca * sb).astype(BF16)


def dft_matrices(seq):
    blk = min(seq, 256)
    ca, sa, cb, sb = _dft_tables(seq, blk)
    aspec = pl.BlockSpec((None, 1, seq), lambda i: (i, 0, 0))
    bspec = pl.BlockSpec((blk, seq), lambda i: (0, 0))
    ospec = pl.BlockSpec((blk, seq), lambda i: (i, 0))
    return pl.pallas_call(
        _dft_gen_body,
        grid=(seq // blk,),
        in_specs=[aspec, aspec, bspec, bspec],
        out_specs=[ospec, ospec],
        out_shape=[jax.ShapeDtypeStruct((seq, seq), BF16)] * 2,
        compiler_params=_cparams("parallel"),
        name="dft_matrices",
    )(ca, sa, cb, sb)


def _alt_sign(rows):
    return jnp.where((rows & 1) == 0, 1.0, -1.0).astype(F32)


def _filt_body(h2_ref, wf_ref, wb_ref, dl_ref, a_ref, d_ref, ny_ref):
    seq = h2_ref.shape[0]
    h2 = h2_ref[...]
    row = lax.broadcasted_iota(jnp.int32, (seq, 1), 0)
    t = row.astype(F32) * (1.0 / (seq - 1))
    win = jnp.exp(-t * dl_ref[...])
    h_hi = h2.astype(BF16)
    h_lo = (h2 - h_hi.astype(F32)).astype(BF16)
    dot3 = lambda w_ref: _dot(h_hi, w_ref[0]) + _dot(h_hi, w_ref[1]) + _dot(h_lo, w_ref[0])
    hf = dot3(wf_ref) * win
    hb = dot3(wb_ref) * win
    hb = jnp.where(row == 0, 0.0, hb)
    nrm = (jnp.sum(jnp.abs(hf), axis=0, keepdims=True) + jnp.sum(jnp.abs(hb), axis=0, keepdims=True))
    inv = 1.0 / nrm
    a = (hf + hb) * inv
    a_ref[...] = a.astype(a_ref.dtype)
    d_ref[...] = ((hb - hf) * inv).astype(d_ref.dtype)
    ny = jnp.sum(a * _alt_sign(row), axis=0, keepdims=True) * (1.0 / (2 * seq))
    ny_ref[...] = jnp.broadcast_to(ny, ny_ref.shape)


def _khat_body(a_ref, d_ref, c_ref, s_ref, kr_ref, ki_ref):
    i = pl.program_id(1)
    tm = c_ref.shape[0]
    seq = c_ref.shape[1]
    f = i * tm + lax.broadcasted_iota(jnp.int32, (tm, 1), 0)
    w = jnp.where(f == 0, 1.0, 2.0).astype(F32) * (1.0 / (2 * seq))
    kr_ref[...] = _dot(c_ref[...], a_ref[...]) * w
    ki_ref[...] = _dot(s_ref[...], d_ref[...]) * w


def hyena_filter_taps(seq, fw1, fb1, fw2, fb2, fw3, freq, taps_dtype):
    d = fw3.shape[1] // 2
    bands_n = (fw1.shape[0] - 1) // 2
    t = np.linspace(0.0, 1.0, seq)[:, None]
    w = (2.0 * math.pi / seq) * np.arange(seq)[:, None]
    bands = np.linspace(1e-4, bands_n - 1, bands_n)[None, :]
    z = jnp.asarray(np.concatenate([t, np.cos(bands * w), -np.sin(bands * w)], axis=-1).astype(np.float32))
    h = jnp.sin(freq * (jnp.dot(z, fw1, precision=HIGHEST) + fb1))
    h2 = jnp.sin(freq * (jnp.dot(h, fw2, precision=HIGHEST) + fb2))
    max_decay = math.log(HY_DECAY_TARGET) / HY_FAST_PCT
    min_decay = math.log(HY_DECAY_TARGET) / HY_SLOW_PCT
    deltas = jnp.abs(jnp.linspace(min_decay, max_decay, d, dtype=F32))[None, :]

    order = h2.shape[1]
    fw3_split = _split_bf16(fw3)
    tn = min(d, 256)
    nj = d // tn
    return pl.pallas_call(
        _filt_body,
        grid=(nj,),
        in_specs=[pl.BlockSpec((seq, order), lambda j: (0, 0)),
                  pl.BlockSpec((2, order, tn), lambda j: (0, 0, j)),
                  pl.BlockSpec((2, order, tn), lambda j: (0, 0, nj + j)),
                  pl.BlockSpec((1, tn), lambda j: (0, j))],
        out_specs=[pl.BlockSpec((seq, tn), lambda j: (0, j)),
                   pl.BlockSpec((seq, tn), lambda j: (0, j)),
                   pl.BlockSpec((SUBLANES, tn), lambda j: (0, j))],
        out_shape=[jax.ShapeDtypeStruct((seq, d), taps_dtype), jax.ShapeDtypeStruct((seq, d), taps_dtype),
                   jax.ShapeDtypeStruct((SUBLANES, d), F32)],
        compiler_params=_cparams("parallel"),
        name="hyena_filter_taps",
    )(h2, fw3_split, fw3_split, deltas)


def hyena_filter_dft(a, dd, cmat, smat):
    seq, d = a.shape
    tm = min(seq, 512)
    tn2 = min(d, 512)
    return pl.pallas_call(
        _khat_body,
        grid=(d // tn2, seq // tm),
        in_specs=[pl.BlockSpec((seq, tn2), lambda j, i: (0, j)),
                  pl.BlockSpec((seq, tn2), lambda j, i: (0, j)),
                  pl.BlockSpec((tm, seq), lambda j, i: (i, 0)),
                  pl.BlockSpec((tm, seq), lambda j, i: (i, 0))],
        out_specs=[pl.BlockSpec((tm, tn2), lambda j, i: (i, j))] * 2,
        out_shape=[jax.ShapeDtypeStruct((seq, d), F32)] * 2,
        compiler_params=_cparams("parallel", "parallel"),
        name="hyena_filter_dft",
    )(a, dd, cmat, smat)


def _dft_fwd_body(v_ref, c_ref, s_ref, kr_ref, ki_ref, kn_ref, ya_ref, yb_ref, yn_ref):
    i = pl.program_id(2)
    v = v_ref[...]
    vr = _dot(c_ref[...], v)
    p = _dot(s_ref[...], v)
    kr = kr_ref[...]
    ki = ki_ref[...]
    ya_ref[...] = (vr * kr + p * ki).astype(BF16)
    yb_ref[...] = (p * kr - vr * ki).astype(BF16)

    @pl.when(i == 0)
    def _():
        seq = v.shape[0]
        row = lax.broadcasted_iota(jnp.int32, (seq, 1), 0)
        vl = jnp.sum(v.astype(F32) * _alt_sign(row), axis=0, keepdims=True)
        yn_ref[...] = jnp.broadcast_to(vl * kn_ref[0:1, :], yn_ref.shape)


def _dft_inv_body(ya_ref, yb_ref, c_ref, s_ref, v_ref, x0_ref, skip_ref, yn_ref, o_ref):
    i = pl.program_id(2)
    tm = c_ref.shape[0]
    acc = _dot(c_ref[...], ya_ref[...]) + _dot(s_ref[...], yb_ref[...])
    t = i * tm + lax.broadcasted_iota(jnp.int32, (tm, 1), 0)
    y = acc + _alt_sign(t) * yn_ref[0:1, :] + skip_ref[...] * v_ref[...].astype(F32)
    o_ref[...] = (y * x0_ref[...].astype(F32)).astype(BF16)


def hyena_conv(v, x0, skip, kr, ki, kn, cmat, smat):
    bsz, seq, d = v.shape
    tm = min(seq, 512)
    tn = min(d, 512)
    grid = (bsz, d // tn, seq // tm)
    full = pl.BlockSpec((None, seq, tn), lambda b, j, i: (b, 0, j))
    mat = pl.BlockSpec((tm, seq), lambda b, j, i: (i, 0))
    tile = pl.BlockSpec((None, tm, tn), lambda b, j, i: (b, i, j))
    ktile = pl.BlockSpec((tm, tn), lambda b, j, i: (i, j))
    nyq = pl.BlockSpec((None, SUBLANES, tn), lambda b, j, i: (b, 0, j))
    ya, yb, yn = pl.pallas_call(
        _dft_fwd_body,
        grid=grid,
        in_specs=[full, mat, mat, ktile, ktile, pl.BlockSpec((SUBLANES, tn), lambda b, j, i: (0, j))],
        out_specs=[tile, tile, nyq],
        out_shape=[jax.ShapeDtypeStruct((bsz, seq, d), BF16)] * 2
        + [jax.ShapeDtypeStruct((bsz, SUBLANES, d), F32)],
        compiler_params=_cparams("parallel", "parallel", "arbitrary"),
        name="hyena_dft_fwd",
    )(v, cmat, smat, kr, ki, kn)
    return pl.pallas_call(
        _dft_inv_body,
        grid=grid,
        in_specs=[full, full, mat, mat, tile, tile, pl.BlockSpec((1, tn), lambda b, j, i: (0, j)), nyq],
        out_specs=tile,
        out_shape=jax.ShapeDtypeStruct((bsz, seq, d), BF16),
        compiler_params=_cparams("parallel", "parallel", "parallel"),
        name="hyena_dft_inv",
    )(ya, yb, cmat, smat, v, x0, skip.reshape(1, d), yn)


FFT_N2 = 128
FFT_MIN_SEQ = 1024
FFT_UNROLL = 8


def _unroll_for(trips, cap):
    return max(u for u in range(1, cap + 1) if trips % u == 0)


def _fft_matrices(seq):
    n = 2 * seq
    n2 = FFT_N2
    n1 = n // n2
    r8 = SUBLANES
    q = np.arange(n2 // r8, dtype=np.int64)[:, None, None, None]
    nf = n1 // 2 + 1
    f1 = np.arange(nf, dtype=np.int64)[None, :, None, None]
    r = np.arange(r8, dtype=np.int64)[None, None, :, None]
    t1 = np.arange(n1 // 2, dtype=np.int64)[None, None, None, :]
    ang = ((f1 * (t1 * n2 + q * r8 + r)) % n) * (2.0 * math.pi / n)
    g = np.stack([np.cos(ang), -np.sin(ang)], axis=3)
    eye = np.eye(r8)[None, None, :, None, None, :]
    ma = (g[..., None] * eye).reshape(n2 // r8, nf * r8 * 2, (n1 // 2) * r8).astype(np.float32)
    f2 = np.arange(n2, dtype=np.int64)[:, None]
    t2 = np.arange(n2, dtype=np.int64)[None, :]
    th = ((f2 * t2) % n2) * (2.0 * math.pi / n2)
    co, si = np.cos(th), np.sin(th)
    wc = np.stack([np.stack([co, si], axis=-1), np.stack([-si, co], axis=-1)], axis=0)
    wc = wc.reshape(2 * n2, 2 * n2).astype(np.float32)
    as_bf16 = lambda m: jnp.asarray(np.ascontiguousarray(m).astype(BF16))
    return as_bf16(ma), as_bf16(np.swapaxes(ma, 1, 2)), as_bf16(wc), as_bf16(wc.T)


def _fft_stage_a(x_ref, ma_ref, s1):
    n1h, n_q, r8, tn = x_ref.shape
    nf = s1.shape[0]

    def body(q, carry):
        x = x_ref[:, pl.ds(q, 1), :, :].reshape(n1h * r8, tn).astype(BF16)
        a = _dot(ma_ref[q], x).astype(BF16)
        s1[:, pl.ds(pl.multiple_of(q * 2 * r8, 2 * r8), 2 * r8), :] = a.reshape(nf, 2 * r8, tn)
        return carry

    lax.fori_loop(0, n_q, body, 0, unroll=FFT_UNROLL)


def _fft_conv_body(v_ref, x0_ref, k_ref, skip_ref, ma_ref, mat_ref, wc_ref, wci_ref, o_ref, s1, ysc):
    n1h, n_q, r8, tn = v_ref.shape
    nf = s1.shape[0]
    n2 = s1.shape[1] // 2
    seq = n1h * n_q * r8
    _fft_stage_a(v_ref, ma_ref, s1)

    def slab(f, carry):
        y = _dot(wc_ref[...], s1[f])
        yr, yi = y[:n2], y[n2:]
        kr = k_ref[f, 0].astype(F32)
        ki = k_ref[f, 1].astype(F32)
        p = jnp.concatenate([yr * kr - yi * ki, yr * ki + yi * kr], axis=0).astype(BF16)
        s1[f] = _dot(wci_ref[...], p).astype(BF16)
        return carry

    lax.fori_loop(0, nf, slab, 0, unroll=_unroll_for(nf, 2 * FFT_UNROLL))

    def inv_a(q, carry):
        z = s1[:, pl.ds(pl.multiple_of(q * 2 * r8, 2 * r8), 2 * r8), :].reshape(nf * 2 * r8, tn)
        ysc[:, pl.ds(q, 1), :, :] = _dot(mat_ref[q], z).reshape(n1h, 1, r8, tn)
        return carry

    lax.fori_loop(0, n_q, inv_a, 0, unroll=FFT_UNROLL)
    y = ysc[...].reshape(seq, tn) + skip_ref[...] * v_ref[...].reshape(seq, tn)
    o_ref[...] = (y * x0_ref[...].astype(F32)).astype(BF16)


def _fft_filter_body(a_ref, d_ref, ma_ref, wc_ref, k_ref, s1):
    nf = s1.shape[0]
    n2 = s1.shape[1] // 2
    scale = 1.0 / (2 * (nf - 1) * n2)
    for src_ref, part, sign in ((a_ref, 0, scale), (d_ref, 1, -scale)):
        _fft_stage_a(src_ref, ma_ref, s1)

        def slab(f, carry):
            y = _dot(wc_ref[part * n2:(part + 1) * n2, :], s1[f])
            mirrored = jnp.logical_and(f > 0, f < nf - 1)
            k_ref[f, part] = (y * (sign * jnp.where(mirrored, 2.0, 1.0))).astype(BF16)
            return carry

        lax.fori_loop(0, nf, slab, 0, unroll=_unroll_for(nf, FFT_UNROLL))


def hyena_conv_fft(v, x0, skip, a, dd):
    bsz, seq, d = v.shape
    n2 = FFT_N2
    n1 = 2 * seq // n2
    nf = n1 // 2 + 1
    n_q = n2 // SUBLANES
    tn = min(d, 256)
    ma, mat, wc, wci = _fft_matrices(seq)
    const = lambda shape: pl.BlockSpec(shape, lambda *_: (0,) * len(shape), pipeline_mode=pl.Buffered(1))
    view = lambda t: t.reshape(t.shape[:-2] + (n1 // 2, n_q, SUBLANES, d))
    tap = pl.BlockSpec((n1 // 2, n_q, SUBLANES, tn), lambda j: (0, 0, 0, j))
    khat = pl.pallas_call(
        _fft_filter_body,
        grid=(d // tn,),
        in_specs=[tap, tap, const(ma.shape), const(wc.shape)],
        out_specs=pl.BlockSpec((nf, 2, n2, tn), lambda j: (0, 0, 0, j)),
        out_shape=jax.ShapeDtypeStruct((nf, 2, n2, d), BF16),
        scratch_shapes=[pltpu.VMEM((nf, 2 * n2, tn), BF16)],
        compiler_params=_cparams("parallel"),
        name="hyena_filter_fft",
    )(view(a), view(dd), ma, wc)
    return pl.pallas_call(
        _fft_conv_body,
        grid=(d // tn, bsz),
        in_specs=[pl.BlockSpec((None, n1 // 2, n_q, SUBLANES, tn), lambda j, b: (b, 0, 0, 0, j)),
                  pl.BlockSpec((None, seq, tn), lambda j, b: (b, 0, j)),
                  pl.BlockSpec((nf, 2, n2, tn), lambda j, b: (0, 0, 0, j), pipeline_mode=pl.Buffered(1)),
                  pl.BlockSpec((1, tn), lambda j, b: (0, j)),
                  const(ma.shape), const(mat.shape), const(wc.shape), const(wci.shape)],
        out_specs=pl.BlockSpec((None, seq, tn), lambda j, b: (b, 0, j)),
        out_shape=jax.ShapeDtypeStruct((bsz, seq, d), BF16),
        scratch_shapes=[pltpu.VMEM((nf, 2 * n2, tn), BF16), pltpu.VMEM((n1 // 2, n_q, SUBLANES, tn), F32)],
        compiler_params=_cparams("parallel", "arbitrary"),
        name="hyena_conv_fft",
    )(view(v), x0, khat, skip.reshape(1, d), ma, mat, wc, wci)


def _mm_res_body(x_ref, w_ref, b_ref, res_ref, gate_ref, o_ref):
    o_ref[...] = res_ref[...] + gate_ref[...] * (_dot(x_ref[...], w_ref[...]) + b_ref[...])


def mm_residual(x, w, b, res, gate):
    bsz, seq, k = x.shape
    n = w.shape[1]
    tm = min(seq, 512)
    return pl.pallas_call(
        _mm_res_body,
        grid=(bsz, seq // tm),
        in_specs=[pl.BlockSpec((None, tm, k), lambda b, i: (b, i, 0)),
                  pl.BlockSpec((k, n), lambda b, i: (0, 0)),
                  pl.BlockSpec((1, n), lambda b, i: (0, 0)),
                  pl.BlockSpec((None, tm, n), lambda b, i: (b, i, 0)),
                  pl.BlockSpec((None, 1, n), lambda b, i: (b, 0, 0))],
        out_specs=pl.BlockSpec((None, tm, n), lambda b, i: (b, i, 0)),
        out_shape=jax.ShapeDtypeStruct((bsz, seq, n), F32),
        compiler_params=_cparams("parallel", "parallel"),
        name="mm_residual",
    )(x, w, b.reshape(1, n), res, gate)


def _moe_pre_body(*refs, n_groups, n_experts, tile_offs):
    n_streams = len(tile_offs) - 1
    g_ref, wr_ref, br_ref, tok_ref, eid_ref, gate_ref = refs[3 * n_streams:]
    i = pl.program_id(0)
    for k in range(n_streams):
        x_ref, sh_ref, sc_ref = refs[3 * k:3 * k + 3]

        @pl.when(jnp.logical_and(i >= tile_offs[k], i < tile_offs[k + 1]))
        def _():
            tok = _norm_mod(x_ref[...], g_ref[...], sh_ref[...], sc_ref[...])
            _route_tokens(tok, wr_ref, br_ref, tok_ref, eid_ref, gate_ref, n_groups, n_experts)


def _route_tokens(tok, wr_ref, br_ref, tok_ref, eid_ref, gate_ref, n_groups, n_experts):
    tok_ref[...] = tok
    t_hi = tok.astype(BF16)
    t_lo = (tok - t_hi.astype(F32)).astype(BF16)
    logits = (_dot(t_hi, wr_ref[0]) + _dot(t_hi, wr_ref[1]) + _dot(t_lo, wr_ref[0])) + br_ref[...]
    lane = lax.broadcasted_iota(jnp.int32, logits.shape, 1)
    per = n_experts // n_groups
    big = jnp.int32(1 << 20)
    gmask = jnp.logical_and(lane >= n_experts, lane < n_experts + n_groups)
    gl = jnp.where(gmask, logits, NEG_BIG)
    gmax = jnp.max(gl, axis=-1, keepdims=True)
    gidx = jnp.min(jnp.where(gl == gmax, lane - n_experts, big), axis=-1, keepdims=True)
    p_top = 1.0 / jnp.sum(jnp.where(gmask, jnp.exp(gl - gmax), 0.0), axis=-1, keepdims=True)
    lo = gidx * per
    emask = jnp.logical_and(lane >= lo, lane < lo + per)
    el = jnp.where(emask, logits, NEG_BIG)
    m1 = jnp.max(el, axis=-1, keepdims=True)
    i1 = jnp.min(jnp.where(el == m1, lane, big), axis=-1, keepdims=True)
    el2 = jnp.where(lane == i1, NEG_BIG, el)
    m2 = jnp.max(el2, axis=-1, keepdims=True)
    i2 = jnp.min(jnp.where(el2 == m2, lane, big), axis=-1, keepdims=True)
    e21 = jnp.exp(m2 - m1)
    g1 = p_top / (1.0 + e21)
    g2 = g1 * e21
    ids = jnp.where(lane == 0, i1, jnp.where(lane == 1, i2, -1))
    ids_t = ids.T
    for h in range(eid_ref.shape[0]):
        eid_ref[h] = ids_t[0:SUBLANES, h * MOE_TM:(h + 1) * MOE_TM]
    gate_ref[...] = jnp.where(lane == 0, g1, jnp.where(lane == 1, g2, 0.0))


def _split_bf16(w):
    hi = w.astype(BF16)
    return jnp.stack([hi, (w - hi.astype(F32)).astype(BF16)])


def _router_weights(wg, bg, we, be):
    pad = LANES - we.shape[1] - wg.shape[1]
    wr = jnp.pad(jnp.concatenate([we, wg], axis=1), ((0, 0), (0, pad)))
    br = jnp.pad(jnp.concatenate([be, bg]), (0, pad)).reshape(1, LANES)
    return wr, br


def moe_pre(streams, g, wr, br, n_groups, n_experts):
    d = streams[0][0].shape[2]
    tm = MOE_TM
    tile_offs = [0]
    in_specs, args = [], []
    for x, shift, scale in streams:
        bsz, seq, _ = x.shape
        nt = seq // tm
        n_tiles = bsz * nt
        off = tile_offs[-1]
        tile_offs.append(off + n_tiles)

        def tile(i, off=off, n_tiles=n_tiles):
            return jnp.clip(i - off, 0, n_tiles - 1)

        in_specs += [pl.BlockSpec((None, tm, d), lambda i, tile=tile, nt=nt: (tile(i) // nt, tile(i) % nt, 0)),
                     pl.BlockSpec((None, 1, d), lambda i, tile=tile, nt=nt: (tile(i) // nt, 0, 0)),
                     pl.BlockSpec((None, 1, d), lambda i, tile=tile, nt=nt: (tile(i) // nt, 0, 0))]
        args += [x, shift, scale]
    in_specs += [pl.BlockSpec((1, d), lambda i: (0, 0)),
                 pl.BlockSpec((2, d, LANES), lambda i: (0, 0, 0)),
                 pl.BlockSpec((1, LANES), lambda i: (0, 0))]
    args += [g.reshape(1, d), _split_bf16(wr), br]
    total = tile_offs[-1] * tm
    rout = pl.BlockSpec((tm, LANES), lambda i: (i, 0))
    tok, eid, gate = pl.pallas_call(
        functools.partial(_moe_pre_body, n_groups=n_groups, n_experts=n_experts, tile_offs=tuple(tile_offs)),
        grid=(tile_offs[-1],),
        in_specs=in_specs,
        out_specs=[pl.BlockSpec((tm, d), lambda i: (i, 0)),
                   pl.BlockSpec((1, SUBLANES, tm), lambda i: (i, 0, 0)), rout],
        out_shape=[jax.ShapeDtypeStruct((total, d), F32),
                   jax.ShapeDtypeStruct((tile_offs[-1], SUBLANES, tm), jnp.int32),
                   jax.ShapeDtypeStruct((total, LANES), F32)],
        compiler_params=_cparams("parallel"),
        name="moe_pre",
    )(*args)
    return tok, eid, gate, tile_offs[:-1]


def _start_row_gather(row_index, n_rows, src_hbm, dst_vmem, sem):
    def body(g, c):
        r0 = pl.multiple_of(g * SUBLANES, SUBLANES)
        dst_tile = dst_vmem.at[pl.ds(r0, SUBLANES)]
        for k in range(SUBLANES):
            pltpu.make_async_copy(src_hbm.at[pl.ds(row_index(r0 + k), 1)], dst_tile.at[pl.ds(k, 1)], sem).start()
        return c

    lax.fori_loop(0, n_rows // SUBLANES, body, 0, unroll=2)


def _wait_row_gather(n_rows, src_hbm, dst_vmem, sem):
    pltpu.make_async_copy(src_hbm.at[pl.ds(0, n_rows)], dst_vmem, sem).wait()


def _expert_body(bv_ref, rk_ref, pe_ref, tot_ref, p0_ref, ts_hbm, tok_ref, wg_hbm, wu_hbm, wd_hbm, o_ref,
                 xbuf, xsem, win, isem, wcache, stg, wsem, cnt, *, layer):
    i = pl.program_id(0)
    n = pl.num_programs(0)
    slot = i % 2
    cr, cc = stg.shape[1:]
    total = tot_ref[0]
    mats_hbm = (wg_hbm, wu_hbm, wd_hbm)

    @pl.when(i == 0)
    def _():
        cnt[0] = 0
        cnt[1] = 0

    def live(b):
        return jnp.logical_and(b < n, bv_ref[jnp.minimum(b, n - 1)] > 0)

    def window(b):
        s = b % 2
        lo = pl.multiple_of((p0_ref[jnp.minimum(b, n - 1)] // MOE_WIN_ALIGN) * MOE_WIN_ALIGN, MOE_WIN_ALIGN)
        return pltpu.make_async_copy(ts_hbm.at[pl.ds(lo, MOE_WIN)],
                                     win.at[pl.ds(pl.multiple_of(s * MOE_WIN, MOE_WIN), MOE_WIN)], isem.at[s])

    def gather(b):
        s = b % 2
        base = s * MOE_WIN + p0_ref[jnp.minimum(b, n - 1)] % MOE_WIN_ALIGN
        _start_row_gather(lambda r: win[base + r], MOE_BM, tok_ref, xbuf.at[s], xsem.at[s])

    @pl.when(jnp.logical_and(i == 0, live(0)))
    def _():
        window(0).start()
        window(0).wait()
        gather(0)

    @pl.when(jnp.logical_and(i == 0, live(1)))
    def _():
        window(1).start()

    @pl.when(live(i + 1))
    def _():
        window(i + 1).wait()
        gather(i + 1)

    @pl.when(live(i + 2))
    def _():
        window(i + 2).start()

    def chunk_geom(c):
        q = c % MOE_NCH
        m = q // 4
        sub = q % 4
        r0 = jnp.where(m < 2, sub, sub // 2) * cr
        c0 = jnp.where(m < 2, 0, sub % 2) * cc
        return m, pl.multiple_of(r0, cr), pl.multiple_of(c0, cc)

    def issue(c):
        e = pe_ref[c // MOE_NCH]
        m, r0, c0 = chunk_geom(c)
        s = c % MOE_STAGE
        for k, w_hbm in enumerate(mats_hbm):
            @pl.when(m == k)
            def _():
                pltpu.make_async_copy(w_hbm.at[layer, e, pl.ds(r0, cr), pl.ds(c0, cc)], stg.at[s],
                                      wsem.at[s]).start()

    def cast(c):
        s = c % MOE_STAGE
        pltpu.make_async_copy(wg_hbm.at[layer, 0, pl.ds(0, cr), pl.ds(0, cc)], stg.at[s], wsem.at[s]).wait()
        ws = (c // MOE_NCH) % 2
        q = c % MOE_NCH
        step = min(MOE_CAST_ROWS, cr)
        assert cr % step == 0

        def slab(k, carry):
            rows = pl.ds(pl.multiple_of(k * step, step), step)
            wcache[ws, q, rows, :] = stg[s, rows, :].astype(BF16)
            return carry

        lax.fori_loop(0, cr // step, slab, 0)

    valid = bv_ref[i] > 0
    rank = rk_ref[i]
    issued = cnt[0]
    done = cnt[1]
    limit = jnp.minimum(total, MOE_NCH * (rank + 2))
    need = jnp.where(valid, MOE_NCH * (rank + 1), done)

    def fill(issued, done):
        hi = jnp.minimum(limit, done + MOE_STAGE)

        def body(c, carry):
            issue(c)
            return carry

        lax.fori_loop(issued, hi, body, 0)
        return jnp.maximum(issued, hi)

    def cast_and_refill(c, issued):
        cast(c)
        more = issued < jnp.minimum(limit, c + 1 + MOE_STAGE)

        @pl.when(more)
        def _():
            issue(issued)

        return issued + more.astype(jnp.int32)

    issued = fill(issued, done)
    issued = lax.fori_loop(done, need, cast_and_refill, issued)
    done = jnp.maximum(done, need)

    @pl.when(valid)
    def _():
        ws = rank % 2
        _wait_row_gather(MOE_BM, tok_ref, xbuf.at[slot], xsem.at[slot])
        x = xbuf[slot].astype(BF16)
        gate = sum(_dot(x[:, k * cr:(k + 1) * cr], wcache[ws, k]) for k in range(4))
        up = sum(_dot(x[:, k * cr:(k + 1) * cr], wcache[ws, 4 + k]) for k in range(4))
        h = (gate * jax.nn.sigmoid(gate) * up).astype(BF16)
        for half in range(2):
            o_ref[:, half * cc:(half + 1) * cc] = sum(
                _dot(h[:, k * cr:(k + 1) * cr], wcache[ws, 8 + 2 * k + half]) for k in range(2))

    @pl.when(jnp.logical_not(valid))
    def _():
        o_ref[...] = jnp.zeros_like(o_ref)

    fetched = issued
    issued = lax.fori_loop(done, fetched, cast_and_refill, issued)
    done = jnp.maximum(done, fetched)
    last = i == n - 1
    tail = jnp.where(last, issued, done)

    def drain(c, carry):
        cast(c)
        return carry

    lax.fori_loop(done, tail, drain, 0)
    cnt[0] = issued
    cnt[1] = jnp.maximum(done, tail)


def moe_experts(tok, tok_sorted, block_p0, block_valid, block_rank, present, n_chunks, w_gate, w_up, w_down,
                layer):
    d = tok.shape[1]
    n_blocks = block_valid.shape[0]
    n_rows = n_blocks * MOE_BM
    dh = w_gate.shape[3]
    assert 2 * dh == d and MOE_NCH == 12
    cr, cc = d // 4, dh
    any_spec = pl.BlockSpec(memory_space=pl.ANY)
    grid_spec = pltpu.PrefetchScalarGridSpec(
        num_scalar_prefetch=5,
        grid=(n_blocks,),
        in_specs=[any_spec, any_spec, any_spec, any_spec, any_spec],
        out_specs=pl.BlockSpec((MOE_BM, d), lambda i, *_: (i, 0)),
        scratch_shapes=[pltpu.VMEM((2, MOE_BM, d), F32), pltpu.SemaphoreType.DMA((2,)),
                        pltpu.SMEM((2 * MOE_WIN,), jnp.int32), pltpu.SemaphoreType.DMA((2,)),
                        pltpu.VMEM((2, MOE_NCH, cr, cc), BF16),
                        pltpu.VMEM((MOE_STAGE, cr, cc), F32), pltpu.SemaphoreType.DMA((MOE_STAGE,)),
                        pltpu.SMEM((2,), jnp.int32)],
    )
    return pl.pallas_call(
        functools.partial(_expert_body, layer=layer),
        grid_spec=grid_spec,
        out_shape=jax.ShapeDtypeStruct((n_rows, d), F32),
        compiler_params=_cparams("arbitrary"),
        name="moe_experts",
    )(block_valid, block_rank, present, n_chunks, block_p0, tok_sorted, tok, w_gate, w_up, w_down)


def _combine_body(dest_ref, nxt_ref, os_ref, gate_ref, res_ref, gt_ref, fg_ref, *rest, final_norm, with_next):
    if with_next:
        ng_ref, nsh_ref, nsc_ref, o_ref, h_ref, buf, sem = rest
    else:
        o_ref, buf, sem = rest
    rows = res_ref.shape[0]
    i = pl.program_id(0)
    n = pl.num_programs(0)
    slot = i % 2

    def start(idx_ref, s):
        for k in range(TOP_K):
            _start_row_gather(lambda r, k=k: idx_ref[k * rows + r], rows, os_ref, buf.at[s, k], sem.at[s])

    @pl.when(i == 0)
    def _():
        start(dest_ref, 0)

    @pl.when(i + 1 < n)
    def _():
        start(nxt_ref, 1 - slot)

    for k in range(TOP_K):
        _wait_row_gather(rows, os_ref, buf.at[slot, k], sem.at[slot])
    gates = gate_ref[...]
    mo = gates[:, 0:1] * buf[slot, 0] + gates[:, 1:2] * buf[slot, 1]
    y = res_ref[...] + gt_ref[...] * mo
    if final_norm:
        ms = jnp.mean(y * y, axis=-1, keepdims=True)
        y = y * lax.rsqrt(ms + NORM_EPS) * fg_ref[...]
    o_ref[...] = y
    if with_next:
        h_ref[...] = _norm_mod(y, ng_ref[...], nsh_ref[...], nsc_ref[...]).astype(h_ref.dtype)


def moe_combine(os, dest, gates, tile0, res, gt, final_g, final_norm, next_mod=None):
    bsz, seq, d = res.shape
    rows = MOE_TM
    nt = seq // rows
    n = bsz * nt
    tile = pl.BlockSpec((None, rows, d), lambda i: (i // nt, i % nt, 0))
    mspec = pl.BlockSpec((None, 1, d), lambda i: (i // nt, 0, 0))
    rspec = pl.BlockSpec((1, d), lambda i: (0, 0))
    dspec = lambda step: pl.BlockSpec((SUBLANES * rows,), lambda i: (tile0 + step(i),), memory_space=pltpu.SMEM)
    in_specs = [dspec(lambda i: i), dspec(lambda i: jnp.minimum(i + 1, n - 1)),
                pl.BlockSpec(memory_space=pl.ANY),
                pl.BlockSpec((rows, LANES), lambda i: (tile0 + i, 0)),
                tile, mspec, rspec]
    args = [dest, dest, os, gates, res, gt, final_g.reshape(1, d)]
    out_specs, out_shape = [tile], [jax.ShapeDtypeStruct((bsz, seq, d), F32)]
    if next_mod is not None:
        in_specs += [rspec, mspec, mspec]
        args += [next_mod[0].reshape(1, d), next_mod[1], next_mod[2]]
        out_specs.append(tile)
        out_shape.append(jax.ShapeDtypeStruct((bsz, seq, d), BF16))
    outs = pl.pallas_call(
        functools.partial(_combine_body, final_norm=final_norm, with_next=next_mod is not None),
        grid=(n,),
        in_specs=in_specs,
        out_specs=out_specs,
        out_shape=out_shape,
        scratch_shapes=[pltpu.VMEM((2, TOP_K, rows, d), F32), pltpu.SemaphoreType.DMA((2,))],
        compiler_params=_cparams("arbitrary"),
        name="moe_combine",
    )(*args)
    return outs if next_mod is not None else outs[0]


def _plan_body(e_ref, dest_ref, tab_ref, blk_ref, present_ref, *, n_experts):
    n_rows, w = e_ref.shape
    e_all = e_ref[...]
    li = lax.broadcasted_iota(jnp.int32, (w, w), 0)
    lj = lax.broadcasted_iota(jnp.int32, (w, w), 1)
    incl = (li <= lj).astype(BF16)
    ri = lax.broadcasted_iota(jnp.int32, (n_rows, n_rows), 0)
    rj = lax.broadcasted_iota(jnp.int32, (n_rows, n_rows), 1)
    before = (rj < ri).astype(BF16)
    elane = lax.broadcasted_iota(jnp.int32, (n_rows, LANES), 1)
    row_tot = jnp.zeros((n_rows, LANES), F32)
    for e in range(n_experts):
        tot = jnp.sum((e_all == e).astype(F32), axis=1, keepdims=True)
        row_tot = row_tot + jnp.where(elane == e, tot, 0.0)
    rows_before = _dot(before, row_tot.astype(BF16))
    counts = jnp.sum(row_tot, axis=0, keepdims=True).astype(jnp.int32)
    lane1 = lax.broadcasted_iota(jnp.int32, (1, LANES), 1)

    def excl_prefix(v):
        acc = v
        sh = 1
        while sh < LANES:
            acc = acc + jnp.where(lane1 >= sh, pltpu.roll(acc, sh, axis=1), 0)
            sh *= 2
        return acc - v

    start = excl_prefix(counts)
    padded = (counts + (MOE_BM - 1)) // MOE_BM * MOE_BM
    pad_start = excl_prefix(padded)
    pad_end = pad_start + padded
    has = (counts > 0).astype(jnp.int32)
    rank = excl_prefix(has)
    n_chunks = MOE_NCH * jnp.sum(has.astype(F32), axis=1, keepdims=True).astype(jnp.int32)
    tab_ref[...] = jnp.concatenate([counts, start, pad_start, pad_end, jnp.broadcast_to(n_chunks, (1, LANES)),
                                    jnp.zeros((SUBLANES - 5, LANES), jnp.int32)], axis=0)
    nb = blk_ref.shape[0]
    f32 = lambda v: v.astype(F32)
    lane_b = lax.broadcasted_iota(jnp.int32, (nb, LANES), 1)
    first_row = lax.broadcasted_iota(jnp.int32, (nb, 1), 0) * MOE_BM
    is_expert = lane_b < n_experts
    expert = jnp.sum(f32(jnp.logical_and(is_expert, pad_end <= first_row)), axis=1, keepdims=True)
    expert = jnp.minimum(expert.astype(jnp.int32), n_experts - 1)
    mine = lane_b == expert
    pick = lambda v: jnp.sum(jnp.where(mine, f32(v), 0.0), axis=1, keepdims=True).astype(jnp.int32)
    live = first_row < jnp.max(f32(pad_end), axis=1, keepdims=True).astype(jnp.int32)
    p0 = jnp.where(live, pick(start) - pick(pad_start) + first_row, 0)
    blk_ref[...] = jnp.where(lane_b == 0, live.astype(jnp.int32),
                             jnp.where(lane_b == 1, pick(rank), jnp.where(lane_b == 2, p0, 0)))
    row_r = lax.broadcasted_iota(jnp.int32, (LANES, LANES), 0)
    lane_e = lax.broadcasted_iota(jnp.int32, (LANES, LANES), 1)
    hit_r = jnp.logical_and(has > 0, rank == row_r)
    present = jnp.sum(jnp.where(hit_r, f32(lane_e), 0.0), axis=1, keepdims=True).astype(jnp.int32)
    present_ref[...] = jnp.broadcast_to(present, (LANES, LANES))
    base = rows_before + pad_start.astype(F32)
    dest = jnp.zeros((n_rows, w), F32)
    for e in range(n_experts):
        hit = e_all == e
        within = _dot(hit.astype(BF16), incl)
        dest = dest + jnp.where(hit, within - 1.0 + base[:, e:e + 1], 0.0)
    dest_ref[...] = dest.astype(jnp.int32)


def _route_plan(eid, n_experts):
    n_tiles, r8, tm = eid.shape
    a = n_tiles * TOP_K * tm
    n_blocks = -(-a // MOE_BM) + n_experts
    assert (tm & (tm - 1)) == 0 and TOP_K == 2
    nb_pad = -(-n_blocks // SUBLANES) * SUBLANES
    dest, tab, blk, present = pl.pallas_call(
        functools.partial(_plan_body, n_experts=n_experts),
        out_shape=[jax.ShapeDtypeStruct((n_tiles * r8, tm), jnp.int32),
                   jax.ShapeDtypeStruct((SUBLANES, LANES), jnp.int32),
                   jax.ShapeDtypeStruct((nb_pad, LANES), jnp.int32),
                   jax.ShapeDtypeStruct((LANES, LANES), jnp.int32)],
        compiler_params=pltpu.CompilerParams(vmem_limit_bytes=V7X_VMEM_LIMIT_BYTES),
        name="moe_route_plan",
    )(eid.reshape(n_tiles * r8, tm))
    e_flat = eid[:, :TOP_K, :].reshape(-1)
    order = jnp.argsort(e_flat).astype(jnp.int32)
    shift = tm.bit_length() - 1
    tok_of = ((order >> (shift + 1)) << shift) | (order & (tm - 1))
    slack = -a % MOE_WIN_ALIGN + MOE_WIN
    tok_sorted = jnp.concatenate([tok_of, jnp.zeros((slack,), jnp.int32)])
    blk = blk[:n_blocks]
    return dest.reshape(-1), tok_sorted, blk[:, 2], blk[:, 0], blk[:, 1], present[:n_experts, 0], tab[4, :1]


def hier_moe(streams, norm_g, wg, bg, we, be, w_gate, w_up, w_down, layer, final_g, final_norm, next_mods=None,
             routed=None):
    n_groups = wg.shape[1]
    n_experts = we.shape[1]
    if routed is None:
        wr, br = _router_weights(wg, bg, we, be)
        tok, eid, gates, tile0s = moe_pre([s[:3] for s in streams], norm_g, wr, br, n_groups, n_experts)
    else:
        (tok, eid, gates), tile0s = routed, [0]
    dest, tok_sorted, block_p0, block_valid, block_rank, present, n_chunks = _route_plan(eid, n_experts)
    os = moe_experts(tok, tok_sorted, block_p0, block_valid, block_rank, present, n_chunks, w_gate, w_up, w_down,
                     layer)
    next_mods = next_mods or [None] * len(streams)
    return [moe_combine(os, dest, gates, tile0, x, gt, final_g, final_norm, nm)
            for (x, _, _, gt), tile0, nm in zip(streams, tile0s, next_mods)]


def _s5_arrange(h):
    bsz, t, d = h.shape
    c = t // (S5_SEGS * S5_TAU)
    h = h.reshape(bsz, S5_SEGS, c, S5_TAU, d // LANES, LANES)
    return h.transpose(2, 0, 1, 4, 3, 5).reshape(c * bsz * S5_SEGS, d * S5_TAU)


def _s5_unarrange(y, bsz):
    r, w = y.shape
    d = w // S5_TAU
    c = r // (bsz * S5_SEGS)
    y = y.reshape(c, bsz, S5_SEGS, d // LANES, S5_TAU, LANES)
    return y.transpose(1, 2, 0, 4, 3, 5).reshape(bsz, S5_SEGS * c * S5_TAU, d)


def _s5_operators(a_re, a_im, log_step, b_re, b_im, c_re, c_im):
    n_g, n_p = a_re.shape[1:]
    n_h = b_re.shape[-1]
    gpt = LANES // n_h
    n_j = n_g // gpt
    tau = S5_TAU
    assert tau * n_h == LANES and 2 * n_p == LANES
    lam_step = lax.complex(a_re, a_im) * jnp.exp(log_step)[..., None]
    lam_bar = jnp.exp(lam_step)
    b_bar = ((lam_bar - 1.0) / lax.complex(a_re, a_im))[..., None] * lax.complex(b_re, b_im)
    c_mat = lax.complex(c_re, c_im)
    ks = jnp.arange(tau + 1, dtype=F32)[None, :, None, None]
    pw = jnp.exp(lam_step[:, None] * ks)
    ein = functools.partial(jnp.einsum, precision=HIGHEST)
    inj_c, cl_c, lt = [], [], []
    tz_c = 0.0
    for d in range(2):
        pos = np.arange(tau) if d == 0 else np.arange(tau)[::-1]
        powers = lambda ks: jnp.exp(lam_step[d][None] * jnp.asarray(ks, F32)[:, None, None])
        inj = (powers(tau - 1 - pos)[..., None] * b_bar[d][None]).reshape(tau, n_j, gpt, n_p, n_h)
        inj = inj.transpose(1, 0, 2, 4, 3).reshape(n_j, tau * LANES, n_p)
        inj_c.append(jnp.concatenate([inj.real, inj.imag], axis=-1))
        cl = (c_mat[d][None] * powers(pos + 1)[:, :, None, :]).reshape(tau, n_j, gpt, n_h, n_p)
        cl = cl.transpose(1, 2, 4, 0, 3).reshape(n_j, gpt * n_p, tau * n_h)
        cl_c.append(jnp.concatenate([cl.real, -cl.imag], axis=1))
        lags = np.arange(tau) if d == 0 else np.arange(tau)[::-1]
        mk = ein('gop,kgp,gph->ghko', c_mat[d], powers(lags), b_bar[d]).real.reshape(n_j, gpt, n_h, tau * n_h)
        zeros = jnp.zeros_like(mk)
        if d == 0:
            padded = jnp.concatenate([zeros, mk], axis=-1)
            rows = [padded[..., (tau - t) * n_h:(2 * tau - t) * n_h] for t in range(tau)]
        else:
            padded = jnp.concatenate([mk, zeros], axis=-1)
            rows = [padded[..., (tau - 1 - t) * n_h:(2 * tau - 1 - t) * n_h] for t in range(tau)]
        tz_c = tz_c + jnp.stack(rows, axis=1).reshape(n_j, tau * LANES, tau * n_h)
        lt_d = pw[d][tau].reshape(n_j, 1, gpt * n_p)
        lt.append(jnp.concatenate([lt_d.real, lt_d.imag], axis=-1))
    ws, wu, wh = s5_expand(jnp.stack(inj_c).astype(BF16), jnp.stack(cl_c).astype(BF16), tz_c.astype(BF16),
                           n_h, n_p)
    return ws, wu, wh, jnp.stack(lt).astype(F32)


def _s5_expand_body(inj_ref, cl_ref, tz_ref, ws_ref, wu_ref, wh_ref, *, n_h, n_p):
    rows = tz_ref.shape[0]
    gpt = LANES // n_h
    row = lax.broadcasted_iota(jnp.int32, (rows, LANES), 0)
    lane = lax.broadcasted_iota(jnp.int32, (rows, LANES), 1)
    sel_r = lax.broadcasted_iota(jnp.int32, (LANES, LANES), 0)
    sel_l = lax.broadcasted_iota(jnp.int32, (LANES, LANES), 1)
    grp_in = (row // n_h) % gpt
    grp_st = (row // n_p) % gpt

    def spread_out(m, t, grp_row):
        sel = jnp.logical_and(sel_r // n_h == t, sel_r % n_h == sel_l % n_h).astype(BF16)
        return jnp.where(grp_row == lane // n_h, _dot(m, sel), 0.0).astype(BF16)

    def spread_state(m, c, q, grp_row):
        sel = jnp.logical_and(sel_r // n_p == c, sel_r % n_p == sel_l % n_p).astype(BF16)
        return jnp.where(grp_row == (LANES // n_p) * q + lane // n_p, _dot(m, sel), 0.0).astype(BF16)

    w2 = 2 * gpt * n_p
    tz = tz_ref[...]
    for t in range(S5_TAU):
        wu_ref[:, t * LANES:(t + 1) * LANES] = spread_out(tz, t, grp_in)
    for d in range(2):
        cl = cl_ref[d]
        inj = inj_ref[d]
        for t in range(S5_TAU):
            wh_ref[d, :, t * LANES:(t + 1) * LANES] = spread_out(cl, t, grp_st)
        for c in range(2):
            for q in range(gpt * n_p // LANES):
                lo = d * w2 + c * gpt * n_p + q * LANES
                ws_ref[:, lo:lo + LANES] = spread_state(inj, c, q, grp_in)


def s5_expand(inj_c, cl_c, tz_c, n_h, n_p):
    n_j, rows, _ = tz_c.shape
    gpt = LANES // n_h
    w2 = 2 * gpt * n_p
    assert rows == S5_TAU * LANES == w2
    cspec = pl.BlockSpec((2, None, rows, LANES), lambda j: (0, j, 0, 0))
    return pl.pallas_call(
        functools.partial(_s5_expand_body, n_h=n_h, n_p=n_p),
        grid=(n_j,),
        in_specs=[cspec, cspec, pl.BlockSpec((None, rows, LANES), lambda j: (j, 0, 0))],
        out_specs=[pl.BlockSpec((None, rows, 2 * w2), lambda j: (j, 0, 0)),
                   pl.BlockSpec((None, rows, rows), lambda j: (j, 0, 0)),
                   pl.BlockSpec((2, None, w2, rows), lambda j: (0, j, 0, 0))],
        out_shape=[jax.ShapeDtypeStruct((n_j, rows, 2 * w2), BF16),
                   jax.ShapeDtypeStruct((n_j, rows, rows), BF16),
                   jax.ShapeDtypeStruct((2, n_j, w2, rows), BF16)],
        compiler_params=_cparams("parallel"),
        name="s5_expand",
    )(inj_c, cl_c, tz_c)


def _s5_inj_body(xc_ref, xl_ref, w_ref, oc_ref, ol_ref):
    w = w_ref[...]
    ol_ref[...] = _dot(xl_ref[...], w)

    @pl.when(pl.program_id(1) == 0)
    def _():
        oc_ref[...] = _dot(xc_ref[...], w)


def s5_inject(xr_c, xr_l, ws):
    r_c, r_l = xr_c.shape[0], xr_l.shape[0]
    n_j, k, n = ws.shape
    tm = r_l // 2 if r_l % 32 == 0 else r_l
    return pl.pallas_call(
        _s5_inj_body,
        grid=(n_j, r_l // tm),
        in_specs=[pl.BlockSpec((r_c, k), lambda j, i: (0, j)),
                  pl.BlockSpec((tm, k), lambda j, i: (i, j)),
                  pl.BlockSpec((None, k, n), lambda j, i: (j, 0, 0))],
        out_specs=[pl.BlockSpec((r_c, n), lambda j, i: (0, j)),
                   pl.BlockSpec((tm, n), lambda j, i: (i, j))],
        out_shape=[jax.ShapeDtypeStruct((r_c, n_j * n), F32), jax.ShapeDtypeStruct((r_l, n_j * n), F32)],
        compiler_params=_cparams("parallel", "arbitrary"),
        name="s5_inject",
    )(xr_c, xr_l, ws)


def _cmul(ar, ai, br, bi):
    return ar * br - ai * bi, ar * bi + ai * br


def _s5_scan_body(sc_ref, sl_ref, lt_ref, h_ref, raw_ref, *, n_ctx, n_lat, bsz):
    d = pl.program_id(1)
    w2 = lt_ref.shape[-1]
    w = w2 // 2
    rows = bsz * S5_SEGS
    seg = lax.broadcasted_iota(jnp.int32, (rows, 1), 0) % S5_SEGS
    is_late = seg != d
    lam_r = lt_ref[:, 0:w]
    lam_i = lt_ref[:, w:w2]
    zero = jnp.zeros((rows, w), F32)
    one = (jnp.ones((1, w), F32), jnp.zeros((1, w), F32))

    def swap_segments(x):
        return jnp.where(seg == 0, pltpu.roll(x, rows - 1, axis=0), pltpu.roll(x, 1, axis=0))

    def phase(s_ref, n_steps, hin_r, hin_i, write):
        def chunk(k):
            return jnp.where(d == 0, k, n_steps - 1 - k)

        def step_raw(k, carry):
            hr, hi = carry
            c = chunk(k)
            raw_ref[c, :, 0:w] = hr
            raw_ref[c, :, w:w2] = hi
            nr, ni = _cmul(lam_r, lam_i, hr, hi)
            return nr + s_ref[c, :, 0:w], ni + s_ref[c, :, w:w2]

        er, ei = lax.fori_loop(0, n_steps, step_raw, (zero, zero), unroll=S5_SCAN_UNROLL)
        pr, pi = lax.fori_loop(0, n_steps, lambda k, q: _cmul(lam_r, lam_i, *q), one)
        dr, di = _cmul(pr, pi, hin_r, hin_i)
        first_r = jnp.where(is_late, 0.0, er + dr)
        first_i = jnp.where(is_late, 0.0, ei + di)
        carry_r = jnp.where(is_late, swap_segments(first_r), hin_r)
        carry_i = jnp.where(is_late, swap_segments(first_i), hin_i)
        if write:
            def step_fix(k, q):
                c = chunk(k)
                fr, fi = _cmul(q[0], q[1], carry_r, carry_i)
                h_ref[c, :, 0:w] = (raw_ref[c, :, 0:w] + fr).astype(h_ref.dtype)
                h_ref[c, :, w:w2] = (raw_ref[c, :, w:w2] + fi).astype(h_ref.dtype)
                return _cmul(lam_r, lam_i, q[0], q[1])

            lax.fori_loop(0, n_steps, step_fix, one, unroll=S5_SCAN_UNROLL)
        lr, li = _cmul(pr, pi, carry_r, carry_i)
        last_r = jnp.where(is_late, er + lr, 0.0)
        last_i = jnp.where(is_late, ei + li, 0.0)
        return (jnp.where(is_late, 0.0, swap_segments(last_r)), jnp.where(is_late, 0.0, swap_segments(last_i)))

    hr, hi = phase(sc_ref, n_ctx, zero, zero, False)
    phase(sl_ref, n_lat, hr, hi, True)


def s5_scan(s_ctx, s_lat, lt, bsz):
    assert S5_SEGS == 2
    n_ctx, rows, _ = s_ctx.shape
    n_lat = s_lat.shape[0]
    n_j = lt.shape[1]
    w2 = lt.shape[-1]
    return pl.pallas_call(
        functools.partial(_s5_scan_body, n_ctx=n_ctx, n_lat=n_lat, bsz=bsz),
        grid=(n_j, 2),
        in_specs=[pl.BlockSpec((n_ctx, rows, w2), lambda j, d: (0, 0, 2 * j + d)),
                  pl.BlockSpec((n_lat, rows, w2), lambda j, d: (0, 0, 2 * j + d)),
                  pl.BlockSpec((None, None, 1, w2), lambda j, d: (d, j, 0, 0))],
        out_specs=pl.BlockSpec((None, n_lat, rows, w2), lambda j, d: (d, 0, 0, j)),
        out_shape=jax.ShapeDtypeStruct((2, n_lat, rows, n_j * w2), BF16),
        scratch_shapes=[pltpu.VMEM((max(n_ctx, n_lat), rows, w2), F32)],
        compiler_params=_cparams("parallel", "parallel"),
        name="s5_scan",
    )(s_ctx, s_lat, lt)


def _s5_out_body(x_ref, hf_ref, hb_ref, wu_ref, whf_ref, whb_ref, o_ref):
    o_ref[...] = (_dot(x_ref[...], wu_ref[...]) + _dot(hf_ref[...], whf_ref[...])
                  + _dot(hb_ref[...], whb_ref[...])).astype(o_ref.dtype)


def s5_readout(xr, h, wu, wh):
    r = xr.shape[0]
    n_j, k, n = wu.shape
    w2 = wh.shape[2]
    tm = min(r, 1024)
    return pl.pallas_call(
        _s5_out_body,
        grid=(n_j, r // tm),
        in_specs=[pl.BlockSpec((tm, k), lambda j, i: (i, j)),
                  pl.BlockSpec((None, tm, w2), lambda j, i: (0, i, j)),
                  pl.BlockSpec((None, tm, w2), lambda j, i: (1, i, j)),
                  pl.BlockSpec((None, k, n), lambda j, i: (j, 0, 0)),
                  pl.BlockSpec((None, None, w2, n), lambda j, i: (0, j, 0, 0)),
                  pl.BlockSpec((None, None, w2, n), lambda j, i: (1, j, 0, 0))],
        out_specs=pl.BlockSpec((tm, n), lambda j, i: (i, j)),
        out_shape=jax.ShapeDtypeStruct((r, n_j * n), BF16),
        compiler_params=_cparams("parallel", "parallel"),
        name="s5_readout",
    )(xr, h, h, wu, wh, wh)


def _gelu_tanh(x):
    return 0.5 * x * (1.0 + jnp.tanh(math.sqrt(2.0 / math.pi) * (x + 0.044715 * (x * x * x))))


def _s5_glu_body(x_ref, y_ref, g_ref, sh_ref, sc_ref, dk_ref, w1_ref, w2_ref, b1_ref, b2_ref, gt_ref,
                 mg_ref, msh_ref, msc_ref, wr_ref, br_ref, o_ref, tok_ref, eid_ref, gate_ref, *, n_groups, n_experts):
    x = x_ref[...]
    u = _norm_mod(x, g_ref[...], sh_ref[...], sc_ref[...])
    y = _gelu_tanh(y_ref[...].astype(F32) + dk_ref[...] * u).astype(BF16)
    o = (_dot(y, w1_ref[...]) + b1_ref[...]) * jax.nn.sigmoid(_dot(y, w2_ref[...]) + b2_ref[...])
    xn = x + gt_ref[...] * o
    o_ref[...] = xn
    tok = _norm_mod(xn, mg_ref[...], msh_ref[...], msc_ref[...])
    _route_tokens(tok, wr_ref, br_ref, tok_ref, eid_ref, gate_ref, n_groups, n_experts)


def s5_glu(x, y, g, shift, scale, d_skip, w1, b1, w2, b2, gate, moe_g, moe_shift, moe_scale, wr, br, n_groups,
           n_experts):
    bsz, seq, d = x.shape
    tm = min(seq, 512)
    nt = seq // tm
    sub = tm // MOE_TM
    row = lambda a: a.reshape(1, d)
    rspec = pl.BlockSpec((1, d), lambda b, i: (0, 0))
    mspec = pl.BlockSpec((None, 1, d), lambda b, i: (b, 0, 0))
    tile = pl.BlockSpec((None, tm, d), lambda b, i: (b, i, 0))
    wspec = pl.BlockSpec((d, d), lambda b, i: (0, 0), pipeline_mode=pl.Buffered(1))
    return pl.pallas_call(
        functools.partial(_s5_glu_body, n_groups=n_groups, n_experts=n_experts),
        grid=(bsz, nt),
        in_specs=[tile, tile, rspec, mspec, mspec, rspec, wspec, wspec, rspec, rspec, mspec,
                  rspec, mspec, mspec,
                  pl.BlockSpec((2, d, LANES), lambda b, i: (0, 0, 0)),
                  pl.BlockSpec((1, LANES), lambda b, i: (0, 0))],
        out_specs=[tile,
                   pl.BlockSpec((tm, d), lambda b, i: (b * nt + i, 0)),
                   pl.BlockSpec((sub, SUBLANES, MOE_TM), lambda b, i: (b * nt + i, 0, 0)),
                   pl.BlockSpec((tm, LANES), lambda b, i: (b * nt + i, 0))],
        out_shape=[jax.ShapeDtypeStruct((bsz, seq, d), F32),
                   jax.ShapeDtypeStruct((bsz * seq, d), F32),
                   jax.ShapeDtypeStruct((bsz * seq // MOE_TM, SUBLANES, MOE_TM), jnp.int32),
                   jax.ShapeDtypeStruct((bsz * seq, LANES), F32)],
        compiler_params=_cparams("parallel", "parallel"),
        name="s5_glu",
    )(x, y, row(g), shift, scale, row(d_skip), w1, w2, row(b1), row(b2), gate,
      row(moe_g), moe_shift, moe_scale, _split_bf16(wr), br)


def s5_mix(xl, hl, hc, g, sh_l, sc_l, gate_l, a_re, a_im, log_step, b_re, b_im, c_re, c_im, d_skip,
           w1, b1, w2, b2, moe_mod, router):
    bsz, seq, d = xl.shape
    xr_c = _s5_arrange(hc)
    xr_l = _s5_arrange(hl)
    rows = bsz * S5_SEGS
    n_ctx = xr_c.shape[0] // rows
    n_lat = xr_l.shape[0] // rows
    ws, wu, wh, lt = _s5_operators(a_re, a_im, log_step, b_re, b_im, c_re, c_im)
    s_ctx, s_lat = s5_inject(xr_c, xr_l, ws)
    h = s5_scan(s_ctx.reshape(n_ctx, rows, -1), s_lat.reshape(n_lat, rows, -1), lt, bsz)
    y = s5_readout(xr_l, h.reshape(2, n_lat * rows, -1), wu, wh)
    y = _s5_unarrange(y, bsz)
    wr, br = _router_weights(*router)
    xl, tok, eid, gates = s5_glu(xl, y, g, sh_l, sc_l, d_skip, w1.astype(BF16), b1, w2.astype(BF16), b2, gate_l,
                                 *moe_mod, wr, br, router[0].shape[1], router[2].shape[1])
    return xl, (tok, eid, gates)


def hyena_mix(x, g, shift, scale, gate, w_in, b_in, conv_w, conv_b, fw1, fb1, fw2, fb2, fw3, freq, skip,
              w_out, b_out):
    seq = x.shape[1]
    if seq >= FFT_MIN_SEQ and (2 * seq) % (2 * FFT_N2) == 0:
        a, dd, _ = hyena_filter_taps(seq, fw1, fb1, fw2, fb2, fw3, freq, F32)
        v, x0 = hyena_in(x, g, shift, scale, w_in, b_in, conv_w, conv_b, F32)
        yg = hyena_conv_fft(v, x0, skip, a, dd)
    else:
        cmat, smat = dft_matrices(seq)
        a, dd, kn = hyena_filter_taps(seq, fw1, fb1, fw2, fb2, fw3, freq, BF16)
        kr, ki = hyena_filter_dft(a, dd, cmat, smat)
        v, x0 = hyena_in(x, g, shift, scale, w_in, b_in, conv_w, conv_b, BF16)
        yg = hyena_conv(v, x0, skip, kr, ki, kn, cmat, smat)
    return mm_residual(yg, w_out, b_out, x, gate)


def kernel(x, c, ctx, c_ctx, ada_w, ada_b, norm_g, final_g, hy_w_in, hy_b_in, hy_conv_w, hy_conv_b, hy_fw1,
           hy_fb1, hy_fw2, hy_fb2, hy_fw3, hy_freq, hy_skip, hy_w_out, hy_b_out, s5_a_re, s5_a_im,
           s5_log_step, s5_b_re, s5_b_im, s5_c_re, s5_c_im, s5_d, s5_w1, s5_b1, s5_w2, s5_b2, moe_wg, moe_bg,
           moe_we, moe_be, moe_w_gate, moe_w_up, moe_w_down):
    bsz, _, d = x.shape
    depth = ada_w.shape[0]
    assert depth == 2 and bsz < SUBLANES
    c_all = jnp.concatenate([c, c_ctx[None, :], jnp.zeros((SUBLANES - bsz - 1, d), F32)], axis=0)
    mods = ada_mod(c_all, ada_w, ada_b)

    def mod_rows(layer, k):
        lat = mods[layer, :bsz, k * d:(k + 1) * d][:, None, :]
        cx = jnp.broadcast_to(mods[layer, bsz, k * d:(k + 1) * d][None, None, :], (bsz, 1, d))
        return lat, cx

    (sh_a, csh_a), (sc_a, csc_a), (gt_a, cgt_a) = mod_rows(0, 0), mod_rows(0, 1), mod_rows(0, 2)
    (sh_f, csh_f), (sc_f, csc_f), (gt_f, cgt_f) = mod_rows(0, 3), mod_rows(0, 4), mod_rows(0, 5)
    hy = (hy_w_in[0].astype(BF16), hy_b_in[0], hy_conv_w[0], hy_conv_b[0], hy_fw1[0], hy_fb1[0], hy_fw2[0],
          hy_fb2[0], hy_fw3[0], hy_freq[0], hy_skip[0], hy_w_out[0].astype(BF16), hy_b_out[0])
    xl = hyena_mix(x, norm_g[0, 0], sh_a, sc_a, gt_a, *hy)
    xc = hyena_mix(ctx, norm_g[0, 0], csh_a, csc_a, cgt_a, *hy)
    (sh_a, csh_a), (sc_a, csc_a), (gt_a, _) = mod_rows(1, 0), mod_rows(1, 1), mod_rows(1, 2)
    (xl, hl), (_, hc) = hier_moe([(xl, sh_f, sc_f, gt_f), (xc, csh_f, csc_f, cgt_f)], norm_g[0, 1],
                                 moe_wg[0], moe_bg[0], moe_we[0], moe_be[0], moe_w_gate, moe_w_up, moe_w_down, 0,
                                 final_g, False,
                                 next_mods=[(norm_g[1, 0], sh_a, sc_a), (norm_g[1, 0], csh_a, csc_a)])
    (sh_f, _), (sc_f, _), (gt_f, _) = mod_rows(1, 3), mod_rows(1, 4), mod_rows(1, 5)
    router = (moe_wg[1], moe_bg[1], moe_we[1], moe_be[1])
    xl, routed = s5_mix(xl, hl, hc, norm_g[1, 0], sh_a, sc_a, gt_a, s5_a_re[0], s5_a_im[0], s5_log_step[0],
                        s5_b_re[0], s5_b_im[0], s5_c_re[0], s5_c_im[0], s5_d[0], s5_w1[0], s5_b1[0], s5_w2[0],
                        s5_b2[0], (norm_g[1, 1], sh_f, sc_f), router)
    (out,) = hier_moe([(xl, sh_f, sc_f, gt_f)], norm_g[1, 1], *router, moe_w_gate, moe_w_up, moe_w_down, 1,
                      final_g, True, routed=routed)
    return out
```
